```python
import jax, jax.numpy as jnp
from jax import lax
import numpy as np

D_MODEL = 1024
BATCH = 8
SEQ = 4096
DEPTH = 4

CHUNK = 64
N_MIXERS = 2
EPS = 1e-6
GM_BLOCK = 128
GM_HEADS = 8
GM_WIDTH = 2 * D_MODEL
GM_HEAD_DIM = GM_WIDTH // GM_HEADS
HG_EXPAND = 128
HG_HEADS = D_MODEL // HG_EXPAND
HG_KEY = HG_EXPAND
HG_VAL = D_MODEL // HG_HEADS
FFN_HIDDEN = 2816
CONV_WIDTH = 3
N_A = (DEPTH + 1) // 2
N_B = DEPTH // 2

kernel_name = "hybrid_gmlp_hgrn2_convffn_adaln"


def rms_norm(x, g):
    xf = x.astype(jnp.float32)
    y = xf * lax.rsqrt(jnp.mean(xf * xf, axis=-1, keepdims=True) + EPS)
    return (y * g.astype(jnp.float32)).astype(x.dtype)


def chunk_causal_mask(n):
    idx = jnp.arange(n) // CHUNK
    return idx[:, None] >= idx[None, :]


def spatial_gating_mixer(h, w_in, ln_g, ln_b, w_s, b_s, w_out):
    bsz, t, _ = h.shape
    z = jax.nn.gelu(h @ w_in, approximate=False)
    u, v = jnp.split(z, 2, axis=-1)
    vf = v.astype(jnp.float32)
    mu = jnp.mean(vf, axis=-1, keepdims=True)
    var = jnp.mean(jnp.square(vf - mu), axis=-1, keepdims=True)
    v = ((vf - mu) * lax.rsqrt(var + EPS) * ln_g + ln_b).astype(h.dtype)
    v = v.reshape(bsz, t // GM_BLOCK, GM_BLOCK, GM_HEADS, GM_HEAD_DIM)
    ws = jnp.where(chunk_causal_mask(GM_BLOCK)[None], w_s, 0)
    s = jnp.einsum('hnm,bcmhd->bcnhd', ws, v) + b_s.T[None, None, :, :, None]
    gated = u * s.reshape(bsz, t, GM_WIDTH)
    return gated @ w_out


def hgrn2_chunked_scan(q, k, v, logf):
    bsz, t, nh, dk = q.shape
    dv = v.shape[-1]
    nc = t // CHUNK

    def to_chunks(a):
        return a.reshape(bsz, nc, CHUNK, nh, a.shape[-1]).transpose(1, 0, 3, 2, 4)

    causal = jnp.tril(jnp.ones((CHUNK, CHUNK), dtype=bool))[:, :, None]

    def step(state, inp):
        qc, kc, vc, gc = inp
        cum = jnp.cumsum(gc, axis=-2)
        rel = cum[..., :, None, :] - cum[..., None, :, :]
        decay = jnp.exp(jnp.where(causal, rel, -jnp.inf))
        scores = jnp.einsum('bhik,bhjk,bhijk->bhij', qc, kc, decay)
        out = jnp.einsum('bhij,bhjv->bhiv', scores, vc) + \
            jnp.einsum('bhik,bhkv->bhiv', qc * jnp.exp(cum), state)
        last = cum[..., -1:, :]
        state = jnp.exp(last)[..., 0, :, None] * state + \
            jnp.einsum('bhjk,bhjv->bhkv', kc * jnp.exp(last - cum), vc)
        return state, out

    s0 = jnp.zeros((bsz, nh, dk, dv), jnp.float32)
    _, out = lax.scan(step, s0, (to_chunks(q), to_chunks(k), to_chunks(v), to_chunks(logf)))
    return out.transpose(1, 0, 3, 2, 4).reshape(bsz, t, nh, dv)


def hgrn2_mixer(h, w_in, lb, gn_g, w_out):
    bsz, t, _ = h.shape
    q, fz, i, g = jnp.split(h @ w_in, 4, axis=-1)
    f = lb + (1.0 - lb) * jax.nn.sigmoid(fz.astype(jnp.float32))
    logf = jnp.log(f)
    k = 1.0 - f
    q = jax.nn.silu(q).astype(jnp.float32)

    def heads(a):
        return a.reshape(bsz, t, HG_HEADS, -1)

    o = hgrn2_chunked_scan(heads(q), heads(k), heads(i.astype(jnp.float32)), heads(logf))
    o = o * lax.rsqrt(jnp.mean(o * o, axis=-1, keepdims=True) + EPS)
    o = (o.reshape(bsz, t, HG_HEADS * HG_VAL) * gn_g.astype(jnp.float32)).astype(h.dtype)
    return (o * jax.nn.silu(g)) @ w_out


def conv_ffn(h, w_up, conv_w, conv_b, w_down):
    t = h.shape[1]
    a = h @ w_up
    ap = jnp.pad(a, ((0, 0), (CONV_WIDTH - 1, 0), (0, 0)))
    y = conv_b
    for j in range(CONV_WIDTH):
        y = y + conv_w[j] * ap[:, j:j + t]
    gate, val = jnp.split(y, 2, axis=-1)
    return (jax.nn.gelu(gate, approximate=False) * val) @ w_down


def _fwd_setup_inputs(seed: int = 0) -> dict:
    key = jax.random.key(seed)
    ks = jax.random.split(key, 20)
    nrm = jax.random.normal
    f32 = jnp.float32
    D, F2 = D_MODEL, 2 * FFN_HIDDEN
    return {
        "x": nrm(ks[0], (BATCH, SEQ, D), f32),
        "c": nrm(ks[1], (BATCH, D), f32),
        "gm_w_in": nrm(ks[2], (N_A, D, 2 * GM_WIDTH), f32) * D ** -0.5,
        "gm_ln_g": 1.0 + 0.02 * nrm(ks[3], (N_A, GM_WIDTH), f32),
        "gm_ln_b": 0.02 * nrm(ks[4], (N_A, GM_WIDTH), f32),
        "gm_w_s": nrm(ks[5], (N_A, GM_HEADS, GM_BLOCK, GM_BLOCK), f32) * GM_BLOCK ** -0.5,
        "gm_b_s": 1.0 + 0.1 * nrm(ks[6], (N_A, GM_HEADS, GM_BLOCK), f32),
        "gm_w_out": nrm(ks[7], (N_A, GM_WIDTH, D), f32) * GM_WIDTH ** -0.5,
        "hg_w_in": nrm(ks[8], (N_B, D, 4 * D), f32) * D ** -0.5,
        "hg_lb": 0.5 * nrm(ks[9], (N_B, D), f32),
        "hg_gn_g": 1.0 + 0.02 * nrm(ks[10], (N_B, D), f32),
        "hg_w_out": nrm(ks[11], (N_B, D, D), f32) * D ** -0.5,
        "ffn_w_up": nrm(ks[12], (DEPTH, D, F2), f32) * D ** -0.5,
        "ffn_conv_w": nrm(ks[13], (DEPTH, CONV_WIDTH, F2), f32) * CONV_WIDTH ** -0.5,
        "ffn_conv_b": 0.02 * nrm(ks[14], (DEPTH, F2), f32),
        "ffn_w_down": nrm(ks[15], (DEPTH, FFN_HIDDEN, D), f32) * FFN_HIDDEN ** -0.5,
        "norm_g": 1.0 + 0.02 * nrm(ks[16], (DEPTH, 2, D), f32),
        "ada_w": nrm(ks[17], (DEPTH, D, 6 * D), f32) * D ** -0.5,
        "ada_b": 0.02 * nrm(ks[18], (DEPTH, 6 * D), f32),
        "final_g": 1.0 + 0.02 * nrm(ks[19], (D,), f32),
    }


def _fwd_reference(x, c, gm_w_in, gm_ln_g, gm_ln_b, gm_w_s, gm_b_s, gm_w_out,
              hg_w_in, hg_lb, hg_gn_g, hg_w_out,
              ffn_w_up, ffn_conv_w, ffn_conv_b, ffn_w_down,
              norm_g, ada_w, ada_b, final_g):
    lb_p = jax.nn.softmax(hg_lb.astype(jnp.float32), axis=0)
    lb_all = jnp.cumsum(lb_p, axis=0) - lb_p[0]
    cond = jax.nn.silu(c)
    for i in range(DEPTH):
        mod = (cond @ ada_w[i] + ada_b[i])[:, None, :]
        sh1, sc1, g1, sh2, sc2, g2 = jnp.split(mod, 6, axis=-1)
        h = rms_norm(x, norm_g[i, 0]) * (1.0 + sc1) + sh1
        j = i // N_MIXERS
        if i % N_MIXERS == 0:
            y = spatial_gating_mixer(h, gm_w_in[j], gm_ln_g[j], gm_ln_b[j],
                                     gm_w_s[j], gm_b_s[j], gm_w_out[j])
        else:
            y = hgrn2_mixer(h, hg_w_in[j], lb_all[j], hg_gn_g[j], hg_w_out[j])
        x = x + g1 * y
        h = rms_norm(x, norm_g[i, 1]) * (1.0 + sc2) + sh2
        x = x + g2 * conv_ffn(h, ffn_w_up[i], ffn_conv_w[i], ffn_conv_b[i], ffn_w_down[i])
    return rms_norm(x, final_g)


import jax as _jax
import jax.numpy as _jnp

TWIN_FORMAT = 'train_step'
FWD_PARAMS = ['x', 'c', 'gm_w_in', 'gm_ln_g', 'gm_ln_b', 'gm_w_s', 'gm_b_s', 'gm_w_out', 'hg_w_in', 'hg_lb', 'hg_gn_g', 'hg_w_out', 'ffn_w_up', 'ffn_conv_w', 'ffn_conv_b', 'ffn_w_down', 'norm_g', 'ada_w', 'ada_b', 'final_g']
TWIN_WEIGHTS = ['gm_w_in', 'gm_ln_g', 'gm_ln_b', 'gm_w_s', 'gm_b_s', 'gm_w_out', 'hg_w_in', 'hg_lb', 'hg_gn_g', 'hg_w_out', 'ffn_w_up', 'ffn_conv_w', 'ffn_conv_b', 'ffn_w_down', 'norm_g', 'ada_w', 'ada_b', 'final_g']
TWIN_DIFF_INPUT = 'x'
TWIN_INPUTS = ['x', 'c', 'gm_w_in', 'gm_ln_g', 'gm_ln_b', 'gm_w_s', 'gm_b_s', 'gm_w_out', 'hg_w_in', 'hg_lb', 'hg_gn_g', 'hg_w_out', 'ffn_w_up', 'ffn_conv_w', 'ffn_conv_b', 'ffn_w_down', 'norm_g', 'ada_w', 'ada_b', 'final_g', 'loss_target', 'm_gm_w_in', 'm_gm_ln_g', 'm_gm_ln_b', 'm_gm_w_s', 'm_gm_b_s', 'm_gm_w_out', 'm_hg_w_in', 'm_hg_lb', 'm_hg_gn_g', 'm_hg_w_out', 'm_ffn_w_up', 'm_ffn_conv_w', 'm_ffn_conv_b', 'm_ffn_w_down', 'm_norm_g', 'm_ada_w', 'm_ada_b', 'm_final_g', 'v_gm_w_in', 'v_gm_ln_g', 'v_gm_ln_b', 'v_gm_w_s', 'v_gm_b_s', 'v_gm_w_out', 'v_hg_w_in', 'v_hg_lb', 'v_hg_gn_g', 'v_hg_w_out', 'v_ffn_w_up', 'v_ffn_conv_w', 'v_ffn_conv_b', 'v_ffn_w_down', 'v_norm_g', 'v_ada_w', 'v_ada_b', 'v_final_g']
TWIN_OUTPUTS = ['loss', 'grad_x', 'grad_gm_w_in', 'grad_gm_ln_g', 'grad_gm_ln_b', 'grad_gm_w_s', 'grad_gm_b_s', 'grad_gm_w_out', 'grad_hg_w_in', 'grad_hg_lb', 'grad_hg_gn_g', 'grad_hg_w_out', 'grad_ffn_w_up', 'grad_ffn_conv_w', 'grad_ffn_conv_b', 'grad_ffn_w_down', 'grad_norm_g', 'grad_ada_w', 'grad_ada_b', 'grad_final_g', 'delta_gm_w_in', 'delta_gm_ln_g', 'delta_gm_ln_b', 'delta_gm_w_s', 'delta_gm_b_s', 'delta_gm_w_out', 'delta_hg_w_in', 'delta_hg_lb', 'delta_hg_gn_g', 'delta_hg_w_out', 'delta_ffn_w_up', 'delta_ffn_conv_w', 'delta_ffn_conv_b', 'delta_ffn_w_down', 'delta_norm_g', 'delta_ada_w', 'delta_ada_b', 'delta_final_g', 'new_m_gm_w_in', 'new_m_gm_ln_g', 'new_m_gm_ln_b', 'new_m_gm_w_s', 'new_m_gm_b_s', 'new_m_gm_w_out', 'new_m_hg_w_in', 'new_m_hg_lb', 'new_m_hg_gn_g', 'new_m_hg_w_out', 'new_m_ffn_w_up', 'new_m_ffn_conv_w', 'new_m_ffn_conv_b', 'new_m_ffn_w_down', 'new_m_norm_g', 'new_m_ada_w', 'new_m_ada_b', 'new_m_final_g', 'new_v_gm_w_in', 'new_v_gm_ln_g', 'new_v_gm_ln_b', 'new_v_gm_w_s', 'new_v_gm_b_s', 'new_v_gm_w_out', 'new_v_hg_w_in', 'new_v_hg_lb', 'new_v_hg_gn_g', 'new_v_hg_w_out', 'new_v_ffn_w_up', 'new_v_ffn_conv_w', 'new_v_ffn_conv_b', 'new_v_ffn_w_down', 'new_v_norm_g', 'new_v_ada_w', 'new_v_ada_b', 'new_v_final_g']
TWIN_LEAF_KINDS = {'loss': 'loss', 'grad_x': 'grad_x', 'grad_gm_w_in': 'grad_w', 'grad_gm_ln_g': 'grad_w', 'grad_gm_ln_b': 'grad_w', 'grad_gm_w_s': 'grad_w', 'grad_gm_b_s': 'grad_w', 'grad_gm_w_out': 'grad_w', 'grad_hg_w_in': 'grad_w', 'grad_hg_lb': 'grad_w', 'grad_hg_gn_g': 'grad_w', 'grad_hg_w_out': 'grad_w', 'grad_ffn_w_up': 'grad_w', 'grad_ffn_conv_w': 'grad_w', 'grad_ffn_conv_b': 'grad_w', 'grad_ffn_w_down': 'grad_w', 'grad_norm_g': 'grad_w', 'grad_ada_w': 'grad_w', 'grad_ada_b': 'grad_w', 'grad_final_g': 'grad_w', 'delta_gm_w_in': 'delta_w', 'delta_gm_ln_g': 'delta_w', 'delta_gm_ln_b': 'delta_w', 'delta_gm_w_s': 'delta_w', 'delta_gm_b_s': 'delta_w', 'delta_gm_w_out': 'delta_w', 'delta_hg_w_in': 'delta_w', 'delta_hg_lb': 'delta_w', 'delta_hg_gn_g': 'delta_w', 'delta_hg_w_out': 'delta_w', 'delta_ffn_w_up': 'delta_w', 'delta_ffn_conv_w': 'delta_w', 'delta_ffn_conv_b': 'delta_w', 'delta_ffn_w_down': 'delta_w', 'delta_norm_g': 'delta_w', 'delta_ada_w': 'delta_w', 'delta_ada_b': 'delta_w', 'delta_final_g': 'delta_w', 'new_m_gm_w_in': 'new_m', 'new_m_gm_ln_g': 'new_m', 'new_m_gm_ln_b': 'new_m', 'new_m_gm_w_s': 'new_m', 'new_m_gm_b_s': 'new_m', 'new_m_gm_w_out': 'new_m', 'new_m_hg_w_in': 'new_m', 'new_m_hg_lb': 'new_m', 'new_m_hg_gn_g': 'new_m', 'new_m_hg_w_out': 'new_m', 'new_m_ffn_w_up': 'new_m', 'new_m_ffn_conv_w': 'new_m', 'new_m_ffn_conv_b': 'new_m', 'new_m_ffn_w_down': 'new_m', 'new_m_norm_g': 'new_m', 'new_m_ada_w': 'new_m', 'new_m_ada_b': 'new_m', 'new_m_final_g': 'new_m', 'new_v_gm_w_in': 'new_v', 'new_v_gm_ln_g': 'new_v', 'new_v_gm_ln_b': 'new_v', 'new_v_gm_w_s': 'new_v', 'new_v_gm_b_s': 'new_v', 'new_v_gm_w_out': 'new_v', 'new_v_hg_w_in': 'new_v', 'new_v_hg_lb': 'new_v', 'new_v_hg_gn_g': 'new_v', 'new_v_hg_w_out': 'new_v', 'new_v_ffn_w_up': 'new_v', 'new_v_ffn_conv_w': 'new_v', 'new_v_ffn_conv_b': 'new_v', 'new_v_ffn_w_down': 'new_v', 'new_v_norm_g': 'new_v', 'new_v_ada_w': 'new_v', 'new_v_ada_b': 'new_v', 'new_v_final_g': 'new_v'}


def _forward(args):
    return _fwd_reference(*[args[k] for k in FWD_PARAMS])


def _output_shape():
    def fwd():
        inp = _fwd_setup_inputs(0)
        return _fwd_reference(*[inp[k] for k in FWD_PARAMS])
    out = _jax.eval_shape(fwd)
    return out.shape, out.dtype

N_MICROBATCH = 1
ADAM_LR = 0.001
ADAM_B1 = 0.9
ADAM_B2 = 0.999
ADAM_EPS = 1e-08
ADAM_WD = 0.01
ADAM_STEP = 10
PER_EXAMPLE_BATCH_AXIS = {'x': 0, 'c': 0, 'loss_target': 0}
SHARED_INPUTS = []
_WEIGHT_DTYPES = {'gm_w_in': _jnp.float32, 'gm_ln_g': _jnp.float32, 'gm_ln_b': _jnp.float32, 'gm_w_s': _jnp.float32, 'gm_b_s': _jnp.float32, 'gm_w_out': _jnp.float32, 'hg_w_in': _jnp.float32, 'hg_lb': _jnp.float32, 'hg_gn_g': _jnp.float32, 'hg_w_out': _jnp.float32, 'ffn_w_up': _jnp.float32, 'ffn_conv_w': _jnp.float32, 'ffn_conv_b': _jnp.float32, 'ffn_w_down': _jnp.float32, 'norm_g': _jnp.float32, 'ada_w': _jnp.float32, 'ada_b': _jnp.float32, 'final_g': _jnp.float32}
MOMENT_SCALE = {'gm_w_in': 6.849400e-02, 'gm_ln_g': 4.909576e-02, 'gm_ln_b': 4.730022e-02, 'gm_w_s': 6.934870e-02, 'gm_b_s': 8.100418e-02, 'gm_w_out': 1.100977e-01, 'hg_w_in': 4.308712e-02, 'hg_lb': 2.195702e-03, 'hg_gn_g': 5.898229e-02, 'hg_w_out': 5.959214e-02, 'ffn_w_up': 5.104587e-02, 'ffn_conv_w': 5.143608e-02, 'ffn_conv_b': 3.832921e-02, 'ffn_w_down': 8.440142e-02, 'norm_g': 1.027098e-01, 'ada_w': 6.956791e-02, 'ada_b': 1.170599e-01, 'final_g': 3.239397e+01}


def _to_microbatches(a, axis):
    t = _jnp.moveaxis(a, axis, 0)
    t = t.reshape((N_MICROBATCH, t.shape[0] // N_MICROBATCH) + t.shape[1:])
    return _jnp.moveaxis(t, 1, axis + 1)


def setup_inputs(seed: int = 0) -> dict:
    inp = _fwd_setup_inputs(seed)
    key = _jax.random.fold_in(_jax.random.key(seed), 7919)
    shape, _ = _output_shape()
    out = dict(inp)
    out["loss_target"] = _jax.random.normal(_jax.random.fold_in(key, 0), shape, _jnp.float32)
    for i, name in enumerate(TWIN_WEIGHTS):
        w = inp[name].astype(_jnp.float32)
        if MOMENT_SCALE is None:
            s = _jnp.sqrt(_jnp.mean(_jnp.square(w)) + 1e-30)
        else:
            s = MOMENT_SCALE[name]
        km, kv = _jax.random.split(_jax.random.fold_in(key, i + 1))
        out[name] = w
        out["m_" + name] = s * _jax.random.normal(km, w.shape, _jnp.float32)
        out["v_" + name] = (s * s) * _jax.random.uniform(kv, w.shape, _jnp.float32, 0.5, 1.5)
    if N_MICROBATCH > 1:
        for name, axis in PER_EXAMPLE_BATCH_AXIS.items():
            out[name] = _to_microbatches(out[name], axis)
    return {'x': out['x'], 'c': out['c'], 'gm_w_in': out['gm_w_in'], 'gm_ln_g': out['gm_ln_g'], 'gm_ln_b': out['gm_ln_b'], 'gm_w_s': out['gm_w_s'], 'gm_b_s': out['gm_b_s'], 'gm_w_out': out['gm_w_out'], 'hg_w_in': out['hg_w_in'], 'hg_lb': out['hg_lb'], 'hg_gn_g': out['hg_gn_g'], 'hg_w_out': out['hg_w_out'], 'ffn_w_up': out['ffn_w_up'], 'ffn_conv_w': out['ffn_conv_w'], 'ffn_conv_b': out['ffn_conv_b'], 'ffn_w_down': out['ffn_w_down'], 'norm_g': out['norm_g'], 'ada_w': out['ada_w'], 'ada_b': out['ada_b'], 'final_g': out['final_g'], 'loss_target': out['loss_target'], 'm_gm_w_in': out['m_gm_w_in'], 'm_gm_ln_g': out['m_gm_ln_g'], 'm_gm_ln_b': out['m_gm_ln_b'], 'm_gm_w_s': out['m_gm_w_s'], 'm_gm_b_s': out['m_gm_b_s'], 'm_gm_w_out': out['m_gm_w_out'], 'm_hg_w_in': out['m_hg_w_in'], 'm_hg_lb': out['m_hg_lb'], 'm_hg_gn_g': out['m_hg_gn_g'], 'm_hg_w_out': out['m_hg_w_out'], 'm_ffn_w_up': out['m_ffn_w_up'], 'm_ffn_conv_w': out['m_ffn_conv_w'], 'm_ffn_conv_b': out['m_ffn_conv_b'], 'm_ffn_w_down': out['m_ffn_w_down'], 'm_norm_g': out['m_norm_g'], 'm_ada_w': out['m_ada_w'], 'm_ada_b': out['m_ada_b'], 'm_final_g': out['m_final_g'], 'v_gm_w_in': out['v_gm_w_in'], 'v_gm_ln_g': out['v_gm_ln_g'], 'v_gm_ln_b': out['v_gm_ln_b'], 'v_gm_w_s': out['v_gm_w_s'], 'v_gm_b_s': out['v_gm_b_s'], 'v_gm_w_out': out['v_gm_w_out'], 'v_hg_w_in': out['v_hg_w_in'], 'v_hg_lb': out['v_hg_lb'], 'v_hg_gn_g': out['v_hg_gn_g'], 'v_hg_w_out': out['v_hg_w_out'], 'v_ffn_w_up': out['v_ffn_w_up'], 'v_ffn_conv_w': out['v_ffn_conv_w'], 'v_ffn_conv_b': out['v_ffn_conv_b'], 'v_ffn_w_down': out['v_ffn_w_down'], 'v_norm_g': out['v_norm_g'], 'v_ada_w': out['v_ada_w'], 'v_ada_b': out['v_ada_b'], 'v_final_g': out['v_final_g']}


def _loss(weights, diff, rest, loss_target):
    with _jax.named_scope("forward"):
        args = {**rest, TWIN_DIFF_INPUT: diff, **{k: w.astype(_WEIGHT_DTYPES[k]) for k, w in weights.items()}}
        y = _forward(args)
    with _jax.named_scope("loss_head"):
        err = _jnp.square(y.astype(_jnp.float32) - loss_target)
        return 0.5 * _jnp.sum(_jnp.mean(err, axis=-1)) if err.ndim else 0.5 * err


def _adamw(w, g, m, v):
    m = ADAM_B1 * m + (1.0 - ADAM_B1) * g
    v = ADAM_B2 * v + (1.0 - ADAM_B2) * _jnp.square(g)
    m_hat = m / (1.0 - ADAM_B1 ** ADAM_STEP)
    v_hat = v / (1.0 - ADAM_B2 ** ADAM_STEP)
    delta = -ADAM_LR * (m_hat / (_jnp.sqrt(v_hat) + ADAM_EPS) + ADAM_WD * w)
    return delta, m, v


def reference(x, c, gm_w_in, gm_ln_g, gm_ln_b, gm_w_s, gm_b_s, gm_w_out, hg_w_in, hg_lb, hg_gn_g, hg_w_out, ffn_w_up, ffn_conv_w, ffn_conv_b, ffn_w_down, norm_g, ada_w, ada_b, final_g, loss_target, m_gm_w_in, m_gm_ln_g, m_gm_ln_b, m_gm_w_s, m_gm_b_s, m_gm_w_out, m_hg_w_in, m_hg_lb, m_hg_gn_g, m_hg_w_out, m_ffn_w_up, m_ffn_conv_w, m_ffn_conv_b, m_ffn_w_down, m_norm_g, m_ada_w, m_ada_b, m_final_g, v_gm_w_in, v_gm_ln_g, v_gm_ln_b, v_gm_w_s, v_gm_b_s, v_gm_w_out, v_hg_w_in, v_hg_lb, v_hg_gn_g, v_hg_w_out, v_ffn_w_up, v_ffn_conv_w, v_ffn_conv_b, v_ffn_w_down, v_norm_g, v_ada_w, v_ada_b, v_final_g):
    given = dict(x=x, c=c, gm_w_in=gm_w_in, gm_ln_g=gm_ln_g, gm_ln_b=gm_ln_b, gm_w_s=gm_w_s, gm_b_s=gm_b_s, gm_w_out=gm_w_out, hg_w_in=hg_w_in, hg_lb=hg_lb, hg_gn_g=hg_gn_g, hg_w_out=hg_w_out, ffn_w_up=ffn_w_up, ffn_conv_w=ffn_conv_w, ffn_conv_b=ffn_conv_b, ffn_w_down=ffn_w_down, norm_g=norm_g, ada_w=ada_w, ada_b=ada_b, final_g=final_g, loss_target=loss_target, m_gm_w_in=m_gm_w_in, m_gm_ln_g=m_gm_ln_g, m_gm_ln_b=m_gm_ln_b, m_gm_w_s=m_gm_w_s, m_gm_b_s=m_gm_b_s, m_gm_w_out=m_gm_w_out, m_hg_w_in=m_hg_w_in, m_hg_lb=m_hg_lb, m_hg_gn_g=m_hg_gn_g, m_hg_w_out=m_hg_w_out, m_ffn_w_up=m_ffn_w_up, m_ffn_conv_w=m_ffn_conv_w, m_ffn_conv_b=m_ffn_conv_b, m_ffn_w_down=m_ffn_w_down, m_norm_g=m_norm_g, m_ada_w=m_ada_w, m_ada_b=m_ada_b, m_final_g=m_final_g, v_gm_w_in=v_gm_w_in, v_gm_ln_g=v_gm_ln_g, v_gm_ln_b=v_gm_ln_b, v_gm_w_s=v_gm_w_s, v_gm_b_s=v_gm_b_s, v_gm_w_out=v_gm_w_out, v_hg_w_in=v_hg_w_in, v_hg_lb=v_hg_lb, v_hg_gn_g=v_hg_gn_g, v_hg_w_out=v_hg_w_out, v_ffn_w_up=v_ffn_w_up, v_ffn_conv_w=v_ffn_conv_w, v_ffn_conv_b=v_ffn_conv_b, v_ffn_w_down=v_ffn_w_down, v_norm_g=v_norm_g, v_ada_w=v_ada_w, v_ada_b=v_ada_b, v_final_g=v_final_g)
    weights = {n: given[n] for n in TWIN_WEIGHTS}
    shared = {n: given[n] for n in SHARED_INPUTS}
    per_example = {n: given[n] for n in ['x', 'c']}
    grad_fn = _jax.value_and_grad(_loss, argnums=(0, 1))

    def one_microbatch(ex, loss_target):
        ex = dict(ex)
        diff = ex.pop(TWIN_DIFF_INPUT)
        return grad_fn(weights, diff, {**shared, **ex}, loss_target)

    if N_MICROBATCH == 1:
        loss, (grad_w, grad_x) = one_microbatch(per_example, given["loss_target"])
    else:
        def body(carry, xs):
            loss_sum, grad_sum = carry
            l_k, (gw_k, gx_k) = one_microbatch(xs[0], xs[1])
            with _jax.named_scope("update"):
                return (loss_sum + l_k, _jax.tree.map(_jnp.add, grad_sum, gw_k)), gx_k

        init = (_jnp.zeros((), _jnp.float32), _jax.tree.map(_jnp.zeros_like, weights))
        (loss, grad_w), grad_x = _jax.lax.scan(body, init, (per_example, given["loss_target"]))
    with _jax.named_scope("update"):
        delta_w, new_m, new_v = {}, {}, {}
        for n in TWIN_WEIGHTS:
            delta_w[n], new_m[n], new_v[n] = _adamw(weights[n], grad_w[n], given["m_" + n], given["v_" + n])
    return (loss, grad_x, *[grad_w[n] for n in TWIN_WEIGHTS], *[delta_w[n] for n in TWIN_WEIGHTS],
            *[new_m[n] for n in TWIN_WEIGHTS], *[new_v[n] for n in TWIN_WEIGHTS])
```

```python
import math

import jax
import jax.numpy as jnp
from jax import lax
from jax.experimental import pallas as pl
from jax.experimental.pallas import tpu as pltpu

f32 = jnp.float32
bf16 = jnp.bfloat16
MESH = pl.DeviceIdType.MESH

N_DEV = 8
D = 1024
DEPTH = 4
EPS = 1e-6
GM_WIDTH = 2048
GM_HEADS = 8
GM_HEAD_DIM = 256
GM_BLOCK = 128
CHUNK = 64
HG_HEADS = 8
HG_DIM = 128
FFN_HIDDEN = 2816
ADA_COLS = 6 * D // N_DEV

HG_SUB = 16
HG_TOKENS = 128

ADAM_LR = 0.001
ADAM_B1 = 0.9
ADAM_B2 = 0.999
ADAM_EPS = 1e-08
ADAM_WD = 0.01
ADAM_STEP = 10

V7X_VMEM_LIMIT = 56 * 1024 * 1024
LANES = 128


def _cparams(*sem):
    return pltpu.CompilerParams(dimension_semantics=sem or None, vmem_limit_bytes=V7X_VMEM_LIMIT)


def _tile(n, target, mult=LANES):
    best = None
    for t in range(mult, min(n, target) + 1, mult):
        if n % t == 0:
            best = t
    return best or n


def _gelu(x):
    return 0.5 * x * (1.0 + lax.erf(x * (1.0 / math.sqrt(2.0))))


def _gelu_grad(x):
    cdf = 0.5 * (1.0 + lax.erf(x * (1.0 / math.sqrt(2.0))))
    pdf = jnp.exp(-0.5 * x * x) * (1.0 / math.sqrt(2.0 * math.pi))
    return cdf + x * pdf


def _mesh_pos():
    return lax.axis_index("x"), lax.axis_index("y"), lax.axis_index("c")


def _flat(pos):
    return 4 * pos[0] + 2 * pos[1] + pos[2]


def _peer(pos, k):
    return ((1 - pos[0]) if k & 4 else pos[0], (1 - pos[1]) if k & 2 else pos[1], (1 - pos[2]) if k & 1 else pos[2])


def _exchange(arrs, gather, name):
    n = len(arrs)

    def body(*refs):
        ins, outs = refs[:n], refs[n:2 * n]
        send_sems, recv_sems, local_sems = refs[2 * n:]
        pos = _mesh_pos()
        me = _flat(pos)

        def src(i, dest):
            return ins[i] if gather else ins[i].at[dest]

        local = [pltpu.make_async_copy(src(i, me), outs[i].at[me], local_sems.at[i]) for i in range(n)]
        for cp in local:
            cp.start()
        sends = []
        for k in range(1, N_DEV):
            peer = _peer(pos, k)
            for i in range(n):
                cp = pltpu.make_async_remote_copy(
                    src_ref=src(i, _flat(peer)), dst_ref=outs[i].at[me],
                    send_sem=send_sems.at[i * 7 + k - 1], recv_sem=recv_sems.at[i * 7 + k - 1],
                    device_id=peer, device_id_type=MESH)
                cp.start()
                sends.append(cp)
        for k in range(1, N_DEV):
            peer = _peer(pos, k)
            for i in range(n):
                pltpu.make_async_remote_copy(
                    src_ref=src(i, _flat(peer)), dst_ref=outs[i].at[_flat(peer)],
                    send_sem=send_sems.at[i * 7 + k - 1], recv_sem=recv_sems.at[i * 7 + k - 1],
                    device_id=peer, device_id_type=MESH).wait_recv()
        for cp in sends:
            cp.wait_send()
        for cp in local:
            cp.wait()

    out_shapes = []
    for a in arrs:
        shp = (N_DEV,) + tuple(a.shape) if gather else tuple(a.shape)
        out_shapes.append(jax.ShapeDtypeStruct(shp, a.dtype))
    any_spec = pl.BlockSpec(memory_space=pl.ANY)
    return pl.pallas_call(
        body, name=name, out_shape=tuple(out_shapes),
        in_specs=[any_spec] * n, out_specs=tuple([any_spec] * n),
        scratch_shapes=[pltpu.SemaphoreType.DMA((7 * n,)), pltpu.SemaphoreType.DMA((7 * n,)),
                        pltpu.SemaphoreType.DMA((n,))],
    )(*arrs)


def all_gather(arrs, name):
    return _exchange(arrs, True, name)


def all_to_all(arrs, name):
    return _exchange(arrs, False, name)


def mm_nn(a, b, out_dtype, name, tm=512, tn=512):
    m, k = a.shape
    n = b.shape[1]
    tm, tn = _tile(m, tm, 8), _tile(n, tn)

    def body(a_ref, b_ref, o_ref):
        o_ref[...] = jnp.dot(a_ref[...], b_ref[...], preferred_element_type=f32).astype(o_ref.dtype)

    return pl.pallas_call(
        body, name=name, grid=(m // tm, n // tn),
        in_specs=[pl.BlockSpec((tm, k), lambda i, j: (i, 0)), pl.BlockSpec((k, tn), lambda i, j: (0, j))],
        out_specs=pl.BlockSpec((tm, tn), lambda i, j: (i, j)),
        out_shape=jax.ShapeDtypeStruct((m, n), out_dtype), compiler_params=_cparams("parallel", "parallel"),
    )(a, b)


def mm_nn_residual(a, b, x, gate, name, tm=512, tn=512):
    m, k = a.shape
    n = b.shape[1]
    tm, tn = _tile(m, tm, 8), _tile(n, tn)

    def body(a_ref, b_ref, x_ref, g_ref, y_ref, o_ref):
        y = jnp.dot(a_ref[...], b_ref[...], preferred_element_type=f32)
        y_ref[...] = y.astype(bf16)
        o_ref[...] = x_ref[...] + g_ref[...] * y

    return pl.pallas_call(
        body, name=name, grid=(m // tm, n // tn),
        in_specs=[pl.BlockSpec((tm, k), lambda i, j: (i, 0)), pl.BlockSpec((k, tn), lambda i, j: (0, j)),
                  pl.BlockSpec((tm, tn), lambda i, j: (i, j)), pl.BlockSpec((1, tn), lambda i, j: (0, j))],
        out_specs=(pl.BlockSpec((tm, tn), lambda i, j: (i, j)), pl.BlockSpec((tm, tn), lambda i, j: (i, j))),
        out_shape=(jax.ShapeDtypeStruct((m, n), bf16), jax.ShapeDtypeStruct((m, n), f32)),
        compiler_params=_cparams("parallel", "parallel"),
    )(a, b, x, gate)


def mm_nt(a, b, out_dtype, name, tm=512, tn=512):
    m, k = a.shape
    n = b.shape[0]
    tm, tn = _tile(m, tm, 8), _tile(n, tn)

    def body(a_ref, b_ref, o_ref):
        o_ref[...] = lax.dot_general(a_ref[...], b_ref[...], (((1,), (1,)), ((), ())),
                                     preferred_element_type=f32).astype(o_ref.dtype)

    return pl.pallas_call(
        body, name=name, grid=(m // tm, n // tn),
        in_specs=[pl.BlockSpec((tm, k), lambda i, j: (i, 0)), pl.BlockSpec((tn, k), lambda i, j: (j, 0))],
        out_specs=pl.BlockSpec((tm, tn), lambda i, j: (i, j)),
        out_shape=jax.ShapeDtypeStruct((m, n), out_dtype), compiler_params=_cparams("parallel", "parallel"),
    )(a, b)


def mm_tn(a, b, out_dtype, name, tm=512, tn=512):
    t, m = a.shape
    n = b.shape[1]
    tm, tn = _tile(m, tm), _tile(n, tn)

    def body(a_ref, b_ref, o_ref):
        o_ref[...] = lax.dot_general(a_ref[...], b_ref[...], (((0,), (0,)), ((), ())),
                                     preferred_element_type=f32).astype(o_ref.dtype)

    return pl.pallas_call(
        body, name=name, grid=(m // tm, n // tn),
        in_specs=[pl.BlockSpec((t, tm), lambda i, j: (0, i)), pl.BlockSpec((t, tn), lambda i, j: (0, j))],
        out_specs=pl.BlockSpec((tm, tn), lambda i, j: (i, j)),
        out_shape=jax.ShapeDtypeStruct((m, n), out_dtype), compiler_params=_cparams("parallel", "parallel"),
    )(a, b)


def _norm_fn(x, gn, sc, sh):
    r = lax.rsqrt(jnp.mean(x * x, axis=-1, keepdims=True) + EPS)
    return (x * r * gn) * (1.0 + sc) + sh


def _row(d):
    return pl.BlockSpec((1, d), lambda i: (0, 0))


def norm_fwd(x, gn, sc, sh, name, tm=512):
    t, d = x.shape
    tm = _tile(t, tm, 8)

    def body(x_ref, gn_ref, sc_ref, sh_ref, h_ref):
        h_ref[...] = _norm_fn(x_ref[...], gn_ref[...], sc_ref[...], sh_ref[...]).astype(bf16)

    return pl.pallas_call(
        body, name=name, grid=(t // tm,),
        in_specs=[pl.BlockSpec((tm, d), lambda i: (i, 0)), _row(d), _row(d), _row(d)],
        out_specs=pl.BlockSpec((tm, d), lambda i: (i, 0)),
        out_shape=jax.ShapeDtypeStruct((t, d), bf16), compiler_params=_cparams("parallel"),
    )(x, gn, sc, sh)


def norm_bwd(x, gn, sc, sh, dh, dres, name, tm=512):
    t, d = x.shape
    tm = _tile(t, tm, 8)

    def body(x_ref, gn_ref, sc_ref, sh_ref, dh_ref, dres_ref, dx_ref, dgn_ref, dsc_ref, dsh_ref):
        _, vjp = jax.vjp(_norm_fn, x_ref[...], gn_ref[...], sc_ref[...], sh_ref[...])
        dx, dgn, dsc, dsh = vjp(dh_ref[...].astype(f32))
        dx_ref[...] = dx + dres_ref[...]

        @pl.when(pl.program_id(0) == 0)
        def _():
            dgn_ref[...] = jnp.zeros_like(dgn_ref)
            dsc_ref[...] = jnp.zeros_like(dsc_ref)
            dsh_ref[...] = jnp.zeros_like(dsh_ref)

        dgn_ref[...] += dgn
        dsc_ref[...] += dsc
        dsh_ref[...] += dsh

    blk = pl.BlockSpec((tm, d), lambda i: (i, 0))
    vec = jax.ShapeDtypeStruct((1, d), f32)
    return pl.pallas_call(
        body, name=name, grid=(t // tm,),
        in_specs=[blk, _row(d), _row(d), _row(d), blk, blk],
        out_specs=(blk, _row(d), _row(d), _row(d)),
        out_shape=(jax.ShapeDtypeStruct((t, d), f32), vec, vec, vec), compiler_params=_cparams("arbitrary"),
    )(x, gn, sc, sh, dh, dres)


def _loss_fn(x, g, tgt):
    r = lax.rsqrt(jnp.mean(x * x, axis=-1, keepdims=True) + EPS)
    err = jnp.square(x * r * g - tgt)
    return 0.5 * jnp.sum(jnp.mean(err, axis=-1, keepdims=True), axis=0, keepdims=True)


def loss_head(x, g, tgt, name, tm=512):
    t, d = x.shape
    tm = _tile(t, tm, 8)

    def body(x_ref, g_ref, t_ref, loss_ref, dx_ref, dg_ref):
        loss, vjp = jax.vjp(_loss_fn, x_ref[...], g_ref[...], t_ref[...])
        dx, dg, _ = vjp(jnp.ones((1, 1), f32))
        dx_ref[...] = dx

        @pl.when(pl.program_id(0) == 0)
        def _():
            loss_ref[...] = jnp.zeros_like(loss_ref)
            dg_ref[...] = jnp.zeros_like(dg_ref)

        loss_ref[...] += loss
        dg_ref[...] += dg

    blk = pl.BlockSpec((tm, d), lambda i: (i, 0))
    return pl.pallas_call(
        body, name=name, grid=(t // tm,),
        in_specs=[blk, _row(d), blk],
        out_specs=(pl.BlockSpec((1, 1), lambda i: (0, 0)), blk, _row(d)),
        out_shape=(jax.ShapeDtypeStruct((1, 1), f32), jax.ShapeDtypeStruct((t, d), f32),
                   jax.ShapeDtypeStruct((1, d), f32)),
        compiler_params=_cparams("arbitrary"),
    )(x, g, tgt)


def rowdot(a, b, name, tm=512):
    t, d = a.shape
    tm = _tile(t, tm, 8)

    def body(a_ref, b_ref, o_ref):
        @pl.when(pl.program_id(0) == 0)
        def _():
            o_ref[...] = jnp.zeros_like(o_ref)

        o_ref[...] += jnp.sum(a_ref[...] * b_ref[...].astype(f32), axis=0, keepdims=True)

    blk = pl.BlockSpec((tm, d), lambda i: (i, 0))
    return pl.pallas_call(
        body, name=name, grid=(t // tm,), in_specs=[blk, blk], out_specs=_row(d),
        out_shape=jax.ShapeDtypeStruct((1, d), f32), compiler_params=_cparams("arbitrary"),
    )(a, b)


def scale_rows(a, g, name, tm=512):
    t, d = a.shape
    tm = _tile(t, tm, 8)

    def body(a_ref, g_ref, o_ref):
        o_ref[...] = (a_ref[...] * g_ref[...]).astype(bf16)

    blk = pl.BlockSpec((tm, d), lambda i: (i, 0))
    return pl.pallas_call(
        body, name=name, grid=(t // tm,), in_specs=[blk, _row(d)], out_specs=blk,
        out_shape=jax.ShapeDtypeStruct((t, d), bf16), compiler_params=_cparams("parallel"),
    )(a, g)


def _gm_block_fn(z, ws, bs, lng, lnb):
    u = _gelu(z[:, :GM_WIDTH])
    vg = _gelu(z[:, GM_WIDTH:])
    mu = jnp.mean(vg, axis=-1, keepdims=True)
    var = jnp.mean(jnp.square(vg - mu), axis=-1, keepdims=True)
    vn = (vg - mu) * lax.rsqrt(var + EPS) * lng + lnb
    row = lax.broadcasted_iota(jnp.int32, (GM_BLOCK, GM_BLOCK), 0) // CHUNK
    col = lax.broadcasted_iota(jnp.int32, (GM_BLOCK, GM_BLOCK), 1) // CHUNK
    parts = []
    for h in range(GM_HEADS):
        w = jnp.where(row >= col, ws[h], 0.0)
        cols = slice(h * GM_HEAD_DIM, (h + 1) * GM_HEAD_DIM)
        s = jnp.dot(w.astype(bf16), vn[:, cols].astype(bf16), preferred_element_type=f32) + bs[h]
        parts.append(u[:, cols] * s)
    return jnp.concatenate(parts, axis=1)


def _gm_param_specs():
    return [pl.BlockSpec((GM_HEADS, GM_BLOCK, GM_BLOCK), lambda i: (0, 0, 0)),
            pl.BlockSpec((GM_HEADS, GM_BLOCK, 1), lambda i: (0, 0, 0)), _row(GM_WIDTH), _row(GM_WIDTH)]


def gm_mix_fwd(z, ws, bs, lng, lnb, name):
    t = z.shape[0]

    def body(z_ref, ws_ref, bs_ref, lng_ref, lnb_ref, o_ref):
        o_ref[...] = _gm_block_fn(z_ref[...].astype(f32), ws_ref[...], bs_ref[...], lng_ref[...],
                                  lnb_ref[...]).astype(bf16)

    return pl.pallas_call(
        body, name=name, grid=(t // GM_BLOCK,),
        in_specs=[pl.BlockSpec((GM_BLOCK, 2 * GM_WIDTH), lambda i: (i, 0))] + _gm_param_specs(),
        out_specs=pl.BlockSpec((GM_BLOCK, GM_WIDTH), lambda i: (i, 0)),
        out_shape=jax.ShapeDtypeStruct((t, GM_WIDTH), bf16), compiler_params=_cparams("parallel"),
    )(z, ws, bs, lng, lnb)


def gm_mix_bwd(z, ws, bs, lng, lnb, dgated, name):
    t = z.shape[0]

    def body(z_ref, ws_ref, bs_ref, lng_ref, lnb_ref, dg_ref, dz_ref, dws_ref, dbs_ref, dlng_ref, dlnb_ref):
        _, vjp = jax.vjp(_gm_block_fn, z_ref[...].astype(f32), ws_ref[...], bs_ref[...], lng_ref[...], lnb_ref[...])
        dz, dws, dbs, dlng, dlnb = vjp(dg_ref[...].astype(f32))
        dz_ref[...] = dz.astype(bf16)

        @pl.when(pl.program_id(0) == 0)
        def _():
            dws_ref[...] = jnp.zeros_like(dws_ref)
            dbs_ref[...] = jnp.zeros_like(dbs_ref)
            dlng_ref[...] = jnp.zeros_like(dlng_ref)
            dlnb_ref[...] = jnp.zeros_like(dlnb_ref)

        dws_ref[...] += dws
        dbs_ref[...] += dbs
        dlng_ref[...] += dlng
        dlnb_ref[...] += dlnb

    zblk = pl.BlockSpec((GM_BLOCK, 2 * GM_WIDTH), lambda i: (i, 0))
    return pl.pallas_call(
        body, name=name, grid=(t // GM_BLOCK,),
        in_specs=[zblk] + _gm_param_specs() + [pl.BlockSpec((GM_BLOCK, GM_WIDTH), lambda i: (i, 0))],
        out_specs=tuple([zblk] + _gm_param_specs()),
        out_shape=(jax.ShapeDtypeStruct((t, 2 * GM_WIDTH), bf16),
                   jax.ShapeDtypeStruct((GM_HEADS, GM_BLOCK, GM_BLOCK), f32),
                   jax.ShapeDtypeStruct((GM_HEADS, GM_BLOCK, 1), f32),
                   jax.ShapeDtypeStruct((1, GM_WIDTH), f32), jax.ShapeDtypeStruct((1, GM_WIDTH), f32)),
        compiler_params=_cparams("arbitrary"),
    )(z, ws, bs, lng, lnb, dgated)


def _hg_block_fn(qp, fz, iv, gp, s0, lb, gn):
    n = HG_SUB
    sub_j = lax.broadcasted_iota(jnp.int32, (n, HG_DIM), 0)
    lane_i = lax.broadcasted_iota(jnp.int32, (n, n), 1)
    tri = (lax.broadcasted_iota(jnp.int32, (n, n), 1) <= lax.broadcasted_iota(jnp.int32, (n, n), 0)).astype(f32)
    state = s0
    outs = []
    for a in range(HG_TOKENS // n):
        rows = slice(a * n, (a + 1) * n)
        f = lb + (1.0 - lb) * jax.nn.sigmoid(fz[rows])
        g = jnp.log(f)
        k = 1.0 - f
        q = qp[rows] * jax.nn.sigmoid(qp[rows])
        v = iv[rows].astype(bf16)
        cum = jnp.dot(tri, g, precision=lax.Precision.HIGHEST, preferred_element_type=f32)
        tot = jnp.sum(g, axis=0, keepdims=True)
        scores_t = jnp.zeros((n, n), f32)
        for i in range(n):
            rel = jnp.where(sub_j <= i, cum[i:i + 1, :] - cum, -1e30)
            col = jnp.sum(q[i:i + 1, :] * k * jnp.exp(rel), axis=1, keepdims=True)
            scores_t = scores_t + jnp.where(lane_i == i, col, 0.0)
        o = lax.dot_general(scores_t.astype(bf16), v, (((0,), (0,)), ((), ())), preferred_element_type=f32)
        o = o + jnp.dot((q * jnp.exp(cum)).astype(bf16), state.astype(bf16), preferred_element_type=f32)
        update = lax.dot_general((k * jnp.exp(tot - cum)).astype(bf16), v, (((0,), (0,)), ((), ())),
                                 preferred_element_type=f32)
        state = jnp.exp(tot).T * state + update
        on = o * lax.rsqrt(jnp.mean(o * o, axis=-1, keepdims=True) + EPS) * gn
        outs.append(on * (gp[rows] * jax.nn.sigmoid(gp[rows])))
    return jnp.concatenate(outs, axis=0), state


def hg_scan_fwd(proj, lb, gn, name):
    t = proj.shape[0]
    nt = t // HG_TOKENS

    def body(q_ref, f_ref, i_ref, g_ref, lb_ref, gn_ref, y_ref, s_ref, state):
        @pl.when(pl.program_id(1) == 0)
        def _():
            state[...] = jnp.zeros_like(state)

        s_ref[0, 0] = state[...]
        y, s1 = _hg_block_fn(q_ref[...], f_ref[...], i_ref[...], g_ref[...], state[...], lb_ref[...], gn_ref[...])
        y_ref[...] = y.astype(bf16)
        state[...] = s1

    def part(p):
        return pl.BlockSpec((HG_TOKENS, HG_DIM), lambda h, i: (i, p * HG_HEADS + h))

    vec = pl.BlockSpec((1, HG_DIM), lambda h, i: (0, h))
    return pl.pallas_call(
        body, name=name, grid=(HG_HEADS, nt),
        in_specs=[part(0), part(1), part(2), part(3), vec, vec],
        out_specs=(pl.BlockSpec((HG_TOKENS, HG_DIM), lambda h, i: (i, h)),
                   pl.BlockSpec((1, 1, HG_DIM, HG_DIM), lambda h, i: (h, i, 0, 0))),
        out_shape=(jax.ShapeDtypeStruct((t, D), bf16), jax.ShapeDtypeStruct((HG_HEADS, nt, HG_DIM, HG_DIM), f32)),
        scratch_shapes=[pltpu.VMEM((HG_DIM, HG_DIM), f32)],
        compiler_params=_cparams("parallel", "arbitrary"),
    )(proj, proj, proj, proj, lb, gn)


def hg_scan_bwd(proj, lb, gn, states, dy, name):
    t = proj.shape[0]
    nt = t // HG_TOKENS

    def body(q_ref, f_ref, i_ref, g_ref, lb_ref, gn_ref, s_ref, dy_ref,
             dq_ref, df_ref, di_ref, dg_ref, dlb_ref, dgn_ref, dstate):
        @pl.when(pl.program_id(1) == 0)
        def _():
            dstate[...] = jnp.zeros_like(dstate)
            dlb_ref[...] = jnp.zeros_like(dlb_ref)
            dgn_ref[...] = jnp.zeros_like(dgn_ref)

        _, vjp = jax.vjp(_hg_block_fn, q_ref[...], f_ref[...], i_ref[...], g_ref[...], s_ref[0, 0],
                         lb_ref[...], gn_ref[...])
        dq, df, di, dg, ds0, dlb, dgn = vjp((dy_ref[...].astype(f32), dstate[...]))
        dq_ref[...] = dq.astype(bf16)
        df_ref[...] = df.astype(bf16)
        di_ref[...] = di.astype(bf16)
        dg_ref[...] = dg.astype(bf16)
        dstate[...] = ds0
        dlb_ref[...] += dlb
        dgn_ref[...] += dgn

    def part(p):
        return pl.BlockSpec((HG_TOKENS, HG_DIM), lambda h, i: (nt - 1 - i, p * HG_HEADS + h))

    vec = pl.BlockSpec((1, HG_DIM), lambda h, i: (0, h))
    tok = pl.BlockSpec((HG_TOKENS, HG_DIM), lambda h, i: (nt - 1 - i, h))
    big = jax.ShapeDtypeStruct((t, D), bf16)
    small = jax.ShapeDtypeStruct((1, D), f32)
    return pl.pallas_call(
        body, name=name, grid=(HG_HEADS, nt),
        in_specs=[part(0), part(1), part(2), part(3), vec, vec,
                  pl.BlockSpec((1, 1, HG_DIM, HG_DIM), lambda h, i: (h, nt - 1 - i, 0, 0)), tok],
        out_specs=(tok, tok, tok, tok, vec, vec),
        out_shape=(big, big, big, big, small, small),
        scratch_shapes=[pltpu.VMEM((HG_DIM, HG_DIM), f32)],
        compiler_params=_cparams("parallel", "arbitrary"),
    )(proj, proj, proj, proj, lb, gn, states, dy)


FFN_COLS = 1408
HALO = 8


def _ffn_specs(tm):
    nb = tm // HALO
    main_g = pl.BlockSpec((tm, FFN_COLS), lambda j, i: (i, j))
    main_v = pl.BlockSpec((tm, FFN_COLS), lambda j, i: (i, j + 2))
    halo_g = pl.BlockSpec((HALO, FFN_COLS), lambda j, i: (jnp.maximum(i * nb - 1, 0), j))
    halo_v = pl.BlockSpec((HALO, FFN_COLS), lambda j, i: (jnp.maximum(i * nb - 1, 0), j + 2))
    w_g = pl.BlockSpec((3, FFN_COLS), lambda j, i: (0, j))
    w_v = pl.BlockSpec((3, FFN_COLS), lambda j, i: (0, j + 2))
    b_g = pl.BlockSpec((1, FFN_COLS), lambda j, i: (0, j))
    b_v = pl.BlockSpec((1, FFN_COLS), lambda j, i: (0, j + 2))
    return [main_g, halo_g, main_v, halo_v, w_g, w_v, b_g, b_v]


def _conv_taps(main_ref, halo_ref, buf, tm):
    first = pl.program_id(1) == 0
    buf[pl.ds(0, HALO), :] = jnp.where(first, 0.0, halo_ref[...].astype(f32))
    buf[pl.ds(HALO, tm), :] = main_ref[...].astype(f32)
    return buf[pl.ds(HALO, tm), :], buf[pl.ds(HALO - 1, tm), :], buf[pl.ds(HALO - 2, tm), :]


def _conv(taps, w_ref, b_ref):
    a0, a1, a2 = taps
    return b_ref[...] + w_ref[0:1, :] * a2 + w_ref[1:2, :] * a1 + w_ref[2:3, :] * a0


def ffn_gate_fwd(a, cw, cb, name, tm=512):
    t = a.shape[0]
    tm = _tile(t, tm, 8)

    def body(ag_ref, hg_ref, av_ref, hv_ref, wg_ref, wv_ref, bg_ref, bv_ref, o_ref, buf_g, buf_v):
        yg = _conv(_conv_taps(ag_ref, hg_ref, buf_g, tm), wg_ref, bg_ref)
        yv = _conv(_conv_taps(av_ref, hv_ref, buf_v, tm), wv_ref, bv_ref)
        o_ref[...] = (_gelu(yg) * yv).astype(bf16)

    return pl.pallas_call(
        body, name=name, grid=(2, t // tm), in_specs=_ffn_specs(tm),
        out_specs=pl.BlockSpec((tm, FFN_COLS), lambda j, i: (i, j)),
        out_shape=jax.ShapeDtypeStruct((t, FFN_HIDDEN), bf16),
        scratch_shapes=[pltpu.VMEM((tm + HALO, FFN_COLS), f32), pltpu.VMEM((tm + HALO, FFN_COLS), f32)],
        compiler_params=_cparams("parallel", "arbitrary"),
    )(a, a, a, a, cw, cw, cb, cb)


def ffn_gate_bwd(a, cw, cb, dhid, name, tm=512):
    t = a.shape[0]
    tm = _tile(t, tm, 8)

    def body(ag_ref, hg_ref, av_ref, hv_ref, wg_ref, wv_ref, bg_ref, bv_ref, dh_ref,
             dy_ref, dwg_ref, dwv_ref, dbg_ref, dbv_ref, buf_g, buf_v):
        taps_g = _conv_taps(ag_ref, hg_ref, buf_g, tm)
        taps_v = _conv_taps(av_ref, hv_ref, buf_v, tm)
        yg = _conv(taps_g, wg_ref, bg_ref)
        yv = _conv(taps_v, wv_ref, bv_ref)
        dh = dh_ref[...].astype(f32)
        dyg = dh * yv * _gelu_grad(yg)
        dyv = dh * _gelu(yg)
        dy_ref[0] = dyg.astype(bf16)
        dy_ref[1] = dyv.astype(bf16)

        @pl.when(pl.program_id(1) == 0)
        def _():
            dwg_ref[...] = jnp.zeros_like(dwg_ref)
            dwv_ref[...] = jnp.zeros_like(dwv_ref)
            dbg_ref[...] = jnp.zeros_like(dbg_ref)
            dbv_ref[...] = jnp.zeros_like(dbv_ref)

        for dy, taps, dw_ref, db_ref in ((dyg, taps_g, dwg_ref, dbg_ref), (dyv, taps_v, dwv_ref, dbv_ref)):
            a0, a1, a2 = taps
            dw_ref[0:1, :] += jnp.sum(dy * a2, axis=0, keepdims=True)
            dw_ref[1:2, :] += jnp.sum(dy * a1, axis=0, keepdims=True)
            dw_ref[2:3, :] += jnp.sum(dy * a0, axis=0, keepdims=True)
            db_ref[...] += jnp.sum(dy, axis=0, keepdims=True)

    half_w = pl.BlockSpec((3, FFN_COLS), lambda j, i: (0, j))
    half_b = pl.BlockSpec((1, FFN_COLS), lambda j, i: (0, j))
    return pl.pallas_call(
        body, name=name, grid=(2, t // tm),
        in_specs=_ffn_specs(tm) + [pl.BlockSpec((tm, FFN_COLS), lambda j, i: (i, j))],
        out_specs=(pl.BlockSpec((2, tm, FFN_COLS), lambda j, i: (0, i, j)), half_w, half_w, half_b, half_b),
        out_shape=(jax.ShapeDtypeStruct((2, t, FFN_HIDDEN), bf16),
                   jax.ShapeDtypeStruct((3, FFN_HIDDEN), f32), jax.ShapeDtypeStruct((3, FFN_HIDDEN), f32),
                   jax.ShapeDtypeStruct((1, FFN_HIDDEN), f32), jax.ShapeDtypeStruct((1, FFN_HIDDEN), f32)),
        scratch_shapes=[pltpu.VMEM((tm + HALO, FFN_COLS), f32), pltpu.VMEM((tm + HALO, FFN_COLS), f32)],
        compiler_params=_cparams("parallel", "arbitrary"),
    )(a, a, a, a, cw, cw, cb, cb, dhid)


def conv_transpose(dy, cw, name, tm=512):
    _, t, fh = dy.shape
    tm = _tile(t, tm, 8)
    nb = tm // HALO
    last_halo = t // HALO - 1
    ncol = fh // FFN_COLS

    def body(main_ref, halo_ref, w_ref, o_ref, buf):
        last = pl.program_id(2) == pl.num_programs(2) - 1
        buf[pl.ds(0, tm), :] = main_ref[0].astype(f32)
        buf[pl.ds(tm, HALO), :] = jnp.where(last, 0.0, halo_ref[0].astype(f32))
        o_ref[0] = (w_ref[2:3, :] * buf[pl.ds(0, tm), :] + w_ref[1:2, :] * buf[pl.ds(1, tm), :]
                    + w_ref[0:1, :] * buf[pl.ds(2, tm), :]).astype(bf16)

    return pl.pallas_call(
        body, name=name, grid=(2, ncol, t // tm),
        in_specs=[pl.BlockSpec((1, tm, FFN_COLS), lambda p, j, i: (p, i, j)),
                  pl.BlockSpec((1, HALO, FFN_COLS), lambda p, j, i: (p, jnp.minimum((i + 1) * nb, last_halo), j)),
                  pl.BlockSpec((3, FFN_COLS), lambda p, j, i: (0, p * ncol + j))],
        out_specs=pl.BlockSpec((1, tm, FFN_COLS), lambda p, j, i: (p, i, j)),
        out_shape=jax.ShapeDtypeStruct((2, t, fh), bf16),
        scratch_shapes=[pltpu.VMEM((tm + HALO, FFN_COLS), f32)],
        compiler_params=_cparams("parallel", "parallel", "arbitrary"),
    )(dy, dy, cw)


def ada_mod(c_all, ada_w, ada_b_cols, name):
    cols = ada_w.shape[2]

    def body(c_ref, w_ref, b_ref, o_ref):
        c = c_ref[...]
        cond = (c * jax.nn.sigmoid(c)).astype(bf16)
        o_ref[0] = jnp.dot(cond, w_ref[0].astype(bf16), preferred_element_type=f32) + b_ref[0]

    return pl.pallas_call(
        body, name=name, grid=(DEPTH,),
        in_specs=[pl.BlockSpec((N_DEV, D), lambda i: (0, 0)), pl.BlockSpec((1, D, cols), lambda i: (i, 0, 0)),
                  pl.BlockSpec((1, 1, cols), lambda i: (i, 0, 0))],
        out_specs=pl.BlockSpec((1, N_DEV, cols), lambda i: (i, 0, 0)),
        out_shape=jax.ShapeDtypeStruct((DEPTH, N_DEV, cols), f32), compiler_params=_cparams("parallel"),
    )(c_all, ada_w, ada_b_cols)


def ada_grads(c_all, dmod_cols, dmod_all, name):
    cols = dmod_cols.shape[2]

    def body(c_ref, dm_ref, da_ref, dw_ref, db_ref):
        c = c_ref[...]
        cond = c * jax.nn.sigmoid(c)
        dw_ref[0] = lax.dot_general(cond, dm_ref[0], (((0,), (0,)), ((), ())), precision=lax.Precision.HIGHEST,
                                    preferred_element_type=f32)
        acc = da_ref[0, 0]
        for e in range(1, N_DEV):
            acc = acc + da_ref[e, 0]
        db_ref[0] = acc

    return pl.pallas_call(
        body, name=name, grid=(DEPTH,),
        in_specs=[pl.BlockSpec((N_DEV, D), lambda i: (0, 0)), pl.BlockSpec((1, N_DEV, cols), lambda i: (i, 0, 0)),
                  pl.BlockSpec((N_DEV, 1, 1, 6 * D), lambda i: (0, i, 0, 0))],
        out_specs=(pl.BlockSpec((1, D, cols), lambda i: (i, 0, 0)), pl.BlockSpec((1, 1, 6 * D), lambda i: (i, 0, 0))),
        out_shape=(jax.ShapeDtypeStruct((DEPTH, D, cols), f32), jax.ShapeDtypeStruct((DEPTH, 1, 6 * D), f32)),
        compiler_params=_cparams("parallel"),
    )(c_all, dmod_cols, dmod_all)


def lower_bound_fwd(hg_lb, name):
    n = hg_lb.shape[1]

    def body(l_ref, o_ref):
        o_ref[...] = jax.nn.sigmoid(l_ref[1:2, :] - l_ref[0:1, :])

    return pl.pallas_call(body, name=name, out_shape=jax.ShapeDtypeStruct((1, n), f32))(hg_lb)


def lower_bound_bwd(hg_lb, dlb, name):
    n = hg_lb.shape[1]

    def body(l_ref, d_ref, o_ref):
        p = jax.nn.sigmoid(l_ref[1:2, :] - l_ref[0:1, :])
        g = d_ref[...] * p * (1.0 - p)
        o_ref[0:1, :] = -g
        o_ref[1:2, :] = g

    return pl.pallas_call(body, name=name, out_shape=jax.ShapeDtypeStruct((2, n), f32))(hg_lb, dlb)


def _adamw(w, g, m, v):
    m = ADAM_B1 * m + (1.0 - ADAM_B1) * g
    v = ADAM_B2 * v + (1.0 - ADAM_B2) * jnp.square(g)
    m_hat = m / (1.0 - ADAM_B1 ** ADAM_STEP)
    v_hat = v / (1.0 - ADAM_B2 ** ADAM_STEP)
    delta = -ADAM_LR * (m_hat / (jnp.sqrt(v_hat) + ADAM_EPS) + ADAM_WD * w)
    return delta, m, v


def adam_reduced(parts, w, m, v, name, tr=128):
    r, c = w.shape
    tr = _tile(r, tr, 8)

    def body(p_ref, w_ref, m_ref, v_ref, g_ref, d_ref, mo_ref, vo_ref):
        g = p_ref[0].astype(f32)
        for j in range(1, N_DEV):
            g = g + p_ref[j].astype(f32)
        g_ref[...] = g
        d_ref[...], mo_ref[...], vo_ref[...] = _adamw(w_ref[...], g, m_ref[...], v_ref[...])

    blk = pl.BlockSpec((tr, c), lambda i: (i, 0))
    out = jax.ShapeDtypeStruct((r, c), f32)
    return pl.pallas_call(
        body, name=name, grid=(r // tr,),
        in_specs=[pl.BlockSpec((N_DEV, tr, c), lambda i: (0, i, 0)), blk, blk, blk],
        out_specs=(blk, blk, blk, blk), out_shape=(out, out, out, out), compiler_params=_cparams("parallel"),
    )(parts, w, m, v)


def adam_plain(g, w, m, v, name, tr=256):
    r, c = w.shape
    tr = _tile(r, tr, 8)

    def body(g_ref, w_ref, m_ref, v_ref, d_ref, mo_ref, vo_ref):
        d_ref[...], mo_ref[...], vo_ref[...] = _adamw(w_ref[...], g_ref[...], m_ref[...], v_ref[...])

    blk = pl.BlockSpec((tr, c), lambda i: (i, 0))
    out = jax.ShapeDtypeStruct((r, c), f32)
    return pl.pallas_call(
        body, name=name, grid=(r // tr,), in_specs=[blk, blk, blk, blk], out_specs=(blk, blk, blk),
        out_shape=(out, out, out), compiler_params=_cparams("parallel"),
    )(g, w, m, v)


def sum_parts(parts, name):
    _, r, c = parts.shape

    def body(p_ref, o_ref):
        acc = p_ref[0]
        for j in range(1, N_DEV):
            acc = acc + p_ref[j]
        o_ref[...] = acc

    return pl.pallas_call(body, name=name, out_shape=jax.ShapeDtypeStruct((r, c), f32))(parts)


def _cols_to_full(g):
    _, l, k, n = g.shape
    return jnp.transpose(g, (1, 2, 0, 3)).reshape(l, k, N_DEV * n)


def _rows_to_full(g):
    _, l, k, n = g.shape
    return jnp.transpose(g, (1, 0, 2, 3)).reshape(l, N_DEV * k, n)


def _full_to_cols(w):
    l, k, n = w.shape
    return jnp.transpose(w.reshape(l, k, N_DEV, n // N_DEV), (2, 0, 1, 3)).reshape(N_DEV, l * k, n // N_DEV)


def _full_to_rows(w):
    l, k, n = w.shape
    return jnp.transpose(w.reshape(l, N_DEV, k // N_DEV, n), (1, 0, 2, 3)).reshape(N_DEV, l * k // N_DEV, n)


def _pack(arrs, rows_mult=8):
    flat = jnp.concatenate([a.reshape(-1) for a in arrs])
    rows = -(-flat.shape[0] // LANES)
    rows = -(-rows // rows_mult) * rows_mult
    return jnp.pad(flat, (0, rows * LANES - flat.shape[0])).reshape(rows, LANES)


def _unpack(flat, shapes):
    out, at = [], 0
    for s in shapes:
        n = math.prod(s)
        out.append(flat[at:at + n].reshape(s))
        at += n
    return out


def kernel(x, c, gm_w_in, gm_ln_g, gm_ln_b, gm_w_s, gm_b_s, gm_w_out, hg_w_in, hg_lb, hg_gn_g, hg_w_out, ffn_w_up, ffn_conv_w, ffn_conv_b, ffn_w_down, norm_g, ada_w, ada_b, final_g, loss_target, m_gm_w_in, m_gm_ln_g, m_gm_ln_b, m_gm_w_s, m_gm_b_s, m_gm_w_out, m_hg_w_in, m_hg_lb, m_hg_gn_g, m_hg_w_out, m_ffn_w_up, m_ffn_conv_w, m_ffn_conv_b, m_ffn_w_down, m_norm_g, m_ada_w, m_ada_b, m_final_g, v_gm_w_in, v_gm_ln_g, v_gm_ln_b, v_gm_w_s, v_gm_b_s, v_gm_w_out, v_hg_w_in, v_hg_lb, v_hg_gn_g, v_hg_w_out, v_ffn_w_up, v_ffn_conv_w, v_ffn_conv_b, v_ffn_w_down, v_norm_g, v_ada_w, v_ada_b, v_final_g):
    me = _flat(_mesh_pos())
    xt = x[0]
    t = xt.shape[0]

    small_shapes = [(1, D), (2, HG_DIM), (2, HG_DIM), (DEPTH, 2, HG_DIM), (DEPTH, 3, 2 * FFN_HIDDEN // N_DEV)]
    (small_all,) = all_gather([_pack([c, hg_lb, hg_gn_g, norm_g, ffn_conv_w])], "gather_small")
    small_all = small_all.reshape(N_DEV, -1)
    at = 0
    pieces = []
    for s in small_shapes:
        n = math.prod(s)
        pieces.append(small_all[:, at:at + n].reshape((N_DEV,) + s))
        at += n
    c_all = pieces[0].reshape(N_DEV, D)
    hg_lb_full = jnp.transpose(pieces[1], (1, 0, 2)).reshape(2, D)
    hg_gn_full = jnp.transpose(pieces[2], (1, 0, 2)).reshape(2, D)
    norm_g_full = jnp.transpose(pieces[3], (1, 2, 0, 3)).reshape(DEPTH, 2, D)
    conv_w_full = jnp.transpose(pieces[4], (1, 2, 0, 3)).reshape(DEPTH, 3, 2 * FFN_HIDDEN)

    lb1 = lower_bound_fwd(hg_lb_full, "lower_bound")
    lbs = [jnp.zeros((1, D), f32), lb1]

    ada_b_cols = lax.dynamic_slice(ada_b, (0, me * ADA_COLS), (DEPTH, ADA_COLS)).reshape(DEPTH, 1, ADA_COLS)
    mod_cols = ada_mod(c_all, ada_w, ada_b_cols, "ada_mod")
    (mod_mine,) = all_to_all([jnp.transpose(mod_cols, (1, 0, 2))], "mod_to_examples")
    mod = jnp.transpose(mod_mine, (1, 0, 2)).reshape(DEPTH, 6, 1, D)

    w_shards = [gm_w_in, gm_w_out, hg_w_in, hg_w_out, ffn_w_up, ffn_w_down]
    col_sharded = [True, False, True, False, True, False]
    gathered = all_gather([w.astype(bf16).reshape(-1, w.shape[-1]) for w in w_shards], "gather_weights")
    full = []
    for w, g, is_col in zip(w_shards, gathered, col_sharded):
        g = g.reshape((N_DEV,) + w.shape)
        full.append(_cols_to_full(g) if is_col else _rows_to_full(g))
    w_gm_in, w_gm_out, w_hg_in, w_hg_out, w_up, w_down = full

    saved = []
    xcur = xt
    for i in range(DEPTH):
        j = i // 2
        sh1, sc1, g1, sh2, sc2, g2 = [mod[i, p] for p in range(6)]
        gn1, gn2 = norm_g_full[i, 0:1], norm_g_full[i, 1:2]
        s = {"x0": xcur}
        h = norm_fwd(xcur, gn1, sc1, sh1, f"norm1_{i}")
        s["h"] = h
        if i % 2 == 0:
            z = mm_nn(h, w_gm_in[j], bf16, f"gm_in_{i}")
            bs = gm_b_s[j].reshape(GM_HEADS, GM_BLOCK, 1)
            mixed = gm_mix_fwd(z, gm_w_s[j], bs, gm_ln_g[j:j + 1], gm_ln_b[j:j + 1], f"gm_mix_{i}")
            s["z"] = z
            w_out = w_gm_out[j]
        else:
            proj = mm_nn(h, w_hg_in[j], f32, f"hg_in_{i}")
            mixed, states = hg_scan_fwd(proj, lbs[j], hg_gn_full[j:j + 1], f"hg_scan_{i}")
            s["proj"], s["states"] = proj, states
            w_out = w_hg_out[j]
        s["mixed"] = mixed
        y, x1 = mm_nn_residual(mixed, w_out, xcur, g1, f"mix_out_{i}")
        s["y"], s["x1"] = y, x1
        h2 = norm_fwd(x1, gn2, sc2, sh2, f"norm2_{i}")
        a = mm_nn(h2, w_up[i], bf16, f"ffn_up_{i}")
        hid = ffn_gate_fwd(a, conv_w_full[i], ffn_conv_b[i:i + 1], f"ffn_gate_{i}")
        fo, x2 = mm_nn_residual(hid, w_down[i], x1, g2, f"ffn_down_{i}")
        s["h2"], s["a"], s["hid"], s["f"] = h2, a, hid, fo
        saved.append(s)
        xcur = x2

    loss_part, dx, d_final_g = loss_head(xcur, final_g.reshape(1, D), loss_target[0], "loss_head")
    loss = lax.psum(loss_part[0, 0], ("x", "y", "c"))

    dmod = [None] * DEPTH
    d_norm_g = [None] * DEPTH
    d_gm = {k: [None, None] for k in ("w_in", "w_out", "ws", "bs", "lng", "lnb")}
    d_hg = {k: [None, None] for k in ("w_in", "w_out", "lb", "gn")}
    d_ffn = {k: [None] * DEPTH for k in ("w_up", "w_down", "cw", "cb")}
    for i in reversed(range(DEPTH)):
        j = i // 2
        s = saved[i]
        sh1, sc1, g1, sh2, sc2, g2 = [mod[i, p] for p in range(6)]
        gn1, gn2 = norm_g_full[i, 0:1], norm_g_full[i, 1:2]
        dg2 = rowdot(dx, s["f"], f"dgate2_{i}")
        df = scale_rows(dx, g2, f"dffn_out_{i}")
        d_ffn["w_down"][i] = mm_tn(s["hid"], df, bf16, f"dw_down_{i}")
        dhid = mm_nt(df, w_down[i], bf16, f"dhid_{i}")
        dyc, dwg, dwv, dbg, dbv = ffn_gate_bwd(s["a"], conv_w_full[i], ffn_conv_b[i:i + 1], dhid, f"ffn_gate_bwd_{i}")
        d_ffn["cw"][i] = jnp.concatenate([dwg, dwv], axis=1)
        d_ffn["cb"][i] = jnp.concatenate([dbg, dbv], axis=1)
        da = conv_transpose(dyc, conv_w_full[i], f"conv_t_{i}")
        da = jnp.transpose(da, (1, 0, 2)).reshape(t, 2 * FFN_HIDDEN)
        d_ffn["w_up"][i] = mm_tn(s["h2"], da, bf16, f"dw_up_{i}")
        dh2 = mm_nt(da, w_up[i], bf16, f"dh2_{i}")
        dx1, dgn2, dsc2, dsh2 = norm_bwd(s["x1"], gn2, sc2, sh2, dh2, dx, f"norm2_bwd_{i}")
        dg1 = rowdot(dx1, s["y"], f"dgate1_{i}")
        dy = scale_rows(dx1, g1, f"dmix_out_{i}")
        if i % 2 == 0:
            d_gm["w_out"][j] = mm_tn(s["mixed"], dy, bf16, f"dw_gm_out_{i}")
            dmixed = mm_nt(dy, w_gm_out[j], bf16, f"dgated_{i}")
            bs = gm_b_s[j].reshape(GM_HEADS, GM_BLOCK, 1)
            dz, dws, dbs, dlng, dlnb = gm_mix_bwd(s["z"], gm_w_s[j], bs, gm_ln_g[j:j + 1], gm_ln_b[j:j + 1], dmixed,
                                                  f"gm_mix_bwd_{i}")
            d_gm["ws"][j], d_gm["bs"][j], d_gm["lng"][j], d_gm["lnb"][j] = dws, dbs.reshape(GM_HEADS, GM_BLOCK), dlng, dlnb
            d_gm["w_in"][j] = mm_tn(s["h"], dz, bf16, f"dw_gm_in_{i}")
            dh = mm_nt(dz, w_gm_in[j], bf16, f"dh_gm_{i}")
        else:
            d_hg["w_out"][j] = mm_tn(s["mixed"], dy, bf16, f"dw_hg_out_{i}")
            dmixed = mm_nt(dy, w_hg_out[j], bf16, f"dyin_{i}")
            dq, dfz, di, dgp, dlb, dgn = hg_scan_bwd(s["proj"], lbs[j], hg_gn_full[j:j + 1], s["states"], dmixed,
                                                     f"hg_scan_bwd_{i}")
            d_hg["lb"][j], d_hg["gn"][j] = dlb, dgn
            dproj = jnp.concatenate([dq, dfz, di, dgp], axis=1)
            d_hg["w_in"][j] = mm_tn(s["h"], dproj, bf16, f"dw_hg_in_{i}")
            dh = mm_nt(dproj, w_hg_in[j], bf16, f"dh_hg_{i}")
        dx, dgn1, dsc1, dsh1 = norm_bwd(s["x0"], gn1, sc1, sh1, dh, dx1, f"norm1_bwd_{i}")
        dmod[i] = jnp.concatenate([dsh1, dsc1, dg1, dsh2, dsc2, dg2], axis=1)
        d_norm_g[i] = jnp.concatenate([dgn1, dgn2], axis=0)
    grad_x = dx.reshape(1, t, D)

    (dmod_all,) = all_gather([jnp.concatenate(dmod, axis=0)], "gather_dmod")
    dmod_cols = jnp.transpose(lax.dynamic_slice(dmod_all, (0, 0, me * ADA_COLS), (N_DEV, DEPTH, ADA_COLS)), (1, 0, 2))
    g_ada_w, g_ada_b = ada_grads(c_all, dmod_cols, dmod_all.reshape(N_DEV, DEPTH, 1, 6 * D), "ada_grads")
    g_ada_b = g_ada_b.reshape(DEPTH, 6 * D)

    small_partials = [jnp.concatenate(d_gm["lng"], axis=0), jnp.concatenate(d_gm["lnb"], axis=0),
                      jnp.stack(d_gm["ws"]), jnp.stack(d_gm["bs"]), jnp.concatenate(d_ffn["cb"], axis=0),
                      d_final_g, d_hg["lb"][1], jnp.concatenate(d_hg["gn"], axis=0), jnp.stack(d_norm_g),
                      jnp.stack(d_ffn["cw"])]
    partial_shapes = [p.shape for p in small_partials]
    packed = _pack(small_partials, rows_mult=8 * N_DEV)
    rows = packed.shape[0] // N_DEV
    (recv,) = all_to_all([packed.reshape(N_DEV, rows, LANES)], "small_grads_exchange")
    (summed,) = all_gather([sum_parts(recv, "small_grads_sum")], "small_grads_gather")
    g_ln_g, g_ln_b, g_ws, g_bs, g_cb, g_final, g_lb1, g_gn, g_norm, g_cw = _unpack(summed.reshape(-1), partial_shapes)
    g_final = g_final.reshape(D)

    def my_cols(a, n):
        start = (0,) * (a.ndim - 1) + (me * n,)
        return lax.dynamic_slice(a, start, a.shape[:-1] + (n,))

    g_hg_lb = lower_bound_bwd(hg_lb, my_cols(g_lb1, HG_DIM), "lower_bound_bwd")
    g_hg_gn = my_cols(g_gn, HG_DIM)
    g_norm_g = my_cols(g_norm, HG_DIM)
    g_conv_w = my_cols(g_cw, 2 * FFN_HIDDEN // N_DEV)

    dw_full = [jnp.stack(d_gm["w_in"]), jnp.stack(d_gm["w_out"]), jnp.stack(d_hg["w_in"]), jnp.stack(d_hg["w_out"]),
               jnp.stack(d_ffn["w_up"]), jnp.stack(d_ffn["w_down"])]
    blocked = [_full_to_cols(d) if is_col else _full_to_rows(d) for d, is_col in zip(dw_full, col_sharded)]
    received = all_to_all(blocked, "weight_grads_exchange")
    big_m = [m_gm_w_in, m_gm_w_out, m_hg_w_in, m_hg_w_out, m_ffn_w_up, m_ffn_w_down]
    big_v = [v_gm_w_in, v_gm_w_out, v_hg_w_in, v_hg_w_out, v_ffn_w_up, v_ffn_w_down]
    big = []
    for idx, (w, m_, v_, parts) in enumerate(zip(w_shards, big_m, big_v, received)):
        two_d = (-1, w.shape[-1])
        outs = adam_reduced(parts, w.reshape(two_d), m_.reshape(two_d), v_.reshape(two_d), f"adam_big_{idx}")
        big.append([o.reshape(w.shape) for o in outs])
    (g_gm_w_in, d_gm_w_in, nm_gm_w_in, nv_gm_w_in), (g_gm_w_out, d_gm_w_out, nm_gm_w_out, nv_gm_w_out), \
        (g_hg_w_in, d_hg_w_in, nm_hg_w_in, nv_hg_w_in), (g_hg_w_out, d_hg_w_out, nm_hg_w_out, nv_hg_w_out), \
        (g_ffn_w_up, d_ffn_w_up, nm_ffn_w_up, nv_ffn_w_up), (g_ffn_w_down, d_ffn_w_down, nm_ffn_w_down, nv_ffn_w_down) = big

    two_d = (-1, ADA_COLS)
    d_ada_w, nm_ada_w, nv_ada_w = [o.reshape(ada_w.shape) for o in adam_plain(
        g_ada_w.reshape(two_d), ada_w.reshape(two_d), m_ada_w.reshape(two_d), v_ada_w.reshape(two_d), "adam_ada_w")]

    small_g = [g_ln_g, g_ln_b, g_ws, g_bs, g_cb, g_ada_b, g_final, g_hg_lb, g_hg_gn, g_norm_g, g_conv_w]
    small_w = [gm_ln_g, gm_ln_b, gm_w_s, gm_b_s, ffn_conv_b, ada_b, final_g, hg_lb, hg_gn_g, norm_g, ffn_conv_w]
    small_m = [m_gm_ln_g, m_gm_ln_b, m_gm_w_s, m_gm_b_s, m_ffn_conv_b, m_ada_b, m_final_g, m_hg_lb, m_hg_gn_g, m_norm_g, m_ffn_conv_w]
    small_v = [v_gm_ln_g, v_gm_ln_b, v_gm_w_s, v_gm_b_s, v_ffn_conv_b, v_ada_b, v_final_g, v_hg_lb, v_hg_gn_g, v_norm_g, v_ffn_conv_w]
    shapes = [w.shape for w in small_w]
    small_g = [g.reshape(s) for g, s in zip(small_g, shapes)]
    outs = adam_plain(_pack(small_g), _pack(small_w), _pack(small_m), _pack(small_v), "adam_small")
    (d_ln_g, d_ln_b, d_ws, d_bs, d_cb, d_ada_b, d_final, d_hg_lb, d_hg_gn, d_norm_g_, d_conv_w), \
        (nm_ln_g, nm_ln_b, nm_ws, nm_bs, nm_cb, nm_ada_b, nm_final, nm_hg_lb, nm_hg_gn, nm_norm_g, nm_conv_w), \
        (nv_ln_g, nv_ln_b, nv_ws, nv_bs, nv_cb, nv_ada_b, nv_final, nv_hg_lb, nv_hg_gn, nv_norm_g, nv_conv_w) = [
            _unpack(o.reshape(-1), shapes) for o in outs]
    g_ln_g, g_ln_b, g_ws, g_bs, g_cb, g_ada_b, g_final, g_hg_lb, g_hg_gn, g_norm_g, g_conv_w = small_g

    grads = (g_gm_w_in, g_ln_g, g_ln_b, g_ws, g_bs, g_gm_w_out, g_hg_w_in, g_hg_lb, g_hg_gn, g_hg_w_out,
             g_ffn_w_up, g_conv_w, g_cb, g_ffn_w_down, g_norm_g, g_ada_w, g_ada_b, g_final)
    deltas = (d_gm_w_in, d_ln_g, d_ln_b, d_ws, d_bs, d_gm_w_out, d_hg_w_in, d_hg_lb, d_hg_gn, d_hg_w_out,
              d_ffn_w_up, d_conv_w, d_cb, d_ffn_w_down, d_norm_g_, d_ada_w, d_ada_b, d_final)
    new_m = (nm_gm_w_in, nm_ln_g, nm_ln_b, nm_ws, nm_bs, nm_gm_w_out, nm_hg_w_in, nm_hg_lb, nm_hg_gn, nm_hg_w_out,
             nm_ffn_w_up, nm_conv_w, nm_cb, nm_ffn_w_down, nm_norm_g, nm_ada_w, nm_ada_b, nm_final)
    new_v = (nv_gm_w_in, nv_ln_g, nv_ln_b, nv_ws, nv_bs, nv_gm_w_out, nv_hg_w_in, nv_hg_lb, nv_hg_gn, nv_hg_w_out,
             nv_ffn_w_up, nv_conv_w, nv_cb, nv_ffn_w_down, nv_norm_g, nv_ada_w, nv_ada_b, nv_final)
    return (loss, grad_x) + grads + deltas + new_m + new_v
```

```python
import functools
import math

import jax
import jax.numpy as jnp
from jax import lax
from jax.experimental import pallas as pl
from jax.experimental.pallas import tpu as pltpu

f32 = jnp.float32
bf16 = jnp.bfloat16
MESH = pl.DeviceIdType.MESH

N_DEV = 8
D = 1024
DEPTH = 4
EPS = 1e-6
GM_WIDTH = 2048
GM_HEADS = 8
GM_HEAD_DIM = 256
GM_BLOCK = 128
CHUNK = 64
HG_HEADS = 8
HG_DIM = 128
FFN_HIDDEN = 2816
ADA_COLS = 6 * D // N_DEV

HG_SUB = 16
HG_TOKENS = 128

ADAM_LR = 0.001
ADAM_B1 = 0.9
ADAM_B2 = 0.999
ADAM_EPS = 1e-08
ADAM_WD = 0.01
ADAM_STEP = 10

V7X_VMEM_LIMIT = 56 * 1024 * 1024
LANES = 128


def _cparams(*sem):
    return pltpu.CompilerParams(dimension_semantics=sem or None, vmem_limit_bytes=V7X_VMEM_LIMIT)


def _tile(n, target, mult=LANES):
    best = None
    for t in range(mult, min(n, target) + 1, mult):
        if n % t == 0:
            best = t
    return best or n


def _gelu(x):
    return 0.5 * x * (1.0 + lax.erf(x * (1.0 / math.sqrt(2.0))))


def _gelu_grad(x):
    cdf = 0.5 * (1.0 + lax.erf(x * (1.0 / math.sqrt(2.0))))
    pdf = jnp.exp(-0.5 * x * x) * (1.0 / math.sqrt(2.0 * math.pi))
    return cdf + x * pdf


def _mesh_pos():
    return lax.axis_index("x"), lax.axis_index("y"), lax.axis_index("c")


def _flat(pos):
    return 4 * pos[0] + 2 * pos[1] + pos[2]


def _peer(pos, k):
    return ((1 - pos[0]) if k & 4 else pos[0], (1 - pos[1]) if k & 2 else pos[1], (1 - pos[2]) if k & 1 else pos[2])


def _exchange_copies(ins, outs, send_sems, recv_sems, local_sems, gather):
    pos = _mesh_pos()
    me = _flat(pos)

    def src(i, dest):
        return ins[i] if gather else ins[i].at[dest]

    local = [pltpu.make_async_copy(src(i, me), outs[i].at[me], local_sems.at[i]) for i in range(len(ins))]
    sends, recvs = [], []
    for k in range(1, N_DEV):
        peer = _peer(pos, k)
        there = _flat(peer)
        for i in range(len(ins)):
            sems = dict(send_sem=send_sems.at[i * 7 + k - 1], recv_sem=recv_sems.at[i * 7 + k - 1],
                        device_id=peer, device_id_type=MESH)
            sends.append(pltpu.make_async_remote_copy(src_ref=src(i, there), dst_ref=outs[i].at[me], **sems))
            recvs.append(pltpu.make_async_remote_copy(src_ref=src(i, there), dst_ref=outs[i].at[there], **sems))
    return local, sends, recvs


def _exchange_start(*refs):
    local, sends, _ = _exchange_copies(*refs)
    for cp in local + sends:
        cp.start()


def _exchange_wait(*refs):
    local, sends, recvs = _exchange_copies(*refs)
    for cp in recvs:
        cp.wait_recv()
    for cp in sends:
        cp.wait_send()
    for cp in local:
        cp.wait()


def _exchange_out_shape(a, gather):
    return jax.ShapeDtypeStruct((N_DEV,) + tuple(a.shape) if gather else tuple(a.shape), a.dtype)


def _exchange_sems(n):
    return [pltpu.SemaphoreType.DMA((7 * n,)), pltpu.SemaphoreType.DMA((7 * n,)), pltpu.SemaphoreType.DMA((n,))]


ANY = pl.BlockSpec(memory_space=pl.ANY)


def _exchange(arrs, gather, name):
    n = len(arrs)

    def body(*refs):
        ins, outs = refs[:n], refs[n:2 * n]
        _exchange_start(ins, outs, *refs[2 * n:], gather)
        _exchange_wait(ins, outs, *refs[2 * n:], gather)

    return pl.pallas_call(
        body, name=name, out_shape=tuple(_exchange_out_shape(a, gather) for a in arrs),
        in_specs=[ANY] * n, out_specs=tuple([ANY] * n), scratch_shapes=_exchange_sems(n),
    )(*arrs)


def all_gather(arrs, name):
    return _exchange(arrs, True, name)


def all_to_all(arrs, name):
    return _exchange(arrs, False, name)


class Rider:
    def __init__(self, arr, gather):
        self.arr, self.gather = arr, gather


def _call(body, *, name, grid, in_specs, out_specs, out_shape, semantics, scratch_shapes=(), rider=None):
    if rider is None:
        return pl.pallas_call(body, name=name, grid=grid, in_specs=in_specs, out_specs=tuple(out_specs),
                              out_shape=tuple(out_shape), scratch_shapes=list(scratch_shapes),
                              compiler_params=_cparams(*semantics))
    n_in, n_out, n_scr = len(in_specs), len(out_specs), len(scratch_shapes)
    gather = rider.gather

    def hosted(*refs):
        ins, r_in = refs[:n_in], refs[n_in]
        outs, r_out = refs[n_in + 1:n_in + 1 + n_out], refs[n_in + 1 + n_out]
        scratch = refs[n_in + 2 + n_out:n_in + 2 + n_out + n_scr]
        sems = refs[n_in + 2 + n_out + n_scr:]
        first = functools.reduce(jnp.logical_and, [pl.program_id(a) == 0 for a in range(len(grid))])
        last = functools.reduce(jnp.logical_and, [pl.program_id(a) == grid[a] - 1 for a in range(len(grid))])

        @pl.when(first)
        def _():
            _exchange_start([r_in], [r_out], *sems, gather)

        body(*ins, *outs, *scratch)

        @pl.when(last)
        def _():
            _exchange_wait([r_in], [r_out], *sems, gather)

    call = pl.pallas_call(
        hosted, name=name, grid=grid, in_specs=list(in_specs) + [ANY], out_specs=tuple(out_specs) + (ANY,),
        out_shape=tuple(out_shape) + (_exchange_out_shape(rider.arr, gather),),
        scratch_shapes=list(scratch_shapes) + _exchange_sems(1),
        compiler_params=_cparams(*(("arbitrary",) * len(grid))))
    return lambda *args: call(*args, rider.arr)


def mm_nn(a, b, out_dtype, name, tm=512, tn=512, rider=None):
    m, k = a.shape
    n = b.shape[1]
    tm, tn = _tile(m, tm, 8), _tile(n, tn)

    def body(a_ref, b_ref, o_ref):
        o_ref[...] = jnp.dot(a_ref[...], b_ref[...], preferred_element_type=f32).astype(o_ref.dtype)

    return _call(
        body, name=name, grid=(m // tm, n // tn),
        in_specs=[pl.BlockSpec((tm, k), lambda i, j: (i, 0)), pl.BlockSpec((k, tn), lambda i, j: (0, j))],
        out_specs=[pl.BlockSpec((tm, tn), lambda i, j: (i, j))],
        out_shape=[jax.ShapeDtypeStruct((m, n), out_dtype)], semantics=("parallel", "parallel"), rider=rider,
    )(a, b)


def mm_nn_residual(a, b, x, gate, name, tm=512, tn=512, rider=None):
    m, k = a.shape
    n = b.shape[1]
    tm, tn = _tile(m, tm, 8), _tile(n, tn)

    def body(a_ref, b_ref, x_ref, g_ref, y_ref, o_ref):
        y = jnp.dot(a_ref[...], b_ref[...], preferred_element_type=f32)
        y_ref[...] = y.astype(bf16)
        o_ref[...] = x_ref[...] + g_ref[...] * y

    return _call(
        body, name=name, grid=(m // tm, n // tn),
        in_specs=[pl.BlockSpec((tm, k), lambda i, j: (i, 0)), pl.BlockSpec((k, tn), lambda i, j: (0, j)),
                  pl.BlockSpec((tm, tn), lambda i, j: (i, j)), pl.BlockSpec((1, tn), lambda i, j: (0, j))],
        out_specs=[pl.BlockSpec((tm, tn), lambda i, j: (i, j)), pl.BlockSpec((tm, tn), lambda i, j: (i, j))],
        out_shape=[jax.ShapeDtypeStruct((m, n), bf16), jax.ShapeDtypeStruct((m, n), f32)],
        semantics=("parallel", "parallel"), rider=rider,
    )(a, b, x, gate)


def mm_nt(a, b, out_dtype, name, tm=512, tn=512, rider=None):
    m, k = a.shape
    n = b.shape[0]
    tm, tn = _tile(m, tm, 8), _tile(n, tn)

    def body(a_ref, b_ref, o_ref):
        o_ref[...] = lax.dot_general(a_ref[...], b_ref[...], (((1,), (1,)), ((), ())),
                                     preferred_element_type=f32).astype(o_ref.dtype)

    return _call(
        body, name=name, grid=(m // tm, n // tn),
        in_specs=[pl.BlockSpec((tm, k), lambda i, j: (i, 0)), pl.BlockSpec((tn, k), lambda i, j: (j, 0))],
        out_specs=[pl.BlockSpec((tm, tn), lambda i, j: (i, j))],
        out_shape=[jax.ShapeDtypeStruct((m, n), out_dtype)], semantics=("parallel", "parallel"), rider=rider,
    )(a, b)


def mm_tn(a, b, out_dtype, name, tm=512, tn=512, rider=None):
    t, m = a.shape
    n = b.shape[1]
    tm, tn = _tile(m, tm), _tile(n, tn)

    def body(a_ref, b_ref, o_ref):
        o_ref[...] = lax.dot_general(a_ref[...], b_ref[...], (((0,), (0,)), ((), ())),
                                     preferred_element_type=f32).astype(o_ref.dtype)

    return _call(
        body, name=name, grid=(m // tm, n // tn),
        in_specs=[pl.BlockSpec((t, tm), lambda i, j: (0, i)), pl.BlockSpec((t, tn), lambda i, j: (0, j))],
        out_specs=[pl.BlockSpec((tm, tn), lambda i, j: (i, j))],
        out_shape=[jax.ShapeDtypeStruct((m, n), out_dtype)], semantics=("parallel", "parallel"), rider=rider,
    )(a, b)


def _norm_fn(x, gn, sc, sh):
    r = lax.rsqrt(jnp.mean(x * x, axis=-1, keepdims=True) + EPS)
    return (x * r * gn) * (1.0 + sc) + sh


def _row(d):
    return pl.BlockSpec((1, d), lambda i: (0, 0))


def norm_fwd(x, gn, sc, sh, name, tm=512):
    t, d = x.shape
    tm = _tile(t, tm, 8)

    def body(x_ref, gn_ref, sc_ref, sh_ref, h_ref):
        h_ref[...] = _norm_fn(x_ref[...], gn_ref[...], sc_ref[...], sh_ref[...]).astype(bf16)

    return pl.pallas_call(
        body, name=name, grid=(t // tm,),
        in_specs=[pl.BlockSpec((tm, d), lambda i: (i, 0)), _row(d), _row(d), _row(d)],
        out_specs=pl.BlockSpec((tm, d), lambda i: (i, 0)),
        out_shape=jax.ShapeDtypeStruct((t, d), bf16), compiler_params=_cparams("parallel"),
    )(x, gn, sc, sh)


def norm_bwd(x, gn, sc, sh, dh, dres, name, tm=512):
    t, d = x.shape
    tm = _tile(t, tm, 8)

    def body(x_ref, gn_ref, sc_ref, sh_ref, dh_ref, dres_ref, dx_ref, dgn_ref, dsc_ref, dsh_ref):
        _, vjp = jax.vjp(_norm_fn, x_ref[...], gn_ref[...], sc_ref[...], sh_ref[...])
        dx, dgn, dsc, dsh = vjp(dh_ref[...].astype(f32))
        dx_ref[...] = dx + dres_ref[...]

        @pl.when(pl.program_id(0) == 0)
        def _():
            dgn_ref[...] = jnp.zeros_like(dgn_ref)
            dsc_ref[...] = jnp.zeros_like(dsc_ref)
            dsh_ref[...] = jnp.zeros_like(dsh_ref)

        dgn_ref[...] += dgn
        dsc_ref[...] += dsc
        dsh_ref[...] += dsh

    blk = pl.BlockSpec((tm, d), lambda i: (i, 0))
    vec = jax.ShapeDtypeStruct((1, d), f32)
    return pl.pallas_call(
        body, name=name, grid=(t // tm,),
        in_specs=[blk, _row(d), _row(d), _row(d), blk, blk],
        out_specs=(blk, _row(d), _row(d), _row(d)),
        out_shape=(jax.ShapeDtypeStruct((t, d), f32), vec, vec, vec), compiler_params=_cparams("arbitrary"),
    )(x, gn, sc, sh, dh, dres)


def _loss_fn(x, g, tgt):
    r = lax.rsqrt(jnp.mean(x * x, axis=-1, keepdims=True) + EPS)
    err = jnp.square(x * r * g - tgt)
    return 0.5 * jnp.sum(jnp.mean(err, axis=-1, keepdims=True), axis=0, keepdims=True)


def loss_head(x, g, tgt, name, tm=512):
    t, d = x.shape
    tm = _tile(t, tm, 8)

    def body(x_ref, g_ref, t_ref, loss_ref, dx_ref, dg_ref):
        loss, vjp = jax.vjp(_loss_fn, x_ref[...], g_ref[...], t_ref[...])
        dx, dg, _ = vjp(jnp.ones((1, 1), f32))
        dx_ref[...] = dx

        @pl.when(pl.program_id(0) == 0)
        def _():
            loss_ref[...] = jnp.zeros_like(loss_ref)
            dg_ref[...] = jnp.zeros_like(dg_ref)

        loss_ref[...] += loss
        dg_ref[...] += dg

    blk = pl.BlockSpec((tm, d), lambda i: (i, 0))
    return pl.pallas_call(
        body, name=name, grid=(t // tm,),
        in_specs=[blk, _row(d), blk],
        out_specs=(pl.BlockSpec((1, 1), lambda i: (0, 0)), blk, _row(d)),
        out_shape=(jax.ShapeDtypeStruct((1, 1), f32), jax.ShapeDtypeStruct((t, d), f32),
                   jax.ShapeDtypeStruct((1, d), f32)),
        compiler_params=_cparams("arbitrary"),
    )(x, g, tgt)


def rowdot(a, b, name, tm=512):
    t, d = a.shape
    tm = _tile(t, tm, 8)

    def body(a_ref, b_ref, o_ref):
        @pl.when(pl.program_id(0) == 0)
        def _():
            o_ref[...] = jnp.zeros_like(o_ref)

        o_ref[...] += jnp.sum(a_ref[...] * b_ref[...].astype(f32), axis=0, keepdims=True)

    blk = pl.BlockSpec((tm, d), lambda i: (i, 0))
    return pl.pallas_call(
        body, name=name, grid=(t // tm,), in_specs=[blk, blk], out_specs=_row(d),
        out_shape=jax.ShapeDtypeStruct((1, d), f32), compiler_params=_cparams("arbitrary"),
    )(a, b)


def scale_rows(a, g, name, tm=512):
    t, d = a.shape
    tm = _tile(t, tm, 8)

    def body(a_ref, g_ref, o_ref):
        o_ref[...] = (a_ref[...] * g_ref[...]).astype(bf16)

    blk = pl.BlockSpec((tm, d), lambda i: (i, 0))
    return pl.pallas_call(
        body, name=name, grid=(t // tm,), in_specs=[blk, _row(d)], out_specs=blk,
        out_shape=jax.ShapeDtypeStruct((t, d), bf16), compiler_params=_cparams("parallel"),
    )(a, g)


def _gm_block_fn(z, ws, bs, lng, lnb):
    u = _gelu(z[:, :GM_WIDTH])
    vg = _gelu(z[:, GM_WIDTH:])
    mu = jnp.mean(vg, axis=-1, keepdims=True)
    var = jnp.mean(jnp.square(vg - mu), axis=-1, keepdims=True)
    vn = (vg - mu) * lax.rsqrt(var + EPS) * lng + lnb
    row = lax.broadcasted_iota(jnp.int32, (GM_BLOCK, GM_BLOCK), 0) // CHUNK
    col = lax.broadcasted_iota(jnp.int32, (GM_BLOCK, GM_BLOCK), 1) // CHUNK
    parts = []
    for h in range(GM_HEADS):
        w = jnp.where(row >= col, ws[h], 0.0)
        cols = slice(h * GM_HEAD_DIM, (h + 1) * GM_HEAD_DIM)
        s = jnp.dot(w.astype(bf16), vn[:, cols].astype(bf16), preferred_element_type=f32) + bs[h]
        parts.append(u[:, cols] * s)
    return jnp.concatenate(parts, axis=1)


def _gm_param_specs():
    return [pl.BlockSpec((GM_HEADS, GM_BLOCK, GM_BLOCK), lambda i: (0, 0, 0)),
            pl.BlockSpec((GM_HEADS, GM_BLOCK, 1), lambda i: (0, 0, 0)), _row(GM_WIDTH), _row(GM_WIDTH)]


def gm_mix_fwd(z, ws, bs, lng, lnb, name):
    t = z.shape[0]

    def body(z_ref, ws_ref, bs_ref, lng_ref, lnb_ref, o_ref):
        o_ref[...] = _gm_block_fn(z_ref[...].astype(f32), ws_ref[...], bs_ref[...], lng_ref[...],
                                  lnb_ref[...]).astype(bf16)

    return pl.pallas_call(
        body, name=name, grid=(t // GM_BLOCK,),
        in_specs=[pl.BlockSpec((GM_BLOCK, 2 * GM_WIDTH), lambda i: (i, 0))] + _gm_param_specs(),
        out_specs=pl.BlockSpec((GM_BLOCK, GM_WIDTH), lambda i: (i, 0)),
        out_shape=jax.ShapeDtypeStruct((t, GM_WIDTH), bf16), compiler_params=_cparams("parallel"),
    )(z, ws, bs, lng, lnb)


def gm_mix_bwd(z, ws, bs, lng, lnb, dgated, name, rider=None):
    t = z.shape[0]

    def body(z_ref, ws_ref, bs_ref, lng_ref, lnb_ref, dg_ref, dz_ref, dws_ref, dbs_ref, dlng_ref, dlnb_ref):
        _, vjp = jax.vjp(_gm_block_fn, z_ref[...].astype(f32), ws_ref[...], bs_ref[...], lng_ref[...], lnb_ref[...])
        dz, dws, dbs, dlng, dlnb = vjp(dg_ref[...].astype(f32))
        dz_ref[...] = dz.astype(bf16)

        @pl.when(pl.program_id(0) == 0)
        def _():
            dws_ref[...] = jnp.zeros_like(dws_ref)
            dbs_ref[...] = jnp.zeros_like(dbs_ref)
            dlng_ref[...] = jnp.zeros_like(dlng_ref)
            dlnb_ref[...] = jnp.zeros_like(dlnb_ref)

        dws_ref[...] += dws
        dbs_ref[...] += dbs
        dlng_ref[...] += dlng
        dlnb_ref[...] += dlnb

    zblk = pl.BlockSpec((GM_BLOCK, 2 * GM_WIDTH), lambda i: (i, 0))
    return _call(
        body, name=name, grid=(t // GM_BLOCK,),
        in_specs=[zblk] + _gm_param_specs() + [pl.BlockSpec((GM_BLOCK, GM_WIDTH), lambda i: (i, 0))],
        out_specs=[zblk] + _gm_param_specs(),
        out_shape=[jax.ShapeDtypeStruct((t, 2 * GM_WIDTH), bf16),
                   jax.ShapeDtypeStruct((GM_HEADS, GM_BLOCK, GM_BLOCK), f32),
                   jax.ShapeDtypeStruct((GM_HEADS, GM_BLOCK, 1), f32),
                   jax.ShapeDtypeStruct((1, GM_WIDTH), f32), jax.ShapeDtypeStruct((1, GM_WIDTH), f32)],
        semantics=("arbitrary",), rider=rider,
    )(z, ws, bs, lng, lnb, dgated)


def _hg_block_fn(qp, fz, iv, gp, s0, lb, gn):
    n, ns, d = HG_SUB, HG_TOKENS // HG_SUB, HG_DIM
    f = lb + (1.0 - lb) * jax.nn.sigmoid(fz)
    g = jnp.log(f)
    k = 1.0 - f
    q = qp * jax.nn.sigmoid(qp)
    v = iv.astype(bf16)
    row = lax.broadcasted_iota(jnp.int32, (HG_TOKENS, HG_TOKENS), 0)
    col = lax.broadcasted_iota(jnp.int32, (HG_TOKENS, HG_TOKENS), 1)
    tri = ((col <= row) & (col // n == row // n)).astype(f32)
    cum = jnp.dot(tri, g, precision=lax.Precision.HIGHEST, preferred_element_type=f32)
    cum3 = cum.reshape(ns, n, d)
    tot3 = cum3[:, n - 1:n, :]
    q3 = q.reshape(ns, n, d)
    k3 = k.reshape(ns, n, d)
    j3 = lax.broadcasted_iota(jnp.int32, (ns, n, d), 1)
    first = (row // n) * n
    scores_t = jnp.zeros((HG_TOKENS, HG_TOKENS), f32)
    for i in range(n):
        rel = jnp.where(j3 <= i, cum3[:, i:i + 1, :] - cum3, -1e30)
        pair = jnp.sum(q3[:, i:i + 1, :] * k3 * jnp.exp(rel), axis=2, keepdims=True)
        scores_t = scores_t + jnp.where(col == first + i, pair.reshape(HG_TOKENS, 1), 0.0)
    o = lax.dot_general(scores_t.astype(bf16), v, (((0,), (0,)), ((), ())), preferred_element_type=f32)
    kt_t = (k3 * jnp.exp(tot3 - cum3)).reshape(HG_TOKENS, d).T
    lane_sub = lax.broadcasted_iota(jnp.int32, (d, HG_TOKENS), 1) // n
    k_by_sub = jnp.concatenate([jnp.where(lane_sub == b, kt_t, 0.0) for b in range(ns)], axis=0).astype(bf16)
    update = jnp.dot(k_by_sub, v, preferred_element_type=f32)
    decay = jnp.exp(tot3.reshape(ns, d)).T
    state = s0
    states = []
    for a in range(ns):
        states.append(state.astype(bf16))
        state = decay[:, a:a + 1] * state + update[a * d:(a + 1) * d]
    qt = q * jnp.exp(cum)
    row_sub = lax.broadcasted_iota(jnp.int32, (HG_TOKENS, d), 0) // n
    q_by_sub = jnp.concatenate([jnp.where(row_sub == a, qt, 0.0) for a in range(ns)], axis=1).astype(bf16)
    o = o + jnp.dot(q_by_sub, jnp.concatenate(states, axis=0), preferred_element_type=f32)
    on = o * lax.rsqrt(jnp.mean(o * o, axis=-1, keepdims=True) + EPS) * gn
    return on * (gp * jax.nn.sigmoid(gp)), state


def hg_scan_fwd(proj, lb, gn, name):
    t = proj.shape[0]
    nt = t // HG_TOKENS

    def body(q_ref, f_ref, i_ref, g_ref, lb_ref, gn_ref, y_ref, s_ref, state):
        @pl.when(pl.program_id(1) == 0)
        def _():
            state[...] = jnp.zeros_like(state)

        s_ref[0, 0] = state[...]
        y, s1 = _hg_block_fn(q_ref[...], f_ref[...], i_ref[...], g_ref[...], state[...], lb_ref[...], gn_ref[...])
        y_ref[...] = y.astype(bf16)
        state[...] = s1

    def part(p):
        return pl.BlockSpec((HG_TOKENS, HG_DIM), lambda h, i: (i, p * HG_HEADS + h))

    vec = pl.BlockSpec((1, HG_DIM), lambda h, i: (0, h))
    return pl.pallas_call(
        body, name=name, grid=(HG_HEADS, nt),
        in_specs=[part(0), part(1), part(2), part(3), vec, vec],
        out_specs=(pl.BlockSpec((HG_TOKENS, HG_DIM), lambda h, i: (i, h)),
                   pl.BlockSpec((1, 1, HG_DIM, HG_DIM), lambda h, i: (h, i, 0, 0))),
        out_shape=(jax.ShapeDtypeStruct((t, D), bf16), jax.ShapeDtypeStruct((HG_HEADS, nt, HG_DIM, HG_DIM), f32)),
        scratch_shapes=[pltpu.VMEM((HG_DIM, HG_DIM), f32)],
        compiler_params=_cparams("parallel", "arbitrary"),
    )(proj, proj, proj, proj, lb, gn)


def hg_scan_bwd(proj, lb, gn, states, dy, name, rider=None):
    t = proj.shape[0]
    nt = t // HG_TOKENS

    def body(q_ref, f_ref, i_ref, g_ref, lb_ref, gn_ref, s_ref, dy_ref,
             dq_ref, df_ref, di_ref, dg_ref, dlb_ref, dgn_ref, dstate):
        @pl.when(pl.program_id(1) == 0)
        def _():
            dstate[...] = jnp.zeros_like(dstate)
            dlb_ref[...] = jnp.zeros_like(dlb_ref)
            dgn_ref[...] = jnp.zeros_like(dgn_ref)

        _, vjp = jax.vjp(_hg_block_fn, q_ref[...], f_ref[...], i_ref[...], g_ref[...], s_ref[0, 0],
                         lb_ref[...], gn_ref[...])
        dq, df, di, dg, ds0, dlb, dgn = vjp((dy_ref[...].astype(f32), dstate[...]))
        dq_ref[...] = dq.astype(bf16)
        df_ref[...] = df.astype(bf16)
        di_ref[...] = di.astype(bf16)
        dg_ref[...] = dg.astype(bf16)
        dstate[...] = ds0
        dlb_ref[...] += dlb
        dgn_ref[...] += dgn

    def part(p):
        return pl.BlockSpec((HG_TOKENS, HG_DIM), lambda h, i: (nt - 1 - i, p * HG_HEADS + h))

    vec = pl.BlockSpec((1, HG_DIM), lambda h, i: (0, h))
    tok = pl.BlockSpec((HG_TOKENS, HG_DIM), lambda h, i: (nt - 1 - i, h))
    big = jax.ShapeDtypeStruct((t, D), bf16)
    small = jax.ShapeDtypeStruct((1, D), f32)
    return _call(
        body, name=name, grid=(HG_HEADS, nt),
        in_specs=[part(0), part(1), part(2), part(3), vec, vec,
                  pl.BlockSpec((1, 1, HG_DIM, HG_DIM), lambda h, i: (h, nt - 1 - i, 0, 0)), tok],
        out_specs=[tok, tok, tok, tok, vec, vec],
        out_shape=[big, big, big, big, small, small],
        scratch_shapes=[pltpu.VMEM((HG_DIM, HG_DIM), f32)],
        semantics=("parallel", "arbitrary"), rider=rider,
    )(proj, proj, proj, proj, lb, gn, states, dy)


FFN_COLS = 1408
HALO = 8


def _ffn_specs(tm):
    nb = tm // HALO
    main_g = pl.BlockSpec((tm, FFN_COLS), lambda j, i: (i, j))
    main_v = pl.BlockSpec((tm, FFN_COLS), lambda j, i: (i, j + 2))
    halo_g = pl.BlockSpec((HALO, FFN_COLS), lambda j, i: (jnp.maximum(i * nb - 1, 0), j))
    halo_v = pl.BlockSpec((HALO, FFN_COLS), lambda j, i: (jnp.maximum(i * nb - 1, 0), j + 2))
    w_g = pl.BlockSpec((3, FFN_COLS), lambda j, i: (0, j))
    w_v = pl.BlockSpec((3, FFN_COLS), lambda j, i: (0, j + 2))
    b_g = pl.BlockSpec((1, FFN_COLS), lambda j, i: (0, j))
    b_v = pl.BlockSpec((1, FFN_COLS), lambda j, i: (0, j + 2))
    return [main_g, halo_g, main_v, halo_v, w_g, w_v, b_g, b_v]


def _conv_taps(main_ref, halo_ref, buf, tm):
    first = pl.program_id(1) == 0
    buf[pl.ds(0, HALO), :] = jnp.where(first, 0.0, halo_ref[...].astype(f32))
    buf[pl.ds(HALO, tm), :] = main_ref[...].astype(f32)
    return buf[pl.ds(HALO, tm), :], buf[pl.ds(HALO - 1, tm), :], buf[pl.ds(HALO - 2, tm), :]


def _conv(taps, w_ref, b_ref):
    a0, a1, a2 = taps
    return b_ref[...] + w_ref[0:1, :] * a2 + w_ref[1:2, :] * a1 + w_ref[2:3, :] * a0


def ffn_gate_fwd(a, cw, cb, name, tm=512):
    t = a.shape[0]
    tm = _tile(t, tm, 8)

    def body(ag_ref, hg_ref, av_ref, hv_ref, wg_ref, wv_ref, bg_ref, bv_ref, o_ref, buf_g, buf_v):
        yg = _conv(_conv_taps(ag_ref, hg_ref, buf_g, tm), wg_ref, bg_ref)
        yv = _conv(_conv_taps(av_ref, hv_ref, buf_v, tm), wv_ref, bv_ref)
        o_ref[...] = (_gelu(yg) * yv).astype(bf16)

    return pl.pallas_call(
        body, name=name, grid=(2, t // tm), in_specs=_ffn_specs(tm),
        out_specs=pl.BlockSpec((tm, FFN_COLS), lambda j, i: (i, j)),
        out_shape=jax.ShapeDtypeStruct((t, FFN_HIDDEN), bf16),
        scratch_shapes=[pltpu.VMEM((tm + HALO, FFN_COLS), f32), pltpu.VMEM((tm + HALO, FFN_COLS), f32)],
        compiler_params=_cparams("parallel", "arbitrary"),
    )(a, a, a, a, cw, cw, cb, cb)


def ffn_gate_bwd(a, cw, cb, dhid, name, tm=512, rider=None):
    t = a.shape[0]
    tm = _tile(t, tm, 8)

    def body(ag_ref, hg_ref, av_ref, hv_ref, wg_ref, wv_ref, bg_ref, bv_ref, dh_ref,
             dy_ref, dwg_ref, dwv_ref, dbg_ref, dbv_ref, buf_g, buf_v):
        taps_g = _conv_taps(ag_ref, hg_ref, buf_g, tm)
        taps_v = _conv_taps(av_ref, hv_ref, buf_v, tm)
        yg = _conv(taps_g, wg_ref, bg_ref)
        yv = _conv(taps_v, wv_ref, bv_ref)
        dh = dh_ref[...].astype(f32)
        dyg = dh * yv * _gelu_grad(yg)
        dyv = dh * _gelu(yg)
        dy_ref[0] = dyg.astype(bf16)
        dy_ref[1] = dyv.astype(bf16)

        @pl.when(pl.program_id(1) == 0)
        def _():
            dwg_ref[...] = jnp.zeros_like(dwg_ref)
            dwv_ref[...] = jnp.zeros_like(dwv_ref)
            dbg_ref[...] = jnp.zeros_like(dbg_ref)
            dbv_ref[...] = jnp.zeros_like(dbv_ref)

        for dy, taps, dw_ref, db_ref in ((dyg, taps_g, dwg_ref, dbg_ref), (dyv, taps_v, dwv_ref, dbv_ref)):
            a0, a1, a2 = taps
            dw_ref[0:1, :] += jnp.sum(dy * a2, axis=0, keepdims=True)
            dw_ref[1:2, :] += jnp.sum(dy * a1, axis=0, keepdims=True)
            dw_ref[2:3, :] += jnp.sum(dy * a0, axis=0, keepdims=True)
            db_ref[...] += jnp.sum(dy, axis=0, keepdims=True)

    half_w = pl.BlockSpec((3, FFN_COLS), lambda j, i: (0, j))
    half_b = pl.BlockSpec((1, FFN_COLS), lambda j, i: (0, j))
    return _call(
        body, name=name, grid=(2, t // tm),
        in_specs=_ffn_specs(tm) + [pl.BlockSpec((tm, FFN_COLS), lambda j, i: (i, j))],
        out_specs=[pl.BlockSpec((2, tm, FFN_COLS), lambda j, i: (0, i, j)), half_w, half_w, half_b, half_b],
        out_shape=[jax.ShapeDtypeStruct((2, t, FFN_HIDDEN), bf16),
                   jax.ShapeDtypeStruct((3, FFN_HIDDEN), f32), jax.ShapeDtypeStruct((3, FFN_HIDDEN), f32),
                   jax.ShapeDtypeStruct((1, FFN_HIDDEN), f32), jax.ShapeDtypeStruct((1, FFN_HIDDEN), f32)],
        scratch_shapes=[pltpu.VMEM((tm + HALO, FFN_COLS), f32), pltpu.VMEM((tm + HALO, FFN_COLS), f32)],
        semantics=("parallel", "arbitrary"), rider=rider,
    )(a, a, a, a, cw, cw, cb, cb, dhid)


def conv_transpose(dy, cw, name, tm=512):
    _, t, fh = dy.shape
    tm = _tile(t, tm, 8)
    nb = tm // HALO
    last_halo = t // HALO - 1
    ncol = fh // FFN_COLS

    def body(main_ref, halo_ref, w_ref, o_ref, buf):
        last = pl.program_id(2) == pl.num_programs(2) - 1
        buf[pl.ds(0, tm), :] = main_ref[0].astype(f32)
        buf[pl.ds(tm, HALO), :] = jnp.where(last, 0.0, halo_ref[0].astype(f32))
        o_ref[0] = (w_ref[2:3, :] * buf[pl.ds(0, tm), :] + w_ref[1:2, :] * buf[pl.ds(1, tm), :]
                    + w_ref[0:1, :] * buf[pl.ds(2, tm), :]).astype(bf16)

    return pl.pallas_call(
        body, name=name, grid=(2, ncol, t // tm),
        in_specs=[pl.BlockSpec((1, tm, FFN_COLS), lambda p, j, i: (p, i, j)),
                  pl.BlockSpec((1, HALO, FFN_COLS), lambda p, j, i: (p, jnp.minimum((i + 1) * nb, last_halo), j)),
                  pl.BlockSpec((3, FFN_COLS), lambda p, j, i: (0, p * ncol + j))],
        out_specs=pl.BlockSpec((1, tm, FFN_COLS), lambda p, j, i: (p, i, j)),
        out_shape=jax.ShapeDtypeStruct((2, t, fh), bf16),
        scratch_shapes=[pltpu.VMEM((tm + HALO, FFN_COLS), f32)],
        compiler_params=_cparams("parallel", "parallel", "arbitrary"),
    )(dy, dy, cw)


def ada_mod(c_all, ada_w, ada_b_cols, name):
    cols = ada_w.shape[2]

    def body(c_ref, w_ref, b_ref, o_ref):
        c = c_ref[...]
        cond = (c * jax.nn.sigmoid(c)).astype(bf16)
        o_ref[0] = jnp.dot(cond, w_ref[0].astype(bf16), preferred_element_type=f32) + b_ref[0]

    return pl.pallas_call(
        body, name=name, grid=(DEPTH,),
        in_specs=[pl.BlockSpec((N_DEV, D), lambda i: (0, 0)), pl.BlockSpec((1, D, cols), lambda i: (i, 0, 0)),
                  pl.BlockSpec((1, 1, cols), lambda i: (i, 0, 0))],
        out_specs=pl.BlockSpec((1, N_DEV, cols), lambda i: (i, 0, 0)),
        out_shape=jax.ShapeDtypeStruct((DEPTH, N_DEV, cols), f32), compiler_params=_cparams("parallel"),
    )(c_all, ada_w, ada_b_cols)


def ada_grads(c_all, dmod_cols, dmod_all, name):
    cols = dmod_cols.shape[2]

    def body(c_ref, dm_ref, da_ref, dw_ref, db_ref):
        c = c_ref[...]
        cond = c * jax.nn.sigmoid(c)
        dw_ref[0] = lax.dot_general(cond, dm_ref[0], (((0,), (0,)), ((), ())), precision=lax.Precision.HIGHEST,
                                    preferred_element_type=f32)
        acc = da_ref[0, 0]
        for e in range(1, N_DEV):
            acc = acc + da_ref[e, 0]
        db_ref[0] = acc

    return pl.pallas_call(
        body, name=name, grid=(DEPTH,),
        in_specs=[pl.BlockSpec((N_DEV, D), lambda i: (0, 0)), pl.BlockSpec((1, N_DEV, cols), lambda i: (i, 0, 0)),
                  pl.BlockSpec((N_DEV, 1, 1, 6 * D), lambda i: (0, i, 0, 0))],
        out_specs=(pl.BlockSpec((1, D, cols), lambda i: (i, 0, 0)), pl.BlockSpec((1, 1, 6 * D), lambda i: (i, 0, 0))),
        out_shape=(jax.ShapeDtypeStruct((DEPTH, D, cols), f32), jax.ShapeDtypeStruct((DEPTH, 1, 6 * D), f32)),
        compiler_params=_cparams("parallel"),
    )(c_all, dmod_cols, dmod_all)


def lower_bound_fwd(hg_lb, name):
    n = hg_lb.shape[1]

    def body(l_ref, o_ref):
        o_ref[...] = jax.nn.sigmoid(l_ref[1:2, :] - l_ref[0:1, :])

    return pl.pallas_call(body, name=name, out_shape=jax.ShapeDtypeStruct((1, n), f32))(hg_lb)


def lower_bound_bwd(hg_lb, dlb, name):
    n = hg_lb.shape[1]

    def body(l_ref, d_ref, o_ref):
        p = jax.nn.sigmoid(l_ref[1:2, :] - l_ref[0:1, :])
        g = d_ref[...] * p * (1.0 - p)
        o_ref[0:1, :] = -g
        o_ref[1:2, :] = g

    return pl.pallas_call(body, name=name, out_shape=jax.ShapeDtypeStruct((2, n), f32))(hg_lb, dlb)


def _adamw(w, g, m, v):
    m = ADAM_B1 * m + (1.0 - ADAM_B1) * g
    v = ADAM_B2 * v + (1.0 - ADAM_B2) * jnp.square(g)
    m_hat = m / (1.0 - ADAM_B1 ** ADAM_STEP)
    v_hat = v / (1.0 - ADAM_B2 ** ADAM_STEP)
    delta = -ADAM_LR * (m_hat / (jnp.sqrt(v_hat) + ADAM_EPS) + ADAM_WD * w)
    return delta, m, v


def adam_reduced(parts, w, m, v, name, tr=128):
    r, c = w.shape
    tr = _tile(r, tr, 8)

    def body(p_ref, w_ref, m_ref, v_ref, g_ref, d_ref, mo_ref, vo_ref):
        g = p_ref[0].astype(f32)
        for j in range(1, N_DEV):
            g = g + p_ref[j].astype(f32)
        g_ref[...] = g
        d_ref[...], mo_ref[...], vo_ref[...] = _adamw(w_ref[...], g, m_ref[...], v_ref[...])

    blk = pl.BlockSpec((tr, c), lambda i: (i, 0))
    out = jax.ShapeDtypeStruct((r, c), f32)
    return pl.pallas_call(
        body, name=name, grid=(r // tr,),
        in_specs=[pl.BlockSpec((N_DEV, tr, c), lambda i: (0, i, 0)), blk, blk, blk],
        out_specs=(blk, blk, blk, blk), out_shape=(out, out, out, out), compiler_params=_cparams("parallel"),
    )(parts, w, m, v)


def adam_plain(g, w, m, v, name, tr=256):
    r, c = w.shape
    tr = _tile(r, tr, 8)

    def body(g_ref, w_ref, m_ref, v_ref, d_ref, mo_ref, vo_ref):
        d_ref[...], mo_ref[...], vo_ref[...] = _adamw(w_ref[...], g_ref[...], m_ref[...], v_ref[...])

    blk = pl.BlockSpec((tr, c), lambda i: (i, 0))
    out = jax.ShapeDtypeStruct((r, c), f32)
    return pl.pallas_call(
        body, name=name, grid=(r // tr,), in_specs=[blk, blk, blk, blk], out_specs=(blk, blk, blk),
        out_shape=(out, out, out), compiler_params=_cparams("parallel"),
    )(g, w, m, v)


def sum_parts(parts, name):
    _, r, c = parts.shape

    def body(p_ref, o_ref):
        acc = p_ref[0]
        for j in range(1, N_DEV):
            acc = acc + p_ref[j]
        o_ref[...] = acc

    return pl.pallas_call(body, name=name, out_shape=jax.ShapeDtypeStruct((r, c), f32))(parts)


def _pack(arrs, rows_mult=8):
    flat = jnp.concatenate([a.reshape(-1) for a in arrs])
    rows = -(-flat.shape[0] // LANES)
    rows = -(-rows // rows_mult) * rows_mult
    return jnp.pad(flat, (0, rows * LANES - flat.shape[0])).reshape(rows, LANES)


def _unpack(flat, shapes):
    out, at = [], 0
    for s in shapes:
        n = math.prod(s)
        out.append(flat[at:at + n].reshape(s))
        at += n
    return out


def kernel(x, c, gm_w_in, gm_ln_g, gm_ln_b, gm_w_s, gm_b_s, gm_w_out, hg_w_in, hg_lb, hg_gn_g, hg_w_out, ffn_w_up, ffn_conv_w, ffn_conv_b, ffn_w_down, norm_g, ada_w, ada_b, final_g, loss_target, m_gm_w_in, m_gm_ln_g, m_gm_ln_b, m_gm_w_s, m_gm_b_s, m_gm_w_out, m_hg_w_in, m_hg_lb, m_hg_gn_g, m_hg_w_out, m_ffn_w_up, m_ffn_conv_w, m_ffn_conv_b, m_ffn_w_down, m_norm_g, m_ada_w, m_ada_b, m_final_g, v_gm_w_in, v_gm_ln_g, v_gm_ln_b, v_gm_w_s, v_gm_b_s, v_gm_w_out, v_hg_w_in, v_hg_lb, v_hg_gn_g, v_hg_w_out, v_ffn_w_up, v_ffn_conv_w, v_ffn_conv_b, v_ffn_w_down, v_norm_g, v_ada_w, v_ada_b, v_final_g):
    me = _flat(_mesh_pos())
    xt = x[0]
    t = xt.shape[0]

    small_shapes = [(1, D), (2, HG_DIM), (2, HG_DIM), (DEPTH, 2, HG_DIM), (DEPTH, 3, 2 * FFN_HIDDEN // N_DEV)]
    (small_all,) = all_gather([_pack([c, hg_lb, hg_gn_g, norm_g, ffn_conv_w])], "gather_small")
    small_all = small_all.reshape(N_DEV, -1)
    at = 0
    pieces = []
    for s in small_shapes:
        n = math.prod(s)
        pieces.append(small_all[:, at:at + n].reshape((N_DEV,) + s))
        at += n
    c_all = pieces[0].reshape(N_DEV, D)
    hg_lb_full = jnp.transpose(pieces[1], (1, 0, 2)).reshape(2, D)
    hg_gn_full = jnp.transpose(pieces[2], (1, 0, 2)).reshape(2, D)
    norm_g_full = jnp.transpose(pieces[3], (1, 2, 0, 3)).reshape(DEPTH, 2, D)
    conv_w_full = jnp.transpose(pieces[4], (1, 2, 0, 3)).reshape(DEPTH, 3, 2 * FFN_HIDDEN)

    lb1 = lower_bound_fwd(hg_lb_full, "lower_bound")
    lbs = [jnp.zeros((1, D), f32), lb1]

    ada_b_cols = lax.dynamic_slice(ada_b, (0, me * ADA_COLS), (DEPTH, ADA_COLS)).reshape(DEPTH, 1, ADA_COLS)
    mod_cols = ada_mod(c_all, ada_w, ada_b_cols, "ada_mod")
    (mod_mine,) = all_to_all([jnp.transpose(mod_cols, (1, 0, 2))], "mod_to_examples")
    mod = jnp.transpose(mod_mine, (1, 0, 2)).reshape(DEPTH, 6, 1, D)

    def layer_shards(i):
        j = i // 2
        w_in, w_out = (gm_w_in, gm_w_out) if i % 2 == 0 else (hg_w_in, hg_w_out)
        return [w_in[j].astype(bf16), w_out[j].astype(bf16), ffn_w_up[i].astype(bf16), ffn_w_down[i].astype(bf16)]

    def full_cols(g):
        return jnp.transpose(g, (1, 0, 2)).reshape(g.shape[1], N_DEV * g.shape[2])

    def full_rows(g):
        return g.reshape(N_DEV * g.shape[1], g.shape[2])

    gathered = list(all_gather(layer_shards(0), "gather_weights_0"))

    saved = []
    weights = []
    xcur = xt
    for i in range(DEPTH):
        j = i // 2
        w_in, w_out, w_up, w_down = full_cols(gathered[0]), full_rows(gathered[1]), full_cols(gathered[2]), full_rows(gathered[3])
        weights.append((w_in, w_out, w_up, w_down))
        riders = [Rider(a, True) for a in layer_shards(i + 1)] if i + 1 < DEPTH else [None] * 4
        gathered = [None] * 4

        def ride(outs, slot):
            if riders[slot] is None:
                return outs
            gathered[slot] = outs[-1]
            return outs[:-1]

        sh1, sc1, g1, sh2, sc2, g2 = [mod[i, p] for p in range(6)]
        gn1, gn2 = norm_g_full[i, 0:1], norm_g_full[i, 1:2]
        s = {"x0": xcur}
        h = norm_fwd(xcur, gn1, sc1, sh1, f"norm1_{i}")
        s["h"] = h
        if i % 2 == 0:
            (z,) = ride(mm_nn(h, w_in, bf16, f"gm_in_{i}", rider=riders[0]), 0)
            bs = gm_b_s[j].reshape(GM_HEADS, GM_BLOCK, 1)
            mixed = gm_mix_fwd(z, gm_w_s[j], bs, gm_ln_g[j:j + 1], gm_ln_b[j:j + 1], f"gm_mix_{i}")
            s["z"] = z
        else:
            (proj,) = ride(mm_nn(h, w_in, f32, f"hg_in_{i}", rider=riders[0]), 0)
            mixed, states = hg_scan_fwd(proj, lbs[j], hg_gn_full[j:j + 1], f"hg_scan_{i}")
            s["proj"], s["states"] = proj, states
        s["mixed"] = mixed
        y, x1 = ride(mm_nn_residual(mixed, w_out, xcur, g1, f"mix_out_{i}", rider=riders[1]), 1)
        s["y"], s["x1"] = y, x1
        h2 = norm_fwd(x1, gn2, sc2, sh2, f"norm2_{i}")
        (a,) = ride(mm_nn(h2, w_up, bf16, f"ffn_up_{i}", rider=riders[2]), 2)
        hid = ffn_gate_fwd(a, conv_w_full[i], ffn_conv_b[i:i + 1], f"ffn_gate_{i}")
        fo, x2 = ride(mm_nn_residual(hid, w_down, x1, g2, f"ffn_down_{i}", rider=riders[3]), 3)
        s["h2"], s["a"], s["hid"], s["f"] = h2, a, hid, fo
        saved.append(s)
        xcur = x2

    loss_part, dx, d_final_g = loss_head(xcur, final_g.reshape(1, D), loss_target[0], "loss_head")
    loss = lax.psum(loss_part[0, 0], ("x", "y", "c"))

    def blocked_cols(dw):
        k, n = dw.shape
        return jnp.transpose(dw.reshape(k, N_DEV, n // N_DEV), (1, 0, 2))

    def blocked_rows(dw):
        k, n = dw.shape
        return dw.reshape(N_DEV, k // N_DEV, n)

    received = [[None] * 4 for _ in range(DEPTH)]
    pending = None

    def take(outs, where):
        if where is None:
            return outs
        received[where[0]][where[1]] = outs[-1]
        return outs[:-1]

    dmod = [None] * DEPTH
    d_norm_g = [None] * DEPTH
    d_gm = {k: [None, None] for k in ("ws", "bs", "lng", "lnb")}
    d_hg = {k: [None, None] for k in ("lb", "gn")}
    d_ffn = {k: [None] * DEPTH for k in ("cw", "cb")}
    for i in reversed(range(DEPTH)):
        j = i // 2
        s = saved[i]
        w_in, w_out, w_up, w_down = weights[i]
        sh1, sc1, g1, sh2, sc2, g2 = [mod[i, p] for p in range(6)]
        gn1, gn2 = norm_g_full[i, 0:1], norm_g_full[i, 1:2]
        dg2 = rowdot(dx, s["f"], f"dgate2_{i}")
        df = scale_rows(dx, g2, f"dffn_out_{i}")
        (dw_down,) = mm_tn(s["hid"], df, bf16, f"dw_down_{i}")
        (dhid,) = take(mm_nt(df, w_down, bf16, f"dhid_{i}", rider=Rider(blocked_rows(dw_down), False)), (i, 3))
        outs = ffn_gate_bwd(s["a"], conv_w_full[i], ffn_conv_b[i:i + 1], dhid, f"ffn_gate_bwd_{i}",
                            rider=None if pending is None else Rider(pending[0], False))
        dyc, dwg, dwv, dbg, dbv = take(outs, None if pending is None else pending[1])
        d_ffn["cw"][i] = jnp.concatenate([dwg, dwv], axis=1)
        d_ffn["cb"][i] = jnp.concatenate([dbg, dbv], axis=1)
        da = conv_transpose(dyc, conv_w_full[i], f"conv_t_{i}")
        da = jnp.transpose(da, (1, 0, 2)).reshape(t, 2 * FFN_HIDDEN)
        (dw_up,) = mm_tn(s["h2"], da, bf16, f"dw_up_{i}")
        (dh2,) = mm_nt(da, w_up, bf16, f"dh2_{i}")
        dx1, dgn2, dsc2, dsh2 = norm_bwd(s["x1"], gn2, sc2, sh2, dh2, dx, f"norm2_bwd_{i}")
        dg1 = rowdot(dx1, s["y"], f"dgate1_{i}")
        dy = scale_rows(dx1, g1, f"dmix_out_{i}")
        (dw_out,) = mm_tn(s["mixed"], dy, bf16, f"dw_mix_out_{i}")
        (dmixed,) = take(mm_nt(dy, w_out, bf16, f"dmixed_{i}", rider=Rider(blocked_rows(dw_out), False)), (i, 1))
        up_rider = Rider(blocked_cols(dw_up), False)
        if i % 2 == 0:
            bs = gm_b_s[j].reshape(GM_HEADS, GM_BLOCK, 1)
            dpre, dws, dbs, dlng, dlnb = take(
                gm_mix_bwd(s["z"], gm_w_s[j], bs, gm_ln_g[j:j + 1], gm_ln_b[j:j + 1], dmixed, f"gm_mix_bwd_{i}",
                           rider=up_rider), (i, 2))
            d_gm["ws"][j], d_gm["bs"][j], d_gm["lng"][j], d_gm["lnb"][j] = dws, dbs.reshape(GM_HEADS, GM_BLOCK), dlng, dlnb
        else:
            dq, dfz, di, dgp, dlb, dgn = take(
                hg_scan_bwd(s["proj"], lbs[j], hg_gn_full[j:j + 1], s["states"], dmixed, f"hg_scan_bwd_{i}",
                            rider=up_rider), (i, 2))
            d_hg["lb"][j], d_hg["gn"][j] = dlb, dgn
            dpre = jnp.concatenate([dq, dfz, di, dgp], axis=1)
        (dw_in,) = mm_tn(s["h"], dpre, bf16, f"dw_mix_in_{i}")
        pending = (blocked_cols(dw_in), (i, 0))
        (dh,) = mm_nt(dpre, w_in, bf16, f"dh_mix_{i}")
        dx, dgn1, dsc1, dsh1 = norm_bwd(s["x0"], gn1, sc1, sh1, dh, dx1, f"norm1_bwd_{i}")
        dmod[i] = jnp.concatenate([dsh1, dsc1, dg1, dsh2, dsc2, dg2], axis=1)
        d_norm_g[i] = jnp.concatenate([dgn1, dgn2], axis=0)
    grad_x = dx.reshape(1, t, D)
    (received[0][0],) = all_to_all([pending[0]], "weight_grads_last")

    (dmod_all,) = all_gather([jnp.concatenate(dmod, axis=0)], "gather_dmod")
    dmod_cols = jnp.transpose(lax.dynamic_slice(dmod_all, (0, 0, me * ADA_COLS), (N_DEV, DEPTH, ADA_COLS)), (1, 0, 2))
    g_ada_w, g_ada_b = ada_grads(c_all, dmod_cols, dmod_all.reshape(N_DEV, DEPTH, 1, 6 * D), "ada_grads")
    g_ada_b = g_ada_b.reshape(DEPTH, 6 * D)

    small_partials = [jnp.concatenate(d_gm["lng"], axis=0), jnp.concatenate(d_gm["lnb"], axis=0),
                      jnp.stack(d_gm["ws"]), jnp.stack(d_gm["bs"]), jnp.concatenate(d_ffn["cb"], axis=0),
                      d_final_g, d_hg["lb"][1], jnp.concatenate(d_hg["gn"], axis=0), jnp.stack(d_norm_g),
                      jnp.stack(d_ffn["cw"])]
    partial_shapes = [p.shape for p in small_partials]
    packed = _pack(small_partials, rows_mult=8 * N_DEV)
    rows = packed.shape[0] // N_DEV
    (recv,) = all_to_all([packed.reshape(N_DEV, rows, LANES)], "small_grads_exchange")
    (summed,) = all_gather([sum_parts(recv, "small_grads_sum")], "small_grads_gather")
    g_ln_g, g_ln_b, g_ws, g_bs, g_cb, g_final, g_lb1, g_gn, g_norm, g_cw = _unpack(summed.reshape(-1), partial_shapes)
    g_final = g_final.reshape(D)

    def my_cols(a, n):
        start = (0,) * (a.ndim - 1) + (me * n,)
        return lax.dynamic_slice(a, start, a.shape[:-1] + (n,))

    g_hg_lb = lower_bound_bwd(hg_lb, my_cols(g_lb1, HG_DIM), "lower_bound_bwd")
    g_hg_gn = my_cols(g_gn, HG_DIM)
    g_norm_g = my_cols(g_norm, HG_DIM)
    g_conv_w = my_cols(g_cw, 2 * FFN_HIDDEN // N_DEV)

    def parts_of(slot, layers):
        return jnp.concatenate([received[i][slot] for i in layers], axis=1)

    w_shards = [gm_w_in, gm_w_out, hg_w_in, hg_w_out, ffn_w_up, ffn_w_down]
    big_parts = [parts_of(0, (0, 2)), parts_of(1, (0, 2)), parts_of(0, (1, 3)), parts_of(1, (1, 3)),
                 parts_of(2, range(DEPTH)), parts_of(3, range(DEPTH))]
    big_m = [m_gm_w_in, m_gm_w_out, m_hg_w_in, m_hg_w_out, m_ffn_w_up, m_ffn_w_down]
    big_v = [v_gm_w_in, v_gm_w_out, v_hg_w_in, v_hg_w_out, v_ffn_w_up, v_ffn_w_down]
    big = []
    for idx, (w, m_, v_, parts) in enumerate(zip(w_shards, big_m, big_v, big_parts)):
        two_d = (-1, w.shape[-1])
        outs = adam_reduced(parts, w.reshape(two_d), m_.reshape(two_d), v_.reshape(two_d), f"adam_big_{idx}")
        big.append([o.reshape(w.shape) for o in outs])
    (g_gm_w_in, d_gm_w_in, nm_gm_w_in, nv_gm_w_in), (g_gm_w_out, d_gm_w_out, nm_gm_w_out, nv_gm_w_out), \
        (g_hg_w_in, d_hg_w_in, nm_hg_w_in, nv_hg_w_in), (g_hg_w_out, d_hg_w_out, nm_hg_w_out, nv_hg_w_out), \
        (g_ffn_w_up, d_ffn_w_up, nm_ffn_w_up, nv_ffn_w_up), (g_ffn_w_down, d_ffn_w_down, nm_ffn_w_down, nv_ffn_w_down) = big

    two_d = (-1, ADA_COLS)
    d_ada_w, nm_ada_w, nv_ada_w = [o.reshape(ada_w.shape) for o in adam_plain(
        g_ada_w.reshape(two_d), ada_w.reshape(two_d), m_ada_w.reshape(two_d), v_ada_w.reshape(two_d), "adam_ada_w")]

    small_g = [g_ln_g, g_ln_b, g_ws, g_bs, g_cb, g_ada_b, g_final, g_hg_lb, g_hg_gn, g_norm_g, g_conv_w]
    small_w = [gm_ln_g, gm_ln_b, gm_w_s, gm_b_s, ffn_conv_b, ada_b, final_g, hg_lb, hg_gn_g, norm_g, ffn_conv_w]
    small_m = [m_gm_ln_g, m_gm_ln_b, m_gm_w_s, m_gm_b_s, m_ffn_conv_b, m_ada_b, m_final_g, m_hg_lb, m_hg_gn_g, m_norm_g, m_ffn_conv_w]
    small_v = [v_gm_ln_g, v_gm_ln_b, v_gm_w_s, v_gm_b_s, v_ffn_conv_b, v_ada_b, v_final_g, v_hg_lb, v_hg_gn_g, v_norm_g, v_ffn_conv_w]
    shapes = [w.shape for w in small_w]
    small_g = [g.reshape(s) for g, s in zip(small_g, shapes)]
    outs = adam_plain(_pack(small_g), _pack(small_w), _pack(small_m), _pack(small_v), "adam_small")
    (d_ln_g, d_ln_b, d_ws, d_bs, d_cb, d_ada_b, d_final, d_hg_lb, d_hg_gn, d_norm_g_, d_conv_w), \
        (nm_ln_g, nm_ln_b, nm_ws, nm_bs, nm_cb, nm_ada_b, nm_final, nm_hg_lb, nm_hg_gn, nm_norm_g, nm_conv_w), \
        (nv_ln_g, nv_ln_b, nv_ws, nv_bs, nv_cb, nv_ada_b, nv_final, nv_hg_lb, nv_hg_gn, nv_norm_g, nv_conv_w) = [
            _unpack(o.reshape(-1), shapes) for o in outs]
    g_ln_g, g_ln_b, g_ws, g_bs, g_cb, g_ada_b, g_final, g_hg_lb, g_hg_gn, g_norm_g, g_conv_w = small_g

    grads = (g_gm_w_in, g_ln_g, g_ln_b, g_ws, g_bs, g_gm_w_out, g_hg_w_in, g_hg_lb, g_hg_gn, g_hg_w_out,
             g_ffn_w_up, g_conv_w, g_cb, g_ffn_w_down, g_norm_g, g_ada_w, g_ada_b, g_final)
    deltas = (d_gm_w_in, d_ln_g, d_ln_b, d_ws, d_bs, d_gm_w_out, d_hg_w_in, d_hg_lb, d_hg_gn, d_hg_w_out,
              d_ffn_w_up, d_conv_w, d_cb, d_ffn_w_down, d_norm_g_, d_ada_w, d_ada_b, d_final)
    new_m = (nm_gm_w_in, nm_ln_g, nm_ln_b, nm_ws, nm_bs, nm_gm_w_out, nm_hg_w_in, nm_hg_lb, nm_hg_gn, nm_hg_w_out,
             nm_ffn_w_up, nm_conv_w, nm_cb, nm_ffn_w_down, nm_norm_g, nm_ada_w, nm_ada_b, nm_final)
    new_v = (nv_gm_w_in, nv_ln_g, nv_ln_b, nv_ws, nv_bs, nv_gm_w_out, nv_hg_w_in, nv_hg_lb, nv_hg_gn, nv_hg_w_out,
             nv_ffn_w_up, nv_conv_w, nv_cb, nv_ffn_w_down, nv_norm_g, nv_ada_w, nv_ada_b, nv_final)
    return (loss, grad_x) + grads + deltas + new_m + new_v
```

```python
import functools
import math

import jax
import jax.numpy as jnp
from jax import lax
from jax.experimental import pallas as pl
from jax.experimental.pallas import tpu as pltpu

f32 = jnp.float32
bf16 = jnp.bfloat16
MESH = pl.DeviceIdType.MESH

N_DEV = 8
D = 1024
DEPTH = 4
EPS = 1e-6
GM_WIDTH = 2048
GM_HEADS = 8
GM_HEAD_DIM = 256
GM_BLOCK = 128
CHUNK = 64
HG_HEADS = 8
HG_DIM = 128
FFN_HIDDEN = 2816
ADA_COLS = 6 * D // N_DEV

HG_SUB = 32
HG_PAIR = 8
HG_TOKENS = 128
HG_GROUP = 8

ADAM_LR = 0.001
ADAM_B1 = 0.9
ADAM_B2 = 0.999
ADAM_EPS = 1e-08
ADAM_WD = 0.01
ADAM_STEP = 10

V7X_VMEM_LIMIT = 56 * 1024 * 1024
LANES = 128


def _cparams(*sem):
    return pltpu.CompilerParams(dimension_semantics=sem or None, vmem_limit_bytes=V7X_VMEM_LIMIT)


def _tile(n, target, mult=LANES):
    best = None
    for t in range(mult, min(n, target) + 1, mult):
        if n % t == 0:
            best = t
    return best or n


def _gelu(x):
    return 0.5 * x * (1.0 + lax.erf(x * (1.0 / math.sqrt(2.0))))


def _mesh_pos():
    return lax.axis_index("x"), lax.axis_index("y"), lax.axis_index("c")


def _flat(pos):
    return 4 * pos[0] + 2 * pos[1] + pos[2]


def _peer(pos, k):
    return ((1 - pos[0]) if k & 4 else pos[0], (1 - pos[1]) if k & 2 else pos[1], (1 - pos[2]) if k & 1 else pos[2])


def _exchange_copies(ins, outs, send_sems, recv_sems, local_sems, gather):
    pos = _mesh_pos()
    me = _flat(pos)

    def src(i, dest):
        return ins[i] if gather else ins[i].at[dest]

    local = [pltpu.make_async_copy(src(i, me), outs[i].at[me], local_sems.at[i]) for i in range(len(ins))]
    sends, recvs = [], []
    for k in range(1, N_DEV):
        peer = _peer(pos, k)
        there = _flat(peer)
        for i in range(len(ins)):
            sems = dict(send_sem=send_sems.at[i * 7 + k - 1], recv_sem=recv_sems.at[i * 7 + k - 1],
                        device_id=peer, device_id_type=MESH)
            sends.append(pltpu.make_async_remote_copy(src_ref=src(i, there), dst_ref=outs[i].at[me], **sems))
            recvs.append(pltpu.make_async_remote_copy(src_ref=src(i, there), dst_ref=outs[i].at[there], **sems))
    return local, sends, recvs


def _exchange_start(*refs):
    local, sends, _ = _exchange_copies(*refs)
    for cp in local + sends:
        cp.start()


def _exchange_wait(*refs):
    local, sends, recvs = _exchange_copies(*refs)
    for cp in recvs:
        cp.wait_recv()
    for cp in sends:
        cp.wait_send()
    for cp in local:
        cp.wait()


def _exchange_out_shape(a, gather):
    return jax.ShapeDtypeStruct((N_DEV,) + tuple(a.shape) if gather else tuple(a.shape), a.dtype)


def _exchange_sems(n):
    return [pltpu.SemaphoreType.DMA((7 * n,)), pltpu.SemaphoreType.DMA((7 * n,)), pltpu.SemaphoreType.DMA((n,))]


ANY = pl.BlockSpec(memory_space=pl.ANY)


def _exchange(arrs, gather, name):
    n = len(arrs)

    def body(*refs):
        ins, outs = refs[:n], refs[n:2 * n]
        _exchange_start(ins, outs, *refs[2 * n:], gather)
        _exchange_wait(ins, outs, *refs[2 * n:], gather)

    return pl.pallas_call(
        body, name=name, out_shape=tuple(_exchange_out_shape(a, gather) for a in arrs),
        in_specs=[ANY] * n, out_specs=tuple([ANY] * n), scratch_shapes=_exchange_sems(n),
    )(*arrs)


def all_gather(arrs, name):
    return _exchange(arrs, True, name)


def all_to_all(arrs, name):
    return _exchange(arrs, False, name)


class Rider:
    def __init__(self, arr, gather):
        self.arr, self.gather = arr, gather


def _call(body, *, name, grid, in_specs, out_specs, out_shape, semantics, scratch_shapes=(), rider=None):
    if rider is None:
        return pl.pallas_call(body, name=name, grid=grid, in_specs=in_specs, out_specs=tuple(out_specs),
                              out_shape=tuple(out_shape), scratch_shapes=list(scratch_shapes),
                              compiler_params=_cparams(*semantics))
    n_in, n_out, n_scr = len(in_specs), len(out_specs), len(scratch_shapes)
    gather = rider.gather

    def hosted(*refs):
        ins, r_in = refs[:n_in], refs[n_in]
        outs, r_out = refs[n_in + 1:n_in + 1 + n_out], refs[n_in + 1 + n_out]
        scratch = refs[n_in + 2 + n_out:n_in + 2 + n_out + n_scr]
        sems = refs[n_in + 2 + n_out + n_scr:]
        first = functools.reduce(jnp.logical_and, [pl.program_id(a) == 0 for a in range(len(grid))])
        last = functools.reduce(jnp.logical_and, [pl.program_id(a) == grid[a] - 1 for a in range(len(grid))])

        @pl.when(first)
        def _():
            _exchange_start([r_in], [r_out], *sems, gather)

        body(*ins, *outs, *scratch)

        @pl.when(last)
        def _():
            _exchange_wait([r_in], [r_out], *sems, gather)

    call = pl.pallas_call(
        hosted, name=name, grid=grid, in_specs=list(in_specs) + [ANY], out_specs=tuple(out_specs) + (ANY,),
        out_shape=tuple(out_shape) + (_exchange_out_shape(rider.arr, gather),),
        scratch_shapes=list(scratch_shapes) + _exchange_sems(1),
        compiler_params=_cparams(*(("arbitrary",) * len(grid))))
    return lambda *args: call(*args, rider.arr)


def mm_nn(a, b, out_dtype, name, tm=512, tn=512, rider=None):
    m, k = a.shape
    n = b.shape[1]
    tm, tn = _tile(m, tm, 8), _tile(n, tn)

    def body(a_ref, b_ref, o_ref):
        o_ref[...] = jnp.dot(a_ref[...], b_ref[...], preferred_element_type=f32).astype(o_ref.dtype)

    return _call(
        body, name=name, grid=(m // tm, n // tn),
        in_specs=[pl.BlockSpec((tm, k), lambda i, j: (i, 0)), pl.BlockSpec((k, tn), lambda i, j: (0, j))],
        out_specs=[pl.BlockSpec((tm, tn), lambda i, j: (i, j))],
        out_shape=[jax.ShapeDtypeStruct((m, n), out_dtype)], semantics=("parallel", "parallel"), rider=rider,
    )(a, b)


def mm_nn_residual(a, b, x, gate, name, tm=512, tn=512, rider=None):
    m, k = a.shape
    n = b.shape[1]
    tm, tn = _tile(m, tm, 8), _tile(n, tn)

    def body(a_ref, b_ref, x_ref, g_ref, y_ref, o_ref):
        y = jnp.dot(a_ref[...], b_ref[...], preferred_element_type=f32)
        y_ref[...] = y.astype(bf16)
        o_ref[...] = x_ref[...] + g_ref[...] * y

    return _call(
        body, name=name, grid=(m // tm, n // tn),
        in_specs=[pl.BlockSpec((tm, k), lambda i, j: (i, 0)), pl.BlockSpec((k, tn), lambda i, j: (0, j)),
                  pl.BlockSpec((tm, tn), lambda i, j: (i, j)), pl.BlockSpec((1, tn), lambda i, j: (0, j))],
        out_specs=[pl.BlockSpec((tm, tn), lambda i, j: (i, j)), pl.BlockSpec((tm, tn), lambda i, j: (i, j))],
        out_shape=[jax.ShapeDtypeStruct((m, n), bf16), jax.ShapeDtypeStruct((m, n), f32)],
        semantics=("parallel", "parallel"), rider=rider,
    )(a, b, x, gate)


def mm_nt(a, b, out_dtype, name, tm=512, tn=512, rider=None):
    m, k = a.shape
    n = b.shape[0]
    tm, tn = _tile(m, tm, 8), _tile(n, tn)

    def body(a_ref, b_ref, o_ref):
        o_ref[...] = lax.dot_general(a_ref[...], b_ref[...], (((1,), (1,)), ((), ())),
                                     preferred_element_type=f32).astype(o_ref.dtype)

    return _call(
        body, name=name, grid=(m // tm, n // tn),
        in_specs=[pl.BlockSpec((tm, k), lambda i, j: (i, 0)), pl.BlockSpec((tn, k), lambda i, j: (j, 0))],
        out_specs=[pl.BlockSpec((tm, tn), lambda i, j: (i, j))],
        out_shape=[jax.ShapeDtypeStruct((m, n), out_dtype)], semantics=("parallel", "parallel"), rider=rider,
    )(a, b)


def mm_tn(a, b, out_dtype, name, tm=512, tn=512, rider=None):
    t, m = a.shape
    n = b.shape[1]
    tm, tn = _tile(m, tm), _tile(n, tn)

    def body(a_ref, b_ref, o_ref):
        o_ref[...] = lax.dot_general(a_ref[...], b_ref[...], (((0,), (0,)), ((), ())),
                                     preferred_element_type=f32).astype(o_ref.dtype)

    return _call(
        body, name=name, grid=(m // tm, n // tn),
        in_specs=[pl.BlockSpec((t, tm), lambda i, j: (0, i)), pl.BlockSpec((t, tn), lambda i, j: (0, j))],
        out_specs=[pl.BlockSpec((tm, tn), lambda i, j: (i, j))],
        out_shape=[jax.ShapeDtypeStruct((m, n), out_dtype)], semantics=("parallel", "parallel"), rider=rider,
    )(a, b)


def _norm_fn(x, gn, sc, sh):
    r = lax.rsqrt(jnp.mean(x * x, axis=-1, keepdims=True) + EPS)
    return (x * r * gn) * (1.0 + sc) + sh


def _row(d):
    return pl.BlockSpec((1, d), lambda i: (0, 0))


def norm_fwd(x, gn, sc, sh, name, tm=512):
    t, d = x.shape
    tm = _tile(t, tm, 8)

    def body(x_ref, gn_ref, sc_ref, sh_ref, h_ref):
        h_ref[...] = _norm_fn(x_ref[...], gn_ref[...], sc_ref[...], sh_ref[...]).astype(bf16)

    return pl.pallas_call(
        body, name=name, grid=(t // tm,),
        in_specs=[pl.BlockSpec((tm, d), lambda i: (i, 0)), _row(d), _row(d), _row(d)],
        out_specs=pl.BlockSpec((tm, d), lambda i: (i, 0)),
        out_shape=jax.ShapeDtypeStruct((t, d), bf16), compiler_params=_cparams("parallel"),
    )(x, gn, sc, sh)


def _gate_bwd(dx, y_ref, g_ref, dgate_ref, dy_ref):
    dgate_ref[...] += jnp.sum(dx * y_ref[...].astype(f32), axis=0, keepdims=True)
    dy_ref[...] = (dx * g_ref[...]).astype(bf16)


def norm_bwd(x, gn, sc, sh, dh, dres, name, gate=None, tm=512):
    t, d = x.shape
    tm = _tile(t, tm, 8)

    def body(x_ref, gn_ref, sc_ref, sh_ref, dh_ref, dres_ref, *rest):
        if gate is not None:
            y_ref, g_ref, dx_ref, dgn_ref, dsc_ref, dsh_ref, dgate_ref, dy_ref = rest
        else:
            dx_ref, dgn_ref, dsc_ref, dsh_ref = rest

        @pl.when(pl.program_id(0) == 0)
        def _():
            dgn_ref[...] = jnp.zeros_like(dgn_ref)
            dsc_ref[...] = jnp.zeros_like(dsc_ref)
            dsh_ref[...] = jnp.zeros_like(dsh_ref)
            if gate is not None:
                dgate_ref[...] = jnp.zeros_like(dgate_ref)

        _, vjp = jax.vjp(_norm_fn, x_ref[...], gn_ref[...], sc_ref[...], sh_ref[...])
        dx, dgn, dsc, dsh = vjp(dh_ref[...].astype(f32))
        dx = dx + dres_ref[...]
        dx_ref[...] = dx
        dgn_ref[...] += dgn
        dsc_ref[...] += dsc
        dsh_ref[...] += dsh
        if gate is not None:
            _gate_bwd(dx, y_ref, g_ref, dgate_ref, dy_ref)

    blk = pl.BlockSpec((tm, d), lambda i: (i, 0))
    vec = jax.ShapeDtypeStruct((1, d), f32)
    in_specs = [blk, _row(d), _row(d), _row(d), blk, blk]
    out_specs = [blk, _row(d), _row(d), _row(d)]
    out_shape = [jax.ShapeDtypeStruct((t, d), f32), vec, vec, vec]
    args = [x, gn, sc, sh, dh, dres]
    if gate is not None:
        in_specs += [blk, _row(d)]
        out_specs += [_row(d), blk]
        out_shape += [vec, jax.ShapeDtypeStruct((t, d), bf16)]
        args += list(gate)
    return pl.pallas_call(
        body, name=name, grid=(t // tm,), in_specs=in_specs, out_specs=tuple(out_specs),
        out_shape=tuple(out_shape), compiler_params=_cparams("arbitrary"),
    )(*args)


def _loss_fn(x, g, tgt):
    r = lax.rsqrt(jnp.mean(x * x, axis=-1, keepdims=True) + EPS)
    err = jnp.square(x * r * g - tgt)
    return 0.5 * jnp.sum(jnp.mean(err, axis=-1, keepdims=True), axis=0, keepdims=True)


def loss_head(x, g, tgt, y, gate, name, tm=512):
    t, d = x.shape
    tm = _tile(t, tm, 8)

    def body(x_ref, g_ref, t_ref, y_ref, gate_ref, loss_ref, dx_ref, dg_ref, dgate_ref, dy_ref):
        @pl.when(pl.program_id(0) == 0)
        def _():
            loss_ref[...] = jnp.zeros_like(loss_ref)
            dg_ref[...] = jnp.zeros_like(dg_ref)
            dgate_ref[...] = jnp.zeros_like(dgate_ref)

        loss, vjp = jax.vjp(_loss_fn, x_ref[...], g_ref[...], t_ref[...])
        dx, dg, _ = vjp(jnp.ones((1, 1), f32))
        dx_ref[...] = dx
        loss_ref[...] += loss
        dg_ref[...] += dg
        _gate_bwd(dx, y_ref, gate_ref, dgate_ref, dy_ref)

    blk = pl.BlockSpec((tm, d), lambda i: (i, 0))
    vec = jax.ShapeDtypeStruct((1, d), f32)
    return pl.pallas_call(
        body, name=name, grid=(t // tm,),
        in_specs=[blk, _row(d), blk, blk, _row(d)],
        out_specs=(pl.BlockSpec((1, 1), lambda i: (0, 0)), blk, _row(d), _row(d), blk),
        out_shape=(jax.ShapeDtypeStruct((1, 1), f32), jax.ShapeDtypeStruct((t, d), f32), vec, vec,
                   jax.ShapeDtypeStruct((t, d), bf16)),
        compiler_params=_cparams("arbitrary"),
    )(x, g, tgt, y, gate)


def _gm_block_fn(z, ws, bs, lng, lnb):
    u = _gelu(z[:, :GM_WIDTH])
    vg = _gelu(z[:, GM_WIDTH:])
    mu = jnp.mean(vg, axis=-1, keepdims=True)
    var = jnp.mean(jnp.square(vg - mu), axis=-1, keepdims=True)
    vn = (vg - mu) * lax.rsqrt(var + EPS) * lng + lnb
    row = lax.broadcasted_iota(jnp.int32, (GM_BLOCK, GM_BLOCK), 0) // CHUNK
    col = lax.broadcasted_iota(jnp.int32, (GM_BLOCK, GM_BLOCK), 1) // CHUNK
    parts = []
    for h in range(GM_HEADS):
        w = jnp.where(row >= col, ws[h], 0.0)
        cols = slice(h * GM_HEAD_DIM, (h + 1) * GM_HEAD_DIM)
        s = jnp.dot(w.astype(bf16), vn[:, cols].astype(bf16), preferred_element_type=f32) + bs[h]
        parts.append(u[:, cols] * s)
    return jnp.concatenate(parts, axis=1)


def _gm_param_specs():
    return [pl.BlockSpec((GM_HEADS, GM_BLOCK, GM_BLOCK), lambda i: (0, 0, 0)),
            pl.BlockSpec((GM_HEADS, GM_BLOCK, 1), lambda i: (0, 0, 0)), _row(GM_WIDTH), _row(GM_WIDTH)]


def gm_mix_fwd(z, ws, bs, lng, lnb, name):
    t = z.shape[0]

    def body(z_ref, ws_ref, bs_ref, lng_ref, lnb_ref, o_ref):
        o_ref[...] = _gm_block_fn(z_ref[...].astype(f32), ws_ref[...], bs_ref[...], lng_ref[...],
                                  lnb_ref[...]).astype(bf16)

    return pl.pallas_call(
        body, name=name, grid=(t // GM_BLOCK,),
        in_specs=[pl.BlockSpec((GM_BLOCK, 2 * GM_WIDTH), lambda i: (i, 0))] + _gm_param_specs(),
        out_specs=pl.BlockSpec((GM_BLOCK, GM_WIDTH), lambda i: (i, 0)),
        out_shape=jax.ShapeDtypeStruct((t, GM_WIDTH), bf16), compiler_params=_cparams("parallel"),
    )(z, ws, bs, lng, lnb)


def gm_mix_bwd(z, ws, bs, lng, lnb, dgated, name, rider=None):
    t = z.shape[0]

    def body(z_ref, ws_ref, bs_ref, lng_ref, lnb_ref, dg_ref, dz_ref, dws_ref, dbs_ref, dlng_ref, dlnb_ref):
        _, vjp = jax.vjp(_gm_block_fn, z_ref[...].astype(f32), ws_ref[...], bs_ref[...], lng_ref[...], lnb_ref[...])
        dz, dws, dbs, dlng, dlnb = vjp(dg_ref[...].astype(f32))
        dz_ref[...] = dz.astype(bf16)

        @pl.when(pl.program_id(0) == 0)
        def _():
            dws_ref[...] = jnp.zeros_like(dws_ref)
            dbs_ref[...] = jnp.zeros_like(dbs_ref)
            dlng_ref[...] = jnp.zeros_like(dlng_ref)
            dlnb_ref[...] = jnp.zeros_like(dlnb_ref)

        dws_ref[...] += dws
        dbs_ref[...] += dbs
        dlng_ref[...] += dlng
        dlnb_ref[...] += dlnb

    zblk = pl.BlockSpec((GM_BLOCK, 2 * GM_WIDTH), lambda i: (i, 0))
    return _call(
        body, name=name, grid=(t // GM_BLOCK,),
        in_specs=[zblk] + _gm_param_specs() + [pl.BlockSpec((GM_BLOCK, GM_WIDTH), lambda i: (i, 0))],
        out_specs=[zblk] + _gm_param_specs(),
        out_shape=[jax.ShapeDtypeStruct((t, 2 * GM_WIDTH), bf16),
                   jax.ShapeDtypeStruct((GM_HEADS, GM_BLOCK, GM_BLOCK), f32),
                   jax.ShapeDtypeStruct((GM_HEADS, GM_BLOCK, 1), f32),
                   jax.ShapeDtypeStruct((1, GM_WIDTH), f32), jax.ShapeDtypeStruct((1, GM_WIDTH), f32)],
        semantics=("arbitrary",), rider=rider,
    )(z, ws, bs, lng, lnb, dgated)


def _hg_block_fn(qp, fz, iv, gp, s0, lb, gn):
    n, ns, d = HG_SUB, HG_TOKENS // HG_SUB, HG_DIM
    p, nb, per_sub = HG_PAIR, HG_TOKENS // HG_PAIR, HG_SUB // HG_PAIR
    f = lb + (1.0 - lb) * jax.nn.sigmoid(fz)
    g = jnp.log(f)
    k = 1.0 - f
    q = qp * jax.nn.sigmoid(qp)
    v = iv.astype(bf16)
    row = lax.broadcasted_iota(jnp.int32, (HG_TOKENS, HG_TOKENS), 0)
    col = lax.broadcasted_iota(jnp.int32, (HG_TOKENS, HG_TOKENS), 1)
    same_sub = col // n == row // n
    tri = ((col <= row) & same_sub).astype(f32)
    cum = jnp.dot(tri, g, precision=lax.Precision.HIGHEST, preferred_element_type=f32)
    cum_b, q_b, k_b = cum.reshape(nb, p, d), q.reshape(nb, p, d), k.reshape(nb, p, d)
    j_b = lax.broadcasted_iota(jnp.int32, (nb, p, d), 1)
    first = (row // p) * p
    scores_t = jnp.zeros((HG_TOKENS, HG_TOKENS), f32)
    for i in range(p):
        rel = jnp.where(j_b <= i, cum_b[:, i:i + 1, :] - cum_b, -1e30)
        pair = jnp.sum(q_b[:, i:i + 1, :] * k_b * jnp.exp(rel), axis=2, keepdims=True)
        scores_t = scores_t + jnp.where(col == first + i, pair.reshape(HG_TOKENS, 1), 0.0)
    o = lax.dot_general(scores_t.astype(bf16), v, (((0,), (0,)), ((), ())), preferred_element_type=f32)
    last = cum_b[:, p - 1:p, :]
    before = jnp.concatenate([jnp.zeros((1, 1, d), f32), last[:-1]], axis=0)
    before = jnp.broadcast_to(before, (nb, p, d)).reshape(HG_TOKENS, d)
    block = (lax.broadcasted_iota(jnp.int32, (HG_TOKENS, d), 0) // p) % per_sub
    q_late = q * jnp.exp(jnp.where(block > 0, cum - before, -1e30))
    last_s = last.reshape(ns, per_sub, d)
    q_parts, k_parts = [], []
    for m in range(1, per_sub):
        split = jnp.broadcast_to(last_s[:, m - 1:m, :], (ns, n, d)).reshape(HG_TOKENS, d)
        k_parts.append(k * jnp.exp(jnp.where(block < m, split - cum, -1e30)))
        q_parts.append(jnp.where(block == m, q_late, 0.0))
    scores = lax.dot_general(jnp.concatenate(q_parts, axis=1).astype(bf16), jnp.concatenate(k_parts, axis=1).astype(bf16),
                             (((1,), (1,)), ((), ())), preferred_element_type=f32)
    o = o + jnp.dot(jnp.where(same_sub, scores, 0.0).astype(bf16), v, preferred_element_type=f32)
    cum_s = cum.reshape(ns, n, d)
    tot = cum_s[:, n - 1:n, :]
    kt_t = (k.reshape(ns, n, d) * jnp.exp(tot - cum_s)).reshape(HG_TOKENS, d).T
    lane_sub = lax.broadcasted_iota(jnp.int32, (d, HG_TOKENS), 1) // n
    k_by_sub = jnp.concatenate([jnp.where(lane_sub == b, kt_t, 0.0) for b in range(ns)], axis=0).astype(bf16)
    update = jnp.dot(k_by_sub, v, preferred_element_type=f32)
    decay = jnp.exp(tot.reshape(ns, d)).T
    state = s0
    states = []
    for a in range(ns):
        states.append(state.astype(bf16))
        state = decay[:, a:a + 1] * state + update[a * d:(a + 1) * d]
    qt = q * jnp.exp(cum)
    row_sub = lax.broadcasted_iota(jnp.int32, (HG_TOKENS, d), 0) // n
    q_by_sub = jnp.concatenate([jnp.where(row_sub == a, qt, 0.0) for a in range(ns)], axis=1).astype(bf16)
    o = o + jnp.dot(q_by_sub, jnp.concatenate(states, axis=0), preferred_element_type=f32)
    on = o * lax.rsqrt(jnp.mean(o * o, axis=-1, keepdims=True) + EPS) * gn
    return on * (gp * jax.nn.sigmoid(gp)), state


def _head_cols(ref_or_val, hh):
    return ref_or_val[:, hh * HG_DIM:(hh + 1) * HG_DIM]


def hg_scan_fwd(proj, lb, gn, name):
    t = proj.shape[0]
    nt = t // HG_TOKENS
    width = HG_GROUP * HG_DIM

    def body(q_ref, f_ref, i_ref, g_ref, lb_ref, gn_ref, y_ref, s_ref, state):
        @pl.when(pl.program_id(1) == 0)
        def _():
            state[...] = jnp.zeros_like(state)

        for hh in range(HG_GROUP):
            s_ref[hh, 0] = state[hh]
            y, s1 = _hg_block_fn(_head_cols(q_ref, hh), _head_cols(f_ref, hh), _head_cols(i_ref, hh),
                                 _head_cols(g_ref, hh), state[hh], _head_cols(lb_ref, hh), _head_cols(gn_ref, hh))
            y_ref[:, hh * HG_DIM:(hh + 1) * HG_DIM] = y.astype(bf16)
            state[hh] = s1

    def part(p):
        return pl.BlockSpec((HG_TOKENS, width), lambda h, i: (i, p * (HG_HEADS // HG_GROUP) + h))

    vec = pl.BlockSpec((1, width), lambda h, i: (0, h))
    return pl.pallas_call(
        body, name=name, grid=(HG_HEADS // HG_GROUP, nt),
        in_specs=[part(0), part(1), part(2), part(3), vec, vec],
        out_specs=(pl.BlockSpec((HG_TOKENS, width), lambda h, i: (i, h)),
                   pl.BlockSpec((HG_GROUP, 1, HG_DIM, HG_DIM), lambda h, i: (h, i, 0, 0))),
        out_shape=(jax.ShapeDtypeStruct((t, D), bf16), jax.ShapeDtypeStruct((HG_HEADS, nt, HG_DIM, HG_DIM), f32)),
        scratch_shapes=[pltpu.VMEM((HG_GROUP, HG_DIM, HG_DIM), f32)],
        compiler_params=_cparams("parallel", "arbitrary"),
    )(proj, proj, proj, proj, lb, gn)


def hg_scan_bwd(proj, lb, gn, states, dy, name, rider=None):
    t = proj.shape[0]
    nt = t // HG_TOKENS
    width = HG_GROUP * HG_DIM

    def body(q_ref, f_ref, i_ref, g_ref, lb_ref, gn_ref, s_ref, dy_ref,
             dq_ref, df_ref, di_ref, dg_ref, dlb_ref, dgn_ref, dstate):
        @pl.when(pl.program_id(1) == 0)
        def _():
            dstate[...] = jnp.zeros_like(dstate)
            dlb_ref[...] = jnp.zeros_like(dlb_ref)
            dgn_ref[...] = jnp.zeros_like(dgn_ref)

        for hh in range(HG_GROUP):
            cols = slice(hh * HG_DIM, (hh + 1) * HG_DIM)
            _, vjp = jax.vjp(_hg_block_fn, q_ref[:, cols], f_ref[:, cols], i_ref[:, cols], g_ref[:, cols], s_ref[hh, 0],
                             lb_ref[:, cols], gn_ref[:, cols])
            dq, df, di, dg, ds0, dlb, dgn = vjp((dy_ref[:, cols].astype(f32), dstate[hh]))
            dq_ref[:, cols] = dq.astype(bf16)
            df_ref[:, cols] = df.astype(bf16)
            di_ref[:, cols] = di.astype(bf16)
            dg_ref[:, cols] = dg.astype(bf16)
            dstate[hh] = ds0
            dlb_ref[:, cols] += dlb
            dgn_ref[:, cols] += dgn

    def part(p):
        return pl.BlockSpec((HG_TOKENS, width), lambda h, i: (nt - 1 - i, p * (HG_HEADS // HG_GROUP) + h))

    vec = pl.BlockSpec((1, width), lambda h, i: (0, h))
    tok = pl.BlockSpec((HG_TOKENS, width), lambda h, i: (nt - 1 - i, h))
    big = jax.ShapeDtypeStruct((t, D), bf16)
    small = jax.ShapeDtypeStruct((1, D), f32)
    return _call(
        body, name=name, grid=(HG_HEADS // HG_GROUP, nt),
        in_specs=[part(0), part(1), part(2), part(3), vec, vec,
                  pl.BlockSpec((HG_GROUP, 1, HG_DIM, HG_DIM), lambda h, i: (h, nt - 1 - i, 0, 0)), tok],
        out_specs=[tok, tok, tok, tok, vec, vec],
        out_shape=[big, big, big, big, small, small],
        scratch_shapes=[pltpu.VMEM((HG_GROUP, HG_DIM, HG_DIM), f32)],
        semantics=("parallel", "arbitrary"), rider=rider,
    )(proj, proj, proj, proj, lb, gn, states, dy)


FFN_COLS = 1408
HALO = 8
STRIP = 16


def _ffn_specs(tm):
    nb = tm // HALO
    main_g = pl.BlockSpec((tm, FFN_COLS), lambda j, i: (i, j))
    main_v = pl.BlockSpec((tm, FFN_COLS), lambda j, i: (i, j + 2))
    halo_g = pl.BlockSpec((HALO, FFN_COLS), lambda j, i: (jnp.maximum(i * nb - 1, 0), j))
    halo_v = pl.BlockSpec((HALO, FFN_COLS), lambda j, i: (jnp.maximum(i * nb - 1, 0), j + 2))
    w_g = pl.BlockSpec((3, FFN_COLS), lambda j, i: (0, j))
    w_v = pl.BlockSpec((3, FFN_COLS), lambda j, i: (0, j + 2))
    b_g = pl.BlockSpec((1, FFN_COLS), lambda j, i: (0, j))
    b_v = pl.BlockSpec((1, FFN_COLS), lambda j, i: (0, j + 2))
    return [main_g, halo_g, main_v, halo_v, w_g, w_v, b_g, b_v]


def _strip_rows(r):
    return pl.ds(r * STRIP, STRIP) if isinstance(r, int) else pl.ds(pl.multiple_of(r * STRIP, STRIP), STRIP)


def _for_strips(nstrip, strip, reverse=False):
    if reverse:
        strip(nstrip - 1, True)
        lax.fori_loop(0, nstrip - 1, lambda k, c: (strip(nstrip - 2 - k, False), c)[1], 0)
    else:
        strip(0, True)
        lax.fori_loop(1, nstrip, lambda r, c: (strip(r, False), c)[1], 0)


def _conv_strip(main_ref, halo_ref, w_ref, b_ref, r, edge, cols, rowi):
    cur = main_ref[_strip_rows(r), cols].astype(f32)
    if edge:
        h = jnp.where(pl.program_id(1) == 0, 0.0, halo_ref[:, cols].astype(f32))
        prev = jnp.concatenate([jnp.zeros_like(h), h], axis=0)
    else:
        prev = main_ref[_strip_rows(r - 1), cols].astype(f32)
    a1 = jnp.where(rowi < 1, pltpu.roll(prev, 1, axis=0), pltpu.roll(cur, 1, axis=0))
    a2 = jnp.where(rowi < 2, pltpu.roll(prev, 2, axis=0), pltpu.roll(cur, 2, axis=0))
    y = b_ref[:, cols] + w_ref[0:1, cols] * a2 + w_ref[1:2, cols] * a1 + w_ref[2:3, cols] * cur
    return y, (cur, a1, a2)


def ffn_gate_fwd(a, cw, cb, name, tm=512):
    t = a.shape[0]
    tm = _tile(t, tm, STRIP)

    def body(ag_ref, hg_ref, av_ref, hv_ref, wg_ref, wv_ref, bg_ref, bv_ref, o_ref):
        rowi = lax.broadcasted_iota(jnp.int32, (STRIP, LANES), 0)

        def strip(r, edge):
            for c in range(FFN_COLS // LANES):
                cols = pl.ds(c * LANES, LANES)
                yg, _ = _conv_strip(ag_ref, hg_ref, wg_ref, bg_ref, r, edge, cols, rowi)
                yv, _ = _conv_strip(av_ref, hv_ref, wv_ref, bv_ref, r, edge, cols, rowi)
                o_ref[_strip_rows(r), cols] = (_gelu(yg) * yv).astype(bf16)

        _for_strips(tm // STRIP, strip)

    return pl.pallas_call(
        body, name=name, grid=(2, t // tm), in_specs=_ffn_specs(tm),
        out_specs=pl.BlockSpec((tm, FFN_COLS), lambda j, i: (i, j)),
        out_shape=jax.ShapeDtypeStruct((t, FFN_HIDDEN), bf16),
        compiler_params=_cparams("parallel", "arbitrary"),
    )(a, a, a, a, cw, cw, cb, cb)


def ffn_gate_bwd(a, cw, cb, dhid, name, tm=512, rider=None):
    t = a.shape[0]
    tm = _tile(t, tm, STRIP)

    def body(ag_ref, hg_ref, av_ref, hv_ref, wg_ref, wv_ref, bg_ref, bv_ref, dh_ref,
             dy_ref, dwg_ref, dwv_ref, dbg_ref, dbv_ref, acc):
        rowi = lax.broadcasted_iota(jnp.int32, (STRIP, LANES), 0)

        @pl.when(pl.program_id(1) == 0)
        def _():
            acc[...] = jnp.zeros_like(acc)

        def strip(r, edge):
            rows = _strip_rows(r)
            for c in range(FFN_COLS // LANES):
                cols = pl.ds(c * LANES, LANES)
                yg, taps_g = _conv_strip(ag_ref, hg_ref, wg_ref, bg_ref, r, edge, cols, rowi)
                yv, taps_v = _conv_strip(av_ref, hv_ref, wv_ref, bv_ref, r, edge, cols, rowi)
                dh = dh_ref[rows, cols].astype(f32)
                cdf = 0.5 * (1.0 + lax.erf(yg * (1.0 / math.sqrt(2.0))))
                pdf = jnp.exp(-0.5 * yg * yg) * (1.0 / math.sqrt(2.0 * math.pi))
                dyg = dh * yv * (cdf + yg * pdf)
                dyv = dh * (yg * cdf)
                dy_ref[0, rows, cols] = dyg.astype(bf16)
                dy_ref[1, rows, cols] = dyv.astype(bf16)
                for p, (dy, (a0, a1, a2)) in enumerate(((dyg, taps_g), (dyv, taps_v))):
                    acc[4 * p + 0, :, cols] += dy * a2
                    acc[4 * p + 1, :, cols] += dy * a1
                    acc[4 * p + 2, :, cols] += dy * a0
                    acc[4 * p + 3, :, cols] += dy

        _for_strips(tm // STRIP, strip)

        @pl.when(pl.program_id(1) == pl.num_programs(1) - 1)
        def _():
            for p, (dw_ref, db_ref) in enumerate(((dwg_ref, dbg_ref), (dwv_ref, dbv_ref))):
                for tap in range(3):
                    dw_ref[tap:tap + 1, :] = jnp.sum(acc[4 * p + tap], axis=0, keepdims=True)
                db_ref[...] = jnp.sum(acc[4 * p + 3], axis=0, keepdims=True)

    half_w = pl.BlockSpec((3, FFN_COLS), lambda j, i: (0, j))
    half_b = pl.BlockSpec((1, FFN_COLS), lambda j, i: (0, j))
    return _call(
        body, name=name, grid=(2, t // tm),
        in_specs=_ffn_specs(tm) + [pl.BlockSpec((tm, FFN_COLS), lambda j, i: (i, j))],
        out_specs=[pl.BlockSpec((2, tm, FFN_COLS), lambda j, i: (0, i, j)), half_w, half_w, half_b, half_b],
        out_shape=[jax.ShapeDtypeStruct((2, t, FFN_HIDDEN), bf16),
                   jax.ShapeDtypeStruct((3, FFN_HIDDEN), f32), jax.ShapeDtypeStruct((3, FFN_HIDDEN), f32),
                   jax.ShapeDtypeStruct((1, FFN_HIDDEN), f32), jax.ShapeDtypeStruct((1, FFN_HIDDEN), f32)],
        scratch_shapes=[pltpu.VMEM((8, STRIP, FFN_COLS), f32)],
        semantics=("parallel", "arbitrary"), rider=rider,
    )(a, a, a, a, cw, cw, cb, cb, dhid)


def conv_transpose(dy, cw, name, tm=512):
    _, t, fh = dy.shape
    tm = _tile(t, tm, STRIP)
    nb = tm // HALO
    last_halo = t // HALO - 1
    ncol = fh // FFN_COLS

    def body(main_ref, halo_ref, w_ref, o_ref):
        rowi = lax.broadcasted_iota(jnp.int32, (STRIP, LANES), 0)
        last_block = pl.program_id(2) == pl.num_programs(2) - 1

        def strip(r, edge):
            rows = _strip_rows(r)
            for c in range(FFN_COLS // LANES):
                cols = pl.ds(c * LANES, LANES)
                cur = main_ref[0, rows, cols].astype(f32)
                if edge:
                    h = jnp.where(last_block, 0.0, halo_ref[0, :, cols].astype(f32))
                    nxt = jnp.concatenate([h, jnp.zeros_like(h)], axis=0)
                else:
                    nxt = main_ref[0, _strip_rows(r + 1), cols].astype(f32)
                d1 = jnp.where(rowi >= STRIP - 1, pltpu.roll(nxt, STRIP - 1, axis=0), pltpu.roll(cur, STRIP - 1, axis=0))
                d2 = jnp.where(rowi >= STRIP - 2, pltpu.roll(nxt, STRIP - 2, axis=0), pltpu.roll(cur, STRIP - 2, axis=0))
                o_ref[0, rows, cols] = (w_ref[2:3, cols] * cur + w_ref[1:2, cols] * d1 + w_ref[0:1, cols] * d2).astype(bf16)

        _for_strips(tm // STRIP, strip, reverse=True)

    return pl.pallas_call(
        body, name=name, grid=(2, ncol, t // tm),
        in_specs=[pl.BlockSpec((1, tm, FFN_COLS), lambda p, j, i: (p, i, j)),
                  pl.BlockSpec((1, HALO, FFN_COLS), lambda p, j, i: (p, jnp.minimum((i + 1) * nb, last_halo), j)),
                  pl.BlockSpec((3, FFN_COLS), lambda p, j, i: (0, p * ncol + j))],
        out_specs=pl.BlockSpec((1, tm, FFN_COLS), lambda p, j, i: (p, i, j)),
        out_shape=jax.ShapeDtypeStruct((2, t, fh), bf16),
        compiler_params=_cparams("parallel", "parallel", "arbitrary"),
    )(dy, dy, cw)


def ada_mod(c_all, ada_w, ada_b_cols, name):
    cols = ada_w.shape[2]

    def body(c_ref, w_ref, b_ref, o_ref):
        c = c_ref[...]
        cond = (c * jax.nn.sigmoid(c)).astype(bf16)
        o_ref[0] = jnp.dot(cond, w_ref[0].astype(bf16), preferred_element_type=f32) + b_ref[0]

    return pl.pallas_call(
        body, name=name, grid=(DEPTH,),
        in_specs=[pl.BlockSpec((N_DEV, D), lambda i: (0, 0)), pl.BlockSpec((1, D, cols), lambda i: (i, 0, 0)),
                  pl.BlockSpec((1, 1, cols), lambda i: (i, 0, 0))],
        out_specs=pl.BlockSpec((1, N_DEV, cols), lambda i: (i, 0, 0)),
        out_shape=jax.ShapeDtypeStruct((DEPTH, N_DEV, cols), f32), compiler_params=_cparams("parallel"),
    )(c_all, ada_w, ada_b_cols)


def ada_grads(c_all, dmod_cols, dmod_all, name):
    cols = dmod_cols.shape[2]

    def body(c_ref, dm_ref, da_ref, dw_ref, db_ref):
        c = c_ref[...]
        cond = c * jax.nn.sigmoid(c)
        dw_ref[0] = lax.dot_general(cond, dm_ref[0], (((0,), (0,)), ((), ())), precision=lax.Precision.HIGHEST,
                                    preferred_element_type=f32)
        acc = da_ref[0, 0]
        for e in range(1, N_DEV):
            acc = acc + da_ref[e, 0]
        db_ref[0] = acc

    return pl.pallas_call(
        body, name=name, grid=(DEPTH,),
        in_specs=[pl.BlockSpec((N_DEV, D), lambda i: (0, 0)), pl.BlockSpec((1, N_DEV, cols), lambda i: (i, 0, 0)),
                  pl.BlockSpec((N_DEV, 1, 1, 6 * D), lambda i: (0, i, 0, 0))],
        out_specs=(pl.BlockSpec((1, D, cols), lambda i: (i, 0, 0)), pl.BlockSpec((1, 1, 6 * D), lambda i: (i, 0, 0))),
        out_shape=(jax.ShapeDtypeStruct((DEPTH, D, cols), f32), jax.ShapeDtypeStruct((DEPTH, 1, 6 * D), f32)),
        compiler_params=_cparams("parallel"),
    )(c_all, dmod_cols, dmod_all)


def lower_bound_fwd(hg_lb, name):
    n = hg_lb.shape[1]

    def body(l_ref, o_ref):
        o_ref[...] = jax.nn.sigmoid(l_ref[1:2, :] - l_ref[0:1, :])

    return pl.pallas_call(body, name=name, out_shape=jax.ShapeDtypeStruct((1, n), f32))(hg_lb)


def lower_bound_bwd(hg_lb, dlb, name):
    n = hg_lb.shape[1]

    def body(l_ref, d_ref, o_ref):
        p = jax.nn.sigmoid(l_ref[1:2, :] - l_ref[0:1, :])
        g = d_ref[...] * p * (1.0 - p)
        o_ref[0:1, :] = -g
        o_ref[1:2, :] = g

    return pl.pallas_call(body, name=name, out_shape=jax.ShapeDtypeStruct((2, n), f32))(hg_lb, dlb)


def _adamw(w, g, m, v):
    m = ADAM_B1 * m + (1.0 - ADAM_B1) * g
    v = ADAM_B2 * v + (1.0 - ADAM_B2) * jnp.square(g)
    m_hat = m / (1.0 - ADAM_B1 ** ADAM_STEP)
    v_hat = v / (1.0 - ADAM_B2 ** ADAM_STEP)
    delta = -ADAM_LR * (m_hat / (jnp.sqrt(v_hat) + ADAM_EPS) + ADAM_WD * w)
    return delta, m, v


def adam_reduced(parts, w, m, v, name, tr=128):
    r, c = w.shape
    tr = _tile(r, tr, 8)

    def body(p_ref, w_ref, m_ref, v_ref, g_ref, d_ref, mo_ref, vo_ref):
        g = p_ref[0].astype(f32)
        for j in range(1, N_DEV):
            g = g + p_ref[j].astype(f32)
        g_ref[...] = g
        d_ref[...], mo_ref[...], vo_ref[...] = _adamw(w_ref[...], g, m_ref[...], v_ref[...])

    blk = pl.BlockSpec((tr, c), lambda i: (i, 0))
    out = jax.ShapeDtypeStruct((r, c), f32)
    return pl.pallas_call(
        body, name=name, grid=(r // tr,),
        in_specs=[pl.BlockSpec((N_DEV, tr, c), lambda i: (0, i, 0)), blk, blk, blk],
        out_specs=(blk, blk, blk, blk), out_shape=(out, out, out, out), compiler_params=_cparams("parallel"),
    )(parts, w, m, v)


def adam_plain(g, w, m, v, name, tr=256):
    r, c = w.shape
    tr = _tile(r, tr, 8)

    def body(g_ref, w_ref, m_ref, v_ref, d_ref, mo_ref, vo_ref):
        d_ref[...], mo_ref[...], vo_ref[...] = _adamw(w_ref[...], g_ref[...], m_ref[...], v_ref[...])

    blk = pl.BlockSpec((tr, c), lambda i: (i, 0))
    out = jax.ShapeDtypeStruct((r, c), f32)
    return pl.pallas_call(
        body, name=name, grid=(r // tr,), in_specs=[blk, blk, blk, blk], out_specs=(blk, blk, blk),
        out_shape=(out, out, out), compiler_params=_cparams("parallel"),
    )(g, w, m, v)


def sum_parts(parts, name):
    _, r, c = parts.shape

    def body(p_ref, o_ref):
        acc = p_ref[0]
        for j in range(1, N_DEV):
            acc = acc + p_ref[j]
        o_ref[...] = acc

    return pl.pallas_call(body, name=name, out_shape=jax.ShapeDtypeStruct((r, c), f32))(parts)


def _pack(arrs, rows_mult=8):
    flat = jnp.concatenate([a.reshape(-1) for a in arrs])
    rows = -(-flat.shape[0] // LANES)
    rows = -(-rows // rows_mult) * rows_mult
    return jnp.pad(flat, (0, rows * LANES - flat.shape[0])).reshape(rows, LANES)


def _unpack(flat, shapes):
    out, at = [], 0
    for s in shapes:
        n = math.prod(s)
        out.append(flat[at:at + n].reshape(s))
        at += n
    return out


def kernel(x, c, gm_w_in, gm_ln_g, gm_ln_b, gm_w_s, gm_b_s, gm_w_out, hg_w_in, hg_lb, hg_gn_g, hg_w_out, ffn_w_up, ffn_conv_w, ffn_conv_b, ffn_w_down, norm_g, ada_w, ada_b, final_g, loss_target, m_gm_w_in, m_gm_ln_g, m_gm_ln_b, m_gm_w_s, m_gm_b_s, m_gm_w_out, m_hg_w_in, m_hg_lb, m_hg_gn_g, m_hg_w_out, m_ffn_w_up, m_ffn_conv_w, m_ffn_conv_b, m_ffn_w_down, m_norm_g, m_ada_w, m_ada_b, m_final_g, v_gm_w_in, v_gm_ln_g, v_gm_ln_b, v_gm_w_s, v_gm_b_s, v_gm_w_out, v_hg_w_in, v_hg_lb, v_hg_gn_g, v_hg_w_out, v_ffn_w_up, v_ffn_conv_w, v_ffn_conv_b, v_ffn_w_down, v_norm_g, v_ada_w, v_ada_b, v_final_g):
    me = _flat(_mesh_pos())
    xt = x[0]
    t = xt.shape[0]

    small_shapes = [(1, D), (2, HG_DIM), (2, HG_DIM), (DEPTH, 2, HG_DIM), (DEPTH, 3, 2 * FFN_HIDDEN // N_DEV)]
    (small_all,) = all_gather([_pack([c, hg_lb, hg_gn_g, norm_g, ffn_conv_w])], "gather_small")
    small_all = small_all.reshape(N_DEV, -1)
    at = 0
    pieces = []
    for s in small_shapes:
        n = math.prod(s)
        pieces.append(small_all[:, at:at + n].reshape((N_DEV,) + s))
        at += n
    c_all = pieces[0].reshape(N_DEV, D)
    hg_lb_full = jnp.transpose(pieces[1], (1, 0, 2)).reshape(2, D)
    hg_gn_full = jnp.transpose(pieces[2], (1, 0, 2)).reshape(2, D)
    norm_g_full = jnp.transpose(pieces[3], (1, 2, 0, 3)).reshape(DEPTH, 2, D)
    conv_w_full = jnp.transpose(pieces[4], (1, 2, 0, 3)).reshape(DEPTH, 3, 2 * FFN_HIDDEN)

    lb1 = lower_bound_fwd(hg_lb_full, "lower_bound")
    lbs = [jnp.zeros((1, D), f32), lb1]

    ada_b_cols = lax.dynamic_slice(ada_b, (0, me * ADA_COLS), (DEPTH, ADA_COLS)).reshape(DEPTH, 1, ADA_COLS)
    mod_cols = ada_mod(c_all, ada_w, ada_b_cols, "ada_mod")
    (mod_mine,) = all_to_all([jnp.transpose(mod_cols, (1, 0, 2))], "mod_to_examples")
    mod = jnp.transpose(mod_mine, (1, 0, 2)).reshape(DEPTH, 6, 1, D)

    def layer_shards(i):
        j = i // 2
        w_in, w_out = (gm_w_in, gm_w_out) if i % 2 == 0 else (hg_w_in, hg_w_out)
        return [w_in[j].astype(bf16), w_out[j].astype(bf16), ffn_w_up[i].astype(bf16), ffn_w_down[i].astype(bf16)]

    def full_cols(g):
        return jnp.transpose(g, (1, 0, 2)).reshape(g.shape[1], N_DEV * g.shape[2])

    def full_rows(g):
        return g.reshape(N_DEV * g.shape[1], g.shape[2])

    gathered = list(all_gather(layer_shards(0), "gather_weights_0"))

    saved = []
    weights = []
    xcur = xt
    for i in range(DEPTH):
        j = i // 2
        w_in, w_out, w_up, w_down = full_cols(gathered[0]), full_rows(gathered[1]), full_cols(gathered[2]), full_rows(gathered[3])
        weights.append((w_in, w_out, w_up, w_down))
        riders = [Rider(a, True) for a in layer_shards(i + 1)] if i + 1 < DEPTH else [None] * 4
        gathered = [None] * 4

        def ride(outs, slot):
            if riders[slot] is None:
                return outs
            gathered[slot] = outs[-1]
            return outs[:-1]

        sh1, sc1, g1, sh2, sc2, g2 = [mod[i, p] for p in range(6)]
        gn1, gn2 = norm_g_full[i, 0:1], norm_g_full[i, 1:2]
        s = {"x0": xcur}
        h = norm_fwd(xcur, gn1, sc1, sh1, f"norm1_{i}")
        s["h"] = h
        if i % 2 == 0:
            (z,) = ride(mm_nn(h, w_in, bf16, f"gm_in_{i}", rider=riders[0]), 0)
            bs = gm_b_s[j].reshape(GM_HEADS, GM_BLOCK, 1)
            mixed = gm_mix_fwd(z, gm_w_s[j], bs, gm_ln_g[j:j + 1], gm_ln_b[j:j + 1], f"gm_mix_{i}")
            s["z"] = z
        else:
            (proj,) = ride(mm_nn(h, w_in, f32, f"hg_in_{i}", rider=riders[0]), 0)
            mixed, states = hg_scan_fwd(proj, lbs[j], hg_gn_full[j:j + 1], f"hg_scan_{i}")
            s["proj"], s["states"] = proj, states
        s["mixed"] = mixed
        y, x1 = ride(mm_nn_residual(mixed, w_out, xcur, g1, f"mix_out_{i}", rider=riders[1]), 1)
        s["y"], s["x1"] = y, x1
        h2 = norm_fwd(x1, gn2, sc2, sh2, f"norm2_{i}")
        (a,) = ride(mm_nn(h2, w_up, bf16, f"ffn_up_{i}", rider=riders[2]), 2)
        hid = ffn_gate_fwd(a, conv_w_full[i], ffn_conv_b[i:i + 1], f"ffn_gate_{i}")
        fo, x2 = ride(mm_nn_residual(hid, w_down, x1, g2, f"ffn_down_{i}", rider=riders[3]), 3)
        s["h2"], s["a"], s["hid"], s["f"] = h2, a, hid, fo
        saved.append(s)
        xcur = x2

    loss_part, dx, d_final_g, dg2, df = loss_head(xcur, final_g.reshape(1, D), loss_target[0], saved[-1]["f"],
                                                  mod[DEPTH - 1, 5], "loss_head")
    loss = lax.psum(loss_part[0, 0], ("x", "y", "c"))

    def blocked_cols(dw):
        k, n = dw.shape
        return jnp.transpose(dw.reshape(k, N_DEV, n // N_DEV), (1, 0, 2))

    def blocked_rows(dw):
        k, n = dw.shape
        return dw.reshape(N_DEV, k // N_DEV, n)

    received = [[None] * 4 for _ in range(DEPTH)]
    pending = None

    def take(outs, where):
        if where is None:
            return outs
        received[where[0]][where[1]] = outs[-1]
        return outs[:-1]

    dmod = [None] * DEPTH
    d_norm_g = [None] * DEPTH
    d_gm = {k: [None, None] for k in ("ws", "bs", "lng", "lnb")}
    d_hg = {k: [None, None] for k in ("lb", "gn")}
    d_ffn = {k: [None] * DEPTH for k in ("cw", "cb")}
    for i in reversed(range(DEPTH)):
        j = i // 2
        s = saved[i]
        w_in, w_out, w_up, w_down = weights[i]
        sh1, sc1, g1, sh2, sc2, g2 = [mod[i, p] for p in range(6)]
        gn1, gn2 = norm_g_full[i, 0:1], norm_g_full[i, 1:2]
        (dw_down,) = mm_tn(s["hid"], df, bf16, f"dw_down_{i}")
        (dhid,) = take(mm_nt(df, w_down, bf16, f"dhid_{i}", tn=FFN_COLS, rider=Rider(blocked_rows(dw_down), False)), (i, 3))
        outs = ffn_gate_bwd(s["a"], conv_w_full[i], ffn_conv_b[i:i + 1], dhid, f"ffn_gate_bwd_{i}",
                            rider=None if pending is None else Rider(pending[0], False))
        dyc, dwg, dwv, dbg, dbv = take(outs, None if pending is None else pending[1])
        d_ffn["cw"][i] = jnp.concatenate([dwg, dwv], axis=1)
        d_ffn["cb"][i] = jnp.concatenate([dbg, dbv], axis=1)
        da = conv_transpose(dyc, conv_w_full[i], f"conv_t_{i}")
        da = jnp.transpose(da, (1, 0, 2)).reshape(t, 2 * FFN_HIDDEN)
        (dw_up,) = mm_tn(s["h2"], da, bf16, f"dw_up_{i}")
        (dh2,) = mm_nt(da, w_up, bf16, f"dh2_{i}")
        dx1, dgn2, dsc2, dsh2, dg1, dy = norm_bwd(s["x1"], gn2, sc2, sh2, dh2, dx, f"norm2_bwd_{i}", gate=(s["y"], g1))
        (dw_out,) = mm_tn(s["mixed"], dy, bf16, f"dw_mix_out_{i}")
        (dmixed,) = take(mm_nt(dy, w_out, bf16, f"dmixed_{i}", rider=Rider(blocked_rows(dw_out), False)), (i, 1))
        up_rider = Rider(blocked_cols(dw_up), False)
        if i % 2 == 0:
            bs = gm_b_s[j].reshape(GM_HEADS, GM_BLOCK, 1)
            dpre, dws, dbs, dlng, dlnb = take(
                gm_mix_bwd(s["z"], gm_w_s[j], bs, gm_ln_g[j:j + 1], gm_ln_b[j:j + 1], dmixed, f"gm_mix_bwd_{i}",
                           rider=up_rider), (i, 2))
            d_gm["ws"][j], d_gm["bs"][j], d_gm["lng"][j], d_gm["lnb"][j] = dws, dbs.reshape(GM_HEADS, GM_BLOCK), dlng, dlnb
        else:
            dq, dfz, di, dgp, dlb, dgn = take(
                hg_scan_bwd(s["proj"], lbs[j], hg_gn_full[j:j + 1], s["states"], dmixed, f"hg_scan_bwd_{i}",
                            rider=up_rider), (i, 2))
            d_hg["lb"][j], d_hg["gn"][j] = dlb, dgn
            dpre = jnp.concatenate([dq, dfz, di, dgp], axis=1)
        (dw_in,) = mm_tn(s["h"], dpre, bf16, f"dw_mix_in_{i}")
        pending = (blocked_cols(dw_in), (i, 0))
        (dh,) = mm_nt(dpre, w_in, bf16, f"dh_mix_{i}")
        dmod_i = [None, None, dg1, dsh2, dsc2, dg2]
        if i > 0:
            dx, dgn1, dsc1, dsh1, dg2, df = norm_bwd(s["x0"], gn1, sc1, sh1, dh, dx1, f"norm1_bwd_{i}",
                                                     gate=(saved[i - 1]["f"], mod[i - 1, 5]))
        else:
            dx, dgn1, dsc1, dsh1 = norm_bwd(s["x0"], gn1, sc1, sh1, dh, dx1, f"norm1_bwd_{i}")
        dmod_i[0], dmod_i[1] = dsh1, dsc1
        dmod[i] = jnp.concatenate(dmod_i, axis=1)
        d_norm_g[i] = jnp.concatenate([dgn1, dgn2], axis=0)
    grad_x = dx.reshape(1, t, D)
    (received[0][0],) = all_to_all([pending[0]], "weight_grads_last")

    (dmod_all,) = all_gather([jnp.concatenate(dmod, axis=0)], "gather_dmod")
    dmod_cols = jnp.transpose(lax.dynamic_slice(dmod_all, (0, 0, me * ADA_COLS), (N_DEV, DEPTH, ADA_COLS)), (1, 0, 2))
    g_ada_w, g_ada_b = ada_grads(c_all, dmod_cols, dmod_all.reshape(N_DEV, DEPTH, 1, 6 * D), "ada_grads")
    g_ada_b = g_ada_b.reshape(DEPTH, 6 * D)

    small_partials = [jnp.concatenate(d_gm["lng"], axis=0), jnp.concatenate(d_gm["lnb"], axis=0),
                      jnp.stack(d_gm["ws"]), jnp.stack(d_gm["bs"]), jnp.concatenate(d_ffn["cb"], axis=0),
                      d_final_g, d_hg["lb"][1], jnp.concatenate(d_hg["gn"], axis=0), jnp.stack(d_norm_g),
                      jnp.stack(d_ffn["cw"])]
    partial_shapes = [p.shape for p in small_partials]
    packed = _pack(small_partials, rows_mult=8 * N_DEV)
    rows = packed.shape[0] // N_DEV
    (recv,) = all_to_all([packed.reshape(N_DEV, rows, LANES)], "small_grads_exchange")
    (summed,) = all_gather([sum_parts(recv, "small_grads_sum")], "small_grads_gather")
    g_ln_g, g_ln_b, g_ws, g_bs, g_cb, g_final, g_lb1, g_gn, g_norm, g_cw = _unpack(summed.reshape(-1), partial_shapes)
    g_final = g_final.reshape(D)

    def my_cols(a, n):
        start = (0,) * (a.ndim - 1) + (me * n,)
        return lax.dynamic_slice(a, start, a.shape[:-1] + (n,))

    g_hg_lb = lower_bound_bwd(hg_lb, my_cols(g_lb1, HG_DIM), "lower_bound_bwd")
    g_hg_gn = my_cols(g_gn, HG_DIM)
    g_norm_g = my_cols(g_norm, HG_DIM)
    g_conv_w = my_cols(g_cw, 2 * FFN_HIDDEN // N_DEV)

    def parts_of(slot, layers):
        return jnp.concatenate([received[i][slot] for i in layers], axis=1)

    w_shards = [gm_w_in, gm_w_out, hg_w_in, hg_w_out, ffn_w_up, ffn_w_down]
    big_parts = [parts_of(0, (0, 2)), parts_of(1, (0, 2)), parts_of(0, (1, 3)), parts_of(1, (1, 3)),
                 parts_of(2, range(DEPTH)), parts_of(3, range(DEPTH))]
    big_m = [m_gm_w_in, m_gm_w_out, m_hg_w_in, m_hg_w_out, m_ffn_w_up, m_ffn_w_down]
    big_v = [v_gm_w_in, v_gm_w_out, v_hg_w_in, v_hg_w_out, v_ffn_w_up, v_ffn_w_down]
    big = []
    for idx, (w, m_, v_, parts) in enumerate(zip(w_shards, big_m, big_v, big_parts)):
        two_d = (-1, w.shape[-1])
        outs = adam_reduced(parts, w.reshape(two_d), m_.reshape(two_d), v_.reshape(two_d), f"adam_big_{idx}")
        big.append([o.reshape(w.shape) for o in outs])
    (g_gm_w_in, d_gm_w_in, nm_gm_w_in, nv_gm_w_in), (g_gm_w_out, d_gm_w_out, nm_gm_w_out, nv_gm_w_out), \
        (g_hg_w_in, d_hg_w_in, nm_hg_w_in, nv_hg_w_in), (g_hg_w_out, d_hg_w_out, nm_hg_w_out, nv_hg_w_out), \
        (g_ffn_w_up, d_ffn_w_up, nm_ffn_w_up, nv_ffn_w_up), (g_ffn_w_down, d_ffn_w_down, nm_ffn_w_down, nv_ffn_w_down) = big

    two_d = (-1, ADA_COLS)
    d_ada_w, nm_ada_w, nv_ada_w = [o.reshape(ada_w.shape) for o in adam_plain(
        g_ada_w.reshape(two_d), ada_w.reshape(two_d), m_ada_w.reshape(two_d), v_ada_w.reshape(two_d), "adam_ada_w")]

    small_g = [g_ln_g, g_ln_b, g_ws, g_bs, g_cb, g_ada_b, g_final, g_hg_lb, g_hg_gn, g_norm_g, g_conv_w]
    small_w = [gm_ln_g, gm_ln_b, gm_w_s, gm_b_s, ffn_conv_b, ada_b, final_g, hg_lb, hg_gn_g, norm_g, ffn_conv_w]
    small_m = [m_gm_ln_g, m_gm_ln_b, m_gm_w_s, m_gm_b_s, m_ffn_conv_b, m_ada_b, m_final_g, m_hg_lb, m_hg_gn_g, m_norm_g, m_ffn_conv_w]
    small_v = [v_gm_ln_g, v_gm_ln_b, v_gm_w_s, v_gm_b_s, v_ffn_conv_b, v_ada_b, v_final_g, v_hg_lb, v_hg_gn_g, v_norm_g, v_ffn_conv_w]
    shapes = [w.shape for w in small_w]
    small_g = [g.reshape(s) for g, s in zip(small_g, shapes)]
    outs = adam_plain(_pack(small_g), _pack(small_w), _pack(small_m), _pack(small_v), "adam_small")
    (d_ln_g, d_ln_b, d_ws, d_bs, d_cb, d_ada_b, d_final, d_hg_lb, d_hg_gn, d_norm_g_, d_conv_w), \
        (nm_ln_g, nm_ln_b, nm_ws, nm_bs, nm_cb, nm_ada_b, nm_final, nm_hg_lb, nm_hg_gn, nm_norm_g, nm_conv_w), \
        (nv_ln_g, nv_ln_b, nv_ws, nv_bs, nv_cb, nv_ada_b, nv_final, nv_hg_lb, nv_hg_gn, nv_norm_g, nv_conv_w) = [
            _unpack(o.reshape(-1), shapes) for o in outs]
    g_ln_g, g_ln_b, g_ws, g_bs, g_cb, g_ada_b, g_final, g_hg_lb, g_hg_gn, g_norm_g, g_conv_w = small_g

    grads = (g_gm_w_in, g_ln_g, g_ln_b, g_ws, g_bs, g_gm_w_out, g_hg_w_in, g_hg_lb, g_hg_gn, g_hg_w_out,
             g_ffn_w_up, g_conv_w, g_cb, g_ffn_w_down, g_norm_g, g_ada_w, g_ada_b, g_final)
    deltas = (d_gm_w_in, d_ln_g, d_ln_b, d_ws, d_bs, d_gm_w_out, d_hg_w_in, d_hg_lb, d_hg_gn, d_hg_w_out,
              d_ffn_w_up, d_conv_w, d_cb, d_ffn_w_down, d_norm_g_, d_ada_w, d_ada_b, d_final)
    new_m = (nm_gm_w_in, nm_ln_g, nm_ln_b, nm_ws, nm_bs, nm_gm_w_out, nm_hg_w_in, nm_hg_lb, nm_hg_gn, nm_hg_w_out,
             nm_ffn_w_up, nm_conv_w, nm_cb, nm_ffn_w_down, nm_norm_g, nm_ada_w, nm_ada_b, nm_final)
    new_v = (nv_gm_w_in, nv_ln_g, nv_ln_b, nv_ws, nv_bs, nv_gm_w_out, nv_hg_w_in, nv_hg_lb, nv_hg_gn, nv_hg_w_out,
             nv_ffn_w_up, nv_conv_w, nv_cb, nv_ffn_w_down, nv_norm_g, nv_ada_w, nv_ada_b, nv_final)
    return (loss, grad_x) + grads + deltas + new_m + new_v
```

```python
import functools
import math

import jax
import jax.numpy as jnp
from jax import lax
from jax.experimental import pallas as pl
from jax.experimental.pallas import tpu as pltpu

f32 = jnp.float32
bf16 = jnp.bfloat16
MESH = pl.DeviceIdType.MESH

N_DEV = 8
D = 1024
DEPTH = 4
EPS = 1e-6
GM_WIDTH = 2048
GM_HEADS = 8
GM_HEAD_DIM = 256
GM_BLOCK = 128
CHUNK = 64
HG_HEADS = 8
HG_DIM = 128
FFN_HIDDEN = 2816
ADA_COLS = 6 * D // N_DEV

HG_SUB = 32
HG_PAIR = 8
HG_TOKENS = 128

ADAM_LR = 0.001
ADAM_B1 = 0.9
ADAM_B2 = 0.999
ADAM_EPS = 1e-08
ADAM_WD = 0.01
ADAM_STEP = 10

V7X_VMEM_LIMIT = 56 * 1024 * 1024
LANES = 128


def _cparams(*sem):
    return pltpu.CompilerParams(dimension_semantics=sem or None, vmem_limit_bytes=V7X_VMEM_LIMIT)


def _tile(n, target, mult=LANES):
    best = None
    for t in range(mult, min(n, target) + 1, mult):
        if n % t == 0:
            best = t
    return best or n


WEIGHT_BLOCK_BYTES = 6 * 1024 * 1024


def _weight_tile(n, k):
    return _tile(n, max(LANES, WEIGHT_BLOCK_BYTES // (2 * k)))


def _gelu(x):
    return 0.5 * x * (1.0 + lax.erf(x * (1.0 / math.sqrt(2.0))))


def _mesh_pos():
    return lax.axis_index("x"), lax.axis_index("y"), lax.axis_index("c")


def _flat(pos):
    return 4 * pos[0] + 2 * pos[1] + pos[2]


def _peer(pos, k):
    return ((1 - pos[0]) if k & 4 else pos[0], (1 - pos[1]) if k & 2 else pos[1], (1 - pos[2]) if k & 1 else pos[2])


def _exchange_copies(ins, outs, send_sems, recv_sems, local_sems, gather):
    pos = _mesh_pos()
    me = _flat(pos)

    def src(i, dest):
        return ins[i] if gather else ins[i].at[dest]

    local = [pltpu.make_async_copy(src(i, me), outs[i].at[me], local_sems.at[i]) for i in range(len(ins))]
    sends, recvs = [], []
    for k in range(1, N_DEV):
        peer = _peer(pos, k)
        there = _flat(peer)
        for i in range(len(ins)):
            sems = dict(send_sem=send_sems.at[i * 7 + k - 1], recv_sem=recv_sems.at[i * 7 + k - 1],
                        device_id=peer, device_id_type=MESH)
            sends.append(pltpu.make_async_remote_copy(src_ref=src(i, there), dst_ref=outs[i].at[me], **sems))
            recvs.append(pltpu.make_async_remote_copy(src_ref=src(i, there), dst_ref=outs[i].at[there], **sems))
    return local, sends, recvs


def _exchange_start(*refs):
    local, sends, _ = _exchange_copies(*refs)
    for cp in local + sends:
        cp.start()


def _exchange_wait(*refs):
    local, sends, recvs = _exchange_copies(*refs)
    for cp in recvs:
        cp.wait_recv()
    for cp in sends:
        cp.wait_send()
    for cp in local:
        cp.wait()


OTHER_CHIPS = (2, 4, 6)


def _relay_copies(ins, outs, send_sems, recv_sems, local_sems):
    pos = _mesh_pos()
    me = _flat(pos)
    sibling = _peer(pos, 1)
    local = [pltpu.make_async_copy(ins[i], outs[i].at[me], local_sems.at[i]) for i in range(len(ins))]
    first, passes, recvs = [], {k: [] for k in OTHER_CHIPS}, {k: [] for k in range(1, N_DEV)}
    for i in range(len(ins)):
        def copy(k, src, block, to):
            return pltpu.make_async_remote_copy(
                src_ref=src, dst_ref=outs[i].at[block], send_sem=send_sems.at[i * 7 + k - 1],
                recv_sem=recv_sems.at[i * 7 + k - 1], device_id=to, device_id_type=MESH)

        for k in (1,) + OTHER_CHIPS:
            first.append(copy(k, ins[i], me, _peer(pos, k)))
        for k in OTHER_CHIPS:
            there = _flat(_peer(pos, k))
            passes[k].append(copy(k ^ 1, outs[i].at[there], there, sibling))
        for k in range(1, N_DEV):
            there = _flat(_peer(pos, k))
            recvs[k].append(copy(k, ins[i], there, _peer(pos, k)))
    return local, first, passes, recvs


def _relay_start(ins, outs, *sems):
    local, first, _, _ = _relay_copies(ins, outs, *sems)
    for cp in local + first:
        cp.start()


def _relay_wait(ins, outs, *sems):
    local, first, passes, recvs = _relay_copies(ins, outs, *sems)
    for k in OTHER_CHIPS:
        for cp in recvs[k]:
            cp.wait_recv()
        for cp in passes[k]:
            cp.start()
    for k in (1, 3, 5, 7):
        for cp in recvs[k]:
            cp.wait_recv()
    for cp in first + [cp for k in OTHER_CHIPS for cp in passes[k]]:
        cp.wait_send()
    for cp in local:
        cp.wait()


def _exchange_out_shape(a, gather):
    return jax.ShapeDtypeStruct((N_DEV,) + tuple(a.shape) if gather else tuple(a.shape), a.dtype)


def _exchange_sems(n):
    return [pltpu.SemaphoreType.DMA((7 * n,)), pltpu.SemaphoreType.DMA((7 * n,)), pltpu.SemaphoreType.DMA((n,))]


ANY = pl.BlockSpec(memory_space=pl.ANY)


def _exchange(arrs, gather, name, relay=False):
    n = len(arrs)

    def body(*refs):
        ins, outs = refs[:n], refs[n:2 * n]
        if relay:
            _relay_start(ins, outs, *refs[2 * n:])
            _relay_wait(ins, outs, *refs[2 * n:])
        else:
            _exchange_start(ins, outs, *refs[2 * n:], gather)
            _exchange_wait(ins, outs, *refs[2 * n:], gather)

    return pl.pallas_call(
        body, name=name, out_shape=tuple(_exchange_out_shape(a, gather) for a in arrs),
        in_specs=[ANY] * n, out_specs=tuple([ANY] * n), scratch_shapes=_exchange_sems(n),
    )(*arrs)


def all_gather(arrs, name, relay=False):
    return _exchange(arrs, True, name, relay)


def all_to_all(arrs, name):
    return _exchange(arrs, False, name)


class Rider:
    def __init__(self, arr, gather):
        self.arr, self.gather = arr, gather


def _call(body, *, name, grid, in_specs, out_specs, out_shape, semantics, scratch_shapes=(), rider=None):
    if rider is None:
        return pl.pallas_call(body, name=name, grid=grid, in_specs=in_specs, out_specs=tuple(out_specs),
                              out_shape=tuple(out_shape), scratch_shapes=list(scratch_shapes),
                              compiler_params=_cparams(*semantics))
    n_in, n_out, n_scr = len(in_specs), len(out_specs), len(scratch_shapes)
    gather = rider.gather

    def hosted(*refs):
        ins, r_in = refs[:n_in], refs[n_in]
        outs, r_out = refs[n_in + 1:n_in + 1 + n_out], refs[n_in + 1 + n_out]
        scratch = refs[n_in + 2 + n_out:n_in + 2 + n_out + n_scr]
        sems = refs[n_in + 2 + n_out + n_scr:]
        first = functools.reduce(jnp.logical_and, [pl.program_id(a) == 0 for a in range(len(grid))])
        last = functools.reduce(jnp.logical_and, [pl.program_id(a) == grid[a] - 1 for a in range(len(grid))])

        @pl.when(first)
        def _():
            if gather:
                _relay_start([r_in], [r_out], *sems)
            else:
                _exchange_start([r_in], [r_out], *sems, gather)

        body(*ins, *outs, *scratch)

        @pl.when(last)
        def _():
            if gather:
                _relay_wait([r_in], [r_out], *sems)
            else:
                _exchange_wait([r_in], [r_out], *sems, gather)

    call = pl.pallas_call(
        hosted, name=name, grid=grid, in_specs=list(in_specs) + [ANY], out_specs=tuple(out_specs) + (ANY,),
        out_shape=tuple(out_shape) + (_exchange_out_shape(rider.arr, gather),),
        scratch_shapes=list(scratch_shapes) + _exchange_sems(1),
        compiler_params=_cparams(*(("arbitrary",) * len(grid))))
    return lambda *args: call(*args, rider.arr)


def mm_nn(a, b, out_dtype, name, tm=512, rider=None):
    m, k = a.shape
    n = b.shape[1]
    tm, tn = _tile(m, tm, 8), _weight_tile(n, k)

    def body(a_ref, b_ref, o_ref):
        o_ref[...] = jnp.dot(a_ref[...], b_ref[...], preferred_element_type=f32).astype(o_ref.dtype)

    return _call(
        body, name=name, grid=(m // tm, n // tn),
        in_specs=[pl.BlockSpec((tm, k), lambda i, j: (i, 0)), pl.BlockSpec((k, tn), lambda i, j: (0, j))],
        out_specs=[pl.BlockSpec((tm, tn), lambda i, j: (i, j))],
        out_shape=[jax.ShapeDtypeStruct((m, n), out_dtype)], semantics=("parallel", "parallel"), rider=rider,
    )(a, b)


def mm_nn_residual(a, b, x, gate, name, tm=512, rider=None):
    m, k = a.shape
    n = b.shape[1]
    tm, tn = _tile(m, tm, 8), _weight_tile(n, k)

    def body(a_ref, b_ref, x_ref, g_ref, y_ref, o_ref):
        y = jnp.dot(a_ref[...], b_ref[...], preferred_element_type=f32)
        y_ref[...] = y.astype(bf16)
        o_ref[...] = x_ref[...] + g_ref[...] * y

    return _call(
        body, name=name, grid=(m // tm, n // tn),
        in_specs=[pl.BlockSpec((tm, k), lambda i, j: (i, 0)), pl.BlockSpec((k, tn), lambda i, j: (0, j)),
                  pl.BlockSpec((tm, tn), lambda i, j: (i, j)), pl.BlockSpec((1, tn), lambda i, j: (0, j))],
        out_specs=[pl.BlockSpec((tm, tn), lambda i, j: (i, j)), pl.BlockSpec((tm, tn), lambda i, j: (i, j))],
        out_shape=[jax.ShapeDtypeStruct((m, n), bf16), jax.ShapeDtypeStruct((m, n), f32)],
        semantics=("parallel", "parallel"), rider=rider,
    )(a, b, x, gate)


def mm_nt(a, b, out_dtype, name, tm=512, rider=None):
    m, k = a.shape
    n = b.shape[0]
    tm, tn = _tile(m, tm, 8), _weight_tile(n, k)

    def body(a_ref, b_ref, o_ref):
        o_ref[...] = lax.dot_general(a_ref[...], b_ref[...], (((1,), (1,)), ((), ())),
                                     preferred_element_type=f32).astype(o_ref.dtype)

    return _call(
        body, name=name, grid=(m // tm, n // tn),
        in_specs=[pl.BlockSpec((tm, k), lambda i, j: (i, 0)), pl.BlockSpec((tn, k), lambda i, j: (j, 0))],
        out_specs=[pl.BlockSpec((tm, tn), lambda i, j: (i, j))],
        out_shape=[jax.ShapeDtypeStruct((m, n), out_dtype)], semantics=("parallel", "parallel"), rider=rider,
    )(a, b)


def mm_tn(a, b, out_dtype, name, tm=512, tn=512, rider=None):
    t, m = a.shape
    n = b.shape[1]
    tm, tn = _tile(m, tm), _tile(n, tn)

    def body(a_ref, b_ref, o_ref):
        o_ref[...] = lax.dot_general(a_ref[...], b_ref[...], (((0,), (0,)), ((), ())),
                                     preferred_element_type=f32).astype(o_ref.dtype)

    return _call(
        body, name=name, grid=(m // tm, n // tn),
        in_specs=[pl.BlockSpec((t, tm), lambda i, j: (0, i)), pl.BlockSpec((t, tn), lambda i, j: (0, j))],
        out_specs=[pl.BlockSpec((tm, tn), lambda i, j: (i, j))],
        out_shape=[jax.ShapeDtypeStruct((m, n), out_dtype)], semantics=("parallel", "parallel"), rider=rider,
    )(a, b)


def _norm_fn(x, gn, sc, sh):
    r = lax.rsqrt(jnp.mean(x * x, axis=-1, keepdims=True) + EPS)
    return (x * r * gn) * (1.0 + sc) + sh


def _row(d):
    return pl.BlockSpec((1, d), lambda i: (0, 0))


def norm_fwd(x, gn, sc, sh, name, tm=512):
    t, d = x.shape
    tm = _tile(t, tm, 8)

    def body(x_ref, gn_ref, sc_ref, sh_ref, h_ref):
        h_ref[...] = _norm_fn(x_ref[...], gn_ref[...], sc_ref[...], sh_ref[...]).astype(bf16)

    return pl.pallas_call(
        body, name=name, grid=(t // tm,),
        in_specs=[pl.BlockSpec((tm, d), lambda i: (i, 0)), _row(d), _row(d), _row(d)],
        out_specs=pl.BlockSpec((tm, d), lambda i: (i, 0)),
        out_shape=jax.ShapeDtypeStruct((t, d), bf16), compiler_params=_cparams("parallel"),
    )(x, gn, sc, sh)


def _gate_bwd(dx, y_ref, g_ref, dgate_ref, dy_ref):
    dgate_ref[...] += jnp.sum(dx * y_ref[...].astype(f32), axis=0, keepdims=True)
    dy_ref[...] = (dx * g_ref[...]).astype(bf16)


def norm_bwd(x, gn, sc, sh, dh, dres, name, gate=None, tm=512):
    t, d = x.shape
    tm = _tile(t, tm, 8)

    def body(x_ref, gn_ref, sc_ref, sh_ref, dh_ref, dres_ref, *rest):
        if gate is not None:
            y_ref, g_ref, dx_ref, dgn_ref, dsc_ref, dsh_ref, dgate_ref, dy_ref = rest
        else:
            dx_ref, dgn_ref, dsc_ref, dsh_ref = rest

        @pl.when(pl.program_id(0) == 0)
        def _():
            dgn_ref[...] = jnp.zeros_like(dgn_ref)
            dsc_ref[...] = jnp.zeros_like(dsc_ref)
            dsh_ref[...] = jnp.zeros_like(dsh_ref)
            if gate is not None:
                dgate_ref[...] = jnp.zeros_like(dgate_ref)

        _, vjp = jax.vjp(_norm_fn, x_ref[...], gn_ref[...], sc_ref[...], sh_ref[...])
        dx, dgn, dsc, dsh = vjp(dh_ref[...].astype(f32))
        dx = dx + dres_ref[...]
        dx_ref[...] = dx
        dgn_ref[...] += dgn
        dsc_ref[...] += dsc
        dsh_ref[...] += dsh
        if gate is not None:
            _gate_bwd(dx, y_ref, g_ref, dgate_ref, dy_ref)

    blk = pl.BlockSpec((tm, d), lambda i: (i, 0))
    vec = jax.ShapeDtypeStruct((1, d), f32)
    in_specs = [blk, _row(d), _row(d), _row(d), blk, blk]
    out_specs = [blk, _row(d), _row(d), _row(d)]
    out_shape = [jax.ShapeDtypeStruct((t, d), f32), vec, vec, vec]
    args = [x, gn, sc, sh, dh, dres]
    if gate is not None:
        in_specs += [blk, _row(d)]
        out_specs += [_row(d), blk]
        out_shape += [vec, jax.ShapeDtypeStruct((t, d), bf16)]
        args += list(gate)
    return pl.pallas_call(
        body, name=name, grid=(t // tm,), in_specs=in_specs, out_specs=tuple(out_specs),
        out_shape=tuple(out_shape), compiler_params=_cparams("arbitrary"),
    )(*args)


def _loss_fn(x, g, tgt):
    r = lax.rsqrt(jnp.mean(x * x, axis=-1, keepdims=True) + EPS)
    err = jnp.square(x * r * g - tgt)
    return 0.5 * jnp.sum(jnp.mean(err, axis=-1, keepdims=True), axis=0, keepdims=True)


def loss_head(x, g, tgt, y, gate, name, tm=512):
    t, d = x.shape
    tm = _tile(t, tm, 8)

    def body(x_ref, g_ref, t_ref, y_ref, gate_ref, loss_ref, dx_ref, dg_ref, dgate_ref, dy_ref):
        @pl.when(pl.program_id(0) == 0)
        def _():
            loss_ref[...] = jnp.zeros_like(loss_ref)
            dg_ref[...] = jnp.zeros_like(dg_ref)
            dgate_ref[...] = jnp.zeros_like(dgate_ref)

        loss, vjp = jax.vjp(_loss_fn, x_ref[...], g_ref[...], t_ref[...])
        dx, dg, _ = vjp(jnp.ones((1, 1), f32))
        dx_ref[...] = dx
        loss_ref[...] += loss
        dg_ref[...] += dg
        _gate_bwd(dx, y_ref, gate_ref, dgate_ref, dy_ref)

    blk = pl.BlockSpec((tm, d), lambda i: (i, 0))
    vec = jax.ShapeDtypeStruct((1, d), f32)
    return pl.pallas_call(
        body, name=name, grid=(t // tm,),
        in_specs=[blk, _row(d), blk, blk, _row(d)],
        out_specs=(pl.BlockSpec((1, 1), lambda i: (0, 0)), blk, _row(d), _row(d), blk),
        out_shape=(jax.ShapeDtypeStruct((1, 1), f32), jax.ShapeDtypeStruct((t, d), f32), vec, vec,
                   jax.ShapeDtypeStruct((t, d), bf16)),
        compiler_params=_cparams("arbitrary"),
    )(x, g, tgt, y, gate)


def _gm_block_fn(z, ws, bs, lng, lnb):
    u = _gelu(z[:, :GM_WIDTH])
    vg = _gelu(z[:, GM_WIDTH:])
    mu = jnp.mean(vg, axis=-1, keepdims=True)
    var = jnp.mean(jnp.square(vg - mu), axis=-1, keepdims=True)
    vn = (vg - mu) * lax.rsqrt(var + EPS) * lng + lnb
    row = lax.broadcasted_iota(jnp.int32, (GM_BLOCK, GM_BLOCK), 0) // CHUNK
    col = lax.broadcasted_iota(jnp.int32, (GM_BLOCK, GM_BLOCK), 1) // CHUNK
    parts = []
    for h in range(GM_HEADS):
        w = jnp.where(row >= col, ws[h], 0.0)
        cols = slice(h * GM_HEAD_DIM, (h + 1) * GM_HEAD_DIM)
        s = jnp.dot(w.astype(bf16), vn[:, cols].astype(bf16), preferred_element_type=f32) + bs[h]
        parts.append(u[:, cols] * s)
    return jnp.concatenate(parts, axis=1)


def _gm_param_specs():
    return [pl.BlockSpec((GM_HEADS, GM_BLOCK, GM_BLOCK), lambda i: (0, 0, 0)),
            pl.BlockSpec((GM_HEADS, GM_BLOCK, 1), lambda i: (0, 0, 0)), _row(GM_WIDTH), _row(GM_WIDTH)]


def gm_mix_fwd(z, ws, bs, lng, lnb, name):
    t = z.shape[0]

    def body(z_ref, ws_ref, bs_ref, lng_ref, lnb_ref, o_ref):
        o_ref[...] = _gm_block_fn(z_ref[...].astype(f32), ws_ref[...], bs_ref[...], lng_ref[...],
                                  lnb_ref[...]).astype(bf16)

    return pl.pallas_call(
        body, name=name, grid=(t // GM_BLOCK,),
        in_specs=[pl.BlockSpec((GM_BLOCK, 2 * GM_WIDTH), lambda i: (i, 0))] + _gm_param_specs(),
        out_specs=pl.BlockSpec((GM_BLOCK, GM_WIDTH), lambda i: (i, 0)),
        out_shape=jax.ShapeDtypeStruct((t, GM_WIDTH), bf16), compiler_params=_cparams("parallel"),
    )(z, ws, bs, lng, lnb)


def gm_mix_bwd(z, ws, bs, lng, lnb, dgated, name, rider=None):
    t = z.shape[0]

    def body(z_ref, ws_ref, bs_ref, lng_ref, lnb_ref, dg_ref, dz_ref, dws_ref, dbs_ref, dlng_ref, dlnb_ref):
        _, vjp = jax.vjp(_gm_block_fn, z_ref[...].astype(f32), ws_ref[...], bs_ref[...], lng_ref[...], lnb_ref[...])
        dz, dws, dbs, dlng, dlnb = vjp(dg_ref[...].astype(f32))
        dz_ref[...] = dz.astype(bf16)

        @pl.when(pl.program_id(0) == 0)
        def _():
            dws_ref[...] = jnp.zeros_like(dws_ref)
            dbs_ref[...] = jnp.zeros_like(dbs_ref)
            dlng_ref[...] = jnp.zeros_like(dlng_ref)
            dlnb_ref[...] = jnp.zeros_like(dlnb_ref)

        dws_ref[...] += dws
        dbs_ref[...] += dbs
        dlng_ref[...] += dlng
        dlnb_ref[...] += dlnb

    zblk = pl.BlockSpec((GM_BLOCK, 2 * GM_WIDTH), lambda i: (i, 0))
    return _call(
        body, name=name, grid=(t // GM_BLOCK,),
        in_specs=[zblk] + _gm_param_specs() + [pl.BlockSpec((GM_BLOCK, GM_WIDTH), lambda i: (i, 0))],
        out_specs=[zblk] + _gm_param_specs(),
        out_shape=[jax.ShapeDtypeStruct((t, 2 * GM_WIDTH), bf16),
                   jax.ShapeDtypeStruct((GM_HEADS, GM_BLOCK, GM_BLOCK), f32),
                   jax.ShapeDtypeStruct((GM_HEADS, GM_BLOCK, 1), f32),
                   jax.ShapeDtypeStruct((1, GM_WIDTH), f32), jax.ShapeDtypeStruct((1, GM_WIDTH), f32)],
        semantics=("arbitrary",), rider=rider,
    )(z, ws, bs, lng, lnb, dgated)


def _hg_block_fn(qp, fz, iv, gp, s0, lb, gn):
    n, ns, d = HG_SUB, HG_TOKENS // HG_SUB, HG_DIM
    p, nb, per_sub = HG_PAIR, HG_TOKENS // HG_PAIR, HG_SUB // HG_PAIR
    f = lb + (1.0 - lb) * jax.nn.sigmoid(fz)
    g = jnp.log(f)
    k = 1.0 - f
    q = qp * jax.nn.sigmoid(qp)
    v = iv.astype(bf16)
    row = lax.broadcasted_iota(jnp.int32, (HG_TOKENS, HG_TOKENS), 0)
    col = lax.broadcasted_iota(jnp.int32, (HG_TOKENS, HG_TOKENS), 1)
    same_sub = col // n == row // n
    tri = ((col <= row) & same_sub).astype(f32)
    cum = jnp.dot(tri, g, precision=lax.Precision.HIGHEST, preferred_element_type=f32)
    cum_b, q_b, k_b = cum.reshape(nb, p, d), q.reshape(nb, p, d), k.reshape(nb, p, d)
    j_b = lax.broadcasted_iota(jnp.int32, (nb, p, d), 1)
    first = (row // p) * p
    scores_t = jnp.zeros((HG_TOKENS, HG_TOKENS), f32)
    for i in range(p):
        rel = jnp.where(j_b <= i, cum_b[:, i:i + 1, :] - cum_b, -1e30)
        pair = jnp.sum(q_b[:, i:i + 1, :] * k_b * jnp.exp(rel), axis=2, keepdims=True)
        scores_t = scores_t + jnp.where(col == first + i, pair.reshape(HG_TOKENS, 1), 0.0)
    o = lax.dot_general(scores_t.astype(bf16), v, (((0,), (0,)), ((), ())), preferred_element_type=f32)
    last = cum_b[:, p - 1:p, :]
    before = jnp.concatenate([jnp.zeros((1, 1, d), f32), last[:-1]], axis=0)
    before = jnp.broadcast_to(before, (nb, p, d)).reshape(HG_TOKENS, d)
    block = (lax.broadcasted_iota(jnp.int32, (HG_TOKENS, d), 0) // p) % per_sub
    q_late = q * jnp.exp(jnp.where(block > 0, cum - before, -1e30))
    last_s = last.reshape(ns, per_sub, d)
    q_parts, k_parts = [], []
    for m in range(1, per_sub):
        split = jnp.broadcast_to(last_s[:, m - 1:m, :], (ns, n, d)).reshape(HG_TOKENS, d)
        k_parts.append(k * jnp.exp(jnp.where(block < m, split - cum, -1e30)))
        q_parts.append(jnp.where(block == m, q_late, 0.0))
    scores = lax.dot_general(jnp.concatenate(q_parts, axis=1).astype(bf16), jnp.concatenate(k_parts, axis=1).astype(bf16),
                             (((1,), (1,)), ((), ())), preferred_element_type=f32)
    o = o + jnp.dot(jnp.where(same_sub, scores, 0.0).astype(bf16), v, preferred_element_type=f32)
    cum_s = cum.reshape(ns, n, d)
    tot = cum_s[:, n - 1:n, :]
    kt_t = (k.reshape(ns, n, d) * jnp.exp(tot - cum_s)).reshape(HG_TOKENS, d).T
    lane_sub = lax.broadcasted_iota(jnp.int32, (d, HG_TOKENS), 1) // n
    k_by_sub = jnp.concatenate([jnp.where(lane_sub == b, kt_t, 0.0) for b in range(ns)], axis=0).astype(bf16)
    update = jnp.dot(k_by_sub, v, preferred_element_type=f32)
    decay = jnp.exp(tot.reshape(ns, d)).T
    state = s0
    states = []
    for a in range(ns):
        states.append(state.astype(bf16))
        state = decay[:, a:a + 1] * state + update[a * d:(a + 1) * d]
    qt = q * jnp.exp(cum)
    row_sub = lax.broadcasted_iota(jnp.int32, (HG_TOKENS, d), 0) // n
    q_by_sub = jnp.concatenate([jnp.where(row_sub == a, qt, 0.0) for a in range(ns)], axis=1).astype(bf16)
    o = o + jnp.dot(q_by_sub, jnp.concatenate(states, axis=0), preferred_element_type=f32)
    on = o * lax.rsqrt(jnp.mean(o * o, axis=-1, keepdims=True) + EPS) * gn
    return on * (gp * jax.nn.sigmoid(gp)), state


def _head_parts(ref, h):
    return [ref[:, p * D + h * HG_DIM:p * D + (h + 1) * HG_DIM] for p in range(4)]


def hg_scan_fwd(proj, lb, gn, name):
    t = proj.shape[0]
    nt = t // HG_TOKENS

    def body(p_ref, lb_ref, gn_ref, y_ref, s_ref, state):
        @pl.when(pl.program_id(0) == 0)
        def _():
            state[...] = jnp.zeros_like(state)

        for h in range(HG_HEADS):
            cols = slice(h * HG_DIM, (h + 1) * HG_DIM)
            s_ref[h, 0] = state[h]
            y, s1 = _hg_block_fn(*_head_parts(p_ref, h), state[h], lb_ref[:, cols], gn_ref[:, cols])
            y_ref[:, cols] = y.astype(bf16)
            state[h] = s1

    return pl.pallas_call(
        body, name=name, grid=(nt,),
        in_specs=[pl.BlockSpec((HG_TOKENS, 4 * D), lambda i: (i, 0)), _row(D), _row(D)],
        out_specs=(pl.BlockSpec((HG_TOKENS, D), lambda i: (i, 0)),
                   pl.BlockSpec((HG_HEADS, 1, HG_DIM, HG_DIM), lambda i: (0, i, 0, 0))),
        out_shape=(jax.ShapeDtypeStruct((t, D), bf16), jax.ShapeDtypeStruct((HG_HEADS, nt, HG_DIM, HG_DIM), f32)),
        scratch_shapes=[pltpu.VMEM((HG_HEADS, HG_DIM, HG_DIM), f32)],
        compiler_params=_cparams("arbitrary"),
    )(proj, lb, gn)


def hg_scan_bwd(proj, lb, gn, states, dy, name, rider=None):
    t = proj.shape[0]
    nt = t // HG_TOKENS

    def body(p_ref, lb_ref, gn_ref, s_ref, dy_ref, dp_ref, dlb_ref, dgn_ref, dstate):
        @pl.when(pl.program_id(0) == 0)
        def _():
            dstate[...] = jnp.zeros_like(dstate)
            dlb_ref[...] = jnp.zeros_like(dlb_ref)
            dgn_ref[...] = jnp.zeros_like(dgn_ref)

        for h in range(HG_HEADS):
            cols = slice(h * HG_DIM, (h + 1) * HG_DIM)
            _, vjp = jax.vjp(_hg_block_fn, *_head_parts(p_ref, h), s_ref[h, 0], lb_ref[:, cols], gn_ref[:, cols])
            grads = vjp((dy_ref[:, cols].astype(f32), dstate[h]))
            for p in range(4):
                dp_ref[:, p * D + h * HG_DIM:p * D + (h + 1) * HG_DIM] = grads[p].astype(bf16)
            dstate[h] = grads[4]
            dlb_ref[:, cols] += grads[5]
            dgn_ref[:, cols] += grads[6]

    small = jax.ShapeDtypeStruct((1, D), f32)
    return _call(
        body, name=name, grid=(nt,),
        in_specs=[pl.BlockSpec((HG_TOKENS, 4 * D), lambda i: (nt - 1 - i, 0)), _row(D), _row(D),
                  pl.BlockSpec((HG_HEADS, 1, HG_DIM, HG_DIM), lambda i: (0, nt - 1 - i, 0, 0)),
                  pl.BlockSpec((HG_TOKENS, D), lambda i: (nt - 1 - i, 0))],
        out_specs=[pl.BlockSpec((HG_TOKENS, 4 * D), lambda i: (nt - 1 - i, 0)), _row(D), _row(D)],
        out_shape=[jax.ShapeDtypeStruct((t, 4 * D), bf16), small, small],
        scratch_shapes=[pltpu.VMEM((HG_HEADS, HG_DIM, HG_DIM), f32)],
        semantics=("arbitrary",), rider=rider,
    )(proj, lb, gn, states, dy)


FFN_COLS = 1408
HALO = 8
STRIP = 16


def _ffn_specs(tm):
    nb = tm // HALO
    main_g = pl.BlockSpec((tm, FFN_COLS), lambda j, i: (i, j))
    main_v = pl.BlockSpec((tm, FFN_COLS), lambda j, i: (i, j + 2))
    halo_g = pl.BlockSpec((HALO, FFN_COLS), lambda j, i: (jnp.maximum(i * nb - 1, 0), j))
    halo_v = pl.BlockSpec((HALO, FFN_COLS), lambda j, i: (jnp.maximum(i * nb - 1, 0), j + 2))
    w_g = pl.BlockSpec((3, FFN_COLS), lambda j, i: (0, j))
    w_v = pl.BlockSpec((3, FFN_COLS), lambda j, i: (0, j + 2))
    b_g = pl.BlockSpec((1, FFN_COLS), lambda j, i: (0, j))
    b_v = pl.BlockSpec((1, FFN_COLS), lambda j, i: (0, j + 2))
    return [main_g, halo_g, main_v, halo_v, w_g, w_v, b_g, b_v]


def _strip_rows(r):
    return pl.ds(r * STRIP, STRIP) if isinstance(r, int) else pl.ds(pl.multiple_of(r * STRIP, STRIP), STRIP)


def _for_strips(nstrip, strip, reverse=False):
    if reverse:
        strip(nstrip - 1, True)
        lax.fori_loop(0, nstrip - 1, lambda k, c: (strip(nstrip - 2 - k, False), c)[1], 0)
    else:
        strip(0, True)
        lax.fori_loop(1, nstrip, lambda r, c: (strip(r, False), c)[1], 0)


def _conv_strip(main_ref, halo_ref, w_ref, b_ref, r, edge, cols, rowi):
    cur = main_ref[_strip_rows(r), cols].astype(f32)
    if edge:
        h = jnp.where(pl.program_id(1) == 0, 0.0, halo_ref[:, cols].astype(f32))
        prev = jnp.concatenate([jnp.zeros_like(h), h], axis=0)
    else:
        prev = main_ref[_strip_rows(r - 1), cols].astype(f32)
    a1 = jnp.where(rowi < 1, pltpu.roll(prev, 1, axis=0), pltpu.roll(cur, 1, axis=0))
    a2 = jnp.where(rowi < 2, pltpu.roll(prev, 2, axis=0), pltpu.roll(cur, 2, axis=0))
    y = b_ref[:, cols] + w_ref[0:1, cols] * a2 + w_ref[1:2, cols] * a1 + w_ref[2:3, cols] * cur
    return y, (cur, a1, a2)


def ffn_gate_fwd(a, cw, cb, name, tm=512):
    t = a.shape[0]
    tm = _tile(t, tm, STRIP)

    def body(ag_ref, hg_ref, av_ref, hv_ref, wg_ref, wv_ref, bg_ref, bv_ref, o_ref):
        rowi = lax.broadcasted_iota(jnp.int32, (STRIP, LANES), 0)

        def strip(r, edge):
            for c in range(FFN_COLS // LANES):
                cols = pl.ds(c * LANES, LANES)
                yg, _ = _conv_strip(ag_ref, hg_ref, wg_ref, bg_ref, r, edge, cols, rowi)
                yv, _ = _conv_strip(av_ref, hv_ref, wv_ref, bv_ref, r, edge, cols, rowi)
                o_ref[_strip_rows(r), cols] = (_gelu(yg) * yv).astype(bf16)

        _for_strips(tm // STRIP, strip)

    return pl.pallas_call(
        body, name=name, grid=(2, t // tm), in_specs=_ffn_specs(tm),
        out_specs=pl.BlockSpec((tm, FFN_COLS), lambda j, i: (i, j)),
        out_shape=jax.ShapeDtypeStruct((t, FFN_HIDDEN), bf16),
        compiler_params=_cparams("parallel", "arbitrary"),
    )(a, a, a, a, cw, cw, cb, cb)


def ffn_gate_bwd(a, cw, cb, dhid, name, tm=512, rider=None):
    t = a.shape[0]
    tm = _tile(t, tm, STRIP)

    def body(ag_ref, hg_ref, av_ref, hv_ref, wg_ref, wv_ref, bg_ref, bv_ref, dh_ref,
             dy_ref, dwg_ref, dwv_ref, dbg_ref, dbv_ref, acc):
        rowi = lax.broadcasted_iota(jnp.int32, (STRIP, LANES), 0)

        @pl.when(pl.program_id(1) == 0)
        def _():
            acc[...] = jnp.zeros_like(acc)

        def strip(r, edge):
            rows = _strip_rows(r)
            for c in range(FFN_COLS // LANES):
                cols = pl.ds(c * LANES, LANES)
                yg, taps_g = _conv_strip(ag_ref, hg_ref, wg_ref, bg_ref, r, edge, cols, rowi)
                yv, taps_v = _conv_strip(av_ref, hv_ref, wv_ref, bv_ref, r, edge, cols, rowi)
                dh = dh_ref[rows, cols].astype(f32)
                cdf = 0.5 * (1.0 + lax.erf(yg * (1.0 / math.sqrt(2.0))))
                pdf = jnp.exp(-0.5 * yg * yg) * (1.0 / math.sqrt(2.0 * math.pi))
                dyg = dh * yv * (cdf + yg * pdf)
                dyv = dh * (yg * cdf)
                dy_ref[0, rows, cols] = dyg.astype(bf16)
                dy_ref[1, rows, cols] = dyv.astype(bf16)
                for p, (dy, (a0, a1, a2)) in enumerate(((dyg, taps_g), (dyv, taps_v))):
                    acc[4 * p + 0, :, cols] += dy * a2
                    acc[4 * p + 1, :, cols] += dy * a1
                    acc[4 * p + 2, :, cols] += dy * a0
                    acc[4 * p + 3, :, cols] += dy

        _for_strips(tm // STRIP, strip)

        @pl.when(pl.program_id(1) == pl.num_programs(1) - 1)
        def _():
            for p, (dw_ref, db_ref) in enumerate(((dwg_ref, dbg_ref), (dwv_ref, dbv_ref))):
                for tap in range(3):
                    dw_ref[tap:tap + 1, :] = jnp.sum(acc[4 * p + tap], axis=0, keepdims=True)
                db_ref[...] = jnp.sum(acc[4 * p + 3], axis=0, keepdims=True)

    half_w = pl.BlockSpec((3, FFN_COLS), lambda j, i: (0, j))
    half_b = pl.BlockSpec((1, FFN_COLS), lambda j, i: (0, j))
    return _call(
        body, name=name, grid=(2, t // tm),
        in_specs=_ffn_specs(tm) + [pl.BlockSpec((tm, FFN_COLS), lambda j, i: (i, j))],
        out_specs=[pl.BlockSpec((2, tm, FFN_COLS), lambda j, i: (0, i, j)), half_w, half_w, half_b, half_b],
        out_shape=[jax.ShapeDtypeStruct((2, t, FFN_HIDDEN), bf16),
                   jax.ShapeDtypeStruct((3, FFN_HIDDEN), f32), jax.ShapeDtypeStruct((3, FFN_HIDDEN), f32),
                   jax.ShapeDtypeStruct((1, FFN_HIDDEN), f32), jax.ShapeDtypeStruct((1, FFN_HIDDEN), f32)],
        scratch_shapes=[pltpu.VMEM((8, STRIP, FFN_COLS), f32)],
        semantics=("parallel", "arbitrary"), rider=rider,
    )(a, a, a, a, cw, cw, cb, cb, dhid)


def conv_transpose(dy, cw, name, tm=512):
    _, t, fh = dy.shape
    tm = _tile(t, tm, STRIP)
    nb = tm // HALO
    last_halo = t // HALO - 1
    ncol = fh // FFN_COLS

    def body(main_ref, halo_ref, w_ref, o_ref):
        rowi = lax.broadcasted_iota(jnp.int32, (STRIP, LANES), 0)
        last_block = pl.program_id(2) == pl.num_programs(2) - 1

        def strip(r, edge):
            rows = _strip_rows(r)
            for c in range(FFN_COLS // LANES):
                cols = pl.ds(c * LANES, LANES)
                cur = main_ref[0, rows, cols].astype(f32)
                if edge:
                    h = jnp.where(last_block, 0.0, halo_ref[0, :, cols].astype(f32))
                    nxt = jnp.concatenate([h, jnp.zeros_like(h)], axis=0)
                else:
                    nxt = main_ref[0, _strip_rows(r + 1), cols].astype(f32)
                d1 = jnp.where(rowi >= STRIP - 1, pltpu.roll(nxt, STRIP - 1, axis=0), pltpu.roll(cur, STRIP - 1, axis=0))
                d2 = jnp.where(rowi >= STRIP - 2, pltpu.roll(nxt, STRIP - 2, axis=0), pltpu.roll(cur, STRIP - 2, axis=0))
                o_ref[rows, cols] = (w_ref[2:3, cols] * cur + w_ref[1:2, cols] * d1 + w_ref[0:1, cols] * d2).astype(bf16)

        _for_strips(tm // STRIP, strip, reverse=True)

    return pl.pallas_call(
        body, name=name, grid=(2, ncol, t // tm),
        in_specs=[pl.BlockSpec((1, tm, FFN_COLS), lambda p, j, i: (p, i, j)),
                  pl.BlockSpec((1, HALO, FFN_COLS), lambda p, j, i: (p, jnp.minimum((i + 1) * nb, last_halo), j)),
                  pl.BlockSpec((3, FFN_COLS), lambda p, j, i: (0, p * ncol + j))],
        out_specs=pl.BlockSpec((tm, FFN_COLS), lambda p, j, i: (i, p * ncol + j)),
        out_shape=jax.ShapeDtypeStruct((t, 2 * fh), bf16),
        compiler_params=_cparams("parallel", "parallel", "arbitrary"),
    )(dy, dy, cw)


def ada_mod(c_all, ada_w, ada_b_cols, name):
    cols = ada_w.shape[2]

    def body(c_ref, w_ref, b_ref, o_ref):
        c = c_ref[...]
        cond = (c * jax.nn.sigmoid(c)).astype(bf16)
        o_ref[0] = jnp.dot(cond, w_ref[0].astype(bf16), preferred_element_type=f32) + b_ref[0]

    return pl.pallas_call(
        body, name=name, grid=(DEPTH,),
        in_specs=[pl.BlockSpec((N_DEV, D), lambda i: (0, 0)), pl.BlockSpec((1, D, cols), lambda i: (i, 0, 0)),
                  pl.BlockSpec((1, 1, cols), lambda i: (i, 0, 0))],
        out_specs=pl.BlockSpec((1, N_DEV, cols), lambda i: (i, 0, 0)),
        out_shape=jax.ShapeDtypeStruct((DEPTH, N_DEV, cols), f32), compiler_params=_cparams("parallel"),
    )(c_all, ada_w, ada_b_cols)


def ada_grads(c_all, dmod_cols, dmod_all, name):
    cols = dmod_cols.shape[2]

    def body(c_ref, dm_ref, da_ref, dw_ref, db_ref):
        c = c_ref[...]
        cond = c * jax.nn.sigmoid(c)
        dw_ref[0] = lax.dot_general(cond, dm_ref[0], (((0,), (0,)), ((), ())), precision=lax.Precision.HIGHEST,
                                    preferred_element_type=f32)
        acc = da_ref[0, 0]
        for e in range(1, N_DEV):
            acc = acc + da_ref[e, 0]
        db_ref[0] = acc

    return pl.pallas_call(
        body, name=name, grid=(DEPTH,),
        in_specs=[pl.BlockSpec((N_DEV, D), lambda i: (0, 0)), pl.BlockSpec((1, N_DEV, cols), lambda i: (i, 0, 0)),
                  pl.BlockSpec((N_DEV, 1, 1, 6 * D), lambda i: (0, i, 0, 0))],
        out_specs=(pl.BlockSpec((1, D, cols), lambda i: (i, 0, 0)), pl.BlockSpec((1, 1, 6 * D), lambda i: (i, 0, 0))),
        out_shape=(jax.ShapeDtypeStruct((DEPTH, D, cols), f32), jax.ShapeDtypeStruct((DEPTH, 1, 6 * D), f32)),
        compiler_params=_cparams("parallel"),
    )(c_all, dmod_cols, dmod_all)


def lower_bound_fwd(hg_lb, name):
    n = hg_lb.shape[1]

    def body(l_ref, o_ref):
        o_ref[...] = jax.nn.sigmoid(l_ref[1:2, :] - l_ref[0:1, :])

    return pl.pallas_call(body, name=name, out_shape=jax.ShapeDtypeStruct((1, n), f32))(hg_lb)


def lower_bound_bwd(hg_lb, dlb, name):
    n = hg_lb.shape[1]

    def body(l_ref, d_ref, o_ref):
        p = jax.nn.sigmoid(l_ref[1:2, :] - l_ref[0:1, :])
        g = d_ref[...] * p * (1.0 - p)
        o_ref[0:1, :] = -g
        o_ref[1:2, :] = g

    return pl.pallas_call(body, name=name, out_shape=jax.ShapeDtypeStruct((2, n), f32))(hg_lb, dlb)


def _adamw(w, g, m, v):
    m = ADAM_B1 * m + (1.0 - ADAM_B1) * g
    v = ADAM_B2 * v + (1.0 - ADAM_B2) * jnp.square(g)
    m_hat = m / (1.0 - ADAM_B1 ** ADAM_STEP)
    v_hat = v / (1.0 - ADAM_B2 ** ADAM_STEP)
    delta = -ADAM_LR * (m_hat / (jnp.sqrt(v_hat) + ADAM_EPS) + ADAM_WD * w)
    return delta, m, v


def adam_reduced(parts, w, m, v, name, tr=128):
    layers = len(parts)
    _, r, c = parts[0].shape
    tr = _tile(r, tr, 16)
    steps = r // tr

    def body(*refs):
        p_refs = refs[:layers]
        w_ref, m_ref, v_ref, g_ref, d_ref, mo_ref, vo_ref = refs[layers:]
        for layer in range(layers):
            @pl.when(pl.program_id(0) == layer)
            def _():
                g = p_refs[layer][0].astype(f32)
                for j in range(1, N_DEV):
                    g = g + p_refs[layer][j].astype(f32)
                g_ref[...] = g
                d_ref[...], mo_ref[...], vo_ref[...] = _adamw(w_ref[...], g, m_ref[...], v_ref[...])

    def part_spec(layer):
        return pl.BlockSpec((N_DEV, tr, c), lambda l, i: (0, jnp.where(l == layer, i, 0), 0))

    blk = pl.BlockSpec((tr, c), lambda l, i: (l * steps + i, 0))
    out = jax.ShapeDtypeStruct((layers * r, c), f32)
    return pl.pallas_call(
        body, name=name, grid=(layers, steps),
        in_specs=[part_spec(layer) for layer in range(layers)] + [blk, blk, blk],
        out_specs=(blk, blk, blk, blk), out_shape=(out, out, out, out), compiler_params=_cparams("arbitrary", "arbitrary"),
    )(*parts, w, m, v)


def adam_plain(g, w, m, v, name, tr=256):
    r, c = w.shape
    tr = _tile(r, tr, 8)

    def body(g_ref, w_ref, m_ref, v_ref, d_ref, mo_ref, vo_ref):
        d_ref[...], mo_ref[...], vo_ref[...] = _adamw(w_ref[...], g_ref[...], m_ref[...], v_ref[...])

    blk = pl.BlockSpec((tr, c), lambda i: (i, 0))
    out = jax.ShapeDtypeStruct((r, c), f32)
    return pl.pallas_call(
        body, name=name, grid=(r // tr,), in_specs=[blk, blk, blk, blk], out_specs=(blk, blk, blk),
        out_shape=(out, out, out), compiler_params=_cparams("parallel"),
    )(g, w, m, v)


def sum_parts(parts, name):
    _, r, c = parts.shape

    def body(p_ref, o_ref):
        acc = p_ref[0]
        for j in range(1, N_DEV):
            acc = acc + p_ref[j]
        o_ref[...] = acc

    return pl.pallas_call(body, name=name, out_shape=jax.ShapeDtypeStruct((r, c), f32))(parts)


def _pack(arrs, rows_mult=8):
    flat = jnp.concatenate([a.reshape(-1) for a in arrs])
    rows = -(-flat.shape[0] // LANES)
    rows = -(-rows // rows_mult) * rows_mult
    return jnp.pad(flat, (0, rows * LANES - flat.shape[0])).reshape(rows, LANES)


def _unpack(flat, shapes):
    out, at = [], 0
    for s in shapes:
        n = math.prod(s)
        out.append(flat[at:at + n].reshape(s))
        at += n
    return out


def kernel(x, c, gm_w_in, gm_ln_g, gm_ln_b, gm_w_s, gm_b_s, gm_w_out, hg_w_in, hg_lb, hg_gn_g, hg_w_out, ffn_w_up, ffn_conv_w, ffn_conv_b, ffn_w_down, norm_g, ada_w, ada_b, final_g, loss_target, m_gm_w_in, m_gm_ln_g, m_gm_ln_b, m_gm_w_s, m_gm_b_s, m_gm_w_out, m_hg_w_in, m_hg_lb, m_hg_gn_g, m_hg_w_out, m_ffn_w_up, m_ffn_conv_w, m_ffn_conv_b, m_ffn_w_down, m_norm_g, m_ada_w, m_ada_b, m_final_g, v_gm_w_in, v_gm_ln_g, v_gm_ln_b, v_gm_w_s, v_gm_b_s, v_gm_w_out, v_hg_w_in, v_hg_lb, v_hg_gn_g, v_hg_w_out, v_ffn_w_up, v_ffn_conv_w, v_ffn_conv_b, v_ffn_w_down, v_norm_g, v_ada_w, v_ada_b, v_final_g):
    me = _flat(_mesh_pos())
    xt = x[0]
    t = xt.shape[0]

    small_shapes = [(1, D), (2, HG_DIM), (2, HG_DIM), (DEPTH, 2, HG_DIM), (DEPTH, 3, 2 * FFN_HIDDEN // N_DEV)]
    (small_all,) = all_gather([_pack([c, hg_lb, hg_gn_g, norm_g, ffn_conv_w])], "gather_small")
    small_all = small_all.reshape(N_DEV, -1)
    at = 0
    pieces = []
    for s in small_shapes:
        n = math.prod(s)
        pieces.append(small_all[:, at:at + n].reshape((N_DEV,) + s))
        at += n
    c_all = pieces[0].reshape(N_DEV, D)
    hg_lb_full = jnp.transpose(pieces[1], (1, 0, 2)).reshape(2, D)
    hg_gn_full = jnp.transpose(pieces[2], (1, 0, 2)).reshape(2, D)
    norm_g_full = jnp.transpose(pieces[3], (1, 2, 0, 3)).reshape(DEPTH, 2, D)
    conv_w_full = jnp.transpose(pieces[4], (1, 2, 0, 3)).reshape(DEPTH, 3, 2 * FFN_HIDDEN)

    lb1 = lower_bound_fwd(hg_lb_full, "lower_bound")
    lbs = [jnp.zeros((1, D), f32), lb1]

    ada_b_cols = lax.dynamic_slice(ada_b, (0, me * ADA_COLS), (DEPTH, ADA_COLS)).reshape(DEPTH, 1, ADA_COLS)
    mod_cols = ada_mod(c_all, ada_w, ada_b_cols, "ada_mod")
    (mod_mine,) = all_to_all([jnp.transpose(mod_cols, (1, 0, 2))], "mod_to_examples")
    mod = jnp.transpose(mod_mine, (1, 0, 2)).reshape(DEPTH, 6, 1, D)

    def layer_shards(i):
        j = i // 2
        w_in, w_out = (gm_w_in, gm_w_out) if i % 2 == 0 else (hg_w_in, hg_w_out)
        return [w_in[j].astype(bf16), w_out[j].astype(bf16), ffn_w_up[i].astype(bf16), ffn_w_down[i].astype(bf16)]

    def full_cols(g):
        return jnp.transpose(g, (1, 0, 2)).reshape(g.shape[1], N_DEV * g.shape[2])

    def full_rows(g):
        return g.reshape(N_DEV * g.shape[1], g.shape[2])

    gathered = list(all_gather(layer_shards(0), "gather_weights_0", relay=True))

    saved = []
    weights = []
    xcur = xt
    for i in range(DEPTH):
        j = i // 2
        w_in, w_out, w_up, w_down = full_cols(gathered[0]), full_rows(gathered[1]), full_cols(gathered[2]), full_rows(gathered[3])
        weights.append((w_in, w_out, w_up, w_down))
        riders = [Rider(a, True) for a in layer_shards(i + 1)] if i + 1 < DEPTH else [None] * 4
        gathered = [None] * 4

        def ride(outs, slot):
            if riders[slot] is None:
                return outs
            gathered[slot] = outs[-1]
            return outs[:-1]

        sh1, sc1, g1, sh2, sc2, g2 = [mod[i, p] for p in range(6)]
        gn1, gn2 = norm_g_full[i, 0:1], norm_g_full[i, 1:2]
        s = {"x0": xcur}
        h = norm_fwd(xcur, gn1, sc1, sh1, f"norm1_{i}")
        s["h"] = h
        if i % 2 == 0:
            (z,) = ride(mm_nn(h, w_in, bf16, f"gm_in_{i}", rider=riders[0]), 0)
            bs = gm_b_s[j].reshape(GM_HEADS, GM_BLOCK, 1)
            mixed = gm_mix_fwd(z, gm_w_s[j], bs, gm_ln_g[j:j + 1], gm_ln_b[j:j + 1], f"gm_mix_{i}")
            s["z"] = z
        else:
            (proj,) = ride(mm_nn(h, w_in, f32, f"hg_in_{i}", rider=riders[0]), 0)
            mixed, states = hg_scan_fwd(proj, lbs[j], hg_gn_full[j:j + 1], f"hg_scan_{i}")
            s["proj"], s["states"] = proj, states
        s["mixed"] = mixed
        y, x1 = ride(mm_nn_residual(mixed, w_out, xcur, g1, f"mix_out_{i}", rider=riders[1]), 1)
        s["y"], s["x1"] = y, x1
        h2 = norm_fwd(x1, gn2, sc2, sh2, f"norm2_{i}")
        (a,) = ride(mm_nn(h2, w_up, bf16, f"ffn_up_{i}", rider=riders[2]), 2)
        hid = ffn_gate_fwd(a, conv_w_full[i], ffn_conv_b[i:i + 1], f"ffn_gate_{i}")
        fo, x2 = ride(mm_nn_residual(hid, w_down, x1, g2, f"ffn_down_{i}", rider=riders[3]), 3)
        s["h2"], s["a"], s["hid"], s["f"] = h2, a, hid, fo
        saved.append(s)
        xcur = x2

    loss_part, dx, d_final_g, dg2, df = loss_head(xcur, final_g.reshape(1, D), loss_target[0], saved[-1]["f"],
                                                  mod[DEPTH - 1, 5], "loss_head")
    loss = lax.psum(loss_part[0, 0], ("x", "y", "c"))

    def blocked_cols(dw):
        k, n = dw.shape
        return jnp.transpose(dw.reshape(k, N_DEV, n // N_DEV), (1, 0, 2))

    def blocked_rows(dw):
        k, n = dw.shape
        return dw.reshape(N_DEV, k // N_DEV, n)

    received = [[None] * 4 for _ in range(DEPTH)]
    pending = None

    def take(outs, where):
        if where is None:
            return outs
        received[where[0]][where[1]] = outs[-1]
        return outs[:-1]

    dmod = [None] * DEPTH
    d_norm_g = [None] * DEPTH
    d_gm = {k: [None, None] for k in ("ws", "bs", "lng", "lnb")}
    d_hg = {k: [None, None] for k in ("lb", "gn")}
    d_ffn = {k: [None] * DEPTH for k in ("cw", "cb")}
    for i in reversed(range(DEPTH)):
        j = i // 2
        s = saved[i]
        w_in, w_out, w_up, w_down = weights[i]
        sh1, sc1, g1, sh2, sc2, g2 = [mod[i, p] for p in range(6)]
        gn1, gn2 = norm_g_full[i, 0:1], norm_g_full[i, 1:2]
        (dw_down,) = mm_tn(s["hid"], df, bf16, f"dw_down_{i}")
        (dhid,) = take(mm_nt(df, w_down, bf16, f"dhid_{i}", rider=Rider(blocked_rows(dw_down), False)), (i, 3))
        outs = ffn_gate_bwd(s["a"], conv_w_full[i], ffn_conv_b[i:i + 1], dhid, f"ffn_gate_bwd_{i}",
                            rider=None if pending is None else Rider(pending[0], False))
        dyc, dwg, dwv, dbg, dbv = take(outs, None if pending is None else pending[1])
        d_ffn["cw"][i] = jnp.concatenate([dwg, dwv], axis=1)
        d_ffn["cb"][i] = jnp.concatenate([dbg, dbv], axis=1)
        da = conv_transpose(dyc, conv_w_full[i], f"conv_t_{i}")
        (dw_up,) = mm_tn(s["h2"], da, bf16, f"dw_up_{i}")
        (dh2,) = mm_nt(da, w_up, bf16, f"dh2_{i}")
        dx1, dgn2, dsc2, dsh2, dg1, dy = norm_bwd(s["x1"], gn2, sc2, sh2, dh2, dx, f"norm2_bwd_{i}", gate=(s["y"], g1))
        (dw_out,) = mm_tn(s["mixed"], dy, bf16, f"dw_mix_out_{i}")
        (dmixed,) = take(mm_nt(dy, w_out, bf16, f"dmixed_{i}", rider=Rider(blocked_rows(dw_out), False)), (i, 1))
        up_rider = Rider(blocked_cols(dw_up), False)
        if i % 2 == 0:
            bs = gm_b_s[j].reshape(GM_HEADS, GM_BLOCK, 1)
            dpre, dws, dbs, dlng, dlnb = take(
                gm_mix_bwd(s["z"], gm_w_s[j], bs, gm_ln_g[j:j + 1], gm_ln_b[j:j + 1], dmixed, f"gm_mix_bwd_{i}",
                           rider=up_rider), (i, 2))
            d_gm["ws"][j], d_gm["bs"][j], d_gm["lng"][j], d_gm["lnb"][j] = dws, dbs.reshape(GM_HEADS, GM_BLOCK), dlng, dlnb
        else:
            dpre, dlb, dgn = take(
                hg_scan_bwd(s["proj"], lbs[j], hg_gn_full[j:j + 1], s["states"], dmixed, f"hg_scan_bwd_{i}",
                            rider=up_rider), (i, 2))
            d_hg["lb"][j], d_hg["gn"][j] = dlb, dgn
        (dw_in,) = mm_tn(s["h"], dpre, bf16, f"dw_mix_in_{i}")
        pending = (blocked_cols(dw_in), (i, 0))
        (dh,) = mm_nt(dpre, w_in, bf16, f"dh_mix_{i}")
        dmod_i = [None, None, dg1, dsh2, dsc2, dg2]
        if i > 0:
            dx, dgn1, dsc1, dsh1, dg2, df = norm_bwd(s["x0"], gn1, sc1, sh1, dh, dx1, f"norm1_bwd_{i}",
                                                     gate=(saved[i - 1]["f"], mod[i - 1, 5]))
        else:
            dx, dgn1, dsc1, dsh1 = norm_bwd(s["x0"], gn1, sc1, sh1, dh, dx1, f"norm1_bwd_{i}")
        dmod_i[0], dmod_i[1] = dsh1, dsc1
        dmod[i] = jnp.concatenate(dmod_i, axis=1)
        d_norm_g[i] = jnp.concatenate([dgn1, dgn2], axis=0)
    grad_x = dx.reshape(1, t, D)
    (received[0][0],) = all_to_all([pending[0]], "weight_grads_last")

    (dmod_all,) = all_gather([jnp.concatenate(dmod, axis=0)], "gather_dmod")
    dmod_cols = jnp.transpose(lax.dynamic_slice(dmod_all, (0, 0, me * ADA_COLS), (N_DEV, DEPTH, ADA_COLS)), (1, 0, 2))
    g_ada_w, g_ada_b = ada_grads(c_all, dmod_cols, dmod_all.reshape(N_DEV, DEPTH, 1, 6 * D), "ada_grads")
    g_ada_b = g_ada_b.reshape(DEPTH, 6 * D)

    small_partials = [jnp.concatenate(d_gm["lng"], axis=0), jnp.concatenate(d_gm["lnb"], axis=0),
                      jnp.stack(d_gm["ws"]), jnp.stack(d_gm["bs"]), jnp.concatenate(d_ffn["cb"], axis=0),
                      d_final_g, d_hg["lb"][1], jnp.concatenate(d_hg["gn"], axis=0), jnp.stack(d_norm_g),
                      jnp.stack(d_ffn["cw"])]
    partial_shapes = [p.shape for p in small_partials]
    packed = _pack(small_partials, rows_mult=8 * N_DEV)
    rows = packed.shape[0] // N_DEV
    (recv,) = all_to_all([packed.reshape(N_DEV, rows, LANES)], "small_grads_exchange")
    (summed,) = all_gather([sum_parts(recv, "small_grads_sum")], "small_grads_gather")
    g_ln_g, g_ln_b, g_ws, g_bs, g_cb, g_final, g_lb1, g_gn, g_norm, g_cw = _unpack(summed.reshape(-1), partial_shapes)
    g_final = g_final.reshape(D)

    def my_cols(a, n):
        start = (0,) * (a.ndim - 1) + (me * n,)
        return lax.dynamic_slice(a, start, a.shape[:-1] + (n,))

    g_hg_lb = lower_bound_bwd(hg_lb, my_cols(g_lb1, HG_DIM), "lower_bound_bwd")
    g_hg_gn = my_cols(g_gn, HG_DIM)
    g_norm_g = my_cols(g_norm, HG_DIM)
    g_conv_w = my_cols(g_cw, 2 * FFN_HIDDEN // N_DEV)

    def parts_of(slot, layers):
        return [received[i][slot] for i in layers]

    w_shards = [gm_w_in, gm_w_out, hg_w_in, hg_w_out, ffn_w_up, ffn_w_down]
    big_parts = [parts_of(0, (0, 2)), parts_of(1, (0, 2)), parts_of(0, (1, 3)), parts_of(1, (1, 3)),
                 parts_of(2, range(DEPTH)), parts_of(3, range(DEPTH))]
    big_m = [m_gm_w_in, m_gm_w_out, m_hg_w_in, m_hg_w_out, m_ffn_w_up, m_ffn_w_down]
    big_v = [v_gm_w_in, v_gm_w_out, v_hg_w_in, v_hg_w_out, v_ffn_w_up, v_ffn_w_down]
    big = []
    for idx, (w, m_, v_, parts) in enumerate(zip(w_shards, big_m, big_v, big_parts)):
        two_d = (-1, w.shape[-1])
        outs = adam_reduced(parts, w.reshape(two_d), m_.reshape(two_d), v_.reshape(two_d), f"adam_big_{idx}")
        big.append([o.reshape(w.shape) for o in outs])
    (g_gm_w_in, d_gm_w_in, nm_gm_w_in, nv_gm_w_in), (g_gm_w_out, d_gm_w_out, nm_gm_w_out, nv_gm_w_out), \
        (g_hg_w_in, d_hg_w_in, nm_hg_w_in, nv_hg_w_in), (g_hg_w_out, d_hg_w_out, nm_hg_w_out, nv_hg_w_out), \
        (g_ffn_w_up, d_ffn_w_up, nm_ffn_w_up, nv_ffn_w_up), (g_ffn_w_down, d_ffn_w_down, nm_ffn_w_down, nv_ffn_w_down) = big

    two_d = (-1, ADA_COLS)
    d_ada_w, nm_ada_w, nv_ada_w = [o.reshape(ada_w.shape) for o in adam_plain(
        g_ada_w.reshape(two_d), ada_w.reshape(two_d), m_ada_w.reshape(two_d), v_ada_w.reshape(two_d), "adam_ada_w")]

    small_g = [g_ln_g, g_ln_b, g_ws, g_bs, g_cb, g_ada_b, g_final, g_hg_lb, g_hg_gn, g_norm_g, g_conv_w]
    small_w = [gm_ln_g, gm_ln_b, gm_w_s, gm_b_s, ffn_conv_b, ada_b, final_g, hg_lb, hg_gn_g, norm_g, ffn_conv_w]
    small_m = [m_gm_ln_g, m_gm_ln_b, m_gm_w_s, m_gm_b_s, m_ffn_conv_b, m_ada_b, m_final_g, m_hg_lb, m_hg_gn_g, m_norm_g, m_ffn_conv_w]
    small_v = [v_gm_ln_g, v_gm_ln_b, v_gm_w_s, v_gm_b_s, v_ffn_conv_b, v_ada_b, v_final_g, v_hg_lb, v_hg_gn_g, v_norm_g, v_ffn_conv_w]
    shapes = [w.shape for w in small_w]
    small_g = [g.reshape(s) for g, s in zip(small_g, shapes)]
    outs = adam_plain(_pack(small_g), _pack(small_w), _pack(small_m), _pack(small_v), "adam_small")
    (d_ln_g, d_ln_b, d_ws, d_bs, d_cb, d_ada_b, d_final, d_hg_lb, d_hg_gn, d_norm_g_, d_conv_w), \
        (nm_ln_g, nm_ln_b, nm_ws, nm_bs, nm_cb, nm_ada_b, nm_final, nm_hg_lb, nm_hg_gn, nm_norm_g, nm_conv_w), \
        (nv_ln_g, nv_ln_b, nv_ws, nv_bs, nv_cb, nv_ada_b, nv_final, nv_hg_lb, nv_hg_gn, nv_norm_g, nv_conv_w) = [
            _unpack(o.reshape(-1), shapes) for o in outs]
    g_ln_g, g_ln_b, g_ws, g_bs, g_cb, g_ada_b, g_final, g_hg_lb, g_hg_gn, g_norm_g, g_conv_w = small_g

    grads = (g_gm_w_in, g_ln_g, g_ln_b, g_ws, g_bs, g_gm_w_out, g_hg_w_in, g_hg_lb, g_hg_gn, g_hg_w_out,
             g_ffn_w_up, g_conv_w, g_cb, g_ffn_w_down, g_norm_g, g_ada_w, g_ada_b, g_final)
    deltas = (d_gm_w_in, d_ln_g, d_ln_b, d_ws, d_bs, d_gm_w_out, d_hg_w_in, d_hg_lb, d_hg_gn, d_hg_w_out,
              d_ffn_w_up, d_conv_w, d_cb, d_ffn_w_down, d_norm_g_, d_ada_w, d_ada_b, d_final)
    new_m = (nm_gm_w_in, nm_ln_g, nm_ln_b, nm_ws, nm_bs, nm_gm_w_out, nm_hg_w_in, nm_hg_lb, nm_hg_gn, nm_hg_w_out,
             nm_ffn_w_up, nm_conv_w, nm_cb, nm_ffn_w_down, nm_norm_g, nm_ada_w, nm_ada_b, nm_final)
    new_v = (nv_gm_w_in, nv_ln_g, nv_ln_b, nv_ws, nv_bs, nv_gm_w_out, nv_hg_w_in, nv_hg_lb, nv_hg_gn, nv_hg_w_out,
             nv_ffn_w_up, nv_conv_w, nv_cb, nv_ffn_w_down, nv_norm_g, nv_ada_w, nv_ada_b, nv_final)
    return (loss, grad_x) + grads + deltas + new_m + new_v
```

```python
import functools
import math

import jax
import jax.numpy as jnp
from jax import lax
from jax.experimental import pallas as pl
from jax.experimental.pallas import tpu as pltpu

f32 = jnp.float32
bf16 = jnp.bfloat16
MESH = pl.DeviceIdType.MESH

N_DEV = 8
D = 1024
DEPTH = 4
EPS = 1e-6
GM_WIDTH = 2048
GM_HEADS = 8
GM_HEAD_DIM = 256
GM_BLOCK = 128
CHUNK = 64
HG_HEADS = 8
HG_DIM = 128
FFN_HIDDEN = 2816
ADA_COLS = 6 * D // N_DEV

HG_SUB = 32
HG_PAIR = 8
HG_TOKENS = 128

ADAM_LR = 0.001
ADAM_B1 = 0.9
ADAM_B2 = 0.999
ADAM_EPS = 1e-08
ADAM_WD = 0.01
ADAM_STEP = 10

V7X_VMEM_LIMIT = 56 * 1024 * 1024
LANES = 128


def _cparams(*sem):
    return pltpu.CompilerParams(dimension_semantics=sem or None, vmem_limit_bytes=V7X_VMEM_LIMIT)


def _tile(n, target, mult=LANES):
    best = None
    for t in range(mult, min(n, target) + 1, mult):
        if n % t == 0:
            best = t
    return best or n


WEIGHT_BLOCK_BYTES = 6 * 1024 * 1024


def _weight_tile(n, k):
    return _tile(n, max(LANES, WEIGHT_BLOCK_BYTES // (2 * k)))


def _gelu(x):
    return 0.5 * x * (1.0 + lax.erf(x * (1.0 / math.sqrt(2.0))))


def _mesh_pos():
    return lax.axis_index("x"), lax.axis_index("y"), lax.axis_index("c")


def _flat(pos):
    return 4 * pos[0] + 2 * pos[1] + pos[2]


def _peer(pos, k):
    return ((1 - pos[0]) if k & 4 else pos[0], (1 - pos[1]) if k & 2 else pos[1], (1 - pos[2]) if k & 1 else pos[2])


def _exchange_copies(ins, outs, send_sems, recv_sems, local_sems, gather):
    pos = _mesh_pos()
    me = _flat(pos)

    def src(i, dest):
        return ins[i] if gather else ins[i].at[dest]

    local = [pltpu.make_async_copy(src(i, me), outs[i].at[me], local_sems.at[i]) for i in range(len(ins))]
    sends, recvs = [], []
    for k in range(1, N_DEV):
        peer = _peer(pos, k)
        there = _flat(peer)
        for i in range(len(ins)):
            sems = dict(send_sem=send_sems.at[i * 7 + k - 1], recv_sem=recv_sems.at[i * 7 + k - 1],
                        device_id=peer, device_id_type=MESH)
            sends.append(pltpu.make_async_remote_copy(src_ref=src(i, there), dst_ref=outs[i].at[me], **sems))
            recvs.append(pltpu.make_async_remote_copy(src_ref=src(i, there), dst_ref=outs[i].at[there], **sems))
    return local, sends, recvs


def _exchange_start(*refs):
    local, sends, _ = _exchange_copies(*refs)
    for cp in local + sends:
        cp.start()


def _exchange_wait(*refs):
    local, sends, recvs = _exchange_copies(*refs)
    for cp in recvs:
        cp.wait_recv()
    for cp in sends:
        cp.wait_send()
    for cp in local:
        cp.wait()


OTHER_CHIPS = (2, 4, 6)


def _relay_copies(ins, outs, send_sems, recv_sems, local_sems):
    pos = _mesh_pos()
    me = _flat(pos)
    sibling = _peer(pos, 1)
    local = [pltpu.make_async_copy(ins[i], outs[i].at[me], local_sems.at[i]) for i in range(len(ins))]
    first, passes, recvs = [], {k: [] for k in OTHER_CHIPS}, {k: [] for k in range(1, N_DEV)}
    for i in range(len(ins)):
        def copy(k, src, block, to):
            return pltpu.make_async_remote_copy(
                src_ref=src, dst_ref=outs[i].at[block], send_sem=send_sems.at[i * 7 + k - 1],
                recv_sem=recv_sems.at[i * 7 + k - 1], device_id=to, device_id_type=MESH)

        for k in (1,) + OTHER_CHIPS:
            first.append(copy(k, ins[i], me, _peer(pos, k)))
        for k in OTHER_CHIPS:
            there = _flat(_peer(pos, k))
            passes[k].append(copy(k ^ 1, outs[i].at[there], there, sibling))
        for k in range(1, N_DEV):
            there = _flat(_peer(pos, k))
            recvs[k].append(copy(k, ins[i], there, _peer(pos, k)))
    return local, first, passes, recvs


def _relay_start(ins, outs, *sems):
    local, first, _, _ = _relay_copies(ins, outs, *sems)
    for cp in local + first:
        cp.start()


def _relay_wait(ins, outs, *sems):
    local, first, passes, recvs = _relay_copies(ins, outs, *sems)
    for k in OTHER_CHIPS:
        for cp in recvs[k]:
            cp.wait_recv()
        for cp in passes[k]:
            cp.start()
    for k in (1, 3, 5, 7):
        for cp in recvs[k]:
            cp.wait_recv()
    for cp in first + [cp for k in OTHER_CHIPS for cp in passes[k]]:
        cp.wait_send()
    for cp in local:
        cp.wait()


def _exchange_out_shape(a, gather):
    return jax.ShapeDtypeStruct((N_DEV,) + tuple(a.shape) if gather else tuple(a.shape), a.dtype)


def _exchange_sems(n):
    return [pltpu.SemaphoreType.DMA((7 * n,)), pltpu.SemaphoreType.DMA((7 * n,)), pltpu.SemaphoreType.DMA((n,))]


ANY = pl.BlockSpec(memory_space=pl.ANY)


def _exchange(arrs, gather, name, relay=False):
    n = len(arrs)

    def body(*refs):
        ins, outs = refs[:n], refs[n:2 * n]
        if relay:
            _relay_start(ins, outs, *refs[2 * n:])
            _relay_wait(ins, outs, *refs[2 * n:])
        else:
            _exchange_start(ins, outs, *refs[2 * n:], gather)
            _exchange_wait(ins, outs, *refs[2 * n:], gather)

    return pl.pallas_call(
        body, name=name, out_shape=tuple(_exchange_out_shape(a, gather) for a in arrs),
        in_specs=[ANY] * n, out_specs=tuple([ANY] * n), scratch_shapes=_exchange_sems(n),
    )(*arrs)


def all_gather(arrs, name, relay=False):
    return _exchange(arrs, True, name, relay)


def all_to_all(arrs, name):
    return _exchange(arrs, False, name)


class Riders:
    def __init__(self, arrs, gather):
        self.arrs, self.gather = list(arrs), gather


def _call(body, *, name, grid, in_specs, out_specs, out_shape, semantics, scratch_shapes=(), riders=None):
    if riders is None or not riders.arrs:
        return pl.pallas_call(body, name=name, grid=grid, in_specs=in_specs, out_specs=tuple(out_specs),
                              out_shape=tuple(out_shape), scratch_shapes=list(scratch_shapes),
                              compiler_params=_cparams(*semantics))
    n_in, n_out, n_scr, n_r = len(in_specs), len(out_specs), len(scratch_shapes), len(riders.arrs)
    gather = riders.gather

    def hosted(*refs):
        ins, r_ins = refs[:n_in], refs[n_in:n_in + n_r]
        at = n_in + n_r
        outs, r_outs = refs[at:at + n_out], refs[at + n_out:at + n_out + n_r]
        at += n_out + n_r
        scratch, sems = refs[at:at + n_scr], refs[at + n_scr:]
        first = functools.reduce(jnp.logical_and, [pl.program_id(a) == 0 for a in range(len(grid))])
        last = functools.reduce(jnp.logical_and, [pl.program_id(a) == grid[a] - 1 for a in range(len(grid))])

        @pl.when(first)
        def _():
            if gather:
                _relay_start(r_ins, r_outs, *sems)
            else:
                _exchange_start(r_ins, r_outs, *sems, gather)

        body(*ins, *outs, *scratch)

        @pl.when(last)
        def _():
            if gather:
                _relay_wait(r_ins, r_outs, *sems)
            else:
                _exchange_wait(r_ins, r_outs, *sems, gather)

    call = pl.pallas_call(
        hosted, name=name, grid=grid, in_specs=list(in_specs) + [ANY] * n_r, out_specs=tuple(out_specs) + (ANY,) * n_r,
        out_shape=tuple(out_shape) + tuple(_exchange_out_shape(a, gather) for a in riders.arrs),
        scratch_shapes=list(scratch_shapes) + _exchange_sems(n_r),
        compiler_params=_cparams(*(("arbitrary",) * len(grid))))
    return lambda *args: call(*args, *riders.arrs)


def mm_nn(a, b, out_dtype, name, tm=512, riders=None):
    m, k = a.shape
    n = b.shape[1]
    tm, tn = _tile(m, tm, 8), _weight_tile(n, k)

    def body(a_ref, b_ref, o_ref):
        o_ref[...] = jnp.dot(a_ref[...], b_ref[...], preferred_element_type=f32).astype(o_ref.dtype)

    return _call(
        body, name=name, grid=(m // tm, n // tn),
        in_specs=[pl.BlockSpec((tm, k), lambda i, j: (i, 0)), pl.BlockSpec((k, tn), lambda i, j: (0, j))],
        out_specs=[pl.BlockSpec((tm, tn), lambda i, j: (i, j))],
        out_shape=[jax.ShapeDtypeStruct((m, n), out_dtype)], semantics=("parallel", "parallel"), riders=riders,
    )(a, b)


def mm_nn_residual(a, b, x, gate, name, tm=512, riders=None):
    m, k = a.shape
    n = b.shape[1]
    tm, tn = _tile(m, tm, 8), _weight_tile(n, k)

    def body(a_ref, b_ref, x_ref, g_ref, y_ref, o_ref):
        y = jnp.dot(a_ref[...], b_ref[...], preferred_element_type=f32)
        y_ref[...] = y.astype(bf16)
        o_ref[...] = x_ref[...] + g_ref[...] * y

    return _call(
        body, name=name, grid=(m // tm, n // tn),
        in_specs=[pl.BlockSpec((tm, k), lambda i, j: (i, 0)), pl.BlockSpec((k, tn), lambda i, j: (0, j)),
                  pl.BlockSpec((tm, tn), lambda i, j: (i, j)), pl.BlockSpec((1, tn), lambda i, j: (0, j))],
        out_specs=[pl.BlockSpec((tm, tn), lambda i, j: (i, j)), pl.BlockSpec((tm, tn), lambda i, j: (i, j))],
        out_shape=[jax.ShapeDtypeStruct((m, n), bf16), jax.ShapeDtypeStruct((m, n), f32)],
        semantics=("parallel", "parallel"), riders=riders,
    )(a, b, x, gate)


def mm_nt(a, b, out_dtype, name, tm=512, riders=None):
    m, k = a.shape
    n = b.shape[0]
    tm, tn = _tile(m, tm, 8), _weight_tile(n, k)

    def body(a_ref, b_ref, o_ref):
        o_ref[...] = lax.dot_general(a_ref[...], b_ref[...], (((1,), (1,)), ((), ())),
                                     preferred_element_type=f32).astype(o_ref.dtype)

    return _call(
        body, name=name, grid=(m // tm, n // tn),
        in_specs=[pl.BlockSpec((tm, k), lambda i, j: (i, 0)), pl.BlockSpec((tn, k), lambda i, j: (j, 0))],
        out_specs=[pl.BlockSpec((tm, tn), lambda i, j: (i, j))],
        out_shape=[jax.ShapeDtypeStruct((m, n), out_dtype)], semantics=("parallel", "parallel"), riders=riders,
    )(a, b)


def mm_tn(a, b, out_dtype, name, tm=512, tn=512, riders=None):
    t, m = a.shape
    n = b.shape[1]
    tm, tn = _tile(m, tm), _tile(n, tn)

    def body(a_ref, b_ref, o_ref):
        o_ref[...] = lax.dot_general(a_ref[...], b_ref[...], (((0,), (0,)), ((), ())),
                                     preferred_element_type=f32).astype(o_ref.dtype)

    return _call(
        body, name=name, grid=(m // tm, n // tn),
        in_specs=[pl.BlockSpec((t, tm), lambda i, j: (0, i)), pl.BlockSpec((t, tn), lambda i, j: (0, j))],
        out_specs=[pl.BlockSpec((tm, tn), lambda i, j: (i, j))],
        out_shape=[jax.ShapeDtypeStruct((m, n), out_dtype)], semantics=("parallel", "parallel"), riders=riders,
    )(a, b)


def mm_tn_by_owner(a, b, name, tm=512, riders=None):
    t, m = a.shape
    n = b.shape[1]
    shard = n // N_DEV
    per_step = 1 if shard % LANES == 0 else 2
    assert (per_step * shard) % LANES == 0
    tm = _tile(m, tm)

    def body(a_ref, b_ref, o_ref):
        acc = lax.dot_general(a_ref[...], b_ref[...], (((0,), (0,)), ((), ())), preferred_element_type=f32)
        for q in range(per_step):
            o_ref[q] = acc[:, q * shard:(q + 1) * shard].astype(bf16)

    return _call(
        body, name=name, grid=(m // tm, N_DEV // per_step),
        in_specs=[pl.BlockSpec((t, tm), lambda i, j: (0, i)), pl.BlockSpec((t, per_step * shard), lambda i, j: (0, j))],
        out_specs=[pl.BlockSpec((per_step, tm, shard), lambda i, j: (j, i, 0))],
        out_shape=[jax.ShapeDtypeStruct((N_DEV, m, shard), bf16)], semantics=("parallel", "parallel"), riders=riders,
    )(a, b)


def _norm_fn(x, gn, sc, sh):
    r = lax.rsqrt(jnp.mean(x * x, axis=-1, keepdims=True) + EPS)
    return (x * r * gn) * (1.0 + sc) + sh


def _row(d):
    return pl.BlockSpec((1, d), lambda i: (0, 0))


def norm_fwd(x, gn, sc, sh, name, tm=512):
    t, d = x.shape
    tm = _tile(t, tm, 8)

    def body(x_ref, gn_ref, sc_ref, sh_ref, h_ref):
        h_ref[...] = _norm_fn(x_ref[...], gn_ref[...], sc_ref[...], sh_ref[...]).astype(bf16)

    return pl.pallas_call(
        body, name=name, grid=(t // tm,),
        in_specs=[pl.BlockSpec((tm, d), lambda i: (i, 0)), _row(d), _row(d), _row(d)],
        out_specs=pl.BlockSpec((tm, d), lambda i: (i, 0)),
        out_shape=jax.ShapeDtypeStruct((t, d), bf16), compiler_params=_cparams("parallel"),
    )(x, gn, sc, sh)


def _gate_bwd(dx, y_ref, g_ref, dgate_ref, dy_ref):
    dgate_ref[...] += jnp.sum(dx * y_ref[...].astype(f32), axis=0, keepdims=True)
    dy_ref[...] = (dx * g_ref[...]).astype(bf16)


def norm_bwd(x, gn, sc, sh, dh, dres, name, gate=None, tm=512, riders=None):
    t, d = x.shape
    tm = _tile(t, tm, 8)

    def body(x_ref, gn_ref, sc_ref, sh_ref, dh_ref, dres_ref, *rest):
        if gate is not None:
            y_ref, g_ref, dx_ref, dgn_ref, dsc_ref, dsh_ref, dgate_ref, dy_ref = rest
        else:
            dx_ref, dgn_ref, dsc_ref, dsh_ref = rest

        @pl.when(pl.program_id(0) == 0)
        def _():
            dgn_ref[...] = jnp.zeros_like(dgn_ref)
            dsc_ref[...] = jnp.zeros_like(dsc_ref)
            dsh_ref[...] = jnp.zeros_like(dsh_ref)
            if gate is not None:
                dgate_ref[...] = jnp.zeros_like(dgate_ref)

        _, vjp = jax.vjp(_norm_fn, x_ref[...], gn_ref[...], sc_ref[...], sh_ref[...])
        dx, dgn, dsc, dsh = vjp(dh_ref[...].astype(f32))
        dx = dx + dres_ref[...]
        dx_ref[...] = dx
        dgn_ref[...] += dgn
        dsc_ref[...] += dsc
        dsh_ref[...] += dsh
        if gate is not None:
            _gate_bwd(dx, y_ref, g_ref, dgate_ref, dy_ref)

    blk = pl.BlockSpec((tm, d), lambda i: (i, 0))
    vec = jax.ShapeDtypeStruct((1, d), f32)
    in_specs = [blk, _row(d), _row(d), _row(d), blk, blk]
    out_specs = [blk, _row(d), _row(d), _row(d)]
    out_shape = [jax.ShapeDtypeStruct((t, d), f32), vec, vec, vec]
    args = [x, gn, sc, sh, dh, dres]
    if gate is not None:
        in_specs += [blk, _row(d)]
        out_specs += [_row(d), blk]
        out_shape += [vec, jax.ShapeDtypeStruct((t, d), bf16)]
        args += list(gate)
    return _call(body, name=name, grid=(t // tm,), in_specs=in_specs, out_specs=out_specs, out_shape=out_shape,
                 semantics=("arbitrary",), riders=riders)(*args)


def _loss_fn(x, g, tgt):
    r = lax.rsqrt(jnp.mean(x * x, axis=-1, keepdims=True) + EPS)
    err = jnp.square(x * r * g - tgt)
    return 0.5 * jnp.sum(jnp.mean(err, axis=-1, keepdims=True), axis=0, keepdims=True)


def loss_head(x, g, tgt, y, gate, name, tm=512):
    t, d = x.shape
    tm = _tile(t, tm, 8)

    def body(x_ref, g_ref, t_ref, y_ref, gate_ref, loss_ref, dx_ref, dg_ref, dgate_ref, dy_ref):
        @pl.when(pl.program_id(0) == 0)
        def _():
            loss_ref[...] = jnp.zeros_like(loss_ref)
            dg_ref[...] = jnp.zeros_like(dg_ref)
            dgate_ref[...] = jnp.zeros_like(dgate_ref)

        loss, vjp = jax.vjp(_loss_fn, x_ref[...], g_ref[...], t_ref[...])
        dx, dg, _ = vjp(jnp.ones((1, 1), f32))
        dx_ref[...] = dx
        loss_ref[...] += loss
        dg_ref[...] += dg
        _gate_bwd(dx, y_ref, gate_ref, dgate_ref, dy_ref)

    blk = pl.BlockSpec((tm, d), lambda i: (i, 0))
    vec = jax.ShapeDtypeStruct((1, d), f32)
    return pl.pallas_call(
        body, name=name, grid=(t // tm,),
        in_specs=[blk, _row(d), blk, blk, _row(d)],
        out_specs=(pl.BlockSpec((1, 1), lambda i: (0, 0)), blk, _row(d), _row(d), blk),
        out_shape=(jax.ShapeDtypeStruct((1, 1), f32), jax.ShapeDtypeStruct((t, d), f32), vec, vec,
                   jax.ShapeDtypeStruct((t, d), bf16)),
        compiler_params=_cparams("arbitrary"),
    )(x, g, tgt, y, gate)


def _gm_block_fn(z, ws, bs, lng, lnb):
    u = _gelu(z[:, :GM_WIDTH])
    vg = _gelu(z[:, GM_WIDTH:])
    mu = jnp.mean(vg, axis=-1, keepdims=True)
    var = jnp.mean(jnp.square(vg - mu), axis=-1, keepdims=True)
    vn = (vg - mu) * lax.rsqrt(var + EPS) * lng + lnb
    row = lax.broadcasted_iota(jnp.int32, (GM_BLOCK, GM_BLOCK), 0) // CHUNK
    col = lax.broadcasted_iota(jnp.int32, (GM_BLOCK, GM_BLOCK), 1) // CHUNK
    parts = []
    for h in range(GM_HEADS):
        w = jnp.where(row >= col, ws[h], 0.0)
        cols = slice(h * GM_HEAD_DIM, (h + 1) * GM_HEAD_DIM)
        s = jnp.dot(w.astype(bf16), vn[:, cols].astype(bf16), preferred_element_type=f32) + bs[h]
        parts.append(u[:, cols] * s)
    return jnp.concatenate(parts, axis=1)


def _gm_param_specs():
    return [pl.BlockSpec((GM_HEADS, GM_BLOCK, GM_BLOCK), lambda i: (0, 0, 0)),
            pl.BlockSpec((GM_HEADS, GM_BLOCK, 1), lambda i: (0, 0, 0)), _row(GM_WIDTH), _row(GM_WIDTH)]


def gm_mix_fwd(z, ws, bs, lng, lnb, name, riders=None):
    t = z.shape[0]

    def body(z_ref, ws_ref, bs_ref, lng_ref, lnb_ref, o_ref):
        o_ref[...] = _gm_block_fn(z_ref[...].astype(f32), ws_ref[...], bs_ref[...], lng_ref[...],
                                  lnb_ref[...]).astype(bf16)

    return _call(
        body, name=name, grid=(t // GM_BLOCK,),
        in_specs=[pl.BlockSpec((GM_BLOCK, 2 * GM_WIDTH), lambda i: (i, 0))] + _gm_param_specs(),
        out_specs=[pl.BlockSpec((GM_BLOCK, GM_WIDTH), lambda i: (i, 0))],
        out_shape=[jax.ShapeDtypeStruct((t, GM_WIDTH), bf16)], semantics=("parallel",), riders=riders,
    )(z, ws, bs, lng, lnb)


def gm_mix_bwd(z, ws, bs, lng, lnb, dgated, name, riders=None):
    t = z.shape[0]

    def body(z_ref, ws_ref, bs_ref, lng_ref, lnb_ref, dg_ref, dz_ref, dws_ref, dbs_ref, dlng_ref, dlnb_ref):
        _, vjp = jax.vjp(_gm_block_fn, z_ref[...].astype(f32), ws_ref[...], bs_ref[...], lng_ref[...], lnb_ref[...])
        dz, dws, dbs, dlng, dlnb = vjp(dg_ref[...].astype(f32))
        dz_ref[...] = dz.astype(bf16)

        @pl.when(pl.program_id(0) == 0)
        def _():
            dws_ref[...] = jnp.zeros_like(dws_ref)
            dbs_ref[...] = jnp.zeros_like(dbs_ref)
            dlng_ref[...] = jnp.zeros_like(dlng_ref)
            dlnb_ref[...] = jnp.zeros_like(dlnb_ref)

        dws_ref[...] += dws
        dbs_ref[...] += dbs
        dlng_ref[...] += dlng
        dlnb_ref[...] += dlnb

    zblk = pl.BlockSpec((GM_BLOCK, 2 * GM_WIDTH), lambda i: (i, 0))
    return _call(
        body, name=name, grid=(t // GM_BLOCK,),
        in_specs=[zblk] + _gm_param_specs() + [pl.BlockSpec((GM_BLOCK, GM_WIDTH), lambda i: (i, 0))],
        out_specs=[zblk] + _gm_param_specs(),
        out_shape=[jax.ShapeDtypeStruct((t, 2 * GM_WIDTH), bf16),
                   jax.ShapeDtypeStruct((GM_HEADS, GM_BLOCK, GM_BLOCK), f32),
                   jax.ShapeDtypeStruct((GM_HEADS, GM_BLOCK, 1), f32),
                   jax.ShapeDtypeStruct((1, GM_WIDTH), f32), jax.ShapeDtypeStruct((1, GM_WIDTH), f32)],
        semantics=("arbitrary",), riders=riders,
    )(z, ws, bs, lng, lnb, dgated)


def _hg_block_fn(qp, fz, iv, gp, s0, lb, gn):
    n, ns, d = HG_SUB, HG_TOKENS // HG_SUB, HG_DIM
    p, nb, per_sub = HG_PAIR, HG_TOKENS // HG_PAIR, HG_SUB // HG_PAIR
    f = lb + (1.0 - lb) * jax.nn.sigmoid(fz)
    g = jnp.log(f)
    k = 1.0 - f
    q = qp * jax.nn.sigmoid(qp)
    v = iv.astype(bf16)
    row = lax.broadcasted_iota(jnp.int32, (HG_TOKENS, HG_TOKENS), 0)
    col = lax.broadcasted_iota(jnp.int32, (HG_TOKENS, HG_TOKENS), 1)
    same_sub = col // n == row // n
    tri = ((col <= row) & same_sub).astype(f32)
    cum = jnp.dot(tri, g, precision=lax.Precision.HIGHEST, preferred_element_type=f32)
    cum_b, q_b, k_b = cum.reshape(nb, p, d), q.reshape(nb, p, d), k.reshape(nb, p, d)
    j_b = lax.broadcasted_iota(jnp.int32, (nb, p, d), 1)
    first = (row // p) * p
    scores_t = jnp.zeros((HG_TOKENS, HG_TOKENS), f32)
    for i in range(p):
        rel = jnp.where(j_b <= i, cum_b[:, i:i + 1, :] - cum_b, -1e30)
        pair = jnp.sum(q_b[:, i:i + 1, :] * k_b * jnp.exp(rel), axis=2, keepdims=True)
        scores_t = scores_t + jnp.where(col == first + i, pair.reshape(HG_TOKENS, 1), 0.0)
    o = lax.dot_general(scores_t.astype(bf16), v, (((0,), (0,)), ((), ())), preferred_element_type=f32)
    last = cum_b[:, p - 1:p, :]
    before = jnp.concatenate([jnp.zeros((1, 1, d), f32), last[:-1]], axis=0)
    before = jnp.broadcast_to(before, (nb, p, d)).reshape(HG_TOKENS, d)
    block = (lax.broadcasted_iota(jnp.int32, (HG_TOKENS, d), 0) // p) % per_sub
    q_late = q * jnp.exp(jnp.where(block > 0, cum - before, -1e30))
    last_s = last.reshape(ns, per_sub, d)
    q_parts, k_parts = [], []
    for m in range(1, per_sub):
        split = jnp.broadcast_to(last_s[:, m - 1:m, :], (ns, n, d)).reshape(HG_TOKENS, d)
        k_parts.append(k * jnp.exp(jnp.where(block < m, split - cum, -1e30)))
        q_parts.append(jnp.where(block == m, q_late, 0.0))
    scores = lax.dot_general(jnp.concatenate(q_parts, axis=1).astype(bf16), jnp.concatenate(k_parts, axis=1).astype(bf16),
                             (((1,), (1,)), ((), ())), preferred_element_type=f32)
    o = o + jnp.dot(jnp.where(same_sub, scores, 0.0).astype(bf16), v, preferred_element_type=f32)
    cum_s = cum.reshape(ns, n, d)
    tot = cum_s[:, n - 1:n, :]
    kt_t = (k.reshape(ns, n, d) * jnp.exp(tot - cum_s)).reshape(HG_TOKENS, d).T
    lane_sub = lax.broadcasted_iota(jnp.int32, (d, HG_TOKENS), 1) // n
    k_by_sub = jnp.concatenate([jnp.where(lane_sub == b, kt_t, 0.0) for b in range(ns)], axis=0).astype(bf16)
    update = jnp.dot(k_by_sub, v, preferred_element_type=f32)
    decay = jnp.exp(tot.reshape(ns, d)).T
    state = s0
    states = []
    for a in range(ns):
        states.append(state.astype(bf16))
        state = decay[:, a:a + 1] * state + update[a * d:(a + 1) * d]
    qt = q * jnp.exp(cum)
    row_sub = lax.broadcasted_iota(jnp.int32, (HG_TOKENS, d), 0) // n
    q_by_sub = jnp.concatenate([jnp.where(row_sub == a, qt, 0.0) for a in range(ns)], axis=1).astype(bf16)
    o = o + jnp.dot(q_by_sub, jnp.concatenate(states, axis=0), preferred_element_type=f32)
    on = o * lax.rsqrt(jnp.mean(o * o, axis=-1, keepdims=True) + EPS) * gn
    return on * (gp * jax.nn.sigmoid(gp)), state


def _head_parts(ref, h):
    return [ref[:, p * D + h * HG_DIM:p * D + (h + 1) * HG_DIM] for p in range(4)]


def hg_scan_fwd(proj, lb, gn, name, riders=None):
    t = proj.shape[0]
    nt = t // HG_TOKENS

    def body(p_ref, lb_ref, gn_ref, y_ref, s_ref, state):
        @pl.when(pl.program_id(0) == 0)
        def _():
            state[...] = jnp.zeros_like(state)

        for h in range(HG_HEADS):
            cols = slice(h * HG_DIM, (h + 1) * HG_DIM)
            s_ref[h, 0] = state[h]
            y, s1 = _hg_block_fn(*_head_parts(p_ref, h), state[h], lb_ref[:, cols], gn_ref[:, cols])
            y_ref[:, cols] = y.astype(bf16)
            state[h] = s1

    return _call(
        body, name=name, grid=(nt,),
        in_specs=[pl.BlockSpec((HG_TOKENS, 4 * D), lambda i: (i, 0)), _row(D), _row(D)],
        out_specs=[pl.BlockSpec((HG_TOKENS, D), lambda i: (i, 0)),
                   pl.BlockSpec((HG_HEADS, 1, HG_DIM, HG_DIM), lambda i: (0, i, 0, 0))],
        out_shape=[jax.ShapeDtypeStruct((t, D), bf16), jax.ShapeDtypeStruct((HG_HEADS, nt, HG_DIM, HG_DIM), f32)],
        scratch_shapes=[pltpu.VMEM((HG_HEADS, HG_DIM, HG_DIM), f32)],
        semantics=("arbitrary",), riders=riders,
    )(proj, lb, gn)


def hg_scan_bwd(proj, lb, gn, states, dy, name, riders=None):
    t = proj.shape[0]
    nt = t // HG_TOKENS

    def body(p_ref, lb_ref, gn_ref, s_ref, dy_ref, dp_ref, dlb_ref, dgn_ref, dstate):
        @pl.when(pl.program_id(0) == 0)
        def _():
            dstate[...] = jnp.zeros_like(dstate)
            dlb_ref[...] = jnp.zeros_like(dlb_ref)
            dgn_ref[...] = jnp.zeros_like(dgn_ref)

        for h in range(HG_HEADS):
            cols = slice(h * HG_DIM, (h + 1) * HG_DIM)
            _, vjp = jax.vjp(_hg_block_fn, *_head_parts(p_ref, h), s_ref[h, 0], lb_ref[:, cols], gn_ref[:, cols])
            grads = vjp((dy_ref[:, cols].astype(f32), dstate[h]))
            for p in range(4):
                dp_ref[:, p * D + h * HG_DIM:p * D + (h + 1) * HG_DIM] = grads[p].astype(bf16)
            dstate[h] = grads[4]
            dlb_ref[:, cols] += grads[5]
            dgn_ref[:, cols] += grads[6]

    small = jax.ShapeDtypeStruct((1, D), f32)
    return _call(
        body, name=name, grid=(nt,),
        in_specs=[pl.BlockSpec((HG_TOKENS, 4 * D), lambda i: (nt - 1 - i, 0)), _row(D), _row(D),
                  pl.BlockSpec((HG_HEADS, 1, HG_DIM, HG_DIM), lambda i: (0, nt - 1 - i, 0, 0)),
                  pl.BlockSpec((HG_TOKENS, D), lambda i: (nt - 1 - i, 0))],
        out_specs=[pl.BlockSpec((HG_TOKENS, 4 * D), lambda i: (nt - 1 - i, 0)), _row(D), _row(D)],
        out_shape=[jax.ShapeDtypeStruct((t, 4 * D), bf16), small, small],
        scratch_shapes=[pltpu.VMEM((HG_HEADS, HG_DIM, HG_DIM), f32)],
        semantics=("arbitrary",), riders=riders,
    )(proj, lb, gn, states, dy)


FFN_COLS = 1408
HALO = 8
STRIP = 16


def _ffn_specs(tm):
    nb = tm // HALO
    main_g = pl.BlockSpec((tm, FFN_COLS), lambda j, i: (i, j))
    main_v = pl.BlockSpec((tm, FFN_COLS), lambda j, i: (i, j + 2))
    halo_g = pl.BlockSpec((HALO, FFN_COLS), lambda j, i: (jnp.maximum(i * nb - 1, 0), j))
    halo_v = pl.BlockSpec((HALO, FFN_COLS), lambda j, i: (jnp.maximum(i * nb - 1, 0), j + 2))
    w_g = pl.BlockSpec((3, FFN_COLS), lambda j, i: (0, j))
    w_v = pl.BlockSpec((3, FFN_COLS), lambda j, i: (0, j + 2))
    b_g = pl.BlockSpec((1, FFN_COLS), lambda j, i: (0, j))
    b_v = pl.BlockSpec((1, FFN_COLS), lambda j, i: (0, j + 2))
    return [main_g, halo_g, main_v, halo_v, w_g, w_v, b_g, b_v]


def _strip_rows(r):
    return pl.ds(r * STRIP, STRIP) if isinstance(r, int) else pl.ds(pl.multiple_of(r * STRIP, STRIP), STRIP)


def _for_strips(nstrip, strip, reverse=False):
    if reverse:
        strip(nstrip - 1, True)
        lax.fori_loop(0, nstrip - 1, lambda k, c: (strip(nstrip - 2 - k, False), c)[1], 0)
    else:
        strip(0, True)
        lax.fori_loop(1, nstrip, lambda r, c: (strip(r, False), c)[1], 0)


def _conv_strip(main_ref, halo_ref, w_ref, b_ref, r, edge, cols, rowi):
    cur = main_ref[_strip_rows(r), cols].astype(f32)
    if edge:
        h = jnp.where(pl.program_id(1) == 0, 0.0, halo_ref[:, cols].astype(f32))
        prev = jnp.concatenate([jnp.zeros_like(h), h], axis=0)
    else:
        prev = main_ref[_strip_rows(r - 1), cols].astype(f32)
    a1 = jnp.where(rowi < 1, pltpu.roll(prev, 1, axis=0), pltpu.roll(cur, 1, axis=0))
    a2 = jnp.where(rowi < 2, pltpu.roll(prev, 2, axis=0), pltpu.roll(cur, 2, axis=0))
    y = b_ref[:, cols] + w_ref[0:1, cols] * a2 + w_ref[1:2, cols] * a1 + w_ref[2:3, cols] * cur
    return y, (cur, a1, a2)


def ffn_gate_fwd(a, cw, cb, name, tm=512, riders=None):
    t = a.shape[0]
    tm = _tile(t, tm, STRIP)

    def body(ag_ref, hg_ref, av_ref, hv_ref, wg_ref, wv_ref, bg_ref, bv_ref, o_ref):
        rowi = lax.broadcasted_iota(jnp.int32, (STRIP, LANES), 0)

        def strip(r, edge):
            for c in range(FFN_COLS // LANES):
                cols = pl.ds(c * LANES, LANES)
                yg, _ = _conv_strip(ag_ref, hg_ref, wg_ref, bg_ref, r, edge, cols, rowi)
                yv, _ = _conv_strip(av_ref, hv_ref, wv_ref, bv_ref, r, edge, cols, rowi)
                o_ref[_strip_rows(r), cols] = (_gelu(yg) * yv).astype(bf16)

        _for_strips(tm // STRIP, strip)

    return _call(
        body, name=name, grid=(2, t // tm), in_specs=_ffn_specs(tm),
        out_specs=[pl.BlockSpec((tm, FFN_COLS), lambda j, i: (i, j))],
        out_shape=[jax.ShapeDtypeStruct((t, FFN_HIDDEN), bf16)],
        semantics=("parallel", "arbitrary"), riders=riders,
    )(a, a, a, a, cw, cw, cb, cb)


def ffn_gate_bwd(a, cw, cb, dhid, name, tm=512, riders=None):
    t = a.shape[0]
    tm = _tile(t, tm, STRIP)

    def body(ag_ref, hg_ref, av_ref, hv_ref, wg_ref, wv_ref, bg_ref, bv_ref, dh_ref,
             dy_ref, dwg_ref, dwv_ref, dbg_ref, dbv_ref, acc):
        rowi = lax.broadcasted_iota(jnp.int32, (STRIP, LANES), 0)

        @pl.when(pl.program_id(1) == 0)
        def _():
            acc[...] = jnp.zeros_like(acc)

        def strip(r, edge):
            rows = _strip_rows(r)
            for c in range(FFN_COLS // LANES):
                cols = pl.ds(c * LANES, LANES)
                yg, taps_g = _conv_strip(ag_ref, hg_ref, wg_ref, bg_ref, r, edge, cols, rowi)
                yv, taps_v = _conv_strip(av_ref, hv_ref, wv_ref, bv_ref, r, edge, cols, rowi)
                dh = dh_ref[rows, cols].astype(f32)
                cdf = 0.5 * (1.0 + lax.erf(yg * (1.0 / math.sqrt(2.0))))
                pdf = jnp.exp(-0.5 * yg * yg) * (1.0 / math.sqrt(2.0 * math.pi))
                dyg = dh * yv * (cdf + yg * pdf)
                dyv = dh * (yg * cdf)
                dy_ref[0, rows, cols] = dyg.astype(bf16)
                dy_ref[1, rows, cols] = dyv.astype(bf16)
                for p, (dy, (a0, a1, a2)) in enumerate(((dyg, taps_g), (dyv, taps_v))):
                    acc[4 * p + 0, :, cols] += dy * a2
                    acc[4 * p + 1, :, cols] += dy * a1
                    acc[4 * p + 2, :, cols] += dy * a0
                    acc[4 * p + 3, :, cols] += dy

        _for_strips(tm // STRIP, strip)

        @pl.when(pl.program_id(1) == pl.num_programs(1) - 1)
        def _():
            for p, (dw_ref, db_ref) in enumerate(((dwg_ref, dbg_ref), (dwv_ref, dbv_ref))):
                for tap in range(3):
                    dw_ref[tap:tap + 1, :] = jnp.sum(acc[4 * p + tap], axis=0, keepdims=True)
                db_ref[...] = jnp.sum(acc[4 * p + 3], axis=0, keepdims=True)

    half_w = pl.BlockSpec((3, FFN_COLS), lambda j, i: (0, j))
    half_b = pl.BlockSpec((1, FFN_COLS), lambda j, i: (0, j))
    return _call(
        body, name=name, grid=(2, t // tm),
        in_specs=_ffn_specs(tm) + [pl.BlockSpec((tm, FFN_COLS), lambda j, i: (i, j))],
        out_specs=[pl.BlockSpec((2, tm, FFN_COLS), lambda j, i: (0, i, j)), half_w, half_w, half_b, half_b],
        out_shape=[jax.ShapeDtypeStruct((2, t, FFN_HIDDEN), bf16),
                   jax.ShapeDtypeStruct((3, FFN_HIDDEN), f32), jax.ShapeDtypeStruct((3, FFN_HIDDEN), f32),
                   jax.ShapeDtypeStruct((1, FFN_HIDDEN), f32), jax.ShapeDtypeStruct((1, FFN_HIDDEN), f32)],
        scratch_shapes=[pltpu.VMEM((8, STRIP, FFN_COLS), f32)],
        semantics=("parallel", "arbitrary"), riders=riders,
    )(a, a, a, a, cw, cw, cb, cb, dhid)


def conv_transpose(dy, cw, name, tm=512):
    _, t, fh = dy.shape
    tm = _tile(t, tm, STRIP)
    nb = tm // HALO
    last_halo = t // HALO - 1
    ncol = fh // FFN_COLS

    def body(main_ref, halo_ref, w_ref, o_ref):
        rowi = lax.broadcasted_iota(jnp.int32, (STRIP, LANES), 0)
        last_block = pl.program_id(2) == pl.num_programs(2) - 1

        def strip(r, edge):
            rows = _strip_rows(r)
            for c in range(FFN_COLS // LANES):
                cols = pl.ds(c * LANES, LANES)
                cur = main_ref[0, rows, cols].astype(f32)
                if edge:
                    h = jnp.where(last_block, 0.0, halo_ref[0, :, cols].astype(f32))
                    nxt = jnp.concatenate([h, jnp.zeros_like(h)], axis=0)
                else:
                    nxt = main_ref[0, _strip_rows(r + 1), cols].astype(f32)
                d1 = jnp.where(rowi >= STRIP - 1, pltpu.roll(nxt, STRIP - 1, axis=0), pltpu.roll(cur, STRIP - 1, axis=0))
                d2 = jnp.where(rowi >= STRIP - 2, pltpu.roll(nxt, STRIP - 2, axis=0), pltpu.roll(cur, STRIP - 2, axis=0))
                o_ref[rows, cols] = (w_ref[2:3, cols] * cur + w_ref[1:2, cols] * d1 + w_ref[0:1, cols] * d2).astype(bf16)

        _for_strips(tm // STRIP, strip, reverse=True)

    return pl.pallas_call(
        body, name=name, grid=(2, ncol, t // tm),
        in_specs=[pl.BlockSpec((1, tm, FFN_COLS), lambda p, j, i: (p, i, j)),
                  pl.BlockSpec((1, HALO, FFN_COLS), lambda p, j, i: (p, jnp.minimum((i + 1) * nb, last_halo), j)),
                  pl.BlockSpec((3, FFN_COLS), lambda p, j, i: (0, p * ncol + j))],
        out_specs=pl.BlockSpec((tm, FFN_COLS), lambda p, j, i: (i, p * ncol + j)),
        out_shape=jax.ShapeDtypeStruct((t, 2 * fh), bf16),
        compiler_params=_cparams("parallel", "parallel", "arbitrary"),
    )(dy, dy, cw)


def ada_mod(c_all, ada_w, ada_b_cols, name):
    cols = ada_w.shape[2]

    def body(c_ref, w_ref, b_ref, o_ref):
        c = c_ref[...]
        cond = (c * jax.nn.sigmoid(c)).astype(bf16)
        o_ref[0] = jnp.dot(cond, w_ref[0].astype(bf16), preferred_element_type=f32) + b_ref[0]

    return pl.pallas_call(
        body, name=name, grid=(DEPTH,),
        in_specs=[pl.BlockSpec((N_DEV, D), lambda i: (0, 0)), pl.BlockSpec((1, D, cols), lambda i: (i, 0, 0)),
                  pl.BlockSpec((1, 1, cols), lambda i: (i, 0, 0))],
        out_specs=pl.BlockSpec((1, N_DEV, cols), lambda i: (i, 0, 0)),
        out_shape=jax.ShapeDtypeStruct((DEPTH, N_DEV, cols), f32), compiler_params=_cparams("parallel"),
    )(c_all, ada_w, ada_b_cols)


def ada_grads(c_all, dmod_cols, dmod_all, name):
    cols = dmod_cols.shape[2]

    def body(c_ref, dm_ref, da_ref, dw_ref, db_ref):
        c = c_ref[...]
        cond = c * jax.nn.sigmoid(c)
        dw_ref[0] = lax.dot_general(cond, dm_ref[0], (((0,), (0,)), ((), ())), precision=lax.Precision.HIGHEST,
                                    preferred_element_type=f32)
        acc = da_ref[0, 0]
        for e in range(1, N_DEV):
            acc = acc + da_ref[e, 0]
        db_ref[0] = acc

    return pl.pallas_call(
        body, name=name, grid=(DEPTH,),
        in_specs=[pl.BlockSpec((N_DEV, D), lambda i: (0, 0)), pl.BlockSpec((1, N_DEV, cols), lambda i: (i, 0, 0)),
                  pl.BlockSpec((N_DEV, 1, 1, 6 * D), lambda i: (0, i, 0, 0))],
        out_specs=(pl.BlockSpec((1, D, cols), lambda i: (i, 0, 0)), pl.BlockSpec((1, 1, 6 * D), lambda i: (i, 0, 0))),
        out_shape=(jax.ShapeDtypeStruct((DEPTH, D, cols), f32), jax.ShapeDtypeStruct((DEPTH, 1, 6 * D), f32)),
        compiler_params=_cparams("parallel"),
    )(c_all, dmod_cols, dmod_all)


def lower_bound_fwd(hg_lb, name):
    n = hg_lb.shape[1]

    def body(l_ref, o_ref):
        o_ref[...] = jax.nn.sigmoid(l_ref[1:2, :] - l_ref[0:1, :])

    return pl.pallas_call(body, name=name, out_shape=jax.ShapeDtypeStruct((1, n), f32))(hg_lb)


def lower_bound_bwd(hg_lb, dlb, name):
    n = hg_lb.shape[1]

    def body(l_ref, d_ref, o_ref):
        p = jax.nn.sigmoid(l_ref[1:2, :] - l_ref[0:1, :])
        g = d_ref[...] * p * (1.0 - p)
        o_ref[0:1, :] = -g
        o_ref[1:2, :] = g

    return pl.pallas_call(body, name=name, out_shape=jax.ShapeDtypeStruct((2, n), f32))(hg_lb, dlb)


def _adamw(w, g, m, v):
    m = ADAM_B1 * m + (1.0 - ADAM_B1) * g
    v = ADAM_B2 * v + (1.0 - ADAM_B2) * jnp.square(g)
    m_hat = m / (1.0 - ADAM_B1 ** ADAM_STEP)
    v_hat = v / (1.0 - ADAM_B2 ** ADAM_STEP)
    delta = -ADAM_LR * (m_hat / (jnp.sqrt(v_hat) + ADAM_EPS) + ADAM_WD * w)
    return delta, m, v


def adam_reduced(parts, w, m, v, name, tr=128):
    layers, r, c = w.shape
    flat = [p for layer_parts in parts for p in layer_parts]
    rows = flat[0].shape[1]
    assert all(p.shape == (N_DEV, rows, c) for p in flat) and rows * len(flat) == layers * r
    per_layer = r // rows
    tr = _tile(rows, tr, 16)
    steps = rows // tr

    def body(*refs):
        p_refs = refs[:len(flat)]
        w_ref, m_ref, v_ref, g_ref, d_ref, mo_ref, vo_ref = refs[len(flat):]
        for idx in range(len(flat)):
            @pl.when(pl.program_id(0) == idx)
            def _():
                g = p_refs[idx][0].astype(f32)
                for j in range(1, N_DEV):
                    g = g + p_refs[idx][j].astype(f32)
                g_ref[...] = g
                d_ref[...], mo_ref[...], vo_ref[...] = _adamw(w_ref[...], g, m_ref[...], v_ref[...])

    def part_spec(idx):
        return pl.BlockSpec((N_DEV, tr, c), lambda p, i: (0, jnp.where(p == idx, i, 0), 0))

    blk = pl.BlockSpec((None, tr, c), lambda p, i: (p // per_layer, (p % per_layer) * steps + i, 0))
    out = jax.ShapeDtypeStruct((layers, r, c), f32)
    return pl.pallas_call(
        body, name=name, grid=(len(flat), steps),
        in_specs=[part_spec(idx) for idx in range(len(flat))] + [blk, blk, blk],
        out_specs=(blk, blk, blk, blk), out_shape=(out, out, out, out), compiler_params=_cparams("arbitrary", "arbitrary"),
    )(*flat, w, m, v)


def adam_plain(g, w, m, v, name, tr=256):
    r, c = w.shape
    tr = _tile(r, tr, 8)

    def body(g_ref, w_ref, m_ref, v_ref, d_ref, mo_ref, vo_ref):
        d_ref[...], mo_ref[...], vo_ref[...] = _adamw(w_ref[...], g_ref[...], m_ref[...], v_ref[...])

    blk = pl.BlockSpec((tr, c), lambda i: (i, 0))
    out = jax.ShapeDtypeStruct((r, c), f32)
    return pl.pallas_call(
        body, name=name, grid=(r // tr,), in_specs=[blk, blk, blk, blk], out_specs=(blk, blk, blk),
        out_shape=(out, out, out), compiler_params=_cparams("parallel"),
    )(g, w, m, v)


def sum_parts(parts, name):
    _, r, c = parts.shape

    def body(p_ref, o_ref):
        acc = p_ref[0]
        for j in range(1, N_DEV):
            acc = acc + p_ref[j]
        o_ref[...] = acc

    return pl.pallas_call(body, name=name, out_shape=jax.ShapeDtypeStruct((r, c), f32))(parts)


def _pack(arrs, rows_mult=8):
    flat = jnp.concatenate([a.reshape(-1) for a in arrs])
    rows = -(-flat.shape[0] // LANES)
    rows = -(-rows // rows_mult) * rows_mult
    return jnp.pad(flat, (0, rows * LANES - flat.shape[0])).reshape(rows, LANES)


def _unpack(flat, shapes):
    out, at = [], 0
    for s in shapes:
        n = math.prod(s)
        out.append(flat[at:at + n].reshape(s))
        at += n
    return out


def kernel(x, c, gm_w_in, gm_ln_g, gm_ln_b, gm_w_s, gm_b_s, gm_w_out, hg_w_in, hg_lb, hg_gn_g, hg_w_out, ffn_w_up, ffn_conv_w, ffn_conv_b, ffn_w_down, norm_g, ada_w, ada_b, final_g, loss_target, m_gm_w_in, m_gm_ln_g, m_gm_ln_b, m_gm_w_s, m_gm_b_s, m_gm_w_out, m_hg_w_in, m_hg_lb, m_hg_gn_g, m_hg_w_out, m_ffn_w_up, m_ffn_conv_w, m_ffn_conv_b, m_ffn_w_down, m_norm_g, m_ada_w, m_ada_b, m_final_g, v_gm_w_in, v_gm_ln_g, v_gm_ln_b, v_gm_w_s, v_gm_b_s, v_gm_w_out, v_hg_w_in, v_hg_lb, v_hg_gn_g, v_hg_w_out, v_ffn_w_up, v_ffn_conv_w, v_ffn_conv_b, v_ffn_w_down, v_norm_g, v_ada_w, v_ada_b, v_final_g):
    me = _flat(_mesh_pos())
    xt = x[0]
    t = xt.shape[0]

    small_shapes = [(1, D), (2, HG_DIM), (2, HG_DIM), (DEPTH, 2, HG_DIM), (DEPTH, 3, 2 * FFN_HIDDEN // N_DEV)]
    (small_all,) = all_gather([_pack([c, hg_lb, hg_gn_g, norm_g, ffn_conv_w])], "gather_small")
    small_all = small_all.reshape(N_DEV, -1)
    at = 0
    pieces = []
    for s in small_shapes:
        n = math.prod(s)
        pieces.append(small_all[:, at:at + n].reshape((N_DEV,) + s))
        at += n
    c_all = pieces[0].reshape(N_DEV, D)
    hg_lb_full = jnp.transpose(pieces[1], (1, 0, 2)).reshape(2, D)
    hg_gn_full = jnp.transpose(pieces[2], (1, 0, 2)).reshape(2, D)
    norm_g_full = jnp.transpose(pieces[3], (1, 2, 0, 3)).reshape(DEPTH, 2, D)
    conv_w_full = jnp.transpose(pieces[4], (1, 2, 0, 3)).reshape(DEPTH, 3, 2 * FFN_HIDDEN)

    lb1 = lower_bound_fwd(hg_lb_full, "lower_bound")
    lbs = [jnp.zeros((1, D), f32), lb1]

    ada_b_cols = lax.dynamic_slice(ada_b, (0, me * ADA_COLS), (DEPTH, ADA_COLS)).reshape(DEPTH, 1, ADA_COLS)
    mod_cols = ada_mod(c_all, ada_w, ada_b_cols, "ada_mod")
    (mod_mine,) = all_to_all([jnp.transpose(mod_cols, (1, 0, 2))], "mod_to_examples")
    mod = jnp.transpose(mod_mine, (1, 0, 2)).reshape(DEPTH, 6, 1, D)

    def layer_shards(i):
        j = i // 2
        w_in, w_out = (gm_w_in, gm_w_out) if i % 2 == 0 else (hg_w_in, hg_w_out)
        return [w_in[j].astype(bf16), w_out[j].astype(bf16), ffn_w_up[i].astype(bf16), ffn_w_down[i].astype(bf16)]

    def full_cols(g):
        return jnp.transpose(g, (1, 0, 2)).reshape(g.shape[1], N_DEV * g.shape[2])

    def full_rows(g):
        return g.reshape(N_DEV * g.shape[1], g.shape[2])

    carried_by = {
        "in_0": [(0, 1), (0, 2)], "mix_0": [(0, 3)], "up_0": [(1, 0), (1, 1)], "gate_0": [(1, 2)], "down_0": [(1, 3)],
        "in_1": [(2, 0)], "mix_1": [(2, 1), (2, 2), (2, 3)], "up_1": [(3, 0), (3, 1)], "gate_1": [(3, 2)], "down_1": [(3, 3)],
    }
    shards = [layer_shards(i) for i in range(DEPTH)]
    gathered = {}
    (gathered[(0, 0)],) = all_gather([shards[0][0]], "gather_weights_0", relay=True)

    def carry(call, site, **kw):
        items = carried_by.get(site, [])
        outs = call(riders=Riders([shards[l][slot] for l, slot in items], True), **kw)
        for item, g in zip(items, outs[len(outs) - len(items):]):
            gathered[item] = g
        return outs[:len(outs) - len(items)]

    saved = []
    weights = []
    xcur = xt
    for i in range(DEPTH):
        j = i // 2
        sh1, sc1, g1, sh2, sc2, g2 = [mod[i, p] for p in range(6)]
        gn1, gn2 = norm_g_full[i, 0:1], norm_g_full[i, 1:2]
        s = {"x0": xcur}
        h = norm_fwd(xcur, gn1, sc1, sh1, f"norm1_{i}")
        s["h"] = h
        w_in = full_cols(gathered[(i, 0)])
        if i % 2 == 0:
            (z,) = carry(functools.partial(mm_nn, h, w_in, bf16, f"gm_in_{i}"), f"in_{i}")
            bs = gm_b_s[j].reshape(GM_HEADS, GM_BLOCK, 1)
            (mixed,) = carry(functools.partial(gm_mix_fwd, z, gm_w_s[j], bs, gm_ln_g[j:j + 1], gm_ln_b[j:j + 1],
                                               f"gm_mix_{i}"), f"mix_{i}")
            s["z"] = z
        else:
            (proj,) = carry(functools.partial(mm_nn, h, w_in, f32, f"hg_in_{i}"), f"in_{i}")
            mixed, states = carry(functools.partial(hg_scan_fwd, proj, lbs[j], hg_gn_full[j:j + 1], f"hg_scan_{i}"),
                                  f"mix_{i}")
            s["proj"], s["states"] = proj, states
        s["mixed"] = mixed
        w_out = full_rows(gathered[(i, 1)])
        y, x1 = carry(functools.partial(mm_nn_residual, mixed, w_out, xcur, g1, f"mix_out_{i}"), f"out_{i}")
        s["y"], s["x1"] = y, x1
        h2 = norm_fwd(x1, gn2, sc2, sh2, f"norm2_{i}")
        w_up = full_cols(gathered[(i, 2)])
        (a,) = carry(functools.partial(mm_nn, h2, w_up, bf16, f"ffn_up_{i}"), f"up_{i}")
        (hid,) = carry(functools.partial(ffn_gate_fwd, a, conv_w_full[i], ffn_conv_b[i:i + 1], f"ffn_gate_{i}"), f"gate_{i}")
        w_down = full_rows(gathered[(i, 3)])
        fo, x2 = carry(functools.partial(mm_nn_residual, hid, w_down, x1, g2, f"ffn_down_{i}"), f"down_{i}")
        s["h2"], s["a"], s["hid"], s["f"] = h2, a, hid, fo
        weights.append((w_in, w_out, w_up, w_down))
        saved.append(s)
        xcur = x2

    loss_part, dx, d_final_g, dg2, df = loss_head(xcur, final_g.reshape(1, D), loss_target[0], saved[-1]["f"],
                                                  mod[DEPTH - 1, 5], "loss_head")
    loss = lax.psum(loss_part[0, 0], ("x", "y", "c"))

    def halves(blocked):
        rows = blocked.shape[1] // 2
        return [blocked[:, :rows], blocked[:, rows:]]

    def by_owner_rows(dw):
        k, n = dw.shape
        return dw.reshape(N_DEV, k // N_DEV, n)

    received = [[[] for _ in range(4)] for _ in range(DEPTH)]

    def send(call, items, **kw):
        outs = call(riders=Riders([arr for arr, _ in items], False), **kw)
        for (_, (layer, slot)), got in zip(items, outs[len(outs) - len(items):]):
            received[layer][slot].append(got)
        return outs[:len(outs) - len(items)]

    dmod = [None] * DEPTH
    d_norm_g = [None] * DEPTH
    d_gm = {k: [None, None] for k in ("ws", "bs", "lng", "lnb")}
    d_hg = {k: [None, None] for k in ("lb", "gn")}
    d_ffn = {k: [None] * DEPTH for k in ("cw", "cb")}
    in_halves = []
    for i in reversed(range(DEPTH)):
        j = i // 2
        s = saved[i]
        w_in, w_out, w_up, w_down = weights[i]
        sh1, sc1, g1, sh2, sc2, g2 = [mod[i, p] for p in range(6)]
        gn1, gn2 = norm_g_full[i, 0:1], norm_g_full[i, 1:2]
        (dw_down,) = mm_tn(s["hid"], df, bf16, f"dw_down_{i}")
        (dhid,) = send(functools.partial(mm_nt, df, w_down, bf16, f"dhid_{i}"), in_halves[:1])
        dyc, dwg, dwv, dbg, dbv = send(
            functools.partial(ffn_gate_bwd, s["a"], conv_w_full[i], ffn_conv_b[i:i + 1], dhid, f"ffn_gate_bwd_{i}"),
            in_halves[1:] + [(by_owner_rows(dw_down), (i, 3))])
        d_ffn["cw"][i] = jnp.concatenate([dwg, dwv], axis=1)
        d_ffn["cb"][i] = jnp.concatenate([dbg, dbv], axis=1)
        da = conv_transpose(dyc, conv_w_full[i], f"conv_t_{i}")
        (dw_up,) = mm_tn_by_owner(s["h2"], da, f"dw_up_{i}")
        up_halves = [(part, (i, 2)) for part in halves(dw_up)]
        (dh2,) = send(functools.partial(mm_nt, da, w_up, bf16, f"dh2_{i}"), up_halves[:1])
        dx1, dgn2, dsc2, dsh2, dg1, dy = norm_bwd(s["x1"], gn2, sc2, sh2, dh2, dx, f"norm2_bwd_{i}", gate=(s["y"], g1))
        (dw_out,) = mm_tn(s["mixed"], dy, bf16, f"dw_mix_out_{i}")
        (dmixed,) = mm_nt(dy, w_out, bf16, f"dmixed_{i}")
        if i % 2 == 0:
            bs = gm_b_s[j].reshape(GM_HEADS, GM_BLOCK, 1)
            dpre, dws, dbs, dlng, dlnb = send(
                functools.partial(gm_mix_bwd, s["z"], gm_w_s[j], bs, gm_ln_g[j:j + 1], gm_ln_b[j:j + 1], dmixed,
                                  f"gm_mix_bwd_{i}"), up_halves[1:])
            d_gm["ws"][j], d_gm["bs"][j], d_gm["lng"][j], d_gm["lnb"][j] = dws, dbs.reshape(GM_HEADS, GM_BLOCK), dlng, dlnb
        else:
            dpre, dlb, dgn = send(
                functools.partial(hg_scan_bwd, s["proj"], lbs[j], hg_gn_full[j:j + 1], s["states"], dmixed,
                                  f"hg_scan_bwd_{i}"), up_halves[1:])
            d_hg["lb"][j], d_hg["gn"][j] = dlb, dgn
        (dw_in,) = send(functools.partial(mm_tn_by_owner, s["h"], dpre, f"dw_mix_in_{i}"), [(by_owner_rows(dw_out), (i, 1))])
        in_halves = [(part, (i, 0)) for part in halves(dw_in)]
        (dh,) = send(functools.partial(mm_nt, dpre, w_in, bf16, f"dh_mix_{i}"), in_halves[:1] if i == 0 else [])
        dmod_i = [None, None, dg1, dsh2, dsc2, dg2]
        if i > 0:
            dx, dgn1, dsc1, dsh1, dg2, df = norm_bwd(s["x0"], gn1, sc1, sh1, dh, dx1, f"norm1_bwd_{i}",
                                                     gate=(saved[i - 1]["f"], mod[i - 1, 5]))
        else:
            dx, dgn1, dsc1, dsh1 = send(functools.partial(norm_bwd, s["x0"], gn1, sc1, sh1, dh, dx1, f"norm1_bwd_{i}"),
                                        in_halves[1:])
        dmod_i[0], dmod_i[1] = dsh1, dsc1
        dmod[i] = jnp.concatenate(dmod_i, axis=1)
        d_norm_g[i] = jnp.concatenate([dgn1, dgn2], axis=0)
    grad_x = dx.reshape(1, t, D)

    (dmod_all,) = all_gather([jnp.concatenate(dmod, axis=0)], "gather_dmod")
    dmod_cols = jnp.transpose(lax.dynamic_slice(dmod_all, (0, 0, me * ADA_COLS), (N_DEV, DEPTH, ADA_COLS)), (1, 0, 2))
    g_ada_w, g_ada_b = ada_grads(c_all, dmod_cols, dmod_all.reshape(N_DEV, DEPTH, 1, 6 * D), "ada_grads")
    g_ada_b = g_ada_b.reshape(DEPTH, 6 * D)

    small_partials = [jnp.concatenate(d_gm["lng"], axis=0), jnp.concatenate(d_gm["lnb"], axis=0),
                      jnp.stack(d_gm["ws"]), jnp.stack(d_gm["bs"]), jnp.concatenate(d_ffn["cb"], axis=0),
                      d_final_g, d_hg["lb"][1], jnp.concatenate(d_hg["gn"], axis=0), jnp.stack(d_norm_g),
                      jnp.stack(d_ffn["cw"])]
    partial_shapes = [p.shape for p in small_partials]
    packed = _pack(small_partials, rows_mult=8 * N_DEV)
    rows = packed.shape[0] // N_DEV
    (recv,) = all_to_all([packed.reshape(N_DEV, rows, LANES)], "small_grads_exchange")
    (summed,) = all_gather([sum_parts(recv, "small_grads_sum")], "small_grads_gather")
    g_ln_g, g_ln_b, g_ws, g_bs, g_cb, g_final, g_lb1, g_gn, g_norm, g_cw = _unpack(summed.reshape(-1), partial_shapes)
    g_final = g_final.reshape(D)

    def my_cols(a, n):
        start = (0,) * (a.ndim - 1) + (me * n,)
        return lax.dynamic_slice(a, start, a.shape[:-1] + (n,))

    g_hg_lb = lower_bound_bwd(hg_lb, my_cols(g_lb1, HG_DIM), "lower_bound_bwd")
    g_hg_gn = my_cols(g_gn, HG_DIM)
    g_norm_g = my_cols(g_norm, HG_DIM)
    g_conv_w = my_cols(g_cw, 2 * FFN_HIDDEN // N_DEV)

    def parts_of(slot, layers):
        return [received[i][slot] for i in layers]

    w_shards = [gm_w_in, gm_w_out, hg_w_in, hg_w_out, ffn_w_up, ffn_w_down]
    big_parts = [parts_of(0, (0, 2)), parts_of(1, (0, 2)), parts_of(0, (1, 3)), parts_of(1, (1, 3)),
                 parts_of(2, range(DEPTH)), parts_of(3, range(DEPTH))]
    big_m = [m_gm_w_in, m_gm_w_out, m_hg_w_in, m_hg_w_out, m_ffn_w_up, m_ffn_w_down]
    big_v = [v_gm_w_in, v_gm_w_out, v_hg_w_in, v_hg_w_out, v_ffn_w_up, v_ffn_w_down]
    big = [adam_reduced(parts, w, m_, v_, f"adam_big_{idx}")
           for idx, (w, m_, v_, parts) in enumerate(zip(w_shards, big_m, big_v, big_parts))]
    (g_gm_w_in, d_gm_w_in, nm_gm_w_in, nv_gm_w_in), (g_gm_w_out, d_gm_w_out, nm_gm_w_out, nv_gm_w_out), \
        (g_hg_w_in, d_hg_w_in, nm_hg_w_in, nv_hg_w_in), (g_hg_w_out, d_hg_w_out, nm_hg_w_out, nv_hg_w_out), \
        (g_ffn_w_up, d_ffn_w_up, nm_ffn_w_up, nv_ffn_w_up), (g_ffn_w_down, d_ffn_w_down, nm_ffn_w_down, nv_ffn_w_down) = big

    two_d = (-1, ADA_COLS)
    d_ada_w, nm_ada_w, nv_ada_w = [o.reshape(ada_w.shape) for o in adam_plain(
        g_ada_w.reshape(two_d), ada_w.reshape(two_d), m_ada_w.reshape(two_d), v_ada_w.reshape(two_d), "adam_ada_w")]

    small_g = [g_ln_g, g_ln_b, g_ws, g_bs, g_cb, g_ada_b, g_final, g_hg_lb, g_hg_gn, g_norm_g, g_conv_w]
    small_w = [gm_ln_g, gm_ln_b, gm_w_s, gm_b_s, ffn_conv_b, ada_b, final_g, hg_lb, hg_gn_g, norm_g, ffn_conv_w]
    small_m = [m_gm_ln_g, m_gm_ln_b, m_gm_w_s, m_gm_b_s, m_ffn_conv_b, m_ada_b, m_final_g, m_hg_lb, m_hg_gn_g, m_norm_g, m_ffn_conv_w]
    small_v = [v_gm_ln_g, v_gm_ln_b, v_gm_w_s, v_gm_b_s, v_ffn_conv_b, v_ada_b, v_final_g, v_hg_lb, v_hg_gn_g, v_norm_g, v_ffn_conv_w]
    shapes = [w.shape for w in small_w]
    small_g = [g.reshape(s) for g, s in zip(small_g, shapes)]
    outs = adam_plain(_pack(small_g), _pack(small_w), _pack(small_m), _pack(small_v), "adam_small")
    (d_ln_g, d_ln_b, d_ws, d_bs, d_cb, d_ada_b, d_final, d_hg_lb, d_hg_gn, d_norm_g_, d_conv_w), \
        (nm_ln_g, nm_ln_b, nm_ws, nm_bs, nm_cb, nm_ada_b, nm_final, nm_hg_lb, nm_hg_gn, nm_norm_g, nm_conv_w), \
        (nv_ln_g, nv_ln_b, nv_ws, nv_bs, nv_cb, nv_ada_b, nv_final, nv_hg_lb, nv_hg_gn, nv_norm_g, nv_conv_w) = [
            _unpack(o.reshape(-1), shapes) for o in outs]
    g_ln_g, g_ln_b, g_ws, g_bs, g_cb, g_ada_b, g_final, g_hg_lb, g_hg_gn, g_norm_g, g_conv_w = small_g

    grads = (g_gm_w_in, g_ln_g, g_ln_b, g_ws, g_bs, g_gm_w_out, g_hg_w_in, g_hg_lb, g_hg_gn, g_hg_w_out,
             g_ffn_w_up, g_conv_w, g_cb, g_ffn_w_down, g_norm_g, g_ada_w, g_ada_b, g_final)
    deltas = (d_gm_w_in, d_ln_g, d_ln_b, d_ws, d_bs, d_gm_w_out, d_hg_w_in, d_hg_lb, d_hg_gn, d_hg_w_out,
              d_ffn_w_up, d_conv_w, d_cb, d_ffn_w_down, d_norm_g_, d_ada_w, d_ada_b, d_final)
    new_m = (nm_gm_w_in, nm_ln_g, nm_ln_b, nm_ws, nm_bs, nm_gm_w_out, nm_hg_w_in, nm_hg_lb, nm_hg_gn, nm_hg_w_out,
             nm_ffn_w_up, nm_conv_w, nm_cb, nm_ffn_w_down, nm_norm_g, nm_ada_w, nm_ada_b, nm_final)
    new_v = (nv_gm_w_in, nv_ln_g, nv_ln_b, nv_ws, nv_bs, nv_gm_w_out, nv_hg_w_in, nv_hg_lb, nv_hg_gn, nv_hg_w_out,
             nv_ffn_w_up, nv_conv_w, nv_cb, nv_ffn_w_down, nv_norm_g, nv_ada_w, nv_ada_b, nv_final)
    return (loss, grad_x) + grads + deltas + new_m + new_v
```

```python
import functools
import math

import jax
import jax.numpy as jnp
from jax import lax
from jax.experimental import pallas as pl
from jax.experimental.pallas import tpu as pltpu

f32 = jnp.float32
bf16 = jnp.bfloat16
MESH = pl.DeviceIdType.MESH

N_DEV = 8
D = 1024
DEPTH = 4
EPS = 1e-6
GM_WIDTH = 2048
GM_HEADS = 8
GM_HEAD_DIM = 256
GM_BLOCK = 128
CHUNK = 64
HG_HEADS = 8
HG_DIM = 128
FFN_HIDDEN = 2816
ADA_COLS = 6 * D // N_DEV

HG_SUB = 32
HG_PAIR = 8
HG_TOKENS = 128

ADAM_LR = 0.001
ADAM_B1 = 0.9
ADAM_B2 = 0.999
ADAM_EPS = 1e-08
ADAM_WD = 0.01
ADAM_STEP = 10

V7X_VMEM_LIMIT = 56 * 1024 * 1024
LANES = 128


def _cparams(*sem):
    return pltpu.CompilerParams(dimension_semantics=sem or None, vmem_limit_bytes=V7X_VMEM_LIMIT)


def _tile(n, target, mult=LANES):
    best = None
    for t in range(mult, min(n, target) + 1, mult):
        if n % t == 0:
            best = t
    return best or n


WEIGHT_BLOCK_BYTES = 6 * 1024 * 1024


def _weight_tile(n, k):
    return _tile(n, max(LANES, WEIGHT_BLOCK_BYTES // (2 * k)))


def _gelu(x):
    return 0.5 * x * (1.0 + lax.erf(x * (1.0 / math.sqrt(2.0))))


def _mesh_pos():
    return lax.axis_index("x"), lax.axis_index("y"), lax.axis_index("c")


def _flat(pos):
    return 4 * pos[0] + 2 * pos[1] + pos[2]


def _peer(pos, k):
    return ((1 - pos[0]) if k & 4 else pos[0], (1 - pos[1]) if k & 2 else pos[1], (1 - pos[2]) if k & 1 else pos[2])


def _exchange_copies(ins, outs, send_sems, recv_sems, local_sems, gather):
    pos = _mesh_pos()
    me = _flat(pos)

    def src(i, dest):
        if gather:
            return ins[i]
        ref, rows = ins[i] if isinstance(ins[i], tuple) else (ins[i], None)
        return ref.at[dest] if rows is None else ref.at[dest, pl.ds(*rows)]

    local = [pltpu.make_async_copy(src(i, me), outs[i].at[me], local_sems.at[i]) for i in range(len(ins))]
    sends, recvs = [], []
    for k in range(1, N_DEV):
        peer = _peer(pos, k)
        there = _flat(peer)
        for i in range(len(ins)):
            sems = dict(send_sem=send_sems.at[i * 7 + k - 1], recv_sem=recv_sems.at[i * 7 + k - 1],
                        device_id=peer, device_id_type=MESH)
            sends.append(pltpu.make_async_remote_copy(src_ref=src(i, there), dst_ref=outs[i].at[me], **sems))
            recvs.append(pltpu.make_async_remote_copy(src_ref=src(i, there), dst_ref=outs[i].at[there], **sems))
    return local, sends, recvs


def _exchange_start(*refs):
    local, sends, _ = _exchange_copies(*refs)
    for cp in local + sends:
        cp.start()


def _exchange_wait(*refs):
    local, sends, recvs = _exchange_copies(*refs)
    for cp in recvs:
        cp.wait_recv()
    for cp in sends:
        cp.wait_send()
    for cp in local:
        cp.wait()


OTHER_CHIPS = (2, 4, 6)


def _relay_copies(ins, outs, send_sems, recv_sems, local_sems):
    pos = _mesh_pos()
    me = _flat(pos)
    sibling = _peer(pos, 1)
    local = [pltpu.make_async_copy(ins[i], outs[i].at[me], local_sems.at[i]) for i in range(len(ins))]
    first, passes, recvs = [], {k: [] for k in OTHER_CHIPS}, {k: [] for k in range(1, N_DEV)}
    for i in range(len(ins)):
        def copy(k, src, block, to):
            return pltpu.make_async_remote_copy(
                src_ref=src, dst_ref=outs[i].at[block], send_sem=send_sems.at[i * 7 + k - 1],
                recv_sem=recv_sems.at[i * 7 + k - 1], device_id=to, device_id_type=MESH)

        for k in (1,) + OTHER_CHIPS:
            first.append(copy(k, ins[i], me, _peer(pos, k)))
        for k in OTHER_CHIPS:
            there = _flat(_peer(pos, k))
            passes[k].append(copy(k ^ 1, outs[i].at[there], there, sibling))
        for k in range(1, N_DEV):
            there = _flat(_peer(pos, k))
            recvs[k].append(copy(k, ins[i], there, _peer(pos, k)))
    return local, first, passes, recvs


def _relay_start(ins, outs, *sems):
    local, first, _, _ = _relay_copies(ins, outs, *sems)
    for cp in local + first:
        cp.start()


def _relay_wait(ins, outs, *sems):
    local, first, passes, recvs = _relay_copies(ins, outs, *sems)
    for k in OTHER_CHIPS:
        for cp in recvs[k]:
            cp.wait_recv()
        for cp in passes[k]:
            cp.start()
    for k in (1, 3, 5, 7):
        for cp in recvs[k]:
            cp.wait_recv()
    for cp in first + [cp for k in OTHER_CHIPS for cp in passes[k]]:
        cp.wait_send()
    for cp in local:
        cp.wait()


def _exchange_out_shape(a, gather):
    return jax.ShapeDtypeStruct((N_DEV,) + tuple(a.shape) if gather else tuple(a.shape), a.dtype)


def _exchange_sems(n):
    return [pltpu.SemaphoreType.DMA((7 * n,)), pltpu.SemaphoreType.DMA((7 * n,)), pltpu.SemaphoreType.DMA((n,))]


ANY = pl.BlockSpec(memory_space=pl.ANY)


def _exchange(arrs, gather, name, relay=False):
    n = len(arrs)

    def body(*refs):
        ins, outs = refs[:n], refs[n:2 * n]
        if relay:
            _relay_start(ins, outs, *refs[2 * n:])
            _relay_wait(ins, outs, *refs[2 * n:])
        else:
            _exchange_start(ins, outs, *refs[2 * n:], gather)
            _exchange_wait(ins, outs, *refs[2 * n:], gather)

    return pl.pallas_call(
        body, name=name, out_shape=tuple(_exchange_out_shape(a, gather) for a in arrs),
        in_specs=[ANY] * n, out_specs=tuple([ANY] * n), scratch_shapes=_exchange_sems(n),
    )(*arrs)


def all_gather(arrs, name, relay=False):
    return _exchange(arrs, True, name, relay)


def all_to_all(arrs, name):
    return _exchange(arrs, False, name)


class Riders:
    def __init__(self, arrs, gather):
        self.gather = gather
        self.rows = [a[1] if isinstance(a, tuple) else None for a in arrs]
        self.arrs = [a[0] if isinstance(a, tuple) else a for a in arrs]

    def out_shapes(self):
        shapes = []
        for a, rows in zip(self.arrs, self.rows):
            shape = tuple(a.shape) if rows is None else (a.shape[0], rows[1], a.shape[2])
            shapes.append(jax.ShapeDtypeStruct((N_DEV,) + shape if self.gather else shape, a.dtype))
        return shapes


def _call(body, *, name, grid, in_specs, out_specs, out_shape, semantics, scratch_shapes=(), riders=None):
    if riders is None or not riders.arrs:
        return pl.pallas_call(body, name=name, grid=grid, in_specs=in_specs, out_specs=tuple(out_specs),
                              out_shape=tuple(out_shape), scratch_shapes=list(scratch_shapes),
                              compiler_params=_cparams(*semantics))
    n_in, n_out, n_scr, n_r = len(in_specs), len(out_specs), len(scratch_shapes), len(riders.arrs)
    gather = riders.gather

    def hosted(*refs):
        ins, r_ins = refs[:n_in], refs[n_in:n_in + n_r]
        at = n_in + n_r
        outs, r_outs = refs[at:at + n_out], refs[at + n_out:at + n_out + n_r]
        at += n_out + n_r
        scratch, sems = refs[at:at + n_scr], refs[at + n_scr:]
        first = functools.reduce(jnp.logical_and, [pl.program_id(a) == 0 for a in range(len(grid))])
        last = functools.reduce(jnp.logical_and, [pl.program_id(a) == grid[a] - 1 for a in range(len(grid))])

        r_ins = [(ref, rows) if rows is not None else ref for ref, rows in zip(r_ins, riders.rows)]

        @pl.when(first)
        def _():
            if gather:
                _relay_start(r_ins, r_outs, *sems)
            else:
                _exchange_start(r_ins, r_outs, *sems, gather)

        body(*ins, *outs, *scratch)

        @pl.when(last)
        def _():
            if gather:
                _relay_wait(r_ins, r_outs, *sems)
            else:
                _exchange_wait(r_ins, r_outs, *sems, gather)

    call = pl.pallas_call(
        hosted, name=name, grid=grid, in_specs=list(in_specs) + [ANY] * n_r, out_specs=tuple(out_specs) + (ANY,) * n_r,
        out_shape=tuple(out_shape) + tuple(riders.out_shapes()),
        scratch_shapes=list(scratch_shapes) + _exchange_sems(n_r),
        compiler_params=_cparams(*(("arbitrary",) * len(grid))))
    return lambda *args: call(*args, *riders.arrs)


def _shards_per_step(b):
    _, k, n = b.shape
    best = None
    for q in (1, 2, 4, 8):
        if (q * n) % LANES == 0 and (best is None or 2 * k * q * n <= WEIGHT_BLOCK_BYTES):
            best = q
    return best


def mm_nn(a, b, out_dtype, name, tm=512, riders=None):
    m, k = a.shape
    tm = _tile(m, tm, 8)
    if b.ndim == 3:
        shard = b.shape[2]
        n = N_DEV * shard
        per_step = _shards_per_step(b)
        tn = per_step * shard
        b_spec = pl.BlockSpec((per_step, k, shard), lambda i, j: (j, 0, 0))

        def body(a_ref, b_ref, o_ref):
            for q in range(per_step):
                o_ref[:, q * shard:(q + 1) * shard] = jnp.dot(a_ref[...], b_ref[q],
                                                              preferred_element_type=f32).astype(o_ref.dtype)
    else:
        n = b.shape[1]
        tn = _weight_tile(n, k)
        b_spec = pl.BlockSpec((k, tn), lambda i, j: (0, j))

        def body(a_ref, b_ref, o_ref):
            o_ref[...] = jnp.dot(a_ref[...], b_ref[...], preferred_element_type=f32).astype(o_ref.dtype)

    return _call(
        body, name=name, grid=(m // tm, n // tn),
        in_specs=[pl.BlockSpec((tm, k), lambda i, j: (i, 0)), b_spec],
        out_specs=[pl.BlockSpec((tm, tn), lambda i, j: (i, j))],
        out_shape=[jax.ShapeDtypeStruct((m, n), out_dtype)], semantics=("parallel", "parallel"), riders=riders,
    )(a, b)


def mm_nn_residual(a, b, x, gate, norm, name, tm=512, riders=None):
    m, k = a.shape
    n = b.shape[1]
    tm = _tile(m, tm, 8)

    def body(a_ref, b_ref, x_ref, g_ref, *rest):
        if norm is not None:
            gn_ref, sc_ref, sh_ref, y_ref, o_ref, h_ref = rest
        else:
            y_ref, o_ref = rest
        y = jnp.dot(a_ref[...], b_ref[...], preferred_element_type=f32)
        y_ref[...] = y.astype(bf16)
        x_new = x_ref[...] + g_ref[...] * y
        o_ref[...] = x_new
        if norm is not None:
            h_ref[...] = _norm_fn(x_new, gn_ref[...], sc_ref[...], sh_ref[...]).astype(bf16)

    blk = pl.BlockSpec((tm, n), lambda i: (i, 0))
    in_specs = [pl.BlockSpec((tm, k), lambda i: (i, 0)), pl.BlockSpec((k, n), lambda i: (0, 0)), blk, _row(n)]
    out_specs = [blk, blk]
    out_shape = [jax.ShapeDtypeStruct((m, n), bf16), jax.ShapeDtypeStruct((m, n), f32)]
    args = [a, b, x, gate]
    if norm is not None:
        in_specs += [_row(n)] * 3
        out_specs += [blk]
        out_shape += [jax.ShapeDtypeStruct((m, n), bf16)]
        args += list(norm)
    return _call(body, name=name, grid=(m // tm,), in_specs=in_specs, out_specs=out_specs, out_shape=out_shape,
                 semantics=("parallel",), riders=riders)(*args)


def mm_nt(a, b, out_dtype, name, tm=512, riders=None):
    m = a.shape[0]
    tm = _tile(m, tm, 8)
    if b.ndim == 3:
        _, k, shard = b.shape
        tk = k
        b_spec = pl.BlockSpec(b.shape, lambda i, j: (0, 0, 0))
        width = N_DEV * shard

        def body(a_ref, b_ref, o_ref):
            acc = None
            for q in range(N_DEV):
                part = lax.dot_general(a_ref[:, q * shard:(q + 1) * shard], b_ref[q], (((1,), (1,)), ((), ())),
                                       preferred_element_type=f32)
                acc = part if acc is None else acc + part
            o_ref[...] = acc.astype(o_ref.dtype)
    else:
        k, width = b.shape
        tk = _weight_tile(k, width)
        b_spec = pl.BlockSpec((tk, width), lambda i, j: (j, 0))

        def body(a_ref, b_ref, o_ref):
            o_ref[...] = lax.dot_general(a_ref[...], b_ref[...], (((1,), (1,)), ((), ())),
                                         preferred_element_type=f32).astype(o_ref.dtype)

    return _call(
        body, name=name, grid=(m // tm, k // tk),
        in_specs=[pl.BlockSpec((tm, width), lambda i, j: (i, 0)), b_spec],
        out_specs=[pl.BlockSpec((tm, tk), lambda i, j: (i, j))],
        out_shape=[jax.ShapeDtypeStruct((m, k), out_dtype)], semantics=("parallel", "parallel"), riders=riders,
    )(a, b)


def mm_tn(a, b, out_dtype, name, tm=512, tn=512, riders=None):
    t, m = a.shape
    n = b.shape[1]
    tm, tn = _tile(m, tm), _tile(n, tn)

    def body(a_ref, b_ref, o_ref):
        o_ref[...] = lax.dot_general(a_ref[...], b_ref[...], (((0,), (0,)), ((), ())),
                                     preferred_element_type=f32).astype(o_ref.dtype)

    return _call(
        body, name=name, grid=(m // tm, n // tn),
        in_specs=[pl.BlockSpec((t, tm), lambda i, j: (0, i)), pl.BlockSpec((t, tn), lambda i, j: (0, j))],
        out_specs=[pl.BlockSpec((tm, tn), lambda i, j: (i, j))],
        out_shape=[jax.ShapeDtypeStruct((m, n), out_dtype)], semantics=("parallel", "parallel"), riders=riders,
    )(a, b)


def mm_tn_by_owner(a, b, name, tm=512, riders=None):
    t, m = a.shape
    n = b.shape[1]
    shard = n // N_DEV
    per_step = 1 if shard % LANES == 0 else 2
    assert (per_step * shard) % LANES == 0
    tm = _tile(m, tm)

    def body(a_ref, b_ref, o_ref):
        acc = lax.dot_general(a_ref[...], b_ref[...], (((0,), (0,)), ((), ())), preferred_element_type=f32)
        for q in range(per_step):
            o_ref[q] = acc[:, q * shard:(q + 1) * shard].astype(bf16)

    return _call(
        body, name=name, grid=(m // tm, N_DEV // per_step),
        in_specs=[pl.BlockSpec((t, tm), lambda i, j: (0, i)), pl.BlockSpec((t, per_step * shard), lambda i, j: (0, j))],
        out_specs=[pl.BlockSpec((per_step, tm, shard), lambda i, j: (j, i, 0))],
        out_shape=[jax.ShapeDtypeStruct((N_DEV, m, shard), bf16)], semantics=("parallel", "parallel"), riders=riders,
    )(a, b)


def _norm_fn(x, gn, sc, sh):
    r = lax.rsqrt(jnp.mean(x * x, axis=-1, keepdims=True) + EPS)
    return (x * r * gn) * (1.0 + sc) + sh


def _row(d):
    return pl.BlockSpec((1, d), lambda i: (0, 0))


def norm_fwd(x, gn, sc, sh, name, tm=512):
    t, d = x.shape
    tm = _tile(t, tm, 8)

    def body(x_ref, gn_ref, sc_ref, sh_ref, h_ref):
        h_ref[...] = _norm_fn(x_ref[...], gn_ref[...], sc_ref[...], sh_ref[...]).astype(bf16)

    return pl.pallas_call(
        body, name=name, grid=(t // tm,),
        in_specs=[pl.BlockSpec((tm, d), lambda i: (i, 0)), _row(d), _row(d), _row(d)],
        out_specs=pl.BlockSpec((tm, d), lambda i: (i, 0)),
        out_shape=jax.ShapeDtypeStruct((t, d), bf16), compiler_params=_cparams("parallel"),
    )(x, gn, sc, sh)


def _gate_bwd(dx, y_ref, g_ref, dgate_ref, dy_ref):
    dgate_ref[...] += jnp.sum(dx * y_ref[...].astype(f32), axis=0, keepdims=True)
    dy_ref[...] = (dx * g_ref[...]).astype(bf16)


def norm_bwd(x, gn, sc, sh, dh, dres, name, gate=None, tm=512, riders=None):
    t, d = x.shape
    tm = _tile(t, tm, 8)

    def body(x_ref, gn_ref, sc_ref, sh_ref, dh_ref, dres_ref, *rest):
        if gate is not None:
            y_ref, g_ref, dx_ref, dgn_ref, dsc_ref, dsh_ref, dgate_ref, dy_ref = rest
        else:
            dx_ref, dgn_ref, dsc_ref, dsh_ref = rest

        @pl.when(pl.program_id(0) == 0)
        def _():
            dgn_ref[...] = jnp.zeros_like(dgn_ref)
            dsc_ref[...] = jnp.zeros_like(dsc_ref)
            dsh_ref[...] = jnp.zeros_like(dsh_ref)
            if gate is not None:
                dgate_ref[...] = jnp.zeros_like(dgate_ref)

        _, vjp = jax.vjp(_norm_fn, x_ref[...], gn_ref[...], sc_ref[...], sh_ref[...])
        dx, dgn, dsc, dsh = vjp(dh_ref[...].astype(f32))
        dx = dx + dres_ref[...]
        dx_ref[...] = dx
        dgn_ref[...] += dgn
        dsc_ref[...] += dsc
        dsh_ref[...] += dsh
        if gate is not None:
            _gate_bwd(dx, y_ref, g_ref, dgate_ref, dy_ref)

    blk = pl.BlockSpec((tm, d), lambda i: (i, 0))
    vec = jax.ShapeDtypeStruct((1, d), f32)
    in_specs = [blk, _row(d), _row(d), _row(d), blk, blk]
    out_specs = [blk, _row(d), _row(d), _row(d)]
    out_shape = [jax.ShapeDtypeStruct((t, d), f32), vec, vec, vec]
    args = [x, gn, sc, sh, dh, dres]
    if gate is not None:
        in_specs += [blk, _row(d)]
        out_specs += [_row(d), blk]
        out_shape += [vec, jax.ShapeDtypeStruct((t, d), bf16)]
        args += list(gate)
    return _call(body, name=name, grid=(t // tm,), in_specs=in_specs, out_specs=out_specs, out_shape=out_shape,
                 semantics=("arbitrary",), riders=riders)(*args)


def _loss_fn(x, g, tgt):
    r = lax.rsqrt(jnp.mean(x * x, axis=-1, keepdims=True) + EPS)
    err = jnp.square(x * r * g - tgt)
    return 0.5 * jnp.sum(jnp.mean(err, axis=-1, keepdims=True), axis=0, keepdims=True)


def loss_head(x, g, tgt, y, gate, name, tm=512):
    t, d = x.shape
    tm = _tile(t, tm, 8)

    def body(x_ref, g_ref, t_ref, y_ref, gate_ref, loss_ref, dx_ref, dg_ref, dgate_ref, dy_ref):
        @pl.when(pl.program_id(0) == 0)
        def _():
            loss_ref[...] = jnp.zeros_like(loss_ref)
            dg_ref[...] = jnp.zeros_like(dg_ref)
            dgate_ref[...] = jnp.zeros_like(dgate_ref)

        loss, vjp = jax.vjp(_loss_fn, x_ref[...], g_ref[...], t_ref[...])
        dx, dg, _ = vjp(jnp.ones((1, 1), f32))
        dx_ref[...] = dx
        loss_ref[...] += loss
        dg_ref[...] += dg
        _gate_bwd(dx, y_ref, gate_ref, dgate_ref, dy_ref)

    blk = pl.BlockSpec((tm, d), lambda i: (i, 0))
    vec = jax.ShapeDtypeStruct((1, d), f32)
    return pl.pallas_call(
        body, name=name, grid=(t // tm,),
        in_specs=[blk, _row(d), blk, blk, _row(d)],
        out_specs=(pl.BlockSpec((1, 1), lambda i: (0, 0)), blk, _row(d), _row(d), blk),
        out_shape=(jax.ShapeDtypeStruct((1, 1), f32), jax.ShapeDtypeStruct((t, d), f32), vec, vec,
                   jax.ShapeDtypeStruct((t, d), bf16)),
        compiler_params=_cparams("arbitrary"),
    )(x, g, tgt, y, gate)


def _gm_block_fn(z, ws, bs, lng, lnb):
    u = _gelu(z[:, :GM_WIDTH])
    vg = _gelu(z[:, GM_WIDTH:])
    mu = jnp.mean(vg, axis=-1, keepdims=True)
    var = jnp.mean(jnp.square(vg - mu), axis=-1, keepdims=True)
    vn = (vg - mu) * lax.rsqrt(var + EPS) * lng + lnb
    row = lax.broadcasted_iota(jnp.int32, (GM_BLOCK, GM_BLOCK), 0) // CHUNK
    col = lax.broadcasted_iota(jnp.int32, (GM_BLOCK, GM_BLOCK), 1) // CHUNK
    parts = []
    for h in range(GM_HEADS):
        w = jnp.where(row >= col, ws[h], 0.0)
        cols = slice(h * GM_HEAD_DIM, (h + 1) * GM_HEAD_DIM)
        s = jnp.dot(w.astype(bf16), vn[:, cols].astype(bf16), preferred_element_type=f32) + bs[h]
        parts.append(u[:, cols] * s)
    return jnp.concatenate(parts, axis=1)


def _gm_param_specs():
    return [pl.BlockSpec((GM_HEADS, GM_BLOCK, GM_BLOCK), lambda i: (0, 0, 0)),
            pl.BlockSpec((GM_HEADS, GM_BLOCK, 1), lambda i: (0, 0, 0)), _row(GM_WIDTH), _row(GM_WIDTH)]


def gm_mix_fwd(z, ws, bs, lng, lnb, name, riders=None):
    t = z.shape[0]

    def body(z_ref, ws_ref, bs_ref, lng_ref, lnb_ref, o_ref):
        o_ref[...] = _gm_block_fn(z_ref[...].astype(f32), ws_ref[...], bs_ref[...], lng_ref[...],
                                  lnb_ref[...]).astype(bf16)

    return _call(
        body, name=name, grid=(t // GM_BLOCK,),
        in_specs=[pl.BlockSpec((GM_BLOCK, 2 * GM_WIDTH), lambda i: (i, 0))] + _gm_param_specs(),
        out_specs=[pl.BlockSpec((GM_BLOCK, GM_WIDTH), lambda i: (i, 0))],
        out_shape=[jax.ShapeDtypeStruct((t, GM_WIDTH), bf16)], semantics=("parallel",), riders=riders,
    )(z, ws, bs, lng, lnb)


def gm_mix_bwd(z, ws, bs, lng, lnb, dgated, name, riders=None):
    t = z.shape[0]

    def body(z_ref, ws_ref, bs_ref, lng_ref, lnb_ref, dg_ref, dz_ref, dws_ref, dbs_ref, dlng_ref, dlnb_ref):
        _, vjp = jax.vjp(_gm_block_fn, z_ref[...].astype(f32), ws_ref[...], bs_ref[...], lng_ref[...], lnb_ref[...])
        dz, dws, dbs, dlng, dlnb = vjp(dg_ref[...].astype(f32))
        dz_ref[...] = dz.astype(bf16)

        @pl.when(pl.program_id(0) == 0)
        def _():
            dws_ref[...] = jnp.zeros_like(dws_ref)
            dbs_ref[...] = jnp.zeros_like(dbs_ref)
            dlng_ref[...] = jnp.zeros_like(dlng_ref)
            dlnb_ref[...] = jnp.zeros_like(dlnb_ref)

        dws_ref[...] += dws
        dbs_ref[...] += dbs
        dlng_ref[...] += dlng
        dlnb_ref[...] += dlnb

    zblk = pl.BlockSpec((GM_BLOCK, 2 * GM_WIDTH), lambda i: (i, 0))
    return _call(
        body, name=name, grid=(t // GM_BLOCK,),
        in_specs=[zblk] + _gm_param_specs() + [pl.BlockSpec((GM_BLOCK, GM_WIDTH), lambda i: (i, 0))],
        out_specs=[zblk] + _gm_param_specs(),
        out_shape=[jax.ShapeDtypeStruct((t, 2 * GM_WIDTH), bf16),
                   jax.ShapeDtypeStruct((GM_HEADS, GM_BLOCK, GM_BLOCK), f32),
                   jax.ShapeDtypeStruct((GM_HEADS, GM_BLOCK, 1), f32),
                   jax.ShapeDtypeStruct((1, GM_WIDTH), f32), jax.ShapeDtypeStruct((1, GM_WIDTH), f32)],
        semantics=("arbitrary",), riders=riders,
    )(z, ws, bs, lng, lnb, dgated)


def _hg_block_fn(qp, fz, iv, gp, s0, lb, gn):
    n, ns, d = HG_SUB, HG_TOKENS // HG_SUB, HG_DIM
    p, nb, per_sub = HG_PAIR, HG_TOKENS // HG_PAIR, HG_SUB // HG_PAIR
    f = lb + (1.0 - lb) * jax.nn.sigmoid(fz)
    g = jnp.log(f)
    k = 1.0 - f
    q = qp * jax.nn.sigmoid(qp)
    v = iv.astype(bf16)
    row = lax.broadcasted_iota(jnp.int32, (HG_TOKENS, HG_TOKENS), 0)
    col = lax.broadcasted_iota(jnp.int32, (HG_TOKENS, HG_TOKENS), 1)
    same_sub = col // n == row // n
    tri = ((col <= row) & same_sub).astype(f32)
    cum = jnp.dot(tri, g, precision=lax.Precision.HIGHEST, preferred_element_type=f32)
    cum_b, q_b, k_b = cum.reshape(nb, p, d), q.reshape(nb, p, d), k.reshape(nb, p, d)
    j_b = lax.broadcasted_iota(jnp.int32, (nb, p, d), 1)
    first = (row // p) * p
    scores_t = jnp.zeros((HG_TOKENS, HG_TOKENS), f32)
    for i in range(p):
        rel = jnp.where(j_b <= i, cum_b[:, i:i + 1, :] - cum_b, -1e30)
        pair = jnp.sum(q_b[:, i:i + 1, :] * k_b * jnp.exp(rel), axis=2, keepdims=True)
        scores_t = scores_t + jnp.where(col == first + i, pair.reshape(HG_TOKENS, 1), 0.0)
    o = lax.dot_general(scores_t.astype(bf16), v, (((0,), (0,)), ((), ())), preferred_element_type=f32)
    last = cum_b[:, p - 1:p, :]
    before = jnp.concatenate([jnp.zeros((1, 1, d), f32), last[:-1]], axis=0)
    before = jnp.broadcast_to(before, (nb, p, d)).reshape(HG_TOKENS, d)
    block = (lax.broadcasted_iota(jnp.int32, (HG_TOKENS, d), 0) // p) % per_sub
    q_late = q * jnp.exp(jnp.where(block > 0, cum - before, -1e30))
    last_s = last.reshape(ns, per_sub, d)
    q_parts, k_parts = [], []
    for m in range(1, per_sub):
        split = jnp.broadcast_to(last_s[:, m - 1:m, :], (ns, n, d)).reshape(HG_TOKENS, d)
        k_parts.append(k * jnp.exp(jnp.where(block < m, split - cum, -1e30)))
        q_parts.append(jnp.where(block == m, q_late, 0.0))
    scores = lax.dot_general(jnp.concatenate(q_parts, axis=1).astype(bf16), jnp.concatenate(k_parts, axis=1).astype(bf16),
                             (((1,), (1,)), ((), ())), preferred_element_type=f32)
    o = o + jnp.dot(jnp.where(same_sub, scores, 0.0).astype(bf16), v, preferred_element_type=f32)
    cum_s = cum.reshape(ns, n, d)
    tot = cum_s[:, n - 1:n, :]
    kt_t = (k.reshape(ns, n, d) * jnp.exp(tot - cum_s)).reshape(HG_TOKENS, d).T
    lane_sub = lax.broadcasted_iota(jnp.int32, (d, HG_TOKENS), 1) // n
    k_by_sub = jnp.concatenate([jnp.where(lane_sub == b, kt_t, 0.0) for b in range(ns)], axis=0).astype(bf16)
    update = jnp.dot(k_by_sub, v, preferred_element_type=f32)
    decay = jnp.exp(tot.reshape(ns, d)).T
    state = s0
    states = []
    for a in range(ns):
        states.append(state.astype(bf16))
        state = decay[:, a:a + 1] * state + update[a * d:(a + 1) * d]
    qt = q * jnp.exp(cum)
    row_sub = lax.broadcasted_iota(jnp.int32, (HG_TOKENS, d), 0) // n
    q_by_sub = jnp.concatenate([jnp.where(row_sub == a, qt, 0.0) for a in range(ns)], axis=1).astype(bf16)
    o = o + jnp.dot(q_by_sub, jnp.concatenate(states, axis=0), preferred_element_type=f32)
    on = o * lax.rsqrt(jnp.mean(o * o, axis=-1, keepdims=True) + EPS) * gn
    return on * (gp * jax.nn.sigmoid(gp)), state


def _head_parts(ref, h):
    return [ref[:, p * D + h * HG_DIM:p * D + (h + 1) * HG_DIM] for p in range(4)]


def hg_scan_fwd(proj, lb, gn, name, riders=None):
    t = proj.shape[0]
    nt = t // HG_TOKENS

    def body(p_ref, lb_ref, gn_ref, y_ref, s_ref, state):
        @pl.when(pl.program_id(0) == 0)
        def _():
            state[...] = jnp.zeros_like(state)

        for h in range(HG_HEADS):
            cols = slice(h * HG_DIM, (h + 1) * HG_DIM)
            s_ref[h, 0] = state[h]
            y, s1 = _hg_block_fn(*_head_parts(p_ref, h), state[h], lb_ref[:, cols], gn_ref[:, cols])
            y_ref[:, cols] = y.astype(bf16)
            state[h] = s1

    return _call(
        body, name=name, grid=(nt,),
        in_specs=[pl.BlockSpec((HG_TOKENS, 4 * D), lambda i: (i, 0)), _row(D), _row(D)],
        out_specs=[pl.BlockSpec((HG_TOKENS, D), lambda i: (i, 0)),
                   pl.BlockSpec((HG_HEADS, 1, HG_DIM, HG_DIM), lambda i: (0, i, 0, 0))],
        out_shape=[jax.ShapeDtypeStruct((t, D), bf16), jax.ShapeDtypeStruct((HG_HEADS, nt, HG_DIM, HG_DIM), f32)],
        scratch_shapes=[pltpu.VMEM((HG_HEADS, HG_DIM, HG_DIM), f32)],
        semantics=("arbitrary",), riders=riders,
    )(proj, lb, gn)


def hg_scan_bwd(proj, lb, gn, states, dy, name, riders=None):
    t = proj.shape[0]
    nt = t // HG_TOKENS

    def body(p_ref, lb_ref, gn_ref, s_ref, dy_ref, dp_ref, dlb_ref, dgn_ref, dstate):
        @pl.when(pl.program_id(0) == 0)
        def _():
            dstate[...] = jnp.zeros_like(dstate)
            dlb_ref[...] = jnp.zeros_like(dlb_ref)
            dgn_ref[...] = jnp.zeros_like(dgn_ref)

        for h in range(HG_HEADS):
            cols = slice(h * HG_DIM, (h + 1) * HG_DIM)
            _, vjp = jax.vjp(_hg_block_fn, *_head_parts(p_ref, h), s_ref[h, 0], lb_ref[:, cols], gn_ref[:, cols])
            grads = vjp((dy_ref[:, cols].astype(f32), dstate[h]))
            for p in range(4):
                dp_ref[:, p * D + h * HG_DIM:p * D + (h + 1) * HG_DIM] = grads[p].astype(bf16)
            dstate[h] = grads[4]
            dlb_ref[:, cols] += grads[5]
            dgn_ref[:, cols] += grads[6]

    small = jax.ShapeDtypeStruct((1, D), f32)
    return _call(
        body, name=name, grid=(nt,),
        in_specs=[pl.BlockSpec((HG_TOKENS, 4 * D), lambda i: (nt - 1 - i, 0)), _row(D), _row(D),
                  pl.BlockSpec((HG_HEADS, 1, HG_DIM, HG_DIM), lambda i: (0, nt - 1 - i, 0, 0)),
                  pl.BlockSpec((HG_TOKENS, D), lambda i: (nt - 1 - i, 0))],
        out_specs=[pl.BlockSpec((HG_TOKENS, 4 * D), lambda i: (nt - 1 - i, 0)), _row(D), _row(D)],
        out_shape=[jax.ShapeDtypeStruct((t, 4 * D), bf16), small, small],
        scratch_shapes=[pltpu.VMEM((HG_HEADS, HG_DIM, HG_DIM), f32)],
        semantics=("arbitrary",), riders=riders,
    )(proj, lb, gn, states, dy)


FFN_COLS = 1408
HALO = 8
STRIP = 16


def _ffn_specs(tm):
    nb = tm // HALO
    main_g = pl.BlockSpec((tm, FFN_COLS), lambda j, i: (i, j))
    main_v = pl.BlockSpec((tm, FFN_COLS), lambda j, i: (i, j + 2))
    halo_g = pl.BlockSpec((HALO, FFN_COLS), lambda j, i: (jnp.maximum(i * nb - 1, 0), j))
    halo_v = pl.BlockSpec((HALO, FFN_COLS), lambda j, i: (jnp.maximum(i * nb - 1, 0), j + 2))
    w_g = pl.BlockSpec((3, FFN_COLS), lambda j, i: (0, j))
    w_v = pl.BlockSpec((3, FFN_COLS), lambda j, i: (0, j + 2))
    b_g = pl.BlockSpec((1, FFN_COLS), lambda j, i: (0, j))
    b_v = pl.BlockSpec((1, FFN_COLS), lambda j, i: (0, j + 2))
    return [main_g, halo_g, main_v, halo_v, w_g, w_v, b_g, b_v]


def _strip_rows(r):
    return pl.ds(r * STRIP, STRIP) if isinstance(r, int) else pl.ds(pl.multiple_of(r * STRIP, STRIP), STRIP)


def _for_strips(nstrip, strip, reverse=False):
    if reverse:
        strip(nstrip - 1, True)
        lax.fori_loop(0, nstrip - 1, lambda k, c: (strip(nstrip - 2 - k, False), c)[1], 0)
    else:
        strip(0, True)
        lax.fori_loop(1, nstrip, lambda r, c: (strip(r, False), c)[1], 0)


def _conv_strip(main_ref, halo_ref, w_ref, b_ref, r, edge, cols, rowi):
    cur = main_ref[_strip_rows(r), cols].astype(f32)
    if edge:
        h = jnp.where(pl.program_id(1) == 0, 0.0, halo_ref[:, cols].astype(f32))
        prev = jnp.concatenate([jnp.zeros_like(h), h], axis=0)
    else:
        prev = main_ref[_strip_rows(r - 1), cols].astype(f32)
    a1 = jnp.where(rowi < 1, pltpu.roll(prev, 1, axis=0), pltpu.roll(cur, 1, axis=0))
    a2 = jnp.where(rowi < 2, pltpu.roll(prev, 2, axis=0), pltpu.roll(cur, 2, axis=0))
    y = b_ref[:, cols] + w_ref[0:1, cols] * a2 + w_ref[1:2, cols] * a1 + w_ref[2:3, cols] * cur
    return y, (cur, a1, a2)


def ffn_gate_fwd(a, cw, cb, name, tm=512, riders=None):
    t = a.shape[0]
    tm = _tile(t, tm, STRIP)

    def body(ag_ref, hg_ref, av_ref, hv_ref, wg_ref, wv_ref, bg_ref, bv_ref, o_ref):
        rowi = lax.broadcasted_iota(jnp.int32, (STRIP, LANES), 0)

        def strip(r, edge):
            for c in range(FFN_COLS // LANES):
                cols = pl.ds(c * LANES, LANES)
                yg, _ = _conv_strip(ag_ref, hg_ref, wg_ref, bg_ref, r, edge, cols, rowi)
                yv, _ = _conv_strip(av_ref, hv_ref, wv_ref, bv_ref, r, edge, cols, rowi)
                o_ref[_strip_rows(r), cols] = (_gelu(yg) * yv).astype(bf16)

        _for_strips(tm // STRIP, strip)

    return _call(
        body, name=name, grid=(2, t // tm), in_specs=_ffn_specs(tm),
        out_specs=[pl.BlockSpec((tm, FFN_COLS), lambda j, i: (i, j))],
        out_shape=[jax.ShapeDtypeStruct((t, FFN_HIDDEN), bf16)],
        semantics=("parallel", "arbitrary"), riders=riders,
    )(a, a, a, a, cw, cw, cb, cb)


def ffn_gate_bwd(a, cw, cb, dhid, name, tm=512, riders=None):
    t = a.shape[0]
    tm = _tile(t, tm, STRIP)

    def body(ag_ref, hg_ref, av_ref, hv_ref, wg_ref, wv_ref, bg_ref, bv_ref, dh_ref,
             dy_ref, dwg_ref, dwv_ref, dbg_ref, dbv_ref, acc):
        rowi = lax.broadcasted_iota(jnp.int32, (STRIP, LANES), 0)

        @pl.when(pl.program_id(1) == 0)
        def _():
            acc[...] = jnp.zeros_like(acc)

        def strip(r, edge):
            rows = _strip_rows(r)
            for c in range(FFN_COLS // LANES):
                cols = pl.ds(c * LANES, LANES)
                yg, taps_g = _conv_strip(ag_ref, hg_ref, wg_ref, bg_ref, r, edge, cols, rowi)
                yv, taps_v = _conv_strip(av_ref, hv_ref, wv_ref, bv_ref, r, edge, cols, rowi)
                dh = dh_ref[rows, cols].astype(f32)
                cdf = 0.5 * (1.0 + lax.erf(yg * (1.0 / math.sqrt(2.0))))
                pdf = jnp.exp(-0.5 * yg * yg) * (1.0 / math.sqrt(2.0 * math.pi))
                dyg = dh * yv * (cdf + yg * pdf)
                dyv = dh * (yg * cdf)
                dy_ref[0, rows, cols] = dyg.astype(bf16)
                dy_ref[1, rows, cols] = dyv.astype(bf16)
                for p, (dy, (a0, a1, a2)) in enumerate(((dyg, taps_g), (dyv, taps_v))):
                    acc[4 * p + 0, :, cols] += dy * a2
                    acc[4 * p + 1, :, cols] += dy * a1
                    acc[4 * p + 2, :, cols] += dy * a0
                    acc[4 * p + 3, :, cols] += dy

        _for_strips(tm // STRIP, strip)

        @pl.when(pl.program_id(1) == pl.num_programs(1) - 1)
        def _():
            for p, (dw_ref, db_ref) in enumerate(((dwg_ref, dbg_ref), (dwv_ref, dbv_ref))):
                for tap in range(3):
                    dw_ref[tap:tap + 1, :] = jnp.sum(acc[4 * p + tap], axis=0, keepdims=True)
                db_ref[...] = jnp.sum(acc[4 * p + 3], axis=0, keepdims=True)

    half_w = pl.BlockSpec((3, FFN_COLS), lambda j, i: (0, j))
    half_b = pl.BlockSpec((1, FFN_COLS), lambda j, i: (0, j))
    return _call(
        body, name=name, grid=(2, t // tm),
        in_specs=_ffn_specs(tm) + [pl.BlockSpec((tm, FFN_COLS), lambda j, i: (i, j))],
        out_specs=[pl.BlockSpec((2, tm, FFN_COLS), lambda j, i: (0, i, j)), half_w, half_w, half_b, half_b],
        out_shape=[jax.ShapeDtypeStruct((2, t, FFN_HIDDEN), bf16),
                   jax.ShapeDtypeStruct((3, FFN_HIDDEN), f32), jax.ShapeDtypeStruct((3, FFN_HIDDEN), f32),
                   jax.ShapeDtypeStruct((1, FFN_HIDDEN), f32), jax.ShapeDtypeStruct((1, FFN_HIDDEN), f32)],
        scratch_shapes=[pltpu.VMEM((8, STRIP, FFN_COLS), f32)],
        semantics=("parallel", "arbitrary"), riders=riders,
    )(a, a, a, a, cw, cw, cb, cb, dhid)


def conv_transpose(dy, cw, name, tm=512):
    _, t, fh = dy.shape
    tm = _tile(t, tm, STRIP)
    nb = tm // HALO
    last_halo = t // HALO - 1
    ncol = fh // FFN_COLS

    def body(main_ref, halo_ref, w_ref, o_ref):
        rowi = lax.broadcasted_iota(jnp.int32, (STRIP, LANES), 0)
        last_block = pl.program_id(2) == pl.num_programs(2) - 1

        def strip(r, edge):
            rows = _strip_rows(r)
            for c in range(FFN_COLS // LANES):
                cols = pl.ds(c * LANES, LANES)
                cur = main_ref[0, rows, cols].astype(f32)
                if edge:
                    h = jnp.where(last_block, 0.0, halo_ref[0, :, cols].astype(f32))
                    nxt = jnp.concatenate([h, jnp.zeros_like(h)], axis=0)
                else:
                    nxt = main_ref[0, _strip_rows(r + 1), cols].astype(f32)
                d1 = jnp.where(rowi >= STRIP - 1, pltpu.roll(nxt, STRIP - 1, axis=0), pltpu.roll(cur, STRIP - 1, axis=0))
                d2 = jnp.where(rowi >= STRIP - 2, pltpu.roll(nxt, STRIP - 2, axis=0), pltpu.roll(cur, STRIP - 2, axis=0))
                o_ref[rows, cols] = (w_ref[2:3, cols] * cur + w_ref[1:2, cols] * d1 + w_ref[0:1, cols] * d2).astype(bf16)

        _for_strips(tm // STRIP, strip, reverse=True)

    return pl.pallas_call(
        body, name=name, grid=(2, ncol, t // tm),
        in_specs=[pl.BlockSpec((1, tm, FFN_COLS), lambda p, j, i: (p, i, j)),
                  pl.BlockSpec((1, HALO, FFN_COLS), lambda p, j, i: (p, jnp.minimum((i + 1) * nb, last_halo), j)),
                  pl.BlockSpec((3, FFN_COLS), lambda p, j, i: (0, p * ncol + j))],
        out_specs=pl.BlockSpec((tm, FFN_COLS), lambda p, j, i: (i, p * ncol + j)),
        out_shape=jax.ShapeDtypeStruct((t, 2 * fh), bf16),
        compiler_params=_cparams("parallel", "parallel", "arbitrary"),
    )(dy, dy, cw)


def ada_mod(c_all, ada_w, ada_b_cols, name):
    cols = ada_w.shape[2]

    def body(c_ref, w_ref, b_ref, o_ref):
        c = c_ref[...]
        cond = (c * jax.nn.sigmoid(c)).astype(bf16)
        o_ref[0] = jnp.dot(cond, w_ref[0].astype(bf16), preferred_element_type=f32) + b_ref[0]

    return pl.pallas_call(
        body, name=name, grid=(DEPTH,),
        in_specs=[pl.BlockSpec((N_DEV, D), lambda i: (0, 0)), pl.BlockSpec((1, D, cols), lambda i: (i, 0, 0)),
                  pl.BlockSpec((1, 1, cols), lambda i: (i, 0, 0))],
        out_specs=pl.BlockSpec((1, N_DEV, cols), lambda i: (i, 0, 0)),
        out_shape=jax.ShapeDtypeStruct((DEPTH, N_DEV, cols), f32), compiler_params=_cparams("parallel"),
    )(c_all, ada_w, ada_b_cols)


def ada_grads(c_all, dmod_cols, dmod_all, name):
    cols = dmod_cols.shape[2]

    def body(c_ref, dm_ref, da_ref, dw_ref, db_ref):
        c = c_ref[...]
        cond = c * jax.nn.sigmoid(c)
        dw_ref[0] = lax.dot_general(cond, dm_ref[0], (((0,), (0,)), ((), ())), precision=lax.Precision.HIGHEST,
                                    preferred_element_type=f32)
        acc = da_ref[0, 0]
        for e in range(1, N_DEV):
            acc = acc + da_ref[e, 0]
        db_ref[0] = acc

    return pl.pallas_call(
        body, name=name, grid=(DEPTH,),
        in_specs=[pl.BlockSpec((N_DEV, D), lambda i: (0, 0)), pl.BlockSpec((1, N_DEV, cols), lambda i: (i, 0, 0)),
                  pl.BlockSpec((N_DEV, 1, 1, 6 * D), lambda i: (0, i, 0, 0))],
        out_specs=(pl.BlockSpec((1, D, cols), lambda i: (i, 0, 0)), pl.BlockSpec((1, 1, 6 * D), lambda i: (i, 0, 0))),
        out_shape=(jax.ShapeDtypeStruct((DEPTH, D, cols), f32), jax.ShapeDtypeStruct((DEPTH, 1, 6 * D), f32)),
        compiler_params=_cparams("parallel"),
    )(c_all, dmod_cols, dmod_all)


def lower_bound_fwd(hg_lb, name):
    n = hg_lb.shape[1]

    def body(l_ref, o_ref):
        o_ref[...] = jax.nn.sigmoid(l_ref[1:2, :] - l_ref[0:1, :])

    return pl.pallas_call(body, name=name, out_shape=jax.ShapeDtypeStruct((1, n), f32))(hg_lb)


def lower_bound_bwd(hg_lb, dlb, name):
    n = hg_lb.shape[1]

    def body(l_ref, d_ref, o_ref):
        p = jax.nn.sigmoid(l_ref[1:2, :] - l_ref[0:1, :])
        g = d_ref[...] * p * (1.0 - p)
        o_ref[0:1, :] = -g
        o_ref[1:2, :] = g

    return pl.pallas_call(body, name=name, out_shape=jax.ShapeDtypeStruct((2, n), f32))(hg_lb, dlb)


def _adamw(w, g, m, v):
    m = ADAM_B1 * m + (1.0 - ADAM_B1) * g
    v = ADAM_B2 * v + (1.0 - ADAM_B2) * jnp.square(g)
    m_hat = m / (1.0 - ADAM_B1 ** ADAM_STEP)
    v_hat = v / (1.0 - ADAM_B2 ** ADAM_STEP)
    delta = -ADAM_LR * (m_hat / (jnp.sqrt(v_hat) + ADAM_EPS) + ADAM_WD * w)
    return delta, m, v


ADAM_BLOCK_BYTES = 24 * 1024 * 1024


def adam_reduced(parts, w, m, v, name):
    layers, r, c = w.shape
    flat = [p for layer_parts in parts for p in layer_parts]
    rows = flat[0].shape[1]
    assert all(p.shape == (N_DEV, rows, c) for p in flat) and rows * len(flat) == layers * r
    per_layer = r // rows
    row_bytes = 2 * (len(flat) * N_DEV * c * 2 + 7 * c * 4)
    tr = _tile(rows, max(16, ADAM_BLOCK_BYTES // row_bytes), 16)
    steps = rows // tr

    def body(*refs):
        p_refs = refs[:len(flat)]
        w_ref, m_ref, v_ref, g_ref, d_ref, mo_ref, vo_ref = refs[len(flat):]
        for idx in range(len(flat)):
            @pl.when(pl.program_id(0) == idx)
            def _():
                g = p_refs[idx][0].astype(f32)
                for j in range(1, N_DEV):
                    g = g + p_refs[idx][j].astype(f32)
                g_ref[...] = g
                d_ref[...], mo_ref[...], vo_ref[...] = _adamw(w_ref[...], g, m_ref[...], v_ref[...])

    def part_spec(idx):
        return pl.BlockSpec((N_DEV, tr, c), lambda p, i: (0, jnp.where(p == idx, i, 0), 0))

    blk = pl.BlockSpec((None, tr, c), lambda p, i: (p // per_layer, (p % per_layer) * steps + i, 0))
    out = jax.ShapeDtypeStruct((layers, r, c), f32)
    return pl.pallas_call(
        body, name=name, grid=(len(flat), steps),
        in_specs=[part_spec(idx) for idx in range(len(flat))] + [blk, blk, blk],
        out_specs=(blk, blk, blk, blk), out_shape=(out, out, out, out), compiler_params=_cparams("arbitrary", "arbitrary"),
    )(*flat, w, m, v)


def adam_plain(g, w, m, v, name, tr=256):
    r, c = w.shape
    tr = _tile(r, tr, 8)

    def body(g_ref, w_ref, m_ref, v_ref, d_ref, mo_ref, vo_ref):
        d_ref[...], mo_ref[...], vo_ref[...] = _adamw(w_ref[...], g_ref[...], m_ref[...], v_ref[...])

    blk = pl.BlockSpec((tr, c), lambda i: (i, 0))
    out = jax.ShapeDtypeStruct((r, c), f32)
    return pl.pallas_call(
        body, name=name, grid=(r // tr,), in_specs=[blk, blk, blk, blk], out_specs=(blk, blk, blk),
        out_shape=(out, out, out), compiler_params=_cparams("parallel"),
    )(g, w, m, v)


def sum_parts(parts, name):
    _, r, c = parts.shape

    def body(p_ref, o_ref):
        acc = p_ref[0]
        for j in range(1, N_DEV):
            acc = acc + p_ref[j]
        o_ref[...] = acc

    return pl.pallas_call(body, name=name, out_shape=jax.ShapeDtypeStruct((r, c), f32))(parts)


def _pack(arrs, rows_mult=8):
    flat = jnp.concatenate([a.reshape(-1) for a in arrs])
    rows = -(-flat.shape[0] // LANES)
    rows = -(-rows // rows_mult) * rows_mult
    return jnp.pad(flat, (0, rows * LANES - flat.shape[0])).reshape(rows, LANES)


def _unpack(flat, shapes):
    out, at = [], 0
    for s in shapes:
        n = math.prod(s)
        out.append(flat[at:at + n].reshape(s))
        at += n
    return out


def kernel(x, c, gm_w_in, gm_ln_g, gm_ln_b, gm_w_s, gm_b_s, gm_w_out, hg_w_in, hg_lb, hg_gn_g, hg_w_out, ffn_w_up, ffn_conv_w, ffn_conv_b, ffn_w_down, norm_g, ada_w, ada_b, final_g, loss_target, m_gm_w_in, m_gm_ln_g, m_gm_ln_b, m_gm_w_s, m_gm_b_s, m_gm_w_out, m_hg_w_in, m_hg_lb, m_hg_gn_g, m_hg_w_out, m_ffn_w_up, m_ffn_conv_w, m_ffn_conv_b, m_ffn_w_down, m_norm_g, m_ada_w, m_ada_b, m_final_g, v_gm_w_in, v_gm_ln_g, v_gm_ln_b, v_gm_w_s, v_gm_b_s, v_gm_w_out, v_hg_w_in, v_hg_lb, v_hg_gn_g, v_hg_w_out, v_ffn_w_up, v_ffn_conv_w, v_ffn_conv_b, v_ffn_w_down, v_norm_g, v_ada_w, v_ada_b, v_final_g):
    me = _flat(_mesh_pos())
    xt = x[0]
    t = xt.shape[0]

    small_shapes = [(1, D), (2, HG_DIM), (2, HG_DIM), (DEPTH, 2, HG_DIM), (DEPTH, 3, 2 * FFN_HIDDEN // N_DEV)]
    (small_all,) = all_gather([_pack([c, hg_lb, hg_gn_g, norm_g, ffn_conv_w])], "gather_small")
    small_all = small_all.reshape(N_DEV, -1)
    at = 0
    pieces = []
    for s in small_shapes:
        n = math.prod(s)
        pieces.append(small_all[:, at:at + n].reshape((N_DEV,) + s))
        at += n
    c_all = pieces[0].reshape(N_DEV, D)
    hg_lb_full = jnp.transpose(pieces[1], (1, 0, 2)).reshape(2, D)
    hg_gn_full = jnp.transpose(pieces[2], (1, 0, 2)).reshape(2, D)
    norm_g_full = jnp.transpose(pieces[3], (1, 2, 0, 3)).reshape(DEPTH, 2, D)
    conv_w_full = jnp.transpose(pieces[4], (1, 2, 0, 3)).reshape(DEPTH, 3, 2 * FFN_HIDDEN)

    lb1 = lower_bound_fwd(hg_lb_full, "lower_bound")
    lbs = [jnp.zeros((1, D), f32), lb1]

    ada_b_cols = lax.dynamic_slice(ada_b, (0, me * ADA_COLS), (DEPTH, ADA_COLS)).reshape(DEPTH, 1, ADA_COLS)
    mod_cols = ada_mod(c_all, ada_w, ada_b_cols, "ada_mod")
    (mod_mine,) = all_to_all([jnp.transpose(mod_cols, (1, 0, 2))], "mod_to_examples")
    mod = jnp.transpose(mod_mine, (1, 0, 2)).reshape(DEPTH, 6, 1, D)

    def layer_shards(i):
        j = i // 2
        w_in, w_out = (gm_w_in, gm_w_out) if i % 2 == 0 else (hg_w_in, hg_w_out)
        return [w_in[j].astype(bf16), w_out[j].astype(bf16), ffn_w_up[i].astype(bf16), ffn_w_down[i].astype(bf16)]

    def full_rows(g):
        return g.reshape(N_DEV * g.shape[1], g.shape[2])

    carried_by = {
        "in_0": [(0, 1), (0, 2)], "mix_0": [(0, 3)], "up_0": [(1, 0), (1, 1)], "gate_0": [(1, 2)], "down_0": [(1, 3)],
        "in_1": [(2, 0)], "mix_1": [(2, 1), (2, 2), (2, 3)], "up_1": [(3, 0), (3, 1)], "gate_1": [(3, 2)], "down_1": [(3, 3)],
    }
    shards = [layer_shards(i) for i in range(DEPTH)]
    gathered = {}
    (gathered[(0, 0)],) = all_gather([shards[0][0]], "gather_weights_0", relay=True)

    def carry(call, site, **kw):
        items = carried_by.get(site, [])
        outs = call(riders=Riders([shards[l][slot] for l, slot in items], True), **kw)
        for item, g in zip(items, outs[len(outs) - len(items):]):
            gathered[item] = g
        return outs[:len(outs) - len(items)]

    saved = []
    weights = []
    xcur = xt
    h = norm_fwd(xcur, norm_g_full[0, 0:1], mod[0, 1], mod[0, 0], "norm1_0")
    for i in range(DEPTH):
        j = i // 2
        sh1, sc1, g1, sh2, sc2, g2 = [mod[i, p] for p in range(6)]
        gn2 = norm_g_full[i, 1:2]
        s = {"x0": xcur, "h": h}
        w_in = gathered[(i, 0)]
        if i % 2 == 0:
            (z,) = carry(functools.partial(mm_nn, h, w_in, bf16, f"gm_in_{i}"), f"in_{i}")
            bs = gm_b_s[j].reshape(GM_HEADS, GM_BLOCK, 1)
            (mixed,) = carry(functools.partial(gm_mix_fwd, z, gm_w_s[j], bs, gm_ln_g[j:j + 1], gm_ln_b[j:j + 1],
                                               f"gm_mix_{i}"), f"mix_{i}")
            s["z"] = z
        else:
            (proj,) = carry(functools.partial(mm_nn, h, w_in, f32, f"hg_in_{i}"), f"in_{i}")
            mixed, states = carry(functools.partial(hg_scan_fwd, proj, lbs[j], hg_gn_full[j:j + 1], f"hg_scan_{i}"),
                                  f"mix_{i}")
            s["proj"], s["states"] = proj, states
        s["mixed"] = mixed
        w_out = full_rows(gathered[(i, 1)])
        y, x1, h2 = carry(functools.partial(mm_nn_residual, mixed, w_out, xcur, g1, (gn2, sc2, sh2), f"mix_out_{i}"),
                          f"out_{i}")
        s["y"], s["x1"] = y, x1
        w_up = gathered[(i, 2)]
        (a,) = carry(functools.partial(mm_nn, h2, w_up, bf16, f"ffn_up_{i}"), f"up_{i}")
        (hid,) = carry(functools.partial(ffn_gate_fwd, a, conv_w_full[i], ffn_conv_b[i:i + 1], f"ffn_gate_{i}"), f"gate_{i}")
        w_down = full_rows(gathered[(i, 3)])
        next_norm = (norm_g_full[i + 1, 0:1], mod[i + 1, 1], mod[i + 1, 0]) if i + 1 < DEPTH else None
        outs = carry(functools.partial(mm_nn_residual, hid, w_down, x1, g2, next_norm, f"ffn_down_{i}"), f"down_{i}")
        fo, x2 = outs[0], outs[1]
        s["h2"], s["a"], s["hid"], s["f"] = h2, a, hid, fo
        weights.append((w_in, w_out, w_up, w_down))
        saved.append(s)
        xcur = x2
        h = outs[2] if next_norm is not None else None

    loss_part, dx, d_final_g, dg2, df = loss_head(xcur, final_g.reshape(1, D), loss_target[0], saved[-1]["f"],
                                                  mod[DEPTH - 1, 5], "loss_head")
    loss = lax.psum(loss_part[0, 0], ("x", "y", "c"))

    def halves(blocked):
        rows = blocked.shape[1] // 2
        return [(blocked, (0, rows)), (blocked, (rows, rows))]

    def by_owner_rows(dw):
        k, n = dw.shape
        return dw.reshape(N_DEV, k // N_DEV, n)

    received = [[[] for _ in range(4)] for _ in range(DEPTH)]

    def send(call, items, **kw):
        outs = call(riders=Riders([arr for arr, _ in items], False), **kw)
        for (_, (layer, slot)), got in zip(items, outs[len(outs) - len(items):]):
            received[layer][slot].append(got)
        return outs[:len(outs) - len(items)]

    dmod = [None] * DEPTH
    d_norm_g = [None] * DEPTH
    d_gm = {k: [None, None] for k in ("ws", "bs", "lng", "lnb")}
    d_hg = {k: [None, None] for k in ("lb", "gn")}
    d_ffn = {k: [None] * DEPTH for k in ("cw", "cb")}
    in_halves = []
    for i in reversed(range(DEPTH)):
        j = i // 2
        s = saved[i]
        w_in, w_out, w_up, w_down = weights[i]
        sh1, sc1, g1, sh2, sc2, g2 = [mod[i, p] for p in range(6)]
        gn1, gn2 = norm_g_full[i, 0:1], norm_g_full[i, 1:2]
        (dw_down,) = mm_tn(s["hid"], df, bf16, f"dw_down_{i}", tn=D)
        (dhid,) = send(functools.partial(mm_nt, df, w_down, bf16, f"dhid_{i}"), in_halves[:1])
        dyc, dwg, dwv, dbg, dbv = send(
            functools.partial(ffn_gate_bwd, s["a"], conv_w_full[i], ffn_conv_b[i:i + 1], dhid, f"ffn_gate_bwd_{i}"),
            in_halves[1:] + [(by_owner_rows(dw_down), (i, 3))])
        d_ffn["cw"][i] = jnp.concatenate([dwg, dwv], axis=1)
        d_ffn["cb"][i] = jnp.concatenate([dbg, dbv], axis=1)
        da = conv_transpose(dyc, conv_w_full[i], f"conv_t_{i}")
        (dw_up,) = mm_tn_by_owner(s["h2"], da, f"dw_up_{i}")
        up_halves = [(part, (i, 2)) for part in halves(dw_up)]
        (dh2,) = send(functools.partial(mm_nt, da, w_up, bf16, f"dh2_{i}"), up_halves[:1])
        dx1, dgn2, dsc2, dsh2, dg1, dy = norm_bwd(s["x1"], gn2, sc2, sh2, dh2, dx, f"norm2_bwd_{i}", gate=(s["y"], g1))
        (dw_out,) = mm_tn(s["mixed"], dy, bf16, f"dw_mix_out_{i}")
        (dmixed,) = mm_nt(dy, w_out, bf16, f"dmixed_{i}")
        if i % 2 == 0:
            bs = gm_b_s[j].reshape(GM_HEADS, GM_BLOCK, 1)
            dpre, dws, dbs, dlng, dlnb = send(
                functools.partial(gm_mix_bwd, s["z"], gm_w_s[j], bs, gm_ln_g[j:j + 1], gm_ln_b[j:j + 1], dmixed,
                                  f"gm_mix_bwd_{i}"), up_halves[1:])
            d_gm["ws"][j], d_gm["bs"][j], d_gm["lng"][j], d_gm["lnb"][j] = dws, dbs.reshape(GM_HEADS, GM_BLOCK), dlng, dlnb
        else:
            dpre, dlb, dgn = send(
                functools.partial(hg_scan_bwd, s["proj"], lbs[j], hg_gn_full[j:j + 1], s["states"], dmixed,
                                  f"hg_scan_bwd_{i}"), up_halves[1:])
            d_hg["lb"][j], d_hg["gn"][j] = dlb, dgn
        (dw_in,) = send(functools.partial(mm_tn_by_owner, s["h"], dpre, f"dw_mix_in_{i}"), [(by_owner_rows(dw_out), (i, 1))])
        in_halves = [(part, (i, 0)) for part in halves(dw_in)]
        (dh,) = send(functools.partial(mm_nt, dpre, w_in, bf16, f"dh_mix_{i}"), in_halves[:1] if i == 0 else [])
        dmod_i = [None, None, dg1, dsh2, dsc2, dg2]
        if i > 0:
            dx, dgn1, dsc1, dsh1, dg2, df = norm_bwd(s["x0"], gn1, sc1, sh1, dh, dx1, f"norm1_bwd_{i}",
                                                     gate=(saved[i - 1]["f"], mod[i - 1, 5]))
        else:
            dx, dgn1, dsc1, dsh1 = send(functools.partial(norm_bwd, s["x0"], gn1, sc1, sh1, dh, dx1, f"norm1_bwd_{i}"),
                                        in_halves[1:])
        dmod_i[0], dmod_i[1] = dsh1, dsc1
        dmod[i] = jnp.concatenate(dmod_i, axis=1)
        d_norm_g[i] = jnp.concatenate([dgn1, dgn2], axis=0)
    grad_x = dx.reshape(1, t, D)

    (dmod_all,) = all_gather([jnp.concatenate(dmod, axis=0)], "gather_dmod")
    dmod_cols = jnp.transpose(lax.dynamic_slice(dmod_all, (0, 0, me * ADA_COLS), (N_DEV, DEPTH, ADA_COLS)), (1, 0, 2))
    g_ada_w, g_ada_b = ada_grads(c_all, dmod_cols, dmod_all.reshape(N_DEV, DEPTH, 1, 6 * D), "ada_grads")
    g_ada_b = g_ada_b.reshape(DEPTH, 6 * D)

    small_partials = [jnp.concatenate(d_gm["lng"], axis=0), jnp.concatenate(d_gm["lnb"], axis=0),
                      jnp.stack(d_gm["ws"]), jnp.stack(d_gm["bs"]), jnp.concatenate(d_ffn["cb"], axis=0),
                      d_final_g, d_hg["lb"][1], jnp.concatenate(d_hg["gn"], axis=0), jnp.stack(d_norm_g),
                      jnp.stack(d_ffn["cw"])]
    partial_shapes = [p.shape for p in small_partials]
    packed = _pack(small_partials, rows_mult=8 * N_DEV)
    rows = packed.shape[0] // N_DEV
    (recv,) = all_to_all([packed.reshape(N_DEV, rows, LANES)], "small_grads_exchange")
    (summed,) = all_gather([sum_parts(recv, "small_grads_sum")], "small_grads_gather")
    g_ln_g, g_ln_b, g_ws, g_bs, g_cb, g_final, g_lb1, g_gn, g_norm, g_cw = _unpack(summed.reshape(-1), partial_shapes)
    g_final = g_final.reshape(D)

    def my_cols(a, n):
        start = (0,) * (a.ndim - 1) + (me * n,)
        return lax.dynamic_slice(a, start, a.shape[:-1] + (n,))

    g_hg_lb = lower_bound_bwd(hg_lb, my_cols(g_lb1, HG_DIM), "lower_bound_bwd")
    g_hg_gn = my_cols(g_gn, HG_DIM)
    g_norm_g = my_cols(g_norm, HG_DIM)
    g_conv_w = my_cols(g_cw, 2 * FFN_HIDDEN // N_DEV)

    def parts_of(slot, layers):
        return [received[i][slot] for i in layers]

    w_shards = [gm_w_in, gm_w_out, hg_w_in, hg_w_out, ffn_w_up, ffn_w_down]
    big_parts = [parts_of(0, (0, 2)), parts_of(1, (0, 2)), parts_of(0, (1, 3)), parts_of(1, (1, 3)),
                 parts_of(2, range(DEPTH)), parts_of(3, range(DEPTH))]
    big_m = [m_gm_w_in, m_gm_w_out, m_hg_w_in, m_hg_w_out, m_ffn_w_up, m_ffn_w_down]
    big_v = [v_gm_w_in, v_gm_w_out, v_hg_w_in, v_hg_w_out, v_ffn_w_up, v_ffn_w_down]
    big = [adam_reduced(parts, w, m_, v_, f"adam_big_{idx}")
           for idx, (w, m_, v_, parts) in enumerate(zip(w_shards, big_m, big_v, big_parts))]
    (g_gm_w_in, d_gm_w_in, nm_gm_w_in, nv_gm_w_in), (g_gm_w_out, d_gm_w_out, nm_gm_w_out, nv_gm_w_out), \
        (g_hg_w_in, d_hg_w_in, nm_hg_w_in, nv_hg_w_in), (g_hg_w_out, d_hg_w_out, nm_hg_w_out, nv_hg_w_out), \
        (g_ffn_w_up, d_ffn_w_up, nm_ffn_w_up, nv_ffn_w_up), (g_ffn_w_down, d_ffn_w_down, nm_ffn_w_down, nv_ffn_w_down) = big

    two_d = (-1, ADA_COLS)
    d_ada_w, nm_ada_w, nv_ada_w = [o.reshape(ada_w.shape) for o in adam_plain(
        g_ada_w.reshape(two_d), ada_w.reshape(two_d), m_ada_w.reshape(two_d), v_ada_w.reshape(two_d), "adam_ada_w")]

    small_g = [g_ln_g, g_ln_b, g_ws, g_bs, g_cb, g_ada_b, g_final, g_hg_lb, g_hg_gn, g_norm_g, g_conv_w]
    small_w = [gm_ln_g, gm_ln_b, gm_w_s, gm_b_s, ffn_conv_b, ada_b, final_g, hg_lb, hg_gn_g, norm_g, ffn_conv_w]
    small_m = [m_gm_ln_g, m_gm_ln_b, m_gm_w_s, m_gm_b_s, m_ffn_conv_b, m_ada_b, m_final_g, m_hg_lb, m_hg_gn_g, m_norm_g, m_ffn_conv_w]
    small_v = [v_gm_ln_g, v_gm_ln_b, v_gm_w_s, v_gm_b_s, v_ffn_conv_b, v_ada_b, v_final_g, v_hg_lb, v_hg_gn_g, v_norm_g, v_ffn_conv_w]
    shapes = [w.shape for w in small_w]
    small_g = [g.reshape(s) for g, s in zip(small_g, shapes)]
    outs = adam_plain(_pack(small_g), _pack(small_w), _pack(small_m), _pack(small_v), "adam_small")
    (d_ln_g, d_ln_b, d_ws, d_bs, d_cb, d_ada_b, d_final, d_hg_lb, d_hg_gn, d_norm_g_, d_conv_w), \
        (nm_ln_g, nm_ln_b, nm_ws, nm_bs, nm_cb, nm_ada_b, nm_final, nm_hg_lb, nm_hg_gn, nm_norm_g, nm_conv_w), \
        (nv_ln_g, nv_ln_b, nv_ws, nv_bs, nv_cb, nv_ada_b, nv_final, nv_hg_lb, nv_hg_gn, nv_norm_g, nv_conv_w) = [
            _unpack(o.reshape(-1), shapes) for o in outs]
    g_ln_g, g_ln_b, g_ws, g_bs, g_cb, g_ada_b, g_final, g_hg_lb, g_hg_gn, g_norm_g, g_conv_w = small_g

    grads = (g_gm_w_in, g_ln_g, g_ln_b, g_ws, g_bs, g_gm_w_out, g_hg_w_in, g_hg_lb, g_hg_gn, g_hg_w_out,
             g_ffn_w_up, g_conv_w, g_cb, g_ffn_w_down, g_norm_g, g_ada_w, g_ada_b, g_final)
    deltas = (d_gm_w_in, d_ln_g, d_ln_b, d_ws, d_bs, d_gm_w_out, d_hg_w_in, d_hg_lb, d_hg_gn, d_hg_w_out,
              d_ffn_w_up, d_conv_w, d_cb, d_ffn_w_down, d_norm_g_, d_ada_w, d_ada_b, d_final)
    new_m = (nm_gm_w_in, nm_ln_g, nm_ln_b, nm_ws, nm_bs, nm_gm_w_out, nm_hg_w_in, nm_hg_lb, nm_hg_gn, nm_hg_w_out,
             nm_ffn_w_up, nm_conv_w, nm_cb, nm_ffn_w_down, nm_norm_g, nm_ada_w, nm_ada_b, nm_final)
    new_v = (nv_gm_w_in, nv_ln_g, nv_ln_b, nv_ws, nv_bs, nv_gm_w_out, nv_hg_w_in, nv_hg_lb, nv_hg_gn, nv_hg_w_out,
             nv_ffn_w_up, nv_conv_w, nv_cb, nv_ffn_w_down, nv_norm_g, nv_ada_w, nv_ada_b, nv_final)
    return (loss, grad_x) + grads + deltas + new_m + new_v
```

```python
import functools
import math

import jax
import jax.numpy as jnp
from jax import lax
from jax.experimental import pallas as pl
from jax.experimental.pallas import tpu as pltpu

f32 = jnp.float32
bf16 = jnp.bfloat16
MESH = pl.DeviceIdType.MESH

N_DEV = 8
D = 1024
DEPTH = 4
EPS = 1e-6
GM_WIDTH = 2048
GM_HEADS = 8
GM_HEAD_DIM = 256
GM_BLOCK = 128
CHUNK = 64
HG_HEADS = 8
HG_DIM = 128
FFN_HIDDEN = 2816
ADA_COLS = 6 * D // N_DEV

HG_SUB = 32
HG_PAIR = 8
HG_TOKENS = 128

ADAM_LR = 0.001
ADAM_B1 = 0.9
ADAM_B2 = 0.999
ADAM_EPS = 1e-08
ADAM_WD = 0.01
ADAM_STEP = 10

V7X_VMEM_LIMIT = 56 * 1024 * 1024
LANES = 128


def _cparams(*sem):
    return pltpu.CompilerParams(dimension_semantics=sem or None, vmem_limit_bytes=V7X_VMEM_LIMIT)


def _tile(n, target, mult=LANES):
    best = None
    for t in range(mult, min(n, target) + 1, mult):
        if n % t == 0:
            best = t
    return best or n


WEIGHT_BLOCK_BYTES = 6 * 1024 * 1024


def _weight_tile(n, k):
    return _tile(n, max(LANES, WEIGHT_BLOCK_BYTES // (2 * k)))


def _gelu(x):
    return 0.5 * x * (1.0 + lax.erf(x * (1.0 / math.sqrt(2.0))))


def _mesh_pos():
    return lax.axis_index("x"), lax.axis_index("y"), lax.axis_index("c")


def _flat(pos):
    return 4 * pos[0] + 2 * pos[1] + pos[2]


def _peer(pos, k):
    return ((1 - pos[0]) if k & 4 else pos[0], (1 - pos[1]) if k & 2 else pos[1], (1 - pos[2]) if k & 1 else pos[2])


def _exchange_copies(ins, outs, send_sems, recv_sems, local_sems, gather):
    pos = _mesh_pos()
    me = _flat(pos)

    def src(i, dest):
        if gather:
            return ins[i]
        ref, rows = ins[i] if isinstance(ins[i], tuple) else (ins[i], None)
        return ref.at[dest] if rows is None else ref.at[dest, pl.ds(*rows)]

    local = [pltpu.make_async_copy(src(i, me), outs[i].at[me], local_sems.at[i]) for i in range(len(ins))]
    sends, recvs = [], []
    for k in range(1, N_DEV):
        peer = _peer(pos, k)
        there = _flat(peer)
        for i in range(len(ins)):
            sems = dict(send_sem=send_sems.at[i * 7 + k - 1], recv_sem=recv_sems.at[i * 7 + k - 1],
                        device_id=peer, device_id_type=MESH)
            sends.append(pltpu.make_async_remote_copy(src_ref=src(i, there), dst_ref=outs[i].at[me], **sems))
            recvs.append(pltpu.make_async_remote_copy(src_ref=src(i, there), dst_ref=outs[i].at[there], **sems))
    return local, sends, recvs


def _exchange_start(*refs):
    local, sends, _ = _exchange_copies(*refs)
    for cp in local + sends:
        cp.start()


def _exchange_wait(*refs):
    local, sends, recvs = _exchange_copies(*refs)
    for cp in recvs:
        cp.wait_recv()
    for cp in sends:
        cp.wait_send()
    for cp in local:
        cp.wait()


OTHER_CHIPS = (2, 4, 6)


def _relay_copies(ins, outs, send_sems, recv_sems, local_sems):
    pos = _mesh_pos()
    me = _flat(pos)
    sibling = _peer(pos, 1)
    local = [pltpu.make_async_copy(ins[i], outs[i].at[me], local_sems.at[i]) for i in range(len(ins))]
    first, passes, recvs = [], {k: [] for k in OTHER_CHIPS}, {k: [] for k in range(1, N_DEV)}
    for i in range(len(ins)):
        def copy(k, src, block, to):
            return pltpu.make_async_remote_copy(
                src_ref=src, dst_ref=outs[i].at[block], send_sem=send_sems.at[i * 7 + k - 1],
                recv_sem=recv_sems.at[i * 7 + k - 1], device_id=to, device_id_type=MESH)

        for k in (1,) + OTHER_CHIPS:
            first.append(copy(k, ins[i], me, _peer(pos, k)))
        for k in OTHER_CHIPS:
            there = _flat(_peer(pos, k))
            passes[k].append(copy(k ^ 1, outs[i].at[there], there, sibling))
        for k in range(1, N_DEV):
            there = _flat(_peer(pos, k))
            recvs[k].append(copy(k, ins[i], there, _peer(pos, k)))
    return local, first, passes, recvs


def _relay_start(ins, outs, *sems):
    local, first, _, _ = _relay_copies(ins, outs, *sems)
    for cp in local + first:
        cp.start()


def _relay_wait(ins, outs, *sems):
    local, first, passes, recvs = _relay_copies(ins, outs, *sems)
    for k in OTHER_CHIPS:
        for cp in recvs[k]:
            cp.wait_recv()
        for cp in passes[k]:
            cp.start()
    for k in (1, 3, 5, 7):
        for cp in recvs[k]:
            cp.wait_recv()
    for cp in first + [cp for k in OTHER_CHIPS for cp in passes[k]]:
        cp.wait_send()
    for cp in local:
        cp.wait()


def _exchange_out_shape(a, gather):
    return jax.ShapeDtypeStruct((N_DEV,) + tuple(a.shape) if gather else tuple(a.shape), a.dtype)


def _exchange_sems(n):
    return [pltpu.SemaphoreType.DMA((7 * n,)), pltpu.SemaphoreType.DMA((7 * n,)), pltpu.SemaphoreType.DMA((n,))]


ANY = pl.BlockSpec(memory_space=pl.ANY)


def _exchange(arrs, gather, name, relay=False):
    n = len(arrs)

    def body(*refs):
        ins, outs = refs[:n], refs[n:2 * n]
        if relay:
            _relay_start(ins, outs, *refs[2 * n:])
            _relay_wait(ins, outs, *refs[2 * n:])
        else:
            _exchange_start(ins, outs, *refs[2 * n:], gather)
            _exchange_wait(ins, outs, *refs[2 * n:], gather)

    return pl.pallas_call(
        body, name=name, out_shape=tuple(_exchange_out_shape(a, gather) for a in arrs),
        in_specs=[ANY] * n, out_specs=tuple([ANY] * n), scratch_shapes=_exchange_sems(n),
    )(*arrs)


def all_gather(arrs, name, relay=False):
    return _exchange(arrs, True, name, relay)


def all_to_all(arrs, name):
    return _exchange(arrs, False, name)


class Riders:
    def __init__(self, arrs, gather):
        self.gather = gather
        self.rows = [a[1] if isinstance(a, tuple) else None for a in arrs]
        self.arrs = [a[0] if isinstance(a, tuple) else a for a in arrs]

    def out_shapes(self):
        shapes = []
        for a, rows in zip(self.arrs, self.rows):
            shape = tuple(a.shape) if rows is None else (a.shape[0], rows[1], a.shape[2])
            shapes.append(jax.ShapeDtypeStruct((N_DEV,) + shape if self.gather else shape, a.dtype))
        return shapes


def _call(body, *, name, grid, in_specs, out_specs, out_shape, semantics, scratch_shapes=(), riders=None):
    if riders is None or not riders.arrs:
        return pl.pallas_call(body, name=name, grid=grid, in_specs=in_specs, out_specs=tuple(out_specs),
                              out_shape=tuple(out_shape), scratch_shapes=list(scratch_shapes),
                              compiler_params=_cparams(*semantics))
    n_in, n_out, n_scr, n_r = len(in_specs), len(out_specs), len(scratch_shapes), len(riders.arrs)
    gather = riders.gather

    def hosted(*refs):
        ins, r_ins = refs[:n_in], refs[n_in:n_in + n_r]
        at = n_in + n_r
        outs, r_outs = refs[at:at + n_out], refs[at + n_out:at + n_out + n_r]
        at += n_out + n_r
        scratch, sems = refs[at:at + n_scr], refs[at + n_scr:]
        first = functools.reduce(jnp.logical_and, [pl.program_id(a) == 0 for a in range(len(grid))])
        last = functools.reduce(jnp.logical_and, [pl.program_id(a) == grid[a] - 1 for a in range(len(grid))])

        r_ins = [(ref, rows) if rows is not None else ref for ref, rows in zip(r_ins, riders.rows)]

        @pl.when(first)
        def _():
            if gather:
                _relay_start(r_ins, r_outs, *sems)
            else:
                _exchange_start(r_ins, r_outs, *sems, gather)

        body(*ins, *outs, *scratch)

        @pl.when(last)
        def _():
            if gather:
                _relay_wait(r_ins, r_outs, *sems)
            else:
                _exchange_wait(r_ins, r_outs, *sems, gather)

    call = pl.pallas_call(
        hosted, name=name, grid=grid, in_specs=list(in_specs) + [ANY] * n_r, out_specs=tuple(out_specs) + (ANY,) * n_r,
        out_shape=tuple(out_shape) + tuple(riders.out_shapes()),
        scratch_shapes=list(scratch_shapes) + _exchange_sems(n_r),
        compiler_params=_cparams(*(("arbitrary",) * len(grid))))
    return lambda *args: call(*args, *riders.arrs)


def _shards_per_step(b):
    _, k, n = b.shape
    best = None
    for q in (1, 2, 4, 8):
        if (q * n) % LANES == 0 and (best is None or 2 * k * q * n <= WEIGHT_BLOCK_BYTES):
            best = q
    return best


def mm_nn(a, b, out_dtype, name, tm=512, riders=None):
    m, k = a.shape
    tm = _tile(m, tm, 8)
    if b.ndim == 3:
        shard = b.shape[2]
        n = N_DEV * shard
        per_step = _shards_per_step(b)
        tn = per_step * shard
        b_spec = pl.BlockSpec((per_step, k, shard), lambda i, j: (j, 0, 0))

        def body(a_ref, b_ref, o_ref):
            for q in range(per_step):
                o_ref[:, q * shard:(q + 1) * shard] = jnp.dot(a_ref[...], b_ref[q],
                                                              preferred_element_type=f32).astype(o_ref.dtype)
    else:
        n = b.shape[1]
        tn = _weight_tile(n, k)
        b_spec = pl.BlockSpec((k, tn), lambda i, j: (0, j))

        def body(a_ref, b_ref, o_ref):
            o_ref[...] = jnp.dot(a_ref[...], b_ref[...], preferred_element_type=f32).astype(o_ref.dtype)

    return _call(
        body, name=name, grid=(m // tm, n // tn),
        in_specs=[pl.BlockSpec((tm, k), lambda i, j: (i, 0)), b_spec],
        out_specs=[pl.BlockSpec((tm, tn), lambda i, j: (i, j))],
        out_shape=[jax.ShapeDtypeStruct((m, n), out_dtype)], semantics=("parallel", "parallel"), riders=riders,
    )(a, b)


def mm_nn_residual(a, b, x, gate, norm, name, tm=512, riders=None):
    m, k = a.shape
    n = b.shape[1]
    tm = _tile(m, tm, 8)

    def body(a_ref, b_ref, x_ref, g_ref, *rest):
        if norm is not None:
            gn_ref, sc_ref, sh_ref, y_ref, o_ref, h_ref = rest
        else:
            y_ref, o_ref = rest
        y = jnp.dot(a_ref[...], b_ref[...], preferred_element_type=f32)
        y_ref[...] = y.astype(bf16)
        x_new = x_ref[...] + g_ref[...] * y
        o_ref[...] = x_new
        if norm is not None:
            h_ref[...] = _norm_fn(x_new, gn_ref[...], sc_ref[...], sh_ref[...]).astype(bf16)

    blk = pl.BlockSpec((tm, n), lambda i: (i, 0))
    in_specs = [pl.BlockSpec((tm, k), lambda i: (i, 0)), pl.BlockSpec((k, n), lambda i: (0, 0)), blk, _row(n)]
    out_specs = [blk, blk]
    out_shape = [jax.ShapeDtypeStruct((m, n), bf16), jax.ShapeDtypeStruct((m, n), f32)]
    args = [a, b, x, gate]
    if norm is not None:
        in_specs += [_row(n)] * 3
        out_specs += [blk]
        out_shape += [jax.ShapeDtypeStruct((m, n), bf16)]
        args += list(norm)
    return _call(body, name=name, grid=(m // tm,), in_specs=in_specs, out_specs=out_specs, out_shape=out_shape,
                 semantics=("parallel",), riders=riders)(*args)


def mm_nt(a, b, out_dtype, name, tm=512, riders=None):
    m = a.shape[0]
    tm = _tile(m, tm, 8)
    if b.ndim == 3:
        _, k, shard = b.shape
        tk = k
        b_spec = pl.BlockSpec(b.shape, lambda i, j: (0, 0, 0))
        width = N_DEV * shard

        def body(a_ref, b_ref, o_ref):
            acc = None
            for q in range(N_DEV):
                part = lax.dot_general(a_ref[:, q * shard:(q + 1) * shard], b_ref[q], (((1,), (1,)), ((), ())),
                                       preferred_element_type=f32)
                acc = part if acc is None else acc + part
            o_ref[...] = acc.astype(o_ref.dtype)
    else:
        k, width = b.shape
        tk = _weight_tile(k, width)
        b_spec = pl.BlockSpec((tk, width), lambda i, j: (j, 0))

        def body(a_ref, b_ref, o_ref):
            o_ref[...] = lax.dot_general(a_ref[...], b_ref[...], (((1,), (1,)), ((), ())),
                                         preferred_element_type=f32).astype(o_ref.dtype)

    return _call(
        body, name=name, grid=(m // tm, k // tk),
        in_specs=[pl.BlockSpec((tm, width), lambda i, j: (i, 0)), b_spec],
        out_specs=[pl.BlockSpec((tm, tk), lambda i, j: (i, j))],
        out_shape=[jax.ShapeDtypeStruct((m, k), out_dtype)], semantics=("parallel", "parallel"), riders=riders,
    )(a, b)


def mm_tn(a, b, out_dtype, name, tm=512, tn=512, riders=None):
    t, m = a.shape
    n = b.shape[1]
    tm, tn = _tile(m, tm), _tile(n, tn)

    def body(a_ref, b_ref, o_ref):
        o_ref[...] = lax.dot_general(a_ref[...], b_ref[...], (((0,), (0,)), ((), ())),
                                     preferred_element_type=f32).astype(o_ref.dtype)

    return _call(
        body, name=name, grid=(m // tm, n // tn),
        in_specs=[pl.BlockSpec((t, tm), lambda i, j: (0, i)), pl.BlockSpec((t, tn), lambda i, j: (0, j))],
        out_specs=[pl.BlockSpec((tm, tn), lambda i, j: (i, j))],
        out_shape=[jax.ShapeDtypeStruct((m, n), out_dtype)], semantics=("parallel", "parallel"), riders=riders,
    )(a, b)


def mm_tn_by_owner(a, b, name, tm=512, riders=None):
    t, m = a.shape
    n = b.shape[1]
    shard = n // N_DEV
    per_step = 1 if shard % LANES == 0 else 2
    assert (per_step * shard) % LANES == 0
    tm = _tile(m, tm)

    def body(a_ref, b_ref, o_ref):
        acc = lax.dot_general(a_ref[...], b_ref[...], (((0,), (0,)), ((), ())), preferred_element_type=f32)
        for q in range(per_step):
            o_ref[q] = acc[:, q * shard:(q + 1) * shard].astype(bf16)

    return _call(
        body, name=name, grid=(m // tm, N_DEV // per_step),
        in_specs=[pl.BlockSpec((t, tm), lambda i, j: (0, i)), pl.BlockSpec((t, per_step * shard), lambda i, j: (0, j))],
        out_specs=[pl.BlockSpec((per_step, tm, shard), lambda i, j: (j, i, 0))],
        out_shape=[jax.ShapeDtypeStruct((N_DEV, m, shard), bf16)], semantics=("parallel", "parallel"), riders=riders,
    )(a, b)


def _norm_fn(x, gn, sc, sh):
    r = lax.rsqrt(jnp.mean(x * x, axis=-1, keepdims=True) + EPS)
    return (x * r * gn) * (1.0 + sc) + sh


def _row(d):
    return pl.BlockSpec((1, d), lambda i: (0, 0))


def norm_fwd(x, gn, sc, sh, name, tm=512):
    t, d = x.shape
    tm = _tile(t, tm, 8)

    def body(x_ref, gn_ref, sc_ref, sh_ref, h_ref):
        h_ref[...] = _norm_fn(x_ref[...], gn_ref[...], sc_ref[...], sh_ref[...]).astype(bf16)

    return pl.pallas_call(
        body, name=name, grid=(t // tm,),
        in_specs=[pl.BlockSpec((tm, d), lambda i: (i, 0)), _row(d), _row(d), _row(d)],
        out_specs=pl.BlockSpec((tm, d), lambda i: (i, 0)),
        out_shape=jax.ShapeDtypeStruct((t, d), bf16), compiler_params=_cparams("parallel"),
    )(x, gn, sc, sh)


def _gate_bwd(dx, y_ref, g_ref, dgate_ref, dy_ref):
    dgate_ref[...] += jnp.sum(dx * y_ref[...].astype(f32), axis=0, keepdims=True)
    dy_ref[...] = (dx * g_ref[...]).astype(bf16)


def norm_bwd(x, gn, sc, sh, dh, dres, name, gate=None, tm=512, riders=None):
    t, d = x.shape
    tm = _tile(t, tm, 8)

    def body(x_ref, gn_ref, sc_ref, sh_ref, dh_ref, dres_ref, *rest):
        if gate is not None:
            y_ref, g_ref, dx_ref, dgn_ref, dsc_ref, dsh_ref, dgate_ref, dy_ref = rest
        else:
            dx_ref, dgn_ref, dsc_ref, dsh_ref = rest

        @pl.when(pl.program_id(0) == 0)
        def _():
            dgn_ref[...] = jnp.zeros_like(dgn_ref)
            dsc_ref[...] = jnp.zeros_like(dsc_ref)
            dsh_ref[...] = jnp.zeros_like(dsh_ref)
            if gate is not None:
                dgate_ref[...] = jnp.zeros_like(dgate_ref)

        _, vjp = jax.vjp(_norm_fn, x_ref[...], gn_ref[...], sc_ref[...], sh_ref[...])
        dx, dgn, dsc, dsh = vjp(dh_ref[...].astype(f32))
        dx = dx + dres_ref[...]
        dx_ref[...] = dx
        dgn_ref[...] += dgn
        dsc_ref[...] += dsc
        dsh_ref[...] += dsh
        if gate is not None:
            _gate_bwd(dx, y_ref, g_ref, dgate_ref, dy_ref)

    blk = pl.BlockSpec((tm, d), lambda i: (i, 0))
    vec = jax.ShapeDtypeStruct((1, d), f32)
    in_specs = [blk, _row(d), _row(d), _row(d), blk, blk]
    out_specs = [blk, _row(d), _row(d), _row(d)]
    out_shape = [jax.ShapeDtypeStruct((t, d), f32), vec, vec, vec]
    args = [x, gn, sc, sh, dh, dres]
    if gate is not None:
        in_specs += [blk, _row(d)]
        out_specs += [_row(d), blk]
        out_shape += [vec, jax.ShapeDtypeStruct((t, d), bf16)]
        args += list(gate)
    return _call(body, name=name, grid=(t // tm,), in_specs=in_specs, out_specs=out_specs, out_shape=out_shape,
                 semantics=("arbitrary",), riders=riders)(*args)


def _loss_fn(x, g, tgt):
    r = lax.rsqrt(jnp.mean(x * x, axis=-1, keepdims=True) + EPS)
    err = jnp.square(x * r * g - tgt)
    return 0.5 * jnp.sum(jnp.mean(err, axis=-1, keepdims=True), axis=0, keepdims=True)


def loss_head(x, g, tgt, y, gate, name, tm=512):
    t, d = x.shape
    tm = _tile(t, tm, 8)

    def body(x_ref, g_ref, t_ref, y_ref, gate_ref, loss_ref, dx_ref, dg_ref, dgate_ref, dy_ref):
        @pl.when(pl.program_id(0) == 0)
        def _():
            loss_ref[...] = jnp.zeros_like(loss_ref)
            dg_ref[...] = jnp.zeros_like(dg_ref)
            dgate_ref[...] = jnp.zeros_like(dgate_ref)

        loss, vjp = jax.vjp(_loss_fn, x_ref[...], g_ref[...], t_ref[...])
        dx, dg, _ = vjp(jnp.ones((1, 1), f32))
        dx_ref[...] = dx
        loss_ref[...] += loss
        dg_ref[...] += dg
        _gate_bwd(dx, y_ref, gate_ref, dgate_ref, dy_ref)

    blk = pl.BlockSpec((tm, d), lambda i: (i, 0))
    vec = jax.ShapeDtypeStruct((1, d), f32)
    return pl.pallas_call(
        body, name=name, grid=(t // tm,),
        in_specs=[blk, _row(d), blk, blk, _row(d)],
        out_specs=(pl.BlockSpec((1, 1), lambda i: (0, 0)), blk, _row(d), _row(d), blk),
        out_shape=(jax.ShapeDtypeStruct((1, 1), f32), jax.ShapeDtypeStruct((t, d), f32), vec, vec,
                   jax.ShapeDtypeStruct((t, d), bf16)),
        compiler_params=_cparams("arbitrary"),
    )(x, g, tgt, y, gate)


def _gm_block_fn(z, ws, bs, lng, lnb):
    u = _gelu(z[:, :GM_WIDTH])
    vg = _gelu(z[:, GM_WIDTH:])
    mu = jnp.mean(vg, axis=-1, keepdims=True)
    var = jnp.mean(jnp.square(vg - mu), axis=-1, keepdims=True)
    vn = (vg - mu) * lax.rsqrt(var + EPS) * lng + lnb
    row = lax.broadcasted_iota(jnp.int32, (GM_BLOCK, GM_BLOCK), 0) // CHUNK
    col = lax.broadcasted_iota(jnp.int32, (GM_BLOCK, GM_BLOCK), 1) // CHUNK
    parts = []
    for h in range(GM_HEADS):
        w = jnp.where(row >= col, ws[h], 0.0)
        cols = slice(h * GM_HEAD_DIM, (h + 1) * GM_HEAD_DIM)
        s = jnp.dot(w.astype(bf16), vn[:, cols].astype(bf16), preferred_element_type=f32) + bs[h]
        parts.append(u[:, cols] * s)
    return jnp.concatenate(parts, axis=1)


def _gm_param_specs():
    return [pl.BlockSpec((GM_HEADS, GM_BLOCK, GM_BLOCK), lambda i: (0, 0, 0)),
            pl.BlockSpec((GM_HEADS, GM_BLOCK, 1), lambda i: (0, 0, 0)), _row(GM_WIDTH), _row(GM_WIDTH)]


def gm_mix_fwd(z, ws, bs, lng, lnb, name, riders=None):
    t = z.shape[0]

    def body(z_ref, ws_ref, bs_ref, lng_ref, lnb_ref, o_ref):
        o_ref[...] = _gm_block_fn(z_ref[...].astype(f32), ws_ref[...], bs_ref[...], lng_ref[...],
                                  lnb_ref[...]).astype(bf16)

    return _call(
        body, name=name, grid=(t // GM_BLOCK,),
        in_specs=[pl.BlockSpec((GM_BLOCK, 2 * GM_WIDTH), lambda i: (i, 0))] + _gm_param_specs(),
        out_specs=[pl.BlockSpec((GM_BLOCK, GM_WIDTH), lambda i: (i, 0))],
        out_shape=[jax.ShapeDtypeStruct((t, GM_WIDTH), bf16)], semantics=("parallel",), riders=riders,
    )(z, ws, bs, lng, lnb)


def gm_mix_bwd(z, ws, bs, lng, lnb, dgated, name, riders=None):
    t = z.shape[0]

    def body(z_ref, ws_ref, bs_ref, lng_ref, lnb_ref, dg_ref, dz_ref, dws_ref, dbs_ref, dlng_ref, dlnb_ref):
        _, vjp = jax.vjp(_gm_block_fn, z_ref[...].astype(f32), ws_ref[...], bs_ref[...], lng_ref[...], lnb_ref[...])
        dz, dws, dbs, dlng, dlnb = vjp(dg_ref[...].astype(f32))
        dz_ref[...] = dz.astype(bf16)

        @pl.when(pl.program_id(0) == 0)
        def _():
            dws_ref[...] = jnp.zeros_like(dws_ref)
            dbs_ref[...] = jnp.zeros_like(dbs_ref)
            dlng_ref[...] = jnp.zeros_like(dlng_ref)
            dlnb_ref[...] = jnp.zeros_like(dlnb_ref)

        dws_ref[...] += dws
        dbs_ref[...] += dbs
        dlng_ref[...] += dlng
        dlnb_ref[...] += dlnb

    zblk = pl.BlockSpec((GM_BLOCK, 2 * GM_WIDTH), lambda i: (i, 0))
    return _call(
        body, name=name, grid=(t // GM_BLOCK,),
        in_specs=[zblk] + _gm_param_specs() + [pl.BlockSpec((GM_BLOCK, GM_WIDTH), lambda i: (i, 0))],
        out_specs=[zblk] + _gm_param_specs(),
        out_shape=[jax.ShapeDtypeStruct((t, 2 * GM_WIDTH), bf16),
                   jax.ShapeDtypeStruct((GM_HEADS, GM_BLOCK, GM_BLOCK), f32),
                   jax.ShapeDtypeStruct((GM_HEADS, GM_BLOCK, 1), f32),
                   jax.ShapeDtypeStruct((1, GM_WIDTH), f32), jax.ShapeDtypeStruct((1, GM_WIDTH), f32)],
        semantics=("arbitrary",), riders=riders,
    )(z, ws, bs, lng, lnb, dgated)


def _hg_block_fn(qp, fz, iv, gp, s0, lb, gn):
    n, ns, d = HG_SUB, HG_TOKENS // HG_SUB, HG_DIM
    p, nb, per_sub = HG_PAIR, HG_TOKENS // HG_PAIR, HG_SUB // HG_PAIR
    f = lb + (1.0 - lb) * jax.nn.sigmoid(fz)
    g = jnp.log(f)
    k = 1.0 - f
    q = qp * jax.nn.sigmoid(qp)
    v = iv.astype(bf16)
    row = lax.broadcasted_iota(jnp.int32, (HG_TOKENS, HG_TOKENS), 0)
    col = lax.broadcasted_iota(jnp.int32, (HG_TOKENS, HG_TOKENS), 1)
    same_sub = col // n == row // n
    tri = ((col <= row) & same_sub).astype(f32)
    cum = jnp.dot(tri, g, precision=lax.Precision.HIGHEST, preferred_element_type=f32)
    cum_b, q_b, k_b, f_b = cum.reshape(nb, p, d), q.reshape(nb, p, d), k.reshape(nb, p, d), f.reshape(nb, p, d)
    j_b = lax.broadcasted_iota(jnp.int32, (nb, p, d), 1)
    first = (row // p) * p
    scores_t = jnp.zeros((HG_TOKENS, HG_TOKENS), f32)
    decayed = jnp.zeros((nb, p, d), f32)
    for i in range(p):
        decayed = jnp.where(j_b == i, k_b, decayed * f_b[:, i:i + 1, :])
        pair = jnp.sum(q_b[:, i:i + 1, :] * decayed, axis=2, keepdims=True)
        scores_t = scores_t + jnp.where(col == first + i, pair.reshape(HG_TOKENS, 1), 0.0)
    o = lax.dot_general(scores_t.astype(bf16), v, (((0,), (0,)), ((), ())), preferred_element_type=f32)
    last = cum_b[:, p - 1:p, :]
    before = jnp.concatenate([jnp.zeros((1, 1, d), f32), last[:-1]], axis=0)
    before = jnp.broadcast_to(before, (nb, p, d)).reshape(HG_TOKENS, d)
    block = (lax.broadcasted_iota(jnp.int32, (HG_TOKENS, d), 0) // p) % per_sub
    q_late = q * jnp.exp(jnp.where(block > 0, cum - before, -1e30))
    last_s = last.reshape(ns, per_sub, d)
    q_parts, k_parts = [], []
    for m in range(1, per_sub):
        split = jnp.broadcast_to(last_s[:, m - 1:m, :], (ns, n, d)).reshape(HG_TOKENS, d)
        k_parts.append(k * jnp.exp(jnp.where(block < m, split - cum, -1e30)))
        q_parts.append(jnp.where(block == m, q_late, 0.0))
    scores = lax.dot_general(jnp.concatenate(q_parts, axis=1).astype(bf16), jnp.concatenate(k_parts, axis=1).astype(bf16),
                             (((1,), (1,)), ((), ())), preferred_element_type=f32)
    o = o + jnp.dot(jnp.where(same_sub, scores, 0.0).astype(bf16), v, preferred_element_type=f32)
    cum_s = cum.reshape(ns, n, d)
    tot = cum_s[:, n - 1:n, :]
    kt_t = (k.reshape(ns, n, d) * jnp.exp(tot - cum_s)).reshape(HG_TOKENS, d).T
    lane_sub = lax.broadcasted_iota(jnp.int32, (d, HG_TOKENS), 1) // n
    k_by_sub = jnp.concatenate([jnp.where(lane_sub == b, kt_t, 0.0) for b in range(ns)], axis=0).astype(bf16)
    update = jnp.dot(k_by_sub, v, preferred_element_type=f32)
    decay = jnp.exp(tot.reshape(ns, d)).T
    state = s0
    states = []
    for a in range(ns):
        states.append(state.astype(bf16))
        state = decay[:, a:a + 1] * state + update[a * d:(a + 1) * d]
    qt = q * jnp.exp(cum)
    row_sub = lax.broadcasted_iota(jnp.int32, (HG_TOKENS, d), 0) // n
    q_by_sub = jnp.concatenate([jnp.where(row_sub == a, qt, 0.0) for a in range(ns)], axis=1).astype(bf16)
    o = o + jnp.dot(q_by_sub, jnp.concatenate(states, axis=0), preferred_element_type=f32)
    on = o * lax.rsqrt(jnp.mean(o * o, axis=-1, keepdims=True) + EPS) * gn
    return on * (gp * jax.nn.sigmoid(gp)), state


def _head_parts(ref, h):
    return [ref[:, p * D + h * HG_DIM:p * D + (h + 1) * HG_DIM] for p in range(4)]


def hg_scan_fwd(proj, lb, gn, name, riders=None):
    t = proj.shape[0]
    nt = t // HG_TOKENS

    def body(p_ref, lb_ref, gn_ref, y_ref, s_ref, state):
        @pl.when(pl.program_id(0) == 0)
        def _():
            state[...] = jnp.zeros_like(state)

        for h in range(HG_HEADS):
            cols = slice(h * HG_DIM, (h + 1) * HG_DIM)
            s_ref[h, 0] = state[h]
            y, s1 = _hg_block_fn(*_head_parts(p_ref, h), state[h], lb_ref[:, cols], gn_ref[:, cols])
            y_ref[:, cols] = y.astype(bf16)
            state[h] = s1

    return _call(
        body, name=name, grid=(nt,),
        in_specs=[pl.BlockSpec((HG_TOKENS, 4 * D), lambda i: (i, 0)), _row(D), _row(D)],
        out_specs=[pl.BlockSpec((HG_TOKENS, D), lambda i: (i, 0)),
                   pl.BlockSpec((HG_HEADS, 1, HG_DIM, HG_DIM), lambda i: (0, i, 0, 0))],
        out_shape=[jax.ShapeDtypeStruct((t, D), bf16), jax.ShapeDtypeStruct((HG_HEADS, nt, HG_DIM, HG_DIM), f32)],
        scratch_shapes=[pltpu.VMEM((HG_HEADS, HG_DIM, HG_DIM), f32)],
        semantics=("arbitrary",), riders=riders,
    )(proj, lb, gn)


def hg_scan_bwd(proj, lb, gn, states, dy, name, riders=None):
    t = proj.shape[0]
    nt = t // HG_TOKENS

    def body(p_ref, lb_ref, gn_ref, s_ref, dy_ref, dp_ref, dlb_ref, dgn_ref, dstate):
        @pl.when(pl.program_id(0) == 0)
        def _():
            dstate[...] = jnp.zeros_like(dstate)
            dlb_ref[...] = jnp.zeros_like(dlb_ref)
            dgn_ref[...] = jnp.zeros_like(dgn_ref)

        for h in range(HG_HEADS):
            cols = slice(h * HG_DIM, (h + 1) * HG_DIM)
            _, vjp = jax.vjp(_hg_block_fn, *_head_parts(p_ref, h), s_ref[h, 0], lb_ref[:, cols], gn_ref[:, cols])
            grads = vjp((dy_ref[:, cols].astype(f32), dstate[h]))
            for p in range(4):
                dp_ref[:, p * D + h * HG_DIM:p * D + (h + 1) * HG_DIM] = grads[p].astype(bf16)
            dstate[h] = grads[4]
            dlb_ref[:, cols] += grads[5]
            dgn_ref[:, cols] += grads[6]

    small = jax.ShapeDtypeStruct((1, D), f32)
    return _call(
        body, name=name, grid=(nt,),
        in_specs=[pl.BlockSpec((HG_TOKENS, 4 * D), lambda i: (nt - 1 - i, 0)), _row(D), _row(D),
                  pl.BlockSpec((HG_HEADS, 1, HG_DIM, HG_DIM), lambda i: (0, nt - 1 - i, 0, 0)),
                  pl.BlockSpec((HG_TOKENS, D), lambda i: (nt - 1 - i, 0))],
        out_specs=[pl.BlockSpec((HG_TOKENS, 4 * D), lambda i: (nt - 1 - i, 0)), _row(D), _row(D)],
        out_shape=[jax.ShapeDtypeStruct((t, 4 * D), bf16), small, small],
        scratch_shapes=[pltpu.VMEM((HG_HEADS, HG_DIM, HG_DIM), f32)],
        semantics=("arbitrary",), riders=riders,
    )(proj, lb, gn, states, dy)


FFN_COLS = 1408
HALO = 8
STRIP = 16


def _ffn_specs(tm):
    nb = tm // HALO
    main_g = pl.BlockSpec((tm, FFN_COLS), lambda j, i: (i, j))
    main_v = pl.BlockSpec((tm, FFN_COLS), lambda j, i: (i, j + 2))
    halo_g = pl.BlockSpec((HALO, FFN_COLS), lambda j, i: (jnp.maximum(i * nb - 1, 0), j))
    halo_v = pl.BlockSpec((HALO, FFN_COLS), lambda j, i: (jnp.maximum(i * nb - 1, 0), j + 2))
    w_g = pl.BlockSpec((3, FFN_COLS), lambda j, i: (0, j))
    w_v = pl.BlockSpec((3, FFN_COLS), lambda j, i: (0, j + 2))
    b_g = pl.BlockSpec((1, FFN_COLS), lambda j, i: (0, j))
    b_v = pl.BlockSpec((1, FFN_COLS), lambda j, i: (0, j + 2))
    return [main_g, halo_g, main_v, halo_v, w_g, w_v, b_g, b_v]


def _strip_rows(r):
    return pl.ds(r * STRIP, STRIP) if isinstance(r, int) else pl.ds(pl.multiple_of(r * STRIP, STRIP), STRIP)


def _for_strips(nstrip, strip, reverse=False):
    if reverse:
        strip(nstrip - 1, True)
        lax.fori_loop(0, nstrip - 1, lambda k, c: (strip(nstrip - 2 - k, False), c)[1], 0)
    else:
        strip(0, True)
        lax.fori_loop(1, nstrip, lambda r, c: (strip(r, False), c)[1], 0)


def _conv_strip(main_ref, halo_ref, w_ref, b_ref, r, edge, cols, rowi):
    cur = main_ref[_strip_rows(r), cols].astype(f32)
    if edge:
        h = jnp.where(pl.program_id(1) == 0, 0.0, halo_ref[:, cols].astype(f32))
        prev = jnp.concatenate([jnp.zeros_like(h), h], axis=0)
    else:
        prev = main_ref[_strip_rows(r - 1), cols].astype(f32)
    a1 = jnp.where(rowi < 1, pltpu.roll(prev, 1, axis=0), pltpu.roll(cur, 1, axis=0))
    a2 = jnp.where(rowi < 2, pltpu.roll(prev, 2, axis=0), pltpu.roll(cur, 2, axis=0))
    y = b_ref[:, cols] + w_ref[0:1, cols] * a2 + w_ref[1:2, cols] * a1 + w_ref[2:3, cols] * cur
    return y, (cur, a1, a2)


def ffn_gate_fwd(a, cw, cb, name, tm=512, riders=None):
    t = a.shape[0]
    tm = _tile(t, tm, STRIP)

    def body(ag_ref, hg_ref, av_ref, hv_ref, wg_ref, wv_ref, bg_ref, bv_ref, o_ref):
        rowi = lax.broadcasted_iota(jnp.int32, (STRIP, LANES), 0)

        def strip(r, edge):
            for c in range(FFN_COLS // LANES):
                cols = pl.ds(c * LANES, LANES)
                yg, _ = _conv_strip(ag_ref, hg_ref, wg_ref, bg_ref, r, edge, cols, rowi)
                yv, _ = _conv_strip(av_ref, hv_ref, wv_ref, bv_ref, r, edge, cols, rowi)
                o_ref[_strip_rows(r), cols] = (_gelu(yg) * yv).astype(bf16)

        _for_strips(tm // STRIP, strip)

    return _call(
        body, name=name, grid=(2, t // tm), in_specs=_ffn_specs(tm),
        out_specs=[pl.BlockSpec((tm, FFN_COLS), lambda j, i: (i, j))],
        out_shape=[jax.ShapeDtypeStruct((t, FFN_HIDDEN), bf16)],
        semantics=("parallel", "arbitrary"), riders=riders,
    )(a, a, a, a, cw, cw, cb, cb)


def ffn_gate_bwd(a, cw, cb, dhid, name, tm=512, riders=None):
    t = a.shape[0]
    tm = _tile(t, tm, STRIP)

    def body(ag_ref, hg_ref, av_ref, hv_ref, wg_ref, wv_ref, bg_ref, bv_ref, dh_ref,
             dy_ref, dwg_ref, dwv_ref, dbg_ref, dbv_ref, acc):
        rowi = lax.broadcasted_iota(jnp.int32, (STRIP, LANES), 0)

        @pl.when(pl.program_id(1) == 0)
        def _():
            acc[...] = jnp.zeros_like(acc)

        def strip(r, edge):
            rows = _strip_rows(r)
            for c in range(FFN_COLS // LANES):
                cols = pl.ds(c * LANES, LANES)
                yg, taps_g = _conv_strip(ag_ref, hg_ref, wg_ref, bg_ref, r, edge, cols, rowi)
                yv, taps_v = _conv_strip(av_ref, hv_ref, wv_ref, bv_ref, r, edge, cols, rowi)
                dh = dh_ref[rows, cols].astype(f32)
                cdf = 0.5 * (1.0 + lax.erf(yg * (1.0 / math.sqrt(2.0))))
                pdf = jnp.exp(-0.5 * yg * yg) * (1.0 / math.sqrt(2.0 * math.pi))
                dyg = dh * yv * (cdf + yg * pdf)
                dyv = dh * (yg * cdf)
                dy_ref[0, rows, cols] = dyg.astype(bf16)
                dy_ref[1, rows, cols] = dyv.astype(bf16)
                for p, (dy, (a0, a1, a2)) in enumerate(((dyg, taps_g), (dyv, taps_v))):
                    acc[4 * p + 0, :, cols] += dy * a2
                    acc[4 * p + 1, :, cols] += dy * a1
                    acc[4 * p + 2, :, cols] += dy * a0
                    acc[4 * p + 3, :, cols] += dy

        _for_strips(tm // STRIP, strip)

        @pl.when(pl.program_id(1) == pl.num_programs(1) - 1)
        def _():
            for p, (dw_ref, db_ref) in enumerate(((dwg_ref, dbg_ref), (dwv_ref, dbv_ref))):
                for tap in range(3):
                    dw_ref[tap:tap + 1, :] = jnp.sum(acc[4 * p + tap], axis=0, keepdims=True)
                db_ref[...] = jnp.sum(acc[4 * p + 3], axis=0, keepdims=True)

    half_w = pl.BlockSpec((3, FFN_COLS), lambda j, i: (0, j))
    half_b = pl.BlockSpec((1, FFN_COLS), lambda j, i: (0, j))
    return _call(
        body, name=name, grid=(2, t // tm),
        in_specs=_ffn_specs(tm) + [pl.BlockSpec((tm, FFN_COLS), lambda j, i: (i, j))],
        out_specs=[pl.BlockSpec((2, tm, FFN_COLS), lambda j, i: (0, i, j)), half_w, half_w, half_b, half_b],
        out_shape=[jax.ShapeDtypeStruct((2, t, FFN_HIDDEN), bf16),
                   jax.ShapeDtypeStruct((3, FFN_HIDDEN), f32), jax.ShapeDtypeStruct((3, FFN_HIDDEN), f32),
                   jax.ShapeDtypeStruct((1, FFN_HIDDEN), f32), jax.ShapeDtypeStruct((1, FFN_HIDDEN), f32)],
        scratch_shapes=[pltpu.VMEM((8, STRIP, FFN_COLS), f32)],
        semantics=("parallel", "arbitrary"), riders=riders,
    )(a, a, a, a, cw, cw, cb, cb, dhid)


def conv_transpose(dy, cw, name, tm=512):
    _, t, fh = dy.shape
    tm = _tile(t, tm, STRIP)
    nb = tm // HALO
    last_halo = t // HALO - 1
    ncol = fh // FFN_COLS

    def body(main_ref, halo_ref, w_ref, o_ref):
        rowi = lax.broadcasted_iota(jnp.int32, (STRIP, LANES), 0)
        last_block = pl.program_id(2) == pl.num_programs(2) - 1

        def strip(r, edge):
            rows = _strip_rows(r)
            for c in range(FFN_COLS // LANES):
                cols = pl.ds(c * LANES, LANES)
                cur = main_ref[0, rows, cols].astype(f32)
                if edge:
                    h = jnp.where(last_block, 0.0, halo_ref[0, :, cols].astype(f32))
                    nxt = jnp.concatenate([h, jnp.zeros_like(h)], axis=0)
                else:
                    nxt = main_ref[0, _strip_rows(r + 1), cols].astype(f32)
                d1 = jnp.where(rowi >= STRIP - 1, pltpu.roll(nxt, STRIP - 1, axis=0), pltpu.roll(cur, STRIP - 1, axis=0))
                d2 = jnp.where(rowi >= STRIP - 2, pltpu.roll(nxt, STRIP - 2, axis=0), pltpu.roll(cur, STRIP - 2, axis=0))
                o_ref[rows, cols] = (w_ref[2:3, cols] * cur + w_ref[1:2, cols] * d1 + w_ref[0:1, cols] * d2).astype(bf16)

        _for_strips(tm // STRIP, strip, reverse=True)

    return pl.pallas_call(
        body, name=name, grid=(2, ncol, t // tm),
        in_specs=[pl.BlockSpec((1, tm, FFN_COLS), lambda p, j, i: (p, i, j)),
                  pl.BlockSpec((1, HALO, FFN_COLS), lambda p, j, i: (p, jnp.minimum((i + 1) * nb, last_halo), j)),
                  pl.BlockSpec((3, FFN_COLS), lambda p, j, i: (0, p * ncol + j))],
        out_specs=pl.BlockSpec((tm, FFN_COLS), lambda p, j, i: (i, p * ncol + j)),
        out_shape=jax.ShapeDtypeStruct((t, 2 * fh), bf16),
        compiler_params=_cparams("parallel", "parallel", "arbitrary"),
    )(dy, dy, cw)


def ada_mod(c_all, ada_w, ada_b_cols, name):
    cols = ada_w.shape[2]

    def body(c_ref, w_ref, b_ref, o_ref):
        c = c_ref[...]
        cond = (c * jax.nn.sigmoid(c)).astype(bf16)
        o_ref[0] = jnp.dot(cond, w_ref[0].astype(bf16), preferred_element_type=f32) + b_ref[0]

    return pl.pallas_call(
        body, name=name, grid=(DEPTH,),
        in_specs=[pl.BlockSpec((N_DEV, D), lambda i: (0, 0)), pl.BlockSpec((1, D, cols), lambda i: (i, 0, 0)),
                  pl.BlockSpec((1, 1, cols), lambda i: (i, 0, 0))],
        out_specs=pl.BlockSpec((1, N_DEV, cols), lambda i: (i, 0, 0)),
        out_shape=jax.ShapeDtypeStruct((DEPTH, N_DEV, cols), f32), compiler_params=_cparams("parallel"),
    )(c_all, ada_w, ada_b_cols)


def ada_grads(c_all, dmod_cols, dmod_all, name):
    cols = dmod_cols.shape[2]

    def body(c_ref, dm_ref, da_ref, dw_ref, db_ref):
        c = c_ref[...]
        cond = c * jax.nn.sigmoid(c)
        dw_ref[0] = lax.dot_general(cond, dm_ref[0], (((0,), (0,)), ((), ())), precision=lax.Precision.HIGHEST,
                                    preferred_element_type=f32)
        acc = da_ref[0, 0]
        for e in range(1, N_DEV):
            acc = acc + da_ref[e, 0]
        db_ref[0] = acc

    return pl.pallas_call(
        body, name=name, grid=(DEPTH,),
        in_specs=[pl.BlockSpec((N_DEV, D), lambda i: (0, 0)), pl.BlockSpec((1, N_DEV, cols), lambda i: (i, 0, 0)),
                  pl.BlockSpec((N_DEV, 1, 1, 6 * D), lambda i: (0, i, 0, 0))],
        out_specs=(pl.BlockSpec((1, D, cols), lambda i: (i, 0, 0)), pl.BlockSpec((1, 1, 6 * D), lambda i: (i, 0, 0))),
        out_shape=(jax.ShapeDtypeStruct((DEPTH, D, cols), f32), jax.ShapeDtypeStruct((DEPTH, 1, 6 * D), f32)),
        compiler_params=_cparams("parallel"),
    )(c_all, dmod_cols, dmod_all)


def lower_bound_fwd(hg_lb, name):
    n = hg_lb.shape[1]

    def body(l_ref, o_ref):
        o_ref[...] = jax.nn.sigmoid(l_ref[1:2, :] - l_ref[0:1, :])

    return pl.pallas_call(body, name=name, out_shape=jax.ShapeDtypeStruct((1, n), f32))(hg_lb)


def lower_bound_bwd(hg_lb, dlb, name):
    n = hg_lb.shape[1]

    def body(l_ref, d_ref, o_ref):
        p = jax.nn.sigmoid(l_ref[1:2, :] - l_ref[0:1, :])
        g = d_ref[...] * p * (1.0 - p)
        o_ref[0:1, :] = -g
        o_ref[1:2, :] = g

    return pl.pallas_call(body, name=name, out_shape=jax.ShapeDtypeStruct((2, n), f32))(hg_lb, dlb)


def _adamw(w, g, m, v):
    m = ADAM_B1 * m + (1.0 - ADAM_B1) * g
    v = ADAM_B2 * v + (1.0 - ADAM_B2) * jnp.square(g)
    m_hat = m / (1.0 - ADAM_B1 ** ADAM_STEP)
    v_hat = v / (1.0 - ADAM_B2 ** ADAM_STEP)
    delta = -ADAM_LR * (m_hat / (jnp.sqrt(v_hat) + ADAM_EPS) + ADAM_WD * w)
    return delta, m, v


ADAM_BLOCK_BYTES = 24 * 1024 * 1024


def adam_reduced(parts, w, m, v, name):
    layers, r, c = w.shape
    flat = [p for layer_parts in parts for p in layer_parts]
    rows = flat[0].shape[1]
    assert all(p.shape == (N_DEV, rows, c) for p in flat) and rows * len(flat) == layers * r
    per_layer = r // rows
    row_bytes = 2 * (len(flat) * N_DEV * c * 2 + 7 * c * 4)
    tr = _tile(rows, max(16, ADAM_BLOCK_BYTES // row_bytes), 16)
    steps = rows // tr

    def body(*refs):
        p_refs = refs[:len(flat)]
        w_ref, m_ref, v_ref, g_ref, d_ref, mo_ref, vo_ref = refs[len(flat):]
        for idx in range(len(flat)):
            @pl.when(pl.program_id(0) == idx)
            def _():
                g = p_refs[idx][0].astype(f32)
                for j in range(1, N_DEV):
                    g = g + p_refs[idx][j].astype(f32)
                g_ref[...] = g
                d_ref[...], mo_ref[...], vo_ref[...] = _adamw(w_ref[...], g, m_ref[...], v_ref[...])

    def part_spec(idx):
        return pl.BlockSpec((N_DEV, tr, c), lambda p, i: (0, jnp.where(p == idx, i, 0), 0))

    blk = pl.BlockSpec((None, tr, c), lambda p, i: (p // per_layer, (p % per_layer) * steps + i, 0))
    out = jax.ShapeDtypeStruct((layers, r, c), f32)
    return pl.pallas_call(
        body, name=name, grid=(len(flat), steps),
        in_specs=[part_spec(idx) for idx in range(len(flat))] + [blk, blk, blk],
        out_specs=(blk, blk, blk, blk), out_shape=(out, out, out, out), compiler_params=_cparams("arbitrary", "arbitrary"),
    )(*flat, w, m, v)


def adam_plain(g, w, m, v, name, tr=256):
    r, c = w.shape
    tr = _tile(r, tr, 8)

    def body(g_ref, w_ref, m_ref, v_ref, d_ref, mo_ref, vo_ref):
        d_ref[...], mo_ref[...], vo_ref[...] = _adamw(w_ref[...], g_ref[...], m_ref[...], v_ref[...])

    blk = pl.BlockSpec((tr, c), lambda i: (i, 0))
    out = jax.ShapeDtypeStruct((r, c), f32)
    return pl.pallas_call(
        body, name=name, grid=(r // tr,), in_specs=[blk, blk, blk, blk], out_specs=(blk, blk, blk),
        out_shape=(out, out, out), compiler_params=_cparams("parallel"),
    )(g, w, m, v)


def sum_parts(parts, name):
    _, r, c = parts.shape

    def body(p_ref, o_ref):
        acc = p_ref[0]
        for j in range(1, N_DEV):
            acc = acc + p_ref[j]
        o_ref[...] = acc

    return pl.pallas_call(body, name=name, out_shape=jax.ShapeDtypeStruct((r, c), f32))(parts)


def _pack(arrs, rows_mult=8):
    flat = jnp.concatenate([a.reshape(-1) for a in arrs])
    rows = -(-flat.shape[0] // LANES)
    rows = -(-rows // rows_mult) * rows_mult
    return jnp.pad(flat, (0, rows * LANES - flat.shape[0])).reshape(rows, LANES)


def _unpack(flat, shapes):
    out, at = [], 0
    for s in shapes:
        n = math.prod(s)
        out.append(flat[at:at + n].reshape(s))
        at += n
    return out


def kernel(x, c, gm_w_in, gm_ln_g, gm_ln_b, gm_w_s, gm_b_s, gm_w_out, hg_w_in, hg_lb, hg_gn_g, hg_w_out, ffn_w_up, ffn_conv_w, ffn_conv_b, ffn_w_down, norm_g, ada_w, ada_b, final_g, loss_target, m_gm_w_in, m_gm_ln_g, m_gm_ln_b, m_gm_w_s, m_gm_b_s, m_gm_w_out, m_hg_w_in, m_hg_lb, m_hg_gn_g, m_hg_w_out, m_ffn_w_up, m_ffn_conv_w, m_ffn_conv_b, m_ffn_w_down, m_norm_g, m_ada_w, m_ada_b, m_final_g, v_gm_w_in, v_gm_ln_g, v_gm_ln_b, v_gm_w_s, v_gm_b_s, v_gm_w_out, v_hg_w_in, v_hg_lb, v_hg_gn_g, v_hg_w_out, v_ffn_w_up, v_ffn_conv_w, v_ffn_conv_b, v_ffn_w_down, v_norm_g, v_ada_w, v_ada_b, v_final_g):
    me = _flat(_mesh_pos())
    xt = x[0]
    t = xt.shape[0]

    small_shapes = [(1, D), (2, HG_DIM), (2, HG_DIM), (DEPTH, 2, HG_DIM), (DEPTH, 3, 2 * FFN_HIDDEN // N_DEV)]
    (small_all,) = all_gather([_pack([c, hg_lb, hg_gn_g, norm_g, ffn_conv_w])], "gather_small")
    small_all = small_all.reshape(N_DEV, -1)
    at = 0
    pieces = []
    for s in small_shapes:
        n = math.prod(s)
        pieces.append(small_all[:, at:at + n].reshape((N_DEV,) + s))
        at += n
    c_all = pieces[0].reshape(N_DEV, D)
    hg_lb_full = jnp.transpose(pieces[1], (1, 0, 2)).reshape(2, D)
    hg_gn_full = jnp.transpose(pieces[2], (1, 0, 2)).reshape(2, D)
    norm_g_full = jnp.transpose(pieces[3], (1, 2, 0, 3)).reshape(DEPTH, 2, D)
    conv_w_full = jnp.transpose(pieces[4], (1, 2, 0, 3)).reshape(DEPTH, 3, 2 * FFN_HIDDEN)

    lb1 = lower_bound_fwd(hg_lb_full, "lower_bound")
    lbs = [jnp.zeros((1, D), f32), lb1]

    ada_b_cols = lax.dynamic_slice(ada_b, (0, me * ADA_COLS), (DEPTH, ADA_COLS)).reshape(DEPTH, 1, ADA_COLS)
    mod_cols = ada_mod(c_all, ada_w, ada_b_cols, "ada_mod")
    (mod_mine,) = all_to_all([jnp.transpose(mod_cols, (1, 0, 2))], "mod_to_examples")
    mod = jnp.transpose(mod_mine, (1, 0, 2)).reshape(DEPTH, 6, 1, D)

    def layer_shards(i):
        j = i // 2
        w_in, w_out = (gm_w_in, gm_w_out) if i % 2 == 0 else (hg_w_in, hg_w_out)
        return [w_in[j].astype(bf16), w_out[j].astype(bf16), ffn_w_up[i].astype(bf16), ffn_w_down[i].astype(bf16)]

    def full_rows(g):
        return g.reshape(N_DEV * g.shape[1], g.shape[2])

    carried_by = {
        "in_0": [(0, 1), (0, 2)], "mix_0": [(0, 3)], "up_0": [(1, 0), (1, 1)], "gate_0": [(1, 2)], "down_0": [(1, 3)],
        "in_1": [(2, 0)], "mix_1": [(2, 1), (2, 2), (2, 3)], "up_1": [(3, 0), (3, 1)], "gate_1": [(3, 2)], "down_1": [(3, 3)],
    }
    shards = [layer_shards(i) for i in range(DEPTH)]
    gathered = {}
    (gathered[(0, 0)],) = all_gather([shards[0][0]], "gather_weights_0", relay=True)

    def carry(call, site, **kw):
        items = carried_by.get(site, [])
        outs = call(riders=Riders([shards[l][slot] for l, slot in items], True), **kw)
        for item, g in zip(items, outs[len(outs) - len(items):]):
            gathered[item] = g
        return outs[:len(outs) - len(items)]

    saved = []
    weights = []
    xcur = xt
    h = norm_fwd(xcur, norm_g_full[0, 0:1], mod[0, 1], mod[0, 0], "norm1_0")
    for i in range(DEPTH):
        j = i // 2
        sh1, sc1, g1, sh2, sc2, g2 = [mod[i, p] for p in range(6)]
        gn2 = norm_g_full[i, 1:2]
        s = {"x0": xcur, "h": h}
        w_in = gathered[(i, 0)]
        if i % 2 == 0:
            (z,) = carry(functools.partial(mm_nn, h, w_in, bf16, f"gm_in_{i}"), f"in_{i}")
            bs = gm_b_s[j].reshape(GM_HEADS, GM_BLOCK, 1)
            (mixed,) = carry(functools.partial(gm_mix_fwd, z, gm_w_s[j], bs, gm_ln_g[j:j + 1], gm_ln_b[j:j + 1],
                                               f"gm_mix_{i}"), f"mix_{i}")
            s["z"] = z
        else:
            (proj,) = carry(functools.partial(mm_nn, h, w_in, f32, f"hg_in_{i}"), f"in_{i}")
            mixed, states = carry(functools.partial(hg_scan_fwd, proj, lbs[j], hg_gn_full[j:j + 1], f"hg_scan_{i}"),
                                  f"mix_{i}")
            s["proj"], s["states"] = proj, states
        s["mixed"] = mixed
        w_out = full_rows(gathered[(i, 1)])
        y, x1, h2 = carry(functools.partial(mm_nn_residual, mixed, w_out, xcur, g1, (gn2, sc2, sh2), f"mix_out_{i}"),
                          f"out_{i}")
        s["y"], s["x1"] = y, x1
        w_up = gathered[(i, 2)]
        (a,) = carry(functools.partial(mm_nn, h2, w_up, bf16, f"ffn_up_{i}"), f"up_{i}")
        (hid,) = carry(functools.partial(ffn_gate_fwd, a, conv_w_full[i], ffn_conv_b[i:i + 1], f"ffn_gate_{i}"), f"gate_{i}")
        w_down = full_rows(gathered[(i, 3)])
        next_norm = (norm_g_full[i + 1, 0:1], mod[i + 1, 1], mod[i + 1, 0]) if i + 1 < DEPTH else None
        outs = carry(functools.partial(mm_nn_residual, hid, w_down, x1, g2, next_norm, f"ffn_down_{i}"), f"down_{i}")
        fo, x2 = outs[0], outs[1]
        s["h2"], s["a"], s["hid"], s["f"] = h2, a, hid, fo
        weights.append((w_in, w_out, w_up, w_down))
        saved.append(s)
        xcur = x2
        h = outs[2] if next_norm is not None else None

    loss_part, dx, d_final_g, dg2, df = loss_head(xcur, final_g.reshape(1, D), loss_target[0], saved[-1]["f"],
                                                  mod[DEPTH - 1, 5], "loss_head")
    loss = lax.psum(loss_part[0, 0], ("x", "y", "c"))

    def halves(blocked):
        rows = blocked.shape[1] // 2
        return [(blocked, (0, rows)), (blocked, (rows, rows))]

    def by_owner_rows(dw):
        k, n = dw.shape
        return dw.reshape(N_DEV, k // N_DEV, n)

    received = [[[] for _ in range(4)] for _ in range(DEPTH)]

    def send(call, items, **kw):
        outs = call(riders=Riders([arr for arr, _ in items], False), **kw)
        for (_, (layer, slot)), got in zip(items, outs[len(outs) - len(items):]):
            received[layer][slot].append(got)
        return outs[:len(outs) - len(items)]

    dmod = [None] * DEPTH
    d_norm_g = [None] * DEPTH
    d_gm = {k: [None, None] for k in ("ws", "bs", "lng", "lnb")}
    d_hg = {k: [None, None] for k in ("lb", "gn")}
    d_ffn = {k: [None] * DEPTH for k in ("cw", "cb")}
    in_halves = []
    for i in reversed(range(DEPTH)):
        j = i // 2
        s = saved[i]
        w_in, w_out, w_up, w_down = weights[i]
        sh1, sc1, g1, sh2, sc2, g2 = [mod[i, p] for p in range(6)]
        gn1, gn2 = norm_g_full[i, 0:1], norm_g_full[i, 1:2]
        (dw_down,) = mm_tn(s["hid"], df, bf16, f"dw_down_{i}", tn=D)
        (dhid,) = send(functools.partial(mm_nt, df, w_down, bf16, f"dhid_{i}"), in_halves[:1])
        dyc, dwg, dwv, dbg, dbv = send(
            functools.partial(ffn_gate_bwd, s["a"], conv_w_full[i], ffn_conv_b[i:i + 1], dhid, f"ffn_gate_bwd_{i}"),
            in_halves[1:] + [(by_owner_rows(dw_down), (i, 3))])
        d_ffn["cw"][i] = jnp.concatenate([dwg, dwv], axis=1)
        d_ffn["cb"][i] = jnp.concatenate([dbg, dbv], axis=1)
        da = conv_transpose(dyc, conv_w_full[i], f"conv_t_{i}")
        (dw_up,) = mm_tn_by_owner(s["h2"], da, f"dw_up_{i}")
        up_halves = [(part, (i, 2)) for part in halves(dw_up)]
        (dh2,) = send(functools.partial(mm_nt, da, w_up, bf16, f"dh2_{i}"), up_halves[:1])
        dx1, dgn2, dsc2, dsh2, dg1, dy = norm_bwd(s["x1"], gn2, sc2, sh2, dh2, dx, f"norm2_bwd_{i}", gate=(s["y"], g1))
        (dw_out,) = mm_tn(s["mixed"], dy, bf16, f"dw_mix_out_{i}")
        (dmixed,) = mm_nt(dy, w_out, bf16, f"dmixed_{i}")
        if i % 2 == 0:
            bs = gm_b_s[j].reshape(GM_HEADS, GM_BLOCK, 1)
            dpre, dws, dbs, dlng, dlnb = send(
                functools.partial(gm_mix_bwd, s["z"], gm_w_s[j], bs, gm_ln_g[j:j + 1], gm_ln_b[j:j + 1], dmixed,
                                  f"gm_mix_bwd_{i}"), up_halves[1:])
            d_gm["ws"][j], d_gm["bs"][j], d_gm["lng"][j], d_gm["lnb"][j] = dws, dbs.reshape(GM_HEADS, GM_BLOCK), dlng, dlnb
        else:
            dpre, dlb, dgn = send(
                functools.partial(hg_scan_bwd, s["proj"], lbs[j], hg_gn_full[j:j + 1], s["states"], dmixed,
                                  f"hg_scan_bwd_{i}"), up_halves[1:])
            d_hg["lb"][j], d_hg["gn"][j] = dlb, dgn
        (dw_in,) = send(functools.partial(mm_tn_by_owner, s["h"], dpre, f"dw_mix_in_{i}"), [(by_owner_rows(dw_out), (i, 1))])
        in_halves = [(part, (i, 0)) for part in halves(dw_in)]
        (dh,) = send(functools.partial(mm_nt, dpre, w_in, bf16, f"dh_mix_{i}"), in_halves[:1] if i == 0 else [])
        dmod_i = [None, None, dg1, dsh2, dsc2, dg2]
        if i > 0:
            dx, dgn1, dsc1, dsh1, dg2, df = norm_bwd(s["x0"], gn1, sc1, sh1, dh, dx1, f"norm1_bwd_{i}",
                                                     gate=(saved[i - 1]["f"], mod[i - 1, 5]))
        else:
            dx, dgn1, dsc1, dsh1 = send(functools.partial(norm_bwd, s["x0"], gn1, sc1, sh1, dh, dx1, f"norm1_bwd_{i}"),
                                        in_halves[1:])
        dmod_i[0], dmod_i[1] = dsh1, dsc1
        dmod[i] = jnp.concatenate(dmod_i, axis=1)
        d_norm_g[i] = jnp.concatenate([dgn1, dgn2], axis=0)
    grad_x = dx.reshape(1, t, D)

    (dmod_all,) = all_gather([jnp.concatenate(dmod, axis=0)], "gather_dmod")
    dmod_cols = jnp.transpose(lax.dynamic_slice(dmod_all, (0, 0, me * ADA_COLS), (N_DEV, DEPTH, ADA_COLS)), (1, 0, 2))
    g_ada_w, g_ada_b = ada_grads(c_all, dmod_cols, dmod_all.reshape(N_DEV, DEPTH, 1, 6 * D), "ada_grads")
    g_ada_b = g_ada_b.reshape(DEPTH, 6 * D)

    small_partials = [jnp.concatenate(d_gm["lng"], axis=0), jnp.concatenate(d_gm["lnb"], axis=0),
                      jnp.stack(d_gm["ws"]), jnp.stack(d_gm["bs"]), jnp.concatenate(d_ffn["cb"], axis=0),
                      d_final_g, d_hg["lb"][1], jnp.concatenate(d_hg["gn"], axis=0), jnp.stack(d_norm_g),
                      jnp.stack(d_ffn["cw"])]
    partial_shapes = [p.shape for p in small_partials]
    packed = _pack(small_partials, rows_mult=8 * N_DEV)
    rows = packed.shape[0] // N_DEV
    (recv,) = all_to_all([packed.reshape(N_DEV, rows, LANES)], "small_grads_exchange")
    (summed,) = all_gather([sum_parts(recv, "small_grads_sum")], "small_grads_gather")
    g_ln_g, g_ln_b, g_ws, g_bs, g_cb, g_final, g_lb1, g_gn, g_norm, g_cw = _unpack(summed.reshape(-1), partial_shapes)
    g_final = g_final.reshape(D)

    def my_cols(a, n):
        start = (0,) * (a.ndim - 1) + (me * n,)
        return lax.dynamic_slice(a, start, a.shape[:-1] + (n,))

    g_hg_lb = lower_bound_bwd(hg_lb, my_cols(g_lb1, HG_DIM), "lower_bound_bwd")
    g_hg_gn = my_cols(g_gn, HG_DIM)
    g_norm_g = my_cols(g_norm, HG_DIM)
    g_conv_w = my_cols(g_cw, 2 * FFN_HIDDEN // N_DEV)

    def parts_of(slot, layers):
        return [received[i][slot] for i in layers]

    w_shards = [gm_w_in, gm_w_out, hg_w_in, hg_w_out, ffn_w_up, ffn_w_down]
    big_parts = [parts_of(0, (0, 2)), parts_of(1, (0, 2)), parts_of(0, (1, 3)), parts_of(1, (1, 3)),
                 parts_of(2, range(DEPTH)), parts_of(3, range(DEPTH))]
    big_m = [m_gm_w_in, m_gm_w_out, m_hg_w_in, m_hg_w_out, m_ffn_w_up, m_ffn_w_down]
    big_v = [v_gm_w_in, v_gm_w_out, v_hg_w_in, v_hg_w_out, v_ffn_w_up, v_ffn_w_down]
    big = [adam_reduced(parts, w, m_, v_, f"adam_big_{idx}")
           for idx, (w, m_, v_, parts) in enumerate(zip(w_shards, big_m, big_v, big_parts))]
    (g_gm_w_in, d_gm_w_in, nm_gm_w_in, nv_gm_w_in), (g_gm_w_out, d_gm_w_out, nm_gm_w_out, nv_gm_w_out), \
        (g_hg_w_in, d_hg_w_in, nm_hg_w_in, nv_hg_w_in), (g_hg_w_out, d_hg_w_out, nm_hg_w_out, nv_hg_w_out), \
        (g_ffn_w_up, d_ffn_w_up, nm_ffn_w_up, nv_ffn_w_up), (g_ffn_w_down, d_ffn_w_down, nm_ffn_w_down, nv_ffn_w_down) = big

    two_d = (-1, ADA_COLS)
    d_ada_w, nm_ada_w, nv_ada_w = [o.reshape(ada_w.shape) for o in adam_plain(
        g_ada_w.reshape(two_d), ada_w.reshape(two_d), m_ada_w.reshape(two_d), v_ada_w.reshape(two_d), "adam_ada_w")]

    small_g = [g_ln_g, g_ln_b, g_ws, g_bs, g_cb, g_ada_b, g_final, g_hg_lb, g_hg_gn, g_norm_g, g_conv_w]
    small_w = [gm_ln_g, gm_ln_b, gm_w_s, gm_b_s, ffn_conv_b, ada_b, final_g, hg_lb, hg_gn_g, norm_g, ffn_conv_w]
    small_m = [m_gm_ln_g, m_gm_ln_b, m_gm_w_s, m_gm_b_s, m_ffn_conv_b, m_ada_b, m_final_g, m_hg_lb, m_hg_gn_g, m_norm_g, m_ffn_conv_w]
    small_v = [v_gm_ln_g, v_gm_ln_b, v_gm_w_s, v_gm_b_s, v_ffn_conv_b, v_ada_b, v_final_g, v_hg_lb, v_hg_gn_g, v_norm_g, v_ffn_conv_w]
    shapes = [w.shape for w in small_w]
    small_g = [g.reshape(s) for g, s in zip(small_g, shapes)]
    outs = adam_plain(_pack(small_g), _pack(small_w), _pack(small_m), _pack(small_v), "adam_small")
    (d_ln_g, d_ln_b, d_ws, d_bs, d_cb, d_ada_b, d_final, d_hg_lb, d_hg_gn, d_norm_g_, d_conv_w), \
        (nm_ln_g, nm_ln_b, nm_ws, nm_bs, nm_cb, nm_ada_b, nm_final, nm_hg_lb, nm_hg_gn, nm_norm_g, nm_conv_w), \
        (nv_ln_g, nv_ln_b, nv_ws, nv_bs, nv_cb, nv_ada_b, nv_final, nv_hg_lb, nv_hg_gn, nv_norm_g, nv_conv_w) = [
            _unpack(o.reshape(-1), shapes) for o in outs]
    g_ln_g, g_ln_b, g_ws, g_bs, g_cb, g_ada_b, g_final, g_hg_lb, g_hg_gn, g_norm_g, g_conv_w = small_g

    grads = (g_gm_w_in, g_ln_g, g_ln_b, g_ws, g_bs, g_gm_w_out, g_hg_w_in, g_hg_lb, g_hg_gn, g_hg_w_out,
             g_ffn_w_up, g_conv_w, g_cb, g_ffn_w_down, g_norm_g, g_ada_w, g_ada_b, g_final)
    deltas = (d_gm_w_in, d_ln_g, d_ln_b, d_ws, d_bs, d_gm_w_out, d_hg_w_in, d_hg_lb, d_hg_gn, d_hg_w_out,
              d_ffn_w_up, d_conv_w, d_cb, d_ffn_w_down, d_norm_g_, d_ada_w, d_ada_b, d_final)
    new_m = (nm_gm_w_in, nm_ln_g, nm_ln_b, nm_ws, nm_bs, nm_gm_w_out, nm_hg_w_in, nm_hg_lb, nm_hg_gn, nm_hg_w_out,
             nm_ffn_w_up, nm_conv_w, nm_cb, nm_ffn_w_down, nm_norm_g, nm_ada_w, nm_ada_b, nm_final)
    new_v = (nv_gm_w_in, nv_ln_g, nv_ln_b, nv_ws, nv_bs, nv_gm_w_out, nv_hg_w_in, nv_hg_lb, nv_hg_gn, nv_hg_w_out,
             nv_ffn_w_up, nv_conv_w, nv_cb, nv_ffn_w_down, nv_norm_g, nv_ada_w, nv_ada_b, nv_final)
    return (loss, grad_x) + grads + deltas + new_m + new_v
```

```python
import functools
import math

import jax
import jax.numpy as jnp
from jax import lax
from jax.experimental import pallas as pl
from jax.experimental.pallas import tpu as pltpu

f32 = jnp.float32
bf16 = jnp.bfloat16
MESH = pl.DeviceIdType.MESH

N_DEV = 8
D = 1024
DEPTH = 4
EPS = 1e-6
GM_WIDTH = 2048
GM_HEADS = 8
GM_HEAD_DIM = 256
GM_BLOCK = 128
CHUNK = 64
HG_HEADS = 8
HG_DIM = 128
FFN_HIDDEN = 2816
ADA_COLS = 6 * D // N_DEV

HG_SUB = 32
HG_PAIR = 8
HG_TOKENS = 128

ADAM_LR = 0.001
ADAM_B1 = 0.9
ADAM_B2 = 0.999
ADAM_EPS = 1e-08
ADAM_WD = 0.01
ADAM_STEP = 10

V7X_VMEM_LIMIT = 56 * 1024 * 1024
LANES = 128


def _cparams(*sem):
    return pltpu.CompilerParams(dimension_semantics=sem or None, vmem_limit_bytes=V7X_VMEM_LIMIT)


def _tile(n, target, mult=LANES):
    best = None
    for t in range(mult, min(n, target) + 1, mult):
        if n % t == 0:
            best = t
    return best or n


WEIGHT_BLOCK_BYTES = 6 * 1024 * 1024


def _weight_tile(n, k):
    return _tile(n, max(LANES, WEIGHT_BLOCK_BYTES // (2 * k)))


def _gelu(x):
    return 0.5 * x * (1.0 + lax.erf(x * (1.0 / math.sqrt(2.0))))


def _mesh_pos():
    return lax.axis_index("x"), lax.axis_index("y"), lax.axis_index("c")


def _flat(pos):
    return 4 * pos[0] + 2 * pos[1] + pos[2]


def _peer(pos, k):
    return ((1 - pos[0]) if k & 4 else pos[0], (1 - pos[1]) if k & 2 else pos[1], (1 - pos[2]) if k & 1 else pos[2])


def _exchange_copies(ins, outs, send_sems, recv_sems, local_sems, gather):
    pos = _mesh_pos()
    me = _flat(pos)

    def src(i, dest):
        if gather:
            return ins[i]
        ref, rows = ins[i] if isinstance(ins[i], tuple) else (ins[i], None)
        return ref.at[dest] if rows is None else ref.at[dest, pl.ds(*rows)]

    local = [pltpu.make_async_copy(src(i, me), outs[i].at[me], local_sems.at[i]) for i in range(len(ins))]
    sends, recvs = [], []
    for k in range(1, N_DEV):
        peer = _peer(pos, k)
        there = _flat(peer)
        for i in range(len(ins)):
            sems = dict(send_sem=send_sems.at[i * 7 + k - 1], recv_sem=recv_sems.at[i * 7 + k - 1],
                        device_id=peer, device_id_type=MESH)
            sends.append(pltpu.make_async_remote_copy(src_ref=src(i, there), dst_ref=outs[i].at[me], **sems))
            recvs.append(pltpu.make_async_remote_copy(src_ref=src(i, there), dst_ref=outs[i].at[there], **sems))
    return local, sends, recvs


def _exchange_start(*refs):
    local, sends, _ = _exchange_copies(*refs)
    for cp in local + sends:
        cp.start()


def _exchange_wait(*refs):
    local, sends, recvs = _exchange_copies(*refs)
    for cp in recvs:
        cp.wait_recv()
    for cp in sends:
        cp.wait_send()
    for cp in local:
        cp.wait()


OTHER_CHIPS = (2, 4, 6)


def _relay_copies(ins, outs, send_sems, recv_sems, local_sems):
    pos = _mesh_pos()
    me = _flat(pos)
    sibling = _peer(pos, 1)
    local = [pltpu.make_async_copy(ins[i], outs[i].at[me], local_sems.at[i]) for i in range(len(ins))]
    first, passes, recvs = [], {k: [] for k in OTHER_CHIPS}, {k: [] for k in range(1, N_DEV)}
    for i in range(len(ins)):
        def copy(k, src, block, to):
            return pltpu.make_async_remote_copy(
                src_ref=src, dst_ref=outs[i].at[block], send_sem=send_sems.at[i * 7 + k - 1],
                recv_sem=recv_sems.at[i * 7 + k - 1], device_id=to, device_id_type=MESH)

        for k in (1,) + OTHER_CHIPS:
            first.append(copy(k, ins[i], me, _peer(pos, k)))
        for k in OTHER_CHIPS:
            there = _flat(_peer(pos, k))
            passes[k].append(copy(k ^ 1, outs[i].at[there], there, sibling))
        for k in range(1, N_DEV):
            there = _flat(_peer(pos, k))
            recvs[k].append(copy(k, ins[i], there, _peer(pos, k)))
    return local, first, passes, recvs


def _relay_start(ins, outs, *sems):
    local, first, _, _ = _relay_copies(ins, outs, *sems)
    for cp in local + first:
        cp.start()


def _relay_wait(ins, outs, *sems):
    local, first, passes, recvs = _relay_copies(ins, outs, *sems)
    for k in OTHER_CHIPS:
        for cp in recvs[k]:
            cp.wait_recv()
        for cp in passes[k]:
            cp.start()
    for k in (1, 3, 5, 7):
        for cp in recvs[k]:
            cp.wait_recv()
    for cp in first + [cp for k in OTHER_CHIPS for cp in passes[k]]:
        cp.wait_send()
    for cp in local:
        cp.wait()


def _exchange_out_shape(a, gather):
    return jax.ShapeDtypeStruct((N_DEV,) + tuple(a.shape) if gather else tuple(a.shape), a.dtype)


def _exchange_sems(n):
    return [pltpu.SemaphoreType.DMA((7 * n,)), pltpu.SemaphoreType.DMA((7 * n,)), pltpu.SemaphoreType.DMA((n,))]


ANY = pl.BlockSpec(memory_space=pl.ANY)


def _exchange(arrs, gather, name, relay=False):
    n = len(arrs)

    def body(*refs):
        ins, outs = refs[:n], refs[n:2 * n]
        if relay:
            _relay_start(ins, outs, *refs[2 * n:])
            _relay_wait(ins, outs, *refs[2 * n:])
        else:
            _exchange_start(ins, outs, *refs[2 * n:], gather)
            _exchange_wait(ins, outs, *refs[2 * n:], gather)

    return pl.pallas_call(
        body, name=name, out_shape=tuple(_exchange_out_shape(a, gather) for a in arrs),
        in_specs=[ANY] * n, out_specs=tuple([ANY] * n), scratch_shapes=_exchange_sems(n),
    )(*arrs)


def all_gather(arrs, name, relay=False):
    return _exchange(arrs, True, name, relay)


def all_to_all(arrs, name):
    return _exchange(arrs, False, name)


class Riders:
    def __init__(self, arrs, gather):
        self.gather = gather
        self.rows = [a[1] if isinstance(a, tuple) else None for a in arrs]
        self.arrs = [a[0] if isinstance(a, tuple) else a for a in arrs]

    def out_shapes(self):
        shapes = []
        for a, rows in zip(self.arrs, self.rows):
            shape = tuple(a.shape) if rows is None else (a.shape[0], rows[1], a.shape[2])
            shapes.append(jax.ShapeDtypeStruct((N_DEV,) + shape if self.gather else shape, a.dtype))
        return shapes


def _call(body, *, name, grid, in_specs, out_specs, out_shape, semantics, scratch_shapes=(), riders=None):
    if riders is None or not riders.arrs:
        return pl.pallas_call(body, name=name, grid=grid, in_specs=in_specs, out_specs=tuple(out_specs),
                              out_shape=tuple(out_shape), scratch_shapes=list(scratch_shapes),
                              compiler_params=_cparams(*semantics))
    n_in, n_out, n_scr, n_r = len(in_specs), len(out_specs), len(scratch_shapes), len(riders.arrs)
    gather = riders.gather

    def hosted(*refs):
        ins, r_ins = refs[:n_in], refs[n_in:n_in + n_r]
        at = n_in + n_r
        outs, r_outs = refs[at:at + n_out], refs[at + n_out:at + n_out + n_r]
        at += n_out + n_r
        scratch, sems = refs[at:at + n_scr], refs[at + n_scr:]
        first = functools.reduce(jnp.logical_and, [pl.program_id(a) == 0 for a in range(len(grid))])
        last = functools.reduce(jnp.logical_and, [pl.program_id(a) == grid[a] - 1 for a in range(len(grid))])

        r_ins = [(ref, rows) if rows is not None else ref for ref, rows in zip(r_ins, riders.rows)]

        @pl.when(first)
        def _():
            if gather:
                _relay_start(r_ins, r_outs, *sems)
            else:
                _exchange_start(r_ins, r_outs, *sems, gather)

        body(*ins, *outs, *scratch)

        @pl.when(last)
        def _():
            if gather:
                _relay_wait(r_ins, r_outs, *sems)
            else:
                _exchange_wait(r_ins, r_outs, *sems, gather)

    call = pl.pallas_call(
        hosted, name=name, grid=grid, in_specs=list(in_specs) + [ANY] * n_r, out_specs=tuple(out_specs) + (ANY,) * n_r,
        out_shape=tuple(out_shape) + tuple(riders.out_shapes()),
        scratch_shapes=list(scratch_shapes) + _exchange_sems(n_r),
        compiler_params=_cparams(*(("arbitrary",) * len(grid))))
    return lambda *args: call(*args, *riders.arrs)


def _shards_per_step(b):
    _, k, n = b.shape
    best = None
    for q in (1, 2, 4, 8):
        if (q * n) % LANES == 0 and (best is None or 2 * k * q * n <= WEIGHT_BLOCK_BYTES):
            best = q
    return best


def mm_nn(a, b, out_dtype, name, tm=512, riders=None):
    m, k = a.shape
    tm = _tile(m, tm, 8)
    if b.ndim == 3:
        shard = b.shape[2]
        n = N_DEV * shard
        per_step = _shards_per_step(b)
        tn = per_step * shard
        b_spec = pl.BlockSpec((per_step, k, shard), lambda i, j: (j, 0, 0))

        def body(a_ref, b_ref, o_ref):
            for q in range(per_step):
                o_ref[:, q * shard:(q + 1) * shard] = jnp.dot(a_ref[...], b_ref[q],
                                                              preferred_element_type=f32).astype(o_ref.dtype)
    else:
        n = b.shape[1]
        tn = _weight_tile(n, k)
        b_spec = pl.BlockSpec((k, tn), lambda i, j: (0, j))

        def body(a_ref, b_ref, o_ref):
            o_ref[...] = jnp.dot(a_ref[...], b_ref[...], preferred_element_type=f32).astype(o_ref.dtype)

    return _call(
        body, name=name, grid=(m // tm, n // tn),
        in_specs=[pl.BlockSpec((tm, k), lambda i, j: (i, 0)), b_spec],
        out_specs=[pl.BlockSpec((tm, tn), lambda i, j: (i, j))],
        out_shape=[jax.ShapeDtypeStruct((m, n), out_dtype)], semantics=("parallel", "parallel"), riders=riders,
    )(a, b)


def mm_nn_residual(a, b, x, gate, norm, name, tm=512, riders=None):
    m, k = a.shape
    n = b.shape[1]
    tm = _tile(m, tm, 8)

    def body(a_ref, b_ref, x_ref, g_ref, *rest):
        if norm is not None:
            gn_ref, sc_ref, sh_ref, y_ref, o_ref, h_ref = rest
        else:
            y_ref, o_ref = rest
        y = jnp.dot(a_ref[...], b_ref[...], preferred_element_type=f32)
        y_ref[...] = y.astype(bf16)
        x_new = x_ref[...] + g_ref[...] * y
        o_ref[...] = x_new
        if norm is not None:
            h_ref[...] = _norm_fn(x_new, gn_ref[...], sc_ref[...], sh_ref[...]).astype(bf16)

    blk = pl.BlockSpec((tm, n), lambda i: (i, 0))
    in_specs = [pl.BlockSpec((tm, k), lambda i: (i, 0)), pl.BlockSpec((k, n), lambda i: (0, 0)), blk, _row(n)]
    out_specs = [blk, blk]
    out_shape = [jax.ShapeDtypeStruct((m, n), bf16), jax.ShapeDtypeStruct((m, n), f32)]
    args = [a, b, x, gate]
    if norm is not None:
        in_specs += [_row(n)] * 3
        out_specs += [blk]
        out_shape += [jax.ShapeDtypeStruct((m, n), bf16)]
        args += list(norm)
    return _call(body, name=name, grid=(m // tm,), in_specs=in_specs, out_specs=out_specs, out_shape=out_shape,
                 semantics=("parallel",), riders=riders)(*args)


def mm_nt(a, b, out_dtype, name, tm=512, riders=None):
    m = a.shape[0]
    tm = _tile(m, tm, 8)
    if b.ndim == 3:
        _, k, shard = b.shape
        tk = k
        b_spec = pl.BlockSpec(b.shape, lambda i, j: (0, 0, 0))
        width = N_DEV * shard

        def body(a_ref, b_ref, o_ref):
            acc = None
            for q in range(N_DEV):
                part = lax.dot_general(a_ref[:, q * shard:(q + 1) * shard], b_ref[q], (((1,), (1,)), ((), ())),
                                       preferred_element_type=f32)
                acc = part if acc is None else acc + part
            o_ref[...] = acc.astype(o_ref.dtype)
    else:
        k, width = b.shape
        tk = _weight_tile(k, width)
        b_spec = pl.BlockSpec((tk, width), lambda i, j: (j, 0))

        def body(a_ref, b_ref, o_ref):
            o_ref[...] = lax.dot_general(a_ref[...], b_ref[...], (((1,), (1,)), ((), ())),
                                         preferred_element_type=f32).astype(o_ref.dtype)

    return _call(
        body, name=name, grid=(m // tm, k // tk),
        in_specs=[pl.BlockSpec((tm, width), lambda i, j: (i, 0)), b_spec],
        out_specs=[pl.BlockSpec((tm, tk), lambda i, j: (i, j))],
        out_shape=[jax.ShapeDtypeStruct((m, k), out_dtype)], semantics=("parallel", "parallel"), riders=riders,
    )(a, b)


def mm_tn(a, b, out_dtype, name, tm=512, tn=512, riders=None):
    t, m = a.shape
    n = b.shape[1]
    tm, tn = _tile(m, tm), _tile(n, tn)

    def body(a_ref, b_ref, o_ref):
        o_ref[...] = lax.dot_general(a_ref[...], b_ref[...], (((0,), (0,)), ((), ())),
                                     preferred_element_type=f32).astype(o_ref.dtype)

    return _call(
        body, name=name, grid=(m // tm, n // tn),
        in_specs=[pl.BlockSpec((t, tm), lambda i, j: (0, i)), pl.BlockSpec((t, tn), lambda i, j: (0, j))],
        out_specs=[pl.BlockSpec((tm, tn), lambda i, j: (i, j))],
        out_shape=[jax.ShapeDtypeStruct((m, n), out_dtype)], semantics=("parallel", "parallel"), riders=riders,
    )(a, b)


def mm_tn_by_owner(a, b, name, tm=512, riders=None):
    t, m = a.shape
    n = b.shape[1]
    shard = n // N_DEV
    per_step = 1 if shard % LANES == 0 else 2
    assert (per_step * shard) % LANES == 0
    tm = _tile(m, tm)

    def body(a_ref, b_ref, o_ref):
        acc = lax.dot_general(a_ref[...], b_ref[...], (((0,), (0,)), ((), ())), preferred_element_type=f32)
        for q in range(per_step):
            o_ref[q] = acc[:, q * shard:(q + 1) * shard].astype(bf16)

    return _call(
        body, name=name, grid=(m // tm, N_DEV // per_step),
        in_specs=[pl.BlockSpec((t, tm), lambda i, j: (0, i)), pl.BlockSpec((t, per_step * shard), lambda i, j: (0, j))],
        out_specs=[pl.BlockSpec((per_step, tm, shard), lambda i, j: (j, i, 0))],
        out_shape=[jax.ShapeDtypeStruct((N_DEV, m, shard), bf16)], semantics=("parallel", "parallel"), riders=riders,
    )(a, b)


def _norm_fn(x, gn, sc, sh):
    r = lax.rsqrt(jnp.mean(x * x, axis=-1, keepdims=True) + EPS)
    return (x * r * gn) * (1.0 + sc) + sh


def _row(d):
    return pl.BlockSpec((1, d), lambda i: (0, 0))


def norm_fwd(x, gn, sc, sh, name, tm=512):
    t, d = x.shape
    tm = _tile(t, tm, 8)

    def body(x_ref, gn_ref, sc_ref, sh_ref, h_ref):
        h_ref[...] = _norm_fn(x_ref[...], gn_ref[...], sc_ref[...], sh_ref[...]).astype(bf16)

    return pl.pallas_call(
        body, name=name, grid=(t // tm,),
        in_specs=[pl.BlockSpec((tm, d), lambda i: (i, 0)), _row(d), _row(d), _row(d)],
        out_specs=pl.BlockSpec((tm, d), lambda i: (i, 0)),
        out_shape=jax.ShapeDtypeStruct((t, d), bf16), compiler_params=_cparams("parallel"),
    )(x, gn, sc, sh)


def _gate_bwd(dx, y_ref, g_ref, dgate_ref, dy_ref):
    dgate_ref[...] += jnp.sum(dx * y_ref[...].astype(f32), axis=0, keepdims=True)
    dy_ref[...] = (dx * g_ref[...]).astype(bf16)


def norm_bwd(x, gn, sc, sh, dh, dres, name, gate=None, tm=512, riders=None):
    t, d = x.shape
    tm = _tile(t, tm, 8)

    def body(x_ref, gn_ref, sc_ref, sh_ref, dh_ref, dres_ref, *rest):
        if gate is not None:
            y_ref, g_ref, dx_ref, dgn_ref, dsc_ref, dsh_ref, dgate_ref, dy_ref = rest
        else:
            dx_ref, dgn_ref, dsc_ref, dsh_ref = rest

        @pl.when(pl.program_id(0) == 0)
        def _():
            dgn_ref[...] = jnp.zeros_like(dgn_ref)
            dsc_ref[...] = jnp.zeros_like(dsc_ref)
            dsh_ref[...] = jnp.zeros_like(dsh_ref)
            if gate is not None:
                dgate_ref[...] = jnp.zeros_like(dgate_ref)

        _, vjp = jax.vjp(_norm_fn, x_ref[...], gn_ref[...], sc_ref[...], sh_ref[...])
        dx, dgn, dsc, dsh = vjp(dh_ref[...].astype(f32))
        dx = dx + dres_ref[...]
        dx_ref[...] = dx
        dgn_ref[...] += dgn
        dsc_ref[...] += dsc
        dsh_ref[...] += dsh
        if gate is not None:
            _gate_bwd(dx, y_ref, g_ref, dgate_ref, dy_ref)

    blk = pl.BlockSpec((tm, d), lambda i: (i, 0))
    vec = jax.ShapeDtypeStruct((1, d), f32)
    in_specs = [blk, _row(d), _row(d), _row(d), blk, blk]
    out_specs = [blk, _row(d), _row(d), _row(d)]
    out_shape = [jax.ShapeDtypeStruct((t, d), f32), vec, vec, vec]
    args = [x, gn, sc, sh, dh, dres]
    if gate is not None:
        in_specs += [blk, _row(d)]
        out_specs += [_row(d), blk]
        out_shape += [vec, jax.ShapeDtypeStruct((t, d), bf16)]
        args += list(gate)
    return _call(body, name=name, grid=(t // tm,), in_specs=in_specs, out_specs=out_specs, out_shape=out_shape,
                 semantics=("arbitrary",), riders=riders)(*args)


def _loss_fn(x, g, tgt):
    r = lax.rsqrt(jnp.mean(x * x, axis=-1, keepdims=True) + EPS)
    err = jnp.square(x * r * g - tgt)
    return 0.5 * jnp.sum(jnp.mean(err, axis=-1, keepdims=True), axis=0, keepdims=True)


def loss_head(x, g, tgt, y, gate, name, tm=512):
    t, d = x.shape
    tm = _tile(t, tm, 8)

    def body(x_ref, g_ref, t_ref, y_ref, gate_ref, loss_ref, dx_ref, dg_ref, dgate_ref, dy_ref):
        @pl.when(pl.program_id(0) == 0)
        def _():
            loss_ref[...] = jnp.zeros_like(loss_ref)
            dg_ref[...] = jnp.zeros_like(dg_ref)
            dgate_ref[...] = jnp.zeros_like(dgate_ref)

        loss, vjp = jax.vjp(_loss_fn, x_ref[...], g_ref[...], t_ref[...])
        dx, dg, _ = vjp(jnp.ones((1, 1), f32))
        dx_ref[...] = dx
        loss_ref[...] += loss
        dg_ref[...] += dg
        _gate_bwd(dx, y_ref, gate_ref, dgate_ref, dy_ref)

    blk = pl.BlockSpec((tm, d), lambda i: (i, 0))
    vec = jax.ShapeDtypeStruct((1, d), f32)
    return pl.pallas_call(
        body, name=name, grid=(t // tm,),
        in_specs=[blk, _row(d), blk, blk, _row(d)],
        out_specs=(pl.BlockSpec((1, 1), lambda i: (0, 0)), blk, _row(d), _row(d), blk),
        out_shape=(jax.ShapeDtypeStruct((1, 1), f32), jax.ShapeDtypeStruct((t, d), f32), vec, vec,
                   jax.ShapeDtypeStruct((t, d), bf16)),
        compiler_params=_cparams("arbitrary"),
    )(x, g, tgt, y, gate)


def _gm_block_fn(z, ws, bs, lng, lnb):
    u = _gelu(z[:, :GM_WIDTH])
    vg = _gelu(z[:, GM_WIDTH:])
    mu = jnp.mean(vg, axis=-1, keepdims=True)
    var = jnp.mean(jnp.square(vg - mu), axis=-1, keepdims=True)
    vn = (vg - mu) * lax.rsqrt(var + EPS) * lng + lnb
    row = lax.broadcasted_iota(jnp.int32, (GM_BLOCK, GM_BLOCK), 0) // CHUNK
    col = lax.broadcasted_iota(jnp.int32, (GM_BLOCK, GM_BLOCK), 1) // CHUNK
    parts = []
    for h in range(GM_HEADS):
        w = jnp.where(row >= col, ws[h], 0.0)
        cols = slice(h * GM_HEAD_DIM, (h + 1) * GM_HEAD_DIM)
        s = jnp.dot(w.astype(bf16), vn[:, cols].astype(bf16), preferred_element_type=f32) + bs[h]
        parts.append(u[:, cols] * s)
    return jnp.concatenate(parts, axis=1)


def _gm_param_specs():
    return [pl.BlockSpec((GM_HEADS, GM_BLOCK, GM_BLOCK), lambda i: (0, 0, 0)),
            pl.BlockSpec((GM_HEADS, GM_BLOCK, 1), lambda i: (0, 0, 0)), _row(GM_WIDTH), _row(GM_WIDTH)]


def gm_mix_fwd(z, ws, bs, lng, lnb, name, riders=None):
    t = z.shape[0]

    def body(z_ref, ws_ref, bs_ref, lng_ref, lnb_ref, o_ref):
        o_ref[...] = _gm_block_fn(z_ref[...].astype(f32), ws_ref[...], bs_ref[...], lng_ref[...],
                                  lnb_ref[...]).astype(bf16)

    return _call(
        body, name=name, grid=(t // GM_BLOCK,),
        in_specs=[pl.BlockSpec((GM_BLOCK, 2 * GM_WIDTH), lambda i: (i, 0))] + _gm_param_specs(),
        out_specs=[pl.BlockSpec((GM_BLOCK, GM_WIDTH), lambda i: (i, 0))],
        out_shape=[jax.ShapeDtypeStruct((t, GM_WIDTH), bf16)], semantics=("parallel",), riders=riders,
    )(z, ws, bs, lng, lnb)


def gm_mix_bwd(z, ws, bs, lng, lnb, dgated, name, riders=None):
    t = z.shape[0]

    def body(z_ref, ws_ref, bs_ref, lng_ref, lnb_ref, dg_ref, dz_ref, dws_ref, dbs_ref, dlng_ref, dlnb_ref):
        _, vjp = jax.vjp(_gm_block_fn, z_ref[...].astype(f32), ws_ref[...], bs_ref[...], lng_ref[...], lnb_ref[...])
        dz, dws, dbs, dlng, dlnb = vjp(dg_ref[...].astype(f32))
        dz_ref[...] = dz.astype(bf16)

        @pl.when(pl.program_id(0) == 0)
        def _():
            dws_ref[...] = jnp.zeros_like(dws_ref)
            dbs_ref[...] = jnp.zeros_like(dbs_ref)
            dlng_ref[...] = jnp.zeros_like(dlng_ref)
            dlnb_ref[...] = jnp.zeros_like(dlnb_ref)

        dws_ref[...] += dws
        dbs_ref[...] += dbs
        dlng_ref[...] += dlng
        dlnb_ref[...] += dlnb

    zblk = pl.BlockSpec((GM_BLOCK, 2 * GM_WIDTH), lambda i: (i, 0))
    return _call(
        body, name=name, grid=(t // GM_BLOCK,),
        in_specs=[zblk] + _gm_param_specs() + [pl.BlockSpec((GM_BLOCK, GM_WIDTH), lambda i: (i, 0))],
        out_specs=[zblk] + _gm_param_specs(),
        out_shape=[jax.ShapeDtypeStruct((t, 2 * GM_WIDTH), bf16),
                   jax.ShapeDtypeStruct((GM_HEADS, GM_BLOCK, GM_BLOCK), f32),
                   jax.ShapeDtypeStruct((GM_HEADS, GM_BLOCK, 1), f32),
                   jax.ShapeDtypeStruct((1, GM_WIDTH), f32), jax.ShapeDtypeStruct((1, GM_WIDTH), f32)],
        semantics=("arbitrary",), riders=riders,
    )(z, ws, bs, lng, lnb, dgated)


@functools.partial(jax.custom_vjp, nondiff_argnums=(1,))
def _rows_up(x, shift):
    return x if shift == 0 else pltpu.roll(x, x.shape[1] - shift, axis=1)


def _rows_up_fwd(x, shift):
    return _rows_up(x, shift), None


def _rows_up_bwd(shift, _, g):
    return (g if shift == 0 else pltpu.roll(g, shift, axis=1),)


_rows_up.defvjp(_rows_up_fwd, _rows_up_bwd)


def _hg_block_fn(qp, fz, iv, gp, s0, lb, gn):
    n, ns, d = HG_SUB, HG_TOKENS // HG_SUB, HG_DIM
    p, nb, per_sub = HG_PAIR, HG_TOKENS // HG_PAIR, HG_SUB // HG_PAIR
    f = lb + (1.0 - lb) * jax.nn.sigmoid(fz)
    g = jnp.log(f)
    k = 1.0 - f
    q = qp * jax.nn.sigmoid(qp)
    v = iv.astype(bf16)
    row = lax.broadcasted_iota(jnp.int32, (HG_TOKENS, HG_TOKENS), 0)
    col = lax.broadcasted_iota(jnp.int32, (HG_TOKENS, HG_TOKENS), 1)
    same_sub = col // n == row // n
    tri = ((col <= row) & same_sub).astype(f32)
    cum = jnp.dot(tri, g, precision=lax.Precision.HIGHEST, preferred_element_type=f32)
    cum_b, q_b, k_b, f_b = cum.reshape(nb, p, d), q.reshape(nb, p, d), k.reshape(nb, p, d), f.reshape(nb, p, d)
    j_b = lax.broadcasted_iota(jnp.int32, (nb, p, d), 1)
    j_col = lax.broadcasted_iota(jnp.int32, (nb, p, 1), 1)
    scores_t = jnp.zeros((HG_TOKENS, HG_TOKENS), f32)
    weight = k_b
    for delta in range(p):
        if delta:
            weight = weight * _rows_up(f_b, delta)
        pair = jnp.sum(_rows_up(q_b, delta) * weight, axis=2, keepdims=True)
        pair = jnp.where(j_col < p - delta, pair, 0.0)
        scores_t = scores_t + jnp.where(col == row + delta, pair.reshape(HG_TOKENS, 1), 0.0)
    o = lax.dot_general(scores_t.astype(bf16), v, (((0,), (0,)), ((), ())), preferred_element_type=f32)
    last = cum_b[:, p - 1:p, :]
    before = jnp.concatenate([jnp.zeros((1, 1, d), f32), last[:-1]], axis=0)
    before = jnp.broadcast_to(before, (nb, p, d)).reshape(HG_TOKENS, d)
    block = (lax.broadcasted_iota(jnp.int32, (HG_TOKENS, d), 0) // p) % per_sub
    q_late = q * jnp.exp(jnp.where(block > 0, cum - before, -1e30))
    last_s = last.reshape(ns, per_sub, d)
    q_parts, k_parts = [], []
    for m in range(1, per_sub):
        split = jnp.broadcast_to(last_s[:, m - 1:m, :], (ns, n, d)).reshape(HG_TOKENS, d)
        k_parts.append(k * jnp.exp(jnp.where(block < m, split - cum, -1e30)))
        q_parts.append(jnp.where(block == m, q_late, 0.0))
    scores = lax.dot_general(jnp.concatenate(q_parts, axis=1).astype(bf16), jnp.concatenate(k_parts, axis=1).astype(bf16),
                             (((1,), (1,)), ((), ())), preferred_element_type=f32)
    o = o + jnp.dot(jnp.where(same_sub, scores, 0.0).astype(bf16), v, preferred_element_type=f32)
    cum_s = cum.reshape(ns, n, d)
    tot = cum_s[:, n - 1:n, :]
    kt_t = (k.reshape(ns, n, d) * jnp.exp(tot - cum_s)).reshape(HG_TOKENS, d).T
    lane_sub = lax.broadcasted_iota(jnp.int32, (d, HG_TOKENS), 1) // n
    k_by_sub = jnp.concatenate([jnp.where(lane_sub == b, kt_t, 0.0) for b in range(ns)], axis=0).astype(bf16)
    update = jnp.dot(k_by_sub, v, preferred_element_type=f32)
    decay = jnp.exp(tot.reshape(ns, d)).T
    state = s0
    states = []
    for a in range(ns):
        states.append(state.astype(bf16))
        state = decay[:, a:a + 1] * state + update[a * d:(a + 1) * d]
    qt = q * jnp.exp(cum)
    row_sub = lax.broadcasted_iota(jnp.int32, (HG_TOKENS, d), 0) // n
    q_by_sub = jnp.concatenate([jnp.where(row_sub == a, qt, 0.0) for a in range(ns)], axis=1).astype(bf16)
    o = o + jnp.dot(q_by_sub, jnp.concatenate(states, axis=0), preferred_element_type=f32)
    on = o * lax.rsqrt(jnp.mean(o * o, axis=-1, keepdims=True) + EPS) * gn
    return on * (gp * jax.nn.sigmoid(gp)), state


def _head_parts(ref, h):
    return [ref[:, p * D + h * HG_DIM:p * D + (h + 1) * HG_DIM] for p in range(4)]


def hg_scan_fwd(proj, lb, gn, name, riders=None):
    t = proj.shape[0]
    nt = t // HG_TOKENS

    def body(p_ref, lb_ref, gn_ref, y_ref, s_ref, state):
        @pl.when(pl.program_id(0) == 0)
        def _():
            state[...] = jnp.zeros_like(state)

        for h in range(HG_HEADS):
            cols = slice(h * HG_DIM, (h + 1) * HG_DIM)
            s_ref[h, 0] = state[h]
            y, s1 = _hg_block_fn(*_head_parts(p_ref, h), state[h], lb_ref[:, cols], gn_ref[:, cols])
            y_ref[:, cols] = y.astype(bf16)
            state[h] = s1

    return _call(
        body, name=name, grid=(nt,),
        in_specs=[pl.BlockSpec((HG_TOKENS, 4 * D), lambda i: (i, 0)), _row(D), _row(D)],
        out_specs=[pl.BlockSpec((HG_TOKENS, D), lambda i: (i, 0)),
                   pl.BlockSpec((HG_HEADS, 1, HG_DIM, HG_DIM), lambda i: (0, i, 0, 0))],
        out_shape=[jax.ShapeDtypeStruct((t, D), bf16), jax.ShapeDtypeStruct((HG_HEADS, nt, HG_DIM, HG_DIM), f32)],
        scratch_shapes=[pltpu.VMEM((HG_HEADS, HG_DIM, HG_DIM), f32)],
        semantics=("arbitrary",), riders=riders,
    )(proj, lb, gn)


def hg_scan_bwd(proj, lb, gn, states, dy, name, riders=None):
    t = proj.shape[0]
    nt = t // HG_TOKENS

    def body(p_ref, lb_ref, gn_ref, s_ref, dy_ref, dp_ref, dlb_ref, dgn_ref, dstate):
        @pl.when(pl.program_id(0) == 0)
        def _():
            dstate[...] = jnp.zeros_like(dstate)
            dlb_ref[...] = jnp.zeros_like(dlb_ref)
            dgn_ref[...] = jnp.zeros_like(dgn_ref)

        for h in range(HG_HEADS):
            cols = slice(h * HG_DIM, (h + 1) * HG_DIM)
            _, vjp = jax.vjp(_hg_block_fn, *_head_parts(p_ref, h), s_ref[h, 0], lb_ref[:, cols], gn_ref[:, cols])
            grads = vjp((dy_ref[:, cols].astype(f32), dstate[h]))
            for p in range(4):
                dp_ref[:, p * D + h * HG_DIM:p * D + (h + 1) * HG_DIM] = grads[p].astype(bf16)
            dstate[h] = grads[4]
            dlb_ref[:, cols] += grads[5]
            dgn_ref[:, cols] += grads[6]

    small = jax.ShapeDtypeStruct((1, D), f32)
    return _call(
        body, name=name, grid=(nt,),
        in_specs=[pl.BlockSpec((HG_TOKENS, 4 * D), lambda i: (nt - 1 - i, 0)), _row(D), _row(D),
                  pl.BlockSpec((HG_HEADS, 1, HG_DIM, HG_DIM), lambda i: (0, nt - 1 - i, 0, 0)),
                  pl.BlockSpec((HG_TOKENS, D), lambda i: (nt - 1 - i, 0))],
        out_specs=[pl.BlockSpec((HG_TOKENS, 4 * D), lambda i: (nt - 1 - i, 0)), _row(D), _row(D)],
        out_shape=[jax.ShapeDtypeStruct((t, 4 * D), bf16), small, small],
        scratch_shapes=[pltpu.VMEM((HG_HEADS, HG_DIM, HG_DIM), f32)],
        semantics=("arbitrary",), riders=riders,
    )(proj, lb, gn, states, dy)


FFN_COLS = 1408
HALO = 8
STRIP = 16


def _ffn_specs(tm):
    nb = tm // HALO
    main_g = pl.BlockSpec((tm, FFN_COLS), lambda j, i: (i, j))
    main_v = pl.BlockSpec((tm, FFN_COLS), lambda j, i: (i, j + 2))
    halo_g = pl.BlockSpec((HALO, FFN_COLS), lambda j, i: (jnp.maximum(i * nb - 1, 0), j))
    halo_v = pl.BlockSpec((HALO, FFN_COLS), lambda j, i: (jnp.maximum(i * nb - 1, 0), j + 2))
    w_g = pl.BlockSpec((3, FFN_COLS), lambda j, i: (0, j))
    w_v = pl.BlockSpec((3, FFN_COLS), lambda j, i: (0, j + 2))
    b_g = pl.BlockSpec((1, FFN_COLS), lambda j, i: (0, j))
    b_v = pl.BlockSpec((1, FFN_COLS), lambda j, i: (0, j + 2))
    return [main_g, halo_g, main_v, halo_v, w_g, w_v, b_g, b_v]


def _strip_rows(r):
    return pl.ds(r * STRIP, STRIP) if isinstance(r, int) else pl.ds(pl.multiple_of(r * STRIP, STRIP), STRIP)


def _for_strips(nstrip, strip, reverse=False):
    if reverse:
        strip(nstrip - 1, True)
        lax.fori_loop(0, nstrip - 1, lambda k, c: (strip(nstrip - 2 - k, False), c)[1], 0)
    else:
        strip(0, True)
        lax.fori_loop(1, nstrip, lambda r, c: (strip(r, False), c)[1], 0)


def _conv_strip(main_ref, halo_ref, w_ref, b_ref, r, edge, cols, rowi):
    cur = main_ref[_strip_rows(r), cols].astype(f32)
    if edge:
        h = jnp.where(pl.program_id(1) == 0, 0.0, halo_ref[:, cols].astype(f32))
        prev = jnp.concatenate([jnp.zeros_like(h), h], axis=0)
    else:
        prev = main_ref[_strip_rows(r - 1), cols].astype(f32)
    a1 = jnp.where(rowi < 1, pltpu.roll(prev, 1, axis=0), pltpu.roll(cur, 1, axis=0))
    a2 = jnp.where(rowi < 2, pltpu.roll(prev, 2, axis=0), pltpu.roll(cur, 2, axis=0))
    y = b_ref[:, cols] + w_ref[0:1, cols] * a2 + w_ref[1:2, cols] * a1 + w_ref[2:3, cols] * cur
    return y, (cur, a1, a2)


def ffn_gate_fwd(a, cw, cb, name, tm=512, riders=None):
    t = a.shape[0]
    tm = _tile(t, tm, STRIP)

    def body(ag_ref, hg_ref, av_ref, hv_ref, wg_ref, wv_ref, bg_ref, bv_ref, o_ref):
        rowi = lax.broadcasted_iota(jnp.int32, (STRIP, LANES), 0)

        def strip(r, edge):
            for c in range(FFN_COLS // LANES):
                cols = pl.ds(c * LANES, LANES)
                yg, _ = _conv_strip(ag_ref, hg_ref, wg_ref, bg_ref, r, edge, cols, rowi)
                yv, _ = _conv_strip(av_ref, hv_ref, wv_ref, bv_ref, r, edge, cols, rowi)
                o_ref[_strip_rows(r), cols] = (_gelu(yg) * yv).astype(bf16)

        _for_strips(tm // STRIP, strip)

    return _call(
        body, name=name, grid=(2, t // tm), in_specs=_ffn_specs(tm),
        out_specs=[pl.BlockSpec((tm, FFN_COLS), lambda j, i: (i, j))],
        out_shape=[jax.ShapeDtypeStruct((t, FFN_HIDDEN), bf16)],
        semantics=("parallel", "arbitrary"), riders=riders,
    )(a, a, a, a, cw, cw, cb, cb)


def ffn_gate_bwd(a, cw, cb, dhid, name, tm=512, riders=None):
    t = a.shape[0]
    tm = _tile(t, tm, STRIP)

    def body(ag_ref, hg_ref, av_ref, hv_ref, wg_ref, wv_ref, bg_ref, bv_ref, dh_ref,
             dy_ref, dwg_ref, dwv_ref, dbg_ref, dbv_ref, acc):
        rowi = lax.broadcasted_iota(jnp.int32, (STRIP, LANES), 0)

        @pl.when(pl.program_id(1) == 0)
        def _():
            acc[...] = jnp.zeros_like(acc)

        def strip(r, edge):
            rows = _strip_rows(r)
            for c in range(FFN_COLS // LANES):
                cols = pl.ds(c * LANES, LANES)
                yg, taps_g = _conv_strip(ag_ref, hg_ref, wg_ref, bg_ref, r, edge, cols, rowi)
                yv, taps_v = _conv_strip(av_ref, hv_ref, wv_ref, bv_ref, r, edge, cols, rowi)
                dh = dh_ref[rows, cols].astype(f32)
                cdf = 0.5 * (1.0 + lax.erf(yg * (1.0 / math.sqrt(2.0))))
                pdf = jnp.exp(-0.5 * yg * yg) * (1.0 / math.sqrt(2.0 * math.pi))
                dyg = dh * yv * (cdf + yg * pdf)
                dyv = dh * (yg * cdf)
                dy_ref[0, rows, cols] = dyg.astype(bf16)
                dy_ref[1, rows, cols] = dyv.astype(bf16)
                for p, (dy, (a0, a1, a2)) in enumerate(((dyg, taps_g), (dyv, taps_v))):
                    acc[4 * p + 0, :, cols] += dy * a2
                    acc[4 * p + 1, :, cols] += dy * a1
                    acc[4 * p + 2, :, cols] += dy * a0
                    acc[4 * p + 3, :, cols] += dy

        _for_strips(tm // STRIP, strip)

        @pl.when(pl.program_id(1) == pl.num_programs(1) - 1)
        def _():
            for p, (dw_ref, db_ref) in enumerate(((dwg_ref, dbg_ref), (dwv_ref, dbv_ref))):
                for tap in range(3):
                    dw_ref[tap:tap + 1, :] = jnp.sum(acc[4 * p + tap], axis=0, keepdims=True)
                db_ref[...] = jnp.sum(acc[4 * p + 3], axis=0, keepdims=True)

    half_w = pl.BlockSpec((3, FFN_COLS), lambda j, i: (0, j))
    half_b = pl.BlockSpec((1, FFN_COLS), lambda j, i: (0, j))
    return _call(
        body, name=name, grid=(2, t // tm),
        in_specs=_ffn_specs(tm) + [pl.BlockSpec((tm, FFN_COLS), lambda j, i: (i, j))],
        out_specs=[pl.BlockSpec((2, tm, FFN_COLS), lambda j, i: (0, i, j)), half_w, half_w, half_b, half_b],
        out_shape=[jax.ShapeDtypeStruct((2, t, FFN_HIDDEN), bf16),
                   jax.ShapeDtypeStruct((3, FFN_HIDDEN), f32), jax.ShapeDtypeStruct((3, FFN_HIDDEN), f32),
                   jax.ShapeDtypeStruct((1, FFN_HIDDEN), f32), jax.ShapeDtypeStruct((1, FFN_HIDDEN), f32)],
        scratch_shapes=[pltpu.VMEM((8, STRIP, FFN_COLS), f32)],
        semantics=("parallel", "arbitrary"), riders=riders,
    )(a, a, a, a, cw, cw, cb, cb, dhid)


def conv_transpose(dy, cw, name, tm=512):
    _, t, fh = dy.shape
    tm = _tile(t, tm, STRIP)
    nb = tm // HALO
    last_halo = t // HALO - 1
    ncol = fh // FFN_COLS

    def body(main_ref, halo_ref, w_ref, o_ref):
        rowi = lax.broadcasted_iota(jnp.int32, (STRIP, LANES), 0)
        last_block = pl.program_id(2) == pl.num_programs(2) - 1

        def strip(r, edge):
            rows = _strip_rows(r)
            for c in range(FFN_COLS // LANES):
                cols = pl.ds(c * LANES, LANES)
                cur = main_ref[0, rows, cols].astype(f32)
                if edge:
                    h = jnp.where(last_block, 0.0, halo_ref[0, :, cols].astype(f32))
                    nxt = jnp.concatenate([h, jnp.zeros_like(h)], axis=0)
                else:
                    nxt = main_ref[0, _strip_rows(r + 1), cols].astype(f32)
                d1 = jnp.where(rowi >= STRIP - 1, pltpu.roll(nxt, STRIP - 1, axis=0), pltpu.roll(cur, STRIP - 1, axis=0))
                d2 = jnp.where(rowi >= STRIP - 2, pltpu.roll(nxt, STRIP - 2, axis=0), pltpu.roll(cur, STRIP - 2, axis=0))
                o_ref[rows, cols] = (w_ref[2:3, cols] * cur + w_ref[1:2, cols] * d1 + w_ref[0:1, cols] * d2).astype(bf16)

        _for_strips(tm // STRIP, strip, reverse=True)

    return pl.pallas_call(
        body, name=name, grid=(2, ncol, t // tm),
        in_specs=[pl.BlockSpec((1, tm, FFN_COLS), lambda p, j, i: (p, i, j)),
                  pl.BlockSpec((1, HALO, FFN_COLS), lambda p, j, i: (p, jnp.minimum((i + 1) * nb, last_halo), j)),
                  pl.BlockSpec((3, FFN_COLS), lambda p, j, i: (0, p * ncol + j))],
        out_specs=pl.BlockSpec((tm, FFN_COLS), lambda p, j, i: (i, p * ncol + j)),
        out_shape=jax.ShapeDtypeStruct((t, 2 * fh), bf16),
        compiler_params=_cparams("parallel", "parallel", "arbitrary"),
    )(dy, dy, cw)


def ada_mod(c_all, ada_w, ada_b_cols, name):
    cols = ada_w.shape[2]

    def body(c_ref, w_ref, b_ref, o_ref):
        c = c_ref[...]
        cond = (c * jax.nn.sigmoid(c)).astype(bf16)
        o_ref[0] = jnp.dot(cond, w_ref[0].astype(bf16), preferred_element_type=f32) + b_ref[0]

    return pl.pallas_call(
        body, name=name, grid=(DEPTH,),
        in_specs=[pl.BlockSpec((N_DEV, D), lambda i: (0, 0)), pl.BlockSpec((1, D, cols), lambda i: (i, 0, 0)),
                  pl.BlockSpec((1, 1, cols), lambda i: (i, 0, 0))],
        out_specs=pl.BlockSpec((1, N_DEV, cols), lambda i: (i, 0, 0)),
        out_shape=jax.ShapeDtypeStruct((DEPTH, N_DEV, cols), f32), compiler_params=_cparams("parallel"),
    )(c_all, ada_w, ada_b_cols)


def ada_grads(c_all, dmod_cols, dmod_all, name):
    cols = dmod_cols.shape[2]

    def body(c_ref, dm_ref, da_ref, dw_ref, db_ref):
        c = c_ref[...]
        cond = c * jax.nn.sigmoid(c)
        dw_ref[0] = lax.dot_general(cond, dm_ref[0], (((0,), (0,)), ((), ())), precision=lax.Precision.HIGHEST,
                                    preferred_element_type=f32)
        acc = da_ref[0, 0]
        for e in range(1, N_DEV):
            acc = acc + da_ref[e, 0]
        db_ref[0] = acc

    return pl.pallas_call(
        body, name=name, grid=(DEPTH,),
        in_specs=[pl.BlockSpec((N_DEV, D), lambda i: (0, 0)), pl.BlockSpec((1, N_DEV, cols), lambda i: (i, 0, 0)),
                  pl.BlockSpec((N_DEV, 1, 1, 6 * D), lambda i: (0, i, 0, 0))],
        out_specs=(pl.BlockSpec((1, D, cols), lambda i: (i, 0, 0)), pl.BlockSpec((1, 1, 6 * D), lambda i: (i, 0, 0))),
        out_shape=(jax.ShapeDtypeStruct((DEPTH, D, cols), f32), jax.ShapeDtypeStruct((DEPTH, 1, 6 * D), f32)),
        compiler_params=_cparams("parallel"),
    )(c_all, dmod_cols, dmod_all)


def lower_bound_fwd(hg_lb, name):
    n = hg_lb.shape[1]

    def body(l_ref, o_ref):
        o_ref[...] = jax.nn.sigmoid(l_ref[1:2, :] - l_ref[0:1, :])

    return pl.pallas_call(body, name=name, out_shape=jax.ShapeDtypeStruct((1, n), f32))(hg_lb)


def lower_bound_bwd(hg_lb, dlb, name):
    n = hg_lb.shape[1]

    def body(l_ref, d_ref, o_ref):
        p = jax.nn.sigmoid(l_ref[1:2, :] - l_ref[0:1, :])
        g = d_ref[...] * p * (1.0 - p)
        o_ref[0:1, :] = -g
        o_ref[1:2, :] = g

    return pl.pallas_call(body, name=name, out_shape=jax.ShapeDtypeStruct((2, n), f32))(hg_lb, dlb)


def _adamw(w, g, m, v):
    m = ADAM_B1 * m + (1.0 - ADAM_B1) * g
    v = ADAM_B2 * v + (1.0 - ADAM_B2) * jnp.square(g)
    m_hat = m / (1.0 - ADAM_B1 ** ADAM_STEP)
    v_hat = v / (1.0 - ADAM_B2 ** ADAM_STEP)
    delta = -ADAM_LR * (m_hat / (jnp.sqrt(v_hat) + ADAM_EPS) + ADAM_WD * w)
    return delta, m, v


ADAM_BLOCK_BYTES = 24 * 1024 * 1024


def adam_reduced(parts, w, m, v, name, riders=None):
    layers, r, c = w.shape
    flat = [p for layer_parts in parts for p in layer_parts]
    rows = flat[0].shape[1]
    assert all(p.shape == (N_DEV, rows, c) for p in flat) and rows * len(flat) == layers * r
    per_layer = r // rows
    row_bytes = 2 * (len(flat) * N_DEV * c * 2 + 7 * c * 4)
    tr = _tile(rows, max(16, ADAM_BLOCK_BYTES // row_bytes), 16)
    steps = rows // tr

    def body(*refs):
        p_refs = refs[:len(flat)]
        w_ref, m_ref, v_ref, g_ref, d_ref, mo_ref, vo_ref = refs[len(flat):]
        for idx in range(len(flat)):
            @pl.when(pl.program_id(0) == idx)
            def _():
                g = p_refs[idx][0].astype(f32)
                for j in range(1, N_DEV):
                    g = g + p_refs[idx][j].astype(f32)
                g_ref[...] = g
                d_ref[...], mo_ref[...], vo_ref[...] = _adamw(w_ref[...], g, m_ref[...], v_ref[...])

    def part_spec(idx):
        return pl.BlockSpec((N_DEV, tr, c), lambda p, i: (0, jnp.where(p == idx, i, 0), 0))

    blk = pl.BlockSpec((None, tr, c), lambda p, i: (p // per_layer, (p % per_layer) * steps + i, 0))
    out = jax.ShapeDtypeStruct((layers, r, c), f32)
    return _call(
        body, name=name, grid=(len(flat), steps),
        in_specs=[part_spec(idx) for idx in range(len(flat))] + [blk, blk, blk],
        out_specs=[blk, blk, blk, blk], out_shape=[out, out, out, out], semantics=("arbitrary", "arbitrary"),
        riders=riders,
    )(*flat, w, m, v)


def adam_plain(g, w, m, v, name, tr=256):
    r, c = w.shape
    tr = _tile(r, tr, 8)

    def body(g_ref, w_ref, m_ref, v_ref, d_ref, mo_ref, vo_ref):
        d_ref[...], mo_ref[...], vo_ref[...] = _adamw(w_ref[...], g_ref[...], m_ref[...], v_ref[...])

    blk = pl.BlockSpec((tr, c), lambda i: (i, 0))
    out = jax.ShapeDtypeStruct((r, c), f32)
    return pl.pallas_call(
        body, name=name, grid=(r // tr,), in_specs=[blk, blk, blk, blk], out_specs=(blk, blk, blk),
        out_shape=(out, out, out), compiler_params=_cparams("parallel"),
    )(g, w, m, v)


def sum_parts(parts, name):
    _, r, c = parts.shape

    def body(p_ref, o_ref):
        acc = p_ref[0]
        for j in range(1, N_DEV):
            acc = acc + p_ref[j]
        o_ref[...] = acc

    return pl.pallas_call(body, name=name, out_shape=jax.ShapeDtypeStruct((r, c), f32))(parts)


def _pack(arrs, rows_mult=8):
    flat = jnp.concatenate([a.reshape(-1) for a in arrs])
    rows = -(-flat.shape[0] // LANES)
    rows = -(-rows // rows_mult) * rows_mult
    return jnp.pad(flat, (0, rows * LANES - flat.shape[0])).reshape(rows, LANES)


def _unpack(flat, shapes):
    out, at = [], 0
    for s in shapes:
        n = math.prod(s)
        out.append(flat[at:at + n].reshape(s))
        at += n
    return out


def kernel(x, c, gm_w_in, gm_ln_g, gm_ln_b, gm_w_s, gm_b_s, gm_w_out, hg_w_in, hg_lb, hg_gn_g, hg_w_out, ffn_w_up, ffn_conv_w, ffn_conv_b, ffn_w_down, norm_g, ada_w, ada_b, final_g, loss_target, m_gm_w_in, m_gm_ln_g, m_gm_ln_b, m_gm_w_s, m_gm_b_s, m_gm_w_out, m_hg_w_in, m_hg_lb, m_hg_gn_g, m_hg_w_out, m_ffn_w_up, m_ffn_conv_w, m_ffn_conv_b, m_ffn_w_down, m_norm_g, m_ada_w, m_ada_b, m_final_g, v_gm_w_in, v_gm_ln_g, v_gm_ln_b, v_gm_w_s, v_gm_b_s, v_gm_w_out, v_hg_w_in, v_hg_lb, v_hg_gn_g, v_hg_w_out, v_ffn_w_up, v_ffn_conv_w, v_ffn_conv_b, v_ffn_w_down, v_norm_g, v_ada_w, v_ada_b, v_final_g):
    me = _flat(_mesh_pos())
    xt = x[0]
    t = xt.shape[0]

    small_shapes = [(1, D), (2, HG_DIM), (2, HG_DIM), (DEPTH, 2, HG_DIM), (DEPTH, 3, 2 * FFN_HIDDEN // N_DEV)]
    (small_all,) = all_gather([_pack([c, hg_lb, hg_gn_g, norm_g, ffn_conv_w])], "gather_small")
    small_all = small_all.reshape(N_DEV, -1)
    at = 0
    pieces = []
    for s in small_shapes:
        n = math.prod(s)
        pieces.append(small_all[:, at:at + n].reshape((N_DEV,) + s))
        at += n
    c_all = pieces[0].reshape(N_DEV, D)
    hg_lb_full = jnp.transpose(pieces[1], (1, 0, 2)).reshape(2, D)
    hg_gn_full = jnp.transpose(pieces[2], (1, 0, 2)).reshape(2, D)
    norm_g_full = jnp.transpose(pieces[3], (1, 2, 0, 3)).reshape(DEPTH, 2, D)
    conv_w_full = jnp.transpose(pieces[4], (1, 2, 0, 3)).reshape(DEPTH, 3, 2 * FFN_HIDDEN)

    lb1 = lower_bound_fwd(hg_lb_full, "lower_bound")
    lbs = [jnp.zeros((1, D), f32), lb1]

    ada_b_cols = lax.dynamic_slice(ada_b, (0, me * ADA_COLS), (DEPTH, ADA_COLS)).reshape(DEPTH, 1, ADA_COLS)
    mod_cols = ada_mod(c_all, ada_w, ada_b_cols, "ada_mod")
    (mod_mine,) = all_to_all([jnp.transpose(mod_cols, (1, 0, 2))], "mod_to_examples")
    mod = jnp.transpose(mod_mine, (1, 0, 2)).reshape(DEPTH, 6, 1, D)

    def layer_shards(i):
        j = i // 2
        w_in, w_out = (gm_w_in, gm_w_out) if i % 2 == 0 else (hg_w_in, hg_w_out)
        return [w_in[j].astype(bf16), w_out[j].astype(bf16), ffn_w_up[i].astype(bf16), ffn_w_down[i].astype(bf16)]

    def full_rows(g):
        return g.reshape(N_DEV * g.shape[1], g.shape[2])

    carried_by = {
        "in_0": [(0, 1), (0, 3)], "mix_0": [(0, 2)], "up_0": [(1, 0), (1, 1)], "gate_0": [(1, 2)], "down_0": [(1, 3)],
        "in_1": [(2, 0)], "mix_1": [(2, 1), (2, 2), (2, 3)], "up_1": [(3, 0), (3, 1)], "gate_1": [(3, 2)], "down_1": [(3, 3)],
    }
    shards = [layer_shards(i) for i in range(DEPTH)]
    gathered = {}
    (gathered[(0, 0)],) = all_gather([shards[0][0]], "gather_weights_0", relay=True)

    def carry(call, site, **kw):
        items = carried_by.get(site, [])
        outs = call(riders=Riders([shards[l][slot] for l, slot in items], True), **kw)
        for item, g in zip(items, outs[len(outs) - len(items):]):
            gathered[item] = g
        return outs[:len(outs) - len(items)]

    saved = []
    weights = []
    xcur = xt
    h = norm_fwd(xcur, norm_g_full[0, 0:1], mod[0, 1], mod[0, 0], "norm1_0")
    for i in range(DEPTH):
        j = i // 2
        sh1, sc1, g1, sh2, sc2, g2 = [mod[i, p] for p in range(6)]
        gn2 = norm_g_full[i, 1:2]
        s = {"x0": xcur, "h": h}
        w_in = gathered[(i, 0)]
        if i % 2 == 0:
            (z,) = carry(functools.partial(mm_nn, h, w_in, bf16, f"gm_in_{i}"), f"in_{i}")
            bs = gm_b_s[j].reshape(GM_HEADS, GM_BLOCK, 1)
            (mixed,) = carry(functools.partial(gm_mix_fwd, z, gm_w_s[j], bs, gm_ln_g[j:j + 1], gm_ln_b[j:j + 1],
                                               f"gm_mix_{i}"), f"mix_{i}")
            s["z"] = z
        else:
            (proj,) = carry(functools.partial(mm_nn, h, w_in, f32, f"hg_in_{i}"), f"in_{i}")
            mixed, states = carry(functools.partial(hg_scan_fwd, proj, lbs[j], hg_gn_full[j:j + 1], f"hg_scan_{i}"),
                                  f"mix_{i}")
            s["proj"], s["states"] = proj, states
        s["mixed"] = mixed
        w_out = full_rows(gathered[(i, 1)])
        y, x1, h2 = carry(functools.partial(mm_nn_residual, mixed, w_out, xcur, g1, (gn2, sc2, sh2), f"mix_out_{i}"),
                          f"out_{i}")
        s["y"], s["x1"] = y, x1
        w_up = gathered[(i, 2)]
        (a,) = carry(functools.partial(mm_nn, h2, w_up, bf16, f"ffn_up_{i}"), f"up_{i}")
        (hid,) = carry(functools.partial(ffn_gate_fwd, a, conv_w_full[i], ffn_conv_b[i:i + 1], f"ffn_gate_{i}"), f"gate_{i}")
        w_down = full_rows(gathered[(i, 3)])
        next_norm = (norm_g_full[i + 1, 0:1], mod[i + 1, 1], mod[i + 1, 0]) if i + 1 < DEPTH else None
        outs = carry(functools.partial(mm_nn_residual, hid, w_down, x1, g2, next_norm, f"ffn_down_{i}"), f"down_{i}")
        fo, x2 = outs[0], outs[1]
        s["h2"], s["a"], s["hid"], s["f"] = h2, a, hid, fo
        weights.append((w_in, w_out, w_up, w_down))
        saved.append(s)
        xcur = x2
        h = outs[2] if next_norm is not None else None

    loss_part, dx, d_final_g, dg2, df = loss_head(xcur, final_g.reshape(1, D), loss_target[0], saved[-1]["f"],
                                                  mod[DEPTH - 1, 5], "loss_head")
    loss = lax.psum(loss_part[0, 0], ("x", "y", "c"))

    def halves(blocked):
        rows = blocked.shape[1] // 2
        return [(blocked, (0, rows)), (blocked, (rows, rows))]

    def by_owner_rows(dw):
        k, n = dw.shape
        return dw.reshape(N_DEV, k // N_DEV, n)

    received = [[[] for _ in range(4)] for _ in range(DEPTH)]

    def send(call, items, **kw):
        outs = call(riders=Riders([arr for arr, _ in items], False), **kw)
        for (_, (layer, slot)), got in zip(items, outs[len(outs) - len(items):]):
            received[layer][slot].append(got)
        return outs[:len(outs) - len(items)]

    dmod = [None] * DEPTH
    d_norm_g = [None] * DEPTH
    d_gm = {k: [None, None] for k in ("ws", "bs", "lng", "lnb")}
    d_hg = {k: [None, None] for k in ("lb", "gn")}
    d_ffn = {k: [None] * DEPTH for k in ("cw", "cb")}
    in_halves = []
    for i in reversed(range(DEPTH)):
        j = i // 2
        s = saved[i]
        w_in, w_out, w_up, w_down = weights[i]
        sh1, sc1, g1, sh2, sc2, g2 = [mod[i, p] for p in range(6)]
        gn1, gn2 = norm_g_full[i, 0:1], norm_g_full[i, 1:2]
        (dw_down,) = mm_tn(s["hid"], df, bf16, f"dw_down_{i}", tn=D)
        (dhid,) = send(functools.partial(mm_nt, df, w_down, bf16, f"dhid_{i}"), in_halves[:1])
        dyc, dwg, dwv, dbg, dbv = send(
            functools.partial(ffn_gate_bwd, s["a"], conv_w_full[i], ffn_conv_b[i:i + 1], dhid, f"ffn_gate_bwd_{i}"),
            in_halves[1:] + [(by_owner_rows(dw_down), (i, 3))])
        d_ffn["cw"][i] = jnp.concatenate([dwg, dwv], axis=1)
        d_ffn["cb"][i] = jnp.concatenate([dbg, dbv], axis=1)
        da = conv_transpose(dyc, conv_w_full[i], f"conv_t_{i}")
        (dw_up,) = mm_tn_by_owner(s["h2"], da, f"dw_up_{i}")
        up_halves = [(part, (i, 2)) for part in halves(dw_up)]
        (dh2,) = send(functools.partial(mm_nt, da, w_up, bf16, f"dh2_{i}"), up_halves[:1])
        dx1, dgn2, dsc2, dsh2, dg1, dy = norm_bwd(s["x1"], gn2, sc2, sh2, dh2, dx, f"norm2_bwd_{i}", gate=(s["y"], g1))
        (dw_out,) = mm_tn(s["mixed"], dy, bf16, f"dw_mix_out_{i}")
        (dmixed,) = mm_nt(dy, w_out, bf16, f"dmixed_{i}")
        if i % 2 == 0:
            bs = gm_b_s[j].reshape(GM_HEADS, GM_BLOCK, 1)
            dpre, dws, dbs, dlng, dlnb = send(
                functools.partial(gm_mix_bwd, s["z"], gm_w_s[j], bs, gm_ln_g[j:j + 1], gm_ln_b[j:j + 1], dmixed,
                                  f"gm_mix_bwd_{i}"), up_halves[1:])
            d_gm["ws"][j], d_gm["bs"][j], d_gm["lng"][j], d_gm["lnb"][j] = dws, dbs.reshape(GM_HEADS, GM_BLOCK), dlng, dlnb
        else:
            dpre, dlb, dgn = send(
                functools.partial(hg_scan_bwd, s["proj"], lbs[j], hg_gn_full[j:j + 1], s["states"], dmixed,
                                  f"hg_scan_bwd_{i}"), up_halves[1:])
            d_hg["lb"][j], d_hg["gn"][j] = dlb, dgn
        (dw_in,) = send(functools.partial(mm_tn_by_owner, s["h"], dpre, f"dw_mix_in_{i}"), [(by_owner_rows(dw_out), (i, 1))])
        in_halves = [(part, (i, 0)) for part in halves(dw_in)]
        (dh,) = send(functools.partial(mm_nt, dpre, w_in, bf16, f"dh_mix_{i}"), in_halves[:1] if i == 0 else [])
        dmod_i = [None, None, dg1, dsh2, dsc2, dg2]
        if i > 0:
            dx, dgn1, dsc1, dsh1, dg2, df = norm_bwd(s["x0"], gn1, sc1, sh1, dh, dx1, f"norm1_bwd_{i}",
                                                     gate=(saved[i - 1]["f"], mod[i - 1, 5]))
        else:
            dx, dgn1, dsc1, dsh1 = send(functools.partial(norm_bwd, s["x0"], gn1, sc1, sh1, dh, dx1, f"norm1_bwd_{i}"),
                                        in_halves[1:])
        dmod_i[0], dmod_i[1] = dsh1, dsc1
        dmod[i] = jnp.concatenate(dmod_i, axis=1)
        d_norm_g[i] = jnp.concatenate([dgn1, dgn2], axis=0)
    grad_x = dx.reshape(1, t, D)

    def parts_of(slot, layers):
        return [received[i][slot] for i in layers]

    w_shards = [gm_w_in, gm_w_out, hg_w_in, hg_w_out, ffn_w_up, ffn_w_down]
    big_parts = [parts_of(0, (0, 2)), parts_of(1, (0, 2)), parts_of(0, (1, 3)), parts_of(1, (1, 3)),
                 parts_of(2, range(DEPTH)), parts_of(3, range(DEPTH))]
    big_m = [m_gm_w_in, m_gm_w_out, m_hg_w_in, m_hg_w_out, m_ffn_w_up, m_ffn_w_down]
    big_v = [v_gm_w_in, v_gm_w_out, v_hg_w_in, v_hg_w_out, v_ffn_w_up, v_ffn_w_down]

    def adam_big(idx, riders=None):
        return adam_reduced(big_parts[idx], w_shards[idx], big_m[idx], big_v[idx], f"adam_big_{idx}", riders=riders)

    small_partials = [jnp.concatenate(d_gm["lng"], axis=0), jnp.concatenate(d_gm["lnb"], axis=0),
                      jnp.stack(d_gm["ws"]), jnp.stack(d_gm["bs"]), jnp.concatenate(d_ffn["cb"], axis=0),
                      d_final_g, d_hg["lb"][1], jnp.concatenate(d_hg["gn"], axis=0), jnp.stack(d_norm_g),
                      jnp.stack(d_ffn["cw"])]
    partial_shapes = [p.shape for p in small_partials]
    packed = _pack(small_partials, rows_mult=8 * N_DEV)
    rows = packed.shape[0] // N_DEV
    big = [None] * 6
    *big[4], recv = adam_big(4, Riders([packed.reshape(N_DEV, rows, LANES)], False))
    *big[5], summed, dmod_all = adam_big(5, Riders([sum_parts(recv, "small_grads_sum"), jnp.concatenate(dmod, axis=0)], True))
    for idx in range(4):
        big[idx] = adam_big(idx)
    (g_gm_w_in, d_gm_w_in, nm_gm_w_in, nv_gm_w_in), (g_gm_w_out, d_gm_w_out, nm_gm_w_out, nv_gm_w_out), \
        (g_hg_w_in, d_hg_w_in, nm_hg_w_in, nv_hg_w_in), (g_hg_w_out, d_hg_w_out, nm_hg_w_out, nv_hg_w_out), \
        (g_ffn_w_up, d_ffn_w_up, nm_ffn_w_up, nv_ffn_w_up), (g_ffn_w_down, d_ffn_w_down, nm_ffn_w_down, nv_ffn_w_down) = big

    dmod_cols = jnp.transpose(lax.dynamic_slice(dmod_all, (0, 0, me * ADA_COLS), (N_DEV, DEPTH, ADA_COLS)), (1, 0, 2))
    g_ada_w, g_ada_b = ada_grads(c_all, dmod_cols, dmod_all.reshape(N_DEV, DEPTH, 1, 6 * D), "ada_grads")
    g_ada_b = g_ada_b.reshape(DEPTH, 6 * D)

    g_ln_g, g_ln_b, g_ws, g_bs, g_cb, g_final, g_lb1, g_gn, g_norm, g_cw = _unpack(summed.reshape(-1), partial_shapes)
    g_final = g_final.reshape(D)

    def my_cols(a, n):
        start = (0,) * (a.ndim - 1) + (me * n,)
        return lax.dynamic_slice(a, start, a.shape[:-1] + (n,))

    g_hg_lb = lower_bound_bwd(hg_lb, my_cols(g_lb1, HG_DIM), "lower_bound_bwd")
    g_hg_gn = my_cols(g_gn, HG_DIM)
    g_norm_g = my_cols(g_norm, HG_DIM)
    g_conv_w = my_cols(g_cw, 2 * FFN_HIDDEN // N_DEV)

    two_d = (-1, ADA_COLS)
    d_ada_w, nm_ada_w, nv_ada_w = [o.reshape(ada_w.shape) for o in adam_plain(
        g_ada_w.reshape(two_d), ada_w.reshape(two_d), m_ada_w.reshape(two_d), v_ada_w.reshape(two_d), "adam_ada_w")]

    small_g = [g_ln_g, g_ln_b, g_ws, g_bs, g_cb, g_ada_b, g_final, g_hg_lb, g_hg_gn, g_norm_g, g_conv_w]
    small_w = [gm_ln_g, gm_ln_b, gm_w_s, gm_b_s, ffn_conv_b, ada_b, final_g, hg_lb, hg_gn_g, norm_g, ffn_conv_w]
    small_m = [m_gm_ln_g, m_gm_ln_b, m_gm_w_s, m_gm_b_s, m_ffn_conv_b, m_ada_b, m_final_g, m_hg_lb, m_hg_gn_g, m_norm_g, m_ffn_conv_w]
    small_v = [v_gm_ln_g, v_gm_ln_b, v_gm_w_s, v_gm_b_s, v_ffn_conv_b, v_ada_b, v_final_g, v_hg_lb, v_hg_gn_g, v_norm_g, v_ffn_conv_w]
    shapes = [w.shape for w in small_w]
    small_g = [g.reshape(s) for g, s in zip(small_g, shapes)]
    outs = adam_plain(_pack(small_g), _pack(small_w), _pack(small_m), _pack(small_v), "adam_small")
    (d_ln_g, d_ln_b, d_ws, d_bs, d_cb, d_ada_b, d_final, d_hg_lb, d_hg_gn, d_norm_g_, d_conv_w), \
        (nm_ln_g, nm_ln_b, nm_ws, nm_bs, nm_cb, nm_ada_b, nm_final, nm_hg_lb, nm_hg_gn, nm_norm_g, nm_conv_w), \
        (nv_ln_g, nv_ln_b, nv_ws, nv_bs, nv_cb, nv_ada_b, nv_final, nv_hg_lb, nv_hg_gn, nv_norm_g, nv_conv_w) = [
            _unpack(o.reshape(-1), shapes) for o in outs]
    g_ln_g, g_ln_b, g_ws, g_bs, g_cb, g_ada_b, g_final, g_hg_lb, g_hg_gn, g_norm_g, g_conv_w = small_g

    grads = (g_gm_w_in, g_ln_g, g_ln_b, g_ws, g_bs, g_gm_w_out, g_hg_w_in, g_hg_lb, g_hg_gn, g_hg_w_out,
             g_ffn_w_up, g_conv_w, g_cb, g_ffn_w_down, g_norm_g, g_ada_w, g_ada_b, g_final)
    deltas = (d_gm_w_in, d_ln_g, d_ln_b, d_ws, d_bs, d_gm_w_out, d_hg_w_in, d_hg_lb, d_hg_gn, d_hg_w_out,
              d_ffn_w_up, d_conv_w, d_cb, d_ffn_w_down, d_norm_g_, d_ada_w, d_ada_b, d_final)
    new_m = (nm_gm_w_in, nm_ln_g, nm_ln_b, nm_ws, nm_bs, nm_gm_w_out, nm_hg_w_in, nm_hg_lb, nm_hg_gn, nm_hg_w_out,
             nm_ffn_w_up, nm_conv_w, nm_cb, nm_ffn_w_down, nm_norm_g, nm_ada_w, nm_ada_b, nm_final)
    new_v = (nv_gm_w_in, nv_ln_g, nv_ln_b, nv_ws, nv_bs, nv_gm_w_out, nv_hg_w_in, nv_hg_lb, nv_hg_gn, nv_hg_w_out,
             nv_ffn_w_up, nv_conv_w, nv_cb, nv_ffn_w_down, nv_norm_g, nv_ada_w, nv_ada_b, nv_final)
    return (loss, grad_x) + grads + deltas + new_m + new_v
```

```python
import functools
import math

import jax
import jax.numpy as jnp
from jax import lax
from jax.experimental import pallas as pl
from jax.experimental.pallas import tpu as pltpu

f32 = jnp.float32
bf16 = jnp.bfloat16
MESH = pl.DeviceIdType.MESH

N_DEV = 8
D = 1024
DEPTH = 4
EPS = 1e-6
GM_WIDTH = 2048
GM_HEADS = 8
GM_HEAD_DIM = 256
GM_BLOCK = 128
CHUNK = 64
HG_HEADS = 8
HG_DIM = 128
FFN_HIDDEN = 2816
ADA_COLS = 6 * D // N_DEV

HG_SUB = 32
HG_PAIR = 8
HG_TOKENS = 128

ADAM_LR = 0.001
ADAM_B1 = 0.9
ADAM_B2 = 0.999
ADAM_EPS = 1e-08
ADAM_WD = 0.01
ADAM_STEP = 10

V7X_VMEM_LIMIT = 56 * 1024 * 1024
LANES = 128


def _cparams(*sem):
    return pltpu.CompilerParams(dimension_semantics=sem or None, vmem_limit_bytes=V7X_VMEM_LIMIT)


def _tile(n, target, mult=LANES):
    best = None
    for t in range(mult, min(n, target) + 1, mult):
        if n % t == 0:
            best = t
    return best or n


WEIGHT_BLOCK_BYTES = 6 * 1024 * 1024


def _weight_tile(n, k):
    return _tile(n, max(LANES, WEIGHT_BLOCK_BYTES // (2 * k)))


def _gelu(x):
    return 0.5 * x * (1.0 + lax.erf(x * (1.0 / math.sqrt(2.0))))


def _mesh_pos():
    return lax.axis_index("x"), lax.axis_index("y"), lax.axis_index("c")


def _flat(pos):
    return 4 * pos[0] + 2 * pos[1] + pos[2]


def _peer(pos, k):
    return ((1 - pos[0]) if k & 4 else pos[0], (1 - pos[1]) if k & 2 else pos[1], (1 - pos[2]) if k & 1 else pos[2])


def _exchange_copies(ins, outs, send_sems, recv_sems, local_sems, gather):
    pos = _mesh_pos()
    me = _flat(pos)

    def src(i, dest):
        if gather:
            return ins[i]
        ref, rows = ins[i] if isinstance(ins[i], tuple) else (ins[i], None)
        return ref.at[dest] if rows is None else ref.at[dest, pl.ds(*rows)]

    local = [pltpu.make_async_copy(src(i, me), outs[i].at[me], local_sems.at[i]) for i in range(len(ins))]
    sends, recvs = [], []
    for k in range(1, N_DEV):
        peer = _peer(pos, k)
        there = _flat(peer)
        for i in range(len(ins)):
            sems = dict(send_sem=send_sems.at[i * 7 + k - 1], recv_sem=recv_sems.at[i * 7 + k - 1],
                        device_id=peer, device_id_type=MESH)
            sends.append(pltpu.make_async_remote_copy(src_ref=src(i, there), dst_ref=outs[i].at[me], **sems))
            recvs.append(pltpu.make_async_remote_copy(src_ref=src(i, there), dst_ref=outs[i].at[there], **sems))
    return local, sends, recvs


def _exchange_start(*refs):
    local, sends, _ = _exchange_copies(*refs)
    for cp in local + sends:
        cp.start()


def _exchange_wait(*refs):
    local, sends, recvs = _exchange_copies(*refs)
    for cp in recvs:
        cp.wait_recv()
    for cp in sends:
        cp.wait_send()
    for cp in local:
        cp.wait()


OTHER_CHIPS = (2, 4, 6)


def _relay_copies(ins, outs, send_sems, recv_sems, local_sems):
    pos = _mesh_pos()
    me = _flat(pos)
    sibling = _peer(pos, 1)
    local = [pltpu.make_async_copy(ins[i], outs[i].at[me], local_sems.at[i]) for i in range(len(ins))]
    first, passes, recvs = [], {k: [] for k in OTHER_CHIPS}, {k: [] for k in range(1, N_DEV)}
    for i in range(len(ins)):
        def copy(k, src, block, to):
            return pltpu.make_async_remote_copy(
                src_ref=src, dst_ref=outs[i].at[block], send_sem=send_sems.at[i * 7 + k - 1],
                recv_sem=recv_sems.at[i * 7 + k - 1], device_id=to, device_id_type=MESH)

        for k in (1,) + OTHER_CHIPS:
            first.append(copy(k, ins[i], me, _peer(pos, k)))
        for k in OTHER_CHIPS:
            there = _flat(_peer(pos, k))
            passes[k].append(copy(k ^ 1, outs[i].at[there], there, sibling))
        for k in range(1, N_DEV):
            there = _flat(_peer(pos, k))
            recvs[k].append(copy(k, ins[i], there, _peer(pos, k)))
    return local, first, passes, recvs


def _relay_start(ins, outs, *sems):
    local, first, _, _ = _relay_copies(ins, outs, *sems)
    for cp in local + first:
        cp.start()


def _relay_wait(ins, outs, *sems):
    local, first, passes, recvs = _relay_copies(ins, outs, *sems)
    for k in OTHER_CHIPS:
        for cp in recvs[k]:
            cp.wait_recv()
        for cp in passes[k]:
            cp.start()
    for k in (1, 3, 5, 7):
        for cp in recvs[k]:
            cp.wait_recv()
    for cp in first + [cp for k in OTHER_CHIPS for cp in passes[k]]:
        cp.wait_send()
    for cp in local:
        cp.wait()


def _exchange_out_shape(a, gather):
    return jax.ShapeDtypeStruct((N_DEV,) + tuple(a.shape) if gather else tuple(a.shape), a.dtype)


def _exchange_sems(n):
    return [pltpu.SemaphoreType.DMA((7 * n,)), pltpu.SemaphoreType.DMA((7 * n,)), pltpu.SemaphoreType.DMA((n,))]


ANY = pl.BlockSpec(memory_space=pl.ANY)


def _exchange(arrs, gather, name, relay=False):
    n = len(arrs)

    def body(*refs):
        ins, outs = refs[:n], refs[n:2 * n]
        if relay:
            _relay_start(ins, outs, *refs[2 * n:])
            _relay_wait(ins, outs, *refs[2 * n:])
        else:
            _exchange_start(ins, outs, *refs[2 * n:], gather)
            _exchange_wait(ins, outs, *refs[2 * n:], gather)

    return pl.pallas_call(
        body, name=name, out_shape=tuple(_exchange_out_shape(a, gather) for a in arrs),
        in_specs=[ANY] * n, out_specs=tuple([ANY] * n), scratch_shapes=_exchange_sems(n),
    )(*arrs)


def all_gather(arrs, name, relay=False):
    return _exchange(arrs, True, name, relay)


def all_to_all(arrs, name):
    return _exchange(arrs, False, name)


class Riders:
    def __init__(self, arrs, gather):
        self.gather = gather
        self.rows = [a[1] if isinstance(a, tuple) else None for a in arrs]
        self.arrs = [a[0] if isinstance(a, tuple) else a for a in arrs]

    def out_shapes(self):
        shapes = []
        for a, rows in zip(self.arrs, self.rows):
            shape = tuple(a.shape) if rows is None else (a.shape[0], rows[1], a.shape[2])
            shapes.append(jax.ShapeDtypeStruct((N_DEV,) + shape if self.gather else shape, a.dtype))
        return shapes


def _call(body, *, name, grid, in_specs, out_specs, out_shape, semantics, scratch_shapes=(), riders=None):
    if riders is None or not riders.arrs:
        return pl.pallas_call(body, name=name, grid=grid, in_specs=in_specs, out_specs=tuple(out_specs),
                              out_shape=tuple(out_shape), scratch_shapes=list(scratch_shapes),
                              compiler_params=_cparams(*semantics))
    n_in, n_out, n_scr, n_r = len(in_specs), len(out_specs), len(scratch_shapes), len(riders.arrs)
    gather = riders.gather

    def hosted(*refs):
        ins, r_ins = refs[:n_in], refs[n_in:n_in + n_r]
        at = n_in + n_r
        outs, r_outs = refs[at:at + n_out], refs[at + n_out:at + n_out + n_r]
        at += n_out + n_r
        scratch, sems = refs[at:at + n_scr], refs[at + n_scr:]
        first = functools.reduce(jnp.logical_and, [pl.program_id(a) == 0 for a in range(len(grid))])
        last = functools.reduce(jnp.logical_and, [pl.program_id(a) == grid[a] - 1 for a in range(len(grid))])

        r_ins = [(ref, rows) if rows is not None else ref for ref, rows in zip(r_ins, riders.rows)]

        @pl.when(first)
        def _():
            if gather:
                _relay_start(r_ins, r_outs, *sems)
            else:
                _exchange_start(r_ins, r_outs, *sems, gather)

        body(*ins, *outs, *scratch)

        @pl.when(last)
        def _():
            if gather:
                _relay_wait(r_ins, r_outs, *sems)
            else:
                _exchange_wait(r_ins, r_outs, *sems, gather)

    call = pl.pallas_call(
        hosted, name=name, grid=grid, in_specs=list(in_specs) + [ANY] * n_r, out_specs=tuple(out_specs) + (ANY,) * n_r,
        out_shape=tuple(out_shape) + tuple(riders.out_shapes()),
        scratch_shapes=list(scratch_shapes) + _exchange_sems(n_r),
        compiler_params=_cparams(*(("arbitrary",) * len(grid))))
    return lambda *args: call(*args, *riders.arrs)


def _shards_per_step(shape):
    _, k, n = shape
    best = None
    for q in (1, 2, 4, 8):
        if (q * n) % LANES == 0 and (best is None or 2 * k * q * n <= WEIGHT_BLOCK_BYTES):
            best = q
    return best


def mm_nn(a, b, out_dtype, name, tm=512, riders=None, transposed=False):
    m, k = a.shape
    tm = _tile(m, tm, 8)
    if b.ndim == 3:
        shard = b.shape[1] if transposed else b.shape[2]
        n = N_DEV * shard
        per_step = _shards_per_step((N_DEV, k, shard))
        tn = per_step * shard
        b_spec = pl.BlockSpec((per_step,) + b.shape[1:], lambda i, j: (j, 0, 0))
        contract = (((1,), (1,)), ((), ())) if transposed else (((1,), (0,)), ((), ()))

        def body(a_ref, b_ref, o_ref):
            for q in range(per_step):
                o_ref[:, q * shard:(q + 1) * shard] = lax.dot_general(
                    a_ref[...], b_ref[q], contract, preferred_element_type=f32).astype(o_ref.dtype)
    else:
        n = b.shape[1]
        tn = _weight_tile(n, k)
        b_spec = pl.BlockSpec((k, tn), lambda i, j: (0, j))

        def body(a_ref, b_ref, o_ref):
            o_ref[...] = jnp.dot(a_ref[...], b_ref[...], preferred_element_type=f32).astype(o_ref.dtype)

    return _call(
        body, name=name, grid=(m // tm, n // tn),
        in_specs=[pl.BlockSpec((tm, k), lambda i, j: (i, 0)), b_spec],
        out_specs=[pl.BlockSpec((tm, tn), lambda i, j: (i, j))],
        out_shape=[jax.ShapeDtypeStruct((m, n), out_dtype)], semantics=("parallel", "parallel"), riders=riders,
    )(a, b)


def mm_nn_residual(a, b, x, gate, norm, name, tm=512, riders=None):
    m, k = a.shape
    n = b.shape[1]
    tm = _tile(m, tm, 8)

    def body(a_ref, b_ref, x_ref, g_ref, *rest):
        if norm is not None:
            gn_ref, sc_ref, sh_ref, y_ref, o_ref, h_ref = rest
        else:
            y_ref, o_ref = rest
        y = jnp.dot(a_ref[...], b_ref[...], preferred_element_type=f32)
        y_ref[...] = y.astype(bf16)
        x_new = x_ref[...] + g_ref[...] * y
        o_ref[...] = x_new
        if norm is not None:
            h_ref[...] = _norm_fn(x_new, gn_ref[...], sc_ref[...], sh_ref[...]).astype(bf16)

    blk = pl.BlockSpec((tm, n), lambda i: (i, 0))
    in_specs = [pl.BlockSpec((tm, k), lambda i: (i, 0)), pl.BlockSpec((k, n), lambda i: (0, 0)), blk, _row(n)]
    out_specs = [blk, blk]
    out_shape = [jax.ShapeDtypeStruct((m, n), bf16), jax.ShapeDtypeStruct((m, n), f32)]
    args = [a, b, x, gate]
    if norm is not None:
        in_specs += [_row(n)] * 3
        out_specs += [blk]
        out_shape += [jax.ShapeDtypeStruct((m, n), bf16)]
        args += list(norm)
    return _call(body, name=name, grid=(m // tm,), in_specs=in_specs, out_specs=out_specs, out_shape=out_shape,
                 semantics=("parallel",), riders=riders)(*args)


def mm_nt(a, b, out_dtype, name, tm=512, riders=None, transposed=False):
    m = a.shape[0]
    tm = _tile(m, tm, 8)
    if b.ndim == 3:
        shard, k = (b.shape[1], b.shape[2]) if transposed else (b.shape[2], b.shape[1])
        tk = k
        b_spec = pl.BlockSpec(b.shape, lambda i, j: (0, 0, 0))
        width = N_DEV * shard
        contract = (((1,), (0,)), ((), ())) if transposed else (((1,), (1,)), ((), ()))

        def body(a_ref, b_ref, o_ref):
            acc = None
            for q in range(N_DEV):
                part = lax.dot_general(a_ref[:, q * shard:(q + 1) * shard], b_ref[q], contract, preferred_element_type=f32)
                acc = part if acc is None else acc + part
            o_ref[...] = acc.astype(o_ref.dtype)
    else:
        k, width = b.shape
        tk = _weight_tile(k, width)
        b_spec = pl.BlockSpec((tk, width), lambda i, j: (j, 0))

        def body(a_ref, b_ref, o_ref):
            o_ref[...] = lax.dot_general(a_ref[...], b_ref[...], (((1,), (1,)), ((), ())),
                                         preferred_element_type=f32).astype(o_ref.dtype)

    return _call(
        body, name=name, grid=(m // tm, k // tk),
        in_specs=[pl.BlockSpec((tm, width), lambda i, j: (i, 0)), b_spec],
        out_specs=[pl.BlockSpec((tm, tk), lambda i, j: (i, j))],
        out_shape=[jax.ShapeDtypeStruct((m, k), out_dtype)], semantics=("parallel", "parallel"), riders=riders,
    )(a, b)


def mm_tn(a, b, out_dtype, name, tm=512, tn=512, riders=None):
    t, m = a.shape
    n = b.shape[1]
    tm, tn = _tile(m, tm), _tile(n, tn)

    def body(a_ref, b_ref, o_ref):
        o_ref[...] = lax.dot_general(a_ref[...], b_ref[...], (((0,), (0,)), ((), ())),
                                     preferred_element_type=f32).astype(o_ref.dtype)

    return _call(
        body, name=name, grid=(m // tm, n // tn),
        in_specs=[pl.BlockSpec((t, tm), lambda i, j: (0, i)), pl.BlockSpec((t, tn), lambda i, j: (0, j))],
        out_specs=[pl.BlockSpec((tm, tn), lambda i, j: (i, j))],
        out_shape=[jax.ShapeDtypeStruct((m, n), out_dtype)], semantics=("parallel", "parallel"), riders=riders,
    )(a, b)


def mm_tn_by_owner(a, b, name, tm=512, riders=None):
    t, m = a.shape
    n = b.shape[1]
    shard = n // N_DEV
    per_step = 1 if shard % LANES == 0 else 2
    assert (per_step * shard) % LANES == 0
    tm = _tile(m, tm)

    def body(a_ref, b_ref, o_ref):
        acc = lax.dot_general(a_ref[...], b_ref[...], (((0,), (0,)), ((), ())), preferred_element_type=f32)
        for q in range(per_step):
            o_ref[q] = acc[:, q * shard:(q + 1) * shard].astype(bf16)

    return _call(
        body, name=name, grid=(m // tm, N_DEV // per_step),
        in_specs=[pl.BlockSpec((t, tm), lambda i, j: (0, i)), pl.BlockSpec((t, per_step * shard), lambda i, j: (0, j))],
        out_specs=[pl.BlockSpec((per_step, tm, shard), lambda i, j: (j, i, 0))],
        out_shape=[jax.ShapeDtypeStruct((N_DEV, m, shard), bf16)], semantics=("parallel", "parallel"), riders=riders,
    )(a, b)


def _norm_fn(x, gn, sc, sh):
    r = lax.rsqrt(jnp.mean(x * x, axis=-1, keepdims=True) + EPS)
    return (x * r * gn) * (1.0 + sc) + sh


def _row(d):
    return pl.BlockSpec((1, d), lambda i: (0, 0))


def norm_fwd(x, gn, sc, sh, name, tm=512):
    t, d = x.shape
    tm = _tile(t, tm, 8)

    def body(x_ref, gn_ref, sc_ref, sh_ref, h_ref):
        h_ref[...] = _norm_fn(x_ref[...], gn_ref[...], sc_ref[...], sh_ref[...]).astype(bf16)

    return pl.pallas_call(
        body, name=name, grid=(t // tm,),
        in_specs=[pl.BlockSpec((tm, d), lambda i: (i, 0)), _row(d), _row(d), _row(d)],
        out_specs=pl.BlockSpec((tm, d), lambda i: (i, 0)),
        out_shape=jax.ShapeDtypeStruct((t, d), bf16), compiler_params=_cparams("parallel"),
    )(x, gn, sc, sh)


def _gate_bwd(dx, y_ref, g_ref, dgate_ref, dy_ref):
    dgate_ref[...] += jnp.sum(dx * y_ref[...].astype(f32), axis=0, keepdims=True)
    dy_ref[...] = (dx * g_ref[...]).astype(bf16)


def norm_bwd(x, gn, sc, sh, dh, dres, name, gate=None, tm=512, riders=None):
    t, d = x.shape
    tm = _tile(t, tm, 8)

    def body(x_ref, gn_ref, sc_ref, sh_ref, dh_ref, dres_ref, *rest):
        if gate is not None:
            y_ref, g_ref, dx_ref, dgn_ref, dsc_ref, dsh_ref, dgate_ref, dy_ref = rest
        else:
            dx_ref, dgn_ref, dsc_ref, dsh_ref = rest

        @pl.when(pl.program_id(0) == 0)
        def _():
            dgn_ref[...] = jnp.zeros_like(dgn_ref)
            dsc_ref[...] = jnp.zeros_like(dsc_ref)
            dsh_ref[...] = jnp.zeros_like(dsh_ref)
            if gate is not None:
                dgate_ref[...] = jnp.zeros_like(dgate_ref)

        _, vjp = jax.vjp(_norm_fn, x_ref[...], gn_ref[...], sc_ref[...], sh_ref[...])
        dx, dgn, dsc, dsh = vjp(dh_ref[...].astype(f32))
        dx = dx + dres_ref[...]
        dx_ref[...] = dx
        dgn_ref[...] += dgn
        dsc_ref[...] += dsc
        dsh_ref[...] += dsh
        if gate is not None:
            _gate_bwd(dx, y_ref, g_ref, dgate_ref, dy_ref)

    blk = pl.BlockSpec((tm, d), lambda i: (i, 0))
    vec = jax.ShapeDtypeStruct((1, d), f32)
    in_specs = [blk, _row(d), _row(d), _row(d), blk, blk]
    out_specs = [blk, _row(d), _row(d), _row(d)]
    out_shape = [jax.ShapeDtypeStruct((t, d), f32), vec, vec, vec]
    args = [x, gn, sc, sh, dh, dres]
    if gate is not None:
        in_specs += [blk, _row(d)]
        out_specs += [_row(d), blk]
        out_shape += [vec, jax.ShapeDtypeStruct((t, d), bf16)]
        args += list(gate)
    return _call(body, name=name, grid=(t // tm,), in_specs=in_specs, out_specs=out_specs, out_shape=out_shape,
                 semantics=("arbitrary",), riders=riders)(*args)


def _loss_fn(x, g, tgt):
    r = lax.rsqrt(jnp.mean(x * x, axis=-1, keepdims=True) + EPS)
    err = jnp.square(x * r * g - tgt)
    return 0.5 * jnp.sum(jnp.mean(err, axis=-1, keepdims=True), axis=0, keepdims=True)


def loss_head(x, g, tgt, y, gate, name, tm=512):
    t, d = x.shape
    tm = _tile(t, tm, 8)

    def body(x_ref, g_ref, t_ref, y_ref, gate_ref, loss_ref, dx_ref, dg_ref, dgate_ref, dy_ref):
        @pl.when(pl.program_id(0) == 0)
        def _():
            loss_ref[...] = jnp.zeros_like(loss_ref)
            dg_ref[...] = jnp.zeros_like(dg_ref)
            dgate_ref[...] = jnp.zeros_like(dgate_ref)

        loss, vjp = jax.vjp(_loss_fn, x_ref[...], g_ref[...], t_ref[...])
        dx, dg, _ = vjp(jnp.ones((1, 1), f32))
        dx_ref[...] = dx
        loss_ref[...] += loss
        dg_ref[...] += dg
        _gate_bwd(dx, y_ref, gate_ref, dgate_ref, dy_ref)

    blk = pl.BlockSpec((tm, d), lambda i: (i, 0))
    vec = jax.ShapeDtypeStruct((1, d), f32)
    return pl.pallas_call(
        body, name=name, grid=(t // tm,),
        in_specs=[blk, _row(d), blk, blk, _row(d)],
        out_specs=(pl.BlockSpec((1, 1), lambda i: (0, 0)), blk, _row(d), _row(d), blk),
        out_shape=(jax.ShapeDtypeStruct((1, 1), f32), jax.ShapeDtypeStruct((t, d), f32), vec, vec,
                   jax.ShapeDtypeStruct((t, d), bf16)),
        compiler_params=_cparams("arbitrary"),
    )(x, g, tgt, y, gate)


def _gm_block_fn(z, ws, bs, lng, lnb):
    u = _gelu(z[:, :GM_WIDTH])
    vg = _gelu(z[:, GM_WIDTH:])
    mu = jnp.mean(vg, axis=-1, keepdims=True)
    var = jnp.mean(jnp.square(vg - mu), axis=-1, keepdims=True)
    vn = (vg - mu) * lax.rsqrt(var + EPS) * lng + lnb
    row = lax.broadcasted_iota(jnp.int32, (GM_BLOCK, GM_BLOCK), 0) // CHUNK
    col = lax.broadcasted_iota(jnp.int32, (GM_BLOCK, GM_BLOCK), 1) // CHUNK
    parts = []
    for h in range(GM_HEADS):
        w = jnp.where(row >= col, ws[h], 0.0)
        cols = slice(h * GM_HEAD_DIM, (h + 1) * GM_HEAD_DIM)
        s = jnp.dot(w.astype(bf16), vn[:, cols].astype(bf16), preferred_element_type=f32) + bs[h]
        parts.append(u[:, cols] * s)
    return jnp.concatenate(parts, axis=1)


def _gm_param_specs():
    return [pl.BlockSpec((GM_HEADS, GM_BLOCK, GM_BLOCK), lambda i: (0, 0, 0)),
            pl.BlockSpec((GM_HEADS, GM_BLOCK, 1), lambda i: (0, 0, 0)), _row(GM_WIDTH), _row(GM_WIDTH)]


def gm_mix_fwd(z, ws, bs, lng, lnb, name, riders=None):
    t = z.shape[0]

    def body(z_ref, ws_ref, bs_ref, lng_ref, lnb_ref, o_ref):
        o_ref[...] = _gm_block_fn(z_ref[...].astype(f32), ws_ref[...], bs_ref[...], lng_ref[...],
                                  lnb_ref[...]).astype(bf16)

    return _call(
        body, name=name, grid=(t // GM_BLOCK,),
        in_specs=[pl.BlockSpec((GM_BLOCK, 2 * GM_WIDTH), lambda i: (i, 0))] + _gm_param_specs(),
        out_specs=[pl.BlockSpec((GM_BLOCK, GM_WIDTH), lambda i: (i, 0))],
        out_shape=[jax.ShapeDtypeStruct((t, GM_WIDTH), bf16)], semantics=("parallel",), riders=riders,
    )(z, ws, bs, lng, lnb)


def gm_mix_bwd(z, ws, bs, lng, lnb, dgated, name, riders=None):
    t = z.shape[0]

    def body(z_ref, ws_ref, bs_ref, lng_ref, lnb_ref, dg_ref, dz_ref, dws_ref, dbs_ref, dlng_ref, dlnb_ref):
        _, vjp = jax.vjp(_gm_block_fn, z_ref[...].astype(f32), ws_ref[...], bs_ref[...], lng_ref[...], lnb_ref[...])
        dz, dws, dbs, dlng, dlnb = vjp(dg_ref[...].astype(f32))
        dz_ref[...] = dz.astype(bf16)

        @pl.when(pl.program_id(0) == 0)
        def _():
            dws_ref[...] = jnp.zeros_like(dws_ref)
            dbs_ref[...] = jnp.zeros_like(dbs_ref)
            dlng_ref[...] = jnp.zeros_like(dlng_ref)
            dlnb_ref[...] = jnp.zeros_like(dlnb_ref)

        dws_ref[...] += dws
        dbs_ref[...] += dbs
        dlng_ref[...] += dlng
        dlnb_ref[...] += dlnb

    zblk = pl.BlockSpec((GM_BLOCK, 2 * GM_WIDTH), lambda i: (i, 0))
    return _call(
        body, name=name, grid=(t // GM_BLOCK,),
        in_specs=[zblk] + _gm_param_specs() + [pl.BlockSpec((GM_BLOCK, GM_WIDTH), lambda i: (i, 0))],
        out_specs=[zblk] + _gm_param_specs(),
        out_shape=[jax.ShapeDtypeStruct((t, 2 * GM_WIDTH), bf16),
                   jax.ShapeDtypeStruct((GM_HEADS, GM_BLOCK, GM_BLOCK), f32),
                   jax.ShapeDtypeStruct((GM_HEADS, GM_BLOCK, 1), f32),
                   jax.ShapeDtypeStruct((1, GM_WIDTH), f32), jax.ShapeDtypeStruct((1, GM_WIDTH), f32)],
        semantics=("arbitrary",), riders=riders,
    )(z, ws, bs, lng, lnb, dgated)


@functools.partial(jax.custom_vjp, nondiff_argnums=(1,))
def _rows_up(x, shift):
    return x if shift == 0 else pltpu.roll(x, x.shape[1] - shift, axis=1)


def _rows_up_fwd(x, shift):
    return _rows_up(x, shift), None


def _rows_up_bwd(shift, _, g):
    return (g if shift == 0 else pltpu.roll(g, shift, axis=1),)


_rows_up.defvjp(_rows_up_fwd, _rows_up_bwd)


def _hg_block_fn(qp, fz, iv, gp, s0, lb, gn):
    n, ns, d = HG_SUB, HG_TOKENS // HG_SUB, HG_DIM
    p, nb, per_sub = HG_PAIR, HG_TOKENS // HG_PAIR, HG_SUB // HG_PAIR
    f = lb + (1.0 - lb) * jax.nn.sigmoid(fz)
    g = jnp.log(f)
    k = 1.0 - f
    q = qp * jax.nn.sigmoid(qp)
    v = iv.astype(bf16)
    row = lax.broadcasted_iota(jnp.int32, (HG_TOKENS, HG_TOKENS), 0)
    col = lax.broadcasted_iota(jnp.int32, (HG_TOKENS, HG_TOKENS), 1)
    same_sub = col // n == row // n
    tri = ((col <= row) & same_sub).astype(f32)
    cum = jnp.dot(tri, g, precision=lax.Precision.HIGHEST, preferred_element_type=f32)
    cum_b, q_b, k_b, f_b = cum.reshape(nb, p, d), q.reshape(nb, p, d), k.reshape(nb, p, d), f.reshape(nb, p, d)
    j_b = lax.broadcasted_iota(jnp.int32, (nb, p, d), 1)
    j_col = lax.broadcasted_iota(jnp.int32, (nb, p, 1), 1)
    scores_t = jnp.zeros((HG_TOKENS, HG_TOKENS), f32)
    weight = k_b
    for delta in range(p):
        if delta:
            weight = weight * _rows_up(f_b, delta)
        pair = jnp.sum(_rows_up(q_b, delta) * weight, axis=2, keepdims=True)
        pair = jnp.where(j_col < p - delta, pair, 0.0)
        scores_t = scores_t + jnp.where(col == row + delta, pair.reshape(HG_TOKENS, 1), 0.0)
    o = lax.dot_general(scores_t.astype(bf16), v, (((0,), (0,)), ((), ())), preferred_element_type=f32)
    last = cum_b[:, p - 1:p, :]
    before = jnp.concatenate([jnp.zeros((1, 1, d), f32), last[:-1]], axis=0)
    before = jnp.broadcast_to(before, (nb, p, d)).reshape(HG_TOKENS, d)
    block = (lax.broadcasted_iota(jnp.int32, (HG_TOKENS, d), 0) // p) % per_sub
    q_late = q * jnp.exp(jnp.where(block > 0, cum - before, -1e30))
    last_s = last.reshape(ns, per_sub, d)
    q_parts, k_parts = [], []
    for m in range(1, per_sub):
        split = jnp.broadcast_to(last_s[:, m - 1:m, :], (ns, n, d)).reshape(HG_TOKENS, d)
        k_parts.append(k * jnp.exp(jnp.where(block < m, split - cum, -1e30)))
        q_parts.append(jnp.where(block == m, q_late, 0.0))
    scores = lax.dot_general(jnp.concatenate(q_parts, axis=1).astype(bf16), jnp.concatenate(k_parts, axis=1).astype(bf16),
                             (((1,), (1,)), ((), ())), preferred_element_type=f32)
    o = o + jnp.dot(jnp.where(same_sub, scores, 0.0).astype(bf16), v, preferred_element_type=f32)
    cum_s = cum.reshape(ns, n, d)
    tot = cum_s[:, n - 1:n, :]
    kt_t = (k.reshape(ns, n, d) * jnp.exp(tot - cum_s)).reshape(HG_TOKENS, d).T
    lane_sub = lax.broadcasted_iota(jnp.int32, (d, HG_TOKENS), 1) // n
    k_by_sub = jnp.concatenate([jnp.where(lane_sub == b, kt_t, 0.0) for b in range(ns)], axis=0).astype(bf16)
    update = jnp.dot(k_by_sub, v, preferred_element_type=f32)
    decay = jnp.exp(tot.reshape(ns, d)).T
    state = s0
    states = []
    for a in range(ns):
        states.append(state.astype(bf16))
        state = decay[:, a:a + 1] * state + update[a * d:(a + 1) * d]
    qt = q * jnp.exp(cum)
    row_sub = lax.broadcasted_iota(jnp.int32, (HG_TOKENS, d), 0) // n
    q_by_sub = jnp.concatenate([jnp.where(row_sub == a, qt, 0.0) for a in range(ns)], axis=1).astype(bf16)
    o = o + jnp.dot(q_by_sub, jnp.concatenate(states, axis=0), preferred_element_type=f32)
    on = o * lax.rsqrt(jnp.mean(o * o, axis=-1, keepdims=True) + EPS) * gn
    return on * (gp * jax.nn.sigmoid(gp)), state


def _head_parts(ref, h):
    return [ref[:, p * D + h * HG_DIM:p * D + (h + 1) * HG_DIM] for p in range(4)]


def hg_scan_fwd(proj, lb, gn, name, riders=None):
    t = proj.shape[0]
    nt = t // HG_TOKENS

    def body(p_ref, lb_ref, gn_ref, y_ref, s_ref, state):
        @pl.when(pl.program_id(0) == 0)
        def _():
            state[...] = jnp.zeros_like(state)

        for h in range(HG_HEADS):
            cols = slice(h * HG_DIM, (h + 1) * HG_DIM)
            s_ref[h, 0] = state[h]
            y, s1 = _hg_block_fn(*_head_parts(p_ref, h), state[h], lb_ref[:, cols], gn_ref[:, cols])
            y_ref[:, cols] = y.astype(bf16)
            state[h] = s1

    return _call(
        body, name=name, grid=(nt,),
        in_specs=[pl.BlockSpec((HG_TOKENS, 4 * D), lambda i: (i, 0)), _row(D), _row(D)],
        out_specs=[pl.BlockSpec((HG_TOKENS, D), lambda i: (i, 0)),
                   pl.BlockSpec((HG_HEADS, 1, HG_DIM, HG_DIM), lambda i: (0, i, 0, 0))],
        out_shape=[jax.ShapeDtypeStruct((t, D), bf16), jax.ShapeDtypeStruct((HG_HEADS, nt, HG_DIM, HG_DIM), f32)],
        scratch_shapes=[pltpu.VMEM((HG_HEADS, HG_DIM, HG_DIM), f32)],
        semantics=("arbitrary",), riders=riders,
    )(proj, lb, gn)


def hg_scan_bwd(proj, lb, gn, states, dy, name, riders=None):
    t = proj.shape[0]
    nt = t // HG_TOKENS

    def body(p_ref, lb_ref, gn_ref, s_ref, dy_ref, dp_ref, dlb_ref, dgn_ref, dstate):
        @pl.when(pl.program_id(0) == 0)
        def _():
            dstate[...] = jnp.zeros_like(dstate)
            dlb_ref[...] = jnp.zeros_like(dlb_ref)
            dgn_ref[...] = jnp.zeros_like(dgn_ref)

        for h in range(HG_HEADS):
            cols = slice(h * HG_DIM, (h + 1) * HG_DIM)
            _, vjp = jax.vjp(_hg_block_fn, *_head_parts(p_ref, h), s_ref[h, 0], lb_ref[:, cols], gn_ref[:, cols])
            grads = vjp((dy_ref[:, cols].astype(f32), dstate[h]))
            for p in range(4):
                dp_ref[:, p * D + h * HG_DIM:p * D + (h + 1) * HG_DIM] = grads[p].astype(bf16)
            dstate[h] = grads[4]
            dlb_ref[:, cols] += grads[5]
            dgn_ref[:, cols] += grads[6]

    small = jax.ShapeDtypeStruct((1, D), f32)
    return _call(
        body, name=name, grid=(nt,),
        in_specs=[pl.BlockSpec((HG_TOKENS, 4 * D), lambda i: (nt - 1 - i, 0)), _row(D), _row(D),
                  pl.BlockSpec((HG_HEADS, 1, HG_DIM, HG_DIM), lambda i: (0, nt - 1 - i, 0, 0)),
                  pl.BlockSpec((HG_TOKENS, D), lambda i: (nt - 1 - i, 0))],
        out_specs=[pl.BlockSpec((HG_TOKENS, 4 * D), lambda i: (nt - 1 - i, 0)), _row(D), _row(D)],
        out_shape=[jax.ShapeDtypeStruct((t, 4 * D), bf16), small, small],
        scratch_shapes=[pltpu.VMEM((HG_HEADS, HG_DIM, HG_DIM), f32)],
        semantics=("arbitrary",), riders=riders,
    )(proj, lb, gn, states, dy)


FFN_COLS = 1408
HALO = 8
STRIP = 16


def _ffn_specs(tm):
    nb = tm // HALO
    main_g = pl.BlockSpec((tm, FFN_COLS), lambda j, i: (i, j))
    main_v = pl.BlockSpec((tm, FFN_COLS), lambda j, i: (i, j + 2))
    halo_g = pl.BlockSpec((HALO, FFN_COLS), lambda j, i: (jnp.maximum(i * nb - 1, 0), j))
    halo_v = pl.BlockSpec((HALO, FFN_COLS), lambda j, i: (jnp.maximum(i * nb - 1, 0), j + 2))
    w_g = pl.BlockSpec((3, FFN_COLS), lambda j, i: (0, j))
    w_v = pl.BlockSpec((3, FFN_COLS), lambda j, i: (0, j + 2))
    b_g = pl.BlockSpec((1, FFN_COLS), lambda j, i: (0, j))
    b_v = pl.BlockSpec((1, FFN_COLS), lambda j, i: (0, j + 2))
    return [main_g, halo_g, main_v, halo_v, w_g, w_v, b_g, b_v]


def _strip_rows(r):
    return pl.ds(r * STRIP, STRIP) if isinstance(r, int) else pl.ds(pl.multiple_of(r * STRIP, STRIP), STRIP)


def _for_strips(nstrip, strip, reverse=False):
    if reverse:
        strip(nstrip - 1, True)
        lax.fori_loop(0, nstrip - 1, lambda k, c: (strip(nstrip - 2 - k, False), c)[1], 0)
    else:
        strip(0, True)
        lax.fori_loop(1, nstrip, lambda r, c: (strip(r, False), c)[1], 0)


def _conv_strip(main_ref, halo_ref, w_ref, b_ref, r, edge, cols, rowi):
    cur = main_ref[_strip_rows(r), cols].astype(f32)
    if edge:
        h = jnp.where(pl.program_id(1) == 0, 0.0, halo_ref[:, cols].astype(f32))
        prev = jnp.concatenate([jnp.zeros_like(h), h], axis=0)
    else:
        prev = main_ref[_strip_rows(r - 1), cols].astype(f32)
    a1 = jnp.where(rowi < 1, pltpu.roll(prev, 1, axis=0), pltpu.roll(cur, 1, axis=0))
    a2 = jnp.where(rowi < 2, pltpu.roll(prev, 2, axis=0), pltpu.roll(cur, 2, axis=0))
    y = b_ref[:, cols] + w_ref[0:1, cols] * a2 + w_ref[1:2, cols] * a1 + w_ref[2:3, cols] * cur
    return y, (cur, a1, a2)


def ffn_gate_fwd(a, cw, cb, name, tm=512, riders=None):
    t = a.shape[0]
    tm = _tile(t, tm, STRIP)

    def body(ag_ref, hg_ref, av_ref, hv_ref, wg_ref, wv_ref, bg_ref, bv_ref, o_ref):
        rowi = lax.broadcasted_iota(jnp.int32, (STRIP, LANES), 0)

        def strip(r, edge):
            for c in range(FFN_COLS // LANES):
                cols = pl.ds(c * LANES, LANES)
                yg, _ = _conv_strip(ag_ref, hg_ref, wg_ref, bg_ref, r, edge, cols, rowi)
                yv, _ = _conv_strip(av_ref, hv_ref, wv_ref, bv_ref, r, edge, cols, rowi)
                o_ref[_strip_rows(r), cols] = (_gelu(yg) * yv).astype(bf16)

        _for_strips(tm // STRIP, strip)

    return _call(
        body, name=name, grid=(2, t // tm), in_specs=_ffn_specs(tm),
        out_specs=[pl.BlockSpec((tm, FFN_COLS), lambda j, i: (i, j))],
        out_shape=[jax.ShapeDtypeStruct((t, FFN_HIDDEN), bf16)],
        semantics=("parallel", "arbitrary"), riders=riders,
    )(a, a, a, a, cw, cw, cb, cb)


def ffn_gate_bwd(a, cw, cb, dhid, name, tm=512, riders=None):
    t = a.shape[0]
    tm = _tile(t, tm, STRIP)

    def body(ag_ref, hg_ref, av_ref, hv_ref, wg_ref, wv_ref, bg_ref, bv_ref, dh_ref,
             dy_ref, dwg_ref, dwv_ref, dbg_ref, dbv_ref, acc):
        rowi = lax.broadcasted_iota(jnp.int32, (STRIP, LANES), 0)

        @pl.when(pl.program_id(1) == 0)
        def _():
            acc[...] = jnp.zeros_like(acc)

        def strip(r, edge):
            rows = _strip_rows(r)
            for c in range(FFN_COLS // LANES):
                cols = pl.ds(c * LANES, LANES)
                yg, taps_g = _conv_strip(ag_ref, hg_ref, wg_ref, bg_ref, r, edge, cols, rowi)
                yv, taps_v = _conv_strip(av_ref, hv_ref, wv_ref, bv_ref, r, edge, cols, rowi)
                dh = dh_ref[rows, cols].astype(f32)
                cdf = 0.5 * (1.0 + lax.erf(yg * (1.0 / math.sqrt(2.0))))
                pdf = jnp.exp(-0.5 * yg * yg) * (1.0 / math.sqrt(2.0 * math.pi))
                dyg = dh * yv * (cdf + yg * pdf)
                dyv = dh * (yg * cdf)
                dy_ref[0, rows, cols] = dyg.astype(bf16)
                dy_ref[1, rows, cols] = dyv.astype(bf16)
                for p, (dy, (a0, a1, a2)) in enumerate(((dyg, taps_g), (dyv, taps_v))):
                    acc[4 * p + 0, :, cols] += dy * a2
                    acc[4 * p + 1, :, cols] += dy * a1
                    acc[4 * p + 2, :, cols] += dy * a0
                    acc[4 * p + 3, :, cols] += dy

        _for_strips(tm // STRIP, strip)

        @pl.when(pl.program_id(1) == pl.num_programs(1) - 1)
        def _():
            for p, (dw_ref, db_ref) in enumerate(((dwg_ref, dbg_ref), (dwv_ref, dbv_ref))):
                for tap in range(3):
                    dw_ref[tap:tap + 1, :] = jnp.sum(acc[4 * p + tap], axis=0, keepdims=True)
                db_ref[...] = jnp.sum(acc[4 * p + 3], axis=0, keepdims=True)

    half_w = pl.BlockSpec((3, FFN_COLS), lambda j, i: (0, j))
    half_b = pl.BlockSpec((1, FFN_COLS), lambda j, i: (0, j))
    return _call(
        body, name=name, grid=(2, t // tm),
        in_specs=_ffn_specs(tm) + [pl.BlockSpec((tm, FFN_COLS), lambda j, i: (i, j))],
        out_specs=[pl.BlockSpec((2, tm, FFN_COLS), lambda j, i: (0, i, j)), half_w, half_w, half_b, half_b],
        out_shape=[jax.ShapeDtypeStruct((2, t, FFN_HIDDEN), bf16),
                   jax.ShapeDtypeStruct((3, FFN_HIDDEN), f32), jax.ShapeDtypeStruct((3, FFN_HIDDEN), f32),
                   jax.ShapeDtypeStruct((1, FFN_HIDDEN), f32), jax.ShapeDtypeStruct((1, FFN_HIDDEN), f32)],
        scratch_shapes=[pltpu.VMEM((8, STRIP, FFN_COLS), f32)],
        semantics=("parallel", "arbitrary"), riders=riders,
    )(a, a, a, a, cw, cw, cb, cb, dhid)


def conv_transpose(dy, cw, name, tm=512):
    _, t, fh = dy.shape
    tm = _tile(t, tm, STRIP)
    nb = tm // HALO
    last_halo = t // HALO - 1
    ncol = fh // FFN_COLS

    def body(main_ref, halo_ref, w_ref, o_ref):
        rowi = lax.broadcasted_iota(jnp.int32, (STRIP, LANES), 0)
        last_block = pl.program_id(2) == pl.num_programs(2) - 1

        def strip(r, edge):
            rows = _strip_rows(r)
            for c in range(FFN_COLS // LANES):
                cols = pl.ds(c * LANES, LANES)
                cur = main_ref[0, rows, cols].astype(f32)
                if edge:
                    h = jnp.where(last_block, 0.0, halo_ref[0, :, cols].astype(f32))
                    nxt = jnp.concatenate([h, jnp.zeros_like(h)], axis=0)
                else:
                    nxt = main_ref[0, _strip_rows(r + 1), cols].astype(f32)
                d1 = jnp.where(rowi >= STRIP - 1, pltpu.roll(nxt, STRIP - 1, axis=0), pltpu.roll(cur, STRIP - 1, axis=0))
                d2 = jnp.where(rowi >= STRIP - 2, pltpu.roll(nxt, STRIP - 2, axis=0), pltpu.roll(cur, STRIP - 2, axis=0))
                o_ref[rows, cols] = (w_ref[2:3, cols] * cur + w_ref[1:2, cols] * d1 + w_ref[0:1, cols] * d2).astype(bf16)

        _for_strips(tm // STRIP, strip, reverse=True)

    return pl.pallas_call(
        body, name=name, grid=(2, ncol, t // tm),
        in_specs=[pl.BlockSpec((1, tm, FFN_COLS), lambda p, j, i: (p, i, j)),
                  pl.BlockSpec((1, HALO, FFN_COLS), lambda p, j, i: (p, jnp.minimum((i + 1) * nb, last_halo), j)),
                  pl.BlockSpec((3, FFN_COLS), lambda p, j, i: (0, p * ncol + j))],
        out_specs=pl.BlockSpec((tm, FFN_COLS), lambda p, j, i: (i, p * ncol + j)),
        out_shape=jax.ShapeDtypeStruct((t, 2 * fh), bf16),
        compiler_params=_cparams("parallel", "parallel", "arbitrary"),
    )(dy, dy, cw)


def ada_mod(c_all, ada_w, ada_b_cols, name):
    cols = ada_w.shape[2]

    def body(c_ref, w_ref, b_ref, o_ref):
        c = c_ref[...]
        cond = (c * jax.nn.sigmoid(c)).astype(bf16)
        o_ref[0] = jnp.dot(cond, w_ref[0].astype(bf16), preferred_element_type=f32) + b_ref[0]

    return pl.pallas_call(
        body, name=name, grid=(DEPTH,),
        in_specs=[pl.BlockSpec((N_DEV, D), lambda i: (0, 0)), pl.BlockSpec((1, D, cols), lambda i: (i, 0, 0)),
                  pl.BlockSpec((1, 1, cols), lambda i: (i, 0, 0))],
        out_specs=pl.BlockSpec((1, N_DEV, cols), lambda i: (i, 0, 0)),
        out_shape=jax.ShapeDtypeStruct((DEPTH, N_DEV, cols), f32), compiler_params=_cparams("parallel"),
    )(c_all, ada_w, ada_b_cols)


def ada_grads(c_all, dmod_cols, dmod_all, name):
    cols = dmod_cols.shape[2]

    def body(c_ref, dm_ref, da_ref, dw_ref, db_ref):
        c = c_ref[...]
        cond = c * jax.nn.sigmoid(c)
        dw_ref[0] = lax.dot_general(cond, dm_ref[0], (((0,), (0,)), ((), ())), precision=lax.Precision.HIGHEST,
                                    preferred_element_type=f32)
        acc = da_ref[0, 0]
        for e in range(1, N_DEV):
            acc = acc + da_ref[e, 0]
        db_ref[0] = acc

    return pl.pallas_call(
        body, name=name, grid=(DEPTH,),
        in_specs=[pl.BlockSpec((N_DEV, D), lambda i: (0, 0)), pl.BlockSpec((1, N_DEV, cols), lambda i: (i, 0, 0)),
                  pl.BlockSpec((N_DEV, 1, 1, 6 * D), lambda i: (0, i, 0, 0))],
        out_specs=(pl.BlockSpec((1, D, cols), lambda i: (i, 0, 0)), pl.BlockSpec((1, 1, 6 * D), lambda i: (i, 0, 0))),
        out_shape=(jax.ShapeDtypeStruct((DEPTH, D, cols), f32), jax.ShapeDtypeStruct((DEPTH, 1, 6 * D), f32)),
        compiler_params=_cparams("parallel"),
    )(c_all, dmod_cols, dmod_all)


def lower_bound_fwd(hg_lb, name):
    n = hg_lb.shape[1]

    def body(l_ref, o_ref):
        o_ref[...] = jax.nn.sigmoid(l_ref[1:2, :] - l_ref[0:1, :])

    return pl.pallas_call(body, name=name, out_shape=jax.ShapeDtypeStruct((1, n), f32))(hg_lb)


def lower_bound_bwd(hg_lb, dlb, name):
    n = hg_lb.shape[1]

    def body(l_ref, d_ref, o_ref):
        p = jax.nn.sigmoid(l_ref[1:2, :] - l_ref[0:1, :])
        g = d_ref[...] * p * (1.0 - p)
        o_ref[0:1, :] = -g
        o_ref[1:2, :] = g

    return pl.pallas_call(body, name=name, out_shape=jax.ShapeDtypeStruct((2, n), f32))(hg_lb, dlb)


def _adamw(w, g, m, v):
    m = ADAM_B1 * m + (1.0 - ADAM_B1) * g
    v = ADAM_B2 * v + (1.0 - ADAM_B2) * jnp.square(g)
    m_hat = m / (1.0 - ADAM_B1 ** ADAM_STEP)
    v_hat = v / (1.0 - ADAM_B2 ** ADAM_STEP)
    delta = -ADAM_LR * (m_hat / (jnp.sqrt(v_hat) + ADAM_EPS) + ADAM_WD * w)
    return delta, m, v


ADAM_BLOCK_BYTES = 32 * 1024 * 1024


def adam_reduced(parts, w, m, v, name):
    layers, r, c = w.shape
    outs = None
    for layer in range(layers):
        outs = _adam_layer(parts[layer], w, m, v, layer, outs, f"{name}_{layer}")
    return outs


def _adam_layer(parts, w, m, v, layer, prev, name):
    layers, r, c = w.shape
    rows = parts[0].shape[1]
    assert all(p.shape == (N_DEV, rows, c) for p in parts) and rows * len(parts) == r
    row_bytes = 2 * (len(parts) * N_DEV * c * 2 + 7 * c * 4)
    tr = _tile(rows, max(16, ADAM_BLOCK_BYTES // row_bytes), 16)
    steps = rows // tr
    n_prev = 0 if prev is None else 4

    def body(*refs):
        p_refs = refs[:len(parts)]
        w_ref, m_ref, v_ref = refs[len(parts):len(parts) + 3]
        g_ref, d_ref, mo_ref, vo_ref = refs[len(parts) + 3 + n_prev:]
        for idx in range(len(parts)):
            @pl.when(pl.program_id(0) == idx)
            def _():
                g = p_refs[idx][0].astype(f32)
                for j in range(1, N_DEV):
                    g = g + p_refs[idx][j].astype(f32)
                g_ref[...] = g
                d_ref[...], mo_ref[...], vo_ref[...] = _adamw(w_ref[...], g, m_ref[...], v_ref[...])

    def part_spec(idx):
        return pl.BlockSpec((N_DEV, tr, c), lambda p, i: (0, jnp.where(p == idx, i, 0), 0))

    blk = pl.BlockSpec((None, tr, c), lambda p, i: (layer, p * steps + i, 0))
    out = jax.ShapeDtypeStruct((layers, r, c), f32)
    n_in = len(parts) + 3
    return pl.pallas_call(
        body, name=name, grid=(len(parts), steps),
        in_specs=[part_spec(idx) for idx in range(len(parts))] + [blk, blk, blk] + [ANY] * n_prev,
        out_specs=(blk, blk, blk, blk), out_shape=(out, out, out, out),
        input_output_aliases={n_in + k: k for k in range(n_prev)},
        compiler_params=_cparams("arbitrary", "arbitrary"),
    )(*parts, w, m, v, *(prev or ()))


def adam_plain(g, w, m, v, name, tr=256):
    r, c = w.shape
    tr = _tile(r, tr, 8)

    def body(g_ref, w_ref, m_ref, v_ref, d_ref, mo_ref, vo_ref):
        d_ref[...], mo_ref[...], vo_ref[...] = _adamw(w_ref[...], g_ref[...], m_ref[...], v_ref[...])

    blk = pl.BlockSpec((tr, c), lambda i: (i, 0))
    out = jax.ShapeDtypeStruct((r, c), f32)
    return pl.pallas_call(
        body, name=name, grid=(r // tr,), in_specs=[blk, blk, blk, blk], out_specs=(blk, blk, blk),
        out_shape=(out, out, out), compiler_params=_cparams("parallel"),
    )(g, w, m, v)


def sum_parts(parts, name):
    _, r, c = parts.shape

    def body(p_ref, o_ref):
        acc = p_ref[0]
        for j in range(1, N_DEV):
            acc = acc + p_ref[j]
        o_ref[...] = acc

    return pl.pallas_call(body, name=name, out_shape=jax.ShapeDtypeStruct((r, c), f32))(parts)


def _pack(arrs, rows_mult=8):
    flat = jnp.concatenate([a.reshape(-1) for a in arrs])
    rows = -(-flat.shape[0] // LANES)
    rows = -(-rows // rows_mult) * rows_mult
    return jnp.pad(flat, (0, rows * LANES - flat.shape[0])).reshape(rows, LANES)


def _unpack(flat, shapes):
    out, at = [], 0
    for s in shapes:
        n = math.prod(s)
        out.append(flat[at:at + n].reshape(s))
        at += n
    return out


def kernel(x, c, gm_w_in, gm_ln_g, gm_ln_b, gm_w_s, gm_b_s, gm_w_out, hg_w_in, hg_lb, hg_gn_g, hg_w_out, ffn_w_up, ffn_conv_w, ffn_conv_b, ffn_w_down, norm_g, ada_w, ada_b, final_g, loss_target, m_gm_w_in, m_gm_ln_g, m_gm_ln_b, m_gm_w_s, m_gm_b_s, m_gm_w_out, m_hg_w_in, m_hg_lb, m_hg_gn_g, m_hg_w_out, m_ffn_w_up, m_ffn_conv_w, m_ffn_conv_b, m_ffn_w_down, m_norm_g, m_ada_w, m_ada_b, m_final_g, v_gm_w_in, v_gm_ln_g, v_gm_ln_b, v_gm_w_s, v_gm_b_s, v_gm_w_out, v_hg_w_in, v_hg_lb, v_hg_gn_g, v_hg_w_out, v_ffn_w_up, v_ffn_conv_w, v_ffn_conv_b, v_ffn_w_down, v_norm_g, v_ada_w, v_ada_b, v_final_g):
    me = _flat(_mesh_pos())
    xt = x[0]
    t = xt.shape[0]

    small_shapes = [(1, D), (2, HG_DIM), (2, HG_DIM), (DEPTH, 2, HG_DIM), (DEPTH, 3, 2 * FFN_HIDDEN // N_DEV)]
    (small_all,) = all_gather([_pack([c, hg_lb, hg_gn_g, norm_g, ffn_conv_w])], "gather_small")
    small_all = small_all.reshape(N_DEV, -1)
    at = 0
    pieces = []
    for s in small_shapes:
        n = math.prod(s)
        pieces.append(small_all[:, at:at + n].reshape((N_DEV,) + s))
        at += n
    c_all = pieces[0].reshape(N_DEV, D)
    hg_lb_full = jnp.transpose(pieces[1], (1, 0, 2)).reshape(2, D)
    hg_gn_full = jnp.transpose(pieces[2], (1, 0, 2)).reshape(2, D)
    norm_g_full = jnp.transpose(pieces[3], (1, 2, 0, 3)).reshape(DEPTH, 2, D)
    conv_w_full = jnp.transpose(pieces[4], (1, 2, 0, 3)).reshape(DEPTH, 3, 2 * FFN_HIDDEN)

    lb1 = lower_bound_fwd(hg_lb_full, "lower_bound")
    lbs = [jnp.zeros((1, D), f32), lb1]

    ada_b_cols = lax.dynamic_slice(ada_b, (0, me * ADA_COLS), (DEPTH, ADA_COLS)).reshape(DEPTH, 1, ADA_COLS)
    mod_cols = ada_mod(c_all, ada_w, ada_b_cols, "ada_mod")
    (mod_mine,) = all_to_all([jnp.transpose(mod_cols, (1, 0, 2))], "mod_to_examples")
    mod = jnp.transpose(mod_mine, (1, 0, 2)).reshape(DEPTH, 6, 1, D)

    def layer_shards(i):
        j = i // 2
        w_in, w_out = (gm_w_in, gm_w_out) if i % 2 == 0 else (hg_w_in, hg_w_out)
        return [w_in[j].astype(bf16), w_out[j].astype(bf16), ffn_w_up[i].T.astype(bf16), ffn_w_down[i].astype(bf16)]

    def full_rows(g):
        return g.reshape(N_DEV * g.shape[1], g.shape[2])

    carried_by = {
        "in_0": [(0, 1), (0, 3)], "mix_0": [(0, 2)], "up_0": [(1, 0), (1, 1)], "gate_0": [(1, 2)], "down_0": [(1, 3)],
        "in_1": [(2, 0)], "mix_1": [(2, 1), (2, 2), (2, 3)], "up_1": [(3, 0), (3, 1)], "gate_1": [(3, 2)], "down_1": [(3, 3)],
    }
    shards = [layer_shards(i) for i in range(DEPTH)]
    gathered = {}
    (gathered[(0, 0)],) = all_gather([shards[0][0]], "gather_weights_0", relay=True)

    def carry(call, site, **kw):
        items = carried_by.get(site, [])
        outs = call(riders=Riders([shards[l][slot] for l, slot in items], True), **kw)
        for item, g in zip(items, outs[len(outs) - len(items):]):
            gathered[item] = g
        return outs[:len(outs) - len(items)]

    saved = []
    weights = []
    xcur = xt
    h = norm_fwd(xcur, norm_g_full[0, 0:1], mod[0, 1], mod[0, 0], "norm1_0")
    for i in range(DEPTH):
        j = i // 2
        sh1, sc1, g1, sh2, sc2, g2 = [mod[i, p] for p in range(6)]
        gn2 = norm_g_full[i, 1:2]
        s = {"x0": xcur, "h": h}
        w_in = gathered[(i, 0)]
        if i % 2 == 0:
            (z,) = carry(functools.partial(mm_nn, h, w_in, bf16, f"gm_in_{i}"), f"in_{i}")
            bs = gm_b_s[j].reshape(GM_HEADS, GM_BLOCK, 1)
            (mixed,) = carry(functools.partial(gm_mix_fwd, z, gm_w_s[j], bs, gm_ln_g[j:j + 1], gm_ln_b[j:j + 1],
                                               f"gm_mix_{i}"), f"mix_{i}")
            s["z"] = z
        else:
            (proj,) = carry(functools.partial(mm_nn, h, w_in, f32, f"hg_in_{i}"), f"in_{i}")
            mixed, states = carry(functools.partial(hg_scan_fwd, proj, lbs[j], hg_gn_full[j:j + 1], f"hg_scan_{i}"),
                                  f"mix_{i}")
            s["proj"], s["states"] = proj, states
        s["mixed"] = mixed
        w_out = full_rows(gathered[(i, 1)])
        y, x1, h2 = carry(functools.partial(mm_nn_residual, mixed, w_out, xcur, g1, (gn2, sc2, sh2), f"mix_out_{i}"),
                          f"out_{i}")
        s["y"], s["x1"] = y, x1
        w_up = gathered[(i, 2)]
        (a,) = carry(functools.partial(mm_nn, h2, w_up, bf16, f"ffn_up_{i}", transposed=True), f"up_{i}")
        (hid,) = carry(functools.partial(ffn_gate_fwd, a, conv_w_full[i], ffn_conv_b[i:i + 1], f"ffn_gate_{i}"), f"gate_{i}")
        w_down = full_rows(gathered[(i, 3)])
        next_norm = (norm_g_full[i + 1, 0:1], mod[i + 1, 1], mod[i + 1, 0]) if i + 1 < DEPTH else None
        outs = carry(functools.partial(mm_nn_residual, hid, w_down, x1, g2, next_norm, f"ffn_down_{i}"), f"down_{i}")
        fo, x2 = outs[0], outs[1]
        s["h2"], s["a"], s["hid"], s["f"] = h2, a, hid, fo
        weights.append((w_in, w_out, w_up, w_down))
        saved.append(s)
        xcur = x2
        h = outs[2] if next_norm is not None else None

    loss_part, dx, d_final_g, dg2, df = loss_head(xcur, final_g.reshape(1, D), loss_target[0], saved[-1]["f"],
                                                  mod[DEPTH - 1, 5], "loss_head")
    loss = lax.psum(loss_part[0, 0], ("x", "y", "c"))

    def halves(blocked):
        rows = blocked.shape[1] // 2
        return [(blocked, (0, rows)), (blocked, (rows, rows))]

    def by_owner_rows(dw):
        k, n = dw.shape
        return dw.reshape(N_DEV, k // N_DEV, n)

    received = [[[] for _ in range(4)] for _ in range(DEPTH)]

    def send(call, items, **kw):
        outs = call(riders=Riders([arr for arr, _ in items], False), **kw)
        for (_, (layer, slot)), got in zip(items, outs[len(outs) - len(items):]):
            received[layer][slot].append(got)
        return outs[:len(outs) - len(items)]

    dmod = [None] * DEPTH
    d_norm_g = [None] * DEPTH
    d_gm = {k: [None, None] for k in ("ws", "bs", "lng", "lnb")}
    d_hg = {k: [None, None] for k in ("lb", "gn")}
    d_ffn = {k: [None] * DEPTH for k in ("cw", "cb")}
    in_halves = []
    for i in reversed(range(DEPTH)):
        j = i // 2
        s = saved[i]
        w_in, w_out, w_up, w_down = weights[i]
        sh1, sc1, g1, sh2, sc2, g2 = [mod[i, p] for p in range(6)]
        gn1, gn2 = norm_g_full[i, 0:1], norm_g_full[i, 1:2]
        (dw_down,) = mm_tn(s["hid"], df, bf16, f"dw_down_{i}", tn=D)
        (dhid,) = send(functools.partial(mm_nt, df, w_down, bf16, f"dhid_{i}"), in_halves[:1])
        dyc, dwg, dwv, dbg, dbv = send(
            functools.partial(ffn_gate_bwd, s["a"], conv_w_full[i], ffn_conv_b[i:i + 1], dhid, f"ffn_gate_bwd_{i}"),
            in_halves[1:] + [(by_owner_rows(dw_down), (i, 3))])
        d_ffn["cw"][i] = jnp.concatenate([dwg, dwv], axis=1)
        d_ffn["cb"][i] = jnp.concatenate([dbg, dbv], axis=1)
        da = conv_transpose(dyc, conv_w_full[i], f"conv_t_{i}")
        (dw_up_t,) = mm_tn(da, s["h2"], bf16, f"dw_up_{i}")
        up_halves = [(part, (i, 2)) for part in halves(dw_up_t.reshape(N_DEV, -1, D))]
        (dh2,) = send(functools.partial(mm_nt, da, w_up, bf16, f"dh2_{i}", transposed=True), up_halves[:1])
        dx1, dgn2, dsc2, dsh2, dg1, dy = norm_bwd(s["x1"], gn2, sc2, sh2, dh2, dx, f"norm2_bwd_{i}", gate=(s["y"], g1))
        (dw_out,) = mm_tn(s["mixed"], dy, bf16, f"dw_mix_out_{i}")
        (dmixed,) = mm_nt(dy, w_out, bf16, f"dmixed_{i}")
        if i % 2 == 0:
            bs = gm_b_s[j].reshape(GM_HEADS, GM_BLOCK, 1)
            dpre, dws, dbs, dlng, dlnb = send(
                functools.partial(gm_mix_bwd, s["z"], gm_w_s[j], bs, gm_ln_g[j:j + 1], gm_ln_b[j:j + 1], dmixed,
                                  f"gm_mix_bwd_{i}"), up_halves[1:])
            d_gm["ws"][j], d_gm["bs"][j], d_gm["lng"][j], d_gm["lnb"][j] = dws, dbs.reshape(GM_HEADS, GM_BLOCK), dlng, dlnb
        else:
            dpre, dlb, dgn = send(
                functools.partial(hg_scan_bwd, s["proj"], lbs[j], hg_gn_full[j:j + 1], s["states"], dmixed,
                                  f"hg_scan_bwd_{i}"), up_halves[1:])
            d_hg["lb"][j], d_hg["gn"][j] = dlb, dgn
        (dw_in,) = send(functools.partial(mm_tn_by_owner, s["h"], dpre, f"dw_mix_in_{i}"), [(by_owner_rows(dw_out), (i, 1))])
        in_halves = [(part, (i, 0)) for part in halves(dw_in)]
        (dh,) = send(functools.partial(mm_nt, dpre, w_in, bf16, f"dh_mix_{i}"), in_halves[:1] if i == 0 else [])
        dmod_i = [None, None, dg1, dsh2, dsc2, dg2]
        if i > 0:
            dx, dgn1, dsc1, dsh1, dg2, df = norm_bwd(s["x0"], gn1, sc1, sh1, dh, dx1, f"norm1_bwd_{i}",
                                                     gate=(saved[i - 1]["f"], mod[i - 1, 5]))
        else:
            dx, dgn1, dsc1, dsh1 = send(functools.partial(norm_bwd, s["x0"], gn1, sc1, sh1, dh, dx1, f"norm1_bwd_{i}"),
                                        in_halves[1:])
        dmod_i[0], dmod_i[1] = dsh1, dsc1
        dmod[i] = jnp.concatenate(dmod_i, axis=1)
        d_norm_g[i] = jnp.concatenate([dgn1, dgn2], axis=0)
    grad_x = dx.reshape(1, t, D)

    (dmod_all,) = all_gather([jnp.concatenate(dmod, axis=0)], "gather_dmod")
    dmod_cols = jnp.transpose(lax.dynamic_slice(dmod_all, (0, 0, me * ADA_COLS), (N_DEV, DEPTH, ADA_COLS)), (1, 0, 2))
    g_ada_w, g_ada_b = ada_grads(c_all, dmod_cols, dmod_all.reshape(N_DEV, DEPTH, 1, 6 * D), "ada_grads")
    g_ada_b = g_ada_b.reshape(DEPTH, 6 * D)

    small_partials = [jnp.concatenate(d_gm["lng"], axis=0), jnp.concatenate(d_gm["lnb"], axis=0),
                      jnp.stack(d_gm["ws"]), jnp.stack(d_gm["bs"]), jnp.concatenate(d_ffn["cb"], axis=0),
                      d_final_g, d_hg["lb"][1], jnp.concatenate(d_hg["gn"], axis=0), jnp.stack(d_norm_g),
                      jnp.stack(d_ffn["cw"])]
    partial_shapes = [p.shape for p in small_partials]
    packed = _pack(small_partials, rows_mult=8 * N_DEV)
    rows = packed.shape[0] // N_DEV
    (recv,) = all_to_all([packed.reshape(N_DEV, rows, LANES)], "small_grads_exchange")
    (summed,) = all_gather([sum_parts(recv, "small_grads_sum")], "small_grads_gather")

    def parts_of(slot, layers):
        return [received[i][slot] for i in layers]

    def swapped(a):
        return jnp.swapaxes(a, 1, 2)

    w_shards = [gm_w_in, gm_w_out, hg_w_in, hg_w_out, swapped(ffn_w_up), ffn_w_down]
    big_parts = [parts_of(0, (0, 2)), parts_of(1, (0, 2)), parts_of(0, (1, 3)), parts_of(1, (1, 3)),
                 parts_of(2, range(DEPTH)), parts_of(3, range(DEPTH))]
    big_m = [m_gm_w_in, m_gm_w_out, m_hg_w_in, m_hg_w_out, swapped(m_ffn_w_up), m_ffn_w_down]
    big_v = [v_gm_w_in, v_gm_w_out, v_hg_w_in, v_hg_w_out, swapped(v_ffn_w_up), v_ffn_w_down]
    big = [adam_reduced(parts, w, m_, v_, f"adam_big_{idx}")
           for idx, (w, m_, v_, parts) in enumerate(zip(w_shards, big_m, big_v, big_parts))]
    big[4] = [swapped(o) for o in big[4]]
    (g_gm_w_in, d_gm_w_in, nm_gm_w_in, nv_gm_w_in), (g_gm_w_out, d_gm_w_out, nm_gm_w_out, nv_gm_w_out), \
        (g_hg_w_in, d_hg_w_in, nm_hg_w_in, nv_hg_w_in), (g_hg_w_out, d_hg_w_out, nm_hg_w_out, nv_hg_w_out), \
        (g_ffn_w_up, d_ffn_w_up, nm_ffn_w_up, nv_ffn_w_up), (g_ffn_w_down, d_ffn_w_down, nm_ffn_w_down, nv_ffn_w_down) = big

    g_ln_g, g_ln_b, g_ws, g_bs, g_cb, g_final, g_lb1, g_gn, g_norm, g_cw = _unpack(summed.reshape(-1), partial_shapes)
    g_final = g_final.reshape(D)

    def my_cols(a, n):
        start = (0,) * (a.ndim - 1) + (me * n,)
        return lax.dynamic_slice(a, start, a.shape[:-1] + (n,))

    g_hg_lb = lower_bound_bwd(hg_lb, my_cols(g_lb1, HG_DIM), "lower_bound_bwd")
    g_hg_gn = my_cols(g_gn, HG_DIM)
    g_norm_g = my_cols(g_norm, HG_DIM)
    g_conv_w = my_cols(g_cw, 2 * FFN_HIDDEN // N_DEV)

    two_d = (-1, ADA_COLS)
    d_ada_w, nm_ada_w, nv_ada_w = [o.reshape(ada_w.shape) for o in adam_plain(
        g_ada_w.reshape(two_d), ada_w.reshape(two_d), m_ada_w.reshape(two_d), v_ada_w.reshape(two_d), "adam_ada_w")]

    small_g = [g_ln_g, g_ln_b, g_ws, g_bs, g_cb, g_ada_b, g_final, g_hg_lb, g_hg_gn, g_norm_g, g_conv_w]
    small_w = [gm_ln_g, gm_ln_b, gm_w_s, gm_b_s, ffn_conv_b, ada_b, final_g, hg_lb, hg_gn_g, norm_g, ffn_conv_w]
    small_m = [m_gm_ln_g, m_gm_ln_b, m_gm_w_s, m_gm_b_s, m_ffn_conv_b, m_ada_b, m_final_g, m_hg_lb, m_hg_gn_g, m_norm_g, m_ffn_conv_w]
    small_v = [v_gm_ln_g, v_gm_ln_b, v_gm_w_s, v_gm_b_s, v_ffn_conv_b, v_ada_b, v_final_g, v_hg_lb, v_hg_gn_g, v_norm_g, v_ffn_conv_w]
    shapes = [w.shape for w in small_w]
    small_g = [g.reshape(s) for g, s in zip(small_g, shapes)]
    outs = adam_plain(_pack(small_g), _pack(small_w), _pack(small_m), _pack(small_v), "adam_small")
    (d_ln_g, d_ln_b, d_ws, d_bs, d_cb, d_ada_b, d_final, d_hg_lb, d_hg_gn, d_norm_g_, d_conv_w), \
        (nm_ln_g, nm_ln_b, nm_ws, nm_bs, nm_cb, nm_ada_b, nm_final, nm_hg_lb, nm_hg_gn, nm_norm_g, nm_conv_w), \
        (nv_ln_g, nv_ln_b, nv_ws, nv_bs, nv_cb, nv_ada_b, nv_final, nv_hg_lb, nv_hg_gn, nv_norm_g, nv_conv_w) = [
            _unpack(o.reshape(-1), shapes) for o in outs]
    g_ln_g, g_ln_b, g_ws, g_bs, g_cb, g_ada_b, g_final, g_hg_lb, g_hg_gn, g_norm_g, g_conv_w = small_g

    grads = (g_gm_w_in, g_ln_g, g_ln_b, g_ws, g_bs, g_gm_w_out, g_hg_w_in, g_hg_lb, g_hg_gn, g_hg_w_out,
             g_ffn_w_up, g_conv_w, g_cb, g_ffn_w_down, g_norm_g, g_ada_w, g_ada_b, g_final)
    deltas = (d_gm_w_in, d_ln_g, d_ln_b, d_ws, d_bs, d_gm_w_out, d_hg_w_in, d_hg_lb, d_hg_gn, d_hg_w_out,
              d_ffn_w_up, d_conv_w, d_cb, d_ffn_w_down, d_norm_g_, d_ada_w, d_ada_b, d_final)
    new_m = (nm_gm_w_in, nm_ln_g, nm_ln_b, nm_ws, nm_bs, nm_gm_w_out, nm_hg_w_in, nm_hg_lb, nm_hg_gn, nm_hg_w_out,
             nm_ffn_w_up, nm_conv_w, nm_cb, nm_ffn_w_down, nm_norm_g, nm_ada_w, nm_ada_b, nm_final)
    new_v = (nv_gm_w_in, nv_ln_g, nv_ln_b, nv_ws, nv_bs, nv_gm_w_out, nv_hg_w_in, nv_hg_lb, nv_hg_gn, nv_hg_w_out,
             nv_ffn_w_up, nv_conv_w, nv_cb, nv_ffn_w_down, nv_norm_g, nv_ada_w, nv_ada_b, nv_final)
    return (loss, grad_x) + grads + deltas + new_m + new_v
```

```python
import functools
import math

import jax
import jax.numpy as jnp
from jax import lax
from jax.experimental import pallas as pl
from jax.experimental.pallas import tpu as pltpu

f32 = jnp.float32
bf16 = jnp.bfloat16
MESH = pl.DeviceIdType.MESH

N_DEV = 8
D = 1024
DEPTH = 4
EPS = 1e-6
GM_WIDTH = 2048
GM_HEADS = 8
GM_HEAD_DIM = 256
GM_BLOCK = 128
CHUNK = 64
HG_HEADS = 8
HG_DIM = 128
FFN_HIDDEN = 2816
ADA_COLS = 6 * D // N_DEV

HG_SUB = 32
HG_PAIR = 8
HG_TOKENS = 128

ADAM_LR = 0.001
ADAM_B1 = 0.9
ADAM_B2 = 0.999
ADAM_EPS = 1e-08
ADAM_WD = 0.01
ADAM_STEP = 10

V7X_VMEM_LIMIT = 56 * 1024 * 1024
LANES = 128


def _cparams(*sem):
    return pltpu.CompilerParams(dimension_semantics=sem or None, vmem_limit_bytes=V7X_VMEM_LIMIT)


def _tile(n, target, mult=LANES):
    best = None
    for t in range(mult, min(n, target) + 1, mult):
        if n % t == 0:
            best = t
    return best or n


WEIGHT_BLOCK_BYTES = 6 * 1024 * 1024


def _weight_tile(n, k):
    return _tile(n, max(LANES, WEIGHT_BLOCK_BYTES // (2 * k)))


def _gelu(x):
    return 0.5 * x * (1.0 + lax.erf(x * (1.0 / math.sqrt(2.0))))


def _mesh_pos():
    return lax.axis_index("x"), lax.axis_index("y"), lax.axis_index("c")


def _flat(pos):
    return 4 * pos[0] + 2 * pos[1] + pos[2]


def _peer(pos, k):
    return ((1 - pos[0]) if k & 4 else pos[0], (1 - pos[1]) if k & 2 else pos[1], (1 - pos[2]) if k & 1 else pos[2])


def _exchange_copies(ins, outs, send_sems, recv_sems, local_sems, gather):
    pos = _mesh_pos()
    me = _flat(pos)

    def src(i, dest):
        if gather:
            return ins[i]
        ref, rows = ins[i] if isinstance(ins[i], tuple) else (ins[i], None)
        return ref.at[dest] if rows is None else ref.at[dest, pl.ds(*rows)]

    local = [pltpu.make_async_copy(src(i, me), outs[i].at[me], local_sems.at[i]) for i in range(len(ins))]
    sends, recvs = [], []
    for k in range(1, N_DEV):
        peer = _peer(pos, k)
        there = _flat(peer)
        for i in range(len(ins)):
            sems = dict(send_sem=send_sems.at[i * 7 + k - 1], recv_sem=recv_sems.at[i * 7 + k - 1],
                        device_id=peer, device_id_type=MESH)
            sends.append(pltpu.make_async_remote_copy(src_ref=src(i, there), dst_ref=outs[i].at[me], **sems))
            recvs.append(pltpu.make_async_remote_copy(src_ref=src(i, there), dst_ref=outs[i].at[there], **sems))
    return local, sends, recvs


def _exchange_start(*refs):
    local, sends, _ = _exchange_copies(*refs)
    for cp in local + sends:
        cp.start()


def _exchange_wait(*refs):
    local, sends, recvs = _exchange_copies(*refs)
    for cp in recvs:
        cp.wait_recv()
    for cp in sends:
        cp.wait_send()
    for cp in local:
        cp.wait()


OTHER_CHIPS = (2, 4, 6)


def _relay_copies(ins, outs, send_sems, recv_sems, local_sems):
    pos = _mesh_pos()
    me = _flat(pos)
    sibling = _peer(pos, 1)
    local = [pltpu.make_async_copy(ins[i], outs[i].at[me], local_sems.at[i]) for i in range(len(ins))]
    first, passes, recvs = [], {k: [] for k in OTHER_CHIPS}, {k: [] for k in range(1, N_DEV)}
    for i in range(len(ins)):
        def copy(k, src, block, to):
            return pltpu.make_async_remote_copy(
                src_ref=src, dst_ref=outs[i].at[block], send_sem=send_sems.at[i * 7 + k - 1],
                recv_sem=recv_sems.at[i * 7 + k - 1], device_id=to, device_id_type=MESH)

        for k in (1,) + OTHER_CHIPS:
            first.append(copy(k, ins[i], me, _peer(pos, k)))
        for k in OTHER_CHIPS:
            there = _flat(_peer(pos, k))
            passes[k].append(copy(k ^ 1, outs[i].at[there], there, sibling))
        for k in range(1, N_DEV):
            there = _flat(_peer(pos, k))
            recvs[k].append(copy(k, ins[i], there, _peer(pos, k)))
    return local, first, passes, recvs


def _relay_start(ins, outs, *sems):
    local, first, _, _ = _relay_copies(ins, outs, *sems)
    for cp in local + first:
        cp.start()


def _relay_wait(ins, outs, *sems):
    local, first, passes, recvs = _relay_copies(ins, outs, *sems)
    for k in OTHER_CHIPS:
        for cp in recvs[k]:
            cp.wait_recv()
        for cp in passes[k]:
            cp.start()
    for k in (1, 3, 5, 7):
        for cp in recvs[k]:
            cp.wait_recv()
    for cp in first + [cp for k in OTHER_CHIPS for cp in passes[k]]:
        cp.wait_send()
    for cp in local:
        cp.wait()


def _exchange_out_shape(a, gather):
    return jax.ShapeDtypeStruct((N_DEV,) + tuple(a.shape) if gather else tuple(a.shape), a.dtype)


def _exchange_sems(n):
    return [pltpu.SemaphoreType.DMA((7 * n,)), pltpu.SemaphoreType.DMA((7 * n,)), pltpu.SemaphoreType.DMA((n,))]


ANY = pl.BlockSpec(memory_space=pl.ANY)


def _exchange(arrs, gather, name, relay=False):
    n = len(arrs)

    def body(*refs):
        ins, outs = refs[:n], refs[n:2 * n]
        if relay:
            _relay_start(ins, outs, *refs[2 * n:])
            _relay_wait(ins, outs, *refs[2 * n:])
        else:
            _exchange_start(ins, outs, *refs[2 * n:], gather)
            _exchange_wait(ins, outs, *refs[2 * n:], gather)

    return pl.pallas_call(
        body, name=name, out_shape=tuple(_exchange_out_shape(a, gather) for a in arrs),
        in_specs=[ANY] * n, out_specs=tuple([ANY] * n), scratch_shapes=_exchange_sems(n),
    )(*arrs)


def all_gather(arrs, name, relay=False):
    return _exchange(arrs, True, name, relay)


def all_to_all(arrs, name):
    return _exchange(arrs, False, name)


class Riders:
    def __init__(self, arrs, gather):
        self.gather = gather
        self.rows = [a[1] if isinstance(a, tuple) else None for a in arrs]
        self.arrs = [a[0] if isinstance(a, tuple) else a for a in arrs]

    def out_shapes(self):
        shapes = []
        for a, rows in zip(self.arrs, self.rows):
            shape = tuple(a.shape) if rows is None else (a.shape[0], rows[1], a.shape[2])
            shapes.append(jax.ShapeDtypeStruct((N_DEV,) + shape if self.gather else shape, a.dtype))
        return shapes


def _call(body, *, name, grid, in_specs, out_specs, out_shape, semantics, scratch_shapes=(), riders=None):
    if riders is None or not riders.arrs:
        return pl.pallas_call(body, name=name, grid=grid, in_specs=in_specs, out_specs=tuple(out_specs),
                              out_shape=tuple(out_shape), scratch_shapes=list(scratch_shapes),
                              compiler_params=_cparams(*semantics))
    n_in, n_out, n_scr, n_r = len(in_specs), len(out_specs), len(scratch_shapes), len(riders.arrs)
    gather = riders.gather

    def hosted(*refs):
        ins, r_ins = refs[:n_in], refs[n_in:n_in + n_r]
        at = n_in + n_r
        outs, r_outs = refs[at:at + n_out], refs[at + n_out:at + n_out + n_r]
        at += n_out + n_r
        scratch, sems = refs[at:at + n_scr], refs[at + n_scr:]
        first = functools.reduce(jnp.logical_and, [pl.program_id(a) == 0 for a in range(len(grid))])
        last = functools.reduce(jnp.logical_and, [pl.program_id(a) == grid[a] - 1 for a in range(len(grid))])

        r_ins = [(ref, rows) if rows is not None else ref for ref, rows in zip(r_ins, riders.rows)]

        @pl.when(first)
        def _():
            if gather:
                _relay_start(r_ins, r_outs, *sems)
            else:
                _exchange_start(r_ins, r_outs, *sems, gather)

        body(*ins, *outs, *scratch)

        @pl.when(last)
        def _():
            if gather:
                _relay_wait(r_ins, r_outs, *sems)
            else:
                _exchange_wait(r_ins, r_outs, *sems, gather)

    call = pl.pallas_call(
        hosted, name=name, grid=grid, in_specs=list(in_specs) + [ANY] * n_r, out_specs=tuple(out_specs) + (ANY,) * n_r,
        out_shape=tuple(out_shape) + tuple(riders.out_shapes()),
        scratch_shapes=list(scratch_shapes) + _exchange_sems(n_r),
        compiler_params=_cparams(*(("arbitrary",) * len(grid))))
    return lambda *args: call(*args, *riders.arrs)


def _shards_per_step(shape):
    _, k, n = shape
    best = None
    for q in (1, 2, 4, 8):
        if (q * n) % LANES == 0 and (best is None or 2 * k * q * n <= WEIGHT_BLOCK_BYTES):
            best = q
    return best


def mm_nn(a, b, out_dtype, name, tm=512, riders=None, transposed=False):
    m, k = a.shape
    tm = _tile(m, tm, 8)
    if b.ndim == 3:
        shard = b.shape[1] if transposed else b.shape[2]
        n = N_DEV * shard
        per_step = _shards_per_step((N_DEV, k, shard))
        tn = per_step * shard
        b_spec = pl.BlockSpec((per_step,) + b.shape[1:], lambda i, j: (j, 0, 0))
        contract = (((1,), (1,)), ((), ())) if transposed else (((1,), (0,)), ((), ()))

        def body(a_ref, b_ref, o_ref):
            for q in range(per_step):
                o_ref[:, q * shard:(q + 1) * shard] = lax.dot_general(
                    a_ref[...], b_ref[q], contract, preferred_element_type=f32).astype(o_ref.dtype)
    else:
        n = b.shape[1]
        tn = _weight_tile(n, k)
        b_spec = pl.BlockSpec((k, tn), lambda i, j: (0, j))

        def body(a_ref, b_ref, o_ref):
            o_ref[...] = jnp.dot(a_ref[...], b_ref[...], preferred_element_type=f32).astype(o_ref.dtype)

    return _call(
        body, name=name, grid=(m // tm, n // tn),
        in_specs=[pl.BlockSpec((tm, k), lambda i, j: (i, 0)), b_spec],
        out_specs=[pl.BlockSpec((tm, tn), lambda i, j: (i, j))],
        out_shape=[jax.ShapeDtypeStruct((m, n), out_dtype)], semantics=("parallel", "parallel"), riders=riders,
    )(a, b)


def mm_nn_residual(a, b, x, gate, norm, name, tm=512, riders=None):
    m, k = a.shape
    n = b.shape[1]
    tm = _tile(m, tm, 8)

    def body(a_ref, b_ref, x_ref, g_ref, *rest):
        if norm is not None:
            gn_ref, sc_ref, sh_ref, y_ref, o_ref, h_ref = rest
        else:
            y_ref, o_ref = rest
        y = jnp.dot(a_ref[...], b_ref[...], preferred_element_type=f32)
        y_ref[...] = y.astype(bf16)
        x_new = x_ref[...] + g_ref[...] * y
        o_ref[...] = x_new
        if norm is not None:
            h_ref[...] = _norm_fn(x_new, gn_ref[...], sc_ref[...], sh_ref[...]).astype(bf16)

    blk = pl.BlockSpec((tm, n), lambda i: (i, 0))
    in_specs = [pl.BlockSpec((tm, k), lambda i: (i, 0)), pl.BlockSpec((k, n), lambda i: (0, 0)), blk, _row(n)]
    out_specs = [blk, blk]
    out_shape = [jax.ShapeDtypeStruct((m, n), bf16), jax.ShapeDtypeStruct((m, n), f32)]
    args = [a, b, x, gate]
    if norm is not None:
        in_specs += [_row(n)] * 3
        out_specs += [blk]
        out_shape += [jax.ShapeDtypeStruct((m, n), bf16)]
        args += list(norm)
    return _call(body, name=name, grid=(m // tm,), in_specs=in_specs, out_specs=out_specs, out_shape=out_shape,
                 semantics=("parallel",), riders=riders)(*args)


def mm_nt(a, b, out_dtype, name, tm=512, riders=None, transposed=False):
    m = a.shape[0]
    tm = _tile(m, tm, 8)
    if b.ndim == 3:
        shard, k = (b.shape[1], b.shape[2]) if transposed else (b.shape[2], b.shape[1])
        tk = k
        b_spec = pl.BlockSpec(b.shape, lambda i, j: (0, 0, 0))
        width = N_DEV * shard
        contract = (((1,), (0,)), ((), ())) if transposed else (((1,), (1,)), ((), ()))

        def body(a_ref, b_ref, o_ref):
            acc = None
            for q in range(N_DEV):
                part = lax.dot_general(a_ref[:, q * shard:(q + 1) * shard], b_ref[q], contract, preferred_element_type=f32)
                acc = part if acc is None else acc + part
            o_ref[...] = acc.astype(o_ref.dtype)
    else:
        k, width = b.shape
        tk = _weight_tile(k, width)
        b_spec = pl.BlockSpec((tk, width), lambda i, j: (j, 0))

        def body(a_ref, b_ref, o_ref):
            o_ref[...] = lax.dot_general(a_ref[...], b_ref[...], (((1,), (1,)), ((), ())),
                                         preferred_element_type=f32).astype(o_ref.dtype)

    return _call(
        body, name=name, grid=(m // tm, k // tk),
        in_specs=[pl.BlockSpec((tm, width), lambda i, j: (i, 0)), b_spec],
        out_specs=[pl.BlockSpec((tm, tk), lambda i, j: (i, j))],
        out_shape=[jax.ShapeDtypeStruct((m, k), out_dtype)], semantics=("parallel", "parallel"), riders=riders,
    )(a, b)


def mm_tn(a, b, out_dtype, name, tm=512, tn=512, riders=None):
    t, m = a.shape
    n = b.shape[1]
    tm, tn = _tile(m, tm), _tile(n, tn)

    def body(a_ref, b_ref, o_ref):
        o_ref[...] = lax.dot_general(a_ref[...], b_ref[...], (((0,), (0,)), ((), ())),
                                     preferred_element_type=f32).astype(o_ref.dtype)

    return _call(
        body, name=name, grid=(m // tm, n // tn),
        in_specs=[pl.BlockSpec((t, tm), lambda i, j: (0, i)), pl.BlockSpec((t, tn), lambda i, j: (0, j))],
        out_specs=[pl.BlockSpec((tm, tn), lambda i, j: (i, j))],
        out_shape=[jax.ShapeDtypeStruct((m, n), out_dtype)], semantics=("parallel", "parallel"), riders=riders,
    )(a, b)


def mm_tn_by_owner(a, b, name, tm=512, riders=None):
    t, m = a.shape
    n = b.shape[1]
    shard = n // N_DEV
    per_step = 1 if shard % LANES == 0 else 2
    assert (per_step * shard) % LANES == 0
    tm = _tile(m, tm)

    def body(a_ref, b_ref, o_ref):
        acc = lax.dot_general(a_ref[...], b_ref[...], (((0,), (0,)), ((), ())), preferred_element_type=f32)
        for q in range(per_step):
            o_ref[q] = acc[:, q * shard:(q + 1) * shard].astype(bf16)

    return _call(
        body, name=name, grid=(m // tm, N_DEV // per_step),
        in_specs=[pl.BlockSpec((t, tm), lambda i, j: (0, i)), pl.BlockSpec((t, per_step * shard), lambda i, j: (0, j))],
        out_specs=[pl.BlockSpec((per_step, tm, shard), lambda i, j: (j, i, 0))],
        out_shape=[jax.ShapeDtypeStruct((N_DEV, m, shard), bf16)], semantics=("parallel", "parallel"), riders=riders,
    )(a, b)


def _norm_fn(x, gn, sc, sh):
    r = lax.rsqrt(jnp.mean(x * x, axis=-1, keepdims=True) + EPS)
    return (x * r * gn) * (1.0 + sc) + sh


def _row(d):
    return pl.BlockSpec((1, d), lambda i: (0, 0))


def norm_fwd(x, gn, sc, sh, name, tm=512):
    t, d = x.shape
    tm = _tile(t, tm, 8)

    def body(x_ref, gn_ref, sc_ref, sh_ref, h_ref):
        h_ref[...] = _norm_fn(x_ref[...], gn_ref[...], sc_ref[...], sh_ref[...]).astype(bf16)

    return pl.pallas_call(
        body, name=name, grid=(t // tm,),
        in_specs=[pl.BlockSpec((tm, d), lambda i: (i, 0)), _row(d), _row(d), _row(d)],
        out_specs=pl.BlockSpec((tm, d), lambda i: (i, 0)),
        out_shape=jax.ShapeDtypeStruct((t, d), bf16), compiler_params=_cparams("parallel"),
    )(x, gn, sc, sh)


def _gate_bwd(dx, y_ref, g_ref, dgate_ref, dy_ref):
    dgate_ref[...] += jnp.sum(dx * y_ref[...].astype(f32), axis=0, keepdims=True)
    dy_ref[...] = (dx * g_ref[...]).astype(bf16)


def norm_bwd(x, gn, sc, sh, dh, dres, name, gate=None, tm=512, riders=None):
    t, d = x.shape
    tm = _tile(t, tm, 8)

    def body(x_ref, gn_ref, sc_ref, sh_ref, dh_ref, dres_ref, *rest):
        if gate is not None:
            y_ref, g_ref, dx_ref, dgn_ref, dsc_ref, dsh_ref, dgate_ref, dy_ref = rest
        else:
            dx_ref, dgn_ref, dsc_ref, dsh_ref = rest

        @pl.when(pl.program_id(0) == 0)
        def _():
            dgn_ref[...] = jnp.zeros_like(dgn_ref)
            dsc_ref[...] = jnp.zeros_like(dsc_ref)
            dsh_ref[...] = jnp.zeros_like(dsh_ref)
            if gate is not None:
                dgate_ref[...] = jnp.zeros_like(dgate_ref)

        _, vjp = jax.vjp(_norm_fn, x_ref[...], gn_ref[...], sc_ref[...], sh_ref[...])
        dx, dgn, dsc, dsh = vjp(dh_ref[...].astype(f32))
        dx = dx + dres_ref[...]
        dx_ref[...] = dx
        dgn_ref[...] += dgn
        dsc_ref[...] += dsc
        dsh_ref[...] += dsh
        if gate is not None:
            _gate_bwd(dx, y_ref, g_ref, dgate_ref, dy_ref)

    blk = pl.BlockSpec((tm, d), lambda i: (i, 0))
    vec = jax.ShapeDtypeStruct((1, d), f32)
    in_specs = [blk, _row(d), _row(d), _row(d), blk, blk]
    out_specs = [blk, _row(d), _row(d), _row(d)]
    out_shape = [jax.ShapeDtypeStruct((t, d), f32), vec, vec, vec]
    args = [x, gn, sc, sh, dh, dres]
    if gate is not None:
        in_specs += [blk, _row(d)]
        out_specs += [_row(d), blk]
        out_shape += [vec, jax.ShapeDtypeStruct((t, d), bf16)]
        args += list(gate)
    return _call(body, name=name, grid=(t // tm,), in_specs=in_specs, out_specs=out_specs, out_shape=out_shape,
                 semantics=("arbitrary",), riders=riders)(*args)


def _loss_fn(x, g, tgt):
    r = lax.rsqrt(jnp.mean(x * x, axis=-1, keepdims=True) + EPS)
    err = jnp.square(x * r * g - tgt)
    return 0.5 * jnp.sum(jnp.mean(err, axis=-1, keepdims=True), axis=0, keepdims=True)


def loss_head(x, g, tgt, y, gate, name, tm=512):
    t, d = x.shape
    tm = _tile(t, tm, 8)

    def body(x_ref, g_ref, t_ref, y_ref, gate_ref, loss_ref, dx_ref, dg_ref, dgate_ref, dy_ref):
        @pl.when(pl.program_id(0) == 0)
        def _():
            loss_ref[...] = jnp.zeros_like(loss_ref)
            dg_ref[...] = jnp.zeros_like(dg_ref)
            dgate_ref[...] = jnp.zeros_like(dgate_ref)

        loss, vjp = jax.vjp(_loss_fn, x_ref[...], g_ref[...], t_ref[...])
        dx, dg, _ = vjp(jnp.ones((1, 1), f32))
        dx_ref[...] = dx
        loss_ref[...] += loss
        dg_ref[...] += dg
        _gate_bwd(dx, y_ref, gate_ref, dgate_ref, dy_ref)

    blk = pl.BlockSpec((tm, d), lambda i: (i, 0))
    vec = jax.ShapeDtypeStruct((1, d), f32)
    return pl.pallas_call(
        body, name=name, grid=(t // tm,),
        in_specs=[blk, _row(d), blk, blk, _row(d)],
        out_specs=(pl.BlockSpec((1, 1), lambda i: (0, 0)), blk, _row(d), _row(d), blk),
        out_shape=(jax.ShapeDtypeStruct((1, 1), f32), jax.ShapeDtypeStruct((t, d), f32), vec, vec,
                   jax.ShapeDtypeStruct((t, d), bf16)),
        compiler_params=_cparams("arbitrary"),
    )(x, g, tgt, y, gate)


def _gm_block_fn(z, ws, bs, lng, lnb):
    u = _gelu(z[:, :GM_WIDTH])
    vg = _gelu(z[:, GM_WIDTH:])
    mu = jnp.mean(vg, axis=-1, keepdims=True)
    var = jnp.mean(jnp.square(vg - mu), axis=-1, keepdims=True)
    vn = (vg - mu) * lax.rsqrt(var + EPS) * lng + lnb
    row = lax.broadcasted_iota(jnp.int32, (GM_BLOCK, GM_BLOCK), 0) // CHUNK
    col = lax.broadcasted_iota(jnp.int32, (GM_BLOCK, GM_BLOCK), 1) // CHUNK
    parts = []
    for h in range(GM_HEADS):
        w = jnp.where(row >= col, ws[h], 0.0)
        cols = slice(h * GM_HEAD_DIM, (h + 1) * GM_HEAD_DIM)
        s = jnp.dot(w.astype(bf16), vn[:, cols].astype(bf16), preferred_element_type=f32) + bs[h]
        parts.append(u[:, cols] * s)
    return jnp.concatenate(parts, axis=1)


def _gm_param_specs():
    return [pl.BlockSpec((GM_HEADS, GM_BLOCK, GM_BLOCK), lambda i: (0, 0, 0)),
            pl.BlockSpec((GM_HEADS, GM_BLOCK, 1), lambda i: (0, 0, 0)), _row(GM_WIDTH), _row(GM_WIDTH)]


def gm_mix_fwd(z, ws, bs, lng, lnb, name, riders=None):
    t = z.shape[0]

    def body(z_ref, ws_ref, bs_ref, lng_ref, lnb_ref, o_ref):
        o_ref[...] = _gm_block_fn(z_ref[...].astype(f32), ws_ref[...], bs_ref[...], lng_ref[...],
                                  lnb_ref[...]).astype(bf16)

    return _call(
        body, name=name, grid=(t // GM_BLOCK,),
        in_specs=[pl.BlockSpec((GM_BLOCK, 2 * GM_WIDTH), lambda i: (i, 0))] + _gm_param_specs(),
        out_specs=[pl.BlockSpec((GM_BLOCK, GM_WIDTH), lambda i: (i, 0))],
        out_shape=[jax.ShapeDtypeStruct((t, GM_WIDTH), bf16)], semantics=("parallel",), riders=riders,
    )(z, ws, bs, lng, lnb)


def gm_mix_bwd(z, ws, bs, lng, lnb, dgated, name, riders=None):
    t = z.shape[0]

    def body(z_ref, ws_ref, bs_ref, lng_ref, lnb_ref, dg_ref, dz_ref, dws_ref, dbs_ref, dlng_ref, dlnb_ref):
        _, vjp = jax.vjp(_gm_block_fn, z_ref[...].astype(f32), ws_ref[...], bs_ref[...], lng_ref[...], lnb_ref[...])
        dz, dws, dbs, dlng, dlnb = vjp(dg_ref[...].astype(f32))
        dz_ref[...] = dz.astype(bf16)

        @pl.when(pl.program_id(0) == 0)
        def _():
            dws_ref[...] = jnp.zeros_like(dws_ref)
            dbs_ref[...] = jnp.zeros_like(dbs_ref)
            dlng_ref[...] = jnp.zeros_like(dlng_ref)
            dlnb_ref[...] = jnp.zeros_like(dlnb_ref)

        dws_ref[...] += dws
        dbs_ref[...] += dbs
        dlng_ref[...] += dlng
        dlnb_ref[...] += dlnb

    zblk = pl.BlockSpec((GM_BLOCK, 2 * GM_WIDTH), lambda i: (i, 0))
    return _call(
        body, name=name, grid=(t // GM_BLOCK,),
        in_specs=[zblk] + _gm_param_specs() + [pl.BlockSpec((GM_BLOCK, GM_WIDTH), lambda i: (i, 0))],
        out_specs=[zblk] + _gm_param_specs(),
        out_shape=[jax.ShapeDtypeStruct((t, 2 * GM_WIDTH), bf16),
                   jax.ShapeDtypeStruct((GM_HEADS, GM_BLOCK, GM_BLOCK), f32),
                   jax.ShapeDtypeStruct((GM_HEADS, GM_BLOCK, 1), f32),
                   jax.ShapeDtypeStruct((1, GM_WIDTH), f32), jax.ShapeDtypeStruct((1, GM_WIDTH), f32)],
        semantics=("arbitrary",), riders=riders,
    )(z, ws, bs, lng, lnb, dgated)


@functools.partial(jax.custom_vjp, nondiff_argnums=(1,))
def _rows_up(x, shift):
    return x if shift == 0 else pltpu.roll(x, x.shape[1] - shift, axis=1)


def _rows_up_fwd(x, shift):
    return _rows_up(x, shift), None


def _rows_up_bwd(shift, _, g):
    return (g if shift == 0 else pltpu.roll(g, shift, axis=1),)


_rows_up.defvjp(_rows_up_fwd, _rows_up_bwd)


def _hg_block_fn(qp, fz, iv, gp, s0, lb, gn):
    n, ns, d = HG_SUB, HG_TOKENS // HG_SUB, HG_DIM
    p, nb, per_sub = HG_PAIR, HG_TOKENS // HG_PAIR, HG_SUB // HG_PAIR
    f = lb + (1.0 - lb) * jax.nn.sigmoid(fz)
    g = jnp.log(f)
    k = 1.0 - f
    q = qp * jax.nn.sigmoid(qp)
    v = iv.astype(bf16)
    row = lax.broadcasted_iota(jnp.int32, (HG_TOKENS, HG_TOKENS), 0)
    col = lax.broadcasted_iota(jnp.int32, (HG_TOKENS, HG_TOKENS), 1)
    same_sub = col // n == row // n
    tri = ((col <= row) & same_sub).astype(f32)
    cum = jnp.dot(tri, g, precision=lax.Precision.HIGHEST, preferred_element_type=f32)
    cum_b, q_b, k_b, f_b = cum.reshape(nb, p, d), q.reshape(nb, p, d), k.reshape(nb, p, d), f.reshape(nb, p, d)
    j_b = lax.broadcasted_iota(jnp.int32, (nb, p, d), 1)
    j_col = lax.broadcasted_iota(jnp.int32, (nb, p, 1), 1)
    scores_t = jnp.zeros((HG_TOKENS, HG_TOKENS), f32)
    weight = k_b
    for delta in range(p):
        if delta:
            weight = weight * _rows_up(f_b, delta)
        pair = jnp.sum(_rows_up(q_b, delta) * weight, axis=2, keepdims=True)
        pair = jnp.where(j_col < p - delta, pair, 0.0)
        scores_t = scores_t + jnp.where(col == row + delta, pair.reshape(HG_TOKENS, 1), 0.0)
    o = lax.dot_general(scores_t.astype(bf16), v, (((0,), (0,)), ((), ())), preferred_element_type=f32)
    last = cum_b[:, p - 1:p, :]
    before = jnp.concatenate([jnp.zeros((1, 1, d), f32), last[:-1]], axis=0)
    before = jnp.broadcast_to(before, (nb, p, d)).reshape(HG_TOKENS, d)
    block = (lax.broadcasted_iota(jnp.int32, (HG_TOKENS, d), 0) // p) % per_sub
    q_late = q * jnp.exp(jnp.where(block > 0, cum - before, -1e30))
    last_s = last.reshape(ns, per_sub, d)
    q_parts, k_parts = [], []
    for m in range(1, per_sub):
        split = jnp.broadcast_to(last_s[:, m - 1:m, :], (ns, n, d)).reshape(HG_TOKENS, d)
        k_parts.append(k * jnp.exp(jnp.where(block < m, split - cum, -1e30)))
        q_parts.append(jnp.where(block == m, q_late, 0.0))
    scores = lax.dot_general(jnp.concatenate(q_parts, axis=1).astype(bf16), jnp.concatenate(k_parts, axis=1).astype(bf16),
                             (((1,), (1,)), ((), ())), preferred_element_type=f32)
    o = o + jnp.dot(jnp.where(same_sub, scores, 0.0).astype(bf16), v, preferred_element_type=f32)
    cum_s = cum.reshape(ns, n, d)
    tot = cum_s[:, n - 1:n, :]
    kt_t = (k.reshape(ns, n, d) * jnp.exp(tot - cum_s)).reshape(HG_TOKENS, d).T
    lane_sub = lax.broadcasted_iota(jnp.int32, (d, HG_TOKENS), 1) // n
    k_by_sub = jnp.concatenate([jnp.where(lane_sub == b, kt_t, 0.0) for b in range(ns)], axis=0).astype(bf16)
    update = jnp.dot(k_by_sub, v, preferred_element_type=f32)
    decay = jnp.exp(tot.reshape(ns, d)).T
    state = s0
    states = []
    for a in range(ns):
        states.append(state.astype(bf16))
        state = decay[:, a:a + 1] * state + update[a * d:(a + 1) * d]
    qt = q * jnp.exp(cum)
    row_sub = lax.broadcasted_iota(jnp.int32, (HG_TOKENS, d), 0) // n
    q_by_sub = jnp.concatenate([jnp.where(row_sub == a, qt, 0.0) for a in range(ns)], axis=1).astype(bf16)
    o = o + jnp.dot(q_by_sub, jnp.concatenate(states, axis=0), preferred_element_type=f32)
    on = o * lax.rsqrt(jnp.mean(o * o, axis=-1, keepdims=True) + EPS) * gn
    return on * (gp * jax.nn.sigmoid(gp)), state


def _head_parts(ref, h):
    return [ref[:, p * D + h * HG_DIM:p * D + (h + 1) * HG_DIM] for p in range(4)]


def hg_scan_fwd(proj, lb, gn, name, riders=None):
    t = proj.shape[0]
    nt = t // HG_TOKENS

    def body(p_ref, lb_ref, gn_ref, y_ref, s_ref, state):
        @pl.when(pl.program_id(0) == 0)
        def _():
            state[...] = jnp.zeros_like(state)

        for h in range(HG_HEADS):
            cols = slice(h * HG_DIM, (h + 1) * HG_DIM)
            s_ref[h, 0] = state[h]
            y, s1 = _hg_block_fn(*_head_parts(p_ref, h), state[h], lb_ref[:, cols], gn_ref[:, cols])
            y_ref[:, cols] = y.astype(bf16)
            state[h] = s1

    return _call(
        body, name=name, grid=(nt,),
        in_specs=[pl.BlockSpec((HG_TOKENS, 4 * D), lambda i: (i, 0)), _row(D), _row(D)],
        out_specs=[pl.BlockSpec((HG_TOKENS, D), lambda i: (i, 0)),
                   pl.BlockSpec((HG_HEADS, 1, HG_DIM, HG_DIM), lambda i: (0, i, 0, 0))],
        out_shape=[jax.ShapeDtypeStruct((t, D), bf16), jax.ShapeDtypeStruct((HG_HEADS, nt, HG_DIM, HG_DIM), f32)],
        scratch_shapes=[pltpu.VMEM((HG_HEADS, HG_DIM, HG_DIM), f32)],
        semantics=("arbitrary",), riders=riders,
    )(proj, lb, gn)


def hg_scan_bwd(proj, lb, gn, states, dy, name, riders=None):
    t = proj.shape[0]
    nt = t // HG_TOKENS

    def body(p_ref, lb_ref, gn_ref, s_ref, dy_ref, dp_ref, dlb_ref, dgn_ref, dstate):
        @pl.when(pl.program_id(0) == 0)
        def _():
            dstate[...] = jnp.zeros_like(dstate)
            dlb_ref[...] = jnp.zeros_like(dlb_ref)
            dgn_ref[...] = jnp.zeros_like(dgn_ref)

        for h in range(HG_HEADS):
            cols = slice(h * HG_DIM, (h + 1) * HG_DIM)
            _, vjp = jax.vjp(_hg_block_fn, *_head_parts(p_ref, h), s_ref[h, 0], lb_ref[:, cols], gn_ref[:, cols])
            grads = vjp((dy_ref[:, cols].astype(f32), dstate[h]))
            for p in range(4):
                dp_ref[:, p * D + h * HG_DIM:p * D + (h + 1) * HG_DIM] = grads[p].astype(bf16)
            dstate[h] = grads[4]
            dlb_ref[:, cols] += grads[5]
            dgn_ref[:, cols] += grads[6]

    small = jax.ShapeDtypeStruct((1, D), f32)
    return _call(
        body, name=name, grid=(nt,),
        in_specs=[pl.BlockSpec((HG_TOKENS, 4 * D), lambda i: (nt - 1 - i, 0)), _row(D), _row(D),
                  pl.BlockSpec((HG_HEADS, 1, HG_DIM, HG_DIM), lambda i: (0, nt - 1 - i, 0, 0)),
                  pl.BlockSpec((HG_TOKENS, D), lambda i: (nt - 1 - i, 0))],
        out_specs=[pl.BlockSpec((HG_TOKENS, 4 * D), lambda i: (nt - 1 - i, 0)), _row(D), _row(D)],
        out_shape=[jax.ShapeDtypeStruct((t, 4 * D), bf16), small, small],
        scratch_shapes=[pltpu.VMEM((HG_HEADS, HG_DIM, HG_DIM), f32)],
        semantics=("arbitrary",), riders=riders,
    )(proj, lb, gn, states, dy)


FFN_COLS = 1408
HALO = 8
STRIP = 16


def _ffn_specs(tm):
    nb = tm // HALO
    main_g = pl.BlockSpec((tm, FFN_COLS), lambda j, i: (i, j))
    main_v = pl.BlockSpec((tm, FFN_COLS), lambda j, i: (i, j + 2))
    halo_g = pl.BlockSpec((HALO, FFN_COLS), lambda j, i: (jnp.maximum(i * nb - 1, 0), j))
    halo_v = pl.BlockSpec((HALO, FFN_COLS), lambda j, i: (jnp.maximum(i * nb - 1, 0), j + 2))
    w_g = pl.BlockSpec((3, FFN_COLS), lambda j, i: (0, j))
    w_v = pl.BlockSpec((3, FFN_COLS), lambda j, i: (0, j + 2))
    b_g = pl.BlockSpec((1, FFN_COLS), lambda j, i: (0, j))
    b_v = pl.BlockSpec((1, FFN_COLS), lambda j, i: (0, j + 2))
    return [main_g, halo_g, main_v, halo_v, w_g, w_v, b_g, b_v]


def _strip_rows(r):
    return pl.ds(r * STRIP, STRIP) if isinstance(r, int) else pl.ds(pl.multiple_of(r * STRIP, STRIP), STRIP)


def _for_strips(nstrip, strip, reverse=False):
    if reverse:
        strip(nstrip - 1, True)
        lax.fori_loop(0, nstrip - 1, lambda k, c: (strip(nstrip - 2 - k, False), c)[1], 0)
    else:
        strip(0, True)
        lax.fori_loop(1, nstrip, lambda r, c: (strip(r, False), c)[1], 0)


SUBLANES = 8


def _rows_down(prev, cur, shift):
    row = lax.broadcasted_iota(jnp.int32, (SUBLANES, LANES), 0)
    tiles = [prev[STRIP - SUBLANES:]] + [cur[q * SUBLANES:(q + 1) * SUBLANES] for q in range(STRIP // SUBLANES)]
    turned = [pltpu.roll(x, shift, axis=0) for x in tiles]
    return jnp.concatenate([jnp.where(row < shift, turned[q], turned[q + 1]) for q in range(STRIP // SUBLANES)], axis=0)


def _rows_ahead(cur, nxt, shift):
    row = lax.broadcasted_iota(jnp.int32, (SUBLANES, LANES), 0)
    tiles = [cur[q * SUBLANES:(q + 1) * SUBLANES] for q in range(STRIP // SUBLANES)] + [nxt[:SUBLANES]]
    turned = [pltpu.roll(x, SUBLANES - shift, axis=0) for x in tiles]
    return jnp.concatenate([jnp.where(row >= SUBLANES - shift, turned[q + 1], turned[q])
                            for q in range(STRIP // SUBLANES)], axis=0)


def _conv_strip(main_ref, halo_ref, w_ref, b_ref, r, edge, cols, rowi):
    cur = main_ref[_strip_rows(r), cols].astype(f32)
    if edge:
        h = jnp.where(pl.program_id(1) == 0, 0.0, halo_ref[:, cols].astype(f32))
        prev = jnp.concatenate([jnp.zeros_like(h), h], axis=0)
    else:
        prev = main_ref[_strip_rows(r - 1), cols].astype(f32)
    a1, a2 = _rows_down(prev, cur, 1), _rows_down(prev, cur, 2)
    y = b_ref[:, cols] + w_ref[0:1, cols] * a2 + w_ref[1:2, cols] * a1 + w_ref[2:3, cols] * cur
    return y, (cur, a1, a2)


def ffn_gate_fwd(a, cw, cb, name, tm=512, riders=None):
    t = a.shape[0]
    tm = _tile(t, tm, STRIP)

    def body(ag_ref, hg_ref, av_ref, hv_ref, wg_ref, wv_ref, bg_ref, bv_ref, o_ref):
        rowi = lax.broadcasted_iota(jnp.int32, (STRIP, LANES), 0)

        def strip(r, edge):
            for c in range(FFN_COLS // LANES):
                cols = pl.ds(c * LANES, LANES)
                yg, _ = _conv_strip(ag_ref, hg_ref, wg_ref, bg_ref, r, edge, cols, rowi)
                yv, _ = _conv_strip(av_ref, hv_ref, wv_ref, bv_ref, r, edge, cols, rowi)
                o_ref[_strip_rows(r), cols] = (_gelu(yg) * yv).astype(bf16)

        _for_strips(tm // STRIP, strip)

    return _call(
        body, name=name, grid=(2, t // tm), in_specs=_ffn_specs(tm),
        out_specs=[pl.BlockSpec((tm, FFN_COLS), lambda j, i: (i, j))],
        out_shape=[jax.ShapeDtypeStruct((t, FFN_HIDDEN), bf16)],
        semantics=("parallel", "arbitrary"), riders=riders,
    )(a, a, a, a, cw, cw, cb, cb)


def ffn_gate_bwd(a, cw, cb, dhid, name, tm=512, riders=None):
    t = a.shape[0]
    tm = _tile(t, tm, STRIP)

    def body(ag_ref, hg_ref, av_ref, hv_ref, wg_ref, wv_ref, bg_ref, bv_ref, dh_ref,
             dy_ref, dwg_ref, dwv_ref, dbg_ref, dbv_ref, acc):
        rowi = lax.broadcasted_iota(jnp.int32, (STRIP, LANES), 0)

        @pl.when(pl.program_id(1) == 0)
        def _():
            acc[...] = jnp.zeros_like(acc)

        def strip(r, edge):
            rows = _strip_rows(r)
            for c in range(FFN_COLS // LANES):
                cols = pl.ds(c * LANES, LANES)
                yg, taps_g = _conv_strip(ag_ref, hg_ref, wg_ref, bg_ref, r, edge, cols, rowi)
                yv, taps_v = _conv_strip(av_ref, hv_ref, wv_ref, bv_ref, r, edge, cols, rowi)
                dh = dh_ref[rows, cols].astype(f32)
                cdf = 0.5 * (1.0 + lax.erf(yg * (1.0 / math.sqrt(2.0))))
                pdf = jnp.exp(-0.5 * yg * yg) * (1.0 / math.sqrt(2.0 * math.pi))
                dyg = dh * yv * (cdf + yg * pdf)
                dyv = dh * (yg * cdf)
                dy_ref[0, rows, cols] = dyg.astype(bf16)
                dy_ref[1, rows, cols] = dyv.astype(bf16)
                for p, (dy, (a0, a1, a2)) in enumerate(((dyg, taps_g), (dyv, taps_v))):
                    acc[4 * p + 0, :, cols] += dy * a2
                    acc[4 * p + 1, :, cols] += dy * a1
                    acc[4 * p + 2, :, cols] += dy * a0
                    acc[4 * p + 3, :, cols] += dy

        _for_strips(tm // STRIP, strip)

        @pl.when(pl.program_id(1) == pl.num_programs(1) - 1)
        def _():
            for p, (dw_ref, db_ref) in enumerate(((dwg_ref, dbg_ref), (dwv_ref, dbv_ref))):
                for tap in range(3):
                    dw_ref[tap:tap + 1, :] = jnp.sum(acc[4 * p + tap], axis=0, keepdims=True)
                db_ref[...] = jnp.sum(acc[4 * p + 3], axis=0, keepdims=True)

    half_w = pl.BlockSpec((3, FFN_COLS), lambda j, i: (0, j))
    half_b = pl.BlockSpec((1, FFN_COLS), lambda j, i: (0, j))
    return _call(
        body, name=name, grid=(2, t // tm),
        in_specs=_ffn_specs(tm) + [pl.BlockSpec((tm, FFN_COLS), lambda j, i: (i, j))],
        out_specs=[pl.BlockSpec((2, tm, FFN_COLS), lambda j, i: (0, i, j)), half_w, half_w, half_b, half_b],
        out_shape=[jax.ShapeDtypeStruct((2, t, FFN_HIDDEN), bf16),
                   jax.ShapeDtypeStruct((3, FFN_HIDDEN), f32), jax.ShapeDtypeStruct((3, FFN_HIDDEN), f32),
                   jax.ShapeDtypeStruct((1, FFN_HIDDEN), f32), jax.ShapeDtypeStruct((1, FFN_HIDDEN), f32)],
        scratch_shapes=[pltpu.VMEM((8, STRIP, FFN_COLS), f32)],
        semantics=("parallel", "arbitrary"), riders=riders,
    )(a, a, a, a, cw, cw, cb, cb, dhid)


def conv_transpose(dy, cw, name, tm=512):
    _, t, fh = dy.shape
    tm = _tile(t, tm, STRIP)
    nb = tm // HALO
    last_halo = t // HALO - 1
    ncol = fh // FFN_COLS

    def body(main_ref, halo_ref, w_ref, o_ref):
        rowi = lax.broadcasted_iota(jnp.int32, (STRIP, LANES), 0)
        last_block = pl.program_id(2) == pl.num_programs(2) - 1

        def strip(r, edge):
            rows = _strip_rows(r)
            for c in range(FFN_COLS // LANES):
                cols = pl.ds(c * LANES, LANES)
                cur = main_ref[0, rows, cols].astype(f32)
                if edge:
                    h = jnp.where(last_block, 0.0, halo_ref[0, :, cols].astype(f32))
                    nxt = jnp.concatenate([h, jnp.zeros_like(h)], axis=0)
                else:
                    nxt = main_ref[0, _strip_rows(r + 1), cols].astype(f32)
                d1, d2 = _rows_ahead(cur, nxt, 1), _rows_ahead(cur, nxt, 2)
                o_ref[rows, cols] = (w_ref[2:3, cols] * cur + w_ref[1:2, cols] * d1 + w_ref[0:1, cols] * d2).astype(bf16)

        _for_strips(tm // STRIP, strip, reverse=True)

    return pl.pallas_call(
        body, name=name, grid=(2, ncol, t // tm),
        in_specs=[pl.BlockSpec((1, tm, FFN_COLS), lambda p, j, i: (p, i, j)),
                  pl.BlockSpec((1, HALO, FFN_COLS), lambda p, j, i: (p, jnp.minimum((i + 1) * nb, last_halo), j)),
                  pl.BlockSpec((3, FFN_COLS), lambda p, j, i: (0, p * ncol + j))],
        out_specs=pl.BlockSpec((tm, FFN_COLS), lambda p, j, i: (i, p * ncol + j)),
        out_shape=jax.ShapeDtypeStruct((t, 2 * fh), bf16),
        compiler_params=_cparams("parallel", "parallel", "arbitrary"),
    )(dy, dy, cw)


def ada_mod(c_all, ada_w, ada_b_cols, name):
    cols = ada_w.shape[2]

    def body(c_ref, w_ref, b_ref, o_ref):
        c = c_ref[...]
        cond = (c * jax.nn.sigmoid(c)).astype(bf16)
        o_ref[0] = jnp.dot(cond, w_ref[0].astype(bf16), preferred_element_type=f32) + b_ref[0]

    return pl.pallas_call(
        body, name=name, grid=(DEPTH,),
        in_specs=[pl.BlockSpec((N_DEV, D), lambda i: (0, 0)), pl.BlockSpec((1, D, cols), lambda i: (i, 0, 0)),
                  pl.BlockSpec((1, 1, cols), lambda i: (i, 0, 0))],
        out_specs=pl.BlockSpec((1, N_DEV, cols), lambda i: (i, 0, 0)),
        out_shape=jax.ShapeDtypeStruct((DEPTH, N_DEV, cols), f32), compiler_params=_cparams("parallel"),
    )(c_all, ada_w, ada_b_cols)


def ada_grads(c_all, dmod_cols, dmod_all, name):
    cols = dmod_cols.shape[2]

    def body(c_ref, dm_ref, da_ref, dw_ref, db_ref):
        c = c_ref[...]
        cond = c * jax.nn.sigmoid(c)
        dw_ref[0] = lax.dot_general(cond, dm_ref[0], (((0,), (0,)), ((), ())), precision=lax.Precision.HIGHEST,
                                    preferred_element_type=f32)
        acc = da_ref[0, 0]
        for e in range(1, N_DEV):
            acc = acc + da_ref[e, 0]
        db_ref[0] = acc

    return pl.pallas_call(
        body, name=name, grid=(DEPTH,),
        in_specs=[pl.BlockSpec((N_DEV, D), lambda i: (0, 0)), pl.BlockSpec((1, N_DEV, cols), lambda i: (i, 0, 0)),
                  pl.BlockSpec((N_DEV, 1, 1, 6 * D), lambda i: (0, i, 0, 0))],
        out_specs=(pl.BlockSpec((1, D, cols), lambda i: (i, 0, 0)), pl.BlockSpec((1, 1, 6 * D), lambda i: (i, 0, 0))),
        out_shape=(jax.ShapeDtypeStruct((DEPTH, D, cols), f32), jax.ShapeDtypeStruct((DEPTH, 1, 6 * D), f32)),
        compiler_params=_cparams("parallel"),
    )(c_all, dmod_cols, dmod_all)


def lower_bound_fwd(hg_lb, name):
    n = hg_lb.shape[1]

    def body(l_ref, o_ref):
        o_ref[...] = jax.nn.sigmoid(l_ref[1:2, :] - l_ref[0:1, :])

    return pl.pallas_call(body, name=name, out_shape=jax.ShapeDtypeStruct((1, n), f32))(hg_lb)


def lower_bound_bwd(hg_lb, dlb, name):
    n = hg_lb.shape[1]

    def body(l_ref, d_ref, o_ref):
        p = jax.nn.sigmoid(l_ref[1:2, :] - l_ref[0:1, :])
        g = d_ref[...] * p * (1.0 - p)
        o_ref[0:1, :] = -g
        o_ref[1:2, :] = g

    return pl.pallas_call(body, name=name, out_shape=jax.ShapeDtypeStruct((2, n), f32))(hg_lb, dlb)


def _adamw(w, g, m, v):
    m = ADAM_B1 * m + (1.0 - ADAM_B1) * g
    v = ADAM_B2 * v + (1.0 - ADAM_B2) * jnp.square(g)
    m_hat = m / (1.0 - ADAM_B1 ** ADAM_STEP)
    v_hat = v / (1.0 - ADAM_B2 ** ADAM_STEP)
    delta = -ADAM_LR * (m_hat / (jnp.sqrt(v_hat) + ADAM_EPS) + ADAM_WD * w)
    return delta, m, v


ADAM_BLOCK_BYTES = 32 * 1024 * 1024


def adam_reduced(parts, w, m, v, name):
    layers, r, c = w.shape
    outs = None
    for layer in range(layers):
        outs = _adam_layer(parts[layer], w, m, v, layer, outs, f"{name}_{layer}")
    return outs


def _adam_layer(parts, w, m, v, layer, prev, name):
    layers, r, c = w.shape
    rows = parts[0].shape[1]
    assert all(p.shape == (N_DEV, rows, c) for p in parts) and rows * len(parts) == r
    row_bytes = 2 * (len(parts) * N_DEV * c * 2 + 7 * c * 4)
    tr = _tile(rows, max(16, ADAM_BLOCK_BYTES // row_bytes), 16)
    steps = rows // tr
    n_prev = 0 if prev is None else 4

    def body(*refs):
        p_refs = refs[:len(parts)]
        w_ref, m_ref, v_ref = refs[len(parts):len(parts) + 3]
        g_ref, d_ref, mo_ref, vo_ref = refs[len(parts) + 3 + n_prev:]
        for idx in range(len(parts)):
            @pl.when(pl.program_id(0) == idx)
            def _():
                g = p_refs[idx][0].astype(f32)
                for j in range(1, N_DEV):
                    g = g + p_refs[idx][j].astype(f32)
                g_ref[...] = g
                d_ref[...], mo_ref[...], vo_ref[...] = _adamw(w_ref[...], g, m_ref[...], v_ref[...])

    def part_spec(idx):
        return pl.BlockSpec((N_DEV, tr, c), lambda p, i: (0, jnp.where(p == idx, i, 0), 0))

    blk = pl.BlockSpec((None, tr, c), lambda p, i: (layer, p * steps + i, 0))
    out = jax.ShapeDtypeStruct((layers, r, c), f32)
    n_in = len(parts) + 3
    return pl.pallas_call(
        body, name=name, grid=(len(parts), steps),
        in_specs=[part_spec(idx) for idx in range(len(parts))] + [blk, blk, blk] + [ANY] * n_prev,
        out_specs=(blk, blk, blk, blk), out_shape=(out, out, out, out),
        input_output_aliases={n_in + k: k for k in range(n_prev)},
        compiler_params=_cparams("arbitrary", "arbitrary"),
    )(*parts, w, m, v, *(prev or ()))


def adam_plain(g, w, m, v, name, tr=256):
    r, c = w.shape
    tr = _tile(r, tr, 8)

    def body(g_ref, w_ref, m_ref, v_ref, d_ref, mo_ref, vo_ref):
        d_ref[...], mo_ref[...], vo_ref[...] = _adamw(w_ref[...], g_ref[...], m_ref[...], v_ref[...])

    blk = pl.BlockSpec((tr, c), lambda i: (i, 0))
    out = jax.ShapeDtypeStruct((r, c), f32)
    return pl.pallas_call(
        body, name=name, grid=(r // tr,), in_specs=[blk, blk, blk, blk], out_specs=(blk, blk, blk),
        out_shape=(out, out, out), compiler_params=_cparams("parallel"),
    )(g, w, m, v)


def sum_parts(parts, name):
    _, r, c = parts.shape

    def body(p_ref, o_ref):
        acc = p_ref[0]
        for j in range(1, N_DEV):
            acc = acc + p_ref[j]
        o_ref[...] = acc

    return pl.pallas_call(body, name=name, out_shape=jax.ShapeDtypeStruct((r, c), f32))(parts)


def _pack(arrs, rows_mult=8):
    flat = jnp.concatenate([a.reshape(-1) for a in arrs])
    rows = -(-flat.shape[0] // LANES)
    rows = -(-rows // rows_mult) * rows_mult
    return jnp.pad(flat, (0, rows * LANES - flat.shape[0])).reshape(rows, LANES)


def _unpack(flat, shapes):
    out, at = [], 0
    for s in shapes:
        n = math.prod(s)
        out.append(flat[at:at + n].reshape(s))
        at += n
    return out


def kernel(x, c, gm_w_in, gm_ln_g, gm_ln_b, gm_w_s, gm_b_s, gm_w_out, hg_w_in, hg_lb, hg_gn_g, hg_w_out, ffn_w_up, ffn_conv_w, ffn_conv_b, ffn_w_down, norm_g, ada_w, ada_b, final_g, loss_target, m_gm_w_in, m_gm_ln_g, m_gm_ln_b, m_gm_w_s, m_gm_b_s, m_gm_w_out, m_hg_w_in, m_hg_lb, m_hg_gn_g, m_hg_w_out, m_ffn_w_up, m_ffn_conv_w, m_ffn_conv_b, m_ffn_w_down, m_norm_g, m_ada_w, m_ada_b, m_final_g, v_gm_w_in, v_gm_ln_g, v_gm_ln_b, v_gm_w_s, v_gm_b_s, v_gm_w_out, v_hg_w_in, v_hg_lb, v_hg_gn_g, v_hg_w_out, v_ffn_w_up, v_ffn_conv_w, v_ffn_conv_b, v_ffn_w_down, v_norm_g, v_ada_w, v_ada_b, v_final_g):
    me = _flat(_mesh_pos())
    xt = x[0]
    t = xt.shape[0]

    small_shapes = [(1, D), (2, HG_DIM), (2, HG_DIM), (DEPTH, 2, HG_DIM), (DEPTH, 3, 2 * FFN_HIDDEN // N_DEV)]
    (small_all,) = all_gather([_pack([c, hg_lb, hg_gn_g, norm_g, ffn_conv_w])], "gather_small")
    small_all = small_all.reshape(N_DEV, -1)
    at = 0
    pieces = []
    for s in small_shapes:
        n = math.prod(s)
        pieces.append(small_all[:, at:at + n].reshape((N_DEV,) + s))
        at += n
    c_all = pieces[0].reshape(N_DEV, D)
    hg_lb_full = jnp.transpose(pieces[1], (1, 0, 2)).reshape(2, D)
    hg_gn_full = jnp.transpose(pieces[2], (1, 0, 2)).reshape(2, D)
    norm_g_full = jnp.transpose(pieces[3], (1, 2, 0, 3)).reshape(DEPTH, 2, D)
    conv_w_full = jnp.transpose(pieces[4], (1, 2, 0, 3)).reshape(DEPTH, 3, 2 * FFN_HIDDEN)

    lb1 = lower_bound_fwd(hg_lb_full, "lower_bound")
    lbs = [jnp.zeros((1, D), f32), lb1]

    ada_b_cols = lax.dynamic_slice(ada_b, (0, me * ADA_COLS), (DEPTH, ADA_COLS)).reshape(DEPTH, 1, ADA_COLS)
    mod_cols = ada_mod(c_all, ada_w, ada_b_cols, "ada_mod")
    (mod_mine,) = all_to_all([jnp.transpose(mod_cols, (1, 0, 2))], "mod_to_examples")
    mod = jnp.transpose(mod_mine, (1, 0, 2)).reshape(DEPTH, 6, 1, D)

    def layer_shards(i):
        j = i // 2
        w_in, w_out = (gm_w_in, gm_w_out) if i % 2 == 0 else (hg_w_in, hg_w_out)
        return [w_in[j].astype(bf16), w_out[j].astype(bf16), ffn_w_up[i].T.astype(bf16), ffn_w_down[i].astype(bf16)]

    def full_rows(g):
        return g.reshape(N_DEV * g.shape[1], g.shape[2])

    carried_by = {
        "in_0": [(0, 1), (0, 3)], "mix_0": [(0, 2)], "up_0": [(1, 0), (1, 1)], "gate_0": [(1, 2)], "down_0": [(1, 3)],
        "in_1": [(2, 0)], "mix_1": [(2, 1), (2, 2), (2, 3)], "up_1": [(3, 0), (3, 1)], "gate_1": [(3, 2)], "down_1": [(3, 3)],
    }
    shards = [layer_shards(i) for i in range(DEPTH)]
    gathered = {}
    (gathered[(0, 0)],) = all_gather([shards[0][0]], "gather_weights_0", relay=True)

    def carry(call, site, **kw):
        items = carried_by.get(site, [])
        outs = call(riders=Riders([shards[l][slot] for l, slot in items], True), **kw)
        for item, g in zip(items, outs[len(outs) - len(items):]):
            gathered[item] = g
        return outs[:len(outs) - len(items)]

    saved = []
    weights = []
    xcur = xt
    h = norm_fwd(xcur, norm_g_full[0, 0:1], mod[0, 1], mod[0, 0], "norm1_0")
    for i in range(DEPTH):
        j = i // 2
        sh1, sc1, g1, sh2, sc2, g2 = [mod[i, p] for p in range(6)]
        gn2 = norm_g_full[i, 1:2]
        s = {"x0": xcur, "h": h}
        w_in = gathered[(i, 0)]
        if i % 2 == 0:
            (z,) = carry(functools.partial(mm_nn, h, w_in, bf16, f"gm_in_{i}"), f"in_{i}")
            bs = gm_b_s[j].reshape(GM_HEADS, GM_BLOCK, 1)
            (mixed,) = carry(functools.partial(gm_mix_fwd, z, gm_w_s[j], bs, gm_ln_g[j:j + 1], gm_ln_b[j:j + 1],
                                               f"gm_mix_{i}"), f"mix_{i}")
            s["z"] = z
        else:
            (proj,) = carry(functools.partial(mm_nn, h, w_in, f32, f"hg_in_{i}"), f"in_{i}")
            mixed, states = carry(functools.partial(hg_scan_fwd, proj, lbs[j], hg_gn_full[j:j + 1], f"hg_scan_{i}"),
                                  f"mix_{i}")
            s["proj"], s["states"] = proj, states
        s["mixed"] = mixed
        w_out = full_rows(gathered[(i, 1)])
        y, x1, h2 = carry(functools.partial(mm_nn_residual, mixed, w_out, xcur, g1, (gn2, sc2, sh2), f"mix_out_{i}"),
                          f"out_{i}")
        s["y"], s["x1"] = y, x1
        w_up = gathered[(i, 2)]
        (a,) = carry(functools.partial(mm_nn, h2, w_up, bf16, f"ffn_up_{i}", transposed=True), f"up_{i}")
        (hid,) = carry(functools.partial(ffn_gate_fwd, a, conv_w_full[i], ffn_conv_b[i:i + 1], f"ffn_gate_{i}"), f"gate_{i}")
        w_down = full_rows(gathered[(i, 3)])
        next_norm = (norm_g_full[i + 1, 0:1], mod[i + 1, 1], mod[i + 1, 0]) if i + 1 < DEPTH else None
        outs = carry(functools.partial(mm_nn_residual, hid, w_down, x1, g2, next_norm, f"ffn_down_{i}"), f"down_{i}")
        fo, x2 = outs[0], outs[1]
        s["h2"], s["a"], s["hid"], s["f"] = h2, a, hid, fo
        weights.append((w_in, w_out, w_up, w_down))
        saved.append(s)
        xcur = x2
        h = outs[2] if next_norm is not None else None

    loss_part, dx, d_final_g, dg2, df = loss_head(xcur, final_g.reshape(1, D), loss_target[0], saved[-1]["f"],
                                                  mod[DEPTH - 1, 5], "loss_head")
    loss = lax.psum(loss_part[0, 0], ("x", "y", "c"))

    def halves(blocked):
        rows = blocked.shape[1] // 2
        return [(blocked, (0, rows)), (blocked, (rows, rows))]

    def by_owner_rows(dw):
        k, n = dw.shape
        return dw.reshape(N_DEV, k // N_DEV, n)

    received = [[[] for _ in range(4)] for _ in range(DEPTH)]

    def send(call, items, **kw):
        outs = call(riders=Riders([arr for arr, _ in items], False), **kw)
        for (_, (layer, slot)), got in zip(items, outs[len(outs) - len(items):]):
            received[layer][slot].append(got)
        return outs[:len(outs) - len(items)]

    dmod = [None] * DEPTH
    d_norm_g = [None] * DEPTH
    d_gm = {k: [None, None] for k in ("ws", "bs", "lng", "lnb")}
    d_hg = {k: [None, None] for k in ("lb", "gn")}
    d_ffn = {k: [None] * DEPTH for k in ("cw", "cb")}
    in_halves = []
    for i in reversed(range(DEPTH)):
        j = i // 2
        s = saved[i]
        w_in, w_out, w_up, w_down = weights[i]
        sh1, sc1, g1, sh2, sc2, g2 = [mod[i, p] for p in range(6)]
        gn1, gn2 = norm_g_full[i, 0:1], norm_g_full[i, 1:2]
        (dw_down,) = mm_tn(s["hid"], df, bf16, f"dw_down_{i}", tn=D)
        (dhid,) = send(functools.partial(mm_nt, df, w_down, bf16, f"dhid_{i}"), in_halves[:1])
        dyc, dwg, dwv, dbg, dbv = send(
            functools.partial(ffn_gate_bwd, s["a"], conv_w_full[i], ffn_conv_b[i:i + 1], dhid, f"ffn_gate_bwd_{i}"),
            in_halves[1:] + [(by_owner_rows(dw_down), (i, 3))])
        d_ffn["cw"][i] = jnp.concatenate([dwg, dwv], axis=1)
        d_ffn["cb"][i] = jnp.concatenate([dbg, dbv], axis=1)
        da = conv_transpose(dyc, conv_w_full[i], f"conv_t_{i}")
        (dw_up_t,) = mm_tn(da, s["h2"], bf16, f"dw_up_{i}")
        up_halves = [(part, (i, 2)) for part in halves(dw_up_t.reshape(N_DEV, -1, D))]
        (dh2,) = send(functools.partial(mm_nt, da, w_up, bf16, f"dh2_{i}", transposed=True), up_halves[:1])
        dx1, dgn2, dsc2, dsh2, dg1, dy = norm_bwd(s["x1"], gn2, sc2, sh2, dh2, dx, f"norm2_bwd_{i}", gate=(s["y"], g1))
        (dw_out,) = mm_tn(s["mixed"], dy, bf16, f"dw_mix_out_{i}")
        (dmixed,) = mm_nt(dy, w_out, bf16, f"dmixed_{i}")
        if i % 2 == 0:
            bs = gm_b_s[j].reshape(GM_HEADS, GM_BLOCK, 1)
            dpre, dws, dbs, dlng, dlnb = send(
                functools.partial(gm_mix_bwd, s["z"], gm_w_s[j], bs, gm_ln_g[j:j + 1], gm_ln_b[j:j + 1], dmixed,
                                  f"gm_mix_bwd_{i}"), up_halves[1:])
            d_gm["ws"][j], d_gm["bs"][j], d_gm["lng"][j], d_gm["lnb"][j] = dws, dbs.reshape(GM_HEADS, GM_BLOCK), dlng, dlnb
        else:
            dpre, dlb, dgn = send(
                functools.partial(hg_scan_bwd, s["proj"], lbs[j], hg_gn_full[j:j + 1], s["states"], dmixed,
                                  f"hg_scan_bwd_{i}"), up_halves[1:])
            d_hg["lb"][j], d_hg["gn"][j] = dlb, dgn
        (dw_in,) = send(functools.partial(mm_tn_by_owner, s["h"], dpre, f"dw_mix_in_{i}"), [(by_owner_rows(dw_out), (i, 1))])
        in_halves = [(part, (i, 0)) for part in halves(dw_in)]
        (dh,) = send(functools.partial(mm_nt, dpre, w_in, bf16, f"dh_mix_{i}"), in_halves[:1] if i == 0 else [])
        dmod_i = [None, None, dg1, dsh2, dsc2, dg2]
        if i > 0:
            dx, dgn1, dsc1, dsh1, dg2, df = norm_bwd(s["x0"], gn1, sc1, sh1, dh, dx1, f"norm1_bwd_{i}",
                                                     gate=(saved[i - 1]["f"], mod[i - 1, 5]))
        else:
            dx, dgn1, dsc1, dsh1 = send(functools.partial(norm_bwd, s["x0"], gn1, sc1, sh1, dh, dx1, f"norm1_bwd_{i}"),
                                        in_halves[1:])
        dmod_i[0], dmod_i[1] = dsh1, dsc1
        dmod[i] = jnp.concatenate(dmod_i, axis=1)
        d_norm_g[i] = jnp.concatenate([dgn1, dgn2], axis=0)
    grad_x = dx.reshape(1, t, D)

    (dmod_all,) = all_gather([jnp.concatenate(dmod, axis=0)], "gather_dmod")
    dmod_cols = jnp.transpose(lax.dynamic_slice(dmod_all, (0, 0, me * ADA_COLS), (N_DEV, DEPTH, ADA_COLS)), (1, 0, 2))
    g_ada_w, g_ada_b = ada_grads(c_all, dmod_cols, dmod_all.reshape(N_DEV, DEPTH, 1, 6 * D), "ada_grads")
    g_ada_b = g_ada_b.reshape(DEPTH, 6 * D)

    small_partials = [jnp.concatenate(d_gm["lng"], axis=0), jnp.concatenate(d_gm["lnb"], axis=0),
                      jnp.stack(d_gm["ws"]), jnp.stack(d_gm["bs"]), jnp.concatenate(d_ffn["cb"], axis=0),
                      d_final_g, d_hg["lb"][1], jnp.concatenate(d_hg["gn"], axis=0), jnp.stack(d_norm_g),
                      jnp.stack(d_ffn["cw"])]
    partial_shapes = [p.shape for p in small_partials]
    packed = _pack(small_partials, rows_mult=8 * N_DEV)
    rows = packed.shape[0] // N_DEV
    (recv,) = all_to_all([packed.reshape(N_DEV, rows, LANES)], "small_grads_exchange")
    (summed,) = all_gather([sum_parts(recv, "small_grads_sum")], "small_grads_gather")

    def parts_of(slot, layers):
        return [received[i][slot] for i in layers]

    def swapped(a):
        return jnp.swapaxes(a, 1, 2)

    w_shards = [gm_w_in, gm_w_out, hg_w_in, hg_w_out, swapped(ffn_w_up), ffn_w_down]
    big_parts = [parts_of(0, (0, 2)), parts_of(1, (0, 2)), parts_of(0, (1, 3)), parts_of(1, (1, 3)),
                 parts_of(2, range(DEPTH)), parts_of(3, range(DEPTH))]
    big_m = [m_gm_w_in, m_gm_w_out, m_hg_w_in, m_hg_w_out, swapped(m_ffn_w_up), m_ffn_w_down]
    big_v = [v_gm_w_in, v_gm_w_out, v_hg_w_in, v_hg_w_out, swapped(v_ffn_w_up), v_ffn_w_down]
    big = [adam_reduced(parts, w, m_, v_, f"adam_big_{idx}")
           for idx, (w, m_, v_, parts) in enumerate(zip(w_shards, big_m, big_v, big_parts))]
    big[4] = [swapped(o) for o in big[4]]
    (g_gm_w_in, d_gm_w_in, nm_gm_w_in, nv_gm_w_in), (g_gm_w_out, d_gm_w_out, nm_gm_w_out, nv_gm_w_out), \
        (g_hg_w_in, d_hg_w_in, nm_hg_w_in, nv_hg_w_in), (g_hg_w_out, d_hg_w_out, nm_hg_w_out, nv_hg_w_out), \
        (g_ffn_w_up, d_ffn_w_up, nm_ffn_w_up, nv_ffn_w_up), (g_ffn_w_down, d_ffn_w_down, nm_ffn_w_down, nv_ffn_w_down) = big

    g_ln_g, g_ln_b, g_ws, g_bs, g_cb, g_final, g_lb1, g_gn, g_norm, g_cw = _unpack(summed.reshape(-1), partial_shapes)
    g_final = g_final.reshape(D)

    def my_cols(a, n):
        start = (0,) * (a.ndim - 1) + (me * n,)
        return lax.dynamic_slice(a, start, a.shape[:-1] + (n,))

    g_hg_lb = lower_bound_bwd(hg_lb, my_cols(g_lb1, HG_DIM), "lower_bound_bwd")
    g_hg_gn = my_cols(g_gn, HG_DIM)
    g_norm_g = my_cols(g_norm, HG_DIM)
    g_conv_w = my_cols(g_cw, 2 * FFN_HIDDEN // N_DEV)

    two_d = (-1, ADA_COLS)
    d_ada_w, nm_ada_w, nv_ada_w = [o.reshape(ada_w.shape) for o in adam_plain(
        g_ada_w.reshape(two_d), ada_w.reshape(two_d), m_ada_w.reshape(two_d), v_ada_w.reshape(two_d), "adam_ada_w")]

    small_g = [g_ln_g, g_ln_b, g_ws, g_bs, g_cb, g_ada_b, g_final, g_hg_lb, g_hg_gn, g_norm_g, g_conv_w]
    small_w = [gm_ln_g, gm_ln_b, gm_w_s, gm_b_s, ffn_conv_b, ada_b, final_g, hg_lb, hg_gn_g, norm_g, ffn_conv_w]
    small_m = [m_gm_ln_g, m_gm_ln_b, m_gm_w_s, m_gm_b_s, m_ffn_conv_b, m_ada_b, m_final_g, m_hg_lb, m_hg_gn_g, m_norm_g, m_ffn_conv_w]
    small_v = [v_gm_ln_g, v_gm_ln_b, v_gm_w_s, v_gm_b_s, v_ffn_conv_b, v_ada_b, v_final_g, v_hg_lb, v_hg_gn_g, v_norm_g, v_ffn_conv_w]
    shapes = [w.shape for w in small_w]
    small_g = [g.reshape(s) for g, s in zip(small_g, shapes)]
    outs = adam_plain(_pack(small_g), _pack(small_w), _pack(small_m), _pack(small_v), "adam_small")
    (d_ln_g, d_ln_b, d_ws, d_bs, d_cb, d_ada_b, d_final, d_hg_lb, d_hg_gn, d_norm_g_, d_conv_w), \
        (nm_ln_g, nm_ln_b, nm_ws, nm_bs, nm_cb, nm_ada_b, nm_final, nm_hg_lb, nm_hg_gn, nm_norm_g, nm_conv_w), \
        (nv_ln_g, nv_ln_b, nv_ws, nv_bs, nv_cb, nv_ada_b, nv_final, nv_hg_lb, nv_hg_gn, nv_norm_g, nv_conv_w) = [
            _unpack(o.reshape(-1), shapes) for o in outs]
    g_ln_g, g_ln_b, g_ws, g_bs, g_cb, g_ada_b, g_final, g_hg_lb, g_hg_gn, g_norm_g, g_conv_w = small_g

    grads = (g_gm_w_in, g_ln_g, g_ln_b, g_ws, g_bs, g_gm_w_out, g_hg_w_in, g_hg_lb, g_hg_gn, g_hg_w_out,
             g_ffn_w_up, g_conv_w, g_cb, g_ffn_w_down, g_norm_g, g_ada_w, g_ada_b, g_final)
    deltas = (d_gm_w_in, d_ln_g, d_ln_b, d_ws, d_bs, d_gm_w_out, d_hg_w_in, d_hg_lb, d_hg_gn, d_hg_w_out,
              d_ffn_w_up, d_conv_w, d_cb, d_ffn_w_down, d_norm_g_, d_ada_w, d_ada_b, d_final)
    new_m = (nm_gm_w_in, nm_ln_g, nm_ln_b, nm_ws, nm_bs, nm_gm_w_out, nm_hg_w_in, nm_hg_lb, nm_hg_gn, nm_hg_w_out,
             nm_ffn_w_up, nm_conv_w, nm_cb, nm_ffn_w_down, nm_norm_g, nm_ada_w, nm_ada_b, nm_final)
    new_v = (nv_gm_w_in, nv_ln_g, nv_ln_b, nv_ws, nv_bs, nv_gm_w_out, nv_hg_w_in, nv_hg_lb, nv_hg_gn, nv_hg_w_out,
             nv_ffn_w_up, nv_conv_w, nv_cb, nv_ffn_w_down, nv_norm_g, nv_ada_w, nv_ada_b, nv_final)
    return (loss, grad_x) + grads + deltas + new_m + new_v
```

```python
import functools
import math

import jax
import jax.numpy as jnp
from jax import lax
from jax.experimental import pallas as pl
from jax.experimental.pallas import tpu as pltpu

f32 = jnp.float32
bf16 = jnp.bfloat16
MESH = pl.DeviceIdType.MESH

N_DEV = 8
D = 1024
DEPTH = 4
EPS = 1e-6
GM_WIDTH = 2048
GM_HEADS = 8
GM_HEAD_DIM = 256
GM_BLOCK = 128
CHUNK = 64
HG_HEADS = 8
HG_DIM = 128
FFN_HIDDEN = 2816
ADA_COLS = 6 * D // N_DEV

HG_SUB = 32
HG_PAIR = 8
HG_TOKENS = 128

ADAM_LR = 0.001
ADAM_B1 = 0.9
ADAM_B2 = 0.999
ADAM_EPS = 1e-08
ADAM_WD = 0.01
ADAM_STEP = 10

V7X_VMEM_LIMIT = 56 * 1024 * 1024
LANES = 128


def _cparams(*sem):
    return pltpu.CompilerParams(dimension_semantics=sem or None, vmem_limit_bytes=V7X_VMEM_LIMIT)


def _tile(n, target, mult=LANES):
    best = None
    for t in range(mult, min(n, target) + 1, mult):
        if n % t == 0:
            best = t
    return best or n


WEIGHT_BLOCK_BYTES = 6 * 1024 * 1024


def _weight_tile(n, k):
    return _tile(n, max(LANES, WEIGHT_BLOCK_BYTES // (2 * k)))


def _gelu(x):
    return 0.5 * x * (1.0 + lax.erf(x * (1.0 / math.sqrt(2.0))))


def _mesh_pos():
    return lax.axis_index("x"), lax.axis_index("y"), lax.axis_index("c")


def _flat(pos):
    return 4 * pos[0] + 2 * pos[1] + pos[2]


def _peer(pos, k):
    return ((1 - pos[0]) if k & 4 else pos[0], (1 - pos[1]) if k & 2 else pos[1], (1 - pos[2]) if k & 1 else pos[2])


def _exchange_copies(ins, outs, send_sems, recv_sems, local_sems, gather):
    pos = _mesh_pos()
    me = _flat(pos)

    def src(i, dest):
        if gather:
            return ins[i]
        ref, rows = ins[i] if isinstance(ins[i], tuple) else (ins[i], None)
        return ref.at[dest] if rows is None else ref.at[dest, pl.ds(*rows)]

    local = [pltpu.make_async_copy(src(i, me), outs[i].at[me], local_sems.at[i]) for i in range(len(ins))]
    sends, recvs = [], []
    for k in range(1, N_DEV):
        peer = _peer(pos, k)
        there = _flat(peer)
        for i in range(len(ins)):
            sems = dict(send_sem=send_sems.at[i * 7 + k - 1], recv_sem=recv_sems.at[i * 7 + k - 1],
                        device_id=peer, device_id_type=MESH)
            sends.append(pltpu.make_async_remote_copy(src_ref=src(i, there), dst_ref=outs[i].at[me], **sems))
            recvs.append(pltpu.make_async_remote_copy(src_ref=src(i, there), dst_ref=outs[i].at[there], **sems))
    return local, sends, recvs


def _exchange_start(*refs):
    local, sends, _ = _exchange_copies(*refs)
    for cp in local + sends:
        cp.start()


def _exchange_wait(*refs):
    local, sends, recvs = _exchange_copies(*refs)
    for cp in recvs:
        cp.wait_recv()
    for cp in sends:
        cp.wait_send()
    for cp in local:
        cp.wait()


OTHER_CHIPS = (2, 4, 6)


def _relay_copies(ins, outs, send_sems, recv_sems, local_sems):
    pos = _mesh_pos()
    me = _flat(pos)
    sibling = _peer(pos, 1)
    local = [pltpu.make_async_copy(ins[i], outs[i].at[me], local_sems.at[i]) for i in range(len(ins))]
    first, passes, recvs = [], {k: [] for k in OTHER_CHIPS}, {k: [] for k in range(1, N_DEV)}
    for i in range(len(ins)):
        def copy(k, src, block, to):
            return pltpu.make_async_remote_copy(
                src_ref=src, dst_ref=outs[i].at[block], send_sem=send_sems.at[i * 7 + k - 1],
                recv_sem=recv_sems.at[i * 7 + k - 1], device_id=to, device_id_type=MESH)

        for k in (1,) + OTHER_CHIPS:
            first.append(copy(k, ins[i], me, _peer(pos, k)))
        for k in OTHER_CHIPS:
            there = _flat(_peer(pos, k))
            passes[k].append(copy(k ^ 1, outs[i].at[there], there, sibling))
        for k in range(1, N_DEV):
            there = _flat(_peer(pos, k))
            recvs[k].append(copy(k, ins[i], there, _peer(pos, k)))
    return local, first, passes, recvs


def _relay_start(ins, outs, *sems):
    local, first, _, _ = _relay_copies(ins, outs, *sems)
    for cp in local + first:
        cp.start()


def _relay_wait(ins, outs, *sems):
    local, first, passes, recvs = _relay_copies(ins, outs, *sems)
    for k in OTHER_CHIPS:
        for cp in recvs[k]:
            cp.wait_recv()
        for cp in passes[k]:
            cp.start()
    for k in (1, 3, 5, 7):
        for cp in recvs[k]:
            cp.wait_recv()
    for cp in first + [cp for k in OTHER_CHIPS for cp in passes[k]]:
        cp.wait_send()
    for cp in local:
        cp.wait()


def _exchange_out_shape(a, gather):
    return jax.ShapeDtypeStruct((N_DEV,) + tuple(a.shape) if gather else tuple(a.shape), a.dtype)


def _exchange_sems(n):
    return [pltpu.SemaphoreType.DMA((7 * n,)), pltpu.SemaphoreType.DMA((7 * n,)), pltpu.SemaphoreType.DMA((n,))]


ANY = pl.BlockSpec(memory_space=pl.ANY)


def _exchange(arrs, gather, name, relay=False):
    n = len(arrs)

    def body(*refs):
        ins, outs = refs[:n], refs[n:2 * n]
        if relay:
            _relay_start(ins, outs, *refs[2 * n:])
            _relay_wait(ins, outs, *refs[2 * n:])
        else:
            _exchange_start(ins, outs, *refs[2 * n:], gather)
            _exchange_wait(ins, outs, *refs[2 * n:], gather)

    return pl.pallas_call(
        body, name=name, out_shape=tuple(_exchange_out_shape(a, gather) for a in arrs),
        in_specs=[ANY] * n, out_specs=tuple([ANY] * n), scratch_shapes=_exchange_sems(n),
    )(*arrs)


def all_gather(arrs, name, relay=False):
    return _exchange(arrs, True, name, relay)


def all_to_all(arrs, name):
    return _exchange(arrs, False, name)


class Riders:
    def __init__(self, arrs, gather):
        self.gather = gather
        self.rows = [a[1] if isinstance(a, tuple) else None for a in arrs]
        self.arrs = [a[0] if isinstance(a, tuple) else a for a in arrs]

    def out_shapes(self):
        shapes = []
        for a, rows in zip(self.arrs, self.rows):
            shape = tuple(a.shape) if rows is None else (a.shape[0], rows[1], a.shape[2])
            shapes.append(jax.ShapeDtypeStruct((N_DEV,) + shape if self.gather else shape, a.dtype))
        return shapes


def _call(body, *, name, grid, in_specs, out_specs, out_shape, semantics, scratch_shapes=(), riders=None):
    if riders is None or not riders.arrs:
        return pl.pallas_call(body, name=name, grid=grid, in_specs=in_specs, out_specs=tuple(out_specs),
                              out_shape=tuple(out_shape), scratch_shapes=list(scratch_shapes),
                              compiler_params=_cparams(*semantics))
    n_in, n_out, n_scr, n_r = len(in_specs), len(out_specs), len(scratch_shapes), len(riders.arrs)
    gather = riders.gather

    def hosted(*refs):
        ins, r_ins = refs[:n_in], refs[n_in:n_in + n_r]
        at = n_in + n_r
        outs, r_outs = refs[at:at + n_out], refs[at + n_out:at + n_out + n_r]
        at += n_out + n_r
        scratch, sems = refs[at:at + n_scr], refs[at + n_scr:]
        first = functools.reduce(jnp.logical_and, [pl.program_id(a) == 0 for a in range(len(grid))])
        last = functools.reduce(jnp.logical_and, [pl.program_id(a) == grid[a] - 1 for a in range(len(grid))])

        r_ins = [(ref, rows) if rows is not None else ref for ref, rows in zip(r_ins, riders.rows)]

        @pl.when(first)
        def _():
            if gather:
                _relay_start(r_ins, r_outs, *sems)
            else:
                _exchange_start(r_ins, r_outs, *sems, gather)

        body(*ins, *outs, *scratch)

        @pl.when(last)
        def _():
            if gather:
                _relay_wait(r_ins, r_outs, *sems)
            else:
                _exchange_wait(r_ins, r_outs, *sems, gather)

    call = pl.pallas_call(
        hosted, name=name, grid=grid, in_specs=list(in_specs) + [ANY] * n_r, out_specs=tuple(out_specs) + (ANY,) * n_r,
        out_shape=tuple(out_shape) + tuple(riders.out_shapes()),
        scratch_shapes=list(scratch_shapes) + _exchange_sems(n_r),
        compiler_params=_cparams(*(("arbitrary",) * len(grid))))
    return lambda *args: call(*args, *riders.arrs)


def _shards_per_step(shape):
    _, k, n = shape
    best = None
    for q in (1, 2, 4, 8):
        if (q * n) % LANES == 0 and (best is None or 2 * k * q * n <= WEIGHT_BLOCK_BYTES):
            best = q
    return best


def mm_nn(a, b, out_dtype, name, tm=512, riders=None, transposed=False):
    m, k = a.shape
    tm = _tile(m, tm, 8)
    if b.ndim == 3:
        shard = b.shape[1] if transposed else b.shape[2]
        n = N_DEV * shard
        per_step = _shards_per_step((N_DEV, k, shard))
        tn = per_step * shard
        b_spec = pl.BlockSpec((per_step,) + b.shape[1:], lambda i, j: (j, 0, 0))
        contract = (((1,), (1,)), ((), ())) if transposed else (((1,), (0,)), ((), ()))

        def body(a_ref, b_ref, o_ref):
            for q in range(per_step):
                o_ref[:, q * shard:(q + 1) * shard] = lax.dot_general(
                    a_ref[...], b_ref[q], contract, preferred_element_type=f32).astype(o_ref.dtype)
    else:
        n = b.shape[1]
        tn = _weight_tile(n, k)
        b_spec = pl.BlockSpec((k, tn), lambda i, j: (0, j))

        def body(a_ref, b_ref, o_ref):
            o_ref[...] = jnp.dot(a_ref[...], b_ref[...], preferred_element_type=f32).astype(o_ref.dtype)

    return _call(
        body, name=name, grid=(m // tm, n // tn),
        in_specs=[pl.BlockSpec((tm, k), lambda i, j: (i, 0)), b_spec],
        out_specs=[pl.BlockSpec((tm, tn), lambda i, j: (i, j))],
        out_shape=[jax.ShapeDtypeStruct((m, n), out_dtype)], semantics=("parallel", "parallel"), riders=riders,
    )(a, b)


def mm_nn_residual(a, b, x, gate, norm, name, tm=512, riders=None):
    m, k = a.shape
    n = b.shape[1]
    tm = _tile(m, tm, 8)

    def body(a_ref, b_ref, x_ref, g_ref, *rest):
        if norm is not None:
            gn_ref, sc_ref, sh_ref, y_ref, o_ref, h_ref = rest
        else:
            y_ref, o_ref = rest
        y = jnp.dot(a_ref[...], b_ref[...], preferred_element_type=f32)
        y_ref[...] = y.astype(bf16)
        x_new = x_ref[...] + g_ref[...] * y
        o_ref[...] = x_new
        if norm is not None:
            h_ref[...] = _norm_fn(x_new, gn_ref[...], sc_ref[...], sh_ref[...]).astype(bf16)

    blk = pl.BlockSpec((tm, n), lambda i: (i, 0))
    in_specs = [pl.BlockSpec((tm, k), lambda i: (i, 0)), pl.BlockSpec((k, n), lambda i: (0, 0)), blk, _row(n)]
    out_specs = [blk, blk]
    out_shape = [jax.ShapeDtypeStruct((m, n), bf16), jax.ShapeDtypeStruct((m, n), f32)]
    args = [a, b, x, gate]
    if norm is not None:
        in_specs += [_row(n)] * 3
        out_specs += [blk]
        out_shape += [jax.ShapeDtypeStruct((m, n), bf16)]
        args += list(norm)
    return _call(body, name=name, grid=(m // tm,), in_specs=in_specs, out_specs=out_specs, out_shape=out_shape,
                 semantics=("parallel",), riders=riders)(*args)


def mm_nt(a, b, out_dtype, name, tm=512, riders=None, transposed=False):
    m = a.shape[0]
    tm = _tile(m, tm, 8)
    if b.ndim == 3:
        shard, k = (b.shape[1], b.shape[2]) if transposed else (b.shape[2], b.shape[1])
        tk = k
        b_spec = pl.BlockSpec(b.shape, lambda i, j: (0, 0, 0))
        width = N_DEV * shard
        contract = (((1,), (0,)), ((), ())) if transposed else (((1,), (1,)), ((), ()))

        def body(a_ref, b_ref, o_ref):
            acc = None
            for q in range(N_DEV):
                part = lax.dot_general(a_ref[:, q * shard:(q + 1) * shard], b_ref[q], contract, preferred_element_type=f32)
                acc = part if acc is None else acc + part
            o_ref[...] = acc.astype(o_ref.dtype)
    else:
        k, width = b.shape
        tk = _weight_tile(k, width)
        b_spec = pl.BlockSpec((tk, width), lambda i, j: (j, 0))

        def body(a_ref, b_ref, o_ref):
            o_ref[...] = lax.dot_general(a_ref[...], b_ref[...], (((1,), (1,)), ((), ())),
                                         preferred_element_type=f32).astype(o_ref.dtype)

    return _call(
        body, name=name, grid=(m // tm, k // tk),
        in_specs=[pl.BlockSpec((tm, width), lambda i, j: (i, 0)), b_spec],
        out_specs=[pl.BlockSpec((tm, tk), lambda i, j: (i, j))],
        out_shape=[jax.ShapeDtypeStruct((m, k), out_dtype)], semantics=("parallel", "parallel"), riders=riders,
    )(a, b)


def mm_tn(a, b, out_dtype, name, tm=512, tn=512, riders=None):
    t, m = a.shape
    n = b.shape[1]
    tm, tn = _tile(m, tm), _tile(n, tn)

    def body(a_ref, b_ref, o_ref):
        o_ref[...] = lax.dot_general(a_ref[...], b_ref[...], (((0,), (0,)), ((), ())),
                                     preferred_element_type=f32).astype(o_ref.dtype)

    return _call(
        body, name=name, grid=(m // tm, n // tn),
        in_specs=[pl.BlockSpec((t, tm), lambda i, j: (0, i)), pl.BlockSpec((t, tn), lambda i, j: (0, j))],
        out_specs=[pl.BlockSpec((tm, tn), lambda i, j: (i, j))],
        out_shape=[jax.ShapeDtypeStruct((m, n), out_dtype)], semantics=("parallel", "parallel"), riders=riders,
    )(a, b)


def mm_tn_by_owner(a, b, name, tm=512, riders=None):
    t, m = a.shape
    n = b.shape[1]
    shard = n // N_DEV
    per_step = 1 if shard % LANES == 0 else 2
    assert (per_step * shard) % LANES == 0
    tm = _tile(m, tm)

    def body(a_ref, b_ref, o_ref):
        acc = lax.dot_general(a_ref[...], b_ref[...], (((0,), (0,)), ((), ())), preferred_element_type=f32)
        for q in range(per_step):
            o_ref[q] = acc[:, q * shard:(q + 1) * shard].astype(bf16)

    return _call(
        body, name=name, grid=(m // tm, N_DEV // per_step),
        in_specs=[pl.BlockSpec((t, tm), lambda i, j: (0, i)), pl.BlockSpec((t, per_step * shard), lambda i, j: (0, j))],
        out_specs=[pl.BlockSpec((per_step, tm, shard), lambda i, j: (j, i, 0))],
        out_shape=[jax.ShapeDtypeStruct((N_DEV, m, shard), bf16)], semantics=("parallel", "parallel"), riders=riders,
    )(a, b)


def _norm_fn(x, gn, sc, sh):
    r = lax.rsqrt(jnp.mean(x * x, axis=-1, keepdims=True) + EPS)
    return (x * r * gn) * (1.0 + sc) + sh


def _row(d):
    return pl.BlockSpec((1, d), lambda i: (0, 0))


def norm_fwd(x, gn, sc, sh, name, tm=512):
    t, d = x.shape
    tm = _tile(t, tm, 8)

    def body(x_ref, gn_ref, sc_ref, sh_ref, h_ref):
        h_ref[...] = _norm_fn(x_ref[...], gn_ref[...], sc_ref[...], sh_ref[...]).astype(bf16)

    return pl.pallas_call(
        body, name=name, grid=(t // tm,),
        in_specs=[pl.BlockSpec((tm, d), lambda i: (i, 0)), _row(d), _row(d), _row(d)],
        out_specs=pl.BlockSpec((tm, d), lambda i: (i, 0)),
        out_shape=jax.ShapeDtypeStruct((t, d), bf16), compiler_params=_cparams("parallel"),
    )(x, gn, sc, sh)


def _gate_bwd(dx, y_ref, g_ref, dgate_ref, dy_ref):
    dgate_ref[...] += jnp.sum(dx * y_ref[...].astype(f32), axis=0, keepdims=True)
    dy_ref[...] = (dx * g_ref[...]).astype(bf16)


def norm_bwd(x, gn, sc, sh, dh, dres, name, gate=None, tm=512, riders=None):
    t, d = x.shape
    tm = _tile(t, tm, 8)

    def body(x_ref, gn_ref, sc_ref, sh_ref, dh_ref, dres_ref, *rest):
        if gate is not None:
            y_ref, g_ref, dx_ref, dgn_ref, dsc_ref, dsh_ref, dgate_ref, dy_ref = rest
        else:
            dx_ref, dgn_ref, dsc_ref, dsh_ref = rest

        @pl.when(pl.program_id(0) == 0)
        def _():
            dgn_ref[...] = jnp.zeros_like(dgn_ref)
            dsc_ref[...] = jnp.zeros_like(dsc_ref)
            dsh_ref[...] = jnp.zeros_like(dsh_ref)
            if gate is not None:
                dgate_ref[...] = jnp.zeros_like(dgate_ref)

        _, vjp = jax.vjp(_norm_fn, x_ref[...], gn_ref[...], sc_ref[...], sh_ref[...])
        dx, dgn, dsc, dsh = vjp(dh_ref[...].astype(f32))
        dx = dx + dres_ref[...]
        dx_ref[...] = dx
        dgn_ref[...] += dgn
        dsc_ref[...] += dsc
        dsh_ref[...] += dsh
        if gate is not None:
            _gate_bwd(dx, y_ref, g_ref, dgate_ref, dy_ref)

    blk = pl.BlockSpec((tm, d), lambda i: (i, 0))
    vec = jax.ShapeDtypeStruct((1, d), f32)
    in_specs = [blk, _row(d), _row(d), _row(d), blk, blk]
    out_specs = [blk, _row(d), _row(d), _row(d)]
    out_shape = [jax.ShapeDtypeStruct((t, d), f32), vec, vec, vec]
    args = [x, gn, sc, sh, dh, dres]
    if gate is not None:
        in_specs += [blk, _row(d)]
        out_specs += [_row(d), blk]
        out_shape += [vec, jax.ShapeDtypeStruct((t, d), bf16)]
        args += list(gate)
    return _call(body, name=name, grid=(t // tm,), in_specs=in_specs, out_specs=out_specs, out_shape=out_shape,
                 semantics=("arbitrary",), riders=riders)(*args)


def _loss_fn(x, g, tgt):
    r = lax.rsqrt(jnp.mean(x * x, axis=-1, keepdims=True) + EPS)
    err = jnp.square(x * r * g - tgt)
    return 0.5 * jnp.sum(jnp.mean(err, axis=-1, keepdims=True), axis=0, keepdims=True)


def loss_head(x, g, tgt, y, gate, name, tm=512):
    t, d = x.shape
    tm = _tile(t, tm, 8)

    def body(x_ref, g_ref, t_ref, y_ref, gate_ref, loss_ref, dx_ref, dg_ref, dgate_ref, dy_ref):
        @pl.when(pl.program_id(0) == 0)
        def _():
            loss_ref[...] = jnp.zeros_like(loss_ref)
            dg_ref[...] = jnp.zeros_like(dg_ref)
            dgate_ref[...] = jnp.zeros_like(dgate_ref)

        loss, vjp = jax.vjp(_loss_fn, x_ref[...], g_ref[...], t_ref[...])
        dx, dg, _ = vjp(jnp.ones((1, 1), f32))
        dx_ref[...] = dx
        loss_ref[...] += loss
        dg_ref[...] += dg
        _gate_bwd(dx, y_ref, gate_ref, dgate_ref, dy_ref)

    blk = pl.BlockSpec((tm, d), lambda i: (i, 0))
    vec = jax.ShapeDtypeStruct((1, d), f32)
    return pl.pallas_call(
        body, name=name, grid=(t // tm,),
        in_specs=[blk, _row(d), blk, blk, _row(d)],
        out_specs=(pl.BlockSpec((1, 1), lambda i: (0, 0)), blk, _row(d), _row(d), blk),
        out_shape=(jax.ShapeDtypeStruct((1, 1), f32), jax.ShapeDtypeStruct((t, d), f32), vec, vec,
                   jax.ShapeDtypeStruct((t, d), bf16)),
        compiler_params=_cparams("arbitrary"),
    )(x, g, tgt, y, gate)


def _gm_block_fn(z, ws, bs, lng, lnb):
    u = _gelu(z[:, :GM_WIDTH])
    vg = _gelu(z[:, GM_WIDTH:])
    mu = jnp.mean(vg, axis=-1, keepdims=True)
    var = jnp.mean(jnp.square(vg - mu), axis=-1, keepdims=True)
    vn = (vg - mu) * lax.rsqrt(var + EPS) * lng + lnb
    row = lax.broadcasted_iota(jnp.int32, (GM_BLOCK, GM_BLOCK), 0) // CHUNK
    col = lax.broadcasted_iota(jnp.int32, (GM_BLOCK, GM_BLOCK), 1) // CHUNK
    parts = []
    for h in range(GM_HEADS):
        w = jnp.where(row >= col, ws[h], 0.0)
        cols = slice(h * GM_HEAD_DIM, (h + 1) * GM_HEAD_DIM)
        s = jnp.dot(w.astype(bf16), vn[:, cols].astype(bf16), preferred_element_type=f32) + bs[h]
        parts.append(u[:, cols] * s)
    return jnp.concatenate(parts, axis=1)


def _gm_param_specs():
    return [pl.BlockSpec((GM_HEADS, GM_BLOCK, GM_BLOCK), lambda i: (0, 0, 0)),
            pl.BlockSpec((GM_HEADS, GM_BLOCK, 1), lambda i: (0, 0, 0)), _row(GM_WIDTH), _row(GM_WIDTH)]


def gm_mix_fwd(z, ws, bs, lng, lnb, name, riders=None):
    t = z.shape[0]

    def body(z_ref, ws_ref, bs_ref, lng_ref, lnb_ref, o_ref):
        o_ref[...] = _gm_block_fn(z_ref[...].astype(f32), ws_ref[...], bs_ref[...], lng_ref[...],
                                  lnb_ref[...]).astype(bf16)

    return _call(
        body, name=name, grid=(t // GM_BLOCK,),
        in_specs=[pl.BlockSpec((GM_BLOCK, 2 * GM_WIDTH), lambda i: (i, 0))] + _gm_param_specs(),
        out_specs=[pl.BlockSpec((GM_BLOCK, GM_WIDTH), lambda i: (i, 0))],
        out_shape=[jax.ShapeDtypeStruct((t, GM_WIDTH), bf16)], semantics=("parallel",), riders=riders,
    )(z, ws, bs, lng, lnb)


def gm_mix_bwd(z, ws, bs, lng, lnb, dgated, name, riders=None):
    t = z.shape[0]

    def body(z_ref, ws_ref, bs_ref, lng_ref, lnb_ref, dg_ref, dz_ref, dws_ref, dbs_ref, dlng_ref, dlnb_ref):
        _, vjp = jax.vjp(_gm_block_fn, z_ref[...].astype(f32), ws_ref[...], bs_ref[...], lng_ref[...], lnb_ref[...])
        dz, dws, dbs, dlng, dlnb = vjp(dg_ref[...].astype(f32))
        dz_ref[...] = dz.astype(bf16)

        @pl.when(pl.program_id(0) == 0)
        def _():
            dws_ref[...] = jnp.zeros_like(dws_ref)
            dbs_ref[...] = jnp.zeros_like(dbs_ref)
            dlng_ref[...] = jnp.zeros_like(dlng_ref)
            dlnb_ref[...] = jnp.zeros_like(dlnb_ref)

        dws_ref[...] += dws
        dbs_ref[...] += dbs
        dlng_ref[...] += dlng
        dlnb_ref[...] += dlnb

    zblk = pl.BlockSpec((GM_BLOCK, 2 * GM_WIDTH), lambda i: (i, 0))
    return _call(
        body, name=name, grid=(t // GM_BLOCK,),
        in_specs=[zblk] + _gm_param_specs() + [pl.BlockSpec((GM_BLOCK, GM_WIDTH), lambda i: (i, 0))],
        out_specs=[zblk] + _gm_param_specs(),
        out_shape=[jax.ShapeDtypeStruct((t, 2 * GM_WIDTH), bf16),
                   jax.ShapeDtypeStruct((GM_HEADS, GM_BLOCK, GM_BLOCK), f32),
                   jax.ShapeDtypeStruct((GM_HEADS, GM_BLOCK, 1), f32),
                   jax.ShapeDtypeStruct((1, GM_WIDTH), f32), jax.ShapeDtypeStruct((1, GM_WIDTH), f32)],
        semantics=("arbitrary",), riders=riders,
    )(z, ws, bs, lng, lnb, dgated)


@functools.partial(jax.custom_vjp, nondiff_argnums=(1,))
def _rows_up(x, shift):
    return x if shift == 0 else pltpu.roll(x, x.shape[1] - shift, axis=1)


def _rows_up_fwd(x, shift):
    return _rows_up(x, shift), None


def _rows_up_bwd(shift, _, g):
    return (g if shift == 0 else pltpu.roll(g, shift, axis=1),)


_rows_up.defvjp(_rows_up_fwd, _rows_up_bwd)


def _hg_block_fn(qp, fz, iv, gp, s0, lb, gn):
    n, ns, d = HG_SUB, HG_TOKENS // HG_SUB, HG_DIM
    p, nb, per_sub = HG_PAIR, HG_TOKENS // HG_PAIR, HG_SUB // HG_PAIR
    f = lb + (1.0 - lb) * jax.nn.sigmoid(fz)
    g = jnp.log(f)
    k = 1.0 - f
    q = qp * jax.nn.sigmoid(qp)
    v = iv.astype(bf16)
    row = lax.broadcasted_iota(jnp.int32, (HG_TOKENS, HG_TOKENS), 0)
    col = lax.broadcasted_iota(jnp.int32, (HG_TOKENS, HG_TOKENS), 1)
    same_sub = col // n == row // n
    tri = ((col <= row) & same_sub).astype(f32)
    cum = jnp.dot(tri, g, precision=lax.Precision.HIGHEST, preferred_element_type=f32)
    cum_b, q_b, k_b, f_b = cum.reshape(nb, p, d), q.reshape(nb, p, d), k.reshape(nb, p, d), f.reshape(nb, p, d)
    j_b = lax.broadcasted_iota(jnp.int32, (nb, p, d), 1)
    j_col = lax.broadcasted_iota(jnp.int32, (nb, p, 1), 1)
    scores_t = jnp.zeros((HG_TOKENS, HG_TOKENS), f32)
    weight = k_b
    for delta in range(p):
        if delta:
            weight = weight * _rows_up(f_b, delta)
        pair = jnp.sum(_rows_up(q_b, delta) * weight, axis=2, keepdims=True)
        pair = jnp.where(j_col < p - delta, pair, 0.0)
        scores_t = scores_t + jnp.where(col == row + delta, pair.reshape(HG_TOKENS, 1), 0.0)
    o = lax.dot_general(scores_t.astype(bf16), v, (((0,), (0,)), ((), ())), preferred_element_type=f32)
    last = cum_b[:, p - 1:p, :]
    before = jnp.concatenate([jnp.zeros((1, 1, d), f32), last[:-1]], axis=0)
    before = jnp.broadcast_to(before, (nb, p, d)).reshape(HG_TOKENS, d)
    block = (lax.broadcasted_iota(jnp.int32, (HG_TOKENS, d), 0) // p) % per_sub
    q_late = q * jnp.exp(jnp.where(block > 0, cum - before, -1e30))
    last_s = last.reshape(ns, per_sub, d)
    q_parts, k_parts = [], []
    for m in range(1, per_sub):
        split = jnp.broadcast_to(last_s[:, m - 1:m, :], (ns, n, d)).reshape(HG_TOKENS, d)
        k_parts.append(k * jnp.exp(jnp.where(block < m, split - cum, -1e30)))
        q_parts.append(jnp.where(block == m, q_late, 0.0))
    scores = lax.dot_general(jnp.concatenate(q_parts, axis=1).astype(bf16), jnp.concatenate(k_parts, axis=1).astype(bf16),
                             (((1,), (1,)), ((), ())), preferred_element_type=f32)
    o = o + jnp.dot(jnp.where(same_sub, scores, 0.0).astype(bf16), v, preferred_element_type=f32)
    cum_s = cum.reshape(ns, n, d)
    tot = cum_s[:, n - 1:n, :]
    kt_t = (k.reshape(ns, n, d) * jnp.exp(tot - cum_s)).reshape(HG_TOKENS, d).T
    lane_sub = lax.broadcasted_iota(jnp.int32, (d, HG_TOKENS), 1) // n
    k_by_sub = jnp.concatenate([jnp.where(lane_sub == b, kt_t, 0.0) for b in range(ns)], axis=0).astype(bf16)
    update = jnp.dot(k_by_sub, v, preferred_element_type=f32)
    decay = jnp.exp(tot.reshape(ns, d)).T
    state = s0
    states = []
    for a in range(ns):
        states.append(state.astype(bf16))
        state = decay[:, a:a + 1] * state + update[a * d:(a + 1) * d]
    qt = q * jnp.exp(cum)
    row_sub = lax.broadcasted_iota(jnp.int32, (HG_TOKENS, d), 0) // n
    q_by_sub = jnp.concatenate([jnp.where(row_sub == a, qt, 0.0) for a in range(ns)], axis=1).astype(bf16)
    o = o + jnp.dot(q_by_sub, jnp.concatenate(states, axis=0), preferred_element_type=f32)
    on = o * lax.rsqrt(jnp.mean(o * o, axis=-1, keepdims=True) + EPS) * gn
    return on * (gp * jax.nn.sigmoid(gp)), state


def _head_parts(ref, h):
    return [ref[:, p * D + h * HG_DIM:p * D + (h + 1) * HG_DIM] for p in range(4)]


def hg_scan_fwd(proj, lb, gn, name, riders=None):
    t = proj.shape[0]
    nt = t // HG_TOKENS

    def body(p_ref, lb_ref, gn_ref, y_ref, s_ref, state):
        @pl.when(pl.program_id(0) == 0)
        def _():
            state[...] = jnp.zeros_like(state)

        for h in range(HG_HEADS):
            cols = slice(h * HG_DIM, (h + 1) * HG_DIM)
            s_ref[h, 0] = state[h]
            y, s1 = _hg_block_fn(*_head_parts(p_ref, h), state[h], lb_ref[:, cols], gn_ref[:, cols])
            y_ref[:, cols] = y.astype(bf16)
            state[h] = s1

    return _call(
        body, name=name, grid=(nt,),
        in_specs=[pl.BlockSpec((HG_TOKENS, 4 * D), lambda i: (i, 0)), _row(D), _row(D)],
        out_specs=[pl.BlockSpec((HG_TOKENS, D), lambda i: (i, 0)),
                   pl.BlockSpec((HG_HEADS, 1, HG_DIM, HG_DIM), lambda i: (0, i, 0, 0))],
        out_shape=[jax.ShapeDtypeStruct((t, D), bf16), jax.ShapeDtypeStruct((HG_HEADS, nt, HG_DIM, HG_DIM), f32)],
        scratch_shapes=[pltpu.VMEM((HG_HEADS, HG_DIM, HG_DIM), f32)],
        semantics=("arbitrary",), riders=riders,
    )(proj, lb, gn)


def hg_scan_bwd(proj, lb, gn, states, dy, name, riders=None):
    t = proj.shape[0]
    nt = t // HG_TOKENS

    def body(p_ref, lb_ref, gn_ref, s_ref, dy_ref, dp_ref, dlb_ref, dgn_ref, dstate):
        @pl.when(pl.program_id(0) == 0)
        def _():
            dstate[...] = jnp.zeros_like(dstate)
            dlb_ref[...] = jnp.zeros_like(dlb_ref)
            dgn_ref[...] = jnp.zeros_like(dgn_ref)

        for h in range(HG_HEADS):
            cols = slice(h * HG_DIM, (h + 1) * HG_DIM)
            _, vjp = jax.vjp(_hg_block_fn, *_head_parts(p_ref, h), s_ref[h, 0], lb_ref[:, cols], gn_ref[:, cols])
            grads = vjp((dy_ref[:, cols].astype(f32), dstate[h]))
            for p in range(4):
                dp_ref[:, p * D + h * HG_DIM:p * D + (h + 1) * HG_DIM] = grads[p].astype(bf16)
            dstate[h] = grads[4]
            dlb_ref[:, cols] += grads[5]
            dgn_ref[:, cols] += grads[6]

    small = jax.ShapeDtypeStruct((1, D), f32)
    return _call(
        body, name=name, grid=(nt,),
        in_specs=[pl.BlockSpec((HG_TOKENS, 4 * D), lambda i: (nt - 1 - i, 0)), _row(D), _row(D),
                  pl.BlockSpec((HG_HEADS, 1, HG_DIM, HG_DIM), lambda i: (0, nt - 1 - i, 0, 0)),
                  pl.BlockSpec((HG_TOKENS, D), lambda i: (nt - 1 - i, 0))],
        out_specs=[pl.BlockSpec((HG_TOKENS, 4 * D), lambda i: (nt - 1 - i, 0)), _row(D), _row(D)],
        out_shape=[jax.ShapeDtypeStruct((t, 4 * D), bf16), small, small],
        scratch_shapes=[pltpu.VMEM((HG_HEADS, HG_DIM, HG_DIM), f32)],
        semantics=("arbitrary",), riders=riders,
    )(proj, lb, gn, states, dy)


FFN_COLS = 1408
HALO = 8
STRIP = 16


def _ffn_specs(tm):
    nb = tm // HALO
    main_g = pl.BlockSpec((tm, FFN_COLS), lambda j, i: (i, j))
    main_v = pl.BlockSpec((tm, FFN_COLS), lambda j, i: (i, j + 2))
    halo_g = pl.BlockSpec((HALO, FFN_COLS), lambda j, i: (jnp.maximum(i * nb - 1, 0), j))
    halo_v = pl.BlockSpec((HALO, FFN_COLS), lambda j, i: (jnp.maximum(i * nb - 1, 0), j + 2))
    w_g = pl.BlockSpec((3, FFN_COLS), lambda j, i: (0, j))
    w_v = pl.BlockSpec((3, FFN_COLS), lambda j, i: (0, j + 2))
    b_g = pl.BlockSpec((1, FFN_COLS), lambda j, i: (0, j))
    b_v = pl.BlockSpec((1, FFN_COLS), lambda j, i: (0, j + 2))
    return [main_g, halo_g, main_v, halo_v, w_g, w_v, b_g, b_v]


def _strip_rows(r):
    return pl.ds(r * STRIP, STRIP) if isinstance(r, int) else pl.ds(pl.multiple_of(r * STRIP, STRIP), STRIP)


def _for_strips(nstrip, strip, reverse=False):
    if reverse:
        strip(nstrip - 1, True)
        lax.fori_loop(0, nstrip - 1, lambda k, c: (strip(nstrip - 2 - k, False), c)[1], 0)
    else:
        strip(0, True)
        lax.fori_loop(1, nstrip, lambda r, c: (strip(r, False), c)[1], 0)


SUBLANES = 8


def _rows_down(prev, cur, shift):
    row = lax.broadcasted_iota(jnp.int32, (SUBLANES, LANES), 0)
    tiles = [prev[STRIP - SUBLANES:]] + [cur[q * SUBLANES:(q + 1) * SUBLANES] for q in range(STRIP // SUBLANES)]
    turned = [pltpu.roll(x, shift, axis=0) for x in tiles]
    return jnp.concatenate([jnp.where(row < shift, turned[q], turned[q + 1]) for q in range(STRIP // SUBLANES)], axis=0)


def _rows_ahead(cur, nxt, shift):
    row = lax.broadcasted_iota(jnp.int32, (SUBLANES, LANES), 0)
    tiles = [cur[q * SUBLANES:(q + 1) * SUBLANES] for q in range(STRIP // SUBLANES)] + [nxt[:SUBLANES]]
    turned = [pltpu.roll(x, SUBLANES - shift, axis=0) for x in tiles]
    return jnp.concatenate([jnp.where(row >= SUBLANES - shift, turned[q + 1], turned[q])
                            for q in range(STRIP // SUBLANES)], axis=0)


def _conv_strip(main_ref, halo_ref, w_ref, b_ref, r, edge, cols, rowi):
    cur = main_ref[_strip_rows(r), cols].astype(f32)
    if edge:
        h = jnp.where(pl.program_id(1) == 0, 0.0, halo_ref[:, cols].astype(f32))
        prev = jnp.concatenate([jnp.zeros_like(h), h], axis=0)
    else:
        prev = main_ref[_strip_rows(r - 1), cols].astype(f32)
    a1, a2 = _rows_down(prev, cur, 1), _rows_down(prev, cur, 2)
    y = b_ref[:, cols] + w_ref[0:1, cols] * a2 + w_ref[1:2, cols] * a1 + w_ref[2:3, cols] * cur
    return y, (cur, a1, a2)


def ffn_gate_fwd(a, cw, cb, name, tm=512, riders=None):
    t = a.shape[0]
    tm = _tile(t, tm, STRIP)

    def body(ag_ref, hg_ref, av_ref, hv_ref, wg_ref, wv_ref, bg_ref, bv_ref, o_ref):
        rowi = lax.broadcasted_iota(jnp.int32, (STRIP, LANES), 0)

        def strip(r, edge):
            for c in range(FFN_COLS // LANES):
                cols = pl.ds(c * LANES, LANES)
                yg, _ = _conv_strip(ag_ref, hg_ref, wg_ref, bg_ref, r, edge, cols, rowi)
                yv, _ = _conv_strip(av_ref, hv_ref, wv_ref, bv_ref, r, edge, cols, rowi)
                o_ref[_strip_rows(r), cols] = (_gelu(yg) * yv).astype(bf16)

        _for_strips(tm // STRIP, strip)

    return _call(
        body, name=name, grid=(2, t // tm), in_specs=_ffn_specs(tm),
        out_specs=[pl.BlockSpec((tm, FFN_COLS), lambda j, i: (i, j))],
        out_shape=[jax.ShapeDtypeStruct((t, FFN_HIDDEN), bf16)],
        semantics=("parallel", "arbitrary"), riders=riders,
    )(a, a, a, a, cw, cw, cb, cb)


def ffn_gate_bwd(a, cw, cb, dhid, name, tm=512, riders=None):
    t = a.shape[0]
    tm = _tile(t, tm, STRIP)

    def body(ag_ref, hg_ref, av_ref, hv_ref, wg_ref, wv_ref, bg_ref, bv_ref, dh_ref,
             dy_ref, dwg_ref, dwv_ref, dbg_ref, dbv_ref, acc):
        rowi = lax.broadcasted_iota(jnp.int32, (STRIP, LANES), 0)

        @pl.when(pl.program_id(1) == 0)
        def _():
            acc[...] = jnp.zeros_like(acc)

        def strip(r, edge):
            rows = _strip_rows(r)
            for c in range(FFN_COLS // LANES):
                cols = pl.ds(c * LANES, LANES)
                yg, taps_g = _conv_strip(ag_ref, hg_ref, wg_ref, bg_ref, r, edge, cols, rowi)
                yv, taps_v = _conv_strip(av_ref, hv_ref, wv_ref, bv_ref, r, edge, cols, rowi)
                dh = dh_ref[rows, cols].astype(f32)
                cdf = 0.5 * (1.0 + lax.erf(yg * (1.0 / math.sqrt(2.0))))
                pdf = jnp.exp(-0.5 * yg * yg) * (1.0 / math.sqrt(2.0 * math.pi))
                dyg = dh * yv * (cdf + yg * pdf)
                dyv = dh * (yg * cdf)
                dy_ref[0, rows, cols] = dyg.astype(bf16)
                dy_ref[1, rows, cols] = dyv.astype(bf16)
                for p, (dy, (a0, a1, a2)) in enumerate(((dyg, taps_g), (dyv, taps_v))):
                    acc[4 * p + 0, :, cols] += dy * a2
                    acc[4 * p + 1, :, cols] += dy * a1
                    acc[4 * p + 2, :, cols] += dy * a0
                    acc[4 * p + 3, :, cols] += dy

        _for_strips(tm // STRIP, strip)

        @pl.when(pl.program_id(1) == pl.num_programs(1) - 1)
        def _():
            for p, (dw_ref, db_ref) in enumerate(((dwg_ref, dbg_ref), (dwv_ref, dbv_ref))):
                for tap in range(3):
                    dw_ref[tap:tap + 1, :] = jnp.sum(acc[4 * p + tap], axis=0, keepdims=True)
                db_ref[...] = jnp.sum(acc[4 * p + 3], axis=0, keepdims=True)

    half_w = pl.BlockSpec((3, FFN_COLS), lambda j, i: (0, j))
    half_b = pl.BlockSpec((1, FFN_COLS), lambda j, i: (0, j))
    return _call(
        body, name=name, grid=(2, t // tm),
        in_specs=_ffn_specs(tm) + [pl.BlockSpec((tm, FFN_COLS), lambda j, i: (i, j))],
        out_specs=[pl.BlockSpec((2, tm, FFN_COLS), lambda j, i: (0, i, j)), half_w, half_w, half_b, half_b],
        out_shape=[jax.ShapeDtypeStruct((2, t, FFN_HIDDEN), bf16),
                   jax.ShapeDtypeStruct((3, FFN_HIDDEN), f32), jax.ShapeDtypeStruct((3, FFN_HIDDEN), f32),
                   jax.ShapeDtypeStruct((1, FFN_HIDDEN), f32), jax.ShapeDtypeStruct((1, FFN_HIDDEN), f32)],
        scratch_shapes=[pltpu.VMEM((8, STRIP, FFN_COLS), f32)],
        semantics=("parallel", "arbitrary"), riders=riders,
    )(a, a, a, a, cw, cw, cb, cb, dhid)


def conv_transpose(dy, cw, name, tm=512):
    _, t, fh = dy.shape
    tm = _tile(t, tm, STRIP)
    nb = tm // HALO
    last_halo = t // HALO - 1
    ncol = fh // FFN_COLS

    def body(main_ref, halo_ref, w_ref, o_ref):
        rowi = lax.broadcasted_iota(jnp.int32, (STRIP, LANES), 0)
        last_block = pl.program_id(2) == pl.num_programs(2) - 1

        def strip(r, edge):
            rows = _strip_rows(r)
            for c in range(FFN_COLS // LANES):
                cols = pl.ds(c * LANES, LANES)
                cur = main_ref[0, rows, cols].astype(f32)
                if edge:
                    h = jnp.where(last_block, 0.0, halo_ref[0, :, cols].astype(f32))
                    nxt = jnp.concatenate([h, jnp.zeros_like(h)], axis=0)
                else:
                    nxt = main_ref[0, _strip_rows(r + 1), cols].astype(f32)
                d1, d2 = _rows_ahead(cur, nxt, 1), _rows_ahead(cur, nxt, 2)
                o_ref[rows, cols] = (w_ref[2:3, cols] * cur + w_ref[1:2, cols] * d1 + w_ref[0:1, cols] * d2).astype(bf16)

        _for_strips(tm // STRIP, strip, reverse=True)

    return pl.pallas_call(
        body, name=name, grid=(2, ncol, t // tm),
        in_specs=[pl.BlockSpec((1, tm, FFN_COLS), lambda p, j, i: (p, i, j)),
                  pl.BlockSpec((1, HALO, FFN_COLS), lambda p, j, i: (p, jnp.minimum((i + 1) * nb, last_halo), j)),
                  pl.BlockSpec((3, FFN_COLS), lambda p, j, i: (0, p * ncol + j))],
        out_specs=pl.BlockSpec((tm, FFN_COLS), lambda p, j, i: (i, p * ncol + j)),
        out_shape=jax.ShapeDtypeStruct((t, 2 * fh), bf16),
        compiler_params=_cparams("parallel", "parallel", "arbitrary"),
    )(dy, dy, cw)


def ada_mod(c_all, ada_w, ada_b_cols, name):
    cols = ada_w.shape[2]

    def body(c_ref, w_ref, b_ref, o_ref):
        c = c_ref[...]
        cond = (c * jax.nn.sigmoid(c)).astype(bf16)
        o_ref[0] = jnp.dot(cond, w_ref[0].astype(bf16), preferred_element_type=f32) + b_ref[0]

    return pl.pallas_call(
        body, name=name, grid=(DEPTH,),
        in_specs=[pl.BlockSpec((N_DEV, D), lambda i: (0, 0)), pl.BlockSpec((1, D, cols), lambda i: (i, 0, 0)),
                  pl.BlockSpec((1, 1, cols), lambda i: (i, 0, 0))],
        out_specs=pl.BlockSpec((1, N_DEV, cols), lambda i: (i, 0, 0)),
        out_shape=jax.ShapeDtypeStruct((DEPTH, N_DEV, cols), f32), compiler_params=_cparams("parallel"),
    )(c_all, ada_w, ada_b_cols)


def ada_grads(c_all, dmod_cols, dmod_all, name):
    cols = dmod_cols.shape[2]

    def body(c_ref, dm_ref, da_ref, dw_ref, db_ref):
        c = c_ref[...]
        cond = c * jax.nn.sigmoid(c)
        dw_ref[0] = lax.dot_general(cond, dm_ref[0], (((0,), (0,)), ((), ())), precision=lax.Precision.HIGHEST,
                                    preferred_element_type=f32)
        acc = da_ref[0, 0]
        for e in range(1, N_DEV):
            acc = acc + da_ref[e, 0]
        db_ref[0] = acc

    return pl.pallas_call(
        body, name=name, grid=(DEPTH,),
        in_specs=[pl.BlockSpec((N_DEV, D), lambda i: (0, 0)), pl.BlockSpec((1, N_DEV, cols), lambda i: (i, 0, 0)),
                  pl.BlockSpec((N_DEV, 1, 1, 6 * D), lambda i: (0, i, 0, 0))],
        out_specs=(pl.BlockSpec((1, D, cols), lambda i: (i, 0, 0)), pl.BlockSpec((1, 1, 6 * D), lambda i: (i, 0, 0))),
        out_shape=(jax.ShapeDtypeStruct((DEPTH, D, cols), f32), jax.ShapeDtypeStruct((DEPTH, 1, 6 * D), f32)),
        compiler_params=_cparams("parallel"),
    )(c_all, dmod_cols, dmod_all)


def lower_bound_fwd(hg_lb, name):
    n = hg_lb.shape[1]

    def body(l_ref, o_ref):
        o_ref[...] = jax.nn.sigmoid(l_ref[1:2, :] - l_ref[0:1, :])

    return pl.pallas_call(body, name=name, out_shape=jax.ShapeDtypeStruct((1, n), f32))(hg_lb)


def lower_bound_bwd(hg_lb, dlb, name):
    n = hg_lb.shape[1]

    def body(l_ref, d_ref, o_ref):
        p = jax.nn.sigmoid(l_ref[1:2, :] - l_ref[0:1, :])
        g = d_ref[...] * p * (1.0 - p)
        o_ref[0:1, :] = -g
        o_ref[1:2, :] = g

    return pl.pallas_call(body, name=name, out_shape=jax.ShapeDtypeStruct((2, n), f32))(hg_lb, dlb)


def _adamw(w, g, m, v):
    m = ADAM_B1 * m + (1.0 - ADAM_B1) * g
    v = ADAM_B2 * v + (1.0 - ADAM_B2) * jnp.square(g)
    m_hat = m / (1.0 - ADAM_B1 ** ADAM_STEP)
    v_hat = v / (1.0 - ADAM_B2 ** ADAM_STEP)
    delta = -ADAM_LR * (m_hat / (jnp.sqrt(v_hat) + ADAM_EPS) + ADAM_WD * w)
    return delta, m, v


ADAM_BLOCK_BYTES = 32 * 1024 * 1024


def adam_reduced(parts, w, m, v, name):
    layers, r, c = w.shape
    outs = None
    for layer in range(layers):
        outs = _adam_layer(parts[layer], w, m, v, layer, outs, f"{name}_{layer}")
    return outs


def _adam_layer(parts, w, m, v, layer, prev, name):
    layers, r, c = w.shape
    rows = parts[0].shape[1]
    assert all(p.shape == (N_DEV, rows, c) for p in parts) and rows * len(parts) == r
    row_bytes = 2 * (len(parts) * N_DEV * c * 2 + 7 * c * 4)
    tr = _tile(rows, max(16, ADAM_BLOCK_BYTES // row_bytes), 16)
    steps = rows // tr
    n_prev = 0 if prev is None else 4

    def body(*refs):
        p_refs = refs[:len(parts)]
        w_ref, m_ref, v_ref = refs[len(parts):len(parts) + 3]
        g_ref, d_ref, mo_ref, vo_ref = refs[len(parts) + 3 + n_prev:]
        for idx in range(len(parts)):
            @pl.when(pl.program_id(0) == idx)
            def _():
                g = p_refs[idx][0].astype(f32)
                for j in range(1, N_DEV):
                    g = g + p_refs[idx][j].astype(f32)
                g_ref[...] = g
                d_ref[...], mo_ref[...], vo_ref[...] = _adamw(w_ref[...], g, m_ref[...], v_ref[...])

    def part_spec(idx):
        return pl.BlockSpec((N_DEV, tr, c), lambda p, i: (0, jnp.where(p == idx, i, 0), 0))

    blk = pl.BlockSpec((None, tr, c), lambda p, i: (layer, p * steps + i, 0))
    out = jax.ShapeDtypeStruct((layers, r, c), f32)
    n_in = len(parts) + 3
    return pl.pallas_call(
        body, name=name, grid=(len(parts), steps),
        in_specs=[part_spec(idx) for idx in range(len(parts))] + [blk, blk, blk] + [ANY] * n_prev,
        out_specs=(blk, blk, blk, blk), out_shape=(out, out, out, out),
        input_output_aliases={n_in + k: k for k in range(n_prev)},
        compiler_params=_cparams("arbitrary", "arbitrary"),
    )(*parts, w, m, v, *(prev or ()))


def adam_plain(g, w, m, v, name, tr=256):
    r, c = w.shape
    tr = _tile(r, tr, 8)

    def body(g_ref, w_ref, m_ref, v_ref, d_ref, mo_ref, vo_ref):
        d_ref[...], mo_ref[...], vo_ref[...] = _adamw(w_ref[...], g_ref[...], m_ref[...], v_ref[...])

    blk = pl.BlockSpec((tr, c), lambda i: (i, 0))
    out = jax.ShapeDtypeStruct((r, c), f32)
    return pl.pallas_call(
        body, name=name, grid=(r // tr,), in_specs=[blk, blk, blk, blk], out_specs=(blk, blk, blk),
        out_shape=(out, out, out), compiler_params=_cparams("parallel"),
    )(g, w, m, v)


def sum_parts(parts, name):
    _, r, c = parts.shape

    def body(p_ref, o_ref):
        acc = p_ref[0]
        for j in range(1, N_DEV):
            acc = acc + p_ref[j]
        o_ref[...] = acc

    return pl.pallas_call(body, name=name, out_shape=jax.ShapeDtypeStruct((r, c), f32))(parts)


def _pack(arrs, rows_mult=8):
    flat = jnp.concatenate([a.reshape(-1) for a in arrs])
    rows = -(-flat.shape[0] // LANES)
    rows = -(-rows // rows_mult) * rows_mult
    return jnp.pad(flat, (0, rows * LANES - flat.shape[0])).reshape(rows, LANES)


def _unpack(flat, shapes):
    out, at = [], 0
    for s in shapes:
        n = math.prod(s)
        out.append(flat[at:at + n].reshape(s))
        at += n
    return out


def kernel(x, c, gm_w_in, gm_ln_g, gm_ln_b, gm_w_s, gm_b_s, gm_w_out, hg_w_in, hg_lb, hg_gn_g, hg_w_out, ffn_w_up, ffn_conv_w, ffn_conv_b, ffn_w_down, norm_g, ada_w, ada_b, final_g, loss_target, m_gm_w_in, m_gm_ln_g, m_gm_ln_b, m_gm_w_s, m_gm_b_s, m_gm_w_out, m_hg_w_in, m_hg_lb, m_hg_gn_g, m_hg_w_out, m_ffn_w_up, m_ffn_conv_w, m_ffn_conv_b, m_ffn_w_down, m_norm_g, m_ada_w, m_ada_b, m_final_g, v_gm_w_in, v_gm_ln_g, v_gm_ln_b, v_gm_w_s, v_gm_b_s, v_gm_w_out, v_hg_w_in, v_hg_lb, v_hg_gn_g, v_hg_w_out, v_ffn_w_up, v_ffn_conv_w, v_ffn_conv_b, v_ffn_w_down, v_norm_g, v_ada_w, v_ada_b, v_final_g):
    me = _flat(_mesh_pos())
    xt = x[0]
    t = xt.shape[0]

    small_shapes = [(1, D), (2, HG_DIM), (2, HG_DIM), (DEPTH, 2, HG_DIM), (DEPTH, 3, 2 * FFN_HIDDEN // N_DEV)]
    (small_all,) = all_gather([_pack([c, hg_lb, hg_gn_g, norm_g, ffn_conv_w])], "gather_small")
    small_all = small_all.reshape(N_DEV, -1)
    at = 0
    pieces = []
    for s in small_shapes:
        n = math.prod(s)
        pieces.append(small_all[:, at:at + n].reshape((N_DEV,) + s))
        at += n
    c_all = pieces[0].reshape(N_DEV, D)
    hg_lb_full = jnp.transpose(pieces[1], (1, 0, 2)).reshape(2, D)
    hg_gn_full = jnp.transpose(pieces[2], (1, 0, 2)).reshape(2, D)
    norm_g_full = jnp.transpose(pieces[3], (1, 2, 0, 3)).reshape(DEPTH, 2, D)
    conv_w_full = jnp.transpose(pieces[4], (1, 2, 0, 3)).reshape(DEPTH, 3, 2 * FFN_HIDDEN)

    lb1 = lower_bound_fwd(hg_lb_full, "lower_bound")
    lbs = [jnp.zeros((1, D), f32), lb1]

    ada_b_cols = lax.dynamic_slice(ada_b, (0, me * ADA_COLS), (DEPTH, ADA_COLS)).reshape(DEPTH, 1, ADA_COLS)
    mod_cols = ada_mod(c_all, ada_w, ada_b_cols, "ada_mod")
    (mod_mine,) = all_to_all([jnp.transpose(mod_cols, (1, 0, 2))], "mod_to_examples")
    mod = jnp.transpose(mod_mine, (1, 0, 2)).reshape(DEPTH, 6, 1, D)

    def layer_shards(i):
        j = i // 2
        w_in, w_out = (gm_w_in, gm_w_out) if i % 2 == 0 else (hg_w_in, hg_w_out)
        return [w_in[j].astype(bf16), w_out[j].astype(bf16), ffn_w_up[i].T.astype(bf16), ffn_w_down[i].astype(bf16)]

    def full_rows(g):
        return g.reshape(N_DEV * g.shape[1], g.shape[2])

    carried_by = {
        "in_0": [(0, 1), (0, 3)], "mix_0": [(0, 2)], "up_0": [(1, 0), (1, 1)], "gate_0": [(1, 2)], "down_0": [(1, 3)],
        "in_1": [(2, 0)], "mix_1": [(2, 1), (2, 2), (2, 3)], "up_1": [(3, 0), (3, 1)], "gate_1": [(3, 2)], "down_1": [(3, 3)],
    }
    shards = [layer_shards(i) for i in range(DEPTH)]
    gathered = {}
    (gathered[(0, 0)],) = all_gather([shards[0][0]], "gather_weights_0", relay=True)

    def carry(call, site, **kw):
        items = carried_by.get(site, [])
        outs = call(riders=Riders([shards[l][slot] for l, slot in items], True), **kw)
        for item, g in zip(items, outs[len(outs) - len(items):]):
            gathered[item] = g
        return outs[:len(outs) - len(items)]

    saved = []
    weights = []
    xcur = xt
    h = norm_fwd(xcur, norm_g_full[0, 0:1], mod[0, 1], mod[0, 0], "norm1_0")
    for i in range(DEPTH):
        j = i // 2
        sh1, sc1, g1, sh2, sc2, g2 = [mod[i, p] for p in range(6)]
        gn2 = norm_g_full[i, 1:2]
        s = {"x0": xcur, "h": h}
        w_in = gathered[(i, 0)]
        if i % 2 == 0:
            (z,) = carry(functools.partial(mm_nn, h, w_in, bf16, f"gm_in_{i}"), f"in_{i}")
            bs = gm_b_s[j].reshape(GM_HEADS, GM_BLOCK, 1)
            (mixed,) = carry(functools.partial(gm_mix_fwd, z, gm_w_s[j], bs, gm_ln_g[j:j + 1], gm_ln_b[j:j + 1],
                                               f"gm_mix_{i}"), f"mix_{i}")
            s["z"] = z
        else:
            (proj,) = carry(functools.partial(mm_nn, h, w_in, f32, f"hg_in_{i}"), f"in_{i}")
            mixed, states = carry(functools.partial(hg_scan_fwd, proj, lbs[j], hg_gn_full[j:j + 1], f"hg_scan_{i}"),
                                  f"mix_{i}")
            s["proj"], s["states"] = proj, states
        s["mixed"] = mixed
        w_out = full_rows(gathered[(i, 1)])
        y, x1, h2 = carry(functools.partial(mm_nn_residual, mixed, w_out, xcur, g1, (gn2, sc2, sh2), f"mix_out_{i}"),
                          f"out_{i}")
        s["y"], s["x1"] = y, x1
        w_up = gathered[(i, 2)]
        (a,) = carry(functools.partial(mm_nn, h2, w_up, bf16, f"ffn_up_{i}", transposed=True), f"up_{i}")
        (hid,) = carry(functools.partial(ffn_gate_fwd, a, conv_w_full[i], ffn_conv_b[i:i + 1], f"ffn_gate_{i}"), f"gate_{i}")
        w_down = full_rows(gathered[(i, 3)])
        next_norm = (norm_g_full[i + 1, 0:1], mod[i + 1, 1], mod[i + 1, 0]) if i + 1 < DEPTH else None
        outs = carry(functools.partial(mm_nn_residual, hid, w_down, x1, g2, next_norm, f"ffn_down_{i}"), f"down_{i}")
        fo, x2 = outs[0], outs[1]
        s["h2"], s["a"], s["hid"], s["f"] = h2, a, hid, fo
        weights.append((w_in, w_out, w_up, w_down))
        saved.append(s)
        xcur = x2
        h = outs[2] if next_norm is not None else None

    loss_part, dx, d_final_g, dg2, df = loss_head(xcur, final_g.reshape(1, D), loss_target[0], saved[-1]["f"],
                                                  mod[DEPTH - 1, 5], "loss_head")
    loss = lax.psum(loss_part[0, 0], ("x", "y", "c"))

    def halves(blocked):
        rows = blocked.shape[1] // 2
        return [(blocked, (0, rows)), (blocked, (rows, rows))]

    def by_owner_rows(dw):
        k, n = dw.shape
        return dw.reshape(N_DEV, k // N_DEV, n)

    received = [[[] for _ in range(4)] for _ in range(DEPTH)]

    def send(call, items, **kw):
        outs = call(riders=Riders([arr for arr, _ in items], False), **kw)
        for (_, (layer, slot)), got in zip(items, outs[len(outs) - len(items):]):
            received[layer][slot].append(got)
        return outs[:len(outs) - len(items)]

    dmod = [None] * DEPTH
    d_norm_g = [None] * DEPTH
    d_gm = {k: [None, None] for k in ("ws", "bs", "lng", "lnb")}
    d_hg = {k: [None, None] for k in ("lb", "gn")}
    d_ffn = {k: [None] * DEPTH for k in ("cw", "cb")}
    in_halves = []
    for i in reversed(range(DEPTH)):
        j = i // 2
        s = saved[i]
        w_in, w_out, w_up, w_down = weights[i]
        sh1, sc1, g1, sh2, sc2, g2 = [mod[i, p] for p in range(6)]
        gn1, gn2 = norm_g_full[i, 0:1], norm_g_full[i, 1:2]
        (dw_down,) = mm_tn(s["hid"], df, bf16, f"dw_down_{i}", tn=D)
        scan_carries = i % 2 == 1
        down_item = (by_owner_rows(dw_down), (i, 3))
        (dhid,) = send(functools.partial(mm_nt, df, w_down, bf16, f"dhid_{i}"), [] if scan_carries else in_halves[:1])
        dyc, dwg, dwv, dbg, dbv = send(
            functools.partial(ffn_gate_bwd, s["a"], conv_w_full[i], ffn_conv_b[i:i + 1], dhid, f"ffn_gate_bwd_{i}"),
            in_halves if scan_carries else in_halves[1:] + [down_item])
        d_ffn["cw"][i] = jnp.concatenate([dwg, dwv], axis=1)
        d_ffn["cb"][i] = jnp.concatenate([dbg, dbv], axis=1)
        da = conv_transpose(dyc, conv_w_full[i], f"conv_t_{i}")
        (dw_up_t,) = mm_tn(da, s["h2"], bf16, f"dw_up_{i}")
        dw_up_t = dw_up_t.reshape(N_DEV, -1, D)
        up_halves = [(dw_up_t, (i, 2))] if scan_carries else [(part, (i, 2)) for part in halves(dw_up_t)]
        (dh2,) = send(functools.partial(mm_nt, da, w_up, bf16, f"dh2_{i}", transposed=True),
                      [] if scan_carries else up_halves[:1])
        dx1, dgn2, dsc2, dsh2, dg1, dy = norm_bwd(s["x1"], gn2, sc2, sh2, dh2, dx, f"norm2_bwd_{i}", gate=(s["y"], g1))
        (dw_out,) = mm_tn(s["mixed"], dy, bf16, f"dw_mix_out_{i}")
        (dmixed,) = mm_nt(dy, w_out, bf16, f"dmixed_{i}")
        if i % 2 == 0:
            bs = gm_b_s[j].reshape(GM_HEADS, GM_BLOCK, 1)
            dpre, dws, dbs, dlng, dlnb = send(
                functools.partial(gm_mix_bwd, s["z"], gm_w_s[j], bs, gm_ln_g[j:j + 1], gm_ln_b[j:j + 1], dmixed,
                                  f"gm_mix_bwd_{i}"), up_halves[1:])
            d_gm["ws"][j], d_gm["bs"][j], d_gm["lng"][j], d_gm["lnb"][j] = dws, dbs.reshape(GM_HEADS, GM_BLOCK), dlng, dlnb
        else:
            dpre, dlb, dgn = send(
                functools.partial(hg_scan_bwd, s["proj"], lbs[j], hg_gn_full[j:j + 1], s["states"], dmixed,
                                  f"hg_scan_bwd_{i}"), [down_item] + up_halves)
            d_hg["lb"][j], d_hg["gn"][j] = dlb, dgn
        (dw_in,) = send(functools.partial(mm_tn_by_owner, s["h"], dpre, f"dw_mix_in_{i}"), [(by_owner_rows(dw_out), (i, 1))])
        in_halves = [(part, (i, 0)) for part in halves(dw_in)]
        (dh,) = send(functools.partial(mm_nt, dpre, w_in, bf16, f"dh_mix_{i}"), in_halves[:1] if i == 0 else [])
        dmod_i = [None, None, dg1, dsh2, dsc2, dg2]
        if i > 0:
            dx, dgn1, dsc1, dsh1, dg2, df = norm_bwd(s["x0"], gn1, sc1, sh1, dh, dx1, f"norm1_bwd_{i}",
                                                     gate=(saved[i - 1]["f"], mod[i - 1, 5]))
        else:
            dx, dgn1, dsc1, dsh1 = send(functools.partial(norm_bwd, s["x0"], gn1, sc1, sh1, dh, dx1, f"norm1_bwd_{i}"),
                                        in_halves[1:])
        dmod_i[0], dmod_i[1] = dsh1, dsc1
        dmod[i] = jnp.concatenate(dmod_i, axis=1)
        d_norm_g[i] = jnp.concatenate([dgn1, dgn2], axis=0)
    grad_x = dx.reshape(1, t, D)

    (dmod_all,) = all_gather([jnp.concatenate(dmod, axis=0)], "gather_dmod")
    dmod_cols = jnp.transpose(lax.dynamic_slice(dmod_all, (0, 0, me * ADA_COLS), (N_DEV, DEPTH, ADA_COLS)), (1, 0, 2))
    g_ada_w, g_ada_b = ada_grads(c_all, dmod_cols, dmod_all.reshape(N_DEV, DEPTH, 1, 6 * D), "ada_grads")
    g_ada_b = g_ada_b.reshape(DEPTH, 6 * D)

    small_partials = [jnp.concatenate(d_gm["lng"], axis=0), jnp.concatenate(d_gm["lnb"], axis=0),
                      jnp.stack(d_gm["ws"]), jnp.stack(d_gm["bs"]), jnp.concatenate(d_ffn["cb"], axis=0),
                      d_final_g, d_hg["lb"][1], jnp.concatenate(d_hg["gn"], axis=0), jnp.stack(d_norm_g),
                      jnp.stack(d_ffn["cw"])]
    partial_shapes = [p.shape for p in small_partials]
    packed = _pack(small_partials, rows_mult=8 * N_DEV)
    rows = packed.shape[0] // N_DEV
    (recv,) = all_to_all([packed.reshape(N_DEV, rows, LANES)], "small_grads_exchange")
    (summed,) = all_gather([sum_parts(recv, "small_grads_sum")], "small_grads_gather")

    def parts_of(slot, layers):
        return [received[i][slot] for i in layers]

    def swapped(a):
        return jnp.swapaxes(a, 1, 2)

    w_shards = [gm_w_in, gm_w_out, hg_w_in, hg_w_out, swapped(ffn_w_up), ffn_w_down]
    big_parts = [parts_of(0, (0, 2)), parts_of(1, (0, 2)), parts_of(0, (1, 3)), parts_of(1, (1, 3)),
                 parts_of(2, range(DEPTH)), parts_of(3, range(DEPTH))]
    big_m = [m_gm_w_in, m_gm_w_out, m_hg_w_in, m_hg_w_out, swapped(m_ffn_w_up), m_ffn_w_down]
    big_v = [v_gm_w_in, v_gm_w_out, v_hg_w_in, v_hg_w_out, swapped(v_ffn_w_up), v_ffn_w_down]
    big = [adam_reduced(parts, w, m_, v_, f"adam_big_{idx}")
           for idx, (w, m_, v_, parts) in enumerate(zip(w_shards, big_m, big_v, big_parts))]
    big[4] = [swapped(o) for o in big[4]]
    (g_gm_w_in, d_gm_w_in, nm_gm_w_in, nv_gm_w_in), (g_gm_w_out, d_gm_w_out, nm_gm_w_out, nv_gm_w_out), \
        (g_hg_w_in, d_hg_w_in, nm_hg_w_in, nv_hg_w_in), (g_hg_w_out, d_hg_w_out, nm_hg_w_out, nv_hg_w_out), \
        (g_ffn_w_up, d_ffn_w_up, nm_ffn_w_up, nv_ffn_w_up), (g_ffn_w_down, d_ffn_w_down, nm_ffn_w_down, nv_ffn_w_down) = big

    g_ln_g, g_ln_b, g_ws, g_bs, g_cb, g_final, g_lb1, g_gn, g_norm, g_cw = _unpack(summed.reshape(-1), partial_shapes)
    g_final = g_final.reshape(D)

    def my_cols(a, n):
        start = (0,) * (a.ndim - 1) + (me * n,)
        return lax.dynamic_slice(a, start, a.shape[:-1] + (n,))

    g_hg_lb = lower_bound_bwd(hg_lb, my_cols(g_lb1, HG_DIM), "lower_bound_bwd")
    g_hg_gn = my_cols(g_gn, HG_DIM)
    g_norm_g = my_cols(g_norm, HG_DIM)
    g_conv_w = my_cols(g_cw, 2 * FFN_HIDDEN // N_DEV)

    two_d = (-1, ADA_COLS)
    d_ada_w, nm_ada_w, nv_ada_w = [o.reshape(ada_w.shape) for o in adam_plain(
        g_ada_w.reshape(two_d), ada_w.reshape(two_d), m_ada_w.reshape(two_d), v_ada_w.reshape(two_d), "adam_ada_w")]

    small_g = [g_ln_g, g_ln_b, g_ws, g_bs, g_cb, g_ada_b, g_final, g_hg_lb, g_hg_gn, g_norm_g, g_conv_w]
    small_w = [gm_ln_g, gm_ln_b, gm_w_s, gm_b_s, ffn_conv_b, ada_b, final_g, hg_lb, hg_gn_g, norm_g, ffn_conv_w]
    small_m = [m_gm_ln_g, m_gm_ln_b, m_gm_w_s, m_gm_b_s, m_ffn_conv_b, m_ada_b, m_final_g, m_hg_lb, m_hg_gn_g, m_norm_g, m_ffn_conv_w]
    small_v = [v_gm_ln_g, v_gm_ln_b, v_gm_w_s, v_gm_b_s, v_ffn_conv_b, v_ada_b, v_final_g, v_hg_lb, v_hg_gn_g, v_norm_g, v_ffn_conv_w]
    shapes = [w.shape for w in small_w]
    small_g = [g.reshape(s) for g, s in zip(small_g, shapes)]
    outs = adam_plain(_pack(small_g), _pack(small_w), _pack(small_m), _pack(small_v), "adam_small")
    (d_ln_g, d_ln_b, d_ws, d_bs, d_cb, d_ada_b, d_final, d_hg_lb, d_hg_gn, d_norm_g_, d_conv_w), \
        (nm_ln_g, nm_ln_b, nm_ws, nm_bs, nm_cb, nm_ada_b, nm_final, nm_hg_lb, nm_hg_gn, nm_norm_g, nm_conv_w), \
        (nv_ln_g, nv_ln_b, nv_ws, nv_bs, nv_cb, nv_ada_b, nv_final, nv_hg_lb, nv_hg_gn, nv_norm_g, nv_conv_w) = [
            _unpack(o.reshape(-1), shapes) for o in outs]
    g_ln_g, g_ln_b, g_ws, g_bs, g_cb, g_ada_b, g_final, g_hg_lb, g_hg_gn, g_norm_g, g_conv_w = small_g

    grads = (g_gm_w_in, g_ln_g, g_ln_b, g_ws, g_bs, g_gm_w_out, g_hg_w_in, g_hg_lb, g_hg_gn, g_hg_w_out,
             g_ffn_w_up, g_conv_w, g_cb, g_ffn_w_down, g_norm_g, g_ada_w, g_ada_b, g_final)
    deltas = (d_gm_w_in, d_ln_g, d_ln_b, d_ws, d_bs, d_gm_w_out, d_hg_w_in, d_hg_lb, d_hg_gn, d_hg_w_out,
              d_ffn_w_up, d_conv_w, d_cb, d_ffn_w_down, d_norm_g_, d_ada_w, d_ada_b, d_final)
    new_m = (nm_gm_w_in, nm_ln_g, nm_ln_b, nm_ws, nm_bs, nm_gm_w_out, nm_hg_w_in, nm_hg_lb, nm_hg_gn, nm_hg_w_out,
             nm_ffn_w_up, nm_conv_w, nm_cb, nm_ffn_w_down, nm_norm_g, nm_ada_w, nm_ada_b, nm_final)
    new_v = (nv_gm_w_in, nv_ln_g, nv_ln_b, nv_ws, nv_bs, nv_gm_w_out, nv_hg_w_in, nv_hg_lb, nv_hg_gn, nv_hg_w_out,
             nv_ffn_w_up, nv_conv_w, nv_cb, nv_ffn_w_down, nv_norm_g, nv_ada_w, nv_ada_b, nv_final)
    return (loss, grad_x) + grads + deltas + new_m + new_v
```

```python
import functools
import math

import jax
import jax.numpy as jnp
from jax import lax
from jax.experimental import pallas as pl
from jax.experimental.pallas import tpu as pltpu

f32 = jnp.float32
bf16 = jnp.bfloat16
MESH = pl.DeviceIdType.MESH

N_DEV = 8
D = 1024
DEPTH = 4
EPS = 1e-6
GM_WIDTH = 2048
GM_HEADS = 8
GM_HEAD_DIM = 256
GM_BLOCK = 128
CHUNK = 64
HG_HEADS = 8
HG_DIM = 128
FFN_HIDDEN = 2816
ADA_COLS = 6 * D // N_DEV

HG_SUB = 32
HG_PAIR = 8
HG_TOKENS = 128

ADAM_LR = 0.001
ADAM_B1 = 0.9
ADAM_B2 = 0.999
ADAM_EPS = 1e-08
ADAM_WD = 0.01
ADAM_STEP = 10

V7X_VMEM_LIMIT = 56 * 1024 * 1024
LANES = 128


def _cparams(*sem):
    return pltpu.CompilerParams(dimension_semantics=sem or None, vmem_limit_bytes=V7X_VMEM_LIMIT)


def _tile(n, target, mult=LANES):
    best = None
    for t in range(mult, min(n, target) + 1, mult):
        if n % t == 0:
            best = t
    return best or n


WEIGHT_BLOCK_BYTES = 6 * 1024 * 1024


def _weight_tile(n, k):
    return _tile(n, max(LANES, WEIGHT_BLOCK_BYTES // (2 * k)))


def _gelu(x):
    return 0.5 * x * (1.0 + lax.erf(x * (1.0 / math.sqrt(2.0))))


def _mesh_pos():
    return lax.axis_index("x"), lax.axis_index("y"), lax.axis_index("c")


def _flat(pos):
    return 4 * pos[0] + 2 * pos[1] + pos[2]


def _peer(pos, k):
    return ((1 - pos[0]) if k & 4 else pos[0], (1 - pos[1]) if k & 2 else pos[1], (1 - pos[2]) if k & 1 else pos[2])


def _exchange_copies(ins, outs, send_sems, recv_sems, local_sems, gather):
    pos = _mesh_pos()
    me = _flat(pos)

    def src(i, dest):
        if gather:
            return ins[i]
        ref, rows = ins[i] if isinstance(ins[i], tuple) else (ins[i], None)
        return ref.at[dest] if rows is None else ref.at[dest, pl.ds(*rows)]

    local = [pltpu.make_async_copy(src(i, me), outs[i].at[me], local_sems.at[i]) for i in range(len(ins))]
    sends, recvs = [], []
    for k in range(1, N_DEV):
        peer = _peer(pos, k)
        there = _flat(peer)
        for i in range(len(ins)):
            sems = dict(send_sem=send_sems.at[i * 7 + k - 1], recv_sem=recv_sems.at[i * 7 + k - 1],
                        device_id=peer, device_id_type=MESH)
            sends.append(pltpu.make_async_remote_copy(src_ref=src(i, there), dst_ref=outs[i].at[me], **sems))
            recvs.append(pltpu.make_async_remote_copy(src_ref=src(i, there), dst_ref=outs[i].at[there], **sems))
    return local, sends, recvs


def _exchange_start(*refs):
    local, sends, _ = _exchange_copies(*refs)
    for cp in local + sends:
        cp.start()


def _exchange_wait(*refs):
    local, sends, recvs = _exchange_copies(*refs)
    for cp in recvs:
        cp.wait_recv()
    for cp in sends:
        cp.wait_send()
    for cp in local:
        cp.wait()


OTHER_CHIPS = (2, 4, 6)


def _relay_copies(ins, outs, send_sems, recv_sems, local_sems):
    pos = _mesh_pos()
    me = _flat(pos)
    sibling = _peer(pos, 1)
    local = [pltpu.make_async_copy(ins[i], outs[i].at[me], local_sems.at[i]) for i in range(len(ins))]
    first, passes, recvs = [], {k: [] for k in OTHER_CHIPS}, {k: [] for k in range(1, N_DEV)}
    for i in range(len(ins)):
        def copy(k, src, block, to):
            return pltpu.make_async_remote_copy(
                src_ref=src, dst_ref=outs[i].at[block], send_sem=send_sems.at[i * 7 + k - 1],
                recv_sem=recv_sems.at[i * 7 + k - 1], device_id=to, device_id_type=MESH)

        for k in (1,) + OTHER_CHIPS:
            first.append(copy(k, ins[i], me, _peer(pos, k)))
        for k in OTHER_CHIPS:
            there = _flat(_peer(pos, k))
            passes[k].append(copy(k ^ 1, outs[i].at[there], there, sibling))
        for k in range(1, N_DEV):
            there = _flat(_peer(pos, k))
            recvs[k].append(copy(k, ins[i], there, _peer(pos, k)))
    return local, first, passes, recvs


def _relay_start(ins, outs, *sems):
    local, first, _, _ = _relay_copies(ins, outs, *sems)
    for cp in local + first:
        cp.start()


def _relay_wait(ins, outs, *sems):
    local, first, passes, recvs = _relay_copies(ins, outs, *sems)
    for k in OTHER_CHIPS:
        for cp in recvs[k]:
            cp.wait_recv()
        for cp in passes[k]:
            cp.start()
    for k in (1, 3, 5, 7):
        for cp in recvs[k]:
            cp.wait_recv()
    for cp in first + [cp for k in OTHER_CHIPS for cp in passes[k]]:
        cp.wait_send()
    for cp in local:
        cp.wait()


def _exchange_out_shape(a, gather):
    return jax.ShapeDtypeStruct((N_DEV,) + tuple(a.shape) if gather else tuple(a.shape), a.dtype)


def _exchange_sems(n):
    return [pltpu.SemaphoreType.DMA((7 * n,)), pltpu.SemaphoreType.DMA((7 * n,)), pltpu.SemaphoreType.DMA((n,))]


ANY = pl.BlockSpec(memory_space=pl.ANY)


def _exchange(arrs, gather, name, relay=False):
    n = len(arrs)

    def body(*refs):
        ins, outs = refs[:n], refs[n:2 * n]
        if relay:
            _relay_start(ins, outs, *refs[2 * n:])
            _relay_wait(ins, outs, *refs[2 * n:])
        else:
            _exchange_start(ins, outs, *refs[2 * n:], gather)
            _exchange_wait(ins, outs, *refs[2 * n:], gather)

    return pl.pallas_call(
        body, name=name, out_shape=tuple(_exchange_out_shape(a, gather) for a in arrs),
        in_specs=[ANY] * n, out_specs=tuple([ANY] * n), scratch_shapes=_exchange_sems(n),
    )(*arrs)


def all_gather(arrs, name, relay=False):
    return _exchange(arrs, True, name, relay)


def all_to_all(arrs, name):
    return _exchange(arrs, False, name)


class Riders:
    def __init__(self, arrs, gather):
        self.gather = gather
        self.rows = [a[1] if isinstance(a, tuple) else None for a in arrs]
        self.arrs = [a[0] if isinstance(a, tuple) else a for a in arrs]

    def out_shapes(self):
        shapes = []
        for a, rows in zip(self.arrs, self.rows):
            shape = tuple(a.shape) if rows is None else (a.shape[0], rows[1], a.shape[2])
            shapes.append(jax.ShapeDtypeStruct((N_DEV,) + shape if self.gather else shape, a.dtype))
        return shapes


def _call(body, *, name, grid, in_specs, out_specs, out_shape, semantics, scratch_shapes=(), riders=None):
    if riders is None or not riders.arrs:
        return pl.pallas_call(body, name=name, grid=grid, in_specs=in_specs, out_specs=tuple(out_specs),
                              out_shape=tuple(out_shape), scratch_shapes=list(scratch_shapes),
                              compiler_params=_cparams(*semantics))
    n_in, n_out, n_scr, n_r = len(in_specs), len(out_specs), len(scratch_shapes), len(riders.arrs)
    gather = riders.gather

    def hosted(*refs):
        ins, r_ins = refs[:n_in], refs[n_in:n_in + n_r]
        at = n_in + n_r
        outs, r_outs = refs[at:at + n_out], refs[at + n_out:at + n_out + n_r]
        at += n_out + n_r
        scratch, sems = refs[at:at + n_scr], refs[at + n_scr:]
        first = functools.reduce(jnp.logical_and, [pl.program_id(a) == 0 for a in range(len(grid))])
        last = functools.reduce(jnp.logical_and, [pl.program_id(a) == grid[a] - 1 for a in range(len(grid))])

        r_ins = [(ref, rows) if rows is not None else ref for ref, rows in zip(r_ins, riders.rows)]

        @pl.when(first)
        def _():
            if gather:
                _relay_start(r_ins, r_outs, *sems)
            else:
                _exchange_start(r_ins, r_outs, *sems, gather)

        body(*ins, *outs, *scratch)

        @pl.when(last)
        def _():
            if gather:
                _relay_wait(r_ins, r_outs, *sems)
            else:
                _exchange_wait(r_ins, r_outs, *sems, gather)

    call = pl.pallas_call(
        hosted, name=name, grid=grid, in_specs=list(in_specs) + [ANY] * n_r, out_specs=tuple(out_specs) + (ANY,) * n_r,
        out_shape=tuple(out_shape) + tuple(riders.out_shapes()),
        scratch_shapes=list(scratch_shapes) + _exchange_sems(n_r),
        compiler_params=_cparams(*(("arbitrary",) * len(grid))))
    return lambda *args: call(*args, *riders.arrs)


def _shards_per_step(shape):
    _, k, n = shape
    best = None
    for q in (1, 2, 4, 8):
        if (q * n) % LANES == 0 and (best is None or 2 * k * q * n <= WEIGHT_BLOCK_BYTES):
            best = q
    return best


def mm_nn(a, b, out_dtype, name, tm=512, riders=None, transposed=False):
    m, k = a.shape
    tm = _tile(m, tm, 8)
    if b.ndim == 3:
        shard = b.shape[1] if transposed else b.shape[2]
        n = N_DEV * shard
        per_step = _shards_per_step((N_DEV, k, shard))
        tn = per_step * shard
        b_spec = pl.BlockSpec((per_step,) + b.shape[1:], lambda i, j: (j, 0, 0))
        contract = (((1,), (1,)), ((), ())) if transposed else (((1,), (0,)), ((), ()))

        def body(a_ref, b_ref, o_ref):
            for q in range(per_step):
                o_ref[:, q * shard:(q + 1) * shard] = lax.dot_general(
                    a_ref[...], b_ref[q], contract, preferred_element_type=f32).astype(o_ref.dtype)
    else:
        n = b.shape[1]
        tn = _weight_tile(n, k)
        b_spec = pl.BlockSpec((k, tn), lambda i, j: (0, j))

        def body(a_ref, b_ref, o_ref):
            o_ref[...] = jnp.dot(a_ref[...], b_ref[...], preferred_element_type=f32).astype(o_ref.dtype)

    return _call(
        body, name=name, grid=(m // tm, n // tn),
        in_specs=[pl.BlockSpec((tm, k), lambda i, j: (i, 0)), b_spec],
        out_specs=[pl.BlockSpec((tm, tn), lambda i, j: (i, j))],
        out_shape=[jax.ShapeDtypeStruct((m, n), out_dtype)], semantics=("parallel", "parallel"), riders=riders,
    )(a, b)


def mm_nn_residual(a, b, x, gate, norm, name, tm=512, riders=None):
    m, k = a.shape
    n = b.shape[1]
    tm = _tile(m, tm, 8)

    def body(a_ref, b_ref, x_ref, g_ref, *rest):
        if norm is not None:
            gn_ref, sc_ref, sh_ref, y_ref, o_ref, h_ref = rest
        else:
            y_ref, o_ref = rest
        y = jnp.dot(a_ref[...], b_ref[...], preferred_element_type=f32)
        y_ref[...] = y.astype(bf16)
        x_new = x_ref[...] + g_ref[...] * y
        o_ref[...] = x_new
        if norm is not None:
            h_ref[...] = _norm_fn(x_new, gn_ref[...], sc_ref[...], sh_ref[...]).astype(bf16)

    blk = pl.BlockSpec((tm, n), lambda i: (i, 0))
    in_specs = [pl.BlockSpec((tm, k), lambda i: (i, 0)), pl.BlockSpec((k, n), lambda i: (0, 0)), blk, _row(n)]
    out_specs = [blk, blk]
    out_shape = [jax.ShapeDtypeStruct((m, n), bf16), jax.ShapeDtypeStruct((m, n), f32)]
    args = [a, b, x, gate]
    if norm is not None:
        in_specs += [_row(n)] * 3
        out_specs += [blk]
        out_shape += [jax.ShapeDtypeStruct((m, n), bf16)]
        args += list(norm)
    return _call(body, name=name, grid=(m // tm,), in_specs=in_specs, out_specs=out_specs, out_shape=out_shape,
                 semantics=("parallel",), riders=riders)(*args)


def mm_nt(a, b, out_dtype, name, tm=512, riders=None, transposed=False):
    m = a.shape[0]
    tm = _tile(m, tm, 8)
    if b.ndim == 3:
        shard, k = (b.shape[1], b.shape[2]) if transposed else (b.shape[2], b.shape[1])
        tk = k
        b_spec = pl.BlockSpec(b.shape, lambda i, j: (0, 0, 0))
        width = N_DEV * shard
        contract = (((1,), (0,)), ((), ())) if transposed else (((1,), (1,)), ((), ()))

        def body(a_ref, b_ref, o_ref):
            acc = None
            for q in range(N_DEV):
                part = lax.dot_general(a_ref[:, q * shard:(q + 1) * shard], b_ref[q], contract, preferred_element_type=f32)
                acc = part if acc is None else acc + part
            o_ref[...] = acc.astype(o_ref.dtype)
    else:
        k, width = b.shape
        tk = _weight_tile(k, width)
        b_spec = pl.BlockSpec((tk, width), lambda i, j: (j, 0))

        def body(a_ref, b_ref, o_ref):
            o_ref[...] = lax.dot_general(a_ref[...], b_ref[...], (((1,), (1,)), ((), ())),
                                         preferred_element_type=f32).astype(o_ref.dtype)

    return _call(
        body, name=name, grid=(m // tm, k // tk),
        in_specs=[pl.BlockSpec((tm, width), lambda i, j: (i, 0)), b_spec],
        out_specs=[pl.BlockSpec((tm, tk), lambda i, j: (i, j))],
        out_shape=[jax.ShapeDtypeStruct((m, k), out_dtype)], semantics=("parallel", "parallel"), riders=riders,
    )(a, b)


def mm_tn(a, b, out_dtype, name, tm=512, tn=512, riders=None):
    t, m = a.shape
    n = b.shape[1]
    tm, tn = _tile(m, tm), _tile(n, tn)

    def body(a_ref, b_ref, o_ref):
        o_ref[...] = lax.dot_general(a_ref[...], b_ref[...], (((0,), (0,)), ((), ())),
                                     preferred_element_type=f32).astype(o_ref.dtype)

    return _call(
        body, name=name, grid=(m // tm, n // tn),
        in_specs=[pl.BlockSpec((t, tm), lambda i, j: (0, i)), pl.BlockSpec((t, tn), lambda i, j: (0, j))],
        out_specs=[pl.BlockSpec((tm, tn), lambda i, j: (i, j))],
        out_shape=[jax.ShapeDtypeStruct((m, n), out_dtype)], semantics=("parallel", "parallel"), riders=riders,
    )(a, b)


def mm_tn_by_owner(a, b, name, tm=512, riders=None):
    t, m = a.shape
    n = b.shape[1]
    shard = n // N_DEV
    per_step = 1 if shard % LANES == 0 else 2
    assert (per_step * shard) % LANES == 0
    tm = _tile(m, tm)

    def body(a_ref, b_ref, o_ref):
        acc = lax.dot_general(a_ref[...], b_ref[...], (((0,), (0,)), ((), ())), preferred_element_type=f32)
        for q in range(per_step):
            o_ref[q] = acc[:, q * shard:(q + 1) * shard].astype(bf16)

    return _call(
        body, name=name, grid=(m // tm, N_DEV // per_step),
        in_specs=[pl.BlockSpec((t, tm), lambda i, j: (0, i)), pl.BlockSpec((t, per_step * shard), lambda i, j: (0, j))],
        out_specs=[pl.BlockSpec((per_step, tm, shard), lambda i, j: (j, i, 0))],
        out_shape=[jax.ShapeDtypeStruct((N_DEV, m, shard), bf16)], semantics=("parallel", "parallel"), riders=riders,
    )(a, b)


def _norm_fn(x, gn, sc, sh):
    r = lax.rsqrt(jnp.mean(x * x, axis=-1, keepdims=True) + EPS)
    return (x * r * gn) * (1.0 + sc) + sh


def _row(d):
    return pl.BlockSpec((1, d), lambda i: (0, 0))


def norm_fwd(x, gn, sc, sh, name, tm=512):
    t, d = x.shape
    tm = _tile(t, tm, 8)

    def body(x_ref, gn_ref, sc_ref, sh_ref, h_ref):
        h_ref[...] = _norm_fn(x_ref[...], gn_ref[...], sc_ref[...], sh_ref[...]).astype(bf16)

    return pl.pallas_call(
        body, name=name, grid=(t // tm,),
        in_specs=[pl.BlockSpec((tm, d), lambda i: (i, 0)), _row(d), _row(d), _row(d)],
        out_specs=pl.BlockSpec((tm, d), lambda i: (i, 0)),
        out_shape=jax.ShapeDtypeStruct((t, d), bf16), compiler_params=_cparams("parallel"),
    )(x, gn, sc, sh)


def _gate_bwd(dx, y_ref, g_ref, dgate_ref, dy_ref):
    dgate_ref[...] += jnp.sum(dx * y_ref[...].astype(f32), axis=0, keepdims=True)
    dy_ref[...] = (dx * g_ref[...]).astype(bf16)


def norm_bwd(x, gn, sc, sh, dh, dres, name, gate=None, tm=512, riders=None):
    t, d = x.shape
    tm = _tile(t, tm, 8)

    def body(x_ref, gn_ref, sc_ref, sh_ref, dh_ref, dres_ref, *rest):
        if gate is not None:
            y_ref, g_ref, dx_ref, dgn_ref, dsc_ref, dsh_ref, dgate_ref, dy_ref = rest
        else:
            dx_ref, dgn_ref, dsc_ref, dsh_ref = rest

        @pl.when(pl.program_id(0) == 0)
        def _():
            dgn_ref[...] = jnp.zeros_like(dgn_ref)
            dsc_ref[...] = jnp.zeros_like(dsc_ref)
            dsh_ref[...] = jnp.zeros_like(dsh_ref)
            if gate is not None:
                dgate_ref[...] = jnp.zeros_like(dgate_ref)

        _, vjp = jax.vjp(_norm_fn, x_ref[...], gn_ref[...], sc_ref[...], sh_ref[...])
        dx, dgn, dsc, dsh = vjp(dh_ref[...].astype(f32))
        dx = dx + dres_ref[...]
        dx_ref[...] = dx
        dgn_ref[...] += dgn
        dsc_ref[...] += dsc
        dsh_ref[...] += dsh
        if gate is not None:
            _gate_bwd(dx, y_ref, g_ref, dgate_ref, dy_ref)

    blk = pl.BlockSpec((tm, d), lambda i: (i, 0))
    vec = jax.ShapeDtypeStruct((1, d), f32)
    in_specs = [blk, _row(d), _row(d), _row(d), blk, blk]
    out_specs = [blk, _row(d), _row(d), _row(d)]
    out_shape = [jax.ShapeDtypeStruct((t, d), f32), vec, vec, vec]
    args = [x, gn, sc, sh, dh, dres]
    if gate is not None:
        in_specs += [blk, _row(d)]
        out_specs += [_row(d), blk]
        out_shape += [vec, jax.ShapeDtypeStruct((t, d), bf16)]
        args += list(gate)
    return _call(body, name=name, grid=(t // tm,), in_specs=in_specs, out_specs=out_specs, out_shape=out_shape,
                 semantics=("arbitrary",), riders=riders)(*args)


def _loss_fn(x, g, tgt):
    r = lax.rsqrt(jnp.mean(x * x, axis=-1, keepdims=True) + EPS)
    err = jnp.square(x * r * g - tgt)
    return 0.5 * jnp.sum(jnp.mean(err, axis=-1, keepdims=True), axis=0, keepdims=True)


def loss_head(x, g, tgt, y, gate, name, tm=512):
    t, d = x.shape
    tm = _tile(t, tm, 8)

    def body(x_ref, g_ref, t_ref, y_ref, gate_ref, loss_ref, dx_ref, dg_ref, dgate_ref, dy_ref):
        @pl.when(pl.program_id(0) == 0)
        def _():
            loss_ref[...] = jnp.zeros_like(loss_ref)
            dg_ref[...] = jnp.zeros_like(dg_ref)
            dgate_ref[...] = jnp.zeros_like(dgate_ref)

        loss, vjp = jax.vjp(_loss_fn, x_ref[...], g_ref[...], t_ref[...])
        dx, dg, _ = vjp(jnp.ones((1, 1), f32))
        dx_ref[...] = dx
        loss_ref[...] += loss
        dg_ref[...] += dg
        _gate_bwd(dx, y_ref, gate_ref, dgate_ref, dy_ref)

    blk = pl.BlockSpec((tm, d), lambda i: (i, 0))
    vec = jax.ShapeDtypeStruct((1, d), f32)
    return pl.pallas_call(
        body, name=name, grid=(t // tm,),
        in_specs=[blk, _row(d), blk, blk, _row(d)],
        out_specs=(pl.BlockSpec((1, 1), lambda i: (0, 0)), blk, _row(d), _row(d), blk),
        out_shape=(jax.ShapeDtypeStruct((1, 1), f32), jax.ShapeDtypeStruct((t, d), f32), vec, vec,
                   jax.ShapeDtypeStruct((t, d), bf16)),
        compiler_params=_cparams("arbitrary"),
    )(x, g, tgt, y, gate)


def _gm_block_fn(z, ws, bs, lng, lnb):
    u = _gelu(z[:, :GM_WIDTH])
    vg = _gelu(z[:, GM_WIDTH:])
    mu = jnp.mean(vg, axis=-1, keepdims=True)
    var = jnp.mean(jnp.square(vg - mu), axis=-1, keepdims=True)
    vn = (vg - mu) * lax.rsqrt(var + EPS) * lng + lnb
    row = lax.broadcasted_iota(jnp.int32, (GM_BLOCK, GM_BLOCK), 0) // CHUNK
    col = lax.broadcasted_iota(jnp.int32, (GM_BLOCK, GM_BLOCK), 1) // CHUNK
    parts = []
    for h in range(GM_HEADS):
        w = jnp.where(row >= col, ws[h], 0.0)
        cols = slice(h * GM_HEAD_DIM, (h + 1) * GM_HEAD_DIM)
        s = jnp.dot(w.astype(bf16), vn[:, cols].astype(bf16), preferred_element_type=f32) + bs[h]
        parts.append(u[:, cols] * s)
    return jnp.concatenate(parts, axis=1)


def _gm_param_specs():
    return [pl.BlockSpec((GM_HEADS, GM_BLOCK, GM_BLOCK), lambda i: (0, 0, 0)),
            pl.BlockSpec((GM_HEADS, GM_BLOCK, 1), lambda i: (0, 0, 0)), _row(GM_WIDTH), _row(GM_WIDTH)]


def gm_mix_fwd(z, ws, bs, lng, lnb, name, riders=None):
    t = z.shape[0]

    def body(z_ref, ws_ref, bs_ref, lng_ref, lnb_ref, o_ref):
        o_ref[...] = _gm_block_fn(z_ref[...].astype(f32), ws_ref[...], bs_ref[...], lng_ref[...],
                                  lnb_ref[...]).astype(bf16)

    return _call(
        body, name=name, grid=(t // GM_BLOCK,),
        in_specs=[pl.BlockSpec((GM_BLOCK, 2 * GM_WIDTH), lambda i: (i, 0))] + _gm_param_specs(),
        out_specs=[pl.BlockSpec((GM_BLOCK, GM_WIDTH), lambda i: (i, 0))],
        out_shape=[jax.ShapeDtypeStruct((t, GM_WIDTH), bf16)], semantics=("parallel",), riders=riders,
    )(z, ws, bs, lng, lnb)


def gm_mix_bwd(z, ws, bs, lng, lnb, dgated, name, riders=None):
    t = z.shape[0]

    def body(z_ref, ws_ref, bs_ref, lng_ref, lnb_ref, dg_ref, dz_ref, dws_ref, dbs_ref, dlng_ref, dlnb_ref):
        _, vjp = jax.vjp(_gm_block_fn, z_ref[...].astype(f32), ws_ref[...], bs_ref[...], lng_ref[...], lnb_ref[...])
        dz, dws, dbs, dlng, dlnb = vjp(dg_ref[...].astype(f32))
        dz_ref[...] = dz.astype(bf16)

        @pl.when(pl.program_id(0) == 0)
        def _():
            dws_ref[...] = jnp.zeros_like(dws_ref)
            dbs_ref[...] = jnp.zeros_like(dbs_ref)
            dlng_ref[...] = jnp.zeros_like(dlng_ref)
            dlnb_ref[...] = jnp.zeros_like(dlnb_ref)

        dws_ref[...] += dws
        dbs_ref[...] += dbs
        dlng_ref[...] += dlng
        dlnb_ref[...] += dlnb

    zblk = pl.BlockSpec((GM_BLOCK, 2 * GM_WIDTH), lambda i: (i, 0))
    return _call(
        body, name=name, grid=(t // GM_BLOCK,),
        in_specs=[zblk] + _gm_param_specs() + [pl.BlockSpec((GM_BLOCK, GM_WIDTH), lambda i: (i, 0))],
        out_specs=[zblk] + _gm_param_specs(),
        out_shape=[jax.ShapeDtypeStruct((t, 2 * GM_WIDTH), bf16),
                   jax.ShapeDtypeStruct((GM_HEADS, GM_BLOCK, GM_BLOCK), f32),
                   jax.ShapeDtypeStruct((GM_HEADS, GM_BLOCK, 1), f32),
                   jax.ShapeDtypeStruct((1, GM_WIDTH), f32), jax.ShapeDtypeStruct((1, GM_WIDTH), f32)],
        semantics=("arbitrary",), riders=riders,
    )(z, ws, bs, lng, lnb, dgated)


@functools.partial(jax.custom_vjp, nondiff_argnums=(1,))
def _rows_up(x, shift):
    return x if shift == 0 else pltpu.roll(x, x.shape[1] - shift, axis=1)


def _rows_up_fwd(x, shift):
    return _rows_up(x, shift), None


def _rows_up_bwd(shift, _, g):
    return (g if shift == 0 else pltpu.roll(g, shift, axis=1),)


_rows_up.defvjp(_rows_up_fwd, _rows_up_bwd)


def _hg_block_fn(qp, fz, iv, gp, s0, lb, gn):
    n, ns, d = HG_SUB, HG_TOKENS // HG_SUB, HG_DIM
    p, nb, per_sub = HG_PAIR, HG_TOKENS // HG_PAIR, HG_SUB // HG_PAIR
    f = lb + (1.0 - lb) * jax.nn.sigmoid(fz)
    g = jnp.log(f)
    k = 1.0 - f
    q = qp * jax.nn.sigmoid(qp)
    v = iv.astype(bf16)
    row = lax.broadcasted_iota(jnp.int32, (HG_TOKENS, HG_TOKENS), 0)
    col = lax.broadcasted_iota(jnp.int32, (HG_TOKENS, HG_TOKENS), 1)
    same_sub = col // n == row // n
    tri = ((col <= row) & same_sub).astype(f32)
    cum = jnp.dot(tri, g, precision=lax.Precision.HIGHEST, preferred_element_type=f32)
    cum_b, q_b, k_b, f_b = cum.reshape(nb, p, d), q.reshape(nb, p, d), k.reshape(nb, p, d), f.reshape(nb, p, d)
    j_b = lax.broadcasted_iota(jnp.int32, (nb, p, d), 1)
    j_col = lax.broadcasted_iota(jnp.int32, (nb, p, 1), 1)
    scores_t = jnp.zeros((HG_TOKENS, HG_TOKENS), f32)
    weight = k_b
    for delta in range(p):
        if delta:
            weight = weight * _rows_up(f_b, delta)
        pair = jnp.sum(_rows_up(q_b, delta) * weight, axis=2, keepdims=True)
        pair = jnp.where(j_col < p - delta, pair, 0.0)
        scores_t = scores_t + jnp.where(col == row + delta, pair.reshape(HG_TOKENS, 1), 0.0)
    o = lax.dot_general(scores_t.astype(bf16), v, (((0,), (0,)), ((), ())), preferred_element_type=f32)
    last = cum_b[:, p - 1:p, :]
    before = jnp.concatenate([jnp.zeros((1, 1, d), f32), last[:-1]], axis=0)
    before = jnp.broadcast_to(before, (nb, p, d)).reshape(HG_TOKENS, d)
    block = (lax.broadcasted_iota(jnp.int32, (HG_TOKENS, d), 0) // p) % per_sub
    q_late = q * jnp.exp(jnp.where(block > 0, cum - before, -1e30))
    last_s = last.reshape(ns, per_sub, d)
    q_parts, k_parts = [], []
    for m in range(1, per_sub):
        split = jnp.broadcast_to(last_s[:, m - 1:m, :], (ns, n, d)).reshape(HG_TOKENS, d)
        k_parts.append(k * jnp.exp(jnp.where(block < m, split - cum, -1e30)))
        q_parts.append(jnp.where(block == m, q_late, 0.0))
    scores = lax.dot_general(jnp.concatenate(q_parts, axis=1).astype(bf16), jnp.concatenate(k_parts, axis=1).astype(bf16),
                             (((1,), (1,)), ((), ())), preferred_element_type=f32)
    o = o + jnp.dot(jnp.where(same_sub, scores, 0.0).astype(bf16), v, preferred_element_type=f32)
    cum_s = cum.reshape(ns, n, d)
    tot = cum_s[:, n - 1:n, :]
    kt_t = (k.reshape(ns, n, d) * jnp.exp(tot - cum_s)).reshape(HG_TOKENS, d).T
    lane_sub = lax.broadcasted_iota(jnp.int32, (d, HG_TOKENS), 1) // n
    k_by_sub = jnp.concatenate([jnp.where(lane_sub == b, kt_t, 0.0) for b in range(ns)], axis=0).astype(bf16)
    update = jnp.dot(k_by_sub, v, preferred_element_type=f32)
    decay = jnp.exp(tot.reshape(ns, d)).T
    state = s0
    states = []
    for a in range(ns):
        states.append(state.astype(bf16))
        state = decay[:, a:a + 1] * state + update[a * d:(a + 1) * d]
    qt = q * jnp.exp(cum)
    row_sub = lax.broadcasted_iota(jnp.int32, (HG_TOKENS, d), 0) // n
    q_by_sub = jnp.concatenate([jnp.where(row_sub == a, qt, 0.0) for a in range(ns)], axis=1).astype(bf16)
    o = o + jnp.dot(q_by_sub, jnp.concatenate(states, axis=0), preferred_element_type=f32)
    on = o * lax.rsqrt(jnp.mean(o * o, axis=-1, keepdims=True) + EPS) * gn
    return on * (gp * jax.nn.sigmoid(gp)), state


def _head_parts(ref, h):
    return [ref[:, p * D + h * HG_DIM:p * D + (h + 1) * HG_DIM] for p in range(4)]


def hg_scan_fwd(proj, lb, gn, name, riders=None):
    t = proj.shape[0]
    nt = t // HG_TOKENS

    def body(p_ref, lb_ref, gn_ref, y_ref, s_ref, state):
        @pl.when(pl.program_id(0) == 0)
        def _():
            state[...] = jnp.zeros_like(state)

        for h in range(HG_HEADS):
            cols = slice(h * HG_DIM, (h + 1) * HG_DIM)
            s_ref[h, 0] = state[h]
            y, s1 = _hg_block_fn(*_head_parts(p_ref, h), state[h], lb_ref[:, cols], gn_ref[:, cols])
            y_ref[:, cols] = y.astype(bf16)
            state[h] = s1

    return _call(
        body, name=name, grid=(nt,),
        in_specs=[pl.BlockSpec((HG_TOKENS, 4 * D), lambda i: (i, 0)), _row(D), _row(D)],
        out_specs=[pl.BlockSpec((HG_TOKENS, D), lambda i: (i, 0)),
                   pl.BlockSpec((HG_HEADS, 1, HG_DIM, HG_DIM), lambda i: (0, i, 0, 0))],
        out_shape=[jax.ShapeDtypeStruct((t, D), bf16), jax.ShapeDtypeStruct((HG_HEADS, nt, HG_DIM, HG_DIM), f32)],
        scratch_shapes=[pltpu.VMEM((HG_HEADS, HG_DIM, HG_DIM), f32)],
        semantics=("arbitrary",), riders=riders,
    )(proj, lb, gn)


def hg_scan_bwd(proj, lb, gn, states, dy, name, riders=None):
    t = proj.shape[0]
    nt = t // HG_TOKENS

    def body(p_ref, lb_ref, gn_ref, s_ref, dy_ref, dp_ref, dlb_ref, dgn_ref, dstate):
        @pl.when(pl.program_id(0) == 0)
        def _():
            dstate[...] = jnp.zeros_like(dstate)
            dlb_ref[...] = jnp.zeros_like(dlb_ref)
            dgn_ref[...] = jnp.zeros_like(dgn_ref)

        for h in range(HG_HEADS):
            cols = slice(h * HG_DIM, (h + 1) * HG_DIM)
            _, vjp = jax.vjp(_hg_block_fn, *_head_parts(p_ref, h), s_ref[h, 0], lb_ref[:, cols], gn_ref[:, cols])
            grads = vjp((dy_ref[:, cols].astype(f32), dstate[h]))
            for p in range(4):
                dp_ref[:, p * D + h * HG_DIM:p * D + (h + 1) * HG_DIM] = grads[p].astype(bf16)
            dstate[h] = grads[4]
            dlb_ref[:, cols] += grads[5]
            dgn_ref[:, cols] += grads[6]

    small = jax.ShapeDtypeStruct((1, D), f32)
    return _call(
        body, name=name, grid=(nt,),
        in_specs=[pl.BlockSpec((HG_TOKENS, 4 * D), lambda i: (nt - 1 - i, 0)), _row(D), _row(D),
                  pl.BlockSpec((HG_HEADS, 1, HG_DIM, HG_DIM), lambda i: (0, nt - 1 - i, 0, 0)),
                  pl.BlockSpec((HG_TOKENS, D), lambda i: (nt - 1 - i, 0))],
        out_specs=[pl.BlockSpec((HG_TOKENS, 4 * D), lambda i: (nt - 1 - i, 0)), _row(D), _row(D)],
        out_shape=[jax.ShapeDtypeStruct((t, 4 * D), bf16), small, small],
        scratch_shapes=[pltpu.VMEM((HG_HEADS, HG_DIM, HG_DIM), f32)],
        semantics=("arbitrary",), riders=riders,
    )(proj, lb, gn, states, dy)


FFN_COLS = 1408
HALO = 8
STRIP = 16


def _ffn_specs(tm):
    nb = tm // HALO
    main_g = pl.BlockSpec((tm, FFN_COLS), lambda j, i: (i, j))
    main_v = pl.BlockSpec((tm, FFN_COLS), lambda j, i: (i, j + 2))
    halo_g = pl.BlockSpec((HALO, FFN_COLS), lambda j, i: (jnp.maximum(i * nb - 1, 0), j))
    halo_v = pl.BlockSpec((HALO, FFN_COLS), lambda j, i: (jnp.maximum(i * nb - 1, 0), j + 2))
    w_g = pl.BlockSpec((3, FFN_COLS), lambda j, i: (0, j))
    w_v = pl.BlockSpec((3, FFN_COLS), lambda j, i: (0, j + 2))
    b_g = pl.BlockSpec((1, FFN_COLS), lambda j, i: (0, j))
    b_v = pl.BlockSpec((1, FFN_COLS), lambda j, i: (0, j + 2))
    return [main_g, halo_g, main_v, halo_v, w_g, w_v, b_g, b_v]


def _strip_rows(r):
    return pl.ds(r * STRIP, STRIP) if isinstance(r, int) else pl.ds(pl.multiple_of(r * STRIP, STRIP), STRIP)


def _for_strips(nstrip, strip, reverse=False):
    if reverse:
        strip(nstrip - 1, True)
        lax.fori_loop(0, nstrip - 1, lambda k, c: (strip(nstrip - 2 - k, False), c)[1], 0)
    else:
        strip(0, True)
        lax.fori_loop(1, nstrip, lambda r, c: (strip(r, False), c)[1], 0)


SUBLANES = 8


def _rows_down(prev, cur, shift):
    row = lax.broadcasted_iota(jnp.int32, (SUBLANES, LANES), 0)
    tiles = [prev[STRIP - SUBLANES:]] + [cur[q * SUBLANES:(q + 1) * SUBLANES] for q in range(STRIP // SUBLANES)]
    turned = [pltpu.roll(x, shift, axis=0) for x in tiles]
    return jnp.concatenate([jnp.where(row < shift, turned[q], turned[q + 1]) for q in range(STRIP // SUBLANES)], axis=0)


def _rows_ahead(cur, nxt, shift):
    row = lax.broadcasted_iota(jnp.int32, (SUBLANES, LANES), 0)
    tiles = [cur[q * SUBLANES:(q + 1) * SUBLANES] for q in range(STRIP // SUBLANES)] + [nxt[:SUBLANES]]
    turned = [pltpu.roll(x, SUBLANES - shift, axis=0) for x in tiles]
    return jnp.concatenate([jnp.where(row >= SUBLANES - shift, turned[q + 1], turned[q])
                            for q in range(STRIP // SUBLANES)], axis=0)


def _conv_strip(main_ref, halo_ref, w_ref, b_ref, r, edge, cols, rowi):
    cur = main_ref[_strip_rows(r), cols].astype(f32)
    if edge:
        h = jnp.where(pl.program_id(1) == 0, 0.0, halo_ref[:, cols].astype(f32))
        prev = jnp.concatenate([jnp.zeros_like(h), h], axis=0)
    else:
        prev = main_ref[_strip_rows(r - 1), cols].astype(f32)
    a1, a2 = _rows_down(prev, cur, 1), _rows_down(prev, cur, 2)
    y = b_ref[:, cols] + w_ref[0:1, cols] * a2 + w_ref[1:2, cols] * a1 + w_ref[2:3, cols] * cur
    return y, (cur, a1, a2)


def ffn_gate_fwd(a, cw, cb, name, tm=512, riders=None):
    t = a.shape[0]
    tm = _tile(t, tm, STRIP)

    def body(ag_ref, hg_ref, av_ref, hv_ref, wg_ref, wv_ref, bg_ref, bv_ref, o_ref):
        rowi = lax.broadcasted_iota(jnp.int32, (STRIP, LANES), 0)

        def strip(r, edge):
            for c in range(FFN_COLS // LANES):
                cols = pl.ds(c * LANES, LANES)
                yg, _ = _conv_strip(ag_ref, hg_ref, wg_ref, bg_ref, r, edge, cols, rowi)
                yv, _ = _conv_strip(av_ref, hv_ref, wv_ref, bv_ref, r, edge, cols, rowi)
                o_ref[_strip_rows(r), cols] = (_gelu(yg) * yv).astype(bf16)

        _for_strips(tm // STRIP, strip)

    return _call(
        body, name=name, grid=(2, t // tm), in_specs=_ffn_specs(tm),
        out_specs=[pl.BlockSpec((tm, FFN_COLS), lambda j, i: (i, j))],
        out_shape=[jax.ShapeDtypeStruct((t, FFN_HIDDEN), bf16)],
        semantics=("parallel", "arbitrary"), riders=riders,
    )(a, a, a, a, cw, cw, cb, cb)


def ffn_gate_bwd(a, cw, cb, dhid, name, tm=512, riders=None):
    t = a.shape[0]
    tm = _tile(t, tm, STRIP)

    def body(ag_ref, hg_ref, av_ref, hv_ref, wg_ref, wv_ref, bg_ref, bv_ref, dh_ref,
             dy_ref, dwg_ref, dwv_ref, dbg_ref, dbv_ref, acc):
        rowi = lax.broadcasted_iota(jnp.int32, (STRIP, LANES), 0)

        @pl.when(pl.program_id(1) == 0)
        def _():
            acc[...] = jnp.zeros_like(acc)

        def strip(r, edge):
            rows = _strip_rows(r)
            for c in range(FFN_COLS // LANES):
                cols = pl.ds(c * LANES, LANES)
                yg, taps_g = _conv_strip(ag_ref, hg_ref, wg_ref, bg_ref, r, edge, cols, rowi)
                yv, taps_v = _conv_strip(av_ref, hv_ref, wv_ref, bv_ref, r, edge, cols, rowi)
                dh = dh_ref[rows, cols].astype(f32)
                cdf = 0.5 * (1.0 + lax.erf(yg * (1.0 / math.sqrt(2.0))))
                pdf = jnp.exp(-0.5 * yg * yg) * (1.0 / math.sqrt(2.0 * math.pi))
                dyg = dh * yv * (cdf + yg * pdf)
                dyv = dh * (yg * cdf)
                dy_ref[0, rows, cols] = dyg.astype(bf16)
                dy_ref[1, rows, cols] = dyv.astype(bf16)
                for p, (dy, (a0, a1, a2)) in enumerate(((dyg, taps_g), (dyv, taps_v))):
                    acc[4 * p + 0, :, cols] += dy * a2
                    acc[4 * p + 1, :, cols] += dy * a1
                    acc[4 * p + 2, :, cols] += dy * a0
                    acc[4 * p + 3, :, cols] += dy

        _for_strips(tm // STRIP, strip)

        @pl.when(pl.program_id(1) == pl.num_programs(1) - 1)
        def _():
            for p, (dw_ref, db_ref) in enumerate(((dwg_ref, dbg_ref), (dwv_ref, dbv_ref))):
                for tap in range(3):
                    dw_ref[tap:tap + 1, :] = jnp.sum(acc[4 * p + tap], axis=0, keepdims=True)
                db_ref[...] = jnp.sum(acc[4 * p + 3], axis=0, keepdims=True)

    half_w = pl.BlockSpec((3, FFN_COLS), lambda j, i: (0, j))
    half_b = pl.BlockSpec((1, FFN_COLS), lambda j, i: (0, j))
    return _call(
        body, name=name, grid=(2, t // tm),
        in_specs=_ffn_specs(tm) + [pl.BlockSpec((tm, FFN_COLS), lambda j, i: (i, j))],
        out_specs=[pl.BlockSpec((2, tm, FFN_COLS), lambda j, i: (0, i, j)), half_w, half_w, half_b, half_b],
        out_shape=[jax.ShapeDtypeStruct((2, t, FFN_HIDDEN), bf16),
                   jax.ShapeDtypeStruct((3, FFN_HIDDEN), f32), jax.ShapeDtypeStruct((3, FFN_HIDDEN), f32),
                   jax.ShapeDtypeStruct((1, FFN_HIDDEN), f32), jax.ShapeDtypeStruct((1, FFN_HIDDEN), f32)],
        scratch_shapes=[pltpu.VMEM((8, STRIP, FFN_COLS), f32)],
        semantics=("parallel", "arbitrary"), riders=riders,
    )(a, a, a, a, cw, cw, cb, cb, dhid)


def conv_transpose(dy, cw, name, tm=512):
    _, t, fh = dy.shape
    tm = _tile(t, tm, STRIP)
    nb = tm // HALO
    last_halo = t // HALO - 1
    ncol = fh // FFN_COLS

    def body(main_ref, halo_ref, w_ref, o_ref):
        rowi = lax.broadcasted_iota(jnp.int32, (STRIP, LANES), 0)
        last_block = pl.program_id(2) == pl.num_programs(2) - 1

        def strip(r, edge):
            rows = _strip_rows(r)
            for c in range(FFN_COLS // LANES):
                cols = pl.ds(c * LANES, LANES)
                cur = main_ref[0, rows, cols].astype(f32)
                if edge:
                    h = jnp.where(last_block, 0.0, halo_ref[0, :, cols].astype(f32))
                    nxt = jnp.concatenate([h, jnp.zeros_like(h)], axis=0)
                else:
                    nxt = main_ref[0, _strip_rows(r + 1), cols].astype(f32)
                d1, d2 = _rows_ahead(cur, nxt, 1), _rows_ahead(cur, nxt, 2)
                o_ref[rows, cols] = (w_ref[2:3, cols] * cur + w_ref[1:2, cols] * d1 + w_ref[0:1, cols] * d2).astype(bf16)

        _for_strips(tm // STRIP, strip, reverse=True)

    return pl.pallas_call(
        body, name=name, grid=(2, ncol, t // tm),
        in_specs=[pl.BlockSpec((1, tm, FFN_COLS), lambda p, j, i: (p, i, j)),
                  pl.BlockSpec((1, HALO, FFN_COLS), lambda p, j, i: (p, jnp.minimum((i + 1) * nb, last_halo), j)),
                  pl.BlockSpec((3, FFN_COLS), lambda p, j, i: (0, p * ncol + j))],
        out_specs=pl.BlockSpec((tm, FFN_COLS), lambda p, j, i: (i, p * ncol + j)),
        out_shape=jax.ShapeDtypeStruct((t, 2 * fh), bf16),
        compiler_params=_cparams("parallel", "parallel", "arbitrary"),
    )(dy, dy, cw)


def ada_mod(c_all, ada_w, ada_b_cols, name):
    cols = ada_w.shape[2]

    def body(c_ref, w_ref, b_ref, o_ref):
        c = c_ref[...]
        cond = (c * jax.nn.sigmoid(c)).astype(bf16)
        o_ref[0] = jnp.dot(cond, w_ref[0].astype(bf16), preferred_element_type=f32) + b_ref[0]

    return pl.pallas_call(
        body, name=name, grid=(DEPTH,),
        in_specs=[pl.BlockSpec((N_DEV, D), lambda i: (0, 0)), pl.BlockSpec((1, D, cols), lambda i: (i, 0, 0)),
                  pl.BlockSpec((1, 1, cols), lambda i: (i, 0, 0))],
        out_specs=pl.BlockSpec((1, N_DEV, cols), lambda i: (i, 0, 0)),
        out_shape=jax.ShapeDtypeStruct((DEPTH, N_DEV, cols), f32), compiler_params=_cparams("parallel"),
    )(c_all, ada_w, ada_b_cols)


def ada_grads(c_all, dmod_cols, dmod_all, name):
    cols = dmod_cols.shape[2]

    def body(c_ref, dm_ref, da_ref, dw_ref, db_ref):
        c = c_ref[...]
        cond = c * jax.nn.sigmoid(c)
        dw_ref[0] = lax.dot_general(cond, dm_ref[0], (((0,), (0,)), ((), ())), precision=lax.Precision.HIGHEST,
                                    preferred_element_type=f32)
        acc = da_ref[0, 0]
        for e in range(1, N_DEV):
            acc = acc + da_ref[e, 0]
        db_ref[0] = acc

    return pl.pallas_call(
        body, name=name, grid=(DEPTH,),
        in_specs=[pl.BlockSpec((N_DEV, D), lambda i: (0, 0)), pl.BlockSpec((1, N_DEV, cols), lambda i: (i, 0, 0)),
                  pl.BlockSpec((N_DEV, 1, 1, 6 * D), lambda i: (0, i, 0, 0))],
        out_specs=(pl.BlockSpec((1, D, cols), lambda i: (i, 0, 0)), pl.BlockSpec((1, 1, 6 * D), lambda i: (i, 0, 0))),
        out_shape=(jax.ShapeDtypeStruct((DEPTH, D, cols), f32), jax.ShapeDtypeStruct((DEPTH, 1, 6 * D), f32)),
        compiler_params=_cparams("parallel"),
    )(c_all, dmod_cols, dmod_all)


def lower_bound_fwd(hg_lb, name):
    n = hg_lb.shape[1]

    def body(l_ref, o_ref):
        o_ref[...] = jax.nn.sigmoid(l_ref[1:2, :] - l_ref[0:1, :])

    return pl.pallas_call(body, name=name, out_shape=jax.ShapeDtypeStruct((1, n), f32))(hg_lb)


def lower_bound_bwd(hg_lb, dlb, name):
    n = hg_lb.shape[1]

    def body(l_ref, d_ref, o_ref):
        p = jax.nn.sigmoid(l_ref[1:2, :] - l_ref[0:1, :])
        g = d_ref[...] * p * (1.0 - p)
        o_ref[0:1, :] = -g
        o_ref[1:2, :] = g

    return pl.pallas_call(body, name=name, out_shape=jax.ShapeDtypeStruct((2, n), f32))(hg_lb, dlb)


def _adamw(w, g, m, v):
    m = ADAM_B1 * m + (1.0 - ADAM_B1) * g
    v = ADAM_B2 * v + (1.0 - ADAM_B2) * jnp.square(g)
    m_hat = m / (1.0 - ADAM_B1 ** ADAM_STEP)
    v_hat = v / (1.0 - ADAM_B2 ** ADAM_STEP)
    delta = -ADAM_LR * (m_hat / (jnp.sqrt(v_hat) + ADAM_EPS) + ADAM_WD * w)
    return delta, m, v


ADAM_BLOCK_BYTES = 32 * 1024 * 1024


def adam_reduced(parts, w, m, v, name):
    layers, r, c = w.shape
    outs = None
    for layer in range(layers):
        outs = _adam_layer(parts[layer], w, m, v, layer, outs, f"{name}_{layer}")
    return outs


def _adam_layer(parts, w, m, v, layer, prev, name):
    layers, r, c = w.shape
    rows = parts[0].shape[1]
    assert all(p.shape == (N_DEV, rows, c) for p in parts) and rows * len(parts) == r
    row_bytes = 2 * (len(parts) * N_DEV * c * 2 + 7 * c * 4)
    tr = _tile(rows, max(16, ADAM_BLOCK_BYTES // row_bytes), 16)
    steps = rows // tr
    n_prev = 0 if prev is None else 4

    def body(*refs):
        p_refs = refs[:len(parts)]
        w_ref, m_ref, v_ref = refs[len(parts):len(parts) + 3]
        g_ref, d_ref, mo_ref, vo_ref = refs[len(parts) + 3 + n_prev:]
        for idx in range(len(parts)):
            @pl.when(pl.program_id(0) == idx)
            def _():
                g = p_refs[idx][0].astype(f32)
                for j in range(1, N_DEV):
                    g = g + p_refs[idx][j].astype(f32)
                g_ref[...] = g
                d_ref[...], mo_ref[...], vo_ref[...] = _adamw(w_ref[...], g, m_ref[...], v_ref[...])

    def part_spec(idx):
        return pl.BlockSpec((N_DEV, tr, c), lambda p, i: (0, jnp.where(p == idx, i, 0), 0))

    blk = pl.BlockSpec((None, tr, c), lambda p, i: (layer, p * steps + i, 0))
    out = jax.ShapeDtypeStruct((layers, r, c), f32)
    n_in = len(parts) + 3
    return pl.pallas_call(
        body, name=name, grid=(len(parts), steps),
        in_specs=[part_spec(idx) for idx in range(len(parts))] + [blk, blk, blk] + [ANY] * n_prev,
        out_specs=(blk, blk, blk, blk), out_shape=(out, out, out, out),
        input_output_aliases={n_in + k: k for k in range(n_prev)},
        compiler_params=_cparams("arbitrary", "arbitrary"),
    )(*parts, w, m, v, *(prev or ()))


def adam_plain(g, w, m, v, name, tr=256):
    r, c = w.shape
    tr = _tile(r, tr, 8)

    def body(g_ref, w_ref, m_ref, v_ref, d_ref, mo_ref, vo_ref):
        d_ref[...], mo_ref[...], vo_ref[...] = _adamw(w_ref[...], g_ref[...], m_ref[...], v_ref[...])

    blk = pl.BlockSpec((tr, c), lambda i: (i, 0))
    out = jax.ShapeDtypeStruct((r, c), f32)
    return pl.pallas_call(
        body, name=name, grid=(r // tr,), in_specs=[blk, blk, blk, blk], out_specs=(blk, blk, blk),
        out_shape=(out, out, out), compiler_params=_cparams("parallel"),
    )(g, w, m, v)


def sum_parts(parts, name):
    _, r, c = parts.shape

    def body(p_ref, o_ref):
        acc = p_ref[0]
        for j in range(1, N_DEV):
            acc = acc + p_ref[j]
        o_ref[...] = acc

    return pl.pallas_call(body, name=name, out_shape=jax.ShapeDtypeStruct((r, c), f32))(parts)


def _pack(arrs, rows_mult=8):
    flat = jnp.concatenate([a.reshape(-1) for a in arrs])
    rows = -(-flat.shape[0] // LANES)
    rows = -(-rows // rows_mult) * rows_mult
    return jnp.pad(flat, (0, rows * LANES - flat.shape[0])).reshape(rows, LANES)


def _unpack(flat, shapes):
    out, at = [], 0
    for s in shapes:
        n = math.prod(s)
        out.append(flat[at:at + n].reshape(s))
        at += n
    return out


def kernel(x, c, gm_w_in, gm_ln_g, gm_ln_b, gm_w_s, gm_b_s, gm_w_out, hg_w_in, hg_lb, hg_gn_g, hg_w_out, ffn_w_up, ffn_conv_w, ffn_conv_b, ffn_w_down, norm_g, ada_w, ada_b, final_g, loss_target, m_gm_w_in, m_gm_ln_g, m_gm_ln_b, m_gm_w_s, m_gm_b_s, m_gm_w_out, m_hg_w_in, m_hg_lb, m_hg_gn_g, m_hg_w_out, m_ffn_w_up, m_ffn_conv_w, m_ffn_conv_b, m_ffn_w_down, m_norm_g, m_ada_w, m_ada_b, m_final_g, v_gm_w_in, v_gm_ln_g, v_gm_ln_b, v_gm_w_s, v_gm_b_s, v_gm_w_out, v_hg_w_in, v_hg_lb, v_hg_gn_g, v_hg_w_out, v_ffn_w_up, v_ffn_conv_w, v_ffn_conv_b, v_ffn_w_down, v_norm_g, v_ada_w, v_ada_b, v_final_g):
    me = _flat(_mesh_pos())
    xt = x[0]
    t = xt.shape[0]

    small_shapes = [(1, D), (2, HG_DIM), (2, HG_DIM), (DEPTH, 2, HG_DIM), (DEPTH, 3, 2 * FFN_HIDDEN // N_DEV)]
    w_in_0 = gm_w_in[0].astype(bf16)
    small_all, w_in_0_all = all_gather([_pack([c, hg_lb, hg_gn_g, norm_g, ffn_conv_w]), w_in_0], "gather_first", relay=True)
    small_all = small_all.reshape(N_DEV, -1)
    at = 0
    pieces = []
    for s in small_shapes:
        n = math.prod(s)
        pieces.append(small_all[:, at:at + n].reshape((N_DEV,) + s))
        at += n
    c_all = pieces[0].reshape(N_DEV, D)
    hg_lb_full = jnp.transpose(pieces[1], (1, 0, 2)).reshape(2, D)
    hg_gn_full = jnp.transpose(pieces[2], (1, 0, 2)).reshape(2, D)
    norm_g_full = jnp.transpose(pieces[3], (1, 2, 0, 3)).reshape(DEPTH, 2, D)
    conv_w_full = jnp.transpose(pieces[4], (1, 2, 0, 3)).reshape(DEPTH, 3, 2 * FFN_HIDDEN)

    lb1 = lower_bound_fwd(hg_lb_full, "lower_bound")
    lbs = [jnp.zeros((1, D), f32), lb1]

    ada_b_cols = lax.dynamic_slice(ada_b, (0, me * ADA_COLS), (DEPTH, ADA_COLS)).reshape(DEPTH, 1, ADA_COLS)
    mod_cols = ada_mod(c_all, ada_w, ada_b_cols, "ada_mod")
    (mod_mine,) = all_to_all([jnp.transpose(mod_cols, (1, 0, 2))], "mod_to_examples")
    mod = jnp.transpose(mod_mine, (1, 0, 2)).reshape(DEPTH, 6, 1, D)

    def layer_shards(i):
        j = i // 2
        w_in, w_out = (gm_w_in, gm_w_out) if i % 2 == 0 else (hg_w_in, hg_w_out)
        return [w_in[j].astype(bf16), w_out[j].astype(bf16), ffn_w_up[i].T.astype(bf16), ffn_w_down[i].astype(bf16)]

    def full_rows(g):
        return g.reshape(N_DEV * g.shape[1], g.shape[2])

    carried_by = {
        "in_0": [(0, 1), (0, 3)], "mix_0": [(0, 2)], "up_0": [(1, 0), (1, 1)], "gate_0": [(1, 2)], "down_0": [(1, 3)],
        "in_1": [(2, 0)], "mix_1": [(2, 1), (2, 2), (2, 3)], "up_1": [(3, 0), (3, 1)], "gate_1": [(3, 2)], "down_1": [(3, 3)],
    }
    shards = [layer_shards(i) for i in range(DEPTH)]
    gathered = {}
    gathered[(0, 0)] = w_in_0_all

    def carry(call, site, **kw):
        items = carried_by.get(site, [])
        outs = call(riders=Riders([shards[l][slot] for l, slot in items], True), **kw)
        for item, g in zip(items, outs[len(outs) - len(items):]):
            gathered[item] = g
        return outs[:len(outs) - len(items)]

    saved = []
    weights = []
    xcur = xt
    h = norm_fwd(xcur, norm_g_full[0, 0:1], mod[0, 1], mod[0, 0], "norm1_0")
    for i in range(DEPTH):
        j = i // 2
        sh1, sc1, g1, sh2, sc2, g2 = [mod[i, p] for p in range(6)]
        gn2 = norm_g_full[i, 1:2]
        s = {"x0": xcur, "h": h}
        w_in = gathered[(i, 0)]
        if i % 2 == 0:
            (z,) = carry(functools.partial(mm_nn, h, w_in, bf16, f"gm_in_{i}"), f"in_{i}")
            bs = gm_b_s[j].reshape(GM_HEADS, GM_BLOCK, 1)
            (mixed,) = carry(functools.partial(gm_mix_fwd, z, gm_w_s[j], bs, gm_ln_g[j:j + 1], gm_ln_b[j:j + 1],
                                               f"gm_mix_{i}"), f"mix_{i}")
            s["z"] = z
        else:
            (proj,) = carry(functools.partial(mm_nn, h, w_in, f32, f"hg_in_{i}"), f"in_{i}")
            mixed, states = carry(functools.partial(hg_scan_fwd, proj, lbs[j], hg_gn_full[j:j + 1], f"hg_scan_{i}"),
                                  f"mix_{i}")
            s["proj"], s["states"] = proj, states
        s["mixed"] = mixed
        w_out = full_rows(gathered[(i, 1)])
        y, x1, h2 = carry(functools.partial(mm_nn_residual, mixed, w_out, xcur, g1, (gn2, sc2, sh2), f"mix_out_{i}"),
                          f"out_{i}")
        s["y"], s["x1"] = y, x1
        w_up = gathered[(i, 2)]
        (a,) = carry(functools.partial(mm_nn, h2, w_up, bf16, f"ffn_up_{i}", transposed=True), f"up_{i}")
        (hid,) = carry(functools.partial(ffn_gate_fwd, a, conv_w_full[i], ffn_conv_b[i:i + 1], f"ffn_gate_{i}"), f"gate_{i}")
        w_down = full_rows(gathered[(i, 3)])
        next_norm = (norm_g_full[i + 1, 0:1], mod[i + 1, 1], mod[i + 1, 0]) if i + 1 < DEPTH else None
        outs = carry(functools.partial(mm_nn_residual, hid, w_down, x1, g2, next_norm, f"ffn_down_{i}"), f"down_{i}")
        fo, x2 = outs[0], outs[1]
        s["h2"], s["a"], s["hid"], s["f"] = h2, a, hid, fo
        weights.append((w_in, w_out, w_up, w_down))
        saved.append(s)
        xcur = x2
        h = outs[2] if next_norm is not None else None

    loss_part, dx, d_final_g, dg2, df = loss_head(xcur, final_g.reshape(1, D), loss_target[0], saved[-1]["f"],
                                                  mod[DEPTH - 1, 5], "loss_head")
    loss = lax.psum(loss_part[0, 0], ("x", "y", "c"))

    def halves(blocked):
        rows = blocked.shape[1] // 2
        return [(blocked, (0, rows)), (blocked, (rows, rows))]

    def by_owner_rows(dw):
        k, n = dw.shape
        return dw.reshape(N_DEV, k // N_DEV, n)

    received = [[[] for _ in range(4)] for _ in range(DEPTH)]

    def send(call, items, **kw):
        outs = call(riders=Riders([arr for arr, _ in items], False), **kw)
        for (_, (layer, slot)), got in zip(items, outs[len(outs) - len(items):]):
            received[layer][slot].append(got)
        return outs[:len(outs) - len(items)]

    dmod = [None] * DEPTH
    d_norm_g = [None] * DEPTH
    d_gm = {k: [None, None] for k in ("ws", "bs", "lng", "lnb")}
    d_hg = {k: [None, None] for k in ("lb", "gn")}
    d_ffn = {k: [None] * DEPTH for k in ("cw", "cb")}
    in_halves = []
    for i in reversed(range(DEPTH)):
        j = i // 2
        s = saved[i]
        w_in, w_out, w_up, w_down = weights[i]
        sh1, sc1, g1, sh2, sc2, g2 = [mod[i, p] for p in range(6)]
        gn1, gn2 = norm_g_full[i, 0:1], norm_g_full[i, 1:2]
        (dw_down,) = mm_tn(s["hid"], df, bf16, f"dw_down_{i}", tn=D)
        scan_carries = i % 2 == 1
        down_item = (by_owner_rows(dw_down), (i, 3))
        (dhid,) = mm_nt(df, w_down, bf16, f"dhid_{i}")
        dyc, dwg, dwv, dbg, dbv = send(
            functools.partial(ffn_gate_bwd, s["a"], conv_w_full[i], ffn_conv_b[i:i + 1], dhid, f"ffn_gate_bwd_{i}"),
            in_halves)
        d_ffn["cw"][i] = jnp.concatenate([dwg, dwv], axis=1)
        d_ffn["cb"][i] = jnp.concatenate([dbg, dbv], axis=1)
        da = conv_transpose(dyc, conv_w_full[i], f"conv_t_{i}")
        (dw_up_t,) = send(functools.partial(mm_tn, da, s["h2"], bf16, f"dw_up_{i}"), [] if scan_carries else [down_item])
        dw_up_t = dw_up_t.reshape(N_DEV, -1, D)
        up_halves = [(dw_up_t, (i, 2))] if scan_carries else [(part, (i, 2)) for part in halves(dw_up_t)]
        (dh2,) = send(functools.partial(mm_nt, da, w_up, bf16, f"dh2_{i}", transposed=True),
                      [] if scan_carries else up_halves[:1])
        dx1, dgn2, dsc2, dsh2, dg1, dy = norm_bwd(s["x1"], gn2, sc2, sh2, dh2, dx, f"norm2_bwd_{i}", gate=(s["y"], g1))
        (dw_out,) = mm_tn(s["mixed"], dy, bf16, f"dw_mix_out_{i}")
        (dmixed,) = mm_nt(dy, w_out, bf16, f"dmixed_{i}")
        if i % 2 == 0:
            bs = gm_b_s[j].reshape(GM_HEADS, GM_BLOCK, 1)
            dpre, dws, dbs, dlng, dlnb = send(
                functools.partial(gm_mix_bwd, s["z"], gm_w_s[j], bs, gm_ln_g[j:j + 1], gm_ln_b[j:j + 1], dmixed,
                                  f"gm_mix_bwd_{i}"), up_halves[1:])
            d_gm["ws"][j], d_gm["bs"][j], d_gm["lng"][j], d_gm["lnb"][j] = dws, dbs.reshape(GM_HEADS, GM_BLOCK), dlng, dlnb
        else:
            dpre, dlb, dgn = send(
                functools.partial(hg_scan_bwd, s["proj"], lbs[j], hg_gn_full[j:j + 1], s["states"], dmixed,
                                  f"hg_scan_bwd_{i}"), [down_item] + up_halves)
            d_hg["lb"][j], d_hg["gn"][j] = dlb, dgn
        (dw_in,) = send(functools.partial(mm_tn_by_owner, s["h"], dpre, f"dw_mix_in_{i}"), [(by_owner_rows(dw_out), (i, 1))])
        in_halves = [(part, (i, 0)) for part in halves(dw_in)]
        (dh,) = send(functools.partial(mm_nt, dpre, w_in, bf16, f"dh_mix_{i}"), in_halves[:1] if i == 0 else [])
        dmod_i = [None, None, dg1, dsh2, dsc2, dg2]
        if i > 0:
            dx, dgn1, dsc1, dsh1, dg2, df = norm_bwd(s["x0"], gn1, sc1, sh1, dh, dx1, f"norm1_bwd_{i}",
                                                     gate=(saved[i - 1]["f"], mod[i - 1, 5]))
        else:
            dx, dgn1, dsc1, dsh1 = send(functools.partial(norm_bwd, s["x0"], gn1, sc1, sh1, dh, dx1, f"norm1_bwd_{i}"),
                                        in_halves[1:])
        dmod_i[0], dmod_i[1] = dsh1, dsc1
        dmod[i] = jnp.concatenate(dmod_i, axis=1)
        d_norm_g[i] = jnp.concatenate([dgn1, dgn2], axis=0)
    grad_x = dx.reshape(1, t, D)

    (dmod_all,) = all_gather([jnp.concatenate(dmod, axis=0)], "gather_dmod")
    dmod_cols = jnp.transpose(lax.dynamic_slice(dmod_all, (0, 0, me * ADA_COLS), (N_DEV, DEPTH, ADA_COLS)), (1, 0, 2))
    g_ada_w, g_ada_b = ada_grads(c_all, dmod_cols, dmod_all.reshape(N_DEV, DEPTH, 1, 6 * D), "ada_grads")
    g_ada_b = g_ada_b.reshape(DEPTH, 6 * D)

    small_partials = [jnp.concatenate(d_gm["lng"], axis=0), jnp.concatenate(d_gm["lnb"], axis=0),
                      jnp.stack(d_gm["ws"]), jnp.stack(d_gm["bs"]), jnp.concatenate(d_ffn["cb"], axis=0),
                      d_final_g, d_hg["lb"][1], jnp.concatenate(d_hg["gn"], axis=0), jnp.stack(d_norm_g),
                      jnp.stack(d_ffn["cw"])]
    partial_shapes = [p.shape for p in small_partials]
    packed = _pack(small_partials, rows_mult=8 * N_DEV)
    rows = packed.shape[0] // N_DEV
    (recv,) = all_to_all([packed.reshape(N_DEV, rows, LANES)], "small_grads_exchange")
    (summed,) = all_gather([sum_parts(recv, "small_grads_sum")], "small_grads_gather")

    def parts_of(slot, layers):
        return [received[i][slot] for i in layers]

    def swapped(a):
        return jnp.swapaxes(a, 1, 2)

    w_shards = [gm_w_in, gm_w_out, hg_w_in, hg_w_out, swapped(ffn_w_up), ffn_w_down]
    big_parts = [parts_of(0, (0, 2)), parts_of(1, (0, 2)), parts_of(0, (1, 3)), parts_of(1, (1, 3)),
                 parts_of(2, range(DEPTH)), parts_of(3, range(DEPTH))]
    big_m = [m_gm_w_in, m_gm_w_out, m_hg_w_in, m_hg_w_out, swapped(m_ffn_w_up), m_ffn_w_down]
    big_v = [v_gm_w_in, v_gm_w_out, v_hg_w_in, v_hg_w_out, swapped(v_ffn_w_up), v_ffn_w_down]
    big = [adam_reduced(parts, w, m_, v_, f"adam_big_{idx}")
           for idx, (w, m_, v_, parts) in enumerate(zip(w_shards, big_m, big_v, big_parts))]
    big[4] = [swapped(o) for o in big[4]]
    (g_gm_w_in, d_gm_w_in, nm_gm_w_in, nv_gm_w_in), (g_gm_w_out, d_gm_w_out, nm_gm_w_out, nv_gm_w_out), \
        (g_hg_w_in, d_hg_w_in, nm_hg_w_in, nv_hg_w_in), (g_hg_w_out, d_hg_w_out, nm_hg_w_out, nv_hg_w_out), \
        (g_ffn_w_up, d_ffn_w_up, nm_ffn_w_up, nv_ffn_w_up), (g_ffn_w_down, d_ffn_w_down, nm_ffn_w_down, nv_ffn_w_down) = big

    g_ln_g, g_ln_b, g_ws, g_bs, g_cb, g_final, g_lb1, g_gn, g_norm, g_cw = _unpack(summed.reshape(-1), partial_shapes)
    g_final = g_final.reshape(D)

    def my_cols(a, n):
        start = (0,) * (a.ndim - 1) + (me * n,)
        return lax.dynamic_slice(a, start, a.shape[:-1] + (n,))

    g_hg_lb = lower_bound_bwd(hg_lb, my_cols(g_lb1, HG_DIM), "lower_bound_bwd")
    g_hg_gn = my_cols(g_gn, HG_DIM)
    g_norm_g = my_cols(g_norm, HG_DIM)
    g_conv_w = my_cols(g_cw, 2 * FFN_HIDDEN // N_DEV)

    two_d = (-1, ADA_COLS)
    d_ada_w, nm_ada_w, nv_ada_w = [o.reshape(ada_w.shape) for o in adam_plain(
        g_ada_w.reshape(two_d), ada_w.reshape(two_d), m_ada_w.reshape(two_d), v_ada_w.reshape(two_d), "adam_ada_w")]

    small_g = [g_ln_g, g_ln_b, g_ws, g_bs, g_cb, g_ada_b, g_final, g_hg_lb, g_hg_gn, g_norm_g, g_conv_w]
    small_w = [gm_ln_g, gm_ln_b, gm_w_s, gm_b_s, ffn_conv_b, ada_b, final_g, hg_lb, hg_gn_g, norm_g, ffn_conv_w]
    small_m = [m_gm_ln_g, m_gm_ln_b, m_gm_w_s, m_gm_b_s, m_ffn_conv_b, m_ada_b, m_final_g, m_hg_lb, m_hg_gn_g, m_norm_g, m_ffn_conv_w]
    small_v = [v_gm_ln_g, v_gm_ln_b, v_gm_w_s, v_gm_b_s, v_ffn_conv_b, v_ada_b, v_final_g, v_hg_lb, v_hg_gn_g, v_norm_g, v_ffn_conv_w]
    shapes = [w.shape for w in small_w]
    small_g = [g.reshape(s) for g, s in zip(small_g, shapes)]
    outs = adam_plain(_pack(small_g), _pack(small_w), _pack(small_m), _pack(small_v), "adam_small")
    (d_ln_g, d_ln_b, d_ws, d_bs, d_cb, d_ada_b, d_final, d_hg_lb, d_hg_gn, d_norm_g_, d_conv_w), \
        (nm_ln_g, nm_ln_b, nm_ws, nm_bs, nm_cb, nm_ada_b, nm_final, nm_hg_lb, nm_hg_gn, nm_norm_g, nm_conv_w), \
        (nv_ln_g, nv_ln_b, nv_ws, nv_bs, nv_cb, nv_ada_b, nv_final, nv_hg_lb, nv_hg_gn, nv_norm_g, nv_conv_w) = [
            _unpack(o.reshape(-1), shapes) for o in outs]
    g_ln_g, g_ln_b, g_ws, g_bs, g_cb, g_ada_b, g_final, g_hg_lb, g_hg_gn, g_norm_g, g_conv_w = small_g

    grads = (g_gm_w_in, g_ln_g, g_ln_b, g_ws, g_bs, g_gm_w_out, g_hg_w_in, g_hg_lb, g_hg_gn, g_hg_w_out,
             g_ffn_w_up, g_conv_w, g_cb, g_ffn_w_down, g_norm_g, g_ada_w, g_ada_b, g_final)
    deltas = (d_gm_w_in, d_ln_g, d_ln_b, d_ws, d_bs, d_gm_w_out, d_hg_w_in, d_hg_lb, d_hg_gn, d_hg_w_out,
              d_ffn_w_up, d_conv_w, d_cb, d_ffn_w_down, d_norm_g_, d_ada_w, d_ada_b, d_final)
    new_m = (nm_gm_w_in, nm_ln_g, nm_ln_b, nm_ws, nm_bs, nm_gm_w_out, nm_hg_w_in, nm_hg_lb, nm_hg_gn, nm_hg_w_out,
             nm_ffn_w_up, nm_conv_w, nm_cb, nm_ffn_w_down, nm_norm_g, nm_ada_w, nm_ada_b, nm_final)
    new_v = (nv_gm_w_in, nv_ln_g, nv_ln_b, nv_ws, nv_bs, nv_gm_w_out, nv_hg_w_in, nv_hg_lb, nv_hg_gn, nv_hg_w_out,
             nv_ffn_w_up, nv_conv_w, nv_cb, nv_ffn_w_down, nv_norm_g, nv_ada_w, nv_ada_b, nv_final)
    return (loss, grad_x) + grads + deltas + new_m + new_v
```

```python
import functools
import math

import jax
import jax.numpy as jnp
from jax import lax
from jax.experimental import pallas as pl
from jax.experimental.pallas import tpu as pltpu

f32 = jnp.float32
bf16 = jnp.bfloat16
MESH = pl.DeviceIdType.MESH

N_DEV = 8
D = 1024
DEPTH = 4
EPS = 1e-6
GM_WIDTH = 2048
GM_HEADS = 8
GM_HEAD_DIM = 256
GM_BLOCK = 128
CHUNK = 64
HG_HEADS = 8
HG_DIM = 128
FFN_HIDDEN = 2816
ADA_COLS = 6 * D // N_DEV

HG_SUB = 32
HG_PAIR = 8
HG_TOKENS = 128

ADAM_LR = 0.001
ADAM_B1 = 0.9
ADAM_B2 = 0.999
ADAM_EPS = 1e-08
ADAM_WD = 0.01
ADAM_STEP = 10

V7X_VMEM_LIMIT = 56 * 1024 * 1024
LANES = 128


def _cparams(*sem):
    return pltpu.CompilerParams(dimension_semantics=sem or None, vmem_limit_bytes=V7X_VMEM_LIMIT)


def _tile(n, target, mult=LANES):
    best = None
    for t in range(mult, min(n, target) + 1, mult):
        if n % t == 0:
            best = t
    return best or n


WEIGHT_BLOCK_BYTES = 6 * 1024 * 1024


def _weight_tile(n, k):
    return _tile(n, max(LANES, WEIGHT_BLOCK_BYTES // (2 * k)))


def _gelu(x):
    return 0.5 * x * (1.0 + lax.erf(x * (1.0 / math.sqrt(2.0))))


def _mesh_pos():
    return lax.axis_index("x"), lax.axis_index("y"), lax.axis_index("c")


def _flat(pos):
    return 4 * pos[0] + 2 * pos[1] + pos[2]


def _peer(pos, k):
    return ((1 - pos[0]) if k & 4 else pos[0], (1 - pos[1]) if k & 2 else pos[1], (1 - pos[2]) if k & 1 else pos[2])


def _exchange_copies(ins, outs, send_sems, recv_sems, local_sems, gather):
    pos = _mesh_pos()
    me = _flat(pos)

    def src(i, dest):
        if gather:
            return ins[i]
        ref, rows = ins[i] if isinstance(ins[i], tuple) else (ins[i], None)
        return ref.at[dest] if rows is None else ref.at[dest, pl.ds(*rows)]

    local = [pltpu.make_async_copy(src(i, me), outs[i].at[me], local_sems.at[i]) for i in range(len(ins))]
    sends, recvs = [], []
    for k in range(1, N_DEV):
        peer = _peer(pos, k)
        there = _flat(peer)
        for i in range(len(ins)):
            sems = dict(send_sem=send_sems.at[i * 7 + k - 1], recv_sem=recv_sems.at[i * 7 + k - 1],
                        device_id=peer, device_id_type=MESH)
            sends.append(pltpu.make_async_remote_copy(src_ref=src(i, there), dst_ref=outs[i].at[me], **sems))
            recvs.append(pltpu.make_async_remote_copy(src_ref=src(i, there), dst_ref=outs[i].at[there], **sems))
    return local, sends, recvs


def _exchange_start(*refs):
    local, sends, _ = _exchange_copies(*refs)
    for cp in local + sends:
        cp.start()


def _exchange_wait(*refs):
    local, sends, recvs = _exchange_copies(*refs)
    for cp in recvs:
        cp.wait_recv()
    for cp in sends:
        cp.wait_send()
    for cp in local:
        cp.wait()


OTHER_CHIPS = (2, 4, 6)


def _relay_copies(ins, outs, send_sems, recv_sems, local_sems):
    pos = _mesh_pos()
    me = _flat(pos)
    sibling = _peer(pos, 1)
    local = [pltpu.make_async_copy(ins[i], outs[i].at[me], local_sems.at[i]) for i in range(len(ins))]
    first, passes, recvs = [], {k: [] for k in OTHER_CHIPS}, {k: [] for k in range(1, N_DEV)}
    for i in range(len(ins)):
        def copy(k, src, block, to):
            return pltpu.make_async_remote_copy(
                src_ref=src, dst_ref=outs[i].at[block], send_sem=send_sems.at[i * 7 + k - 1],
                recv_sem=recv_sems.at[i * 7 + k - 1], device_id=to, device_id_type=MESH)

        for k in (1,) + OTHER_CHIPS:
            first.append(copy(k, ins[i], me, _peer(pos, k)))
        for k in OTHER_CHIPS:
            there = _flat(_peer(pos, k))
            passes[k].append(copy(k ^ 1, outs[i].at[there], there, sibling))
        for k in range(1, N_DEV):
            there = _flat(_peer(pos, k))
            recvs[k].append(copy(k, ins[i], there, _peer(pos, k)))
    return local, first, passes, recvs


def _relay_start(ins, outs, *sems):
    local, first, _, _ = _relay_copies(ins, outs, *sems)
    for cp in local + first:
        cp.start()


def _relay_wait(ins, outs, *sems):
    local, first, passes, recvs = _relay_copies(ins, outs, *sems)
    for k in OTHER_CHIPS:
        for cp in recvs[k]:
            cp.wait_recv()
        for cp in passes[k]:
            cp.start()
    for k in (1, 3, 5, 7):
        for cp in recvs[k]:
            cp.wait_recv()
    for cp in first + [cp for k in OTHER_CHIPS for cp in passes[k]]:
        cp.wait_send()
    for cp in local:
        cp.wait()


def _exchange_out_shape(a, gather):
    return jax.ShapeDtypeStruct((N_DEV,) + tuple(a.shape) if gather else tuple(a.shape), a.dtype)


def _exchange_sems(n):
    return [pltpu.SemaphoreType.DMA((7 * n,)), pltpu.SemaphoreType.DMA((7 * n,)), pltpu.SemaphoreType.DMA((n,))]


ANY = pl.BlockSpec(memory_space=pl.ANY)


def _exchange(arrs, gather, name, relay=False):
    n = len(arrs)

    def body(*refs):
        ins, outs = refs[:n], refs[n:2 * n]
        if relay:
            _relay_start(ins, outs, *refs[2 * n:])
            _relay_wait(ins, outs, *refs[2 * n:])
        else:
            _exchange_start(ins, outs, *refs[2 * n:], gather)
            _exchange_wait(ins, outs, *refs[2 * n:], gather)

    return pl.pallas_call(
        body, name=name, out_shape=tuple(_exchange_out_shape(a, gather) for a in arrs),
        in_specs=[ANY] * n, out_specs=tuple([ANY] * n), scratch_shapes=_exchange_sems(n),
    )(*arrs)


def all_gather(arrs, name, relay=False):
    return _exchange(arrs, True, name, relay)


def all_to_all(arrs, name):
    return _exchange(arrs, False, name)


class Riders:
    def __init__(self, arrs, gather):
        self.gather = gather
        self.rows = [a[1] if isinstance(a, tuple) else None for a in arrs]
        self.arrs = [a[0] if isinstance(a, tuple) else a for a in arrs]

    def out_shapes(self):
        shapes = []
        for a, rows in zip(self.arrs, self.rows):
            shape = tuple(a.shape) if rows is None else (a.shape[0], rows[1], a.shape[2])
            shapes.append(jax.ShapeDtypeStruct((N_DEV,) + shape if self.gather else shape, a.dtype))
        return shapes


def _call(body, *, name, grid, in_specs, out_specs, out_shape, semantics, scratch_shapes=(), riders=None):
    if riders is None or not riders.arrs:
        return pl.pallas_call(body, name=name, grid=grid, in_specs=in_specs, out_specs=tuple(out_specs),
                              out_shape=tuple(out_shape), scratch_shapes=list(scratch_shapes),
                              compiler_params=_cparams(*semantics))
    n_in, n_out, n_scr, n_r = len(in_specs), len(out_specs), len(scratch_shapes), len(riders.arrs)
    gather = riders.gather

    def hosted(*refs):
        ins, r_ins = refs[:n_in], refs[n_in:n_in + n_r]
        at = n_in + n_r
        outs, r_outs = refs[at:at + n_out], refs[at + n_out:at + n_out + n_r]
        at += n_out + n_r
        scratch, sems = refs[at:at + n_scr], refs[at + n_scr:]
        first = functools.reduce(jnp.logical_and, [pl.program_id(a) == 0 for a in range(len(grid))])
        last = functools.reduce(jnp.logical_and, [pl.program_id(a) == grid[a] - 1 for a in range(len(grid))])

        r_ins = [(ref, rows) if rows is not None else ref for ref, rows in zip(r_ins, riders.rows)]

        @pl.when(first)
        def _():
            if gather:
                _relay_start(r_ins, r_outs, *sems)
            else:
                _exchange_start(r_ins, r_outs, *sems, gather)

        body(*ins, *outs, *scratch)

        @pl.when(last)
        def _():
            if gather:
                _relay_wait(r_ins, r_outs, *sems)
            else:
                _exchange_wait(r_ins, r_outs, *sems, gather)

    call = pl.pallas_call(
        hosted, name=name, grid=grid, in_specs=list(in_specs) + [ANY] * n_r, out_specs=tuple(out_specs) + (ANY,) * n_r,
        out_shape=tuple(out_shape) + tuple(riders.out_shapes()),
        scratch_shapes=list(scratch_shapes) + _exchange_sems(n_r),
        compiler_params=_cparams(*(("arbitrary",) * len(grid))))
    return lambda *args: call(*args, *riders.arrs)


def _shards_per_step(shape):
    _, k, n = shape
    best = None
    for q in (1, 2, 4, 8):
        if (q * n) % LANES == 0 and (best is None or 2 * k * q * n <= WEIGHT_BLOCK_BYTES):
            best = q
    return best


def mm_nn(a, b, out_dtype, name, tm=512, riders=None, transposed=False):
    m, k = a.shape
    tm = _tile(m, tm, 8)
    if b.ndim == 3:
        shard = b.shape[1] if transposed else b.shape[2]
        n = N_DEV * shard
        per_step = _shards_per_step((N_DEV, k, shard))
        tn = per_step * shard
        b_spec = pl.BlockSpec((per_step,) + b.shape[1:], lambda i, j: (j, 0, 0))
        contract = (((1,), (1,)), ((), ())) if transposed else (((1,), (0,)), ((), ()))

        def body(a_ref, b_ref, o_ref):
            for q in range(per_step):
                o_ref[:, q * shard:(q + 1) * shard] = lax.dot_general(
                    a_ref[...], b_ref[q], contract, preferred_element_type=f32).astype(o_ref.dtype)
    else:
        n = b.shape[1]
        tn = _weight_tile(n, k)
        b_spec = pl.BlockSpec((k, tn), lambda i, j: (0, j))

        def body(a_ref, b_ref, o_ref):
            o_ref[...] = jnp.dot(a_ref[...], b_ref[...], preferred_element_type=f32).astype(o_ref.dtype)

    return _call(
        body, name=name, grid=(m // tm, n // tn),
        in_specs=[pl.BlockSpec((tm, k), lambda i, j: (i, 0)), b_spec],
        out_specs=[pl.BlockSpec((tm, tn), lambda i, j: (i, j))],
        out_shape=[jax.ShapeDtypeStruct((m, n), out_dtype)], semantics=("parallel", "parallel"), riders=riders,
    )(a, b)


def mm_nn_residual(a, b, x, gate, norm, name, tm=512, riders=None):
    m, k = a.shape
    n = b.shape[1]
    tm = _tile(m, tm, 8)

    def body(a_ref, b_ref, x_ref, g_ref, *rest):
        if norm is not None:
            gn_ref, sc_ref, sh_ref, y_ref, o_ref, h_ref = rest
        else:
            y_ref, o_ref = rest
        y = jnp.dot(a_ref[...], b_ref[...], preferred_element_type=f32)
        y_ref[...] = y.astype(bf16)
        x_new = x_ref[...] + g_ref[...] * y
        o_ref[...] = x_new
        if norm is not None:
            h_ref[...] = _norm_fn(x_new, gn_ref[...], sc_ref[...], sh_ref[...]).astype(bf16)

    blk = pl.BlockSpec((tm, n), lambda i: (i, 0))
    in_specs = [pl.BlockSpec((tm, k), lambda i: (i, 0)), pl.BlockSpec((k, n), lambda i: (0, 0)), blk, _row(n)]
    out_specs = [blk, blk]
    out_shape = [jax.ShapeDtypeStruct((m, n), bf16), jax.ShapeDtypeStruct((m, n), f32)]
    args = [a, b, x, gate]
    if norm is not None:
        in_specs += [_row(n)] * 3
        out_specs += [blk]
        out_shape += [jax.ShapeDtypeStruct((m, n), bf16)]
        args += list(norm)
    return _call(body, name=name, grid=(m // tm,), in_specs=in_specs, out_specs=out_specs, out_shape=out_shape,
                 semantics=("parallel",), riders=riders)(*args)


def mm_nt(a, b, out_dtype, name, tm=512, riders=None, transposed=False):
    m = a.shape[0]
    tm = _tile(m, tm, 8)
    if b.ndim == 3:
        shard, k = (b.shape[1], b.shape[2]) if transposed else (b.shape[2], b.shape[1])
        tk = k
        b_spec = pl.BlockSpec(b.shape, lambda i, j: (0, 0, 0))
        width = N_DEV * shard
        contract = (((1,), (0,)), ((), ())) if transposed else (((1,), (1,)), ((), ()))

        def body(a_ref, b_ref, o_ref):
            acc = None
            for q in range(N_DEV):
                part = lax.dot_general(a_ref[:, q * shard:(q + 1) * shard], b_ref[q], contract, preferred_element_type=f32)
                acc = part if acc is None else acc + part
            o_ref[...] = acc.astype(o_ref.dtype)
    else:
        k, width = b.shape
        tk = _weight_tile(k, width)
        b_spec = pl.BlockSpec((tk, width), lambda i, j: (j, 0))

        def body(a_ref, b_ref, o_ref):
            o_ref[...] = lax.dot_general(a_ref[...], b_ref[...], (((1,), (1,)), ((), ())),
                                         preferred_element_type=f32).astype(o_ref.dtype)

    return _call(
        body, name=name, grid=(m // tm, k // tk),
        in_specs=[pl.BlockSpec((tm, width), lambda i, j: (i, 0)), b_spec],
        out_specs=[pl.BlockSpec((tm, tk), lambda i, j: (i, j))],
        out_shape=[jax.ShapeDtypeStruct((m, k), out_dtype)], semantics=("parallel", "parallel"), riders=riders,
    )(a, b)


def mm_tn(a, b, out_dtype, name, tm=512, tn=512, riders=None):
    t, m = a.shape
    n = b.shape[1]
    tm, tn = _tile(m, tm), _tile(n, tn)

    def body(a_ref, b_ref, o_ref):
        o_ref[...] = lax.dot_general(a_ref[...], b_ref[...], (((0,), (0,)), ((), ())),
                                     preferred_element_type=f32).astype(o_ref.dtype)

    return _call(
        body, name=name, grid=(m // tm, n // tn),
        in_specs=[pl.BlockSpec((t, tm), lambda i, j: (0, i)), pl.BlockSpec((t, tn), lambda i, j: (0, j))],
        out_specs=[pl.BlockSpec((tm, tn), lambda i, j: (i, j))],
        out_shape=[jax.ShapeDtypeStruct((m, n), out_dtype)], semantics=("parallel", "parallel"), riders=riders,
    )(a, b)


def mm_tn_by_owner(a, b, name, tm=512, riders=None):
    t, m = a.shape
    n = b.shape[1]
    shard = n // N_DEV
    per_step = 1 if shard % LANES == 0 else 2
    assert (per_step * shard) % LANES == 0
    tm = _tile(m, tm)

    def body(a_ref, b_ref, o_ref):
        acc = lax.dot_general(a_ref[...], b_ref[...], (((0,), (0,)), ((), ())), preferred_element_type=f32)
        for q in range(per_step):
            o_ref[q] = acc[:, q * shard:(q + 1) * shard].astype(bf16)

    return _call(
        body, name=name, grid=(m // tm, N_DEV // per_step),
        in_specs=[pl.BlockSpec((t, tm), lambda i, j: (0, i)), pl.BlockSpec((t, per_step * shard), lambda i, j: (0, j))],
        out_specs=[pl.BlockSpec((per_step, tm, shard), lambda i, j: (j, i, 0))],
        out_shape=[jax.ShapeDtypeStruct((N_DEV, m, shard), bf16)], semantics=("parallel", "parallel"), riders=riders,
    )(a, b)


def _norm_fn(x, gn, sc, sh):
    r = lax.rsqrt(jnp.mean(x * x, axis=-1, keepdims=True) + EPS)
    return (x * r * gn) * (1.0 + sc) + sh


def _row(d):
    return pl.BlockSpec((1, d), lambda i: (0, 0))


def norm_fwd(x, gn, sc, sh, name, tm=512):
    t, d = x.shape
    tm = _tile(t, tm, 8)

    def body(x_ref, gn_ref, sc_ref, sh_ref, h_ref):
        h_ref[...] = _norm_fn(x_ref[...], gn_ref[...], sc_ref[...], sh_ref[...]).astype(bf16)

    return pl.pallas_call(
        body, name=name, grid=(t // tm,),
        in_specs=[pl.BlockSpec((tm, d), lambda i: (i, 0)), _row(d), _row(d), _row(d)],
        out_specs=pl.BlockSpec((tm, d), lambda i: (i, 0)),
        out_shape=jax.ShapeDtypeStruct((t, d), bf16), compiler_params=_cparams("parallel"),
    )(x, gn, sc, sh)


def _gate_bwd(dx, y_ref, g_ref, dgate_ref, dy_ref):
    dgate_ref[...] += jnp.sum(dx * y_ref[...].astype(f32), axis=0, keepdims=True)
    dy_ref[...] = (dx * g_ref[...]).astype(bf16)


def norm_bwd(x, gn, sc, sh, dh, dres, name, gate=None, tm=512, riders=None):
    t, d = x.shape
    tm = _tile(t, tm, 8)

    def body(x_ref, gn_ref, sc_ref, sh_ref, dh_ref, dres_ref, *rest):
        if gate is not None:
            y_ref, g_ref, dx_ref, dgn_ref, dsc_ref, dsh_ref, dgate_ref, dy_ref = rest
        else:
            dx_ref, dgn_ref, dsc_ref, dsh_ref = rest

        @pl.when(pl.program_id(0) == 0)
        def _():
            dgn_ref[...] = jnp.zeros_like(dgn_ref)
            dsc_ref[...] = jnp.zeros_like(dsc_ref)
            dsh_ref[...] = jnp.zeros_like(dsh_ref)
            if gate is not None:
                dgate_ref[...] = jnp.zeros_like(dgate_ref)

        _, vjp = jax.vjp(_norm_fn, x_ref[...], gn_ref[...], sc_ref[...], sh_ref[...])
        dx, dgn, dsc, dsh = vjp(dh_ref[...].astype(f32))
        dx = dx + dres_ref[...]
        dx_ref[...] = dx
        dgn_ref[...] += dgn
        dsc_ref[...] += dsc
        dsh_ref[...] += dsh
        if gate is not None:
            _gate_bwd(dx, y_ref, g_ref, dgate_ref, dy_ref)

    blk = pl.BlockSpec((tm, d), lambda i: (i, 0))
    vec = jax.ShapeDtypeStruct((1, d), f32)
    in_specs = [blk, _row(d), _row(d), _row(d), blk, blk]
    out_specs = [blk, _row(d), _row(d), _row(d)]
    out_shape = [jax.ShapeDtypeStruct((t, d), f32), vec, vec, vec]
    args = [x, gn, sc, sh, dh, dres]
    if gate is not None:
        in_specs += [blk, _row(d)]
        out_specs += [_row(d), blk]
        out_shape += [vec, jax.ShapeDtypeStruct((t, d), bf16)]
        args += list(gate)
    return _call(body, name=name, grid=(t // tm,), in_specs=in_specs, out_specs=out_specs, out_shape=out_shape,
                 semantics=("arbitrary",), riders=riders)(*args)


def _loss_fn(x, g, tgt):
    r = lax.rsqrt(jnp.mean(x * x, axis=-1, keepdims=True) + EPS)
    err = jnp.square(x * r * g - tgt)
    return 0.5 * jnp.sum(jnp.mean(err, axis=-1, keepdims=True), axis=0, keepdims=True)


def loss_head(x, g, tgt, y, gate, name, tm=512):
    t, d = x.shape
    tm = _tile(t, tm, 8)

    def body(x_ref, g_ref, t_ref, y_ref, gate_ref, loss_ref, dx_ref, dg_ref, dgate_ref, dy_ref):
        @pl.when(pl.program_id(0) == 0)
        def _():
            loss_ref[...] = jnp.zeros_like(loss_ref)
            dg_ref[...] = jnp.zeros_like(dg_ref)
            dgate_ref[...] = jnp.zeros_like(dgate_ref)

        loss, vjp = jax.vjp(_loss_fn, x_ref[...], g_ref[...], t_ref[...])
        dx, dg, _ = vjp(jnp.ones((1, 1), f32))
        dx_ref[...] = dx
        loss_ref[...] += loss
        dg_ref[...] += dg
        _gate_bwd(dx, y_ref, gate_ref, dgate_ref, dy_ref)

    blk = pl.BlockSpec((tm, d), lambda i: (i, 0))
    vec = jax.ShapeDtypeStruct((1, d), f32)
    return pl.pallas_call(
        body, name=name, grid=(t // tm,),
        in_specs=[blk, _row(d), blk, blk, _row(d)],
        out_specs=(pl.BlockSpec((1, 1), lambda i: (0, 0)), blk, _row(d), _row(d), blk),
        out_shape=(jax.ShapeDtypeStruct((1, 1), f32), jax.ShapeDtypeStruct((t, d), f32), vec, vec,
                   jax.ShapeDtypeStruct((t, d), bf16)),
        compiler_params=_cparams("arbitrary"),
    )(x, g, tgt, y, gate)


def _gm_block_fn(z, ws, bs, lng, lnb):
    u = _gelu(z[:, :GM_WIDTH])
    vg = _gelu(z[:, GM_WIDTH:])
    mu = jnp.mean(vg, axis=-1, keepdims=True)
    var = jnp.mean(jnp.square(vg - mu), axis=-1, keepdims=True)
    vn = (vg - mu) * lax.rsqrt(var + EPS) * lng + lnb
    row = lax.broadcasted_iota(jnp.int32, (GM_BLOCK, GM_BLOCK), 0) // CHUNK
    col = lax.broadcasted_iota(jnp.int32, (GM_BLOCK, GM_BLOCK), 1) // CHUNK
    parts = []
    for h in range(GM_HEADS):
        w = jnp.where(row >= col, ws[h], 0.0)
        cols = slice(h * GM_HEAD_DIM, (h + 1) * GM_HEAD_DIM)
        s = jnp.dot(w.astype(bf16), vn[:, cols].astype(bf16), preferred_element_type=f32) + bs[h]
        parts.append(u[:, cols] * s)
    return jnp.concatenate(parts, axis=1)


def _gm_param_specs():
    return [pl.BlockSpec((GM_HEADS, GM_BLOCK, GM_BLOCK), lambda i: (0, 0, 0)),
            pl.BlockSpec((GM_HEADS, GM_BLOCK, 1), lambda i: (0, 0, 0)), _row(GM_WIDTH), _row(GM_WIDTH)]


def gm_mix_fwd(z, ws, bs, lng, lnb, name, riders=None):
    t = z.shape[0]

    def body(z_ref, ws_ref, bs_ref, lng_ref, lnb_ref, o_ref):
        o_ref[...] = _gm_block_fn(z_ref[...].astype(f32), ws_ref[...], bs_ref[...], lng_ref[...],
                                  lnb_ref[...]).astype(bf16)

    return _call(
        body, name=name, grid=(t // GM_BLOCK,),
        in_specs=[pl.BlockSpec((GM_BLOCK, 2 * GM_WIDTH), lambda i: (i, 0))] + _gm_param_specs(),
        out_specs=[pl.BlockSpec((GM_BLOCK, GM_WIDTH), lambda i: (i, 0))],
        out_shape=[jax.ShapeDtypeStruct((t, GM_WIDTH), bf16)], semantics=("parallel",), riders=riders,
    )(z, ws, bs, lng, lnb)


def gm_mix_bwd(z, ws, bs, lng, lnb, dgated, name, riders=None):
    t = z.shape[0]

    def body(z_ref, ws_ref, bs_ref, lng_ref, lnb_ref, dg_ref, dz_ref, dws_ref, dbs_ref, dlng_ref, dlnb_ref):
        _, vjp = jax.vjp(_gm_block_fn, z_ref[...].astype(f32), ws_ref[...], bs_ref[...], lng_ref[...], lnb_ref[...])
        dz, dws, dbs, dlng, dlnb = vjp(dg_ref[...].astype(f32))
        dz_ref[...] = dz.astype(bf16)

        @pl.when(pl.program_id(0) == 0)
        def _():
            dws_ref[...] = jnp.zeros_like(dws_ref)
            dbs_ref[...] = jnp.zeros_like(dbs_ref)
            dlng_ref[...] = jnp.zeros_like(dlng_ref)
            dlnb_ref[...] = jnp.zeros_like(dlnb_ref)

        dws_ref[...] += dws
        dbs_ref[...] += dbs
        dlng_ref[...] += dlng
        dlnb_ref[...] += dlnb

    zblk = pl.BlockSpec((GM_BLOCK, 2 * GM_WIDTH), lambda i: (i, 0))
    return _call(
        body, name=name, grid=(t // GM_BLOCK,),
        in_specs=[zblk] + _gm_param_specs() + [pl.BlockSpec((GM_BLOCK, GM_WIDTH), lambda i: (i, 0))],
        out_specs=[zblk] + _gm_param_specs(),
        out_shape=[jax.ShapeDtypeStruct((t, 2 * GM_WIDTH), bf16),
                   jax.ShapeDtypeStruct((GM_HEADS, GM_BLOCK, GM_BLOCK), f32),
                   jax.ShapeDtypeStruct((GM_HEADS, GM_BLOCK, 1), f32),
                   jax.ShapeDtypeStruct((1, GM_WIDTH), f32), jax.ShapeDtypeStruct((1, GM_WIDTH), f32)],
        semantics=("arbitrary",), riders=riders,
    )(z, ws, bs, lng, lnb, dgated)


@functools.partial(jax.custom_vjp, nondiff_argnums=(1,))
def _rows_up(x, shift):
    return x if shift == 0 else pltpu.roll(x, x.shape[1] - shift, axis=1)


def _rows_up_fwd(x, shift):
    return _rows_up(x, shift), None


def _rows_up_bwd(shift, _, g):
    return (g if shift == 0 else pltpu.roll(g, shift, axis=1),)


_rows_up.defvjp(_rows_up_fwd, _rows_up_bwd)


def _hg_block_fn(qp, fz, iv, gp, s0, lb, gn):
    n, ns, d = HG_SUB, HG_TOKENS // HG_SUB, HG_DIM
    p, nb, per_sub = HG_PAIR, HG_TOKENS // HG_PAIR, HG_SUB // HG_PAIR
    f = lb + (1.0 - lb) * jax.nn.sigmoid(fz)
    g = jnp.log(f)
    k = 1.0 - f
    q = qp * jax.nn.sigmoid(qp)
    v = iv.astype(bf16)
    row = lax.broadcasted_iota(jnp.int32, (HG_TOKENS, HG_TOKENS), 0)
    col = lax.broadcasted_iota(jnp.int32, (HG_TOKENS, HG_TOKENS), 1)
    same_sub = col // n == row // n
    tri = ((col <= row) & same_sub).astype(f32)
    cum = jnp.dot(tri, g, precision=lax.Precision.HIGHEST, preferred_element_type=f32)
    cum_b, q_b, k_b, f_b = cum.reshape(nb, p, d), q.reshape(nb, p, d), k.reshape(nb, p, d), f.reshape(nb, p, d)
    j_b = lax.broadcasted_iota(jnp.int32, (nb, p, d), 1)
    j_col = lax.broadcasted_iota(jnp.int32, (nb, p, 1), 1)
    scores_t = jnp.zeros((HG_TOKENS, HG_TOKENS), f32)
    weight = k_b
    for delta in range(p):
        if delta:
            weight = weight * _rows_up(f_b, delta)
        pair = jnp.sum(_rows_up(q_b, delta) * weight, axis=2, keepdims=True)
        pair = jnp.where(j_col < p - delta, pair, 0.0)
        scores_t = scores_t + jnp.where(col == row + delta, pair.reshape(HG_TOKENS, 1), 0.0)
    o = lax.dot_general(scores_t.astype(bf16), v, (((0,), (0,)), ((), ())), preferred_element_type=f32)
    last = cum_b[:, p - 1:p, :]
    before = jnp.concatenate([jnp.zeros((1, 1, d), f32), last[:-1]], axis=0)
    before = jnp.broadcast_to(before, (nb, p, d)).reshape(HG_TOKENS, d)
    block = (lax.broadcasted_iota(jnp.int32, (HG_TOKENS, d), 0) // p) % per_sub
    q_late = q * jnp.exp(jnp.where(block > 0, cum - before, -1e30))
    last_s = last.reshape(ns, per_sub, d)
    q_parts, k_parts = [], []
    for m in range(1, per_sub):
        split = jnp.broadcast_to(last_s[:, m - 1:m, :], (ns, n, d)).reshape(HG_TOKENS, d)
        k_parts.append(k * jnp.exp(jnp.where(block < m, split - cum, -1e30)))
        q_parts.append(jnp.where(block == m, q_late, 0.0))
    scores = lax.dot_general(jnp.concatenate(q_parts, axis=1).astype(bf16), jnp.concatenate(k_parts, axis=1).astype(bf16),
                             (((1,), (1,)), ((), ())), preferred_element_type=f32)
    o = o + jnp.dot(jnp.where(same_sub, scores, 0.0).astype(bf16), v, preferred_element_type=f32)
    cum_s = cum.reshape(ns, n, d)
    tot = cum_s[:, n - 1:n, :]
    kt_t = (k.reshape(ns, n, d) * jnp.exp(tot - cum_s)).reshape(HG_TOKENS, d).T
    lane_sub = lax.broadcasted_iota(jnp.int32, (d, HG_TOKENS), 1) // n
    k_by_sub = jnp.concatenate([jnp.where(lane_sub == b, kt_t, 0.0) for b in range(ns)], axis=0).astype(bf16)
    update = jnp.dot(k_by_sub, v, preferred_element_type=f32)
    decay = jnp.exp(tot.reshape(ns, d)).T
    state = s0
    states = []
    for a in range(ns):
        states.append(state.astype(bf16))
        state = decay[:, a:a + 1] * state + update[a * d:(a + 1) * d]
    qt = q * jnp.exp(cum)
    row_sub = lax.broadcasted_iota(jnp.int32, (HG_TOKENS, d), 0) // n
    q_by_sub = jnp.concatenate([jnp.where(row_sub == a, qt, 0.0) for a in range(ns)], axis=1).astype(bf16)
    o = o + jnp.dot(q_by_sub, jnp.concatenate(states, axis=0), preferred_element_type=f32)
    on = o * lax.rsqrt(jnp.mean(o * o, axis=-1, keepdims=True) + EPS) * gn
    return on * (gp * jax.nn.sigmoid(gp)), state


def _head_parts(ref, h):
    return [ref[:, p * D + h * HG_DIM:p * D + (h + 1) * HG_DIM] for p in range(4)]


def hg_scan_fwd(proj, lb, gn, name, riders=None):
    t = proj.shape[0]
    nt = t // HG_TOKENS

    def body(p_ref, lb_ref, gn_ref, y_ref, s_ref, state):
        @pl.when(pl.program_id(0) == 0)
        def _():
            state[...] = jnp.zeros_like(state)

        for h in range(HG_HEADS):
            cols = slice(h * HG_DIM, (h + 1) * HG_DIM)
            s_ref[h, 0] = state[h]
            y, s1 = _hg_block_fn(*_head_parts(p_ref, h), state[h], lb_ref[:, cols], gn_ref[:, cols])
            y_ref[:, cols] = y.astype(bf16)
            state[h] = s1

    return _call(
        body, name=name, grid=(nt,),
        in_specs=[pl.BlockSpec((HG_TOKENS, 4 * D), lambda i: (i, 0)), _row(D), _row(D)],
        out_specs=[pl.BlockSpec((HG_TOKENS, D), lambda i: (i, 0)),
                   pl.BlockSpec((HG_HEADS, 1, HG_DIM, HG_DIM), lambda i: (0, i, 0, 0))],
        out_shape=[jax.ShapeDtypeStruct((t, D), bf16), jax.ShapeDtypeStruct((HG_HEADS, nt, HG_DIM, HG_DIM), f32)],
        scratch_shapes=[pltpu.VMEM((HG_HEADS, HG_DIM, HG_DIM), f32)],
        semantics=("arbitrary",), riders=riders,
    )(proj, lb, gn)


def hg_scan_bwd(proj, lb, gn, states, dy, name, riders=None):
    t = proj.shape[0]
    nt = t // HG_TOKENS

    def body(p_ref, lb_ref, gn_ref, s_ref, dy_ref, dp_ref, dlb_ref, dgn_ref, dstate):
        @pl.when(pl.program_id(0) == 0)
        def _():
            dstate[...] = jnp.zeros_like(dstate)
            dlb_ref[...] = jnp.zeros_like(dlb_ref)
            dgn_ref[...] = jnp.zeros_like(dgn_ref)

        for h in range(HG_HEADS):
            cols = slice(h * HG_DIM, (h + 1) * HG_DIM)
            _, vjp = jax.vjp(_hg_block_fn, *_head_parts(p_ref, h), s_ref[h, 0], lb_ref[:, cols], gn_ref[:, cols])
            grads = vjp((dy_ref[:, cols].astype(f32), dstate[h]))
            for p in range(4):
                dp_ref[:, p * D + h * HG_DIM:p * D + (h + 1) * HG_DIM] = grads[p].astype(bf16)
            dstate[h] = grads[4]
            dlb_ref[:, cols] += grads[5]
            dgn_ref[:, cols] += grads[6]

    small = jax.ShapeDtypeStruct((1, D), f32)
    return _call(
        body, name=name, grid=(nt,),
        in_specs=[pl.BlockSpec((HG_TOKENS, 4 * D), lambda i: (nt - 1 - i, 0)), _row(D), _row(D),
                  pl.BlockSpec((HG_HEADS, 1, HG_DIM, HG_DIM), lambda i: (0, nt - 1 - i, 0, 0)),
                  pl.BlockSpec((HG_TOKENS, D), lambda i: (nt - 1 - i, 0))],
        out_specs=[pl.BlockSpec((HG_TOKENS, 4 * D), lambda i: (nt - 1 - i, 0)), _row(D), _row(D)],
        out_shape=[jax.ShapeDtypeStruct((t, 4 * D), bf16), small, small],
        scratch_shapes=[pltpu.VMEM((HG_HEADS, HG_DIM, HG_DIM), f32)],
        semantics=("arbitrary",), riders=riders,
    )(proj, lb, gn, states, dy)


FFN_COLS = 1408
HALO = 8
STRIP = 16


def _ffn_specs(tm):
    nb = tm // HALO
    main_g = pl.BlockSpec((tm, FFN_COLS), lambda j, i: (i, j))
    main_v = pl.BlockSpec((tm, FFN_COLS), lambda j, i: (i, j + 2))
    halo_g = pl.BlockSpec((HALO, FFN_COLS), lambda j, i: (jnp.maximum(i * nb - 1, 0), j))
    halo_v = pl.BlockSpec((HALO, FFN_COLS), lambda j, i: (jnp.maximum(i * nb - 1, 0), j + 2))
    w_g = pl.BlockSpec((3, FFN_COLS), lambda j, i: (0, j))
    w_v = pl.BlockSpec((3, FFN_COLS), lambda j, i: (0, j + 2))
    b_g = pl.BlockSpec((1, FFN_COLS), lambda j, i: (0, j))
    b_v = pl.BlockSpec((1, FFN_COLS), lambda j, i: (0, j + 2))
    return [main_g, halo_g, main_v, halo_v, w_g, w_v, b_g, b_v]


def _strip_rows(r):
    return pl.ds(r * STRIP, STRIP) if isinstance(r, int) else pl.ds(pl.multiple_of(r * STRIP, STRIP), STRIP)


def _for_strips(nstrip, strip, reverse=False):
    if reverse:
        strip(nstrip - 1, True)
        lax.fori_loop(0, nstrip - 1, lambda k, c: (strip(nstrip - 2 - k, False), c)[1], 0)
    else:
        strip(0, True)
        lax.fori_loop(1, nstrip, lambda r, c: (strip(r, False), c)[1], 0)


SUBLANES = 8


def _rows_down(prev, cur, shift):
    row = lax.broadcasted_iota(jnp.int32, (SUBLANES, LANES), 0)
    tiles = [prev[STRIP - SUBLANES:]] + [cur[q * SUBLANES:(q + 1) * SUBLANES] for q in range(STRIP // SUBLANES)]
    turned = [pltpu.roll(x, shift, axis=0) for x in tiles]
    return jnp.concatenate([jnp.where(row < shift, turned[q], turned[q + 1]) for q in range(STRIP // SUBLANES)], axis=0)


def _rows_ahead(cur, nxt, shift):
    row = lax.broadcasted_iota(jnp.int32, (SUBLANES, LANES), 0)
    tiles = [cur[q * SUBLANES:(q + 1) * SUBLANES] for q in range(STRIP // SUBLANES)] + [nxt[:SUBLANES]]
    turned = [pltpu.roll(x, SUBLANES - shift, axis=0) for x in tiles]
    return jnp.concatenate([jnp.where(row >= SUBLANES - shift, turned[q + 1], turned[q])
                            for q in range(STRIP // SUBLANES)], axis=0)


def _conv_strip(main_ref, halo_ref, w_ref, b_ref, r, edge, cols, first_block):
    cur = main_ref[_strip_rows(r), cols].astype(f32)
    if edge:
        h = jnp.where(first_block, 0.0, halo_ref[:, cols].astype(f32))
        prev = jnp.concatenate([jnp.zeros_like(h), h], axis=0)
    else:
        prev = main_ref[_strip_rows(r - 1), cols].astype(f32)
    a1, a2 = _rows_down(prev, cur, 1), _rows_down(prev, cur, 2)
    y = b_ref[:, cols] + w_ref[0:1, cols] * a2 + w_ref[1:2, cols] * a1 + w_ref[2:3, cols] * cur
    return y, (cur, a1, a2)


def ffn_gate_fwd(a, cw, cb, name, tm=512, riders=None):
    t = a.shape[0]
    tm = _tile(t, tm, STRIP)

    def body(ag_ref, hg_ref, av_ref, hv_ref, wg_ref, wv_ref, bg_ref, bv_ref, o_ref):
        first_block = pl.program_id(1) == 0

        def strip(r, edge):
            for c in range(FFN_COLS // LANES):
                cols = pl.ds(c * LANES, LANES)
                yg, _ = _conv_strip(ag_ref, hg_ref, wg_ref, bg_ref, r, edge, cols, first_block)
                yv, _ = _conv_strip(av_ref, hv_ref, wv_ref, bv_ref, r, edge, cols, first_block)
                o_ref[_strip_rows(r), cols] = (_gelu(yg) * yv).astype(bf16)

        _for_strips(tm // STRIP, strip)

    return _call(
        body, name=name, grid=(2, t // tm), in_specs=_ffn_specs(tm),
        out_specs=[pl.BlockSpec((tm, FFN_COLS), lambda j, i: (i, j))],
        out_shape=[jax.ShapeDtypeStruct((t, FFN_HIDDEN), bf16)],
        semantics=("parallel", "arbitrary"), riders=riders,
    )(a, a, a, a, cw, cw, cb, cb)


def _conv_tile(prev, cur, w_ref, b_ref, cols):
    row = lax.broadcasted_iota(jnp.int32, (SUBLANES, LANES), 0)
    a1 = jnp.where(row < 1, pltpu.roll(prev, 1, axis=0), pltpu.roll(cur, 1, axis=0))
    a2 = jnp.where(row < 2, pltpu.roll(prev, 2, axis=0), pltpu.roll(cur, 2, axis=0))
    return b_ref[:, cols] + w_ref[0:1, cols] * a2 + w_ref[1:2, cols] * a1 + w_ref[2:3, cols] * cur


def _gate_grads(yg, yv, dh):
    cdf = 0.5 * (1.0 + lax.erf(yg * (1.0 / math.sqrt(2.0))))
    pdf = jnp.exp(-0.5 * yg * yg) * (1.0 / math.sqrt(2.0 * math.pi))
    return dh * yv * (cdf + yg * pdf), dh * (yg * cdf)


def ffn_gate_bwd(a, cw, cb, dhid, name, tm=512, riders=None):
    t = a.shape[0]
    tm = _tile(t, tm, STRIP)
    nb = tm // HALO
    last_halo = t // HALO - 1
    nstrip = tm // STRIP
    fh = FFN_HIDDEN

    def body(a_ref, ha_ref, na_ref, w_ref, b_ref, dh_ref, ndh_ref, da_ref, dw_ref, db_ref, acc, dybuf):
        first_block = pl.program_id(0) == 0
        last_block = pl.program_id(0) == pl.num_programs(0) - 1

        @pl.when(first_block)
        def _():
            acc[...] = jnp.zeros_like(acc)

        def transposed_conv(dy, ahead, cols):
            return (w_ref[2:3, cols] * dy + w_ref[1:2, cols] * _rows_ahead(dy, ahead, 1)
                    + w_ref[0:1, cols] * _rows_ahead(dy, ahead, 2)).astype(bf16)

        def strip(r, edge):
            rows = _strip_rows(r)
            for c in range(fh // LANES):
                gate, val = pl.ds(c * LANES, LANES), pl.ds(fh + c * LANES, LANES)
                yg, taps_g = _conv_strip(a_ref, ha_ref, w_ref, b_ref, r, edge, gate, first_block)
                yv, taps_v = _conv_strip(a_ref, ha_ref, w_ref, b_ref, r, edge, val, first_block)
                dyg, dyv = _gate_grads(yg, yv, dh_ref[rows, gate].astype(f32))
                for p, (dy, (a0, a1, a2), cols) in enumerate(((dyg, taps_g, gate), (dyv, taps_v, val))):
                    acc[0, :, cols] += dy * a2
                    acc[1, :, cols] += dy * a1
                    acc[2, :, cols] += dy * a0
                    acc[3, :, cols] += dy
                    if not edge:
                        da_ref[_strip_rows(r - 1), cols] = transposed_conv(dybuf[:, cols], dy, cols)
                    dybuf[:, cols] = dy

        _for_strips(nstrip, strip)

        last_rows = pl.ds((nstrip - 1) * STRIP + SUBLANES, SUBLANES)
        for c in range(fh // LANES):
            gate, val = pl.ds(c * LANES, LANES), pl.ds(fh + c * LANES, LANES)
            yg = _conv_tile(a_ref[last_rows, gate].astype(f32), na_ref[:, gate].astype(f32), w_ref, b_ref, gate)
            yv = _conv_tile(a_ref[last_rows, val].astype(f32), na_ref[:, val].astype(f32), w_ref, b_ref, val)
            for dy, cols in zip(_gate_grads(yg, yv, ndh_ref[:, gate].astype(f32)), (gate, val)):
                dy = jnp.where(last_block, 0.0, dy)
                ahead = jnp.concatenate([dy, jnp.zeros_like(dy)], axis=0)
                da_ref[_strip_rows(nstrip - 1), cols] = transposed_conv(dybuf[:, cols], ahead, cols)

        @pl.when(last_block)
        def _():
            for tap in range(3):
                dw_ref[tap:tap + 1, :] = jnp.sum(acc[tap], axis=0, keepdims=True)
            db_ref[...] = jnp.sum(acc[3], axis=0, keepdims=True)

    def after(i):
        return jnp.minimum((i + 1) * nb, last_halo)

    return _call(
        body, name=name, grid=(t // tm,),
        in_specs=[pl.BlockSpec((tm, 2 * fh), lambda i: (i, 0)),
                  pl.BlockSpec((HALO, 2 * fh), lambda i: (jnp.maximum(i * nb - 1, 0), 0)),
                  pl.BlockSpec((HALO, 2 * fh), lambda i: (after(i), 0)),
                  pl.BlockSpec((3, 2 * fh), lambda i: (0, 0)), pl.BlockSpec((1, 2 * fh), lambda i: (0, 0)),
                  pl.BlockSpec((tm, fh), lambda i: (i, 0)), pl.BlockSpec((HALO, fh), lambda i: (after(i), 0))],
        out_specs=[pl.BlockSpec((tm, 2 * fh), lambda i: (i, 0)), pl.BlockSpec((3, 2 * fh), lambda i: (0, 0)),
                   pl.BlockSpec((1, 2 * fh), lambda i: (0, 0))],
        out_shape=[jax.ShapeDtypeStruct((t, 2 * fh), bf16), jax.ShapeDtypeStruct((3, 2 * fh), f32),
                   jax.ShapeDtypeStruct((1, 2 * fh), f32)],
        scratch_shapes=[pltpu.VMEM((4, STRIP, 2 * fh), f32), pltpu.VMEM((STRIP, 2 * fh), f32)],
        semantics=("arbitrary",), riders=riders,
    )(a, a, a, cw, cb, dhid, dhid)


def ada_mod(c_all, ada_w, ada_b_cols, name):
    cols = ada_w.shape[2]

    def body(c_ref, w_ref, b_ref, o_ref):
        c = c_ref[...]
        cond = (c * jax.nn.sigmoid(c)).astype(bf16)
        o_ref[0] = jnp.dot(cond, w_ref[0].astype(bf16), preferred_element_type=f32) + b_ref[0]

    return pl.pallas_call(
        body, name=name, grid=(DEPTH,),
        in_specs=[pl.BlockSpec((N_DEV, D), lambda i: (0, 0)), pl.BlockSpec((1, D, cols), lambda i: (i, 0, 0)),
                  pl.BlockSpec((1, 1, cols), lambda i: (i, 0, 0))],
        out_specs=pl.BlockSpec((1, N_DEV, cols), lambda i: (i, 0, 0)),
        out_shape=jax.ShapeDtypeStruct((DEPTH, N_DEV, cols), f32), compiler_params=_cparams("parallel"),
    )(c_all, ada_w, ada_b_cols)


def ada_grads(c_all, dmod_cols, dmod_all, name):
    cols = dmod_cols.shape[2]

    def body(c_ref, dm_ref, da_ref, dw_ref, db_ref):
        c = c_ref[...]
        cond = c * jax.nn.sigmoid(c)
        dw_ref[0] = lax.dot_general(cond, dm_ref[0], (((0,), (0,)), ((), ())), precision=lax.Precision.HIGHEST,
                                    preferred_element_type=f32)
        acc = da_ref[0, 0]
        for e in range(1, N_DEV):
            acc = acc + da_ref[e, 0]
        db_ref[0] = acc

    return pl.pallas_call(
        body, name=name, grid=(DEPTH,),
        in_specs=[pl.BlockSpec((N_DEV, D), lambda i: (0, 0)), pl.BlockSpec((1, N_DEV, cols), lambda i: (i, 0, 0)),
                  pl.BlockSpec((N_DEV, 1, 1, 6 * D), lambda i: (0, i, 0, 0))],
        out_specs=(pl.BlockSpec((1, D, cols), lambda i: (i, 0, 0)), pl.BlockSpec((1, 1, 6 * D), lambda i: (i, 0, 0))),
        out_shape=(jax.ShapeDtypeStruct((DEPTH, D, cols), f32), jax.ShapeDtypeStruct((DEPTH, 1, 6 * D), f32)),
        compiler_params=_cparams("parallel"),
    )(c_all, dmod_cols, dmod_all)


def lower_bound_fwd(hg_lb, name):
    n = hg_lb.shape[1]

    def body(l_ref, o_ref):
        o_ref[...] = jax.nn.sigmoid(l_ref[1:2, :] - l_ref[0:1, :])

    return pl.pallas_call(body, name=name, out_shape=jax.ShapeDtypeStruct((1, n), f32))(hg_lb)


def lower_bound_bwd(hg_lb, dlb, name):
    n = hg_lb.shape[1]

    def body(l_ref, d_ref, o_ref):
        p = jax.nn.sigmoid(l_ref[1:2, :] - l_ref[0:1, :])
        g = d_ref[...] * p * (1.0 - p)
        o_ref[0:1, :] = -g
        o_ref[1:2, :] = g

    return pl.pallas_call(body, name=name, out_shape=jax.ShapeDtypeStruct((2, n), f32))(hg_lb, dlb)


def _adamw(w, g, m, v):
    m = ADAM_B1 * m + (1.0 - ADAM_B1) * g
    v = ADAM_B2 * v + (1.0 - ADAM_B2) * jnp.square(g)
    m_hat = m / (1.0 - ADAM_B1 ** ADAM_STEP)
    v_hat = v / (1.0 - ADAM_B2 ** ADAM_STEP)
    delta = -ADAM_LR * (m_hat / (jnp.sqrt(v_hat) + ADAM_EPS) + ADAM_WD * w)
    return delta, m, v


ADAM_BLOCK_BYTES = 32 * 1024 * 1024


def adam_reduced(parts, w, m, v, name):
    layers, r, c = w.shape
    outs = None
    for layer in range(layers):
        outs = _adam_layer(parts[layer], w, m, v, layer, outs, f"{name}_{layer}")
    return outs


def _adam_layer(parts, w, m, v, layer, prev, name):
    layers, r, c = w.shape
    rows = parts[0].shape[1]
    assert all(p.shape == (N_DEV, rows, c) for p in parts) and rows * len(parts) == r
    row_bytes = 2 * (len(parts) * N_DEV * c * 2 + 7 * c * 4)
    tr = _tile(rows, max(16, ADAM_BLOCK_BYTES // row_bytes), 16)
    steps = rows // tr
    n_prev = 0 if prev is None else 4

    def body(*refs):
        p_refs = refs[:len(parts)]
        w_ref, m_ref, v_ref = refs[len(parts):len(parts) + 3]
        g_ref, d_ref, mo_ref, vo_ref = refs[len(parts) + 3 + n_prev:]
        for idx in range(len(parts)):
            @pl.when(pl.program_id(0) == idx)
            def _():
                g = p_refs[idx][0].astype(f32)
                for j in range(1, N_DEV):
                    g = g + p_refs[idx][j].astype(f32)
                g_ref[...] = g
                d_ref[...], mo_ref[...], vo_ref[...] = _adamw(w_ref[...], g, m_ref[...], v_ref[...])

    def part_spec(idx):
        return pl.BlockSpec((N_DEV, tr, c), lambda p, i: (0, jnp.where(p == idx, i, 0), 0))

    blk = pl.BlockSpec((None, tr, c), lambda p, i: (layer, p * steps + i, 0))
    out = jax.ShapeDtypeStruct((layers, r, c), f32)
    n_in = len(parts) + 3
    return pl.pallas_call(
        body, name=name, grid=(len(parts), steps),
        in_specs=[part_spec(idx) for idx in range(len(parts))] + [blk, blk, blk] + [ANY] * n_prev,
        out_specs=(blk, blk, blk, blk), out_shape=(out, out, out, out),
        input_output_aliases={n_in + k: k for k in range(n_prev)},
        compiler_params=_cparams("arbitrary", "arbitrary"),
    )(*parts, w, m, v, *(prev or ()))


def adam_plain(g, w, m, v, name, tr=256):
    r, c = w.shape
    tr = _tile(r, tr, 8)

    def body(g_ref, w_ref, m_ref, v_ref, d_ref, mo_ref, vo_ref):
        d_ref[...], mo_ref[...], vo_ref[...] = _adamw(w_ref[...], g_ref[...], m_ref[...], v_ref[...])

    blk = pl.BlockSpec((tr, c), lambda i: (i, 0))
    out = jax.ShapeDtypeStruct((r, c), f32)
    return pl.pallas_call(
        body, name=name, grid=(r // tr,), in_specs=[blk, blk, blk, blk], out_specs=(blk, blk, blk),
        out_shape=(out, out, out), compiler_params=_cparams("parallel"),
    )(g, w, m, v)


def sum_parts(parts, name):
    _, r, c = parts.shape

    def body(p_ref, o_ref):
        acc = p_ref[0]
        for j in range(1, N_DEV):
            acc = acc + p_ref[j]
        o_ref[...] = acc

    return pl.pallas_call(body, name=name, out_shape=jax.ShapeDtypeStruct((r, c), f32))(parts)


def _pack(arrs, rows_mult=8):
    flat = jnp.concatenate([a.reshape(-1) for a in arrs])
    rows = -(-flat.shape[0] // LANES)
    rows = -(-rows // rows_mult) * rows_mult
    return jnp.pad(flat, (0, rows * LANES - flat.shape[0])).reshape(rows, LANES)


def _unpack(flat, shapes):
    out, at = [], 0
    for s in shapes:
        n = math.prod(s)
        out.append(flat[at:at + n].reshape(s))
        at += n
    return out


def kernel(x, c, gm_w_in, gm_ln_g, gm_ln_b, gm_w_s, gm_b_s, gm_w_out, hg_w_in, hg_lb, hg_gn_g, hg_w_out, ffn_w_up, ffn_conv_w, ffn_conv_b, ffn_w_down, norm_g, ada_w, ada_b, final_g, loss_target, m_gm_w_in, m_gm_ln_g, m_gm_ln_b, m_gm_w_s, m_gm_b_s, m_gm_w_out, m_hg_w_in, m_hg_lb, m_hg_gn_g, m_hg_w_out, m_ffn_w_up, m_ffn_conv_w, m_ffn_conv_b, m_ffn_w_down, m_norm_g, m_ada_w, m_ada_b, m_final_g, v_gm_w_in, v_gm_ln_g, v_gm_ln_b, v_gm_w_s, v_gm_b_s, v_gm_w_out, v_hg_w_in, v_hg_lb, v_hg_gn_g, v_hg_w_out, v_ffn_w_up, v_ffn_conv_w, v_ffn_conv_b, v_ffn_w_down, v_norm_g, v_ada_w, v_ada_b, v_final_g):
    me = _flat(_mesh_pos())
    xt = x[0]
    t = xt.shape[0]

    small_shapes = [(1, D), (2, HG_DIM), (2, HG_DIM), (DEPTH, 2, HG_DIM), (DEPTH, 3, 2 * FFN_HIDDEN // N_DEV)]
    w_in_0 = gm_w_in[0].astype(bf16)
    small_all, w_in_0_all = all_gather([_pack([c, hg_lb, hg_gn_g, norm_g, ffn_conv_w]), w_in_0], "gather_first", relay=True)
    small_all = small_all.reshape(N_DEV, -1)
    at = 0
    pieces = []
    for s in small_shapes:
        n = math.prod(s)
        pieces.append(small_all[:, at:at + n].reshape((N_DEV,) + s))
        at += n
    c_all = pieces[0].reshape(N_DEV, D)
    hg_lb_full = jnp.transpose(pieces[1], (1, 0, 2)).reshape(2, D)
    hg_gn_full = jnp.transpose(pieces[2], (1, 0, 2)).reshape(2, D)
    norm_g_full = jnp.transpose(pieces[3], (1, 2, 0, 3)).reshape(DEPTH, 2, D)
    conv_w_full = jnp.transpose(pieces[4], (1, 2, 0, 3)).reshape(DEPTH, 3, 2 * FFN_HIDDEN)

    lb1 = lower_bound_fwd(hg_lb_full, "lower_bound")
    lbs = [jnp.zeros((1, D), f32), lb1]

    ada_b_cols = lax.dynamic_slice(ada_b, (0, me * ADA_COLS), (DEPTH, ADA_COLS)).reshape(DEPTH, 1, ADA_COLS)
    mod_cols = ada_mod(c_all, ada_w, ada_b_cols, "ada_mod")
    (mod_mine,) = all_to_all([jnp.transpose(mod_cols, (1, 0, 2))], "mod_to_examples")
    mod = jnp.transpose(mod_mine, (1, 0, 2)).reshape(DEPTH, 6, 1, D)

    def layer_shards(i):
        j = i // 2
        w_in, w_out = (gm_w_in, gm_w_out) if i % 2 == 0 else (hg_w_in, hg_w_out)
        return [w_in[j].astype(bf16), w_out[j].astype(bf16), ffn_w_up[i].T.astype(bf16), ffn_w_down[i].astype(bf16)]

    def full_rows(g):
        return g.reshape(N_DEV * g.shape[1], g.shape[2])

    carried_by = {
        "in_0": [(0, 1), (0, 3)], "mix_0": [(0, 2)], "up_0": [(1, 0), (1, 1)], "gate_0": [(1, 2)], "down_0": [(1, 3)],
        "in_1": [(2, 0)], "mix_1": [(2, 1), (2, 2), (2, 3)], "up_1": [(3, 0), (3, 1)], "gate_1": [(3, 2)], "down_1": [(3, 3)],
    }
    shards = [layer_shards(i) for i in range(DEPTH)]
    gathered = {}
    gathered[(0, 0)] = w_in_0_all

    def carry(call, site, **kw):
        items = carried_by.get(site, [])
        outs = call(riders=Riders([shards[l][slot] for l, slot in items], True), **kw)
        for item, g in zip(items, outs[len(outs) - len(items):]):
            gathered[item] = g
        return outs[:len(outs) - len(items)]

    saved = []
    weights = []
    xcur = xt
    h = norm_fwd(xcur, norm_g_full[0, 0:1], mod[0, 1], mod[0, 0], "norm1_0")
    for i in range(DEPTH):
        j = i // 2
        sh1, sc1, g1, sh2, sc2, g2 = [mod[i, p] for p in range(6)]
        gn2 = norm_g_full[i, 1:2]
        s = {"x0": xcur, "h": h}
        w_in = gathered[(i, 0)]
        if i % 2 == 0:
            (z,) = carry(functools.partial(mm_nn, h, w_in, bf16, f"gm_in_{i}"), f"in_{i}")
            bs = gm_b_s[j].reshape(GM_HEADS, GM_BLOCK, 1)
            (mixed,) = carry(functools.partial(gm_mix_fwd, z, gm_w_s[j], bs, gm_ln_g[j:j + 1], gm_ln_b[j:j + 1],
                                               f"gm_mix_{i}"), f"mix_{i}")
            s["z"] = z
        else:
            (proj,) = carry(functools.partial(mm_nn, h, w_in, f32, f"hg_in_{i}"), f"in_{i}")
            mixed, states = carry(functools.partial(hg_scan_fwd, proj, lbs[j], hg_gn_full[j:j + 1], f"hg_scan_{i}"),
                                  f"mix_{i}")
            s["proj"], s["states"] = proj, states
        s["mixed"] = mixed
        w_out = full_rows(gathered[(i, 1)])
        y, x1, h2 = carry(functools.partial(mm_nn_residual, mixed, w_out, xcur, g1, (gn2, sc2, sh2), f"mix_out_{i}"),
                          f"out_{i}")
        s["y"], s["x1"] = y, x1
        w_up = gathered[(i, 2)]
        (a,) = carry(functools.partial(mm_nn, h2, w_up, bf16, f"ffn_up_{i}", transposed=True), f"up_{i}")
        (hid,) = carry(functools.partial(ffn_gate_fwd, a, conv_w_full[i], ffn_conv_b[i:i + 1], f"ffn_gate_{i}"), f"gate_{i}")
        w_down = full_rows(gathered[(i, 3)])
        next_norm = (norm_g_full[i + 1, 0:1], mod[i + 1, 1], mod[i + 1, 0]) if i + 1 < DEPTH else None
        outs = carry(functools.partial(mm_nn_residual, hid, w_down, x1, g2, next_norm, f"ffn_down_{i}"), f"down_{i}")
        fo, x2 = outs[0], outs[1]
        s["h2"], s["a"], s["hid"], s["f"] = h2, a, hid, fo
        weights.append((w_in, w_out, w_up, w_down))
        saved.append(s)
        xcur = x2
        h = outs[2] if next_norm is not None else None

    loss_part, dx, d_final_g, dg2, df = loss_head(xcur, final_g.reshape(1, D), loss_target[0], saved[-1]["f"],
                                                  mod[DEPTH - 1, 5], "loss_head")
    loss = lax.psum(loss_part[0, 0], ("x", "y", "c"))

    def halves(blocked):
        rows = blocked.shape[1] // 2
        return [(blocked, (0, rows)), (blocked, (rows, rows))]

    def by_owner_rows(dw):
        k, n = dw.shape
        return dw.reshape(N_DEV, k // N_DEV, n)

    received = [[[] for _ in range(4)] for _ in range(DEPTH)]

    def send(call, items, **kw):
        outs = call(riders=Riders([arr for arr, _ in items], False), **kw)
        for (_, (layer, slot)), got in zip(items, outs[len(outs) - len(items):]):
            received[layer][slot].append(got)
        return outs[:len(outs) - len(items)]

    dmod = [None] * DEPTH
    d_norm_g = [None] * DEPTH
    d_gm = {k: [None, None] for k in ("ws", "bs", "lng", "lnb")}
    d_hg = {k: [None, None] for k in ("lb", "gn")}
    d_ffn = {k: [None] * DEPTH for k in ("cw", "cb")}
    in_halves = []
    for i in reversed(range(DEPTH)):
        j = i // 2
        s = saved[i]
        w_in, w_out, w_up, w_down = weights[i]
        sh1, sc1, g1, sh2, sc2, g2 = [mod[i, p] for p in range(6)]
        gn1, gn2 = norm_g_full[i, 0:1], norm_g_full[i, 1:2]
        (dw_down,) = mm_tn(s["hid"], df, bf16, f"dw_down_{i}", tn=D)
        scan_carries = i % 2 == 1
        down_item = (by_owner_rows(dw_down), (i, 3))
        (dhid,) = mm_nt(df, w_down, bf16, f"dhid_{i}")
        da, d_ffn["cw"][i], d_ffn["cb"][i] = send(
            functools.partial(ffn_gate_bwd, s["a"], conv_w_full[i], ffn_conv_b[i:i + 1], dhid, f"ffn_gate_bwd_{i}"),
            in_halves)
        (dw_up_t,) = send(functools.partial(mm_tn, da, s["h2"], bf16, f"dw_up_{i}"), [] if scan_carries else [down_item])
        dw_up_t = dw_up_t.reshape(N_DEV, -1, D)
        up_halves = [(dw_up_t, (i, 2))] if scan_carries else [(part, (i, 2)) for part in halves(dw_up_t)]
        (dh2,) = send(functools.partial(mm_nt, da, w_up, bf16, f"dh2_{i}", transposed=True),
                      [] if scan_carries else up_halves[:1])
        dx1, dgn2, dsc2, dsh2, dg1, dy = norm_bwd(s["x1"], gn2, sc2, sh2, dh2, dx, f"norm2_bwd_{i}", gate=(s["y"], g1))
        (dw_out,) = mm_tn(s["mixed"], dy, bf16, f"dw_mix_out_{i}")
        (dmixed,) = mm_nt(dy, w_out, bf16, f"dmixed_{i}")
        if i % 2 == 0:
            bs = gm_b_s[j].reshape(GM_HEADS, GM_BLOCK, 1)
            dpre, dws, dbs, dlng, dlnb = send(
                functools.partial(gm_mix_bwd, s["z"], gm_w_s[j], bs, gm_ln_g[j:j + 1], gm_ln_b[j:j + 1], dmixed,
                                  f"gm_mix_bwd_{i}"), up_halves[1:])
            d_gm["ws"][j], d_gm["bs"][j], d_gm["lng"][j], d_gm["lnb"][j] = dws, dbs.reshape(GM_HEADS, GM_BLOCK), dlng, dlnb
        else:
            dpre, dlb, dgn = send(
                functools.partial(hg_scan_bwd, s["proj"], lbs[j], hg_gn_full[j:j + 1], s["states"], dmixed,
                                  f"hg_scan_bwd_{i}"), [down_item] + up_halves)
            d_hg["lb"][j], d_hg["gn"][j] = dlb, dgn
        (dw_in,) = send(functools.partial(mm_tn_by_owner, s["h"], dpre, f"dw_mix_in_{i}"), [(by_owner_rows(dw_out), (i, 1))])
        in_halves = [(part, (i, 0)) for part in halves(dw_in)]
        (dh,) = send(functools.partial(mm_nt, dpre, w_in, bf16, f"dh_mix_{i}"), in_halves[:1] if i == 0 else [])
        dmod_i = [None, None, dg1, dsh2, dsc2, dg2]
        if i > 0:
            dx, dgn1, dsc1, dsh1, dg2, df = norm_bwd(s["x0"], gn1, sc1, sh1, dh, dx1, f"norm1_bwd_{i}",
                                                     gate=(saved[i - 1]["f"], mod[i - 1, 5]))
        else:
            dx, dgn1, dsc1, dsh1 = send(functools.partial(norm_bwd, s["x0"], gn1, sc1, sh1, dh, dx1, f"norm1_bwd_{i}"),
                                        in_halves[1:])
        dmod_i[0], dmod_i[1] = dsh1, dsc1
        dmod[i] = jnp.concatenate(dmod_i, axis=1)
        d_norm_g[i] = jnp.concatenate([dgn1, dgn2], axis=0)
    grad_x = dx.reshape(1, t, D)

    (dmod_all,) = all_gather([jnp.concatenate(dmod, axis=0)], "gather_dmod")
    dmod_cols = jnp.transpose(lax.dynamic_slice(dmod_all, (0, 0, me * ADA_COLS), (N_DEV, DEPTH, ADA_COLS)), (1, 0, 2))
    g_ada_w, g_ada_b = ada_grads(c_all, dmod_cols, dmod_all.reshape(N_DEV, DEPTH, 1, 6 * D), "ada_grads")
    g_ada_b = g_ada_b.reshape(DEPTH, 6 * D)

    small_partials = [jnp.concatenate(d_gm["lng"], axis=0), jnp.concatenate(d_gm["lnb"], axis=0),
                      jnp.stack(d_gm["ws"]), jnp.stack(d_gm["bs"]), jnp.concatenate(d_ffn["cb"], axis=0),
                      d_final_g, d_hg["lb"][1], jnp.concatenate(d_hg["gn"], axis=0), jnp.stack(d_norm_g),
                      jnp.stack(d_ffn["cw"])]
    partial_shapes = [p.shape for p in small_partials]
    packed = _pack(small_partials, rows_mult=8 * N_DEV)
    rows = packed.shape[0] // N_DEV
    (recv,) = all_to_all([packed.reshape(N_DEV, rows, LANES)], "small_grads_exchange")
    (summed,) = all_gather([sum_parts(recv, "small_grads_sum")], "small_grads_gather")

    def parts_of(slot, layers):
        return [received[i][slot] for i in layers]

    def swapped(a):
        return jnp.swapaxes(a, 1, 2)

    w_shards = [gm_w_in, gm_w_out, hg_w_in, hg_w_out, swapped(ffn_w_up), ffn_w_down]
    big_parts = [parts_of(0, (0, 2)), parts_of(1, (0, 2)), parts_of(0, (1, 3)), parts_of(1, (1, 3)),
                 parts_of(2, range(DEPTH)), parts_of(3, range(DEPTH))]
    big_m = [m_gm_w_in, m_gm_w_out, m_hg_w_in, m_hg_w_out, swapped(m_ffn_w_up), m_ffn_w_down]
    big_v = [v_gm_w_in, v_gm_w_out, v_hg_w_in, v_hg_w_out, swapped(v_ffn_w_up), v_ffn_w_down]
    big = [adam_reduced(parts, w, m_, v_, f"adam_big_{idx}")
           for idx, (w, m_, v_, parts) in enumerate(zip(w_shards, big_m, big_v, big_parts))]
    big[4] = [swapped(o) for o in big[4]]
    (g_gm_w_in, d_gm_w_in, nm_gm_w_in, nv_gm_w_in), (g_gm_w_out, d_gm_w_out, nm_gm_w_out, nv_gm_w_out), \
        (g_hg_w_in, d_hg_w_in, nm_hg_w_in, nv_hg_w_in), (g_hg_w_out, d_hg_w_out, nm_hg_w_out, nv_hg_w_out), \
        (g_ffn_w_up, d_ffn_w_up, nm_ffn_w_up, nv_ffn_w_up), (g_ffn_w_down, d_ffn_w_down, nm_ffn_w_down, nv_ffn_w_down) = big

    g_ln_g, g_ln_b, g_ws, g_bs, g_cb, g_final, g_lb1, g_gn, g_norm, g_cw = _unpack(summed.reshape(-1), partial_shapes)
    g_final = g_final.reshape(D)

    def my_cols(a, n):
        start = (0,) * (a.ndim - 1) + (me * n,)
        return lax.dynamic_slice(a, start, a.shape[:-1] + (n,))

    g_hg_lb = lower_bound_bwd(hg_lb, my_cols(g_lb1, HG_DIM), "lower_bound_bwd")
    g_hg_gn = my_cols(g_gn, HG_DIM)
    g_norm_g = my_cols(g_norm, HG_DIM)
    g_conv_w = my_cols(g_cw, 2 * FFN_HIDDEN // N_DEV)

    two_d = (-1, ADA_COLS)
    d_ada_w, nm_ada_w, nv_ada_w = [o.reshape(ada_w.shape) for o in adam_plain(
        g_ada_w.reshape(two_d), ada_w.reshape(two_d), m_ada_w.reshape(two_d), v_ada_w.reshape(two_d), "adam_ada_w")]

    small_g = [g_ln_g, g_ln_b, g_ws, g_bs, g_cb, g_ada_b, g_final, g_hg_lb, g_hg_gn, g_norm_g, g_conv_w]
    small_w = [gm_ln_g, gm_ln_b, gm_w_s, gm_b_s, ffn_conv_b, ada_b, final_g, hg_lb, hg_gn_g, norm_g, ffn_conv_w]
    small_m = [m_gm_ln_g, m_gm_ln_b, m_gm_w_s, m_gm_b_s, m_ffn_conv_b, m_ada_b, m_final_g, m_hg_lb, m_hg_gn_g, m_norm_g, m_ffn_conv_w]
    small_v = [v_gm_ln_g, v_gm_ln_b, v_gm_w_s, v_gm_b_s, v_ffn_conv_b, v_ada_b, v_final_g, v_hg_lb, v_hg_gn_g, v_norm_g, v_ffn_conv_w]
    shapes = [w.shape for w in small_w]
    small_g = [g.reshape(s) for g, s in zip(small_g, shapes)]
    outs = adam_plain(_pack(small_g), _pack(small_w), _pack(small_m), _pack(small_v), "adam_small")
    (d_ln_g, d_ln_b, d_ws, d_bs, d_cb, d_ada_b, d_final, d_hg_lb, d_hg_gn, d_norm_g_, d_conv_w), \
        (nm_ln_g, nm_ln_b, nm_ws, nm_bs, nm_cb, nm_ada_b, nm_final, nm_hg_lb, nm_hg_gn, nm_norm_g, nm_conv_w), \
        (nv_ln_g, nv_ln_b, nv_ws, nv_bs, nv_cb, nv_ada_b, nv_final, nv_hg_lb, nv_hg_gn, nv_norm_g, nv_conv_w) = [
            _unpack(o.reshape(-1), shapes) for o in outs]
    g_ln_g, g_ln_b, g_ws, g_bs, g_cb, g_ada_b, g_final, g_hg_lb, g_hg_gn, g_norm_g, g_conv_w = small_g

    grads = (g_gm_w_in, g_ln_g, g_ln_b, g_ws, g_bs, g_gm_w_out, g_hg_w_in, g_hg_lb, g_hg_gn, g_hg_w_out,
             g_ffn_w_up, g_conv_w, g_cb, g_ffn_w_down, g_norm_g, g_ada_w, g_ada_b, g_final)
    deltas = (d_gm_w_in, d_ln_g, d_ln_b, d_ws, d_bs, d_gm_w_out, d_hg_w_in, d_hg_lb, d_hg_gn, d_hg_w_out,
              d_ffn_w_up, d_conv_w, d_cb, d_ffn_w_down, d_norm_g_, d_ada_w, d_ada_b, d_final)
    new_m = (nm_gm_w_in, nm_ln_g, nm_ln_b, nm_ws, nm_bs, nm_gm_w_out, nm_hg_w_in, nm_hg_lb, nm_hg_gn, nm_hg_w_out,
             nm_ffn_w_up, nm_conv_w, nm_cb, nm_ffn_w_down, nm_norm_g, nm_ada_w, nm_ada_b, nm_final)
    new_v = (nv_gm_w_in, nv_ln_g, nv_ln_b, nv_ws, nv_bs, nv_gm_w_out, nv_hg_w_in, nv_hg_lb, nv_hg_gn, nv_hg_w_out,
             nv_ffn_w_up, nv_conv_w, nv_cb, nv_ffn_w_down, nv_norm_g, nv_ada_w, nv_ada_b, nv_final)
    return (loss, grad_x) + grads + deltas + new_m + new_v
```

```python
import functools
import math

import jax
import jax.numpy as jnp
from jax import lax
from jax.experimental import pallas as pl
from jax.experimental.pallas import tpu as pltpu

f32 = jnp.float32
bf16 = jnp.bfloat16
MESH = pl.DeviceIdType.MESH

N_DEV = 8
D = 1024
DEPTH = 4
EPS = 1e-6
GM_WIDTH = 2048
GM_HEADS = 8
GM_HEAD_DIM = 256
GM_BLOCK = 128
CHUNK = 64
HG_HEADS = 8
HG_DIM = 128
FFN_HIDDEN = 2816
ADA_COLS = 6 * D // N_DEV

HG_SUB_FWD = 64
HG_SUB_BWD = 32
HG_PAIR = 8
HG_TOKENS = 128

ADAM_LR = 0.001
ADAM_B1 = 0.9
ADAM_B2 = 0.999
ADAM_EPS = 1e-08
ADAM_WD = 0.01
ADAM_STEP = 10

V7X_VMEM_LIMIT = 56 * 1024 * 1024
LANES = 128


def _cparams(*sem):
    return pltpu.CompilerParams(dimension_semantics=sem or None, vmem_limit_bytes=V7X_VMEM_LIMIT)


def _tile(n, target, mult=LANES):
    best = None
    for t in range(mult, min(n, target) + 1, mult):
        if n % t == 0:
            best = t
    return best or n


WEIGHT_BLOCK_BYTES = 6 * 1024 * 1024


def _weight_tile(n, k):
    return _tile(n, max(LANES, WEIGHT_BLOCK_BYTES // (2 * k)))


def _gelu(x):
    return 0.5 * x * (1.0 + lax.erf(x * (1.0 / math.sqrt(2.0))))


def _mesh_pos():
    return lax.axis_index("x"), lax.axis_index("y"), lax.axis_index("c")


def _flat(pos):
    return 4 * pos[0] + 2 * pos[1] + pos[2]


def _peer(pos, k):
    return ((1 - pos[0]) if k & 4 else pos[0], (1 - pos[1]) if k & 2 else pos[1], (1 - pos[2]) if k & 1 else pos[2])


def _exchange_copies(ins, outs, send_sems, recv_sems, local_sems, gather):
    pos = _mesh_pos()
    me = _flat(pos)

    def src(i, dest):
        if gather:
            return ins[i]
        ref, rows = ins[i] if isinstance(ins[i], tuple) else (ins[i], None)
        return ref.at[dest] if rows is None else ref.at[dest, pl.ds(*rows)]

    local = [pltpu.make_async_copy(src(i, me), outs[i].at[me], local_sems.at[i]) for i in range(len(ins))]
    sends, recvs = [], []
    for k in range(1, N_DEV):
        peer = _peer(pos, k)
        there = _flat(peer)
        for i in range(len(ins)):
            sems = dict(send_sem=send_sems.at[i * 7 + k - 1], recv_sem=recv_sems.at[i * 7 + k - 1],
                        device_id=peer, device_id_type=MESH)
            sends.append(pltpu.make_async_remote_copy(src_ref=src(i, there), dst_ref=outs[i].at[me], **sems))
            recvs.append(pltpu.make_async_remote_copy(src_ref=src(i, there), dst_ref=outs[i].at[there], **sems))
    return local, sends, recvs


def _exchange_start(*refs):
    local, sends, _ = _exchange_copies(*refs)
    for cp in local + sends:
        cp.start()


def _exchange_wait(*refs):
    local, sends, recvs = _exchange_copies(*refs)
    for cp in recvs:
        cp.wait_recv()
    for cp in sends:
        cp.wait_send()
    for cp in local:
        cp.wait()


OTHER_CHIPS = (2, 4, 6)


def _relay_copies(ins, outs, send_sems, recv_sems, local_sems):
    pos = _mesh_pos()
    me = _flat(pos)
    sibling = _peer(pos, 1)
    local = [pltpu.make_async_copy(ins[i], outs[i].at[me], local_sems.at[i]) for i in range(len(ins))]
    first, passes, recvs = [], {k: [] for k in OTHER_CHIPS}, {k: [] for k in range(1, N_DEV)}
    for i in range(len(ins)):
        def copy(k, src, block, to):
            return pltpu.make_async_remote_copy(
                src_ref=src, dst_ref=outs[i].at[block], send_sem=send_sems.at[i * 7 + k - 1],
                recv_sem=recv_sems.at[i * 7 + k - 1], device_id=to, device_id_type=MESH)

        for k in (1,) + OTHER_CHIPS:
            first.append(copy(k, ins[i], me, _peer(pos, k)))
        for k in OTHER_CHIPS:
            there = _flat(_peer(pos, k))
            passes[k].append(copy(k ^ 1, outs[i].at[there], there, sibling))
        for k in range(1, N_DEV):
            there = _flat(_peer(pos, k))
            recvs[k].append(copy(k, ins[i], there, _peer(pos, k)))
    return local, first, passes, recvs


def _relay_start(ins, outs, *sems):
    local, first, _, _ = _relay_copies(ins, outs, *sems)
    for cp in local + first:
        cp.start()


def _relay_wait(ins, outs, *sems):
    local, first, passes, recvs = _relay_copies(ins, outs, *sems)
    for k in OTHER_CHIPS:
        for cp in recvs[k]:
            cp.wait_recv()
        for cp in passes[k]:
            cp.start()
    for k in (1, 3, 5, 7):
        for cp in recvs[k]:
            cp.wait_recv()
    for cp in first + [cp for k in OTHER_CHIPS for cp in passes[k]]:
        cp.wait_send()
    for cp in local:
        cp.wait()


def _exchange_out_shape(a, gather):
    return jax.ShapeDtypeStruct((N_DEV,) + tuple(a.shape) if gather else tuple(a.shape), a.dtype)


def _exchange_sems(n):
    return [pltpu.SemaphoreType.DMA((7 * n,)), pltpu.SemaphoreType.DMA((7 * n,)), pltpu.SemaphoreType.DMA((n,))]


ANY = pl.BlockSpec(memory_space=pl.ANY)


def _exchange(arrs, gather, name, relay=False):
    n = len(arrs)

    def body(*refs):
        ins, outs = refs[:n], refs[n:2 * n]
        if relay:
            _relay_start(ins, outs, *refs[2 * n:])
            _relay_wait(ins, outs, *refs[2 * n:])
        else:
            _exchange_start(ins, outs, *refs[2 * n:], gather)
            _exchange_wait(ins, outs, *refs[2 * n:], gather)

    return pl.pallas_call(
        body, name=name, out_shape=tuple(_exchange_out_shape(a, gather) for a in arrs),
        in_specs=[ANY] * n, out_specs=tuple([ANY] * n), scratch_shapes=_exchange_sems(n),
    )(*arrs)


def all_gather(arrs, name, relay=False):
    return _exchange(arrs, True, name, relay)


def all_to_all(arrs, name):
    return _exchange(arrs, False, name)


class Riders:
    def __init__(self, arrs, gather):
        self.gather = gather
        self.rows = [a[1] if isinstance(a, tuple) else None for a in arrs]
        self.arrs = [a[0] if isinstance(a, tuple) else a for a in arrs]

    def out_shapes(self):
        shapes = []
        for a, rows in zip(self.arrs, self.rows):
            shape = tuple(a.shape) if rows is None else (a.shape[0], rows[1], a.shape[2])
            shapes.append(jax.ShapeDtypeStruct((N_DEV,) + shape if self.gather else shape, a.dtype))
        return shapes


def _call(body, *, name, grid, in_specs, out_specs, out_shape, semantics, scratch_shapes=(), riders=None):
    if riders is None or not riders.arrs:
        return pl.pallas_call(body, name=name, grid=grid, in_specs=in_specs, out_specs=tuple(out_specs),
                              out_shape=tuple(out_shape), scratch_shapes=list(scratch_shapes),
                              compiler_params=_cparams(*semantics))
    n_in, n_out, n_scr, n_r = len(in_specs), len(out_specs), len(scratch_shapes), len(riders.arrs)
    gather = riders.gather

    def hosted(*refs):
        ins, r_ins = refs[:n_in], refs[n_in:n_in + n_r]
        at = n_in + n_r
        outs, r_outs = refs[at:at + n_out], refs[at + n_out:at + n_out + n_r]
        at += n_out + n_r
        scratch, sems = refs[at:at + n_scr], refs[at + n_scr:]
        first = functools.reduce(jnp.logical_and, [pl.program_id(a) == 0 for a in range(len(grid))])
        last = functools.reduce(jnp.logical_and, [pl.program_id(a) == grid[a] - 1 for a in range(len(grid))])

        r_ins = [(ref, rows) if rows is not None else ref for ref, rows in zip(r_ins, riders.rows)]

        @pl.when(first)
        def _():
            if gather:
                _relay_start(r_ins, r_outs, *sems)
            else:
                _exchange_start(r_ins, r_outs, *sems, gather)

        body(*ins, *outs, *scratch)

        @pl.when(last)
        def _():
            if gather:
                _relay_wait(r_ins, r_outs, *sems)
            else:
                _exchange_wait(r_ins, r_outs, *sems, gather)

    call = pl.pallas_call(
        hosted, name=name, grid=grid, in_specs=list(in_specs) + [ANY] * n_r, out_specs=tuple(out_specs) + (ANY,) * n_r,
        out_shape=tuple(out_shape) + tuple(riders.out_shapes()),
        scratch_shapes=list(scratch_shapes) + _exchange_sems(n_r),
        compiler_params=_cparams(*(("arbitrary",) * len(grid))))
    return lambda *args: call(*args, *riders.arrs)


def _shards_per_step(shape):
    _, k, n = shape
    best = None
    for q in (1, 2, 4, 8):
        if (q * n) % LANES == 0 and (best is None or 2 * k * q * n <= WEIGHT_BLOCK_BYTES):
            best = q
    return best


def mm_nn(a, b, out_dtype, name, tm=512, riders=None, transposed=False):
    m, k = a.shape
    tm = _tile(m, tm, 8)
    if b.ndim == 3:
        shard = b.shape[1] if transposed else b.shape[2]
        n = N_DEV * shard
        per_step = _shards_per_step((N_DEV, k, shard))
        tn = per_step * shard
        b_spec = pl.BlockSpec((per_step,) + b.shape[1:], lambda i, j: (j, 0, 0))
        contract = (((1,), (1,)), ((), ())) if transposed else (((1,), (0,)), ((), ()))

        def body(a_ref, b_ref, o_ref):
            for q in range(per_step):
                o_ref[:, q * shard:(q + 1) * shard] = lax.dot_general(
                    a_ref[...], b_ref[q], contract, preferred_element_type=f32).astype(o_ref.dtype)
    else:
        n = b.shape[1]
        tn = _weight_tile(n, k)
        b_spec = pl.BlockSpec((k, tn), lambda i, j: (0, j))

        def body(a_ref, b_ref, o_ref):
            o_ref[...] = jnp.dot(a_ref[...], b_ref[...], preferred_element_type=f32).astype(o_ref.dtype)

    return _call(
        body, name=name, grid=(m // tm, n // tn),
        in_specs=[pl.BlockSpec((tm, k), lambda i, j: (i, 0)), b_spec],
        out_specs=[pl.BlockSpec((tm, tn), lambda i, j: (i, j))],
        out_shape=[jax.ShapeDtypeStruct((m, n), out_dtype)], semantics=("parallel", "parallel"), riders=riders,
    )(a, b)


def mm_nn_residual(a, b, x, gate, norm, name, tm=512, riders=None):
    m, k = a.shape
    n = b.shape[1]
    tm = _tile(m, tm, 8)

    def body(a_ref, b_ref, x_ref, g_ref, *rest):
        if norm is not None:
            gn_ref, sc_ref, sh_ref, y_ref, o_ref, h_ref = rest
        else:
            y_ref, o_ref = rest
        y = jnp.dot(a_ref[...], b_ref[...], preferred_element_type=f32)
        y_ref[...] = y.astype(bf16)
        x_new = x_ref[...] + g_ref[...] * y
        o_ref[...] = x_new
        if norm is not None:
            h_ref[...] = _norm_fn(x_new, gn_ref[...], sc_ref[...], sh_ref[...]).astype(bf16)

    blk = pl.BlockSpec((tm, n), lambda i: (i, 0))
    in_specs = [pl.BlockSpec((tm, k), lambda i: (i, 0)), pl.BlockSpec((k, n), lambda i: (0, 0)), blk, _row(n)]
    out_specs = [blk, blk]
    out_shape = [jax.ShapeDtypeStruct((m, n), bf16), jax.ShapeDtypeStruct((m, n), f32)]
    args = [a, b, x, gate]
    if norm is not None:
        in_specs += [_row(n)] * 3
        out_specs += [blk]
        out_shape += [jax.ShapeDtypeStruct((m, n), bf16)]
        args += list(norm)
    return _call(body, name=name, grid=(m // tm,), in_specs=in_specs, out_specs=out_specs, out_shape=out_shape,
                 semantics=("parallel",), riders=riders)(*args)


def mm_nt(a, b, out_dtype, name, tm=512, riders=None, transposed=False):
    m = a.shape[0]
    tm = _tile(m, tm, 8)
    if b.ndim == 3:
        shard, k = (b.shape[1], b.shape[2]) if transposed else (b.shape[2], b.shape[1])
        tk = k
        b_spec = pl.BlockSpec(b.shape, lambda i, j: (0, 0, 0))
        width = N_DEV * shard
        contract = (((1,), (0,)), ((), ())) if transposed else (((1,), (1,)), ((), ()))

        def body(a_ref, b_ref, o_ref):
            acc = None
            for q in range(N_DEV):
                part = lax.dot_general(a_ref[:, q * shard:(q + 1) * shard], b_ref[q], contract, preferred_element_type=f32)
                acc = part if acc is None else acc + part
            o_ref[...] = acc.astype(o_ref.dtype)
    else:
        k, width = b.shape
        tk = _weight_tile(k, width)
        b_spec = pl.BlockSpec((tk, width), lambda i, j: (j, 0))

        def body(a_ref, b_ref, o_ref):
            o_ref[...] = lax.dot_general(a_ref[...], b_ref[...], (((1,), (1,)), ((), ())),
                                         preferred_element_type=f32).astype(o_ref.dtype)

    return _call(
        body, name=name, grid=(m // tm, k // tk),
        in_specs=[pl.BlockSpec((tm, width), lambda i, j: (i, 0)), b_spec],
        out_specs=[pl.BlockSpec((tm, tk), lambda i, j: (i, j))],
        out_shape=[jax.ShapeDtypeStruct((m, k), out_dtype)], semantics=("parallel", "parallel"), riders=riders,
    )(a, b)


def mm_tn(a, b, out_dtype, name, tm=512, tn=512, riders=None):
    t, m = a.shape
    n = b.shape[1]
    tm, tn = _tile(m, tm), _tile(n, tn)

    def body(a_ref, b_ref, o_ref):
        o_ref[...] = lax.dot_general(a_ref[...], b_ref[...], (((0,), (0,)), ((), ())),
                                     preferred_element_type=f32).astype(o_ref.dtype)

    return _call(
        body, name=name, grid=(m // tm, n // tn),
        in_specs=[pl.BlockSpec((t, tm), lambda i, j: (0, i)), pl.BlockSpec((t, tn), lambda i, j: (0, j))],
        out_specs=[pl.BlockSpec((tm, tn), lambda i, j: (i, j))],
        out_shape=[jax.ShapeDtypeStruct((m, n), out_dtype)], semantics=("parallel", "parallel"), riders=riders,
    )(a, b)


def mm_tn_by_owner(a, b, name, tm=512, riders=None):
    t, m = a.shape
    n = b.shape[1]
    shard = n // N_DEV
    per_step = 1 if shard % LANES == 0 else 2
    assert (per_step * shard) % LANES == 0
    tm = _tile(m, tm)

    def body(a_ref, b_ref, o_ref):
        acc = lax.dot_general(a_ref[...], b_ref[...], (((0,), (0,)), ((), ())), preferred_element_type=f32)
        for q in range(per_step):
            o_ref[q] = acc[:, q * shard:(q + 1) * shard].astype(bf16)

    return _call(
        body, name=name, grid=(m // tm, N_DEV // per_step),
        in_specs=[pl.BlockSpec((t, tm), lambda i, j: (0, i)), pl.BlockSpec((t, per_step * shard), lambda i, j: (0, j))],
        out_specs=[pl.BlockSpec((per_step, tm, shard), lambda i, j: (j, i, 0))],
        out_shape=[jax.ShapeDtypeStruct((N_DEV, m, shard), bf16)], semantics=("parallel", "parallel"), riders=riders,
    )(a, b)


def _norm_fn(x, gn, sc, sh):
    r = lax.rsqrt(jnp.mean(x * x, axis=-1, keepdims=True) + EPS)
    return (x * r * gn) * (1.0 + sc) + sh


def _row(d):
    return pl.BlockSpec((1, d), lambda i: (0, 0))


def norm_fwd(x, gn, sc, sh, name, tm=512):
    t, d = x.shape
    tm = _tile(t, tm, 8)

    def body(x_ref, gn_ref, sc_ref, sh_ref, h_ref):
        h_ref[...] = _norm_fn(x_ref[...], gn_ref[...], sc_ref[...], sh_ref[...]).astype(bf16)

    return pl.pallas_call(
        body, name=name, grid=(t // tm,),
        in_specs=[pl.BlockSpec((tm, d), lambda i: (i, 0)), _row(d), _row(d), _row(d)],
        out_specs=pl.BlockSpec((tm, d), lambda i: (i, 0)),
        out_shape=jax.ShapeDtypeStruct((t, d), bf16), compiler_params=_cparams("parallel"),
    )(x, gn, sc, sh)


def _gate_bwd(dx, y_ref, g_ref, dgate_ref, dy_ref):
    dgate_ref[...] += jnp.sum(dx * y_ref[...].astype(f32), axis=0, keepdims=True)
    dy_ref[...] = (dx * g_ref[...]).astype(bf16)


def norm_bwd(x, gn, sc, sh, dh, dres, name, gate=None, tm=512, riders=None):
    t, d = x.shape
    tm = _tile(t, tm, 8)

    def body(x_ref, gn_ref, sc_ref, sh_ref, dh_ref, dres_ref, *rest):
        if gate is not None:
            y_ref, g_ref, dx_ref, dgn_ref, dsc_ref, dsh_ref, dgate_ref, dy_ref = rest
        else:
            dx_ref, dgn_ref, dsc_ref, dsh_ref = rest

        @pl.when(pl.program_id(0) == 0)
        def _():
            dgn_ref[...] = jnp.zeros_like(dgn_ref)
            dsc_ref[...] = jnp.zeros_like(dsc_ref)
            dsh_ref[...] = jnp.zeros_like(dsh_ref)
            if gate is not None:
                dgate_ref[...] = jnp.zeros_like(dgate_ref)

        _, vjp = jax.vjp(_norm_fn, x_ref[...], gn_ref[...], sc_ref[...], sh_ref[...])
        dx, dgn, dsc, dsh = vjp(dh_ref[...].astype(f32))
        dx = dx + dres_ref[...]
        dx_ref[...] = dx
        dgn_ref[...] += dgn
        dsc_ref[...] += dsc
        dsh_ref[...] += dsh
        if gate is not None:
            _gate_bwd(dx, y_ref, g_ref, dgate_ref, dy_ref)

    blk = pl.BlockSpec((tm, d), lambda i: (i, 0))
    vec = jax.ShapeDtypeStruct((1, d), f32)
    in_specs = [blk, _row(d), _row(d), _row(d), blk, blk]
    out_specs = [blk, _row(d), _row(d), _row(d)]
    out_shape = [jax.ShapeDtypeStruct((t, d), f32), vec, vec, vec]
    args = [x, gn, sc, sh, dh, dres]
    if gate is not None:
        in_specs += [blk, _row(d)]
        out_specs += [_row(d), blk]
        out_shape += [vec, jax.ShapeDtypeStruct((t, d), bf16)]
        args += list(gate)
    return _call(body, name=name, grid=(t // tm,), in_specs=in_specs, out_specs=out_specs, out_shape=out_shape,
                 semantics=("arbitrary",), riders=riders)(*args)


def _loss_fn(x, g, tgt):
    r = lax.rsqrt(jnp.mean(x * x, axis=-1, keepdims=True) + EPS)
    err = jnp.square(x * r * g - tgt)
    return 0.5 * jnp.sum(jnp.mean(err, axis=-1, keepdims=True), axis=0, keepdims=True)


def loss_head(x, g, tgt, y, gate, name, tm=512):
    t, d = x.shape
    tm = _tile(t, tm, 8)

    def body(x_ref, g_ref, t_ref, y_ref, gate_ref, loss_ref, dx_ref, dg_ref, dgate_ref, dy_ref):
        @pl.when(pl.program_id(0) == 0)
        def _():
            loss_ref[...] = jnp.zeros_like(loss_ref)
            dg_ref[...] = jnp.zeros_like(dg_ref)
            dgate_ref[...] = jnp.zeros_like(dgate_ref)

        loss, vjp = jax.vjp(_loss_fn, x_ref[...], g_ref[...], t_ref[...])
        dx, dg, _ = vjp(jnp.ones((1, 1), f32))
        dx_ref[...] = dx
        loss_ref[...] += loss
        dg_ref[...] += dg
        _gate_bwd(dx, y_ref, gate_ref, dgate_ref, dy_ref)

    blk = pl.BlockSpec((tm, d), lambda i: (i, 0))
    vec = jax.ShapeDtypeStruct((1, d), f32)
    return pl.pallas_call(
        body, name=name, grid=(t // tm,),
        in_specs=[blk, _row(d), blk, blk, _row(d)],
        out_specs=(pl.BlockSpec((1, 1), lambda i: (0, 0)), blk, _row(d), _row(d), blk),
        out_shape=(jax.ShapeDtypeStruct((1, 1), f32), jax.ShapeDtypeStruct((t, d), f32), vec, vec,
                   jax.ShapeDtypeStruct((t, d), bf16)),
        compiler_params=_cparams("arbitrary"),
    )(x, g, tgt, y, gate)


def _gm_block_fn(z, ws, bs, lng, lnb):
    u = _gelu(z[:, :GM_WIDTH])
    vg = _gelu(z[:, GM_WIDTH:])
    mu = jnp.mean(vg, axis=-1, keepdims=True)
    var = jnp.mean(jnp.square(vg - mu), axis=-1, keepdims=True)
    vn = (vg - mu) * lax.rsqrt(var + EPS) * lng + lnb
    row = lax.broadcasted_iota(jnp.int32, (GM_BLOCK, GM_BLOCK), 0) // CHUNK
    col = lax.broadcasted_iota(jnp.int32, (GM_BLOCK, GM_BLOCK), 1) // CHUNK
    parts = []
    for h in range(GM_HEADS):
        w = jnp.where(row >= col, ws[h], 0.0)
        cols = slice(h * GM_HEAD_DIM, (h + 1) * GM_HEAD_DIM)
        s = jnp.dot(w.astype(bf16), vn[:, cols].astype(bf16), preferred_element_type=f32) + bs[h]
        parts.append(u[:, cols] * s)
    return jnp.concatenate(parts, axis=1)


def _gm_param_specs():
    return [pl.BlockSpec((GM_HEADS, GM_BLOCK, GM_BLOCK), lambda i: (0, 0, 0)),
            pl.BlockSpec((GM_HEADS, GM_BLOCK, 1), lambda i: (0, 0, 0)), _row(GM_WIDTH), _row(GM_WIDTH)]


def gm_mix_fwd(z, ws, bs, lng, lnb, name, riders=None):
    t = z.shape[0]

    def body(z_ref, ws_ref, bs_ref, lng_ref, lnb_ref, o_ref):
        o_ref[...] = _gm_block_fn(z_ref[...].astype(f32), ws_ref[...], bs_ref[...], lng_ref[...],
                                  lnb_ref[...]).astype(bf16)

    return _call(
        body, name=name, grid=(t // GM_BLOCK,),
        in_specs=[pl.BlockSpec((GM_BLOCK, 2 * GM_WIDTH), lambda i: (i, 0))] + _gm_param_specs(),
        out_specs=[pl.BlockSpec((GM_BLOCK, GM_WIDTH), lambda i: (i, 0))],
        out_shape=[jax.ShapeDtypeStruct((t, GM_WIDTH), bf16)], semantics=("parallel",), riders=riders,
    )(z, ws, bs, lng, lnb)


def gm_mix_bwd(z, ws, bs, lng, lnb, dgated, name, riders=None):
    t = z.shape[0]

    def body(z_ref, ws_ref, bs_ref, lng_ref, lnb_ref, dg_ref, dz_ref, dws_ref, dbs_ref, dlng_ref, dlnb_ref):
        _, vjp = jax.vjp(_gm_block_fn, z_ref[...].astype(f32), ws_ref[...], bs_ref[...], lng_ref[...], lnb_ref[...])
        dz, dws, dbs, dlng, dlnb = vjp(dg_ref[...].astype(f32))
        dz_ref[...] = dz.astype(bf16)

        @pl.when(pl.program_id(0) == 0)
        def _():
            dws_ref[...] = jnp.zeros_like(dws_ref)
            dbs_ref[...] = jnp.zeros_like(dbs_ref)
            dlng_ref[...] = jnp.zeros_like(dlng_ref)
            dlnb_ref[...] = jnp.zeros_like(dlnb_ref)

        dws_ref[...] += dws
        dbs_ref[...] += dbs
        dlng_ref[...] += dlng
        dlnb_ref[...] += dlnb

    zblk = pl.BlockSpec((GM_BLOCK, 2 * GM_WIDTH), lambda i: (i, 0))
    return _call(
        body, name=name, grid=(t // GM_BLOCK,),
        in_specs=[zblk] + _gm_param_specs() + [pl.BlockSpec((GM_BLOCK, GM_WIDTH), lambda i: (i, 0))],
        out_specs=[zblk] + _gm_param_specs(),
        out_shape=[jax.ShapeDtypeStruct((t, 2 * GM_WIDTH), bf16),
                   jax.ShapeDtypeStruct((GM_HEADS, GM_BLOCK, GM_BLOCK), f32),
                   jax.ShapeDtypeStruct((GM_HEADS, GM_BLOCK, 1), f32),
                   jax.ShapeDtypeStruct((1, GM_WIDTH), f32), jax.ShapeDtypeStruct((1, GM_WIDTH), f32)],
        semantics=("arbitrary",), riders=riders,
    )(z, ws, bs, lng, lnb, dgated)


@functools.partial(jax.custom_vjp, nondiff_argnums=(1,))
def _rows_up(x, shift):
    return x if shift == 0 else pltpu.roll(x, x.shape[1] - shift, axis=1)


def _rows_up_fwd(x, shift):
    return _rows_up(x, shift), None


def _rows_up_bwd(shift, _, g):
    return (g if shift == 0 else pltpu.roll(g, shift, axis=1),)


_rows_up.defvjp(_rows_up_fwd, _rows_up_bwd)


def _hg_block_fn(sub, qp, fz, iv, gp, s0, lb, gn):
    n, ns, d = sub, HG_TOKENS // sub, HG_DIM
    p, nb, per_sub = HG_PAIR, HG_TOKENS // HG_PAIR, sub // HG_PAIR
    f = lb + (1.0 - lb) * jax.nn.sigmoid(fz)
    g = jnp.log(f)
    k = 1.0 - f
    q = qp * jax.nn.sigmoid(qp)
    v = iv.astype(bf16)
    row = lax.broadcasted_iota(jnp.int32, (HG_TOKENS, HG_TOKENS), 0)
    col = lax.broadcasted_iota(jnp.int32, (HG_TOKENS, HG_TOKENS), 1)
    same_sub = col // n == row // n
    tri = ((col <= row) & same_sub).astype(f32)
    cum = jnp.dot(tri, g, precision=lax.Precision.HIGHEST, preferred_element_type=f32)
    cum_b, q_b, k_b, f_b = cum.reshape(nb, p, d), q.reshape(nb, p, d), k.reshape(nb, p, d), f.reshape(nb, p, d)
    j_b = lax.broadcasted_iota(jnp.int32, (nb, p, d), 1)
    j_col = lax.broadcasted_iota(jnp.int32, (nb, p, 1), 1)
    scores_t = jnp.zeros((HG_TOKENS, HG_TOKENS), f32)
    weight = k_b
    for delta in range(p):
        if delta:
            weight = weight * _rows_up(f_b, delta)
        pair = jnp.sum(_rows_up(q_b, delta) * weight, axis=2, keepdims=True)
        pair = jnp.where(j_col < p - delta, pair, 0.0)
        scores_t = scores_t + jnp.where(col == row + delta, pair.reshape(HG_TOKENS, 1), 0.0)
    o = lax.dot_general(scores_t.astype(bf16), v, (((0,), (0,)), ((), ())), preferred_element_type=f32)
    last = cum_b[:, p - 1:p, :]
    before = jnp.concatenate([jnp.zeros((1, 1, d), f32), last[:-1]], axis=0)
    before = jnp.broadcast_to(before, (nb, p, d)).reshape(HG_TOKENS, d)
    block = (lax.broadcasted_iota(jnp.int32, (HG_TOKENS, d), 0) // p) % per_sub
    q_late = q * jnp.exp(jnp.where(block > 0, cum - before, -1e30))
    last_s = last.reshape(ns, per_sub, d)
    q_parts, k_parts = [], []
    for m in range(1, per_sub):
        split = jnp.broadcast_to(last_s[:, m - 1:m, :], (ns, n, d)).reshape(HG_TOKENS, d)
        k_parts.append(k * jnp.exp(jnp.where(block < m, split - cum, -1e30)))
        q_parts.append(jnp.where(block == m, q_late, 0.0))
    scores = lax.dot_general(jnp.concatenate(q_parts, axis=1).astype(bf16), jnp.concatenate(k_parts, axis=1).astype(bf16),
                             (((1,), (1,)), ((), ())), preferred_element_type=f32)
    o = o + jnp.dot(jnp.where(same_sub, scores, 0.0).astype(bf16), v, preferred_element_type=f32)
    cum_s = cum.reshape(ns, n, d)
    tot = cum_s[:, n - 1:n, :]
    kt_t = (k.reshape(ns, n, d) * jnp.exp(tot - cum_s)).reshape(HG_TOKENS, d).T
    lane_sub = lax.broadcasted_iota(jnp.int32, (d, HG_TOKENS), 1) // n
    k_by_sub = jnp.concatenate([jnp.where(lane_sub == b, kt_t, 0.0) for b in range(ns)], axis=0).astype(bf16)
    update = jnp.dot(k_by_sub, v, preferred_element_type=f32)
    decay = jnp.exp(tot.reshape(ns, d)).T
    state = s0
    states = []
    for a in range(ns):
        states.append(state.astype(bf16))
        state = decay[:, a:a + 1] * state + update[a * d:(a + 1) * d]
    qt = q * jnp.exp(cum)
    row_sub = lax.broadcasted_iota(jnp.int32, (HG_TOKENS, d), 0) // n
    q_by_sub = jnp.concatenate([jnp.where(row_sub == a, qt, 0.0) for a in range(ns)], axis=1).astype(bf16)
    o = o + jnp.dot(q_by_sub, jnp.concatenate(states, axis=0), preferred_element_type=f32)
    on = o * lax.rsqrt(jnp.mean(o * o, axis=-1, keepdims=True) + EPS) * gn
    return on * (gp * jax.nn.sigmoid(gp)), state


def _head_parts(ref, h):
    return [ref[:, p * D + h * HG_DIM:p * D + (h + 1) * HG_DIM] for p in range(4)]


def hg_scan_fwd(proj, lb, gn, name, riders=None):
    t = proj.shape[0]
    nt = t // HG_TOKENS

    def body(p_ref, lb_ref, gn_ref, y_ref, s_ref, state):
        @pl.when(pl.program_id(0) == 0)
        def _():
            state[...] = jnp.zeros_like(state)

        for h in range(HG_HEADS):
            cols = slice(h * HG_DIM, (h + 1) * HG_DIM)
            s_ref[h, 0] = state[h]
            y, s1 = _hg_block_fn(HG_SUB_FWD, *_head_parts(p_ref, h), state[h], lb_ref[:, cols], gn_ref[:, cols])
            y_ref[:, cols] = y.astype(bf16)
            state[h] = s1

    return _call(
        body, name=name, grid=(nt,),
        in_specs=[pl.BlockSpec((HG_TOKENS, 4 * D), lambda i: (i, 0)), _row(D), _row(D)],
        out_specs=[pl.BlockSpec((HG_TOKENS, D), lambda i: (i, 0)),
                   pl.BlockSpec((HG_HEADS, 1, HG_DIM, HG_DIM), lambda i: (0, i, 0, 0))],
        out_shape=[jax.ShapeDtypeStruct((t, D), bf16), jax.ShapeDtypeStruct((HG_HEADS, nt, HG_DIM, HG_DIM), f32)],
        scratch_shapes=[pltpu.VMEM((HG_HEADS, HG_DIM, HG_DIM), f32)],
        semantics=("arbitrary",), riders=riders,
    )(proj, lb, gn)


def hg_scan_bwd(proj, lb, gn, states, dy, name, riders=None):
    t = proj.shape[0]
    nt = t // HG_TOKENS

    def body(p_ref, lb_ref, gn_ref, s_ref, dy_ref, dp_ref, dlb_ref, dgn_ref, dstate):
        @pl.when(pl.program_id(0) == 0)
        def _():
            dstate[...] = jnp.zeros_like(dstate)
            dlb_ref[...] = jnp.zeros_like(dlb_ref)
            dgn_ref[...] = jnp.zeros_like(dgn_ref)

        for h in range(HG_HEADS):
            cols = slice(h * HG_DIM, (h + 1) * HG_DIM)
            _, vjp = jax.vjp(functools.partial(_hg_block_fn, HG_SUB_BWD), *_head_parts(p_ref, h), s_ref[h, 0],
                             lb_ref[:, cols], gn_ref[:, cols])
            grads = vjp((dy_ref[:, cols].astype(f32), dstate[h]))
            for p in range(4):
                dp_ref[:, p * D + h * HG_DIM:p * D + (h + 1) * HG_DIM] = grads[p].astype(bf16)
            dstate[h] = grads[4]
            dlb_ref[:, cols] += grads[5]
            dgn_ref[:, cols] += grads[6]

    small = jax.ShapeDtypeStruct((1, D), f32)
    return _call(
        body, name=name, grid=(nt,),
        in_specs=[pl.BlockSpec((HG_TOKENS, 4 * D), lambda i: (nt - 1 - i, 0)), _row(D), _row(D),
                  pl.BlockSpec((HG_HEADS, 1, HG_DIM, HG_DIM), lambda i: (0, nt - 1 - i, 0, 0)),
                  pl.BlockSpec((HG_TOKENS, D), lambda i: (nt - 1 - i, 0))],
        out_specs=[pl.BlockSpec((HG_TOKENS, 4 * D), lambda i: (nt - 1 - i, 0)), _row(D), _row(D)],
        out_shape=[jax.ShapeDtypeStruct((t, 4 * D), bf16), small, small],
        scratch_shapes=[pltpu.VMEM((HG_HEADS, HG_DIM, HG_DIM), f32)],
        semantics=("arbitrary",), riders=riders,
    )(proj, lb, gn, states, dy)


FFN_COLS = 1408
HALO = 8
STRIP = 16


def _ffn_specs(tm):
    nb = tm // HALO
    main_g = pl.BlockSpec((tm, FFN_COLS), lambda j, i: (i, j))
    main_v = pl.BlockSpec((tm, FFN_COLS), lambda j, i: (i, j + 2))
    halo_g = pl.BlockSpec((HALO, FFN_COLS), lambda j, i: (jnp.maximum(i * nb - 1, 0), j))
    halo_v = pl.BlockSpec((HALO, FFN_COLS), lambda j, i: (jnp.maximum(i * nb - 1, 0), j + 2))
    w_g = pl.BlockSpec((3, FFN_COLS), lambda j, i: (0, j))
    w_v = pl.BlockSpec((3, FFN_COLS), lambda j, i: (0, j + 2))
    b_g = pl.BlockSpec((1, FFN_COLS), lambda j, i: (0, j))
    b_v = pl.BlockSpec((1, FFN_COLS), lambda j, i: (0, j + 2))
    return [main_g, halo_g, main_v, halo_v, w_g, w_v, b_g, b_v]


def _strip_rows(r):
    return pl.ds(r * STRIP, STRIP) if isinstance(r, int) else pl.ds(pl.multiple_of(r * STRIP, STRIP), STRIP)


def _for_strips(nstrip, strip, reverse=False):
    if reverse:
        strip(nstrip - 1, True)
        lax.fori_loop(0, nstrip - 1, lambda k, c: (strip(nstrip - 2 - k, False), c)[1], 0)
    else:
        strip(0, True)
        lax.fori_loop(1, nstrip, lambda r, c: (strip(r, False), c)[1], 0)


SUBLANES = 8


def _rows_down(prev, cur, shift):
    row = lax.broadcasted_iota(jnp.int32, (SUBLANES, LANES), 0)
    tiles = [prev[STRIP - SUBLANES:]] + [cur[q * SUBLANES:(q + 1) * SUBLANES] for q in range(STRIP // SUBLANES)]
    turned = [pltpu.roll(x, shift, axis=0) for x in tiles]
    return jnp.concatenate([jnp.where(row < shift, turned[q], turned[q + 1]) for q in range(STRIP // SUBLANES)], axis=0)


def _rows_ahead(cur, nxt, shift):
    row = lax.broadcasted_iota(jnp.int32, (SUBLANES, LANES), 0)
    tiles = [cur[q * SUBLANES:(q + 1) * SUBLANES] for q in range(STRIP // SUBLANES)] + [nxt[:SUBLANES]]
    turned = [pltpu.roll(x, SUBLANES - shift, axis=0) for x in tiles]
    return jnp.concatenate([jnp.where(row >= SUBLANES - shift, turned[q + 1], turned[q])
                            for q in range(STRIP // SUBLANES)], axis=0)


def _conv_strip(main_ref, halo_ref, w_ref, b_ref, r, edge, cols, first_block):
    cur = main_ref[_strip_rows(r), cols].astype(f32)
    if edge:
        h = jnp.where(first_block, 0.0, halo_ref[:, cols].astype(f32))
        prev = jnp.concatenate([jnp.zeros_like(h), h], axis=0)
    else:
        prev = main_ref[_strip_rows(r - 1), cols].astype(f32)
    a1, a2 = _rows_down(prev, cur, 1), _rows_down(prev, cur, 2)
    y = b_ref[:, cols] + w_ref[0:1, cols] * a2 + w_ref[1:2, cols] * a1 + w_ref[2:3, cols] * cur
    return y, (cur, a1, a2)


def ffn_gate_fwd(a, cw, cb, name, tm=512, riders=None):
    t = a.shape[0]
    tm = _tile(t, tm, STRIP)

    def body(ag_ref, hg_ref, av_ref, hv_ref, wg_ref, wv_ref, bg_ref, bv_ref, o_ref):
        first_block = pl.program_id(1) == 0

        def strip(r, edge):
            for c in range(FFN_COLS // LANES):
                cols = pl.ds(c * LANES, LANES)
                yg, _ = _conv_strip(ag_ref, hg_ref, wg_ref, bg_ref, r, edge, cols, first_block)
                yv, _ = _conv_strip(av_ref, hv_ref, wv_ref, bv_ref, r, edge, cols, first_block)
                o_ref[_strip_rows(r), cols] = (_gelu(yg) * yv).astype(bf16)

        _for_strips(tm // STRIP, strip)

    return _call(
        body, name=name, grid=(2, t // tm), in_specs=_ffn_specs(tm),
        out_specs=[pl.BlockSpec((tm, FFN_COLS), lambda j, i: (i, j))],
        out_shape=[jax.ShapeDtypeStruct((t, FFN_HIDDEN), bf16)],
        semantics=("parallel", "arbitrary"), riders=riders,
    )(a, a, a, a, cw, cw, cb, cb)


def _conv_tile(prev, cur, w_ref, b_ref, cols):
    row = lax.broadcasted_iota(jnp.int32, (SUBLANES, LANES), 0)
    a1 = jnp.where(row < 1, pltpu.roll(prev, 1, axis=0), pltpu.roll(cur, 1, axis=0))
    a2 = jnp.where(row < 2, pltpu.roll(prev, 2, axis=0), pltpu.roll(cur, 2, axis=0))
    return b_ref[:, cols] + w_ref[0:1, cols] * a2 + w_ref[1:2, cols] * a1 + w_ref[2:3, cols] * cur


def _gate_grads(yg, yv, dh):
    cdf = 0.5 * (1.0 + lax.erf(yg * (1.0 / math.sqrt(2.0))))
    pdf = jnp.exp(-0.5 * yg * yg) * (1.0 / math.sqrt(2.0 * math.pi))
    return dh * yv * (cdf + yg * pdf), dh * (yg * cdf)


def ffn_gate_bwd(a, cw, cb, dhid, name, tm=512, riders=None):
    t = a.shape[0]
    tm = _tile(t, tm, STRIP)
    nb = tm // HALO
    last_halo = t // HALO - 1
    nstrip = tm // STRIP
    fh = FFN_HIDDEN

    def body(a_ref, ha_ref, na_ref, w_ref, b_ref, dh_ref, ndh_ref, da_ref, dw_ref, db_ref, acc, dybuf):
        first_block = pl.program_id(0) == 0
        last_block = pl.program_id(0) == pl.num_programs(0) - 1

        @pl.when(first_block)
        def _():
            acc[...] = jnp.zeros_like(acc)

        def transposed_conv(dy, ahead, cols):
            return (w_ref[2:3, cols] * dy + w_ref[1:2, cols] * _rows_ahead(dy, ahead, 1)
                    + w_ref[0:1, cols] * _rows_ahead(dy, ahead, 2)).astype(bf16)

        def strip(r, edge):
            rows = _strip_rows(r)
            for c in range(fh // LANES):
                gate, val = pl.ds(c * LANES, LANES), pl.ds(fh + c * LANES, LANES)
                yg, taps_g = _conv_strip(a_ref, ha_ref, w_ref, b_ref, r, edge, gate, first_block)
                yv, taps_v = _conv_strip(a_ref, ha_ref, w_ref, b_ref, r, edge, val, first_block)
                dyg, dyv = _gate_grads(yg, yv, dh_ref[rows, gate].astype(f32))
                for p, (dy, (a0, a1, a2), cols) in enumerate(((dyg, taps_g, gate), (dyv, taps_v, val))):
                    acc[0, :, cols] += dy * a2
                    acc[1, :, cols] += dy * a1
                    acc[2, :, cols] += dy * a0
                    acc[3, :, cols] += dy
                    if not edge:
                        da_ref[_strip_rows(r - 1), cols] = transposed_conv(dybuf[:, cols], dy, cols)
                    dybuf[:, cols] = dy

        _for_strips(nstrip, strip)

        last_rows = pl.ds((nstrip - 1) * STRIP + SUBLANES, SUBLANES)
        for c in range(fh // LANES):
            gate, val = pl.ds(c * LANES, LANES), pl.ds(fh + c * LANES, LANES)
            yg = _conv_tile(a_ref[last_rows, gate].astype(f32), na_ref[:, gate].astype(f32), w_ref, b_ref, gate)
            yv = _conv_tile(a_ref[last_rows, val].astype(f32), na_ref[:, val].astype(f32), w_ref, b_ref, val)
            for dy, cols in zip(_gate_grads(yg, yv, ndh_ref[:, gate].astype(f32)), (gate, val)):
                dy = jnp.where(last_block, 0.0, dy)
                ahead = jnp.concatenate([dy, jnp.zeros_like(dy)], axis=0)
                da_ref[_strip_rows(nstrip - 1), cols] = transposed_conv(dybuf[:, cols], ahead, cols)

        @pl.when(last_block)
        def _():
            for tap in range(3):
                dw_ref[tap:tap + 1, :] = jnp.sum(acc[tap], axis=0, keepdims=True)
            db_ref[...] = jnp.sum(acc[3], axis=0, keepdims=True)

    def after(i):
        return jnp.minimum((i + 1) * nb, last_halo)

    return _call(
        body, name=name, grid=(t // tm,),
        in_specs=[pl.BlockSpec((tm, 2 * fh), lambda i: (i, 0)),
                  pl.BlockSpec((HALO, 2 * fh), lambda i: (jnp.maximum(i * nb - 1, 0), 0)),
                  pl.BlockSpec((HALO, 2 * fh), lambda i: (after(i), 0)),
                  pl.BlockSpec((3, 2 * fh), lambda i: (0, 0)), pl.BlockSpec((1, 2 * fh), lambda i: (0, 0)),
                  pl.BlockSpec((tm, fh), lambda i: (i, 0)), pl.BlockSpec((HALO, fh), lambda i: (after(i), 0))],
        out_specs=[pl.BlockSpec((tm, 2 * fh), lambda i: (i, 0)), pl.BlockSpec((3, 2 * fh), lambda i: (0, 0)),
                   pl.BlockSpec((1, 2 * fh), lambda i: (0, 0))],
        out_shape=[jax.ShapeDtypeStruct((t, 2 * fh), bf16), jax.ShapeDtypeStruct((3, 2 * fh), f32),
                   jax.ShapeDtypeStruct((1, 2 * fh), f32)],
        scratch_shapes=[pltpu.VMEM((4, STRIP, 2 * fh), f32), pltpu.VMEM((STRIP, 2 * fh), f32)],
        semantics=("arbitrary",), riders=riders,
    )(a, a, a, cw, cb, dhid, dhid)


def ada_mod(c_all, ada_w, ada_b_cols, name):
    cols = ada_w.shape[2]

    def body(c_ref, w_ref, b_ref, o_ref):
        c = c_ref[...]
        cond = (c * jax.nn.sigmoid(c)).astype(bf16)
        o_ref[0] = jnp.dot(cond, w_ref[0].astype(bf16), preferred_element_type=f32) + b_ref[0]

    return pl.pallas_call(
        body, name=name, grid=(DEPTH,),
        in_specs=[pl.BlockSpec((N_DEV, D), lambda i: (0, 0)), pl.BlockSpec((1, D, cols), lambda i: (i, 0, 0)),
                  pl.BlockSpec((1, 1, cols), lambda i: (i, 0, 0))],
        out_specs=pl.BlockSpec((1, N_DEV, cols), lambda i: (i, 0, 0)),
        out_shape=jax.ShapeDtypeStruct((DEPTH, N_DEV, cols), f32), compiler_params=_cparams("parallel"),
    )(c_all, ada_w, ada_b_cols)


def ada_grads(c_all, dmod_cols, dmod_all, name):
    cols = dmod_cols.shape[2]

    def body(c_ref, dm_ref, da_ref, dw_ref, db_ref):
        c = c_ref[...]
        cond = c * jax.nn.sigmoid(c)
        dw_ref[0] = lax.dot_general(cond, dm_ref[0], (((0,), (0,)), ((), ())), precision=lax.Precision.HIGHEST,
                                    preferred_element_type=f32)
        acc = da_ref[0, 0]
        for e in range(1, N_DEV):
            acc = acc + da_ref[e, 0]
        db_ref[0] = acc

    return pl.pallas_call(
        body, name=name, grid=(DEPTH,),
        in_specs=[pl.BlockSpec((N_DEV, D), lambda i: (0, 0)), pl.BlockSpec((1, N_DEV, cols), lambda i: (i, 0, 0)),
                  pl.BlockSpec((N_DEV, 1, 1, 6 * D), lambda i: (0, i, 0, 0))],
        out_specs=(pl.BlockSpec((1, D, cols), lambda i: (i, 0, 0)), pl.BlockSpec((1, 1, 6 * D), lambda i: (i, 0, 0))),
        out_shape=(jax.ShapeDtypeStruct((DEPTH, D, cols), f32), jax.ShapeDtypeStruct((DEPTH, 1, 6 * D), f32)),
        compiler_params=_cparams("parallel"),
    )(c_all, dmod_cols, dmod_all)


def lower_bound_fwd(hg_lb, name):
    n = hg_lb.shape[1]

    def body(l_ref, o_ref):
        o_ref[...] = jax.nn.sigmoid(l_ref[1:2, :] - l_ref[0:1, :])

    return pl.pallas_call(body, name=name, out_shape=jax.ShapeDtypeStruct((1, n), f32))(hg_lb)


def lower_bound_bwd(hg_lb, dlb, name):
    n = hg_lb.shape[1]

    def body(l_ref, d_ref, o_ref):
        p = jax.nn.sigmoid(l_ref[1:2, :] - l_ref[0:1, :])
        g = d_ref[...] * p * (1.0 - p)
        o_ref[0:1, :] = -g
        o_ref[1:2, :] = g

    return pl.pallas_call(body, name=name, out_shape=jax.ShapeDtypeStruct((2, n), f32))(hg_lb, dlb)


def _adamw(w, g, m, v):
    m = ADAM_B1 * m + (1.0 - ADAM_B1) * g
    v = ADAM_B2 * v + (1.0 - ADAM_B2) * jnp.square(g)
    m_hat = m / (1.0 - ADAM_B1 ** ADAM_STEP)
    v_hat = v / (1.0 - ADAM_B2 ** ADAM_STEP)
    delta = -ADAM_LR * (m_hat / (jnp.sqrt(v_hat) + ADAM_EPS) + ADAM_WD * w)
    return delta, m, v


ADAM_BLOCK_BYTES = 32 * 1024 * 1024


def adam_reduced(parts, w, m, v, name):
    layers, r, c = w.shape
    outs = None
    for layer in range(layers):
        outs = _adam_layer(parts[layer], w, m, v, layer, outs, f"{name}_{layer}")
    return outs


def _adam_layer(parts, w, m, v, layer, prev, name):
    layers, r, c = w.shape
    rows = parts[0].shape[1]
    assert all(p.shape == (N_DEV, rows, c) for p in parts) and rows * len(parts) == r
    row_bytes = 2 * (len(parts) * N_DEV * c * 2 + 7 * c * 4)
    tr = _tile(rows, max(16, ADAM_BLOCK_BYTES // row_bytes), 16)
    steps = rows // tr
    n_prev = 0 if prev is None else 4

    def body(*refs):
        p_refs = refs[:len(parts)]
        w_ref, m_ref, v_ref = refs[len(parts):len(parts) + 3]
        g_ref, d_ref, mo_ref, vo_ref = refs[len(parts) + 3 + n_prev:]
        for idx in range(len(parts)):
            @pl.when(pl.program_id(0) == idx)
            def _():
                g = p_refs[idx][0].astype(f32)
                for j in range(1, N_DEV):
                    g = g + p_refs[idx][j].astype(f32)
                g_ref[...] = g
                d_ref[...], mo_ref[...], vo_ref[...] = _adamw(w_ref[...], g, m_ref[...], v_ref[...])

    def part_spec(idx):
        return pl.BlockSpec((N_DEV, tr, c), lambda p, i: (0, jnp.where(p == idx, i, 0), 0))

    blk = pl.BlockSpec((None, tr, c), lambda p, i: (layer, p * steps + i, 0))
    out = jax.ShapeDtypeStruct((layers, r, c), f32)
    n_in = len(parts) + 3
    return pl.pallas_call(
        body, name=name, grid=(len(parts), steps),
        in_specs=[part_spec(idx) for idx in range(len(parts))] + [blk, blk, blk] + [ANY] * n_prev,
        out_specs=(blk, blk, blk, blk), out_shape=(out, out, out, out),
        input_output_aliases={n_in + k: k for k in range(n_prev)},
        compiler_params=_cparams("arbitrary", "arbitrary"),
    )(*parts, w, m, v, *(prev or ()))


def adam_plain(g, w, m, v, name, tr=256):
    r, c = w.shape
    tr = _tile(r, tr, 8)

    def body(g_ref, w_ref, m_ref, v_ref, d_ref, mo_ref, vo_ref):
        d_ref[...], mo_ref[...], vo_ref[...] = _adamw(w_ref[...], g_ref[...], m_ref[...], v_ref[...])

    blk = pl.BlockSpec((tr, c), lambda i: (i, 0))
    out = jax.ShapeDtypeStruct((r, c), f32)
    return pl.pallas_call(
        body, name=name, grid=(r // tr,), in_specs=[blk, blk, blk, blk], out_specs=(blk, blk, blk),
        out_shape=(out, out, out), compiler_params=_cparams("parallel"),
    )(g, w, m, v)


def sum_parts(parts, name):
    _, r, c = parts.shape

    def body(p_ref, o_ref):
        acc = p_ref[0]
        for j in range(1, N_DEV):
            acc = acc + p_ref[j]
        o_ref[...] = acc

    return pl.pallas_call(body, name=name, out_shape=jax.ShapeDtypeStruct((r, c), f32))(parts)


def _pack(arrs, rows_mult=8):
    flat = jnp.concatenate([a.reshape(-1) for a in arrs])
    rows = -(-flat.shape[0] // LANES)
    rows = -(-rows // rows_mult) * rows_mult
    return jnp.pad(flat, (0, rows * LANES - flat.shape[0])).reshape(rows, LANES)


def _unpack(flat, shapes):
    out, at = [], 0
    for s in shapes:
        n = math.prod(s)
        out.append(flat[at:at + n].reshape(s))
        at += n
    return out


def kernel(x, c, gm_w_in, gm_ln_g, gm_ln_b, gm_w_s, gm_b_s, gm_w_out, hg_w_in, hg_lb, hg_gn_g, hg_w_out, ffn_w_up, ffn_conv_w, ffn_conv_b, ffn_w_down, norm_g, ada_w, ada_b, final_g, loss_target, m_gm_w_in, m_gm_ln_g, m_gm_ln_b, m_gm_w_s, m_gm_b_s, m_gm_w_out, m_hg_w_in, m_hg_lb, m_hg_gn_g, m_hg_w_out, m_ffn_w_up, m_ffn_conv_w, m_ffn_conv_b, m_ffn_w_down, m_norm_g, m_ada_w, m_ada_b, m_final_g, v_gm_w_in, v_gm_ln_g, v_gm_ln_b, v_gm_w_s, v_gm_b_s, v_gm_w_out, v_hg_w_in, v_hg_lb, v_hg_gn_g, v_hg_w_out, v_ffn_w_up, v_ffn_conv_w, v_ffn_conv_b, v_ffn_w_down, v_norm_g, v_ada_w, v_ada_b, v_final_g):
    me = _flat(_mesh_pos())
    xt = x[0]
    t = xt.shape[0]

    small_shapes = [(1, D), (2, HG_DIM), (2, HG_DIM), (DEPTH, 2, HG_DIM), (DEPTH, 3, 2 * FFN_HIDDEN // N_DEV)]
    w_in_0 = gm_w_in[0].astype(bf16)
    small_all, w_in_0_all = all_gather([_pack([c, hg_lb, hg_gn_g, norm_g, ffn_conv_w]), w_in_0], "gather_first", relay=True)
    small_all = small_all.reshape(N_DEV, -1)
    at = 0
    pieces = []
    for s in small_shapes:
        n = math.prod(s)
        pieces.append(small_all[:, at:at + n].reshape((N_DEV,) + s))
        at += n
    c_all = pieces[0].reshape(N_DEV, D)
    hg_lb_full = jnp.transpose(pieces[1], (1, 0, 2)).reshape(2, D)
    hg_gn_full = jnp.transpose(pieces[2], (1, 0, 2)).reshape(2, D)
    norm_g_full = jnp.transpose(pieces[3], (1, 2, 0, 3)).reshape(DEPTH, 2, D)
    conv_w_full = jnp.transpose(pieces[4], (1, 2, 0, 3)).reshape(DEPTH, 3, 2 * FFN_HIDDEN)

    lb1 = lower_bound_fwd(hg_lb_full, "lower_bound")
    lbs = [jnp.zeros((1, D), f32), lb1]

    ada_b_cols = lax.dynamic_slice(ada_b, (0, me * ADA_COLS), (DEPTH, ADA_COLS)).reshape(DEPTH, 1, ADA_COLS)
    mod_cols = ada_mod(c_all, ada_w, ada_b_cols, "ada_mod")
    (mod_mine,) = all_to_all([jnp.transpose(mod_cols, (1, 0, 2))], "mod_to_examples")
    mod = jnp.transpose(mod_mine, (1, 0, 2)).reshape(DEPTH, 6, 1, D)

    def layer_shards(i):
        j = i // 2
        w_in, w_out = (gm_w_in, gm_w_out) if i % 2 == 0 else (hg_w_in, hg_w_out)
        return [w_in[j].astype(bf16), w_out[j].astype(bf16), ffn_w_up[i].T.astype(bf16), ffn_w_down[i].astype(bf16)]

    def full_rows(g):
        return g.reshape(N_DEV * g.shape[1], g.shape[2])

    carried_by = {
        "in_0": [(0, 1), (0, 3)], "mix_0": [(0, 2)], "up_0": [(1, 0), (1, 1)], "gate_0": [(1, 2)], "down_0": [(1, 3)],
        "in_1": [(2, 0)], "mix_1": [(2, 1), (2, 2), (2, 3)], "up_1": [(3, 0), (3, 1)], "gate_1": [(3, 2)], "down_1": [(3, 3)],
    }
    shards = [layer_shards(i) for i in range(DEPTH)]
    gathered = {}
    gathered[(0, 0)] = w_in_0_all

    def carry(call, site, **kw):
        items = carried_by.get(site, [])
        outs = call(riders=Riders([shards[l][slot] for l, slot in items], True), **kw)
        for item, g in zip(items, outs[len(outs) - len(items):]):
            gathered[item] = g
        return outs[:len(outs) - len(items)]

    saved = []
    weights = []
    xcur = xt
    h = norm_fwd(xcur, norm_g_full[0, 0:1], mod[0, 1], mod[0, 0], "norm1_0")
    for i in range(DEPTH):
        j = i // 2
        sh1, sc1, g1, sh2, sc2, g2 = [mod[i, p] for p in range(6)]
        gn2 = norm_g_full[i, 1:2]
        s = {"x0": xcur, "h": h}
        w_in = gathered[(i, 0)]
        if i % 2 == 0:
            (z,) = carry(functools.partial(mm_nn, h, w_in, bf16, f"gm_in_{i}"), f"in_{i}")
            bs = gm_b_s[j].reshape(GM_HEADS, GM_BLOCK, 1)
            (mixed,) = carry(functools.partial(gm_mix_fwd, z, gm_w_s[j], bs, gm_ln_g[j:j + 1], gm_ln_b[j:j + 1],
                                               f"gm_mix_{i}"), f"mix_{i}")
            s["z"] = z
        else:
            (proj,) = carry(functools.partial(mm_nn, h, w_in, f32, f"hg_in_{i}"), f"in_{i}")
            mixed, states = carry(functools.partial(hg_scan_fwd, proj, lbs[j], hg_gn_full[j:j + 1], f"hg_scan_{i}"),
                                  f"mix_{i}")
            s["proj"], s["states"] = proj, states
        s["mixed"] = mixed
        w_out = full_rows(gathered[(i, 1)])
        y, x1, h2 = carry(functools.partial(mm_nn_residual, mixed, w_out, xcur, g1, (gn2, sc2, sh2), f"mix_out_{i}"),
                          f"out_{i}")
        s["y"], s["x1"] = y, x1
        w_up = gathered[(i, 2)]
        (a,) = carry(functools.partial(mm_nn, h2, w_up, bf16, f"ffn_up_{i}", transposed=True), f"up_{i}")
        (hid,) = carry(functools.partial(ffn_gate_fwd, a, conv_w_full[i], ffn_conv_b[i:i + 1], f"ffn_gate_{i}"), f"gate_{i}")
        w_down = full_rows(gathered[(i, 3)])
        next_norm = (norm_g_full[i + 1, 0:1], mod[i + 1, 1], mod[i + 1, 0]) if i + 1 < DEPTH else None
        outs = carry(functools.partial(mm_nn_residual, hid, w_down, x1, g2, next_norm, f"ffn_down_{i}"), f"down_{i}")
        fo, x2 = outs[0], outs[1]
        s["h2"], s["a"], s["hid"], s["f"] = h2, a, hid, fo
        weights.append((w_in, w_out, w_up, w_down))
        saved.append(s)
        xcur = x2
        h = outs[2] if next_norm is not None else None

    loss_part, dx, d_final_g, dg2, df = loss_head(xcur, final_g.reshape(1, D), loss_target[0], saved[-1]["f"],
                                                  mod[DEPTH - 1, 5], "loss_head")
    loss = lax.psum(loss_part[0, 0], ("x", "y", "c"))

    def halves(blocked):
        rows = blocked.shape[1] // 2
        return [(blocked, (0, rows)), (blocked, (rows, rows))]

    def by_owner_rows(dw):
        k, n = dw.shape
        return dw.reshape(N_DEV, k // N_DEV, n)

    received = [[[] for _ in range(4)] for _ in range(DEPTH)]

    def send(call, items, **kw):
        outs = call(riders=Riders([arr for arr, _ in items], False), **kw)
        for (_, (layer, slot)), got in zip(items, outs[len(outs) - len(items):]):
            received[layer][slot].append(got)
        return outs[:len(outs) - len(items)]

    dmod = [None] * DEPTH
    d_norm_g = [None] * DEPTH
    d_gm = {k: [None, None] for k in ("ws", "bs", "lng", "lnb")}
    d_hg = {k: [None, None] for k in ("lb", "gn")}
    d_ffn = {k: [None] * DEPTH for k in ("cw", "cb")}
    in_halves = []
    for i in reversed(range(DEPTH)):
        j = i // 2
        s = saved[i]
        w_in, w_out, w_up, w_down = weights[i]
        sh1, sc1, g1, sh2, sc2, g2 = [mod[i, p] for p in range(6)]
        gn1, gn2 = norm_g_full[i, 0:1], norm_g_full[i, 1:2]
        (dw_down,) = mm_tn(s["hid"], df, bf16, f"dw_down_{i}", tn=D)
        scan_carries = i % 2 == 1
        down_item = (by_owner_rows(dw_down), (i, 3))
        (dhid,) = mm_nt(df, w_down, bf16, f"dhid_{i}")
        da, d_ffn["cw"][i], d_ffn["cb"][i] = send(
            functools.partial(ffn_gate_bwd, s["a"], conv_w_full[i], ffn_conv_b[i:i + 1], dhid, f"ffn_gate_bwd_{i}"),
            in_halves)
        (dw_up_t,) = send(functools.partial(mm_tn, da, s["h2"], bf16, f"dw_up_{i}"), [] if scan_carries else [down_item])
        dw_up_t = dw_up_t.reshape(N_DEV, -1, D)
        up_halves = [(dw_up_t, (i, 2))] if scan_carries else [(part, (i, 2)) for part in halves(dw_up_t)]
        (dh2,) = send(functools.partial(mm_nt, da, w_up, bf16, f"dh2_{i}", transposed=True),
                      [] if scan_carries else up_halves[:1])
        dx1, dgn2, dsc2, dsh2, dg1, dy = norm_bwd(s["x1"], gn2, sc2, sh2, dh2, dx, f"norm2_bwd_{i}", gate=(s["y"], g1))
        (dw_out,) = mm_tn(s["mixed"], dy, bf16, f"dw_mix_out_{i}")
        (dmixed,) = mm_nt(dy, w_out, bf16, f"dmixed_{i}")
        if i % 2 == 0:
            bs = gm_b_s[j].reshape(GM_HEADS, GM_BLOCK, 1)
            dpre, dws, dbs, dlng, dlnb = send(
                functools.partial(gm_mix_bwd, s["z"], gm_w_s[j], bs, gm_ln_g[j:j + 1], gm_ln_b[j:j + 1], dmixed,
                                  f"gm_mix_bwd_{i}"), up_halves[1:])
            d_gm["ws"][j], d_gm["bs"][j], d_gm["lng"][j], d_gm["lnb"][j] = dws, dbs.reshape(GM_HEADS, GM_BLOCK), dlng, dlnb
        else:
            dpre, dlb, dgn = send(
                functools.partial(hg_scan_bwd, s["proj"], lbs[j], hg_gn_full[j:j + 1], s["states"], dmixed,
                                  f"hg_scan_bwd_{i}"), [down_item] + up_halves)
            d_hg["lb"][j], d_hg["gn"][j] = dlb, dgn
        (dw_in,) = send(functools.partial(mm_tn_by_owner, s["h"], dpre, f"dw_mix_in_{i}"), [(by_owner_rows(dw_out), (i, 1))])
        in_halves = [(part, (i, 0)) for part in halves(dw_in)]
        (dh,) = send(functools.partial(mm_nt, dpre, w_in, bf16, f"dh_mix_{i}"), in_halves[:1] if i == 0 else [])
        dmod_i = [None, None, dg1, dsh2, dsc2, dg2]
        if i > 0:
            dx, dgn1, dsc1, dsh1, dg2, df = norm_bwd(s["x0"], gn1, sc1, sh1, dh, dx1, f"norm1_bwd_{i}",
                                                     gate=(saved[i - 1]["f"], mod[i - 1, 5]))
        else:
            dx, dgn1, dsc1, dsh1 = send(functools.partial(norm_bwd, s["x0"], gn1, sc1, sh1, dh, dx1, f"norm1_bwd_{i}"),
                                        in_halves[1:])
        dmod_i[0], dmod_i[1] = dsh1, dsc1
        dmod[i] = jnp.concatenate(dmod_i, axis=1)
        d_norm_g[i] = jnp.concatenate([dgn1, dgn2], axis=0)
    grad_x = dx.reshape(1, t, D)

    (dmod_all,) = all_gather([jnp.concatenate(dmod, axis=0)], "gather_dmod")
    dmod_cols = jnp.transpose(lax.dynamic_slice(dmod_all, (0, 0, me * ADA_COLS), (N_DEV, DEPTH, ADA_COLS)), (1, 0, 2))
    g_ada_w, g_ada_b = ada_grads(c_all, dmod_cols, dmod_all.reshape(N_DEV, DEPTH, 1, 6 * D), "ada_grads")
    g_ada_b = g_ada_b.reshape(DEPTH, 6 * D)

    small_partials = [jnp.concatenate(d_gm["lng"], axis=0), jnp.concatenate(d_gm["lnb"], axis=0),
                      jnp.stack(d_gm["ws"]), jnp.stack(d_gm["bs"]), jnp.concatenate(d_ffn["cb"], axis=0),
                      d_final_g, d_hg["lb"][1], jnp.concatenate(d_hg["gn"], axis=0), jnp.stack(d_norm_g),
                      jnp.stack(d_ffn["cw"])]
    partial_shapes = [p.shape for p in small_partials]
    packed = _pack(small_partials, rows_mult=8 * N_DEV)
    rows = packed.shape[0] // N_DEV
    (recv,) = all_to_all([packed.reshape(N_DEV, rows, LANES)], "small_grads_exchange")
    (summed,) = all_gather([sum_parts(recv, "small_grads_sum")], "small_grads_gather")

    def parts_of(slot, layers):
        return [received[i][slot] for i in layers]

    def swapped(a):
        return jnp.swapaxes(a, 1, 2)

    w_shards = [gm_w_in, gm_w_out, hg_w_in, hg_w_out, swapped(ffn_w_up), ffn_w_down]
    big_parts = [parts_of(0, (0, 2)), parts_of(1, (0, 2)), parts_of(0, (1, 3)), parts_of(1, (1, 3)),
                 parts_of(2, range(DEPTH)), parts_of(3, range(DEPTH))]
    big_m = [m_gm_w_in, m_gm_w_out, m_hg_w_in, m_hg_w_out, swapped(m_ffn_w_up), m_ffn_w_down]
    big_v = [v_gm_w_in, v_gm_w_out, v_hg_w_in, v_hg_w_out, swapped(v_ffn_w_up), v_ffn_w_down]
    big = [adam_reduced(parts, w, m_, v_, f"adam_big_{idx}")
           for idx, (w, m_, v_, parts) in enumerate(zip(w_shards, big_m, big_v, big_parts))]
    big[4] = [swapped(o) for o in big[4]]
    (g_gm_w_in, d_gm_w_in, nm_gm_w_in, nv_gm_w_in), (g_gm_w_out, d_gm_w_out, nm_gm_w_out, nv_gm_w_out), \
        (g_hg_w_in, d_hg_w_in, nm_hg_w_in, nv_hg_w_in), (g_hg_w_out, d_hg_w_out, nm_hg_w_out, nv_hg_w_out), \
        (g_ffn_w_up, d_ffn_w_up, nm_ffn_w_up, nv_ffn_w_up), (g_ffn_w_down, d_ffn_w_down, nm_ffn_w_down, nv_ffn_w_down) = big

    g_ln_g, g_ln_b, g_ws, g_bs, g_cb, g_final, g_lb1, g_gn, g_norm, g_cw = _unpack(summed.reshape(-1), partial_shapes)
    g_final = g_final.reshape(D)

    def my_cols(a, n):
        start = (0,) * (a.ndim - 1) + (me * n,)
        return lax.dynamic_slice(a, start, a.shape[:-1] + (n,))

    g_hg_lb = lower_bound_bwd(hg_lb, my_cols(g_lb1, HG_DIM), "lower_bound_bwd")
    g_hg_gn = my_cols(g_gn, HG_DIM)
    g_norm_g = my_cols(g_norm, HG_DIM)
    g_conv_w = my_cols(g_cw, 2 * FFN_HIDDEN // N_DEV)

    two_d = (-1, ADA_COLS)
    d_ada_w, nm_ada_w, nv_ada_w = [o.reshape(ada_w.shape) for o in adam_plain(
        g_ada_w.reshape(two_d), ada_w.reshape(two_d), m_ada_w.reshape(two_d), v_ada_w.reshape(two_d), "adam_ada_w")]

    small_g = [g_ln_g, g_ln_b, g_ws, g_bs, g_cb, g_ada_b, g_final, g_hg_lb, g_hg_gn, g_norm_g, g_conv_w]
    small_w = [gm_ln_g, gm_ln_b, gm_w_s, gm_b_s, ffn_conv_b, ada_b, final_g, hg_lb, hg_gn_g, norm_g, ffn_conv_w]
    small_m = [m_gm_ln_g, m_gm_ln_b, m_gm_w_s, m_gm_b_s, m_ffn_conv_b, m_ada_b, m_final_g, m_hg_lb, m_hg_gn_g, m_norm_g, m_ffn_conv_w]
    small_v = [v_gm_ln_g, v_gm_ln_b, v_gm_w_s, v_gm_b_s, v_ffn_conv_b, v_ada_b, v_final_g, v_hg_lb, v_hg_gn_g, v_norm_g, v_ffn_conv_w]
    shapes = [w.shape for w in small_w]
    small_g = [g.reshape(s) for g, s in zip(small_g, shapes)]
    outs = adam_plain(_pack(small_g), _pack(small_w), _pack(small_m), _pack(small_v), "adam_small")
    (d_ln_g, d_ln_b, d_ws, d_bs, d_cb, d_ada_b, d_final, d_hg_lb, d_hg_gn, d_norm_g_, d_conv_w), \
        (nm_ln_g, nm_ln_b, nm_ws, nm_bs, nm_cb, nm_ada_b, nm_final, nm_hg_lb, nm_hg_gn, nm_norm_g, nm_conv_w), \
        (nv_ln_g, nv_ln_b, nv_ws, nv_bs, nv_cb, nv_ada_b, nv_final, nv_hg_lb, nv_hg_gn, nv_norm_g, nv_conv_w) = [
            _unpack(o.reshape(-1), shapes) for o in outs]
    g_ln_g, g_ln_b, g_ws, g_bs, g_cb, g_ada_b, g_final, g_hg_lb, g_hg_gn, g_norm_g, g_conv_w = small_g

    grads = (g_gm_w_in, g_ln_g, g_ln_b, g_ws, g_bs, g_gm_w_out, g_hg_w_in, g_hg_lb, g_hg_gn, g_hg_w_out,
             g_ffn_w_up, g_conv_w, g_cb, g_ffn_w_down, g_norm_g, g_ada_w, g_ada_b, g_final)
    deltas = (d_gm_w_in, d_ln_g, d_ln_b, d_ws, d_bs, d_gm_w_out, d_hg_w_in, d_hg_lb, d_hg_gn, d_hg_w_out,
              d_ffn_w_up, d_conv_w, d_cb, d_ffn_w_down, d_norm_g_, d_ada_w, d_ada_b, d_final)
    new_m = (nm_gm_w_in, nm_ln_g, nm_ln_b, nm_ws, nm_bs, nm_gm_w_out, nm_hg_w_in, nm_hg_lb, nm_hg_gn, nm_hg_w_out,
             nm_ffn_w_up, nm_conv_w, nm_cb, nm_ffn_w_down, nm_norm_g, nm_ada_w, nm_ada_b, nm_final)
    new_v = (nv_gm_w_in, nv_ln_g, nv_ln_b, nv_ws, nv_bs, nv_gm_w_out, nv_hg_w_in, nv_hg_lb, nv_hg_gn, nv_hg_w_out,
             nv_ffn_w_up, nv_conv_w, nv_cb, nv_ffn_w_down, nv_norm_g, nv_ada_w, nv_ada_b, nv_final)
    return (loss, grad_x) + grads + deltas + new_m + new_v
```

```python
import functools
import math

import jax
import jax.numpy as jnp
from jax import lax
from jax.experimental import pallas as pl
from jax.experimental.pallas import tpu as pltpu

f32 = jnp.float32
bf16 = jnp.bfloat16
MESH = pl.DeviceIdType.MESH

N_DEV = 8
D = 1024
DEPTH = 4
EPS = 1e-6
GM_WIDTH = 2048
GM_HEADS = 8
GM_HEAD_DIM = 256
GM_BLOCK = 128
CHUNK = 64
HG_HEADS = 8
HG_DIM = 128
FFN_HIDDEN = 2816
ADA_COLS = 6 * D // N_DEV

HG_SUB_FWD = 64
HG_SUB_BWD = 32
HG_PAIR = 8
HG_TOKENS = 128

ADAM_LR = 0.001
ADAM_B1 = 0.9
ADAM_B2 = 0.999
ADAM_EPS = 1e-08
ADAM_WD = 0.01
ADAM_STEP = 10

V7X_VMEM_LIMIT = 56 * 1024 * 1024
LANES = 128


def _cparams(*sem):
    return pltpu.CompilerParams(dimension_semantics=sem or None, vmem_limit_bytes=V7X_VMEM_LIMIT)


def _tile(n, target, mult=LANES):
    best = None
    for t in range(mult, min(n, target) + 1, mult):
        if n % t == 0:
            best = t
    return best or n


WEIGHT_BLOCK_BYTES = 6 * 1024 * 1024


def _weight_tile(n, k):
    return _tile(n, max(LANES, WEIGHT_BLOCK_BYTES // (2 * k)))


def _gelu(x):
    return 0.5 * x * (1.0 + lax.erf(x * (1.0 / math.sqrt(2.0))))


def _mesh_pos():
    return lax.axis_index("x"), lax.axis_index("y"), lax.axis_index("c")


def _flat(pos):
    return 4 * pos[0] + 2 * pos[1] + pos[2]


def _peer(pos, k):
    return ((1 - pos[0]) if k & 4 else pos[0], (1 - pos[1]) if k & 2 else pos[1], (1 - pos[2]) if k & 1 else pos[2])


def _exchange_copies(ins, outs, send_sems, recv_sems, local_sems, gather):
    pos = _mesh_pos()
    me = _flat(pos)

    def src(i, dest):
        if gather:
            return ins[i]
        ref, rows = ins[i] if isinstance(ins[i], tuple) else (ins[i], None)
        return ref.at[dest] if rows is None else ref.at[dest, pl.ds(*rows)]

    local = [pltpu.make_async_copy(src(i, me), outs[i].at[me], local_sems.at[i]) for i in range(len(ins))]
    sends, recvs = [], []
    for k in range(1, N_DEV):
        peer = _peer(pos, k)
        there = _flat(peer)
        for i in range(len(ins)):
            sems = dict(send_sem=send_sems.at[i * 7 + k - 1], recv_sem=recv_sems.at[i * 7 + k - 1],
                        device_id=peer, device_id_type=MESH)
            sends.append(pltpu.make_async_remote_copy(src_ref=src(i, there), dst_ref=outs[i].at[me], **sems))
            recvs.append(pltpu.make_async_remote_copy(src_ref=src(i, there), dst_ref=outs[i].at[there], **sems))
    return local, sends, recvs


def _exchange_start(*refs):
    local, sends, _ = _exchange_copies(*refs)
    for cp in local + sends:
        cp.start()


def _exchange_wait(*refs):
    local, sends, recvs = _exchange_copies(*refs)
    for cp in recvs:
        cp.wait_recv()
    for cp in sends:
        cp.wait_send()
    for cp in local:
        cp.wait()


OTHER_CHIPS = (2, 4, 6)


def _relay_copies(ins, outs, send_sems, recv_sems, local_sems):
    pos = _mesh_pos()
    me = _flat(pos)
    sibling = _peer(pos, 1)
    local = [pltpu.make_async_copy(ins[i], outs[i].at[me], local_sems.at[i]) for i in range(len(ins))]
    first, passes, recvs = [], {k: [] for k in OTHER_CHIPS}, {k: [] for k in range(1, N_DEV)}
    for i in range(len(ins)):
        def copy(k, src, block, to):
            return pltpu.make_async_remote_copy(
                src_ref=src, dst_ref=outs[i].at[block], send_sem=send_sems.at[i * 7 + k - 1],
                recv_sem=recv_sems.at[i * 7 + k - 1], device_id=to, device_id_type=MESH)

        for k in (1,) + OTHER_CHIPS:
            first.append(copy(k, ins[i], me, _peer(pos, k)))
        for k in OTHER_CHIPS:
            there = _flat(_peer(pos, k))
            passes[k].append(copy(k ^ 1, outs[i].at[there], there, sibling))
        for k in range(1, N_DEV):
            there = _flat(_peer(pos, k))
            recvs[k].append(copy(k, ins[i], there, _peer(pos, k)))
    return local, first, passes, recvs


def _relay_start(ins, outs, *sems):
    local, first, _, _ = _relay_copies(ins, outs, *sems)
    for cp in local + first:
        cp.start()


def _relay_wait(ins, outs, *sems):
    local, first, passes, recvs = _relay_copies(ins, outs, *sems)
    for k in OTHER_CHIPS:
        for cp in recvs[k]:
            cp.wait_recv()
        for cp in passes[k]:
            cp.start()
    for k in (1, 3, 5, 7):
        for cp in recvs[k]:
            cp.wait_recv()
    for cp in first + [cp for k in OTHER_CHIPS for cp in passes[k]]:
        cp.wait_send()
    for cp in local:
        cp.wait()


def _exchange_out_shape(a, gather):
    return jax.ShapeDtypeStruct((N_DEV,) + tuple(a.shape) if gather else tuple(a.shape), a.dtype)


def _exchange_sems(n):
    return [pltpu.SemaphoreType.DMA((7 * n,)), pltpu.SemaphoreType.DMA((7 * n,)), pltpu.SemaphoreType.DMA((n,))]


ANY = pl.BlockSpec(memory_space=pl.ANY)


def _exchange(arrs, gather, name, relay=False):
    n = len(arrs)

    def body(*refs):
        ins, outs = refs[:n], refs[n:2 * n]
        if relay:
            _relay_start(ins, outs, *refs[2 * n:])
            _relay_wait(ins, outs, *refs[2 * n:])
        else:
            _exchange_start(ins, outs, *refs[2 * n:], gather)
            _exchange_wait(ins, outs, *refs[2 * n:], gather)

    return pl.pallas_call(
        body, name=name, out_shape=tuple(_exchange_out_shape(a, gather) for a in arrs),
        in_specs=[ANY] * n, out_specs=tuple([ANY] * n), scratch_shapes=_exchange_sems(n),
    )(*arrs)


def all_gather(arrs, name, relay=False):
    return _exchange(arrs, True, name, relay)


def all_to_all(arrs, name):
    return _exchange(arrs, False, name)


class Riders:
    def __init__(self, arrs, gather):
        self.gather = gather
        self.rows = [a[1] if isinstance(a, tuple) else None for a in arrs]
        self.arrs = [a[0] if isinstance(a, tuple) else a for a in arrs]

    def out_shapes(self):
        shapes = []
        for a, rows in zip(self.arrs, self.rows):
            shape = tuple(a.shape) if rows is None else (a.shape[0], rows[1], a.shape[2])
            shapes.append(jax.ShapeDtypeStruct((N_DEV,) + shape if self.gather else shape, a.dtype))
        return shapes


def _call(body, *, name, grid, in_specs, out_specs, out_shape, semantics, scratch_shapes=(), riders=None):
    if riders is None or not riders.arrs:
        return pl.pallas_call(body, name=name, grid=grid, in_specs=in_specs, out_specs=tuple(out_specs),
                              out_shape=tuple(out_shape), scratch_shapes=list(scratch_shapes),
                              compiler_params=_cparams(*semantics))
    n_in, n_out, n_scr, n_r = len(in_specs), len(out_specs), len(scratch_shapes), len(riders.arrs)
    gather = riders.gather

    def hosted(*refs):
        ins, r_ins = refs[:n_in], refs[n_in:n_in + n_r]
        at = n_in + n_r
        outs, r_outs = refs[at:at + n_out], refs[at + n_out:at + n_out + n_r]
        at += n_out + n_r
        scratch, sems = refs[at:at + n_scr], refs[at + n_scr:]
        first = functools.reduce(jnp.logical_and, [pl.program_id(a) == 0 for a in range(len(grid))])
        last = functools.reduce(jnp.logical_and, [pl.program_id(a) == grid[a] - 1 for a in range(len(grid))])

        r_ins = [(ref, rows) if rows is not None else ref for ref, rows in zip(r_ins, riders.rows)]

        @pl.when(first)
        def _():
            if gather:
                _relay_start(r_ins, r_outs, *sems)
            else:
                _exchange_start(r_ins, r_outs, *sems, gather)

        body(*ins, *outs, *scratch)

        @pl.when(last)
        def _():
            if gather:
                _relay_wait(r_ins, r_outs, *sems)
            else:
                _exchange_wait(r_ins, r_outs, *sems, gather)

    call = pl.pallas_call(
        hosted, name=name, grid=grid, in_specs=list(in_specs) + [ANY] * n_r, out_specs=tuple(out_specs) + (ANY,) * n_r,
        out_shape=tuple(out_shape) + tuple(riders.out_shapes()),
        scratch_shapes=list(scratch_shapes) + _exchange_sems(n_r),
        compiler_params=_cparams(*(("arbitrary",) * len(grid))))
    return lambda *args: call(*args, *riders.arrs)


def _shards_per_step(shape):
    _, k, n = shape
    best = None
    for q in (1, 2, 4, 8):
        if (q * n) % LANES == 0 and (best is None or 2 * k * q * n <= WEIGHT_BLOCK_BYTES):
            best = q
    return best


def mm_nn(a, b, out_dtype, name, tm=512, riders=None, transposed=False):
    m, k = a.shape
    tm = _tile(m, tm, 8)
    if b.ndim == 3:
        shard = b.shape[1] if transposed else b.shape[2]
        n = N_DEV * shard
        per_step = _shards_per_step((N_DEV, k, shard))
        tn = per_step * shard
        b_spec = pl.BlockSpec((per_step,) + b.shape[1:], lambda i, j: (j, 0, 0))
        contract = (((1,), (1,)), ((), ())) if transposed else (((1,), (0,)), ((), ()))

        def body(a_ref, b_ref, o_ref):
            for q in range(per_step):
                o_ref[:, q * shard:(q + 1) * shard] = lax.dot_general(
                    a_ref[...], b_ref[q], contract, preferred_element_type=f32).astype(o_ref.dtype)
    else:
        n = b.shape[1]
        tn = _weight_tile(n, k)
        b_spec = pl.BlockSpec((k, tn), lambda i, j: (0, j))

        def body(a_ref, b_ref, o_ref):
            o_ref[...] = jnp.dot(a_ref[...], b_ref[...], preferred_element_type=f32).astype(o_ref.dtype)

    return _call(
        body, name=name, grid=(m // tm, n // tn),
        in_specs=[pl.BlockSpec((tm, k), lambda i, j: (i, 0)), b_spec],
        out_specs=[pl.BlockSpec((tm, tn), lambda i, j: (i, j))],
        out_shape=[jax.ShapeDtypeStruct((m, n), out_dtype)], semantics=("parallel", "parallel"), riders=riders,
    )(a, b)


def mm_nn_residual(a, b, x, gate, norm, name, tm=512, riders=None):
    m, k = a.shape
    n = b.shape[1]
    tm = _tile(m, tm, 8)

    def body(a_ref, b_ref, x_ref, g_ref, *rest):
        if norm is not None:
            gn_ref, sc_ref, sh_ref, y_ref, o_ref, h_ref = rest
        else:
            y_ref, o_ref = rest
        y = jnp.dot(a_ref[...], b_ref[...], preferred_element_type=f32)
        y_ref[...] = y.astype(bf16)
        x_new = x_ref[...] + g_ref[...] * y
        o_ref[...] = x_new
        if norm is not None:
            h_ref[...] = _norm_fn(x_new, gn_ref[...], sc_ref[...], sh_ref[...]).astype(bf16)

    blk = pl.BlockSpec((tm, n), lambda i: (i, 0))
    in_specs = [pl.BlockSpec((tm, k), lambda i: (i, 0)), pl.BlockSpec((k, n), lambda i: (0, 0)), blk, _row(n)]
    out_specs = [blk, blk]
    out_shape = [jax.ShapeDtypeStruct((m, n), bf16), jax.ShapeDtypeStruct((m, n), f32)]
    args = [a, b, x, gate]
    if norm is not None:
        in_specs += [_row(n)] * 3
        out_specs += [blk]
        out_shape += [jax.ShapeDtypeStruct((m, n), bf16)]
        args += list(norm)
    return _call(body, name=name, grid=(m // tm,), in_specs=in_specs, out_specs=out_specs, out_shape=out_shape,
                 semantics=("parallel",), riders=riders)(*args)


def mm_nt(a, b, out_dtype, name, tm=512, riders=None, transposed=False):
    m = a.shape[0]
    tm = _tile(m, tm, 8)
    if b.ndim == 3:
        shard, k = (b.shape[1], b.shape[2]) if transposed else (b.shape[2], b.shape[1])
        tk = k
        b_spec = pl.BlockSpec(b.shape, lambda i, j: (0, 0, 0))
        width = N_DEV * shard
        contract = (((1,), (0,)), ((), ())) if transposed else (((1,), (1,)), ((), ()))

        def body(a_ref, b_ref, o_ref):
            acc = None
            for q in range(N_DEV):
                part = lax.dot_general(a_ref[:, q * shard:(q + 1) * shard], b_ref[q], contract, preferred_element_type=f32)
                acc = part if acc is None else acc + part
            o_ref[...] = acc.astype(o_ref.dtype)
    else:
        k, width = b.shape
        tk = _weight_tile(k, width)
        b_spec = pl.BlockSpec((tk, width), lambda i, j: (j, 0))

        def body(a_ref, b_ref, o_ref):
            o_ref[...] = lax.dot_general(a_ref[...], b_ref[...], (((1,), (1,)), ((), ())),
                                         preferred_element_type=f32).astype(o_ref.dtype)

    return _call(
        body, name=name, grid=(m // tm, k // tk),
        in_specs=[pl.BlockSpec((tm, width), lambda i, j: (i, 0)), b_spec],
        out_specs=[pl.BlockSpec((tm, tk), lambda i, j: (i, j))],
        out_shape=[jax.ShapeDtypeStruct((m, k), out_dtype)], semantics=("parallel", "parallel"), riders=riders,
    )(a, b)


def mm_tn(a, b, out_dtype, name, tm=512, tn=512, riders=None):
    t, m = a.shape
    n = b.shape[1]
    tm, tn = _tile(m, tm), _tile(n, tn)

    def body(a_ref, b_ref, o_ref):
        o_ref[...] = lax.dot_general(a_ref[...], b_ref[...], (((0,), (0,)), ((), ())),
                                     preferred_element_type=f32).astype(o_ref.dtype)

    return _call(
        body, name=name, grid=(m // tm, n // tn),
        in_specs=[pl.BlockSpec((t, tm), lambda i, j: (0, i)), pl.BlockSpec((t, tn), lambda i, j: (0, j))],
        out_specs=[pl.BlockSpec((tm, tn), lambda i, j: (i, j))],
        out_shape=[jax.ShapeDtypeStruct((m, n), out_dtype)], semantics=("parallel", "parallel"), riders=riders,
    )(a, b)


def mm_tn_by_owner(a, b, name, tm=512, riders=None):
    t, m = a.shape
    n = b.shape[1]
    shard = n // N_DEV
    per_step = 1 if shard % LANES == 0 else 2
    assert (per_step * shard) % LANES == 0
    tm = _tile(m, tm)

    def body(a_ref, b_ref, o_ref):
        acc = lax.dot_general(a_ref[...], b_ref[...], (((0,), (0,)), ((), ())), preferred_element_type=f32)
        for q in range(per_step):
            o_ref[q] = acc[:, q * shard:(q + 1) * shard].astype(bf16)

    return _call(
        body, name=name, grid=(m // tm, N_DEV // per_step),
        in_specs=[pl.BlockSpec((t, tm), lambda i, j: (0, i)), pl.BlockSpec((t, per_step * shard), lambda i, j: (0, j))],
        out_specs=[pl.BlockSpec((per_step, tm, shard), lambda i, j: (j, i, 0))],
        out_shape=[jax.ShapeDtypeStruct((N_DEV, m, shard), bf16)], semantics=("parallel", "parallel"), riders=riders,
    )(a, b)


def _norm_fn(x, gn, sc, sh):
    r = lax.rsqrt(jnp.mean(x * x, axis=-1, keepdims=True) + EPS)
    return (x * r * gn) * (1.0 + sc) + sh


def _row(d):
    return pl.BlockSpec((1, d), lambda i: (0, 0))


def norm_fwd(x, gn, sc, sh, name, tm=512):
    t, d = x.shape
    tm = _tile(t, tm, 8)

    def body(x_ref, gn_ref, sc_ref, sh_ref, h_ref):
        h_ref[...] = _norm_fn(x_ref[...], gn_ref[...], sc_ref[...], sh_ref[...]).astype(bf16)

    return pl.pallas_call(
        body, name=name, grid=(t // tm,),
        in_specs=[pl.BlockSpec((tm, d), lambda i: (i, 0)), _row(d), _row(d), _row(d)],
        out_specs=pl.BlockSpec((tm, d), lambda i: (i, 0)),
        out_shape=jax.ShapeDtypeStruct((t, d), bf16), compiler_params=_cparams("parallel"),
    )(x, gn, sc, sh)


def _gate_bwd(dx, y_ref, g_ref, dgate_ref, dy_ref):
    dgate_ref[...] += jnp.sum(dx * y_ref[...].astype(f32), axis=0, keepdims=True)
    dy_ref[...] = (dx * g_ref[...]).astype(bf16)


def norm_bwd(x, gn, sc, sh, dh, dres, name, gate=None, tm=512, riders=None):
    t, d = x.shape
    tm = _tile(t, tm, 8)

    def body(x_ref, gn_ref, sc_ref, sh_ref, dh_ref, dres_ref, *rest):
        if gate is not None:
            y_ref, g_ref, dx_ref, dgn_ref, dsc_ref, dsh_ref, dgate_ref, dy_ref = rest
        else:
            dx_ref, dgn_ref, dsc_ref, dsh_ref = rest

        @pl.when(pl.program_id(0) == 0)
        def _():
            dgn_ref[...] = jnp.zeros_like(dgn_ref)
            dsc_ref[...] = jnp.zeros_like(dsc_ref)
            dsh_ref[...] = jnp.zeros_like(dsh_ref)
            if gate is not None:
                dgate_ref[...] = jnp.zeros_like(dgate_ref)

        _, vjp = jax.vjp(_norm_fn, x_ref[...], gn_ref[...], sc_ref[...], sh_ref[...])
        dx, dgn, dsc, dsh = vjp(dh_ref[...].astype(f32))
        dx = dx + dres_ref[...]
        dx_ref[...] = dx
        dgn_ref[...] += dgn
        dsc_ref[...] += dsc
        dsh_ref[...] += dsh
        if gate is not None:
            _gate_bwd(dx, y_ref, g_ref, dgate_ref, dy_ref)

    blk = pl.BlockSpec((tm, d), lambda i: (i, 0))
    vec = jax.ShapeDtypeStruct((1, d), f32)
    in_specs = [blk, _row(d), _row(d), _row(d), blk, blk]
    out_specs = [blk, _row(d), _row(d), _row(d)]
    out_shape = [jax.ShapeDtypeStruct((t, d), f32), vec, vec, vec]
    args = [x, gn, sc, sh, dh, dres]
    if gate is not None:
        in_specs += [blk, _row(d)]
        out_specs += [_row(d), blk]
        out_shape += [vec, jax.ShapeDtypeStruct((t, d), bf16)]
        args += list(gate)
    return _call(body, name=name, grid=(t // tm,), in_specs=in_specs, out_specs=out_specs, out_shape=out_shape,
                 semantics=("arbitrary",), riders=riders)(*args)


def _loss_fn(x, g, tgt):
    r = lax.rsqrt(jnp.mean(x * x, axis=-1, keepdims=True) + EPS)
    err = jnp.square(x * r * g - tgt)
    return 0.5 * jnp.sum(jnp.mean(err, axis=-1, keepdims=True), axis=0, keepdims=True)


def loss_head(x, g, tgt, y, gate, name, tm=512):
    t, d = x.shape
    tm = _tile(t, tm, 8)

    def body(x_ref, g_ref, t_ref, y_ref, gate_ref, loss_ref, dx_ref, dg_ref, dgate_ref, dy_ref):
        @pl.when(pl.program_id(0) == 0)
        def _():
            loss_ref[...] = jnp.zeros_like(loss_ref)
            dg_ref[...] = jnp.zeros_like(dg_ref)
            dgate_ref[...] = jnp.zeros_like(dgate_ref)

        loss, vjp = jax.vjp(_loss_fn, x_ref[...], g_ref[...], t_ref[...])
        dx, dg, _ = vjp(jnp.ones((1, 1), f32))
        dx_ref[...] = dx
        loss_ref[...] += loss
        dg_ref[...] += dg
        _gate_bwd(dx, y_ref, gate_ref, dgate_ref, dy_ref)

    blk = pl.BlockSpec((tm, d), lambda i: (i, 0))
    vec = jax.ShapeDtypeStruct((1, d), f32)
    return pl.pallas_call(
        body, name=name, grid=(t // tm,),
        in_specs=[blk, _row(d), blk, blk, _row(d)],
        out_specs=(pl.BlockSpec((1, 1), lambda i: (0, 0)), blk, _row(d), _row(d), blk),
        out_shape=(jax.ShapeDtypeStruct((1, 1), f32), jax.ShapeDtypeStruct((t, d), f32), vec, vec,
                   jax.ShapeDtypeStruct((t, d), bf16)),
        compiler_params=_cparams("arbitrary"),
    )(x, g, tgt, y, gate)


def _gm_block_fn(z, ws, bs, lng, lnb):
    u = _gelu(z[:, :GM_WIDTH])
    vg = _gelu(z[:, GM_WIDTH:])
    mu = jnp.mean(vg, axis=-1, keepdims=True)
    var = jnp.mean(jnp.square(vg - mu), axis=-1, keepdims=True)
    vn = (vg - mu) * lax.rsqrt(var + EPS) * lng + lnb
    row = lax.broadcasted_iota(jnp.int32, (GM_BLOCK, GM_BLOCK), 0) // CHUNK
    col = lax.broadcasted_iota(jnp.int32, (GM_BLOCK, GM_BLOCK), 1) // CHUNK
    parts = []
    for h in range(GM_HEADS):
        w = jnp.where(row >= col, ws[h], 0.0)
        cols = slice(h * GM_HEAD_DIM, (h + 1) * GM_HEAD_DIM)
        s = jnp.dot(w.astype(bf16), vn[:, cols].astype(bf16), preferred_element_type=f32) + bs[h]
        parts.append(u[:, cols] * s)
    return jnp.concatenate(parts, axis=1)


def _gm_param_specs():
    return [pl.BlockSpec((GM_HEADS, GM_BLOCK, GM_BLOCK), lambda i: (0, 0, 0)),
            pl.BlockSpec((GM_HEADS, GM_BLOCK, 1), lambda i: (0, 0, 0)), _row(GM_WIDTH), _row(GM_WIDTH)]


def gm_mix_fwd(z, ws, bs, lng, lnb, name, riders=None):
    t = z.shape[0]

    def body(z_ref, ws_ref, bs_ref, lng_ref, lnb_ref, o_ref):
        o_ref[...] = _gm_block_fn(z_ref[...].astype(f32), ws_ref[...], bs_ref[...], lng_ref[...],
                                  lnb_ref[...]).astype(bf16)

    return _call(
        body, name=name, grid=(t // GM_BLOCK,),
        in_specs=[pl.BlockSpec((GM_BLOCK, 2 * GM_WIDTH), lambda i: (i, 0))] + _gm_param_specs(),
        out_specs=[pl.BlockSpec((GM_BLOCK, GM_WIDTH), lambda i: (i, 0))],
        out_shape=[jax.ShapeDtypeStruct((t, GM_WIDTH), bf16)], semantics=("parallel",), riders=riders,
    )(z, ws, bs, lng, lnb)


def gm_mix_bwd(z, ws, bs, lng, lnb, dgated, name, riders=None):
    t = z.shape[0]

    def body(z_ref, ws_ref, bs_ref, lng_ref, lnb_ref, dg_ref, dz_ref, dws_ref, dbs_ref, dlng_ref, dlnb_ref):
        _, vjp = jax.vjp(_gm_block_fn, z_ref[...].astype(f32), ws_ref[...], bs_ref[...], lng_ref[...], lnb_ref[...])
        dz, dws, dbs, dlng, dlnb = vjp(dg_ref[...].astype(f32))
        dz_ref[...] = dz.astype(bf16)

        @pl.when(pl.program_id(0) == 0)
        def _():
            dws_ref[...] = jnp.zeros_like(dws_ref)
            dbs_ref[...] = jnp.zeros_like(dbs_ref)
            dlng_ref[...] = jnp.zeros_like(dlng_ref)
            dlnb_ref[...] = jnp.zeros_like(dlnb_ref)

        dws_ref[...] += dws
        dbs_ref[...] += dbs
        dlng_ref[...] += dlng
        dlnb_ref[...] += dlnb

    zblk = pl.BlockSpec((GM_BLOCK, 2 * GM_WIDTH), lambda i: (i, 0))
    return _call(
        body, name=name, grid=(t // GM_BLOCK,),
        in_specs=[zblk] + _gm_param_specs() + [pl.BlockSpec((GM_BLOCK, GM_WIDTH), lambda i: (i, 0))],
        out_specs=[zblk] + _gm_param_specs(),
        out_shape=[jax.ShapeDtypeStruct((t, 2 * GM_WIDTH), bf16),
                   jax.ShapeDtypeStruct((GM_HEADS, GM_BLOCK, GM_BLOCK), f32),
                   jax.ShapeDtypeStruct((GM_HEADS, GM_BLOCK, 1), f32),
                   jax.ShapeDtypeStruct((1, GM_WIDTH), f32), jax.ShapeDtypeStruct((1, GM_WIDTH), f32)],
        semantics=("arbitrary",), riders=riders,
    )(z, ws, bs, lng, lnb, dgated)


@functools.partial(jax.custom_vjp, nondiff_argnums=(1,))
def _rows_up(x, shift):
    return x if shift == 0 else pltpu.roll(x, x.shape[1] - shift, axis=1)


def _rows_up_fwd(x, shift):
    return _rows_up(x, shift), None


def _rows_up_bwd(shift, _, g):
    return (g if shift == 0 else pltpu.roll(g, shift, axis=1),)


_rows_up.defvjp(_rows_up_fwd, _rows_up_bwd)


def _hg_block_fn(sub, qp, fz, iv, gp, s0, lb, gn):
    n, ns, d = sub, HG_TOKENS // sub, HG_DIM
    p, nb, per_sub = HG_PAIR, HG_TOKENS // HG_PAIR, sub // HG_PAIR
    f = lb + (1.0 - lb) * jax.nn.sigmoid(fz)
    g = jnp.log(f)
    k = 1.0 - f
    q = qp * jax.nn.sigmoid(qp)
    v = iv.astype(bf16)
    row = lax.broadcasted_iota(jnp.int32, (HG_TOKENS, HG_TOKENS), 0)
    col = lax.broadcasted_iota(jnp.int32, (HG_TOKENS, HG_TOKENS), 1)
    same_sub = col // n == row // n
    tri = ((col <= row) & same_sub).astype(f32)
    cum = jnp.dot(tri, g, precision=lax.Precision.HIGHEST, preferred_element_type=f32)
    cum_b, q_b, k_b, f_b = cum.reshape(nb, p, d), q.reshape(nb, p, d), k.reshape(nb, p, d), f.reshape(nb, p, d)
    j_b = lax.broadcasted_iota(jnp.int32, (nb, p, d), 1)
    j_col = lax.broadcasted_iota(jnp.int32, (nb, p, 1), 1)
    scores_t = jnp.zeros((HG_TOKENS, HG_TOKENS), f32)
    weight = k_b
    for delta in range(p):
        if delta:
            weight = weight * _rows_up(f_b, delta)
        pair = jnp.sum(_rows_up(q_b, delta) * weight, axis=2, keepdims=True)
        pair = jnp.where(j_col < p - delta, pair, 0.0)
        scores_t = scores_t + jnp.where(col == row + delta, pair.reshape(HG_TOKENS, 1), 0.0)
    o = lax.dot_general(scores_t.astype(bf16), v, (((0,), (0,)), ((), ())), preferred_element_type=f32)
    last = cum_b[:, p - 1:p, :]
    before = jnp.concatenate([jnp.zeros((1, 1, d), f32), last[:-1]], axis=0)
    before = jnp.broadcast_to(before, (nb, p, d)).reshape(HG_TOKENS, d)
    block = (lax.broadcasted_iota(jnp.int32, (HG_TOKENS, d), 0) // p) % per_sub
    q_late = q * jnp.exp(jnp.where(block > 0, cum - before, -1e30))
    last_s = last.reshape(ns, per_sub, d)
    q_parts, k_parts = [], []
    for m in range(1, per_sub):
        split = jnp.broadcast_to(last_s[:, m - 1:m, :], (ns, n, d)).reshape(HG_TOKENS, d)
        k_parts.append(k * jnp.exp(jnp.where(block < m, split - cum, -1e30)))
        q_parts.append(jnp.where(block == m, q_late, 0.0))
    scores = lax.dot_general(jnp.concatenate(q_parts, axis=1).astype(bf16), jnp.concatenate(k_parts, axis=1).astype(bf16),
                             (((1,), (1,)), ((), ())), preferred_element_type=f32)
    o = o + jnp.dot(jnp.where(same_sub, scores, 0.0).astype(bf16), v, preferred_element_type=f32)
    cum_s = cum.reshape(ns, n, d)
    tot = cum_s[:, n - 1:n, :]
    kt_t = (k.reshape(ns, n, d) * jnp.exp(tot - cum_s)).reshape(HG_TOKENS, d).T
    lane_sub = lax.broadcasted_iota(jnp.int32, (d, HG_TOKENS), 1) // n
    k_by_sub = jnp.concatenate([jnp.where(lane_sub == b, kt_t, 0.0) for b in range(ns)], axis=0).astype(bf16)
    update = jnp.dot(k_by_sub, v, preferred_element_type=f32)
    decay = jnp.exp(tot.reshape(ns, d)).T
    state = s0
    states = []
    for a in range(ns):
        states.append(state.astype(bf16))
        state = decay[:, a:a + 1] * state + update[a * d:(a + 1) * d]
    qt = q * jnp.exp(cum)
    row_sub = lax.broadcasted_iota(jnp.int32, (HG_TOKENS, d), 0) // n
    q_by_sub = jnp.concatenate([jnp.where(row_sub == a, qt, 0.0) for a in range(ns)], axis=1).astype(bf16)
    o = o + jnp.dot(q_by_sub, jnp.concatenate(states, axis=0), preferred_element_type=f32)
    on = o * lax.rsqrt(jnp.mean(o * o, axis=-1, keepdims=True) + EPS) * gn
    return on * (gp * jax.nn.sigmoid(gp)), state


def _head_parts(ref, h):
    return [ref[:, p * D + h * HG_DIM:p * D + (h + 1) * HG_DIM] for p in range(4)]


def hg_scan_fwd(proj, lb, gn, name, riders=None):
    t = proj.shape[0]
    nt = t // HG_TOKENS

    def body(p_ref, lb_ref, gn_ref, y_ref, s_ref, state):
        @pl.when(pl.program_id(0) == 0)
        def _():
            state[...] = jnp.zeros_like(state)

        for h in range(HG_HEADS):
            cols = slice(h * HG_DIM, (h + 1) * HG_DIM)
            s_ref[h, 0] = state[h]
            y, s1 = _hg_block_fn(HG_SUB_FWD, *_head_parts(p_ref, h), state[h], lb_ref[:, cols], gn_ref[:, cols])
            y_ref[:, cols] = y.astype(bf16)
            state[h] = s1

    return _call(
        body, name=name, grid=(nt,),
        in_specs=[pl.BlockSpec((HG_TOKENS, 4 * D), lambda i: (i, 0)), _row(D), _row(D)],
        out_specs=[pl.BlockSpec((HG_TOKENS, D), lambda i: (i, 0)),
                   pl.BlockSpec((HG_HEADS, 1, HG_DIM, HG_DIM), lambda i: (0, i, 0, 0))],
        out_shape=[jax.ShapeDtypeStruct((t, D), bf16), jax.ShapeDtypeStruct((HG_HEADS, nt, HG_DIM, HG_DIM), f32)],
        scratch_shapes=[pltpu.VMEM((HG_HEADS, HG_DIM, HG_DIM), f32)],
        semantics=("arbitrary",), riders=riders,
    )(proj, lb, gn)


def hg_scan_bwd(proj, lb, gn, states, dy, name, riders=None):
    t = proj.shape[0]
    nt = t // HG_TOKENS

    def body(p_ref, lb_ref, gn_ref, s_ref, dy_ref, dp_ref, dlb_ref, dgn_ref, dstate):
        @pl.when(pl.program_id(0) == 0)
        def _():
            dstate[...] = jnp.zeros_like(dstate)
            dlb_ref[...] = jnp.zeros_like(dlb_ref)
            dgn_ref[...] = jnp.zeros_like(dgn_ref)

        for h in range(HG_HEADS):
            cols = slice(h * HG_DIM, (h + 1) * HG_DIM)
            _, vjp = jax.vjp(functools.partial(_hg_block_fn, HG_SUB_BWD), *_head_parts(p_ref, h), s_ref[h, 0],
                             lb_ref[:, cols], gn_ref[:, cols])
            grads = vjp((dy_ref[:, cols].astype(f32), dstate[h]))
            for p in range(4):
                dp_ref[:, p * D + h * HG_DIM:p * D + (h + 1) * HG_DIM] = grads[p].astype(bf16)
            dstate[h] = grads[4]
            dlb_ref[:, cols] += grads[5]
            dgn_ref[:, cols] += grads[6]

    small = jax.ShapeDtypeStruct((1, D), f32)
    return _call(
        body, name=name, grid=(nt,),
        in_specs=[pl.BlockSpec((HG_TOKENS, 4 * D), lambda i: (nt - 1 - i, 0)), _row(D), _row(D),
                  pl.BlockSpec((HG_HEADS, 1, HG_DIM, HG_DIM), lambda i: (0, nt - 1 - i, 0, 0)),
                  pl.BlockSpec((HG_TOKENS, D), lambda i: (nt - 1 - i, 0))],
        out_specs=[pl.BlockSpec((HG_TOKENS, 4 * D), lambda i: (nt - 1 - i, 0)), _row(D), _row(D)],
        out_shape=[jax.ShapeDtypeStruct((t, 4 * D), bf16), small, small],
        scratch_shapes=[pltpu.VMEM((HG_HEADS, HG_DIM, HG_DIM), f32)],
        semantics=("arbitrary",), riders=riders,
    )(proj, lb, gn, states, dy)


FFN_COLS = 1408
HALO = 8
STRIP = 16


def _ffn_specs(tm):
    nb = tm // HALO
    main_g = pl.BlockSpec((tm, FFN_COLS), lambda j, i: (i, j))
    main_v = pl.BlockSpec((tm, FFN_COLS), lambda j, i: (i, j + 2))
    halo_g = pl.BlockSpec((HALO, FFN_COLS), lambda j, i: (jnp.maximum(i * nb - 1, 0), j))
    halo_v = pl.BlockSpec((HALO, FFN_COLS), lambda j, i: (jnp.maximum(i * nb - 1, 0), j + 2))
    w_g = pl.BlockSpec((3, FFN_COLS), lambda j, i: (0, j))
    w_v = pl.BlockSpec((3, FFN_COLS), lambda j, i: (0, j + 2))
    b_g = pl.BlockSpec((1, FFN_COLS), lambda j, i: (0, j))
    b_v = pl.BlockSpec((1, FFN_COLS), lambda j, i: (0, j + 2))
    return [main_g, halo_g, main_v, halo_v, w_g, w_v, b_g, b_v]


def _strip_rows(r):
    return pl.ds(r * STRIP, STRIP) if isinstance(r, int) else pl.ds(pl.multiple_of(r * STRIP, STRIP), STRIP)


def _for_strips(nstrip, strip, reverse=False):
    if reverse:
        strip(nstrip - 1, True)
        lax.fori_loop(0, nstrip - 1, lambda k, c: (strip(nstrip - 2 - k, False), c)[1], 0)
    else:
        strip(0, True)
        lax.fori_loop(1, nstrip, lambda r, c: (strip(r, False), c)[1], 0)


SUBLANES = 8


def _rows_down(prev, cur, shift):
    row = lax.broadcasted_iota(jnp.int32, (SUBLANES, LANES), 0)
    tiles = [prev[STRIP - SUBLANES:]] + [cur[q * SUBLANES:(q + 1) * SUBLANES] for q in range(STRIP // SUBLANES)]
    turned = [pltpu.roll(x, shift, axis=0) for x in tiles]
    return jnp.concatenate([jnp.where(row < shift, turned[q], turned[q + 1]) for q in range(STRIP // SUBLANES)], axis=0)


def _rows_ahead(cur, nxt, shift):
    row = lax.broadcasted_iota(jnp.int32, (SUBLANES, LANES), 0)
    tiles = [cur[q * SUBLANES:(q + 1) * SUBLANES] for q in range(STRIP // SUBLANES)] + [nxt[:SUBLANES]]
    turned = [pltpu.roll(x, SUBLANES - shift, axis=0) for x in tiles]
    return jnp.concatenate([jnp.where(row >= SUBLANES - shift, turned[q + 1], turned[q])
                            for q in range(STRIP // SUBLANES)], axis=0)


def _conv_strip(main_ref, halo_ref, w_ref, b_ref, r, edge, cols, first_block):
    cur = main_ref[_strip_rows(r), cols].astype(f32)
    if edge:
        h = jnp.where(first_block, 0.0, halo_ref[:, cols].astype(f32))
        prev = jnp.concatenate([jnp.zeros_like(h), h], axis=0)
    else:
        prev = main_ref[_strip_rows(r - 1), cols].astype(f32)
    a1, a2 = _rows_down(prev, cur, 1), _rows_down(prev, cur, 2)
    y = b_ref[:, cols] + w_ref[0:1, cols] * a2 + w_ref[1:2, cols] * a1 + w_ref[2:3, cols] * cur
    return y, (cur, a1, a2)


def ffn_gate_fwd(a, cw, cb, name, tm=512, riders=None):
    t = a.shape[0]
    tm = _tile(t, tm, STRIP)

    def body(ag_ref, hg_ref, av_ref, hv_ref, wg_ref, wv_ref, bg_ref, bv_ref, o_ref):
        first_block = pl.program_id(1) == 0

        def strip(r, edge):
            for c in range(FFN_COLS // LANES):
                cols = pl.ds(c * LANES, LANES)
                yg, _ = _conv_strip(ag_ref, hg_ref, wg_ref, bg_ref, r, edge, cols, first_block)
                yv, _ = _conv_strip(av_ref, hv_ref, wv_ref, bv_ref, r, edge, cols, first_block)
                o_ref[_strip_rows(r), cols] = (_gelu(yg) * yv).astype(bf16)

        _for_strips(tm // STRIP, strip)

    return _call(
        body, name=name, grid=(2, t // tm), in_specs=_ffn_specs(tm),
        out_specs=[pl.BlockSpec((tm, FFN_COLS), lambda j, i: (i, j))],
        out_shape=[jax.ShapeDtypeStruct((t, FFN_HIDDEN), bf16)],
        semantics=("parallel", "arbitrary"), riders=riders,
    )(a, a, a, a, cw, cw, cb, cb)


def _conv_tile(prev, cur, w_ref, b_ref, cols):
    row = lax.broadcasted_iota(jnp.int32, (SUBLANES, LANES), 0)
    a1 = jnp.where(row < 1, pltpu.roll(prev, 1, axis=0), pltpu.roll(cur, 1, axis=0))
    a2 = jnp.where(row < 2, pltpu.roll(prev, 2, axis=0), pltpu.roll(cur, 2, axis=0))
    return b_ref[:, cols] + w_ref[0:1, cols] * a2 + w_ref[1:2, cols] * a1 + w_ref[2:3, cols] * cur


def _gate_grads(yg, yv, dh):
    cdf = 0.5 * (1.0 + lax.erf(yg * (1.0 / math.sqrt(2.0))))
    pdf = jnp.exp(-0.5 * yg * yg) * (1.0 / math.sqrt(2.0 * math.pi))
    return dh * yv * (cdf + yg * pdf), dh * (yg * cdf)


def ffn_gate_bwd(a, cw, cb, dhid, name, tm=512, riders=None):
    t = a.shape[0]
    tm = _tile(t, tm, STRIP)
    nb = tm // HALO
    last_halo = t // HALO - 1
    nstrip = tm // STRIP
    fh = FFN_HIDDEN

    def body(a_ref, ha_ref, na_ref, w_ref, b_ref, dh_ref, ndh_ref, da_ref, dw_ref, db_ref, acc, dybuf):
        first_block = pl.program_id(0) == 0
        last_block = pl.program_id(0) == pl.num_programs(0) - 1

        @pl.when(first_block)
        def _():
            acc[...] = jnp.zeros_like(acc)

        def transposed_conv(dy, ahead, cols):
            return (w_ref[2:3, cols] * dy + w_ref[1:2, cols] * _rows_ahead(dy, ahead, 1)
                    + w_ref[0:1, cols] * _rows_ahead(dy, ahead, 2)).astype(bf16)

        def strip(r, edge):
            rows = _strip_rows(r)
            for c in range(fh // LANES):
                gate, val = pl.ds(c * LANES, LANES), pl.ds(fh + c * LANES, LANES)
                yg, taps_g = _conv_strip(a_ref, ha_ref, w_ref, b_ref, r, edge, gate, first_block)
                yv, taps_v = _conv_strip(a_ref, ha_ref, w_ref, b_ref, r, edge, val, first_block)
                dyg, dyv = _gate_grads(yg, yv, dh_ref[rows, gate].astype(f32))
                for p, (dy, (a0, a1, a2), cols) in enumerate(((dyg, taps_g, gate), (dyv, taps_v, val))):
                    acc[0, :, cols] += dy * a2
                    acc[1, :, cols] += dy * a1
                    acc[2, :, cols] += dy * a0
                    acc[3, :, cols] += dy
                    if not edge:
                        da_ref[_strip_rows(r - 1), cols] = transposed_conv(dybuf[:, cols], dy, cols)
                    dybuf[:, cols] = dy

        _for_strips(nstrip, strip)

        last_rows = pl.ds((nstrip - 1) * STRIP + SUBLANES, SUBLANES)
        for c in range(fh // LANES):
            gate, val = pl.ds(c * LANES, LANES), pl.ds(fh + c * LANES, LANES)
            yg = _conv_tile(a_ref[last_rows, gate].astype(f32), na_ref[:, gate].astype(f32), w_ref, b_ref, gate)
            yv = _conv_tile(a_ref[last_rows, val].astype(f32), na_ref[:, val].astype(f32), w_ref, b_ref, val)
            for dy, cols in zip(_gate_grads(yg, yv, ndh_ref[:, gate].astype(f32)), (gate, val)):
                dy = jnp.where(last_block, 0.0, dy)
                ahead = jnp.concatenate([dy, jnp.zeros_like(dy)], axis=0)
                da_ref[_strip_rows(nstrip - 1), cols] = transposed_conv(dybuf[:, cols], ahead, cols)

        @pl.when(last_block)
        def _():
            for tap in range(3):
                dw_ref[tap:tap + 1, :] = jnp.sum(acc[tap], axis=0, keepdims=True)
            db_ref[...] = jnp.sum(acc[3], axis=0, keepdims=True)

    def after(i):
        return jnp.minimum((i + 1) * nb, last_halo)

    return _call(
        body, name=name, grid=(t // tm,),
        in_specs=[pl.BlockSpec((tm, 2 * fh), lambda i: (i, 0)),
                  pl.BlockSpec((HALO, 2 * fh), lambda i: (jnp.maximum(i * nb - 1, 0), 0)),
                  pl.BlockSpec((HALO, 2 * fh), lambda i: (after(i), 0)),
                  pl.BlockSpec((3, 2 * fh), lambda i: (0, 0)), pl.BlockSpec((1, 2 * fh), lambda i: (0, 0)),
                  pl.BlockSpec((tm, fh), lambda i: (i, 0)), pl.BlockSpec((HALO, fh), lambda i: (after(i), 0))],
        out_specs=[pl.BlockSpec((tm, 2 * fh), lambda i: (i, 0)), pl.BlockSpec((3, 2 * fh), lambda i: (0, 0)),
                   pl.BlockSpec((1, 2 * fh), lambda i: (0, 0))],
        out_shape=[jax.ShapeDtypeStruct((t, 2 * fh), bf16), jax.ShapeDtypeStruct((3, 2 * fh), f32),
                   jax.ShapeDtypeStruct((1, 2 * fh), f32)],
        scratch_shapes=[pltpu.VMEM((4, STRIP, 2 * fh), f32), pltpu.VMEM((STRIP, 2 * fh), f32)],
        semantics=("arbitrary",), riders=riders,
    )(a, a, a, cw, cb, dhid, dhid)


def ada_mod(c_all, ada_w, ada_b_cols, name):
    cols = ada_w.shape[2]

    def body(c_ref, w_ref, b_ref, o_ref):
        c = c_ref[...]
        cond = (c * jax.nn.sigmoid(c)).astype(bf16)
        o_ref[0] = jnp.dot(cond, w_ref[0].astype(bf16), preferred_element_type=f32) + b_ref[0]

    return pl.pallas_call(
        body, name=name, grid=(DEPTH,),
        in_specs=[pl.BlockSpec((N_DEV, D), lambda i: (0, 0)), pl.BlockSpec((1, D, cols), lambda i: (i, 0, 0)),
                  pl.BlockSpec((1, 1, cols), lambda i: (i, 0, 0))],
        out_specs=pl.BlockSpec((1, N_DEV, cols), lambda i: (i, 0, 0)),
        out_shape=jax.ShapeDtypeStruct((DEPTH, N_DEV, cols), f32), compiler_params=_cparams("parallel"),
    )(c_all, ada_w, ada_b_cols)


def ada_grads(c_all, dmod_cols, dmod_all, name):
    cols = dmod_cols.shape[2]

    def body(c_ref, dm_ref, da_ref, dw_ref, db_ref):
        c = c_ref[...]
        cond = c * jax.nn.sigmoid(c)
        dw_ref[0] = lax.dot_general(cond, dm_ref[0], (((0,), (0,)), ((), ())), precision=lax.Precision.HIGHEST,
                                    preferred_element_type=f32)
        acc = da_ref[0, 0]
        for e in range(1, N_DEV):
            acc = acc + da_ref[e, 0]
        db_ref[0] = acc

    return pl.pallas_call(
        body, name=name, grid=(DEPTH,),
        in_specs=[pl.BlockSpec((N_DEV, D), lambda i: (0, 0)), pl.BlockSpec((1, N_DEV, cols), lambda i: (i, 0, 0)),
                  pl.BlockSpec((N_DEV, 1, 1, 6 * D), lambda i: (0, i, 0, 0))],
        out_specs=(pl.BlockSpec((1, D, cols), lambda i: (i, 0, 0)), pl.BlockSpec((1, 1, 6 * D), lambda i: (i, 0, 0))),
        out_shape=(jax.ShapeDtypeStruct((DEPTH, D, cols), f32), jax.ShapeDtypeStruct((DEPTH, 1, 6 * D), f32)),
        compiler_params=_cparams("parallel"),
    )(c_all, dmod_cols, dmod_all)


def lower_bound_fwd(hg_lb, name):
    n = hg_lb.shape[1]

    def body(l_ref, o_ref):
        o_ref[...] = jax.nn.sigmoid(l_ref[1:2, :] - l_ref[0:1, :])

    return pl.pallas_call(body, name=name, out_shape=jax.ShapeDtypeStruct((1, n), f32))(hg_lb)


def lower_bound_bwd(hg_lb, dlb, name):
    n = hg_lb.shape[1]

    def body(l_ref, d_ref, o_ref):
        p = jax.nn.sigmoid(l_ref[1:2, :] - l_ref[0:1, :])
        g = d_ref[...] * p * (1.0 - p)
        o_ref[0:1, :] = -g
        o_ref[1:2, :] = g

    return pl.pallas_call(body, name=name, out_shape=jax.ShapeDtypeStruct((2, n), f32))(hg_lb, dlb)


def _adamw(w, g, m, v):
    m = ADAM_B1 * m + (1.0 - ADAM_B1) * g
    v = ADAM_B2 * v + (1.0 - ADAM_B2) * jnp.square(g)
    m_hat = m / (1.0 - ADAM_B1 ** ADAM_STEP)
    v_hat = v / (1.0 - ADAM_B2 ** ADAM_STEP)
    delta = -ADAM_LR * (m_hat / (jnp.sqrt(v_hat) + ADAM_EPS) + ADAM_WD * w)
    return delta, m, v


ADAM_BLOCK_BYTES = 32 * 1024 * 1024


def adam_reduced(parts, w, m, v, name):
    layers, r, c = w.shape
    outs = None
    for layer in range(layers):
        outs = _adam_layer(parts[layer], w, m, v, layer, outs, f"{name}_{layer}")
    return outs


def _adam_layer(parts, w, m, v, layer, prev, name):
    layers, r, c = w.shape
    rows = parts[0].shape[1]
    assert all(p.shape == (N_DEV, rows, c) for p in parts) and rows * len(parts) == r
    row_bytes = 2 * (len(parts) * N_DEV * c * 2 + 7 * c * 4)
    tr = _tile(rows, max(16, ADAM_BLOCK_BYTES // row_bytes), 16)
    steps = rows // tr
    n_prev = 0 if prev is None else 4

    def body(*refs):
        p_refs = refs[:len(parts)]
        w_ref, m_ref, v_ref = refs[len(parts):len(parts) + 3]
        g_ref, d_ref, mo_ref, vo_ref = refs[len(parts) + 3 + n_prev:]
        for idx in range(len(parts)):
            @pl.when(pl.program_id(0) == idx)
            def _():
                g = p_refs[idx][0].astype(f32)
                for j in range(1, N_DEV):
                    g = g + p_refs[idx][j].astype(f32)
                g_ref[...] = g
                d_ref[...], mo_ref[...], vo_ref[...] = _adamw(w_ref[...], g, m_ref[...], v_ref[...])

    def part_spec(idx):
        return pl.BlockSpec((N_DEV, tr, c), lambda p, i: (0, jnp.where(p == idx, i, 0), 0))

    blk = pl.BlockSpec((None, tr, c), lambda p, i: (layer, p * steps + i, 0))
    out = jax.ShapeDtypeStruct((layers, r, c), f32)
    n_in = len(parts) + 3
    return pl.pallas_call(
        body, name=name, grid=(len(parts), steps),
        in_specs=[part_spec(idx) for idx in range(len(parts))] + [blk, blk, blk] + [ANY] * n_prev,
        out_specs=(blk, blk, blk, blk), out_shape=(out, out, out, out),
        input_output_aliases={n_in + k: k for k in range(n_prev)},
        compiler_params=_cparams("arbitrary", "arbitrary"),
    )(*parts, w, m, v, *(prev or ()))


def adam_plain(g, w, m, v, name, tr=256):
    r, c = w.shape
    tr = _tile(r, tr, 8)

    def body(g_ref, w_ref, m_ref, v_ref, d_ref, mo_ref, vo_ref):
        d_ref[...], mo_ref[...], vo_ref[...] = _adamw(w_ref[...], g_ref[...], m_ref[...], v_ref[...])

    blk = pl.BlockSpec((tr, c), lambda i: (i, 0))
    out = jax.ShapeDtypeStruct((r, c), f32)
    return pl.pallas_call(
        body, name=name, grid=(r // tr,), in_specs=[blk, blk, blk, blk], out_specs=(blk, blk, blk),
        out_shape=(out, out, out), compiler_params=_cparams("parallel"),
    )(g, w, m, v)


def sum_parts(parts, name):
    _, r, c = parts.shape

    def body(p_ref, o_ref):
        acc = p_ref[0]
        for j in range(1, N_DEV):
            acc = acc + p_ref[j]
        o_ref[...] = acc

    return pl.pallas_call(body, name=name, out_shape=jax.ShapeDtypeStruct((r, c), f32))(parts)


def _pack(arrs, rows_mult=8):
    flat = jnp.concatenate([a.reshape(-1) for a in arrs])
    rows = -(-flat.shape[0] // LANES)
    rows = -(-rows // rows_mult) * rows_mult
    return jnp.pad(flat, (0, rows * LANES - flat.shape[0])).reshape(rows, LANES)


def _unpack(flat, shapes):
    out, at = [], 0
    for s in shapes:
        n = math.prod(s)
        out.append(flat[at:at + n].reshape(s))
        at += n
    return out


def kernel(x, c, gm_w_in, gm_ln_g, gm_ln_b, gm_w_s, gm_b_s, gm_w_out, hg_w_in, hg_lb, hg_gn_g, hg_w_out, ffn_w_up, ffn_conv_w, ffn_conv_b, ffn_w_down, norm_g, ada_w, ada_b, final_g, loss_target, m_gm_w_in, m_gm_ln_g, m_gm_ln_b, m_gm_w_s, m_gm_b_s, m_gm_w_out, m_hg_w_in, m_hg_lb, m_hg_gn_g, m_hg_w_out, m_ffn_w_up, m_ffn_conv_w, m_ffn_conv_b, m_ffn_w_down, m_norm_g, m_ada_w, m_ada_b, m_final_g, v_gm_w_in, v_gm_ln_g, v_gm_ln_b, v_gm_w_s, v_gm_b_s, v_gm_w_out, v_hg_w_in, v_hg_lb, v_hg_gn_g, v_hg_w_out, v_ffn_w_up, v_ffn_conv_w, v_ffn_conv_b, v_ffn_w_down, v_norm_g, v_ada_w, v_ada_b, v_final_g):
    me = _flat(_mesh_pos())
    xt = x[0]
    t = xt.shape[0]

    small_shapes = [(1, D), (2, HG_DIM), (2, HG_DIM), (DEPTH, 2, HG_DIM), (DEPTH, 3, 2 * FFN_HIDDEN // N_DEV)]
    w_in_0 = gm_w_in[0].astype(bf16)
    small_all, w_in_0_all = all_gather([_pack([c, hg_lb, hg_gn_g, norm_g, ffn_conv_w]), w_in_0], "gather_first", relay=True)
    small_all = small_all.reshape(N_DEV, -1)
    at = 0
    pieces = []
    for s in small_shapes:
        n = math.prod(s)
        pieces.append(small_all[:, at:at + n].reshape((N_DEV,) + s))
        at += n
    c_all = pieces[0].reshape(N_DEV, D)
    hg_lb_full = jnp.transpose(pieces[1], (1, 0, 2)).reshape(2, D)
    hg_gn_full = jnp.transpose(pieces[2], (1, 0, 2)).reshape(2, D)
    norm_g_full = jnp.transpose(pieces[3], (1, 2, 0, 3)).reshape(DEPTH, 2, D)
    conv_w_full = jnp.transpose(pieces[4], (1, 2, 0, 3)).reshape(DEPTH, 3, 2 * FFN_HIDDEN)

    lb1 = lower_bound_fwd(hg_lb_full, "lower_bound")
    lbs = [jnp.zeros((1, D), f32), lb1]

    ada_b_cols = lax.dynamic_slice(ada_b, (0, me * ADA_COLS), (DEPTH, ADA_COLS)).reshape(DEPTH, 1, ADA_COLS)
    mod_cols = ada_mod(c_all, ada_w, ada_b_cols, "ada_mod")
    (mod_mine,) = all_to_all([jnp.transpose(mod_cols, (1, 0, 2))], "mod_to_examples")
    mod = jnp.transpose(mod_mine, (1, 0, 2)).reshape(DEPTH, 6, 1, D)

    def layer_shards(i):
        j = i // 2
        w_in, w_out = (gm_w_in, gm_w_out) if i % 2 == 0 else (hg_w_in, hg_w_out)
        return [w_in[j].astype(bf16), w_out[j].astype(bf16), ffn_w_up[i].T.astype(bf16), ffn_w_down[i].astype(bf16)]

    def full_rows(g):
        return g.reshape(N_DEV * g.shape[1], g.shape[2])

    carried_by = {
        "in_0": [(0, 1), (0, 3)], "mix_0": [(0, 2)], "up_0": [(1, 0), (1, 1)], "gate_0": [(1, 2)], "down_0": [(1, 3)],
        "in_1": [(2, 0)], "mix_1": [(2, 1), (2, 2), (2, 3)], "up_1": [(3, 0), (3, 1)], "gate_1": [(3, 2)], "down_1": [(3, 3)],
    }
    shards = [layer_shards(i) for i in range(DEPTH)]
    gathered = {}
    gathered[(0, 0)] = w_in_0_all

    def carry(call, site, **kw):
        items = carried_by.get(site, [])
        outs = call(riders=Riders([shards[l][slot] for l, slot in items], True), **kw)
        for item, g in zip(items, outs[len(outs) - len(items):]):
            gathered[item] = g
        return outs[:len(outs) - len(items)]

    saved = []
    weights = []
    xcur = xt
    h = norm_fwd(xcur, norm_g_full[0, 0:1], mod[0, 1], mod[0, 0], "norm1_0")
    for i in range(DEPTH):
        j = i // 2
        sh1, sc1, g1, sh2, sc2, g2 = [mod[i, p] for p in range(6)]
        gn2 = norm_g_full[i, 1:2]
        s = {"x0": xcur, "h": h}
        w_in = gathered[(i, 0)]
        if i % 2 == 0:
            (z,) = carry(functools.partial(mm_nn, h, w_in, bf16, f"gm_in_{i}"), f"in_{i}")
            bs = gm_b_s[j].reshape(GM_HEADS, GM_BLOCK, 1)
            (mixed,) = carry(functools.partial(gm_mix_fwd, z, gm_w_s[j], bs, gm_ln_g[j:j + 1], gm_ln_b[j:j + 1],
                                               f"gm_mix_{i}"), f"mix_{i}")
            s["z"] = z
        else:
            (proj,) = carry(functools.partial(mm_nn, h, w_in, f32, f"hg_in_{i}"), f"in_{i}")
            mixed, states = carry(functools.partial(hg_scan_fwd, proj, lbs[j], hg_gn_full[j:j + 1], f"hg_scan_{i}"),
                                  f"mix_{i}")
            s["proj"], s["states"] = proj, states
        s["mixed"] = mixed
        w_out = full_rows(gathered[(i, 1)])
        y, x1, h2 = carry(functools.partial(mm_nn_residual, mixed, w_out, xcur, g1, (gn2, sc2, sh2), f"mix_out_{i}"),
                          f"out_{i}")
        s["y"], s["x1"] = y, x1
        w_up = gathered[(i, 2)]
        (a,) = carry(functools.partial(mm_nn, h2, w_up, bf16, f"ffn_up_{i}", transposed=True), f"up_{i}")
        (hid,) = carry(functools.partial(ffn_gate_fwd, a, conv_w_full[i], ffn_conv_b[i:i + 1], f"ffn_gate_{i}"), f"gate_{i}")
        w_down = full_rows(gathered[(i, 3)])
        next_norm = (norm_g_full[i + 1, 0:1], mod[i + 1, 1], mod[i + 1, 0]) if i + 1 < DEPTH else None
        outs = carry(functools.partial(mm_nn_residual, hid, w_down, x1, g2, next_norm, f"ffn_down_{i}"), f"down_{i}")
        fo, x2 = outs[0], outs[1]
        s["h2"], s["a"], s["hid"], s["f"] = h2, a, hid, fo
        weights.append((w_in, w_out, w_up, w_down))
        saved.append(s)
        xcur = x2
        h = outs[2] if next_norm is not None else None

    loss_part, dx, d_final_g, dg2, df = loss_head(xcur, final_g.reshape(1, D), loss_target[0], saved[-1]["f"],
                                                  mod[DEPTH - 1, 5], "loss_head")
    loss = lax.psum(loss_part[0, 0], ("x", "y", "c"))

    def halves(blocked):
        rows = blocked.shape[1] // 2
        return [(blocked, (0, rows)), (blocked, (rows, rows))]

    def by_owner_rows(dw):
        k, n = dw.shape
        return dw.reshape(N_DEV, k // N_DEV, n)

    received = [[[] for _ in range(4)] for _ in range(DEPTH)]

    def send(call, items, **kw):
        outs = call(riders=Riders([arr for arr, _ in items], False), **kw)
        for (_, (layer, slot)), got in zip(items, outs[len(outs) - len(items):]):
            received[layer][slot].append(got)
        return outs[:len(outs) - len(items)]

    dmod = [None] * DEPTH
    d_norm_g = [None] * DEPTH
    d_gm = {k: [None, None] for k in ("ws", "bs", "lng", "lnb")}
    d_hg = {k: [None, None] for k in ("lb", "gn")}
    d_ffn = {k: [None] * DEPTH for k in ("cw", "cb")}
    in_halves = []
    for i in reversed(range(DEPTH)):
        j = i // 2
        s = saved[i]
        w_in, w_out, w_up, w_down = weights[i]
        sh1, sc1, g1, sh2, sc2, g2 = [mod[i, p] for p in range(6)]
        gn1, gn2 = norm_g_full[i, 0:1], norm_g_full[i, 1:2]
        (dw_down,) = mm_tn(s["hid"], df, bf16, f"dw_down_{i}", tn=D)
        scan_carries = i % 2 == 1
        down_item = (by_owner_rows(dw_down), (i, 3))
        (dhid,) = mm_nt(df, w_down, bf16, f"dhid_{i}")
        da, d_ffn["cw"][i], d_ffn["cb"][i] = send(
            functools.partial(ffn_gate_bwd, s["a"], conv_w_full[i], ffn_conv_b[i:i + 1], dhid, f"ffn_gate_bwd_{i}"),
            in_halves)
        (dw_up_t,) = send(functools.partial(mm_tn, da, s["h2"], bf16, f"dw_up_{i}", tn=D), [] if scan_carries else [down_item])
        dw_up_t = dw_up_t.reshape(N_DEV, -1, D)
        up_halves = [(dw_up_t, (i, 2))] if scan_carries else [(part, (i, 2)) for part in halves(dw_up_t)]
        (dh2,) = send(functools.partial(mm_nt, da, w_up, bf16, f"dh2_{i}", transposed=True),
                      [] if scan_carries else up_halves[:1])
        dx1, dgn2, dsc2, dsh2, dg1, dy = norm_bwd(s["x1"], gn2, sc2, sh2, dh2, dx, f"norm2_bwd_{i}", gate=(s["y"], g1))
        (dw_out,) = mm_tn(s["mixed"], dy, bf16, f"dw_mix_out_{i}")
        (dmixed,) = mm_nt(dy, w_out, bf16, f"dmixed_{i}")
        if i % 2 == 0:
            bs = gm_b_s[j].reshape(GM_HEADS, GM_BLOCK, 1)
            dpre, dws, dbs, dlng, dlnb = send(
                functools.partial(gm_mix_bwd, s["z"], gm_w_s[j], bs, gm_ln_g[j:j + 1], gm_ln_b[j:j + 1], dmixed,
                                  f"gm_mix_bwd_{i}"), up_halves[1:])
            d_gm["ws"][j], d_gm["bs"][j], d_gm["lng"][j], d_gm["lnb"][j] = dws, dbs.reshape(GM_HEADS, GM_BLOCK), dlng, dlnb
        else:
            dpre, dlb, dgn = send(
                functools.partial(hg_scan_bwd, s["proj"], lbs[j], hg_gn_full[j:j + 1], s["states"], dmixed,
                                  f"hg_scan_bwd_{i}"), [down_item] + up_halves)
            d_hg["lb"][j], d_hg["gn"][j] = dlb, dgn
        (dw_in,) = send(functools.partial(mm_tn_by_owner, s["h"], dpre, f"dw_mix_in_{i}", tm=D), [(by_owner_rows(dw_out), (i, 1))])
        in_halves = [(part, (i, 0)) for part in halves(dw_in)]
        (dh,) = send(functools.partial(mm_nt, dpre, w_in, bf16, f"dh_mix_{i}"), in_halves[:1] if i == 0 else [])
        dmod_i = [None, None, dg1, dsh2, dsc2, dg2]
        if i > 0:
            dx, dgn1, dsc1, dsh1, dg2, df = norm_bwd(s["x0"], gn1, sc1, sh1, dh, dx1, f"norm1_bwd_{i}",
                                                     gate=(saved[i - 1]["f"], mod[i - 1, 5]))
        else:
            dx, dgn1, dsc1, dsh1 = send(functools.partial(norm_bwd, s["x0"], gn1, sc1, sh1, dh, dx1, f"norm1_bwd_{i}"),
                                        in_halves[1:])
        dmod_i[0], dmod_i[1] = dsh1, dsc1
        dmod[i] = jnp.concatenate(dmod_i, axis=1)
        d_norm_g[i] = jnp.concatenate([dgn1, dgn2], axis=0)
    grad_x = dx.reshape(1, t, D)

    (dmod_all,) = all_gather([jnp.concatenate(dmod, axis=0)], "gather_dmod")
    dmod_cols = jnp.transpose(lax.dynamic_slice(dmod_all, (0, 0, me * ADA_COLS), (N_DEV, DEPTH, ADA_COLS)), (1, 0, 2))
    g_ada_w, g_ada_b = ada_grads(c_all, dmod_cols, dmod_all.reshape(N_DEV, DEPTH, 1, 6 * D), "ada_grads")
    g_ada_b = g_ada_b.reshape(DEPTH, 6 * D)

    small_partials = [jnp.concatenate(d_gm["lng"], axis=0), jnp.concatenate(d_gm["lnb"], axis=0),
                      jnp.stack(d_gm["ws"]), jnp.stack(d_gm["bs"]), jnp.concatenate(d_ffn["cb"], axis=0),
                      d_final_g, d_hg["lb"][1], jnp.concatenate(d_hg["gn"], axis=0), jnp.stack(d_norm_g),
                      jnp.stack(d_ffn["cw"])]
    partial_shapes = [p.shape for p in small_partials]
    packed = _pack(small_partials, rows_mult=8 * N_DEV)
    rows = packed.shape[0] // N_DEV
    (recv,) = all_to_all([packed.reshape(N_DEV, rows, LANES)], "small_grads_exchange")
    (summed,) = all_gather([sum_parts(recv, "small_grads_sum")], "small_grads_gather")

    def parts_of(slot, layers):
        return [received[i][slot] for i in layers]

    def swapped(a):
        return jnp.swapaxes(a, 1, 2)

    w_shards = [gm_w_in, gm_w_out, hg_w_in, hg_w_out, swapped(ffn_w_up), ffn_w_down]
    big_parts = [parts_of(0, (0, 2)), parts_of(1, (0, 2)), parts_of(0, (1, 3)), parts_of(1, (1, 3)),
                 parts_of(2, range(DEPTH)), parts_of(3, range(DEPTH))]
    big_m = [m_gm_w_in, m_gm_w_out, m_hg_w_in, m_hg_w_out, swapped(m_ffn_w_up), m_ffn_w_down]
    big_v = [v_gm_w_in, v_gm_w_out, v_hg_w_in, v_hg_w_out, swapped(v_ffn_w_up), v_ffn_w_down]
    big = [adam_reduced(parts, w, m_, v_, f"adam_big_{idx}")
           for idx, (w, m_, v_, parts) in enumerate(zip(w_shards, big_m, big_v, big_parts))]
    big[4] = [swapped(o) for o in big[4]]
    (g_gm_w_in, d_gm_w_in, nm_gm_w_in, nv_gm_w_in), (g_gm_w_out, d_gm_w_out, nm_gm_w_out, nv_gm_w_out), \
        (g_hg_w_in, d_hg_w_in, nm_hg_w_in, nv_hg_w_in), (g_hg_w_out, d_hg_w_out, nm_hg_w_out, nv_hg_w_out), \
        (g_ffn_w_up, d_ffn_w_up, nm_ffn_w_up, nv_ffn_w_up), (g_ffn_w_down, d_ffn_w_down, nm_ffn_w_down, nv_ffn_w_down) = big

    g_ln_g, g_ln_b, g_ws, g_bs, g_cb, g_final, g_lb1, g_gn, g_norm, g_cw = _unpack(summed.reshape(-1), partial_shapes)
    g_final = g_final.reshape(D)

    def my_cols(a, n):
        start = (0,) * (a.ndim - 1) + (me * n,)
        return lax.dynamic_slice(a, start, a.shape[:-1] + (n,))

    g_hg_lb = lower_bound_bwd(hg_lb, my_cols(g_lb1, HG_DIM), "lower_bound_bwd")
    g_hg_gn = my_cols(g_gn, HG_DIM)
    g_norm_g = my_cols(g_norm, HG_DIM)
    g_conv_w = my_cols(g_cw, 2 * FFN_HIDDEN // N_DEV)

    two_d = (-1, ADA_COLS)
    d_ada_w, nm_ada_w, nv_ada_w = [o.reshape(ada_w.shape) for o in adam_plain(
        g_ada_w.reshape(two_d), ada_w.reshape(two_d), m_ada_w.reshape(two_d), v_ada_w.reshape(two_d), "adam_ada_w")]

    small_g = [g_ln_g, g_ln_b, g_ws, g_bs, g_cb, g_ada_b, g_final, g_hg_lb, g_hg_gn, g_norm_g, g_conv_w]
    small_w = [gm_ln_g, gm_ln_b, gm_w_s, gm_b_s, ffn_conv_b, ada_b, final_g, hg_lb, hg_gn_g, norm_g, ffn_conv_w]
    small_m = [m_gm_ln_g, m_gm_ln_b, m_gm_w_s, m_gm_b_s, m_ffn_conv_b, m_ada_b, m_final_g, m_hg_lb, m_hg_gn_g, m_norm_g, m_ffn_conv_w]
    small_v = [v_gm_ln_g, v_gm_ln_b, v_gm_w_s, v_gm_b_s, v_ffn_conv_b, v_ada_b, v_final_g, v_hg_lb, v_hg_gn_g, v_norm_g, v_ffn_conv_w]
    shapes = [w.shape for w in small_w]
    small_g = [g.reshape(s) for g, s in zip(small_g, shapes)]
    outs = adam_plain(_pack(small_g), _pack(small_w), _pack(small_m), _pack(small_v), "adam_small")
    (d_ln_g, d_ln_b, d_ws, d_bs, d_cb, d_ada_b, d_final, d_hg_lb, d_hg_gn, d_norm_g_, d_conv_w), \
        (nm_ln_g, nm_ln_b, nm_ws, nm_bs, nm_cb, nm_ada_b, nm_final, nm_hg_lb, nm_hg_gn, nm_norm_g, nm_conv_w), \
        (nv_ln_g, nv_ln_b, nv_ws, nv_bs, nv_cb, nv_ada_b, nv_final, nv_hg_lb, nv_hg_gn, nv_norm_g, nv_conv_w) = [
            _unpack(o.reshape(-1), shapes) for o in outs]
    g_ln_g, g_ln_b, g_ws, g_bs, g_cb, g_ada_b, g_final, g_hg_lb, g_hg_gn, g_norm_g, g_conv_w = small_g

    grads = (g_gm_w_in, g_ln_g, g_ln_b, g_ws, g_bs, g_gm_w_out, g_hg_w_in, g_hg_lb, g_hg_gn, g_hg_w_out,
             g_ffn_w_up, g_conv_w, g_cb, g_ffn_w_down, g_norm_g, g_ada_w, g_ada_b, g_final)
    deltas = (d_gm_w_in, d_ln_g, d_ln_b, d_ws, d_bs, d_gm_w_out, d_hg_w_in, d_hg_lb, d_hg_gn, d_hg_w_out,
              d_ffn_w_up, d_conv_w, d_cb, d_ffn_w_down, d_norm_g_, d_ada_w, d_ada_b, d_final)
    new_m = (nm_gm_w_in, nm_ln_g, nm_ln_b, nm_ws, nm_bs, nm_gm_w_out, nm_hg_w_in, nm_hg_lb, nm_hg_gn, nm_hg_w_out,
             nm_ffn_w_up, nm_conv_w, nm_cb, nm_ffn_w_down, nm_norm_g, nm_ada_w, nm_ada_b, nm_final)
    new_v = (nv_gm_w_in, nv_ln_g, nv_ln_b, nv_ws, nv_bs, nv_gm_w_out, nv_hg_w_in, nv_hg_lb, nv_hg_gn, nv_hg_w_out,
             nv_ffn_w_up, nv_conv_w, nv_cb, nv_ffn_w_down, nv_norm_g, nv_ada_w, nv_ada_b, nv_final)
    return (loss, grad_x) + grads + deltas + new_m + new_v
```

```python
import functools
import math

import jax
import jax.numpy as jnp
from jax import lax
from jax.experimental import pallas as pl
from jax.experimental.pallas import tpu as pltpu

f32 = jnp.float32
bf16 = jnp.bfloat16
MESH = pl.DeviceIdType.MESH

N_DEV = 8
D = 1024
DEPTH = 4
EPS = 1e-6
GM_WIDTH = 2048
GM_HEADS = 8
GM_HEAD_DIM = 256
GM_BLOCK = 128
CHUNK = 64
HG_HEADS = 8
HG_DIM = 128
FFN_HIDDEN = 2816
ADA_COLS = 6 * D // N_DEV

HG_SUB_FWD = 64
HG_SUB_BWD = 32
HG_PAIR = 8
HG_TOKENS = 128

ADAM_LR = 0.001
ADAM_B1 = 0.9
ADAM_B2 = 0.999
ADAM_EPS = 1e-08
ADAM_WD = 0.01
ADAM_STEP = 10

V7X_VMEM_LIMIT = 56 * 1024 * 1024
LANES = 128


def _cparams(*sem):
    return pltpu.CompilerParams(dimension_semantics=sem or None, vmem_limit_bytes=V7X_VMEM_LIMIT)


def _tile(n, target, mult=LANES):
    best = None
    for t in range(mult, min(n, target) + 1, mult):
        if n % t == 0:
            best = t
    return best or n


WEIGHT_BLOCK_BYTES = 6 * 1024 * 1024


def _weight_tile(n, k):
    return _tile(n, max(LANES, WEIGHT_BLOCK_BYTES // (2 * k)))


def _gelu(x):
    return 0.5 * x * (1.0 + lax.erf(x * (1.0 / math.sqrt(2.0))))


def _mesh_pos():
    return lax.axis_index("x"), lax.axis_index("y"), lax.axis_index("c")


def _flat(pos):
    return 4 * pos[0] + 2 * pos[1] + pos[2]


def _peer(pos, k):
    return ((1 - pos[0]) if k & 4 else pos[0], (1 - pos[1]) if k & 2 else pos[1], (1 - pos[2]) if k & 1 else pos[2])


def _exchange_copies(ins, outs, send_sems, recv_sems, local_sems, gather):
    pos = _mesh_pos()
    me = _flat(pos)

    def src(i, dest):
        if gather:
            return ins[i]
        ref, rows = ins[i] if isinstance(ins[i], tuple) else (ins[i], None)
        return ref.at[dest] if rows is None else ref.at[dest, pl.ds(*rows)]

    local = [pltpu.make_async_copy(src(i, me), outs[i].at[me], local_sems.at[i]) for i in range(len(ins))]
    sends, recvs = [], []
    for k in range(1, N_DEV):
        peer = _peer(pos, k)
        there = _flat(peer)
        for i in range(len(ins)):
            sems = dict(send_sem=send_sems.at[i * 7 + k - 1], recv_sem=recv_sems.at[i * 7 + k - 1],
                        device_id=peer, device_id_type=MESH)
            sends.append(pltpu.make_async_remote_copy(src_ref=src(i, there), dst_ref=outs[i].at[me], **sems))
            recvs.append(pltpu.make_async_remote_copy(src_ref=src(i, there), dst_ref=outs[i].at[there], **sems))
    return local, sends, recvs


def _exchange_start(*refs):
    local, sends, _ = _exchange_copies(*refs)
    for cp in local + sends:
        cp.start()


def _exchange_wait(*refs):
    local, sends, recvs = _exchange_copies(*refs)
    for cp in recvs:
        cp.wait_recv()
    for cp in sends:
        cp.wait_send()
    for cp in local:
        cp.wait()


OTHER_CHIPS = (2, 4, 6)


def _relay_copies(ins, outs, send_sems, recv_sems, local_sems):
    pos = _mesh_pos()
    me = _flat(pos)
    sibling = _peer(pos, 1)
    local = [pltpu.make_async_copy(ins[i], outs[i].at[me], local_sems.at[i]) for i in range(len(ins))]
    first, passes, recvs = [], {k: [] for k in OTHER_CHIPS}, {k: [] for k in range(1, N_DEV)}
    for i in range(len(ins)):
        def copy(k, src, block, to):
            return pltpu.make_async_remote_copy(
                src_ref=src, dst_ref=outs[i].at[block], send_sem=send_sems.at[i * 7 + k - 1],
                recv_sem=recv_sems.at[i * 7 + k - 1], device_id=to, device_id_type=MESH)

        for k in (1,) + OTHER_CHIPS:
            first.append(copy(k, ins[i], me, _peer(pos, k)))
        for k in OTHER_CHIPS:
            there = _flat(_peer(pos, k))
            passes[k].append(copy(k ^ 1, outs[i].at[there], there, sibling))
        for k in range(1, N_DEV):
            there = _flat(_peer(pos, k))
            recvs[k].append(copy(k, ins[i], there, _peer(pos, k)))
    return local, first, passes, recvs


def _relay_start(ins, outs, *sems):
    local, first, _, _ = _relay_copies(ins, outs, *sems)
    for cp in local + first:
        cp.start()


def _relay_wait(ins, outs, *sems):
    local, first, passes, recvs = _relay_copies(ins, outs, *sems)
    for k in OTHER_CHIPS:
        for cp in recvs[k]:
            cp.wait_recv()
        for cp in passes[k]:
            cp.start()
    for k in (1, 3, 5, 7):
        for cp in recvs[k]:
            cp.wait_recv()
    for cp in first + [cp for k in OTHER_CHIPS for cp in passes[k]]:
        cp.wait_send()
    for cp in local:
        cp.wait()


def _exchange_out_shape(a, gather):
    return jax.ShapeDtypeStruct((N_DEV,) + tuple(a.shape) if gather else tuple(a.shape), a.dtype)


def _exchange_sems(n):
    return [pltpu.SemaphoreType.DMA((7 * n,)), pltpu.SemaphoreType.DMA((7 * n,)), pltpu.SemaphoreType.DMA((n,))]


ANY = pl.BlockSpec(memory_space=pl.ANY)


def _exchange(arrs, gather, name, relay=False):
    n = len(arrs)

    def body(*refs):
        ins, outs = refs[:n], refs[n:2 * n]
        if relay:
            _relay_start(ins, outs, *refs[2 * n:])
            _relay_wait(ins, outs, *refs[2 * n:])
        else:
            _exchange_start(ins, outs, *refs[2 * n:], gather)
            _exchange_wait(ins, outs, *refs[2 * n:], gather)

    return pl.pallas_call(
        body, name=name, out_shape=tuple(_exchange_out_shape(a, gather) for a in arrs),
        in_specs=[ANY] * n, out_specs=tuple([ANY] * n), scratch_shapes=_exchange_sems(n),
    )(*arrs)


def all_gather(arrs, name, relay=False):
    return _exchange(arrs, True, name, relay)


def all_to_all(arrs, name):
    return _exchange(arrs, False, name)


class Riders:
    def __init__(self, arrs, gather):
        self.gather = gather
        self.rows = [a[1] if isinstance(a, tuple) else None for a in arrs]
        self.arrs = [a[0] if isinstance(a, tuple) else a for a in arrs]

    def out_shapes(self):
        shapes = []
        for a, rows in zip(self.arrs, self.rows):
            shape = tuple(a.shape) if rows is None else (a.shape[0], rows[1], a.shape[2])
            shapes.append(jax.ShapeDtypeStruct((N_DEV,) + shape if self.gather else shape, a.dtype))
        return shapes


def _call(body, *, name, grid, in_specs, out_specs, out_shape, semantics, scratch_shapes=(), riders=None):
    if riders is None or not riders.arrs:
        return pl.pallas_call(body, name=name, grid=grid, in_specs=in_specs, out_specs=tuple(out_specs),
                              out_shape=tuple(out_shape), scratch_shapes=list(scratch_shapes),
                              compiler_params=_cparams(*semantics))
    n_in, n_out, n_scr, n_r = len(in_specs), len(out_specs), len(scratch_shapes), len(riders.arrs)
    gather = riders.gather

    def hosted(*refs):
        ins, r_ins = refs[:n_in], refs[n_in:n_in + n_r]
        at = n_in + n_r
        outs, r_outs = refs[at:at + n_out], refs[at + n_out:at + n_out + n_r]
        at += n_out + n_r
        scratch, sems = refs[at:at + n_scr], refs[at + n_scr:]
        first = functools.reduce(jnp.logical_and, [pl.program_id(a) == 0 for a in range(len(grid))])
        last = functools.reduce(jnp.logical_and, [pl.program_id(a) == grid[a] - 1 for a in range(len(grid))])

        r_ins = [(ref, rows) if rows is not None else ref for ref, rows in zip(r_ins, riders.rows)]

        @pl.when(first)
        def _():
            if gather:
                _relay_start(r_ins, r_outs, *sems)
            else:
                _exchange_start(r_ins, r_outs, *sems, gather)

        body(*ins, *outs, *scratch)

        @pl.when(last)
        def _():
            if gather:
                _relay_wait(r_ins, r_outs, *sems)
            else:
                _exchange_wait(r_ins, r_outs, *sems, gather)

    call = pl.pallas_call(
        hosted, name=name, grid=grid, in_specs=list(in_specs) + [ANY] * n_r, out_specs=tuple(out_specs) + (ANY,) * n_r,
        out_shape=tuple(out_shape) + tuple(riders.out_shapes()),
        scratch_shapes=list(scratch_shapes) + _exchange_sems(n_r),
        compiler_params=_cparams(*(("arbitrary",) * len(grid))))
    return lambda *args: call(*args, *riders.arrs)


def _shards_per_step(shape):
    _, k, n = shape
    best = None
    for q in (1, 2, 4, 8):
        if (q * n) % LANES == 0 and (best is None or 2 * k * q * n <= WEIGHT_BLOCK_BYTES):
            best = q
    return best


def mm_nn(a, b, out_dtype, name, tm=512, riders=None, transposed=False):
    m, k = a.shape
    tm = _tile(m, tm, 8)
    if b.ndim == 3:
        shard = b.shape[1] if transposed else b.shape[2]
        n = N_DEV * shard
        per_step = _shards_per_step((N_DEV, k, shard))
        tn = per_step * shard
        b_spec = pl.BlockSpec((per_step,) + b.shape[1:], lambda j, i: (j, 0, 0))
        contract = (((1,), (1,)), ((), ())) if transposed else (((1,), (0,)), ((), ()))

        def body(a_ref, b_ref, o_ref):
            for q in range(per_step):
                o_ref[:, q * shard:(q + 1) * shard] = lax.dot_general(
                    a_ref[...], b_ref[q], contract, preferred_element_type=f32).astype(o_ref.dtype)
    else:
        n = b.shape[1]
        tn = _weight_tile(n, k)
        b_spec = pl.BlockSpec((k, tn), lambda j, i: (0, j))

        def body(a_ref, b_ref, o_ref):
            o_ref[...] = jnp.dot(a_ref[...], b_ref[...], preferred_element_type=f32).astype(o_ref.dtype)

    return _call(
        body, name=name, grid=(n // tn, m // tm),
        in_specs=[pl.BlockSpec((tm, k), lambda j, i: (i, 0)), b_spec],
        out_specs=[pl.BlockSpec((tm, tn), lambda j, i: (i, j))],
        out_shape=[jax.ShapeDtypeStruct((m, n), out_dtype)], semantics=("parallel", "parallel"), riders=riders,
    )(a, b)


def mm_nn_residual(a, b, x, gate, norm, name, tm=512, riders=None):
    m, k = a.shape
    n = b.shape[1]
    tm = _tile(m, tm, 8)

    def body(a_ref, b_ref, x_ref, g_ref, *rest):
        if norm is not None:
            gn_ref, sc_ref, sh_ref, y_ref, o_ref, h_ref = rest
        else:
            y_ref, o_ref = rest
        y = jnp.dot(a_ref[...], b_ref[...], preferred_element_type=f32)
        y_ref[...] = y.astype(bf16)
        x_new = x_ref[...] + g_ref[...] * y
        o_ref[...] = x_new
        if norm is not None:
            h_ref[...] = _norm_fn(x_new, gn_ref[...], sc_ref[...], sh_ref[...]).astype(bf16)

    blk = pl.BlockSpec((tm, n), lambda i: (i, 0))
    in_specs = [pl.BlockSpec((tm, k), lambda i: (i, 0)), pl.BlockSpec((k, n), lambda i: (0, 0)), blk, _row(n)]
    out_specs = [blk, blk]
    out_shape = [jax.ShapeDtypeStruct((m, n), bf16), jax.ShapeDtypeStruct((m, n), f32)]
    args = [a, b, x, gate]
    if norm is not None:
        in_specs += [_row(n)] * 3
        out_specs += [blk]
        out_shape += [jax.ShapeDtypeStruct((m, n), bf16)]
        args += list(norm)
    return _call(body, name=name, grid=(m // tm,), in_specs=in_specs, out_specs=out_specs, out_shape=out_shape,
                 semantics=("parallel",), riders=riders)(*args)


def mm_nt(a, b, out_dtype, name, tm=512, riders=None, transposed=False):
    m = a.shape[0]
    tm = _tile(m, tm, 8)
    if b.ndim == 3:
        shard, k = (b.shape[1], b.shape[2]) if transposed else (b.shape[2], b.shape[1])
        tk = k
        b_spec = pl.BlockSpec(b.shape, lambda i, j: (0, 0, 0))
        width = N_DEV * shard
        contract = (((1,), (0,)), ((), ())) if transposed else (((1,), (1,)), ((), ()))

        def body(a_ref, b_ref, o_ref):
            acc = None
            for q in range(N_DEV):
                part = lax.dot_general(a_ref[:, q * shard:(q + 1) * shard], b_ref[q], contract, preferred_element_type=f32)
                acc = part if acc is None else acc + part
            o_ref[...] = acc.astype(o_ref.dtype)
    else:
        k, width = b.shape
        tk = _weight_tile(k, width)
        b_spec = pl.BlockSpec((tk, width), lambda i, j: (j, 0))

        def body(a_ref, b_ref, o_ref):
            o_ref[...] = lax.dot_general(a_ref[...], b_ref[...], (((1,), (1,)), ((), ())),
                                         preferred_element_type=f32).astype(o_ref.dtype)

    return _call(
        body, name=name, grid=(m // tm, k // tk),
        in_specs=[pl.BlockSpec((tm, width), lambda i, j: (i, 0)), b_spec],
        out_specs=[pl.BlockSpec((tm, tk), lambda i, j: (i, j))],
        out_shape=[jax.ShapeDtypeStruct((m, k), out_dtype)], semantics=("parallel", "parallel"), riders=riders,
    )(a, b)


def mm_tn(a, b, out_dtype, name, tm=512, tn=512, riders=None):
    t, m = a.shape
    n = b.shape[1]
    tm, tn = _tile(m, tm), _tile(n, tn)

    def body(a_ref, b_ref, o_ref):
        o_ref[...] = lax.dot_general(a_ref[...], b_ref[...], (((0,), (0,)), ((), ())),
                                     preferred_element_type=f32).astype(o_ref.dtype)

    return _call(
        body, name=name, grid=(m // tm, n // tn),
        in_specs=[pl.BlockSpec((t, tm), lambda i, j: (0, i)), pl.BlockSpec((t, tn), lambda i, j: (0, j))],
        out_specs=[pl.BlockSpec((tm, tn), lambda i, j: (i, j))],
        out_shape=[jax.ShapeDtypeStruct((m, n), out_dtype)], semantics=("parallel", "parallel"), riders=riders,
    )(a, b)


def mm_tn_by_owner(a, b, name, tm=512, riders=None):
    t, m = a.shape
    n = b.shape[1]
    shard = n // N_DEV
    per_step = 1 if shard % LANES == 0 else 2
    assert (per_step * shard) % LANES == 0
    tm = _tile(m, tm)

    def body(a_ref, b_ref, o_ref):
        acc = lax.dot_general(a_ref[...], b_ref[...], (((0,), (0,)), ((), ())), preferred_element_type=f32)
        for q in range(per_step):
            o_ref[q] = acc[:, q * shard:(q + 1) * shard].astype(bf16)

    return _call(
        body, name=name, grid=(m // tm, N_DEV // per_step),
        in_specs=[pl.BlockSpec((t, tm), lambda i, j: (0, i)), pl.BlockSpec((t, per_step * shard), lambda i, j: (0, j))],
        out_specs=[pl.BlockSpec((per_step, tm, shard), lambda i, j: (j, i, 0))],
        out_shape=[jax.ShapeDtypeStruct((N_DEV, m, shard), bf16)], semantics=("parallel", "parallel"), riders=riders,
    )(a, b)


def _norm_fn(x, gn, sc, sh):
    r = lax.rsqrt(jnp.mean(x * x, axis=-1, keepdims=True) + EPS)
    return (x * r * gn) * (1.0 + sc) + sh


def _row(d):
    return pl.BlockSpec((1, d), lambda i: (0, 0))


def norm_fwd(x, gn, sc, sh, name, tm=512):
    t, d = x.shape
    tm = _tile(t, tm, 8)

    def body(x_ref, gn_ref, sc_ref, sh_ref, h_ref):
        h_ref[...] = _norm_fn(x_ref[...], gn_ref[...], sc_ref[...], sh_ref[...]).astype(bf16)

    return pl.pallas_call(
        body, name=name, grid=(t // tm,),
        in_specs=[pl.BlockSpec((tm, d), lambda i: (i, 0)), _row(d), _row(d), _row(d)],
        out_specs=pl.BlockSpec((tm, d), lambda i: (i, 0)),
        out_shape=jax.ShapeDtypeStruct((t, d), bf16), compiler_params=_cparams("parallel"),
    )(x, gn, sc, sh)


def _gate_bwd(dx, y_ref, g_ref, dgate_ref, dy_ref):
    dgate_ref[...] += jnp.sum(dx * y_ref[...].astype(f32), axis=0, keepdims=True)
    dy_ref[...] = (dx * g_ref[...]).astype(bf16)


def norm_bwd(x, gn, sc, sh, dh, dres, name, gate=None, tm=512, riders=None):
    t, d = x.shape
    tm = _tile(t, tm, 8)

    def body(x_ref, gn_ref, sc_ref, sh_ref, dh_ref, dres_ref, *rest):
        if gate is not None:
            y_ref, g_ref, dx_ref, dgn_ref, dsc_ref, dsh_ref, dgate_ref, dy_ref = rest
        else:
            dx_ref, dgn_ref, dsc_ref, dsh_ref = rest

        @pl.when(pl.program_id(0) == 0)
        def _():
            dgn_ref[...] = jnp.zeros_like(dgn_ref)
            dsc_ref[...] = jnp.zeros_like(dsc_ref)
            dsh_ref[...] = jnp.zeros_like(dsh_ref)
            if gate is not None:
                dgate_ref[...] = jnp.zeros_like(dgate_ref)

        _, vjp = jax.vjp(_norm_fn, x_ref[...], gn_ref[...], sc_ref[...], sh_ref[...])
        dx, dgn, dsc, dsh = vjp(dh_ref[...].astype(f32))
        dx = dx + dres_ref[...]
        dx_ref[...] = dx
        dgn_ref[...] += dgn
        dsc_ref[...] += dsc
        dsh_ref[...] += dsh
        if gate is not None:
            _gate_bwd(dx, y_ref, g_ref, dgate_ref, dy_ref)

    blk = pl.BlockSpec((tm, d), lambda i: (i, 0))
    vec = jax.ShapeDtypeStruct((1, d), f32)
    in_specs = [blk, _row(d), _row(d), _row(d), blk, blk]
    out_specs = [blk, _row(d), _row(d), _row(d)]
    out_shape = [jax.ShapeDtypeStruct((t, d), f32), vec, vec, vec]
    args = [x, gn, sc, sh, dh, dres]
    if gate is not None:
        in_specs += [blk, _row(d)]
        out_specs += [_row(d), blk]
        out_shape += [vec, jax.ShapeDtypeStruct((t, d), bf16)]
        args += list(gate)
    return _call(body, name=name, grid=(t // tm,), in_specs=in_specs, out_specs=out_specs, out_shape=out_shape,
                 semantics=("arbitrary",), riders=riders)(*args)


def _loss_fn(x, g, tgt):
    r = lax.rsqrt(jnp.mean(x * x, axis=-1, keepdims=True) + EPS)
    err = jnp.square(x * r * g - tgt)
    return 0.5 * jnp.sum(jnp.mean(err, axis=-1, keepdims=True), axis=0, keepdims=True)


def loss_head(x, g, tgt, y, gate, name, tm=512):
    t, d = x.shape
    tm = _tile(t, tm, 8)

    def body(x_ref, g_ref, t_ref, y_ref, gate_ref, loss_ref, dx_ref, dg_ref, dgate_ref, dy_ref):
        @pl.when(pl.program_id(0) == 0)
        def _():
            loss_ref[...] = jnp.zeros_like(loss_ref)
            dg_ref[...] = jnp.zeros_like(dg_ref)
            dgate_ref[...] = jnp.zeros_like(dgate_ref)

        loss, vjp = jax.vjp(_loss_fn, x_ref[...], g_ref[...], t_ref[...])
        dx, dg, _ = vjp(jnp.ones((1, 1), f32))
        dx_ref[...] = dx
        loss_ref[...] += loss
        dg_ref[...] += dg
        _gate_bwd(dx, y_ref, gate_ref, dgate_ref, dy_ref)

    blk = pl.BlockSpec((tm, d), lambda i: (i, 0))
    vec = jax.ShapeDtypeStruct((1, d), f32)
    return pl.pallas_call(
        body, name=name, grid=(t // tm,),
        in_specs=[blk, _row(d), blk, blk, _row(d)],
        out_specs=(pl.BlockSpec((1, 1), lambda i: (0, 0)), blk, _row(d), _row(d), blk),
        out_shape=(jax.ShapeDtypeStruct((1, 1), f32), jax.ShapeDtypeStruct((t, d), f32), vec, vec,
                   jax.ShapeDtypeStruct((t, d), bf16)),
        compiler_params=_cparams("arbitrary"),
    )(x, g, tgt, y, gate)


def _gm_block_fn(z, ws, bs, lng, lnb):
    u = _gelu(z[:, :GM_WIDTH])
    vg = _gelu(z[:, GM_WIDTH:])
    mu = jnp.mean(vg, axis=-1, keepdims=True)
    var = jnp.mean(jnp.square(vg - mu), axis=-1, keepdims=True)
    vn = (vg - mu) * lax.rsqrt(var + EPS) * lng + lnb
    row = lax.broadcasted_iota(jnp.int32, (GM_BLOCK, GM_BLOCK), 0) // CHUNK
    col = lax.broadcasted_iota(jnp.int32, (GM_BLOCK, GM_BLOCK), 1) // CHUNK
    parts = []
    for h in range(GM_HEADS):
        w = jnp.where(row >= col, ws[h], 0.0)
        cols = slice(h * GM_HEAD_DIM, (h + 1) * GM_HEAD_DIM)
        s = jnp.dot(w.astype(bf16), vn[:, cols].astype(bf16), preferred_element_type=f32) + bs[h]
        parts.append(u[:, cols] * s)
    return jnp.concatenate(parts, axis=1)


def _gm_param_specs():
    return [pl.BlockSpec((GM_HEADS, GM_BLOCK, GM_BLOCK), lambda i: (0, 0, 0)),
            pl.BlockSpec((GM_HEADS, GM_BLOCK, 1), lambda i: (0, 0, 0)), _row(GM_WIDTH), _row(GM_WIDTH)]


def gm_mix_fwd(z, ws, bs, lng, lnb, name, riders=None):
    t = z.shape[0]

    def body(z_ref, ws_ref, bs_ref, lng_ref, lnb_ref, o_ref):
        o_ref[...] = _gm_block_fn(z_ref[...].astype(f32), ws_ref[...], bs_ref[...], lng_ref[...],
                                  lnb_ref[...]).astype(bf16)

    return _call(
        body, name=name, grid=(t // GM_BLOCK,),
        in_specs=[pl.BlockSpec((GM_BLOCK, 2 * GM_WIDTH), lambda i: (i, 0))] + _gm_param_specs(),
        out_specs=[pl.BlockSpec((GM_BLOCK, GM_WIDTH), lambda i: (i, 0))],
        out_shape=[jax.ShapeDtypeStruct((t, GM_WIDTH), bf16)], semantics=("parallel",), riders=riders,
    )(z, ws, bs, lng, lnb)


def gm_mix_bwd(z, ws, bs, lng, lnb, dgated, name, riders=None):
    t = z.shape[0]

    def body(z_ref, ws_ref, bs_ref, lng_ref, lnb_ref, dg_ref, dz_ref, dws_ref, dbs_ref, dlng_ref, dlnb_ref):
        _, vjp = jax.vjp(_gm_block_fn, z_ref[...].astype(f32), ws_ref[...], bs_ref[...], lng_ref[...], lnb_ref[...])
        dz, dws, dbs, dlng, dlnb = vjp(dg_ref[...].astype(f32))
        dz_ref[...] = dz.astype(bf16)

        @pl.when(pl.program_id(0) == 0)
        def _():
            dws_ref[...] = jnp.zeros_like(dws_ref)
            dbs_ref[...] = jnp.zeros_like(dbs_ref)
            dlng_ref[...] = jnp.zeros_like(dlng_ref)
            dlnb_ref[...] = jnp.zeros_like(dlnb_ref)

        dws_ref[...] += dws
        dbs_ref[...] += dbs
        dlng_ref[...] += dlng
        dlnb_ref[...] += dlnb

    zblk = pl.BlockSpec((GM_BLOCK, 2 * GM_WIDTH), lambda i: (i, 0))
    return _call(
        body, name=name, grid=(t // GM_BLOCK,),
        in_specs=[zblk] + _gm_param_specs() + [pl.BlockSpec((GM_BLOCK, GM_WIDTH), lambda i: (i, 0))],
        out_specs=[zblk] + _gm_param_specs(),
        out_shape=[jax.ShapeDtypeStruct((t, 2 * GM_WIDTH), bf16),
                   jax.ShapeDtypeStruct((GM_HEADS, GM_BLOCK, GM_BLOCK), f32),
                   jax.ShapeDtypeStruct((GM_HEADS, GM_BLOCK, 1), f32),
                   jax.ShapeDtypeStruct((1, GM_WIDTH), f32), jax.ShapeDtypeStruct((1, GM_WIDTH), f32)],
        semantics=("arbitrary",), riders=riders,
    )(z, ws, bs, lng, lnb, dgated)


@functools.partial(jax.custom_vjp, nondiff_argnums=(1,))
def _rows_up(x, shift):
    return x if shift == 0 else pltpu.roll(x, x.shape[1] - shift, axis=1)


def _rows_up_fwd(x, shift):
    return _rows_up(x, shift), None


def _rows_up_bwd(shift, _, g):
    return (g if shift == 0 else pltpu.roll(g, shift, axis=1),)


_rows_up.defvjp(_rows_up_fwd, _rows_up_bwd)


def _hg_block_fn(sub, qp, fz, iv, gp, s0, lb, gn):
    n, ns, d = sub, HG_TOKENS // sub, HG_DIM
    p, nb, per_sub = HG_PAIR, HG_TOKENS // HG_PAIR, sub // HG_PAIR
    f = lb + (1.0 - lb) * jax.nn.sigmoid(fz)
    g = jnp.log(f)
    k = 1.0 - f
    q = qp * jax.nn.sigmoid(qp)
    v = iv.astype(bf16)
    row = lax.broadcasted_iota(jnp.int32, (HG_TOKENS, HG_TOKENS), 0)
    col = lax.broadcasted_iota(jnp.int32, (HG_TOKENS, HG_TOKENS), 1)
    same_sub = col // n == row // n
    tri = ((col <= row) & same_sub).astype(f32)
    cum = jnp.dot(tri, g, precision=lax.Precision.HIGHEST, preferred_element_type=f32)
    cum_b, q_b, k_b, f_b = cum.reshape(nb, p, d), q.reshape(nb, p, d), k.reshape(nb, p, d), f.reshape(nb, p, d)
    j_b = lax.broadcasted_iota(jnp.int32, (nb, p, d), 1)
    j_col = lax.broadcasted_iota(jnp.int32, (nb, p, 1), 1)
    scores_t = jnp.zeros((HG_TOKENS, HG_TOKENS), f32)
    weight = k_b
    for delta in range(p):
        if delta:
            weight = weight * _rows_up(f_b, delta)
        pair = jnp.sum(_rows_up(q_b, delta) * weight, axis=2, keepdims=True)
        pair = jnp.where(j_col < p - delta, pair, 0.0)
        scores_t = scores_t + jnp.where(col == row + delta, pair.reshape(HG_TOKENS, 1), 0.0)
    o = lax.dot_general(scores_t.astype(bf16), v, (((0,), (0,)), ((), ())), preferred_element_type=f32)
    last = cum_b[:, p - 1:p, :]
    before = jnp.concatenate([jnp.zeros((1, 1, d), f32), last[:-1]], axis=0)
    before = jnp.broadcast_to(before, (nb, p, d)).reshape(HG_TOKENS, d)
    block = (lax.broadcasted_iota(jnp.int32, (HG_TOKENS, d), 0) // p) % per_sub
    q_late = q * jnp.exp(jnp.where(block > 0, cum - before, -1e30))
    last_s = last.reshape(ns, per_sub, d)
    q_parts, k_parts = [], []
    for m in range(1, per_sub):
        split = jnp.broadcast_to(last_s[:, m - 1:m, :], (ns, n, d)).reshape(HG_TOKENS, d)
        k_parts.append(k * jnp.exp(jnp.where(block < m, split - cum, -1e30)))
        q_parts.append(jnp.where(block == m, q_late, 0.0))
    scores = lax.dot_general(jnp.concatenate(q_parts, axis=1).astype(bf16), jnp.concatenate(k_parts, axis=1).astype(bf16),
                             (((1,), (1,)), ((), ())), preferred_element_type=f32)
    o = o + jnp.dot(jnp.where(same_sub, scores, 0.0).astype(bf16), v, preferred_element_type=f32)
    cum_s = cum.reshape(ns, n, d)
    tot = cum_s[:, n - 1:n, :]
    kt_t = (k.reshape(ns, n, d) * jnp.exp(tot - cum_s)).reshape(HG_TOKENS, d).T
    lane_sub = lax.broadcasted_iota(jnp.int32, (d, HG_TOKENS), 1) // n
    k_by_sub = jnp.concatenate([jnp.where(lane_sub == b, kt_t, 0.0) for b in range(ns)], axis=0).astype(bf16)
    update = jnp.dot(k_by_sub, v, preferred_element_type=f32)
    decay = jnp.exp(tot.reshape(ns, d)).T
    state = s0
    states = []
    for a in range(ns):
        states.append(state.astype(bf16))
        state = decay[:, a:a + 1] * state + update[a * d:(a + 1) * d]
    qt = q * jnp.exp(cum)
    row_sub = lax.broadcasted_iota(jnp.int32, (HG_TOKENS, d), 0) // n
    q_by_sub = jnp.concatenate([jnp.where(row_sub == a, qt, 0.0) for a in range(ns)], axis=1).astype(bf16)
    o = o + jnp.dot(q_by_sub, jnp.concatenate(states, axis=0), preferred_element_type=f32)
    on = o * lax.rsqrt(jnp.mean(o * o, axis=-1, keepdims=True) + EPS) * gn
    return on * (gp * jax.nn.sigmoid(gp)), state


def _head_parts(ref, h):
    return [ref[:, p * D + h * HG_DIM:p * D + (h + 1) * HG_DIM] for p in range(4)]


def hg_scan_fwd(proj, lb, gn, name, riders=None):
    t = proj.shape[0]
    nt = t // HG_TOKENS

    def body(p_ref, lb_ref, gn_ref, y_ref, s_ref, state):
        @pl.when(pl.program_id(0) == 0)
        def _():
            state[...] = jnp.zeros_like(state)

        for h in range(HG_HEADS):
            cols = slice(h * HG_DIM, (h + 1) * HG_DIM)
            s_ref[h, 0] = state[h]
            y, s1 = _hg_block_fn(HG_SUB_FWD, *_head_parts(p_ref, h), state[h], lb_ref[:, cols], gn_ref[:, cols])
            y_ref[:, cols] = y.astype(bf16)
            state[h] = s1

    return _call(
        body, name=name, grid=(nt,),
        in_specs=[pl.BlockSpec((HG_TOKENS, 4 * D), lambda i: (i, 0)), _row(D), _row(D)],
        out_specs=[pl.BlockSpec((HG_TOKENS, D), lambda i: (i, 0)),
                   pl.BlockSpec((HG_HEADS, 1, HG_DIM, HG_DIM), lambda i: (0, i, 0, 0))],
        out_shape=[jax.ShapeDtypeStruct((t, D), bf16), jax.ShapeDtypeStruct((HG_HEADS, nt, HG_DIM, HG_DIM), f32)],
        scratch_shapes=[pltpu.VMEM((HG_HEADS, HG_DIM, HG_DIM), f32)],
        semantics=("arbitrary",), riders=riders,
    )(proj, lb, gn)


def hg_scan_bwd(proj, lb, gn, states, dy, name, riders=None):
    t = proj.shape[0]
    nt = t // HG_TOKENS

    def body(p_ref, lb_ref, gn_ref, s_ref, dy_ref, dp_ref, dlb_ref, dgn_ref, dstate):
        @pl.when(pl.program_id(0) == 0)
        def _():
            dstate[...] = jnp.zeros_like(dstate)
            dlb_ref[...] = jnp.zeros_like(dlb_ref)
            dgn_ref[...] = jnp.zeros_like(dgn_ref)

        for h in range(HG_HEADS):
            cols = slice(h * HG_DIM, (h + 1) * HG_DIM)
            _, vjp = jax.vjp(functools.partial(_hg_block_fn, HG_SUB_BWD), *_head_parts(p_ref, h), s_ref[h, 0],
                             lb_ref[:, cols], gn_ref[:, cols])
            grads = vjp((dy_ref[:, cols].astype(f32), dstate[h]))
            for p in range(4):
                dp_ref[:, p * D + h * HG_DIM:p * D + (h + 1) * HG_DIM] = grads[p].astype(bf16)
            dstate[h] = grads[4]
            dlb_ref[:, cols] += grads[5]
            dgn_ref[:, cols] += grads[6]

    small = jax.ShapeDtypeStruct((1, D), f32)
    return _call(
        body, name=name, grid=(nt,),
        in_specs=[pl.BlockSpec((HG_TOKENS, 4 * D), lambda i: (nt - 1 - i, 0)), _row(D), _row(D),
                  pl.BlockSpec((HG_HEADS, 1, HG_DIM, HG_DIM), lambda i: (0, nt - 1 - i, 0, 0)),
                  pl.BlockSpec((HG_TOKENS, D), lambda i: (nt - 1 - i, 0))],
        out_specs=[pl.BlockSpec((HG_TOKENS, 4 * D), lambda i: (nt - 1 - i, 0)), _row(D), _row(D)],
        out_shape=[jax.ShapeDtypeStruct((t, 4 * D), bf16), small, small],
        scratch_shapes=[pltpu.VMEM((HG_HEADS, HG_DIM, HG_DIM), f32)],
        semantics=("arbitrary",), riders=riders,
    )(proj, lb, gn, states, dy)


FFN_COLS = 1408
HALO = 8
STRIP = 16


def _ffn_specs(tm):
    nb = tm // HALO
    main_g = pl.BlockSpec((tm, FFN_COLS), lambda j, i: (i, j))
    main_v = pl.BlockSpec((tm, FFN_COLS), lambda j, i: (i, j + 2))
    halo_g = pl.BlockSpec((HALO, FFN_COLS), lambda j, i: (jnp.maximum(i * nb - 1, 0), j))
    halo_v = pl.BlockSpec((HALO, FFN_COLS), lambda j, i: (jnp.maximum(i * nb - 1, 0), j + 2))
    w_g = pl.BlockSpec((3, FFN_COLS), lambda j, i: (0, j))
    w_v = pl.BlockSpec((3, FFN_COLS), lambda j, i: (0, j + 2))
    b_g = pl.BlockSpec((1, FFN_COLS), lambda j, i: (0, j))
    b_v = pl.BlockSpec((1, FFN_COLS), lambda j, i: (0, j + 2))
    return [main_g, halo_g, main_v, halo_v, w_g, w_v, b_g, b_v]


def _strip_rows(r):
    return pl.ds(r * STRIP, STRIP) if isinstance(r, int) else pl.ds(pl.multiple_of(r * STRIP, STRIP), STRIP)


def _for_strips(nstrip, strip, reverse=False):
    if reverse:
        strip(nstrip - 1, True)
        lax.fori_loop(0, nstrip - 1, lambda k, c: (strip(nstrip - 2 - k, False), c)[1], 0)
    else:
        strip(0, True)
        lax.fori_loop(1, nstrip, lambda r, c: (strip(r, False), c)[1], 0)


SUBLANES = 8


def _rows_down(prev, cur, shift):
    row = lax.broadcasted_iota(jnp.int32, (SUBLANES, LANES), 0)
    tiles = [prev[STRIP - SUBLANES:]] + [cur[q * SUBLANES:(q + 1) * SUBLANES] for q in range(STRIP // SUBLANES)]
    turned = [pltpu.roll(x, shift, axis=0) for x in tiles]
    return jnp.concatenate([jnp.where(row < shift, turned[q], turned[q + 1]) for q in range(STRIP // SUBLANES)], axis=0)


def _rows_ahead(cur, nxt, shift):
    row = lax.broadcasted_iota(jnp.int32, (SUBLANES, LANES), 0)
    tiles = [cur[q * SUBLANES:(q + 1) * SUBLANES] for q in range(STRIP // SUBLANES)] + [nxt[:SUBLANES]]
    turned = [pltpu.roll(x, SUBLANES - shift, axis=0) for x in tiles]
    return jnp.concatenate([jnp.where(row >= SUBLANES - shift, turned[q + 1], turned[q])
                            for q in range(STRIP // SUBLANES)], axis=0)


def _conv_strip(main_ref, halo_ref, w_ref, b_ref, r, edge, cols, first_block):
    cur = main_ref[_strip_rows(r), cols].astype(f32)
    if edge:
        h = jnp.where(first_block, 0.0, halo_ref[:, cols].astype(f32))
        prev = jnp.concatenate([jnp.zeros_like(h), h], axis=0)
    else:
        prev = main_ref[_strip_rows(r - 1), cols].astype(f32)
    a1, a2 = _rows_down(prev, cur, 1), _rows_down(prev, cur, 2)
    y = b_ref[:, cols] + w_ref[0:1, cols] * a2 + w_ref[1:2, cols] * a1 + w_ref[2:3, cols] * cur
    return y, (cur, a1, a2)


def ffn_gate_fwd(a, cw, cb, name, tm=512, riders=None):
    t = a.shape[0]
    tm = _tile(t, tm, STRIP)

    def body(ag_ref, hg_ref, av_ref, hv_ref, wg_ref, wv_ref, bg_ref, bv_ref, o_ref):
        first_block = pl.program_id(1) == 0

        def strip(r, edge):
            for c in range(FFN_COLS // LANES):
                cols = pl.ds(c * LANES, LANES)
                yg, _ = _conv_strip(ag_ref, hg_ref, wg_ref, bg_ref, r, edge, cols, first_block)
                yv, _ = _conv_strip(av_ref, hv_ref, wv_ref, bv_ref, r, edge, cols, first_block)
                o_ref[_strip_rows(r), cols] = (_gelu(yg) * yv).astype(bf16)

        _for_strips(tm // STRIP, strip)

    return _call(
        body, name=name, grid=(2, t // tm), in_specs=_ffn_specs(tm),
        out_specs=[pl.BlockSpec((tm, FFN_COLS), lambda j, i: (i, j))],
        out_shape=[jax.ShapeDtypeStruct((t, FFN_HIDDEN), bf16)],
        semantics=("parallel", "arbitrary"), riders=riders,
    )(a, a, a, a, cw, cw, cb, cb)


def _conv_tile(prev, cur, w_ref, b_ref, cols):
    row = lax.broadcasted_iota(jnp.int32, (SUBLANES, LANES), 0)
    a1 = jnp.where(row < 1, pltpu.roll(prev, 1, axis=0), pltpu.roll(cur, 1, axis=0))
    a2 = jnp.where(row < 2, pltpu.roll(prev, 2, axis=0), pltpu.roll(cur, 2, axis=0))
    return b_ref[:, cols] + w_ref[0:1, cols] * a2 + w_ref[1:2, cols] * a1 + w_ref[2:3, cols] * cur


def _gate_grads(yg, yv, dh):
    cdf = 0.5 * (1.0 + lax.erf(yg * (1.0 / math.sqrt(2.0))))
    pdf = jnp.exp(-0.5 * yg * yg) * (1.0 / math.sqrt(2.0 * math.pi))
    return dh * yv * (cdf + yg * pdf), dh * (yg * cdf)


def ffn_gate_bwd(a, cw, cb, dhid, name, tm=512, riders=None):
    t = a.shape[0]
    tm = _tile(t, tm, STRIP)
    nb = tm // HALO
    last_halo = t // HALO - 1
    nstrip = tm // STRIP
    fh = FFN_HIDDEN

    def body(a_ref, ha_ref, na_ref, w_ref, b_ref, dh_ref, ndh_ref, da_ref, dw_ref, db_ref, acc, dybuf):
        first_block = pl.program_id(0) == 0
        last_block = pl.program_id(0) == pl.num_programs(0) - 1

        @pl.when(first_block)
        def _():
            acc[...] = jnp.zeros_like(acc)

        def transposed_conv(dy, ahead, cols):
            return (w_ref[2:3, cols] * dy + w_ref[1:2, cols] * _rows_ahead(dy, ahead, 1)
                    + w_ref[0:1, cols] * _rows_ahead(dy, ahead, 2)).astype(bf16)

        def strip(r, edge):
            rows = _strip_rows(r)
            for c in range(fh // LANES):
                gate, val = pl.ds(c * LANES, LANES), pl.ds(fh + c * LANES, LANES)
                yg, taps_g = _conv_strip(a_ref, ha_ref, w_ref, b_ref, r, edge, gate, first_block)
                yv, taps_v = _conv_strip(a_ref, ha_ref, w_ref, b_ref, r, edge, val, first_block)
                dyg, dyv = _gate_grads(yg, yv, dh_ref[rows, gate].astype(f32))
                for p, (dy, (a0, a1, a2), cols) in enumerate(((dyg, taps_g, gate), (dyv, taps_v, val))):
                    acc[0, :, cols] += dy * a2
                    acc[1, :, cols] += dy * a1
                    acc[2, :, cols] += dy * a0
                    acc[3, :, cols] += dy
                    if not edge:
                        da_ref[_strip_rows(r - 1), cols] = transposed_conv(dybuf[:, cols], dy, cols)
                    dybuf[:, cols] = dy

        _for_strips(nstrip, strip)

        last_rows = pl.ds((nstrip - 1) * STRIP + SUBLANES, SUBLANES)
        for c in range(fh // LANES):
            gate, val = pl.ds(c * LANES, LANES), pl.ds(fh + c * LANES, LANES)
            yg = _conv_tile(a_ref[last_rows, gate].astype(f32), na_ref[:, gate].astype(f32), w_ref, b_ref, gate)
            yv = _conv_tile(a_ref[last_rows, val].astype(f32), na_ref[:, val].astype(f32), w_ref, b_ref, val)
            for dy, cols in zip(_gate_grads(yg, yv, ndh_ref[:, gate].astype(f32)), (gate, val)):
                dy = jnp.where(last_block, 0.0, dy)
                ahead = jnp.concatenate([dy, jnp.zeros_like(dy)], axis=0)
                da_ref[_strip_rows(nstrip - 1), cols] = transposed_conv(dybuf[:, cols], ahead, cols)

        @pl.when(last_block)
        def _():
            for tap in range(3):
                dw_ref[tap:tap + 1, :] = jnp.sum(acc[tap], axis=0, keepdims=True)
            db_ref[...] = jnp.sum(acc[3], axis=0, keepdims=True)

    def after(i):
        return jnp.minimum((i + 1) * nb, last_halo)

    return _call(
        body, name=name, grid=(t // tm,),
        in_specs=[pl.BlockSpec((tm, 2 * fh), lambda i: (i, 0)),
                  pl.BlockSpec((HALO, 2 * fh), lambda i: (jnp.maximum(i * nb - 1, 0), 0)),
                  pl.BlockSpec((HALO, 2 * fh), lambda i: (after(i), 0)),
                  pl.BlockSpec((3, 2 * fh), lambda i: (0, 0)), pl.BlockSpec((1, 2 * fh), lambda i: (0, 0)),
                  pl.BlockSpec((tm, fh), lambda i: (i, 0)), pl.BlockSpec((HALO, fh), lambda i: (after(i), 0))],
        out_specs=[pl.BlockSpec((tm, 2 * fh), lambda i: (i, 0)), pl.BlockSpec((3, 2 * fh), lambda i: (0, 0)),
                   pl.BlockSpec((1, 2 * fh), lambda i: (0, 0))],
        out_shape=[jax.ShapeDtypeStruct((t, 2 * fh), bf16), jax.ShapeDtypeStruct((3, 2 * fh), f32),
                   jax.ShapeDtypeStruct((1, 2 * fh), f32)],
        scratch_shapes=[pltpu.VMEM((4, STRIP, 2 * fh), f32), pltpu.VMEM((STRIP, 2 * fh), f32)],
        semantics=("arbitrary",), riders=riders,
    )(a, a, a, cw, cb, dhid, dhid)


def ada_mod(c_all, ada_w, ada_b_cols, name):
    cols = ada_w.shape[2]

    def body(c_ref, w_ref, b_ref, o_ref):
        c = c_ref[...]
        cond = (c * jax.nn.sigmoid(c)).astype(bf16)
        o_ref[0] = jnp.dot(cond, w_ref[0].astype(bf16), preferred_element_type=f32) + b_ref[0]

    return pl.pallas_call(
        body, name=name, grid=(DEPTH,),
        in_specs=[pl.BlockSpec((N_DEV, D), lambda i: (0, 0)), pl.BlockSpec((1, D, cols), lambda i: (i, 0, 0)),
                  pl.BlockSpec((1, 1, cols), lambda i: (i, 0, 0))],
        out_specs=pl.BlockSpec((1, N_DEV, cols), lambda i: (i, 0, 0)),
        out_shape=jax.ShapeDtypeStruct((DEPTH, N_DEV, cols), f32), compiler_params=_cparams("parallel"),
    )(c_all, ada_w, ada_b_cols)


def ada_grads(c_all, dmod_cols, dmod_all, name):
    cols = dmod_cols.shape[2]

    def body(c_ref, dm_ref, da_ref, dw_ref, db_ref):
        c = c_ref[...]
        cond = c * jax.nn.sigmoid(c)
        dw_ref[0] = lax.dot_general(cond, dm_ref[0], (((0,), (0,)), ((), ())), precision=lax.Precision.HIGHEST,
                                    preferred_element_type=f32)
        acc = da_ref[0, 0]
        for e in range(1, N_DEV):
            acc = acc + da_ref[e, 0]
        db_ref[0] = acc

    return pl.pallas_call(
        body, name=name, grid=(DEPTH,),
        in_specs=[pl.BlockSpec((N_DEV, D), lambda i: (0, 0)), pl.BlockSpec((1, N_DEV, cols), lambda i: (i, 0, 0)),
                  pl.BlockSpec((N_DEV, 1, 1, 6 * D), lambda i: (0, i, 0, 0))],
        out_specs=(pl.BlockSpec((1, D, cols), lambda i: (i, 0, 0)), pl.BlockSpec((1, 1, 6 * D), lambda i: (i, 0, 0))),
        out_shape=(jax.ShapeDtypeStruct((DEPTH, D, cols), f32), jax.ShapeDtypeStruct((DEPTH, 1, 6 * D), f32)),
        compiler_params=_cparams("parallel"),
    )(c_all, dmod_cols, dmod_all)


def lower_bound_fwd(hg_lb, name):
    n = hg_lb.shape[1]

    def body(l_ref, o_ref):
        o_ref[...] = jax.nn.sigmoid(l_ref[1:2, :] - l_ref[0:1, :])

    return pl.pallas_call(body, name=name, out_shape=jax.ShapeDtypeStruct((1, n), f32))(hg_lb)


def lower_bound_bwd(hg_lb, dlb, name):
    n = hg_lb.shape[1]

    def body(l_ref, d_ref, o_ref):
        p = jax.nn.sigmoid(l_ref[1:2, :] - l_ref[0:1, :])
        g = d_ref[...] * p * (1.0 - p)
        o_ref[0:1, :] = -g
        o_ref[1:2, :] = g

    return pl.pallas_call(body, name=name, out_shape=jax.ShapeDtypeStruct((2, n), f32))(hg_lb, dlb)


def _adamw(w, g, m, v):
    m = ADAM_B1 * m + (1.0 - ADAM_B1) * g
    v = ADAM_B2 * v + (1.0 - ADAM_B2) * jnp.square(g)
    m_hat = m / (1.0 - ADAM_B1 ** ADAM_STEP)
    v_hat = v / (1.0 - ADAM_B2 ** ADAM_STEP)
    delta = -ADAM_LR * (m_hat / (jnp.sqrt(v_hat) + ADAM_EPS) + ADAM_WD * w)
    return delta, m, v


ADAM_BLOCK_BYTES = 32 * 1024 * 1024


def adam_reduced(parts, w, m, v, name):
    layers, r, c = w.shape
    outs = None
    for layer in range(layers):
        outs = _adam_layer(parts[layer], w, m, v, layer, outs, f"{name}_{layer}")
    return outs


def _adam_layer(parts, w, m, v, layer, prev, name):
    layers, r, c = w.shape
    rows = parts[0].shape[1]
    assert all(p.shape == (N_DEV, rows, c) for p in parts) and rows * len(parts) == r
    row_bytes = 2 * (len(parts) * N_DEV * c * 2 + 7 * c * 4)
    tr = _tile(rows, max(16, ADAM_BLOCK_BYTES // row_bytes), 16)
    steps = rows // tr
    n_prev = 0 if prev is None else 4

    def body(*refs):
        p_refs = refs[:len(parts)]
        w_ref, m_ref, v_ref = refs[len(parts):len(parts) + 3]
        g_ref, d_ref, mo_ref, vo_ref = refs[len(parts) + 3 + n_prev:]
        for idx in range(len(parts)):
            @pl.when(pl.program_id(0) == idx)
            def _():
                g = p_refs[idx][0].astype(f32)
                for j in range(1, N_DEV):
                    g = g + p_refs[idx][j].astype(f32)
                g_ref[...] = g
                d_ref[...], mo_ref[...], vo_ref[...] = _adamw(w_ref[...], g, m_ref[...], v_ref[...])

    def part_spec(idx):
        return pl.BlockSpec((N_DEV, tr, c), lambda p, i: (0, jnp.where(p == idx, i, 0), 0))

    blk = pl.BlockSpec((None, tr, c), lambda p, i: (layer, p * steps + i, 0))
    out = jax.ShapeDtypeStruct((layers, r, c), f32)
    n_in = len(parts) + 3
    return pl.pallas_call(
        body, name=name, grid=(len(parts), steps),
        in_specs=[part_spec(idx) for idx in range(len(parts))] + [blk, blk, blk] + [ANY] * n_prev,
        out_specs=(blk, blk, blk, blk), out_shape=(out, out, out, out),
        input_output_aliases={n_in + k: k for k in range(n_prev)},
        compiler_params=_cparams("arbitrary", "arbitrary"),
    )(*parts, w, m, v, *(prev or ()))


def adam_plain(g, w, m, v, name, tr=256):
    r, c = w.shape
    tr = _tile(r, tr, 8)

    def body(g_ref, w_ref, m_ref, v_ref, d_ref, mo_ref, vo_ref):
        d_ref[...], mo_ref[...], vo_ref[...] = _adamw(w_ref[...], g_ref[...], m_ref[...], v_ref[...])

    blk = pl.BlockSpec((tr, c), lambda i: (i, 0))
    out = jax.ShapeDtypeStruct((r, c), f32)
    return pl.pallas_call(
        body, name=name, grid=(r // tr,), in_specs=[blk, blk, blk, blk], out_specs=(blk, blk, blk),
        out_shape=(out, out, out), compiler_params=_cparams("parallel"),
    )(g, w, m, v)


def sum_parts(parts, name):
    _, r, c = parts.shape

    def body(p_ref, o_ref):
        acc = p_ref[0]
        for j in range(1, N_DEV):
            acc = acc + p_ref[j]
        o_ref[...] = acc

    return pl.pallas_call(body, name=name, out_shape=jax.ShapeDtypeStruct((r, c), f32))(parts)


def _pack(arrs, rows_mult=8):
    flat = jnp.concatenate([a.reshape(-1) for a in arrs])
    rows = -(-flat.shape[0] // LANES)
    rows = -(-rows // rows_mult) * rows_mult
    return jnp.pad(flat, (0, rows * LANES - flat.shape[0])).reshape(rows, LANES)


def _unpack(flat, shapes):
    out, at = [], 0
    for s in shapes:
        n = math.prod(s)
        out.append(flat[at:at + n].reshape(s))
        at += n
    return out


def kernel(x, c, gm_w_in, gm_ln_g, gm_ln_b, gm_w_s, gm_b_s, gm_w_out, hg_w_in, hg_lb, hg_gn_g, hg_w_out, ffn_w_up, ffn_conv_w, ffn_conv_b, ffn_w_down, norm_g, ada_w, ada_b, final_g, loss_target, m_gm_w_in, m_gm_ln_g, m_gm_ln_b, m_gm_w_s, m_gm_b_s, m_gm_w_out, m_hg_w_in, m_hg_lb, m_hg_gn_g, m_hg_w_out, m_ffn_w_up, m_ffn_conv_w, m_ffn_conv_b, m_ffn_w_down, m_norm_g, m_ada_w, m_ada_b, m_final_g, v_gm_w_in, v_gm_ln_g, v_gm_ln_b, v_gm_w_s, v_gm_b_s, v_gm_w_out, v_hg_w_in, v_hg_lb, v_hg_gn_g, v_hg_w_out, v_ffn_w_up, v_ffn_conv_w, v_ffn_conv_b, v_ffn_w_down, v_norm_g, v_ada_w, v_ada_b, v_final_g):
    me = _flat(_mesh_pos())
    xt = x[0]
    t = xt.shape[0]

    small_shapes = [(1, D), (2, HG_DIM), (2, HG_DIM), (DEPTH, 2, HG_DIM), (DEPTH, 3, 2 * FFN_HIDDEN // N_DEV)]
    w_in_0 = gm_w_in[0].astype(bf16)
    small_all, w_in_0_all = all_gather([_pack([c, hg_lb, hg_gn_g, norm_g, ffn_conv_w]), w_in_0], "gather_first", relay=True)
    small_all = small_all.reshape(N_DEV, -1)
    at = 0
    pieces = []
    for s in small_shapes:
        n = math.prod(s)
        pieces.append(small_all[:, at:at + n].reshape((N_DEV,) + s))
        at += n
    c_all = pieces[0].reshape(N_DEV, D)
    hg_lb_full = jnp.transpose(pieces[1], (1, 0, 2)).reshape(2, D)
    hg_gn_full = jnp.transpose(pieces[2], (1, 0, 2)).reshape(2, D)
    norm_g_full = jnp.transpose(pieces[3], (1, 2, 0, 3)).reshape(DEPTH, 2, D)
    conv_w_full = jnp.transpose(pieces[4], (1, 2, 0, 3)).reshape(DEPTH, 3, 2 * FFN_HIDDEN)

    lb1 = lower_bound_fwd(hg_lb_full, "lower_bound")
    lbs = [jnp.zeros((1, D), f32), lb1]

    ada_b_cols = lax.dynamic_slice(ada_b, (0, me * ADA_COLS), (DEPTH, ADA_COLS)).reshape(DEPTH, 1, ADA_COLS)
    mod_cols = ada_mod(c_all, ada_w, ada_b_cols, "ada_mod")
    (mod_mine,) = all_to_all([jnp.transpose(mod_cols, (1, 0, 2))], "mod_to_examples")
    mod = jnp.transpose(mod_mine, (1, 0, 2)).reshape(DEPTH, 6, 1, D)

    def layer_shards(i):
        j = i // 2
        w_in, w_out = (gm_w_in, gm_w_out) if i % 2 == 0 else (hg_w_in, hg_w_out)
        return [w_in[j].astype(bf16), w_out[j].astype(bf16), ffn_w_up[i].T.astype(bf16), ffn_w_down[i].astype(bf16)]

    def full_rows(g):
        return g.reshape(N_DEV * g.shape[1], g.shape[2])

    carried_by = {
        "in_0": [(0, 1), (0, 3)], "mix_0": [(0, 2)], "up_0": [(1, 0), (1, 1)], "gate_0": [(1, 2)], "down_0": [(1, 3)],
        "in_1": [(2, 0)], "mix_1": [(2, 1), (2, 2), (2, 3)], "up_1": [(3, 0), (3, 1)], "gate_1": [(3, 2)], "down_1": [(3, 3)],
    }
    shards = [layer_shards(i) for i in range(DEPTH)]
    gathered = {}
    gathered[(0, 0)] = w_in_0_all

    def carry(call, site, **kw):
        items = carried_by.get(site, [])
        outs = call(riders=Riders([shards[l][slot] for l, slot in items], True), **kw)
        for item, g in zip(items, outs[len(outs) - len(items):]):
            gathered[item] = g
        return outs[:len(outs) - len(items)]

    saved = []
    weights = []
    xcur = xt
    h = norm_fwd(xcur, norm_g_full[0, 0:1], mod[0, 1], mod[0, 0], "norm1_0")
    for i in range(DEPTH):
        j = i // 2
        sh1, sc1, g1, sh2, sc2, g2 = [mod[i, p] for p in range(6)]
        gn2 = norm_g_full[i, 1:2]
        s = {"x0": xcur, "h": h}
        w_in = gathered[(i, 0)]
        if i % 2 == 0:
            (z,) = carry(functools.partial(mm_nn, h, w_in, bf16, f"gm_in_{i}"), f"in_{i}")
            bs = gm_b_s[j].reshape(GM_HEADS, GM_BLOCK, 1)
            (mixed,) = carry(functools.partial(gm_mix_fwd, z, gm_w_s[j], bs, gm_ln_g[j:j + 1], gm_ln_b[j:j + 1],
                                               f"gm_mix_{i}"), f"mix_{i}")
            s["z"] = z
        else:
            (proj,) = carry(functools.partial(mm_nn, h, w_in, f32, f"hg_in_{i}"), f"in_{i}")
            mixed, states = carry(functools.partial(hg_scan_fwd, proj, lbs[j], hg_gn_full[j:j + 1], f"hg_scan_{i}"),
                                  f"mix_{i}")
            s["proj"], s["states"] = proj, states
        s["mixed"] = mixed
        w_out = full_rows(gathered[(i, 1)])
        y, x1, h2 = carry(functools.partial(mm_nn_residual, mixed, w_out, xcur, g1, (gn2, sc2, sh2), f"mix_out_{i}"),
                          f"out_{i}")
        s["y"], s["x1"] = y, x1
        w_up = gathered[(i, 2)]
        (a,) = carry(functools.partial(mm_nn, h2, w_up, bf16, f"ffn_up_{i}", transposed=True), f"up_{i}")
        (hid,) = carry(functools.partial(ffn_gate_fwd, a, conv_w_full[i], ffn_conv_b[i:i + 1], f"ffn_gate_{i}"), f"gate_{i}")
        w_down = full_rows(gathered[(i, 3)])
        next_norm = (norm_g_full[i + 1, 0:1], mod[i + 1, 1], mod[i + 1, 0]) if i + 1 < DEPTH else None
        outs = carry(functools.partial(mm_nn_residual, hid, w_down, x1, g2, next_norm, f"ffn_down_{i}"), f"down_{i}")
        fo, x2 = outs[0], outs[1]
        s["h2"], s["a"], s["hid"], s["f"] = h2, a, hid, fo
        weights.append((w_in, w_out, w_up, w_down))
        saved.append(s)
        xcur = x2
        h = outs[2] if next_norm is not None else None

    loss_part, dx, d_final_g, dg2, df = loss_head(xcur, final_g.reshape(1, D), loss_target[0], saved[-1]["f"],
                                                  mod[DEPTH - 1, 5], "loss_head")
    loss = lax.psum(loss_part[0, 0], ("x", "y", "c"))

    def halves(blocked):
        rows = blocked.shape[1] // 2
        return [(blocked, (0, rows)), (blocked, (rows, rows))]

    def by_owner_rows(dw):
        k, n = dw.shape
        return dw.reshape(N_DEV, k // N_DEV, n)

    received = [[[] for _ in range(4)] for _ in range(DEPTH)]

    def send(call, items, **kw):
        outs = call(riders=Riders([arr for arr, _ in items], False), **kw)
        for (_, (layer, slot)), got in zip(items, outs[len(outs) - len(items):]):
            received[layer][slot].append(got)
        return outs[:len(outs) - len(items)]

    dmod = [None] * DEPTH
    d_norm_g = [None] * DEPTH
    d_gm = {k: [None, None] for k in ("ws", "bs", "lng", "lnb")}
    d_hg = {k: [None, None] for k in ("lb", "gn")}
    d_ffn = {k: [None] * DEPTH for k in ("cw", "cb")}
    in_halves = []
    for i in reversed(range(DEPTH)):
        j = i // 2
        s = saved[i]
        w_in, w_out, w_up, w_down = weights[i]
        sh1, sc1, g1, sh2, sc2, g2 = [mod[i, p] for p in range(6)]
        gn1, gn2 = norm_g_full[i, 0:1], norm_g_full[i, 1:2]
        (dw_down,) = mm_tn(s["hid"], df, bf16, f"dw_down_{i}", tn=D)
        scan_carries = i % 2 == 1
        down_item = (by_owner_rows(dw_down), (i, 3))
        (dhid,) = mm_nt(df, w_down, bf16, f"dhid_{i}")
        da, d_ffn["cw"][i], d_ffn["cb"][i] = send(
            functools.partial(ffn_gate_bwd, s["a"], conv_w_full[i], ffn_conv_b[i:i + 1], dhid, f"ffn_gate_bwd_{i}"),
            in_halves)
        (dw_up_t,) = send(functools.partial(mm_tn, da, s["h2"], bf16, f"dw_up_{i}", tn=D), [] if scan_carries else [down_item])
        dw_up_t = dw_up_t.reshape(N_DEV, -1, D)
        up_halves = [(dw_up_t, (i, 2))] if scan_carries else [(part, (i, 2)) for part in halves(dw_up_t)]
        (dh2,) = send(functools.partial(mm_nt, da, w_up, bf16, f"dh2_{i}", transposed=True),
                      [] if scan_carries else up_halves[:1])
        dx1, dgn2, dsc2, dsh2, dg1, dy = norm_bwd(s["x1"], gn2, sc2, sh2, dh2, dx, f"norm2_bwd_{i}", gate=(s["y"], g1))
        (dw_out,) = mm_tn(s["mixed"], dy, bf16, f"dw_mix_out_{i}")
        (dmixed,) = mm_nt(dy, w_out, bf16, f"dmixed_{i}")
        if i % 2 == 0:
            bs = gm_b_s[j].reshape(GM_HEADS, GM_BLOCK, 1)
            dpre, dws, dbs, dlng, dlnb = send(
                functools.partial(gm_mix_bwd, s["z"], gm_w_s[j], bs, gm_ln_g[j:j + 1], gm_ln_b[j:j + 1], dmixed,
                                  f"gm_mix_bwd_{i}"), up_halves[1:])
            d_gm["ws"][j], d_gm["bs"][j], d_gm["lng"][j], d_gm["lnb"][j] = dws, dbs.reshape(GM_HEADS, GM_BLOCK), dlng, dlnb
        else:
            dpre, dlb, dgn = send(
                functools.partial(hg_scan_bwd, s["proj"], lbs[j], hg_gn_full[j:j + 1], s["states"], dmixed,
                                  f"hg_scan_bwd_{i}"), [down_item] + up_halves)
            d_hg["lb"][j], d_hg["gn"][j] = dlb, dgn
        (dw_in,) = send(functools.partial(mm_tn_by_owner, s["h"], dpre, f"dw_mix_in_{i}", tm=D), [(by_owner_rows(dw_out), (i, 1))])
        in_halves = [(part, (i, 0)) for part in halves(dw_in)]
        (dh,) = send(functools.partial(mm_nt, dpre, w_in, bf16, f"dh_mix_{i}"), in_halves[:1] if i == 0 else [])
        dmod_i = [None, None, dg1, dsh2, dsc2, dg2]
        if i > 0:
            dx, dgn1, dsc1, dsh1, dg2, df = norm_bwd(s["x0"], gn1, sc1, sh1, dh, dx1, f"norm1_bwd_{i}",
                                                     gate=(saved[i - 1]["f"], mod[i - 1, 5]))
        else:
            dx, dgn1, dsc1, dsh1 = send(functools.partial(norm_bwd, s["x0"], gn1, sc1, sh1, dh, dx1, f"norm1_bwd_{i}"),
                                        in_halves[1:])
        dmod_i[0], dmod_i[1] = dsh1, dsc1
        dmod[i] = jnp.concatenate(dmod_i, axis=1)
        d_norm_g[i] = jnp.concatenate([dgn1, dgn2], axis=0)
    grad_x = dx.reshape(1, t, D)

    (dmod_all,) = all_gather([jnp.concatenate(dmod, axis=0)], "gather_dmod")
    dmod_cols = jnp.transpose(lax.dynamic_slice(dmod_all, (0, 0, me * ADA_COLS), (N_DEV, DEPTH, ADA_COLS)), (1, 0, 2))
    g_ada_w, g_ada_b = ada_grads(c_all, dmod_cols, dmod_all.reshape(N_DEV, DEPTH, 1, 6 * D), "ada_grads")
    g_ada_b = g_ada_b.reshape(DEPTH, 6 * D)

    small_partials = [jnp.concatenate(d_gm["lng"], axis=0), jnp.concatenate(d_gm["lnb"], axis=0),
                      jnp.stack(d_gm["ws"]), jnp.stack(d_gm["bs"]), jnp.concatenate(d_ffn["cb"], axis=0),
                      d_final_g, d_hg["lb"][1], jnp.concatenate(d_hg["gn"], axis=0), jnp.stack(d_norm_g),
                      jnp.stack(d_ffn["cw"])]
    partial_shapes = [p.shape for p in small_partials]
    packed = _pack(small_partials, rows_mult=8 * N_DEV)
    rows = packed.shape[0] // N_DEV
    (recv,) = all_to_all([packed.reshape(N_DEV, rows, LANES)], "small_grads_exchange")
    (summed,) = all_gather([sum_parts(recv, "small_grads_sum")], "small_grads_gather")

    def parts_of(slot, layers):
        return [received[i][slot] for i in layers]

    def swapped(a):
        return jnp.swapaxes(a, 1, 2)

    w_shards = [gm_w_in, gm_w_out, hg_w_in, hg_w_out, swapped(ffn_w_up), ffn_w_down]
    big_parts = [parts_of(0, (0, 2)), parts_of(1, (0, 2)), parts_of(0, (1, 3)), parts_of(1, (1, 3)),
                 parts_of(2, range(DEPTH)), parts_of(3, range(DEPTH))]
    big_m = [m_gm_w_in, m_gm_w_out, m_hg_w_in, m_hg_w_out, swapped(m_ffn_w_up), m_ffn_w_down]
    big_v = [v_gm_w_in, v_gm_w_out, v_hg_w_in, v_hg_w_out, swapped(v_ffn_w_up), v_ffn_w_down]
    big = [adam_reduced(parts, w, m_, v_, f"adam_big_{idx}")
           for idx, (w, m_, v_, parts) in enumerate(zip(w_shards, big_m, big_v, big_parts))]
    big[4] = [swapped(o) for o in big[4]]
    (g_gm_w_in, d_gm_w_in, nm_gm_w_in, nv_gm_w_in), (g_gm_w_out, d_gm_w_out, nm_gm_w_out, nv_gm_w_out), \
        (g_hg_w_in, d_hg_w_in, nm_hg_w_in, nv_hg_w_in), (g_hg_w_out, d_hg_w_out, nm_hg_w_out, nv_hg_w_out), \
        (g_ffn_w_up, d_ffn_w_up, nm_ffn_w_up, nv_ffn_w_up), (g_ffn_w_down, d_ffn_w_down, nm_ffn_w_down, nv_ffn_w_down) = big

    g_ln_g, g_ln_b, g_ws, g_bs, g_cb, g_final, g_lb1, g_gn, g_norm, g_cw = _unpack(summed.reshape(-1), partial_shapes)
    g_final = g_final.reshape(D)

    def my_cols(a, n):
        start = (0,) * (a.ndim - 1) + (me * n,)
        return lax.dynamic_slice(a, start, a.shape[:-1] + (n,))

    g_hg_lb = lower_bound_bwd(hg_lb, my_cols(g_lb1, HG_DIM), "lower_bound_bwd")
    g_hg_gn = my_cols(g_gn, HG_DIM)
    g_norm_g = my_cols(g_norm, HG_DIM)
    g_conv_w = my_cols(g_cw, 2 * FFN_HIDDEN // N_DEV)

    two_d = (-1, ADA_COLS)
    d_ada_w, nm_ada_w, nv_ada_w = [o.reshape(ada_w.shape) for o in adam_plain(
        g_ada_w.reshape(two_d), ada_w.reshape(two_d), m_ada_w.reshape(two_d), v_ada_w.reshape(two_d), "adam_ada_w")]

    small_g = [g_ln_g, g_ln_b, g_ws, g_bs, g_cb, g_ada_b, g_final, g_hg_lb, g_hg_gn, g_norm_g, g_conv_w]
    small_w = [gm_ln_g, gm_ln_b, gm_w_s, gm_b_s, ffn_conv_b, ada_b, final_g, hg_lb, hg_gn_g, norm_g, ffn_conv_w]
    small_m = [m_gm_ln_g, m_gm_ln_b, m_gm_w_s, m_gm_b_s, m_ffn_conv_b, m_ada_b, m_final_g, m_hg_lb, m_hg_gn_g, m_norm_g, m_ffn_conv_w]
    small_v = [v_gm_ln_g, v_gm_ln_b, v_gm_w_s, v_gm_b_s, v_ffn_conv_b, v_ada_b, v_final_g, v_hg_lb, v_hg_gn_g, v_norm_g, v_ffn_conv_w]
    shapes = [w.shape for w in small_w]
    small_g = [g.reshape(s) for g, s in zip(small_g, shapes)]
    outs = adam_plain(_pack(small_g), _pack(small_w), _pack(small_m), _pack(small_v), "adam_small")
    (d_ln_g, d_ln_b, d_ws, d_bs, d_cb, d_ada_b, d_final, d_hg_lb, d_hg_gn, d_norm_g_, d_conv_w), \
        (nm_ln_g, nm_ln_b, nm_ws, nm_bs, nm_cb, nm_ada_b, nm_final, nm_hg_lb, nm_hg_gn, nm_norm_g, nm_conv_w), \
        (nv_ln_g, nv_ln_b, nv_ws, nv_bs, nv_cb, nv_ada_b, nv_final, nv_hg_lb, nv_hg_gn, nv_norm_g, nv_conv_w) = [
            _unpack(o.reshape(-1), shapes) for o in outs]
    g_ln_g, g_ln_b, g_ws, g_bs, g_cb, g_ada_b, g_final, g_hg_lb, g_hg_gn, g_norm_g, g_conv_w = small_g

    grads = (g_gm_w_in, g_ln_g, g_ln_b, g_ws, g_bs, g_gm_w_out, g_hg_w_in, g_hg_lb, g_hg_gn, g_hg_w_out,
             g_ffn_w_up, g_conv_w, g_cb, g_ffn_w_down, g_norm_g, g_ada_w, g_ada_b, g_final)
    deltas = (d_gm_w_in, d_ln_g, d_ln_b, d_ws, d_bs, d_gm_w_out, d_hg_w_in, d_hg_lb, d_hg_gn, d_hg_w_out,
              d_ffn_w_up, d_conv_w, d_cb, d_ffn_w_down, d_norm_g_, d_ada_w, d_ada_b, d_final)
    new_m = (nm_gm_w_in, nm_ln_g, nm_ln_b, nm_ws, nm_bs, nm_gm_w_out, nm_hg_w_in, nm_hg_lb, nm_hg_gn, nm_hg_w_out,
             nm_ffn_w_up, nm_conv_w, nm_cb, nm_ffn_w_down, nm_norm_g, nm_ada_w, nm_ada_b, nm_final)
    new_v = (nv_gm_w_in, nv_ln_g, nv_ln_b, nv_ws, nv_bs, nv_gm_w_out, nv_hg_w_in, nv_hg_lb, nv_hg_gn, nv_hg_w_out,
             nv_ffn_w_up, nv_conv_w, nv_cb, nv_ffn_w_down, nv_norm_g, nv_ada_w, nv_ada_b, nv_final)
    return (loss, grad_x) + grads + deltas + new_m + new_v
```

```python
import functools
import math

import jax
import jax.numpy as jnp
from jax import lax
from jax.experimental import pallas as pl
from jax.experimental.pallas import tpu as pltpu

f32 = jnp.float32
bf16 = jnp.bfloat16
MESH = pl.DeviceIdType.MESH

N_DEV = 8
D = 1024
DEPTH = 4
EPS = 1e-6
GM_WIDTH = 2048
GM_HEADS = 8
GM_HEAD_DIM = 256
GM_BLOCK = 128
CHUNK = 64
HG_HEADS = 8
HG_DIM = 128
FFN_HIDDEN = 2816
ADA_COLS = 6 * D // N_DEV

HG_SUB_FWD = 64
HG_SUB_BWD = 32
HG_PAIR = 8
HG_TOKENS = 128

ADAM_LR = 0.001
ADAM_B1 = 0.9
ADAM_B2 = 0.999
ADAM_EPS = 1e-08
ADAM_WD = 0.01
ADAM_STEP = 10

V7X_VMEM_LIMIT = 56 * 1024 * 1024
LANES = 128


def _cparams(*sem):
    return pltpu.CompilerParams(dimension_semantics=sem or None, vmem_limit_bytes=V7X_VMEM_LIMIT)


def _tile(n, target, mult=LANES):
    best = None
    for t in range(mult, min(n, target) + 1, mult):
        if n % t == 0:
            best = t
    return best or n


WEIGHT_BLOCK_BYTES = 6 * 1024 * 1024


def _weight_tile(n, k):
    return _tile(n, max(LANES, WEIGHT_BLOCK_BYTES // (2 * k)))


def _gelu(x):
    return 0.5 * x * (1.0 + lax.erf(x * (1.0 / math.sqrt(2.0))))


def _mesh_pos():
    return lax.axis_index("x"), lax.axis_index("y"), lax.axis_index("c")


def _flat(pos):
    return 4 * pos[0] + 2 * pos[1] + pos[2]


def _peer(pos, k):
    return ((1 - pos[0]) if k & 4 else pos[0], (1 - pos[1]) if k & 2 else pos[1], (1 - pos[2]) if k & 1 else pos[2])


def _exchange_copies(ins, outs, send_sems, recv_sems, local_sems, gather):
    pos = _mesh_pos()
    me = _flat(pos)

    def src(i, dest):
        if gather:
            return ins[i]
        ref, rows = ins[i] if isinstance(ins[i], tuple) else (ins[i], None)
        return ref.at[dest] if rows is None else ref.at[dest, pl.ds(*rows)]

    local = [pltpu.make_async_copy(src(i, me), outs[i].at[me], local_sems.at[i]) for i in range(len(ins))]
    sends, recvs = [], []
    for k in range(1, N_DEV):
        peer = _peer(pos, k)
        there = _flat(peer)
        for i in range(len(ins)):
            sems = dict(send_sem=send_sems.at[i * 7 + k - 1], recv_sem=recv_sems.at[i * 7 + k - 1],
                        device_id=peer, device_id_type=MESH)
            sends.append(pltpu.make_async_remote_copy(src_ref=src(i, there), dst_ref=outs[i].at[me], **sems))
            recvs.append(pltpu.make_async_remote_copy(src_ref=src(i, there), dst_ref=outs[i].at[there], **sems))
    return local, sends, recvs


def _exchange_start(*refs):
    local, sends, _ = _exchange_copies(*refs)
    for cp in local + sends:
        cp.start()


def _exchange_wait(*refs):
    local, sends, recvs = _exchange_copies(*refs)
    for cp in recvs:
        cp.wait_recv()
    for cp in sends:
        cp.wait_send()
    for cp in local:
        cp.wait()


OTHER_CHIPS = (2, 4, 6)


def _relay_copies(ins, outs, send_sems, recv_sems, local_sems):
    pos = _mesh_pos()
    me = _flat(pos)
    sibling = _peer(pos, 1)
    local = [pltpu.make_async_copy(ins[i], outs[i].at[me], local_sems.at[i]) for i in range(len(ins))]
    first, passes, recvs = [], {k: [] for k in OTHER_CHIPS}, {k: [] for k in range(1, N_DEV)}
    for i in range(len(ins)):
        def copy(k, src, block, to):
            return pltpu.make_async_remote_copy(
                src_ref=src, dst_ref=outs[i].at[block], send_sem=send_sems.at[i * 7 + k - 1],
                recv_sem=recv_sems.at[i * 7 + k - 1], device_id=to, device_id_type=MESH)

        for k in (1,) + OTHER_CHIPS:
            first.append(copy(k, ins[i], me, _peer(pos, k)))
        for k in OTHER_CHIPS:
            there = _flat(_peer(pos, k))
            passes[k].append(copy(k ^ 1, outs[i].at[there], there, sibling))
        for k in range(1, N_DEV):
            there = _flat(_peer(pos, k))
            recvs[k].append(copy(k, ins[i], there, _peer(pos, k)))
    return local, first, passes, recvs


def _relay_start(ins, outs, *sems):
    local, first, _, _ = _relay_copies(ins, outs, *sems)
    for cp in local + first:
        cp.start()


def _relay_wait(ins, outs, *sems):
    local, first, passes, recvs = _relay_copies(ins, outs, *sems)
    for k in OTHER_CHIPS:
        for cp in recvs[k]:
            cp.wait_recv()
        for cp in passes[k]:
            cp.start()
    for k in (1, 3, 5, 7):
        for cp in recvs[k]:
            cp.wait_recv()
    for cp in first + [cp for k in OTHER_CHIPS for cp in passes[k]]:
        cp.wait_send()
    for cp in local:
        cp.wait()


def _exchange_out_shape(a, gather):
    return jax.ShapeDtypeStruct((N_DEV,) + tuple(a.shape) if gather else tuple(a.shape), a.dtype)


def _exchange_sems(n):
    return [pltpu.SemaphoreType.DMA((7 * n,)), pltpu.SemaphoreType.DMA((7 * n,)), pltpu.SemaphoreType.DMA((n,))]


ANY = pl.BlockSpec(memory_space=pl.ANY)


def _exchange(arrs, gather, name, relay=False):
    n = len(arrs)

    def body(*refs):
        ins, outs = refs[:n], refs[n:2 * n]
        if relay:
            _relay_start(ins, outs, *refs[2 * n:])
            _relay_wait(ins, outs, *refs[2 * n:])
        else:
            _exchange_start(ins, outs, *refs[2 * n:], gather)
            _exchange_wait(ins, outs, *refs[2 * n:], gather)

    return pl.pallas_call(
        body, name=name, out_shape=tuple(_exchange_out_shape(a, gather) for a in arrs),
        in_specs=[ANY] * n, out_specs=tuple([ANY] * n), scratch_shapes=_exchange_sems(n),
    )(*arrs)


def all_gather(arrs, name, relay=False):
    return _exchange(arrs, True, name, relay)


def all_to_all(arrs, name):
    return _exchange(arrs, False, name)


class Riders:
    def __init__(self, arrs, gather):
        self.gather = gather
        self.rows = [a[1] if isinstance(a, tuple) else None for a in arrs]
        self.arrs = [a[0] if isinstance(a, tuple) else a for a in arrs]

    def out_shapes(self):
        shapes = []
        for a, rows in zip(self.arrs, self.rows):
            shape = tuple(a.shape) if rows is None else (a.shape[0], rows[1], a.shape[2])
            shapes.append(jax.ShapeDtypeStruct((N_DEV,) + shape if self.gather else shape, a.dtype))
        return shapes


def _call(body, *, name, grid, in_specs, out_specs, out_shape, semantics, scratch_shapes=(), riders=None):
    if riders is None or not riders.arrs:
        return pl.pallas_call(body, name=name, grid=grid, in_specs=in_specs, out_specs=tuple(out_specs),
                              out_shape=tuple(out_shape), scratch_shapes=list(scratch_shapes),
                              compiler_params=_cparams(*semantics))
    n_in, n_out, n_scr, n_r = len(in_specs), len(out_specs), len(scratch_shapes), len(riders.arrs)
    gather = riders.gather

    def hosted(*refs):
        ins, r_ins = refs[:n_in], refs[n_in:n_in + n_r]
        at = n_in + n_r
        outs, r_outs = refs[at:at + n_out], refs[at + n_out:at + n_out + n_r]
        at += n_out + n_r
        scratch, sems = refs[at:at + n_scr], refs[at + n_scr:]
        first = functools.reduce(jnp.logical_and, [pl.program_id(a) == 0 for a in range(len(grid))])
        last = functools.reduce(jnp.logical_and, [pl.program_id(a) == grid[a] - 1 for a in range(len(grid))])

        r_ins = [(ref, rows) if rows is not None else ref for ref, rows in zip(r_ins, riders.rows)]

        @pl.when(first)
        def _():
            if gather:
                _relay_start(r_ins, r_outs, *sems)
            else:
                _exchange_start(r_ins, r_outs, *sems, gather)

        body(*ins, *outs, *scratch)

        @pl.when(last)
        def _():
            if gather:
                _relay_wait(r_ins, r_outs, *sems)
            else:
                _exchange_wait(r_ins, r_outs, *sems, gather)

    call = pl.pallas_call(
        hosted, name=name, grid=grid, in_specs=list(in_specs) + [ANY] * n_r, out_specs=tuple(out_specs) + (ANY,) * n_r,
        out_shape=tuple(out_shape) + tuple(riders.out_shapes()),
        scratch_shapes=list(scratch_shapes) + _exchange_sems(n_r),
        compiler_params=_cparams(*(("arbitrary",) * len(grid))))
    return lambda *args: call(*args, *riders.arrs)


def _shards_per_step(shape):
    _, k, n = shape
    best = None
    for q in (1, 2, 4, 8):
        if (q * n) % LANES == 0 and (best is None or 2 * k * q * n <= WEIGHT_BLOCK_BYTES):
            best = q
    return best


def mm_nn(a, b, out_dtype, name, tm=512, riders=None, transposed=False):
    m, k = a.shape
    tm = _tile(m, tm, 8)
    if b.ndim == 3:
        shard = b.shape[1] if transposed else b.shape[2]
        n = N_DEV * shard
        per_step = _shards_per_step((N_DEV, k, shard))
        tn = per_step * shard
        b_spec = pl.BlockSpec((per_step,) + b.shape[1:], lambda j, i: (j, 0, 0))
        contract = (((1,), (1,)), ((), ())) if transposed else (((1,), (0,)), ((), ()))

        def body(a_ref, b_ref, o_ref):
            for q in range(per_step):
                o_ref[:, q * shard:(q + 1) * shard] = lax.dot_general(
                    a_ref[...], b_ref[q], contract, preferred_element_type=f32).astype(o_ref.dtype)
    else:
        n = b.shape[1]
        tn = _weight_tile(n, k)
        b_spec = pl.BlockSpec((k, tn), lambda j, i: (0, j))

        def body(a_ref, b_ref, o_ref):
            o_ref[...] = jnp.dot(a_ref[...], b_ref[...], preferred_element_type=f32).astype(o_ref.dtype)

    return _call(
        body, name=name, grid=(n // tn, m // tm),
        in_specs=[pl.BlockSpec((tm, k), lambda j, i: (i, 0)), b_spec],
        out_specs=[pl.BlockSpec((tm, tn), lambda j, i: (i, j))],
        out_shape=[jax.ShapeDtypeStruct((m, n), out_dtype)], semantics=("parallel", "parallel"), riders=riders,
    )(a, b)


def mm_nn_residual(a, b, x, gate, norm, name, tm=512, riders=None):
    m, k = a.shape
    n = b.shape[1]
    tm = _tile(m, tm, 8)

    def body(a_ref, b_ref, x_ref, g_ref, *rest):
        if norm is not None:
            gn_ref, sc_ref, sh_ref, y_ref, o_ref, h_ref = rest
        else:
            y_ref, o_ref = rest
        y = jnp.dot(a_ref[...], b_ref[...], preferred_element_type=f32)
        y_ref[...] = y.astype(bf16)
        x_new = x_ref[...] + g_ref[...] * y
        o_ref[...] = x_new
        if norm is not None:
            h_ref[...] = _norm_fn(x_new, gn_ref[...], sc_ref[...], sh_ref[...]).astype(bf16)

    blk = pl.BlockSpec((tm, n), lambda i: (i, 0))
    in_specs = [pl.BlockSpec((tm, k), lambda i: (i, 0)), pl.BlockSpec((k, n), lambda i: (0, 0)), blk, _row(n)]
    out_specs = [blk, blk]
    out_shape = [jax.ShapeDtypeStruct((m, n), bf16), jax.ShapeDtypeStruct((m, n), f32)]
    args = [a, b, x, gate]
    if norm is not None:
        in_specs += [_row(n)] * 3
        out_specs += [blk]
        out_shape += [jax.ShapeDtypeStruct((m, n), bf16)]
        args += list(norm)
    return _call(body, name=name, grid=(m // tm,), in_specs=in_specs, out_specs=out_specs, out_shape=out_shape,
                 semantics=("parallel",), riders=riders)(*args)


def mm_nt(a, b, out_dtype, name, tm=512, riders=None, transposed=False):
    m = a.shape[0]
    tm = _tile(m, tm, 8)
    if b.ndim == 3:
        shard, k = (b.shape[1], b.shape[2]) if transposed else (b.shape[2], b.shape[1])
        tk = k
        b_spec = pl.BlockSpec(b.shape, lambda i, j: (0, 0, 0))
        width = N_DEV * shard
        contract = (((1,), (0,)), ((), ())) if transposed else (((1,), (1,)), ((), ()))

        def body(a_ref, b_ref, o_ref):
            acc = None
            for q in range(N_DEV):
                part = lax.dot_general(a_ref[:, q * shard:(q + 1) * shard], b_ref[q], contract, preferred_element_type=f32)
                acc = part if acc is None else acc + part
            o_ref[...] = acc.astype(o_ref.dtype)
    else:
        k, width = b.shape
        tk = _weight_tile(k, width)
        b_spec = pl.BlockSpec((tk, width), lambda i, j: (j, 0))

        def body(a_ref, b_ref, o_ref):
            o_ref[...] = lax.dot_general(a_ref[...], b_ref[...], (((1,), (1,)), ((), ())),
                                         preferred_element_type=f32).astype(o_ref.dtype)

    return _call(
        body, name=name, grid=(m // tm, k // tk),
        in_specs=[pl.BlockSpec((tm, width), lambda i, j: (i, 0)), b_spec],
        out_specs=[pl.BlockSpec((tm, tk), lambda i, j: (i, j))],
        out_shape=[jax.ShapeDtypeStruct((m, k), out_dtype)], semantics=("parallel", "parallel"), riders=riders,
    )(a, b)


def mm_tn(a, b, out_dtype, name, tm=512, tn=512, riders=None):
    t, m = a.shape
    n = b.shape[1]
    tm, tn = _tile(m, tm), _tile(n, tn)

    def body(a_ref, b_ref, o_ref):
        o_ref[...] = lax.dot_general(a_ref[...], b_ref[...], (((0,), (0,)), ((), ())),
                                     preferred_element_type=f32).astype(o_ref.dtype)

    return _call(
        body, name=name, grid=(m // tm, n // tn),
        in_specs=[pl.BlockSpec((t, tm), lambda i, j: (0, i)), pl.BlockSpec((t, tn), lambda i, j: (0, j))],
        out_specs=[pl.BlockSpec((tm, tn), lambda i, j: (i, j))],
        out_shape=[jax.ShapeDtypeStruct((m, n), out_dtype)], semantics=("parallel", "parallel"), riders=riders,
    )(a, b)


def mm_tn_by_owner(a, b, name, tm=512, riders=None):
    t, m = a.shape
    n = b.shape[1]
    shard = n // N_DEV
    per_step = 1 if shard % LANES == 0 else 2
    assert (per_step * shard) % LANES == 0
    tm = _tile(m, tm)

    def body(a_ref, b_ref, o_ref):
        acc = lax.dot_general(a_ref[...], b_ref[...], (((0,), (0,)), ((), ())), preferred_element_type=f32)
        for q in range(per_step):
            o_ref[q] = acc[:, q * shard:(q + 1) * shard].astype(bf16)

    return _call(
        body, name=name, grid=(m // tm, N_DEV // per_step),
        in_specs=[pl.BlockSpec((t, tm), lambda i, j: (0, i)), pl.BlockSpec((t, per_step * shard), lambda i, j: (0, j))],
        out_specs=[pl.BlockSpec((per_step, tm, shard), lambda i, j: (j, i, 0))],
        out_shape=[jax.ShapeDtypeStruct((N_DEV, m, shard), bf16)], semantics=("parallel", "parallel"), riders=riders,
    )(a, b)


def _norm_fn(x, gn, sc, sh):
    r = lax.rsqrt(jnp.mean(x * x, axis=-1, keepdims=True) + EPS)
    return (x * r * gn) * (1.0 + sc) + sh


def _row(d):
    return pl.BlockSpec((1, d), lambda i: (0, 0))


def norm_fwd(x, gn, sc, sh, name, tm=512):
    t, d = x.shape
    tm = _tile(t, tm, 8)

    def body(x_ref, gn_ref, sc_ref, sh_ref, h_ref):
        h_ref[...] = _norm_fn(x_ref[...], gn_ref[...], sc_ref[...], sh_ref[...]).astype(bf16)

    return pl.pallas_call(
        body, name=name, grid=(t // tm,),
        in_specs=[pl.BlockSpec((tm, d), lambda i: (i, 0)), _row(d), _row(d), _row(d)],
        out_specs=pl.BlockSpec((tm, d), lambda i: (i, 0)),
        out_shape=jax.ShapeDtypeStruct((t, d), bf16), compiler_params=_cparams("parallel"),
    )(x, gn, sc, sh)


def _gate_bwd(dx, y_ref, g_ref, dgate_ref, dy_ref):
    dgate_ref[...] += jnp.sum(dx * y_ref[...].astype(f32), axis=0, keepdims=True)
    dy_ref[...] = (dx * g_ref[...]).astype(bf16)


def norm_bwd(x, gn, sc, sh, dh, dres, name, gate=None, tm=512, riders=None):
    t, d = x.shape
    tm = _tile(t, tm, 8)

    def body(x_ref, gn_ref, sc_ref, sh_ref, dh_ref, dres_ref, *rest):
        if gate is not None:
            y_ref, g_ref, dx_ref, dgn_ref, dsc_ref, dsh_ref, dgate_ref, dy_ref = rest
        else:
            dx_ref, dgn_ref, dsc_ref, dsh_ref = rest

        @pl.when(pl.program_id(0) == 0)
        def _():
            dgn_ref[...] = jnp.zeros_like(dgn_ref)
            dsc_ref[...] = jnp.zeros_like(dsc_ref)
            dsh_ref[...] = jnp.zeros_like(dsh_ref)
            if gate is not None:
                dgate_ref[...] = jnp.zeros_like(dgate_ref)

        _, vjp = jax.vjp(_norm_fn, x_ref[...], gn_ref[...], sc_ref[...], sh_ref[...])
        dx, dgn, dsc, dsh = vjp(dh_ref[...].astype(f32))
        dx = dx + dres_ref[...]
        dx_ref[...] = dx
        dgn_ref[...] += dgn
        dsc_ref[...] += dsc
        dsh_ref[...] += dsh
        if gate is not None:
            _gate_bwd(dx, y_ref, g_ref, dgate_ref, dy_ref)

    blk = pl.BlockSpec((tm, d), lambda i: (i, 0))
    vec = jax.ShapeDtypeStruct((1, d), f32)
    in_specs = [blk, _row(d), _row(d), _row(d), blk, blk]
    out_specs = [blk, _row(d), _row(d), _row(d)]
    out_shape = [jax.ShapeDtypeStruct((t, d), f32), vec, vec, vec]
    args = [x, gn, sc, sh, dh, dres]
    if gate is not None:
        in_specs += [blk, _row(d)]
        out_specs += [_row(d), blk]
        out_shape += [vec, jax.ShapeDtypeStruct((t, d), bf16)]
        args += list(gate)
    return _call(body, name=name, grid=(t // tm,), in_specs=in_specs, out_specs=out_specs, out_shape=out_shape,
                 semantics=("arbitrary",), riders=riders)(*args)


def _loss_fn(x, g, tgt):
    r = lax.rsqrt(jnp.mean(x * x, axis=-1, keepdims=True) + EPS)
    err = jnp.square(x * r * g - tgt)
    return 0.5 * jnp.sum(jnp.mean(err, axis=-1, keepdims=True), axis=0, keepdims=True)


def loss_head(x, g, tgt, y, gate, name, tm=512):
    t, d = x.shape
    tm = _tile(t, tm, 8)

    def body(x_ref, g_ref, t_ref, y_ref, gate_ref, loss_ref, dx_ref, dg_ref, dgate_ref, dy_ref):
        @pl.when(pl.program_id(0) == 0)
        def _():
            loss_ref[...] = jnp.zeros_like(loss_ref)
            dg_ref[...] = jnp.zeros_like(dg_ref)
            dgate_ref[...] = jnp.zeros_like(dgate_ref)

        loss, vjp = jax.vjp(_loss_fn, x_ref[...], g_ref[...], t_ref[...])
        dx, dg, _ = vjp(jnp.ones((1, 1), f32))
        dx_ref[...] = dx
        loss_ref[...] += loss
        dg_ref[...] += dg
        _gate_bwd(dx, y_ref, gate_ref, dgate_ref, dy_ref)

    blk = pl.BlockSpec((tm, d), lambda i: (i, 0))
    vec = jax.ShapeDtypeStruct((1, d), f32)
    return pl.pallas_call(
        body, name=name, grid=(t // tm,),
        in_specs=[blk, _row(d), blk, blk, _row(d)],
        out_specs=(pl.BlockSpec((1, 1), lambda i: (0, 0)), blk, _row(d), _row(d), blk),
        out_shape=(jax.ShapeDtypeStruct((1, 1), f32), jax.ShapeDtypeStruct((t, d), f32), vec, vec,
                   jax.ShapeDtypeStruct((t, d), bf16)),
        compiler_params=_cparams("arbitrary"),
    )(x, g, tgt, y, gate)


def _gm_block_fn(z, ws, bs, lng, lnb):
    u = _gelu(z[:, :GM_WIDTH])
    vg = _gelu(z[:, GM_WIDTH:])
    mu = jnp.mean(vg, axis=-1, keepdims=True)
    var = jnp.mean(jnp.square(vg - mu), axis=-1, keepdims=True)
    vn = (vg - mu) * lax.rsqrt(var + EPS) * lng + lnb
    row = lax.broadcasted_iota(jnp.int32, (GM_BLOCK, GM_BLOCK), 0) // CHUNK
    col = lax.broadcasted_iota(jnp.int32, (GM_BLOCK, GM_BLOCK), 1) // CHUNK
    parts = []
    for h in range(GM_HEADS):
        w = jnp.where(row >= col, ws[h], 0.0)
        cols = slice(h * GM_HEAD_DIM, (h + 1) * GM_HEAD_DIM)
        s = jnp.dot(w.astype(bf16), vn[:, cols].astype(bf16), preferred_element_type=f32) + bs[h]
        parts.append(u[:, cols] * s)
    return jnp.concatenate(parts, axis=1)


def _gm_param_specs():
    return [pl.BlockSpec((GM_HEADS, GM_BLOCK, GM_BLOCK), lambda i: (0, 0, 0)),
            pl.BlockSpec((GM_HEADS, GM_BLOCK, 1), lambda i: (0, 0, 0)), _row(GM_WIDTH), _row(GM_WIDTH)]


def gm_mix_fwd(z, ws, bs, lng, lnb, name, riders=None):
    t = z.shape[0]

    def body(z_ref, ws_ref, bs_ref, lng_ref, lnb_ref, o_ref):
        o_ref[...] = _gm_block_fn(z_ref[...].astype(f32), ws_ref[...], bs_ref[...], lng_ref[...],
                                  lnb_ref[...]).astype(bf16)

    return _call(
        body, name=name, grid=(t // GM_BLOCK,),
        in_specs=[pl.BlockSpec((GM_BLOCK, 2 * GM_WIDTH), lambda i: (i, 0))] + _gm_param_specs(),
        out_specs=[pl.BlockSpec((GM_BLOCK, GM_WIDTH), lambda i: (i, 0))],
        out_shape=[jax.ShapeDtypeStruct((t, GM_WIDTH), bf16)], semantics=("parallel",), riders=riders,
    )(z, ws, bs, lng, lnb)


def gm_mix_bwd(z, ws, bs, lng, lnb, dgated, name, riders=None):
    t = z.shape[0]

    def body(z_ref, ws_ref, bs_ref, lng_ref, lnb_ref, dg_ref, dz_ref, dws_ref, dbs_ref, dlng_ref, dlnb_ref):
        _, vjp = jax.vjp(_gm_block_fn, z_ref[...].astype(f32), ws_ref[...], bs_ref[...], lng_ref[...], lnb_ref[...])
        dz, dws, dbs, dlng, dlnb = vjp(dg_ref[...].astype(f32))
        dz_ref[...] = dz.astype(bf16)

        @pl.when(pl.program_id(0) == 0)
        def _():
            dws_ref[...] = jnp.zeros_like(dws_ref)
            dbs_ref[...] = jnp.zeros_like(dbs_ref)
            dlng_ref[...] = jnp.zeros_like(dlng_ref)
            dlnb_ref[...] = jnp.zeros_like(dlnb_ref)

        dws_ref[...] += dws
        dbs_ref[...] += dbs
        dlng_ref[...] += dlng
        dlnb_ref[...] += dlnb

    zblk = pl.BlockSpec((GM_BLOCK, 2 * GM_WIDTH), lambda i: (i, 0))
    return _call(
        body, name=name, grid=(t // GM_BLOCK,),
        in_specs=[zblk] + _gm_param_specs() + [pl.BlockSpec((GM_BLOCK, GM_WIDTH), lambda i: (i, 0))],
        out_specs=[zblk] + _gm_param_specs(),
        out_shape=[jax.ShapeDtypeStruct((t, 2 * GM_WIDTH), bf16),
                   jax.ShapeDtypeStruct((GM_HEADS, GM_BLOCK, GM_BLOCK), f32),
                   jax.ShapeDtypeStruct((GM_HEADS, GM_BLOCK, 1), f32),
                   jax.ShapeDtypeStruct((1, GM_WIDTH), f32), jax.ShapeDtypeStruct((1, GM_WIDTH), f32)],
        semantics=("arbitrary",), riders=riders,
    )(z, ws, bs, lng, lnb, dgated)


@functools.partial(jax.custom_vjp, nondiff_argnums=(1,))
def _rows_up(x, shift):
    return x if shift == 0 else pltpu.roll(x, x.shape[1] - shift, axis=1)


def _rows_up_fwd(x, shift):
    return _rows_up(x, shift), None


def _rows_up_bwd(shift, _, g):
    return (g if shift == 0 else pltpu.roll(g, shift, axis=1),)


_rows_up.defvjp(_rows_up_fwd, _rows_up_bwd)


def _hg_block_fn(sub, qp, fz, iv, gp, s0, lb, gn):
    n, ns, d = sub, HG_TOKENS // sub, HG_DIM
    p, nb, per_sub = HG_PAIR, HG_TOKENS // HG_PAIR, sub // HG_PAIR
    f = lb + (1.0 - lb) * jax.nn.sigmoid(fz)
    g = jnp.log(f)
    k = 1.0 - f
    q = qp * jax.nn.sigmoid(qp)
    v = iv.astype(bf16)
    row = lax.broadcasted_iota(jnp.int32, (HG_TOKENS, HG_TOKENS), 0)
    col = lax.broadcasted_iota(jnp.int32, (HG_TOKENS, HG_TOKENS), 1)
    same_sub = col // n == row // n
    tri = ((col <= row) & same_sub).astype(f32)
    cum = jnp.dot(tri, g, precision=lax.Precision.HIGHEST, preferred_element_type=f32)
    cum_b, q_b, k_b, f_b = cum.reshape(nb, p, d), q.reshape(nb, p, d), k.reshape(nb, p, d), f.reshape(nb, p, d)
    j_b = lax.broadcasted_iota(jnp.int32, (nb, p, d), 1)
    j_col = lax.broadcasted_iota(jnp.int32, (nb, p, 1), 1)
    scores_t = jnp.zeros((HG_TOKENS, HG_TOKENS), f32)
    weight = k_b
    for delta in range(p):
        if delta:
            weight = weight * _rows_up(f_b, delta)
        pair = jnp.sum(_rows_up(q_b, delta) * weight, axis=2, keepdims=True)
        pair = jnp.where(j_col < p - delta, pair, 0.0)
        scores_t = scores_t + jnp.where(col == row + delta, pair.reshape(HG_TOKENS, 1), 0.0)
    o = lax.dot_general(scores_t.astype(bf16), v, (((0,), (0,)), ((), ())), preferred_element_type=f32)
    last = cum_b[:, p - 1:p, :]
    before = jnp.concatenate([jnp.zeros((1, 1, d), f32), last[:-1]], axis=0)
    before = jnp.broadcast_to(before, (nb, p, d)).reshape(HG_TOKENS, d)
    block = (lax.broadcasted_iota(jnp.int32, (HG_TOKENS, d), 0) // p) % per_sub
    q_late = q * jnp.exp(jnp.where(block > 0, cum - before, -1e30))
    last_s = last.reshape(ns, per_sub, d)
    q_parts, k_parts = [], []
    for m in range(1, per_sub):
        split = jnp.broadcast_to(last_s[:, m - 1:m, :], (ns, n, d)).reshape(HG_TOKENS, d)
        k_parts.append(k * jnp.exp(jnp.where(block < m, split - cum, -1e30)))
        q_parts.append(jnp.where(block == m, q_late, 0.0))
    scores = lax.dot_general(jnp.concatenate(q_parts, axis=1).astype(bf16), jnp.concatenate(k_parts, axis=1).astype(bf16),
                             (((1,), (1,)), ((), ())), preferred_element_type=f32)
    o = o + jnp.dot(jnp.where(same_sub, scores, 0.0).astype(bf16), v, preferred_element_type=f32)
    cum_s = cum.reshape(ns, n, d)
    tot = cum_s[:, n - 1:n, :]
    kt_t = (k.reshape(ns, n, d) * jnp.exp(tot - cum_s)).reshape(HG_TOKENS, d).T
    lane_sub = lax.broadcasted_iota(jnp.int32, (d, HG_TOKENS), 1) // n
    k_by_sub = jnp.concatenate([jnp.where(lane_sub == b, kt_t, 0.0) for b in range(ns)], axis=0).astype(bf16)
    update = jnp.dot(k_by_sub, v, preferred_element_type=f32)
    decay = jnp.exp(tot.reshape(ns, d)).T
    state = s0
    states = []
    for a in range(ns):
        states.append(state.astype(bf16))
        state = decay[:, a:a + 1] * state + update[a * d:(a + 1) * d]
    qt = q * jnp.exp(cum)
    row_sub = lax.broadcasted_iota(jnp.int32, (HG_TOKENS, d), 0) // n
    q_by_sub = jnp.concatenate([jnp.where(row_sub == a, qt, 0.0) for a in range(ns)], axis=1).astype(bf16)
    o = o + jnp.dot(q_by_sub, jnp.concatenate(states, axis=0), preferred_element_type=f32)
    on = o * lax.rsqrt(jnp.mean(o * o, axis=-1, keepdims=True) + EPS) * gn
    return on * (gp * jax.nn.sigmoid(gp)), state


def _head_parts(ref, h):
    return [ref[:, p * D + h * HG_DIM:p * D + (h + 1) * HG_DIM] for p in range(4)]


def hg_scan_fwd(proj, lb, gn, name, riders=None):
    t = proj.shape[0]
    nt = t // HG_TOKENS

    def body(p_ref, lb_ref, gn_ref, y_ref, s_ref, state):
        @pl.when(pl.program_id(0) == 0)
        def _():
            state[...] = jnp.zeros_like(state)

        for h in range(HG_HEADS):
            cols = slice(h * HG_DIM, (h + 1) * HG_DIM)
            s_ref[h, 0] = state[h]
            y, s1 = _hg_block_fn(HG_SUB_FWD, *_head_parts(p_ref, h), state[h], lb_ref[:, cols], gn_ref[:, cols])
            y_ref[:, cols] = y.astype(bf16)
            state[h] = s1

    return _call(
        body, name=name, grid=(nt,),
        in_specs=[pl.BlockSpec((HG_TOKENS, 4 * D), lambda i: (i, 0)), _row(D), _row(D)],
        out_specs=[pl.BlockSpec((HG_TOKENS, D), lambda i: (i, 0)),
                   pl.BlockSpec((HG_HEADS, 1, HG_DIM, HG_DIM), lambda i: (0, i, 0, 0))],
        out_shape=[jax.ShapeDtypeStruct((t, D), bf16), jax.ShapeDtypeStruct((HG_HEADS, nt, HG_DIM, HG_DIM), f32)],
        scratch_shapes=[pltpu.VMEM((HG_HEADS, HG_DIM, HG_DIM), f32)],
        semantics=("arbitrary",), riders=riders,
    )(proj, lb, gn)


def hg_scan_bwd(proj, lb, gn, states, dy, name, riders=None):
    t = proj.shape[0]
    nt = t // HG_TOKENS

    def body(p_ref, lb_ref, gn_ref, s_ref, dy_ref, dp_ref, dlb_ref, dgn_ref, dstate):
        @pl.when(pl.program_id(0) == 0)
        def _():
            dstate[...] = jnp.zeros_like(dstate)
            dlb_ref[...] = jnp.zeros_like(dlb_ref)
            dgn_ref[...] = jnp.zeros_like(dgn_ref)

        for h in range(HG_HEADS):
            cols = slice(h * HG_DIM, (h + 1) * HG_DIM)
            _, vjp = jax.vjp(functools.partial(_hg_block_fn, HG_SUB_BWD), *_head_parts(p_ref, h), s_ref[h, 0],
                             lb_ref[:, cols], gn_ref[:, cols])
            grads = vjp((dy_ref[:, cols].astype(f32), dstate[h]))
            for p in range(4):
                dp_ref[:, p * D + h * HG_DIM:p * D + (h + 1) * HG_DIM] = grads[p].astype(bf16)
            dstate[h] = grads[4]
            dlb_ref[:, cols] += grads[5]
            dgn_ref[:, cols] += grads[6]

    small = jax.ShapeDtypeStruct((1, D), f32)
    return _call(
        body, name=name, grid=(nt,),
        in_specs=[pl.BlockSpec((HG_TOKENS, 4 * D), lambda i: (nt - 1 - i, 0)), _row(D), _row(D),
                  pl.BlockSpec((HG_HEADS, 1, HG_DIM, HG_DIM), lambda i: (0, nt - 1 - i, 0, 0)),
                  pl.BlockSpec((HG_TOKENS, D), lambda i: (nt - 1 - i, 0))],
        out_specs=[pl.BlockSpec((HG_TOKENS, 4 * D), lambda i: (nt - 1 - i, 0)), _row(D), _row(D)],
        out_shape=[jax.ShapeDtypeStruct((t, 4 * D), bf16), small, small],
        scratch_shapes=[pltpu.VMEM((HG_HEADS, HG_DIM, HG_DIM), f32)],
        semantics=("arbitrary",), riders=riders,
    )(proj, lb, gn, states, dy)


FFN_COLS = 1408
HALO = 8
STRIP = 16


def _ffn_specs(tm):
    nb = tm // HALO
    main_g = pl.BlockSpec((tm, FFN_COLS), lambda j, i: (i, j))
    main_v = pl.BlockSpec((tm, FFN_COLS), lambda j, i: (i, j + 2))
    halo_g = pl.BlockSpec((HALO, FFN_COLS), lambda j, i: (jnp.maximum(i * nb - 1, 0), j))
    halo_v = pl.BlockSpec((HALO, FFN_COLS), lambda j, i: (jnp.maximum(i * nb - 1, 0), j + 2))
    w_g = pl.BlockSpec((3, FFN_COLS), lambda j, i: (0, j))
    w_v = pl.BlockSpec((3, FFN_COLS), lambda j, i: (0, j + 2))
    b_g = pl.BlockSpec((1, FFN_COLS), lambda j, i: (0, j))
    b_v = pl.BlockSpec((1, FFN_COLS), lambda j, i: (0, j + 2))
    return [main_g, halo_g, main_v, halo_v, w_g, w_v, b_g, b_v]


def _strip_rows(r):
    return pl.ds(r * STRIP, STRIP) if isinstance(r, int) else pl.ds(pl.multiple_of(r * STRIP, STRIP), STRIP)


def _for_strips(nstrip, strip, reverse=False):
    if reverse:
        strip(nstrip - 1, True)
        lax.fori_loop(0, nstrip - 1, lambda k, c: (strip(nstrip - 2 - k, False), c)[1], 0)
    else:
        strip(0, True)
        lax.fori_loop(1, nstrip, lambda r, c: (strip(r, False), c)[1], 0)


SUBLANES = 8


def _rows_down(prev, cur, shift):
    row = lax.broadcasted_iota(jnp.int32, (SUBLANES, LANES), 0)
    tiles = [prev[STRIP - SUBLANES:]] + [cur[q * SUBLANES:(q + 1) * SUBLANES] for q in range(STRIP // SUBLANES)]
    turned = [pltpu.roll(x, shift, axis=0) for x in tiles]
    return jnp.concatenate([jnp.where(row < shift, turned[q], turned[q + 1]) for q in range(STRIP // SUBLANES)], axis=0)


def _rows_ahead(cur, nxt, shift):
    row = lax.broadcasted_iota(jnp.int32, (SUBLANES, LANES), 0)
    tiles = [cur[q * SUBLANES:(q + 1) * SUBLANES] for q in range(STRIP // SUBLANES)] + [nxt[:SUBLANES]]
    turned = [pltpu.roll(x, SUBLANES - shift, axis=0) for x in tiles]
    return jnp.concatenate([jnp.where(row >= SUBLANES - shift, turned[q + 1], turned[q])
                            for q in range(STRIP // SUBLANES)], axis=0)


def _conv_strip(main_ref, halo_ref, w_ref, b_ref, r, edge, cols, first_block):
    cur = main_ref[_strip_rows(r), cols].astype(f32)
    if edge:
        h = jnp.where(first_block, 0.0, halo_ref[:, cols].astype(f32))
        prev = jnp.concatenate([jnp.zeros_like(h), h], axis=0)
    else:
        prev = main_ref[_strip_rows(r - 1), cols].astype(f32)
    a1, a2 = _rows_down(prev, cur, 1), _rows_down(prev, cur, 2)
    y = b_ref[:, cols] + w_ref[0:1, cols] * a2 + w_ref[1:2, cols] * a1 + w_ref[2:3, cols] * cur
    return y, (cur, a1, a2)


def ffn_gate_fwd(a, cw, cb, name, tm=512, riders=None):
    t = a.shape[0]
    tm = _tile(t, tm, STRIP)

    def body(ag_ref, hg_ref, av_ref, hv_ref, wg_ref, wv_ref, bg_ref, bv_ref, o_ref):
        first_block = pl.program_id(1) == 0

        def strip(r, edge):
            for c in range(FFN_COLS // LANES):
                cols = pl.ds(c * LANES, LANES)
                yg, _ = _conv_strip(ag_ref, hg_ref, wg_ref, bg_ref, r, edge, cols, first_block)
                yv, _ = _conv_strip(av_ref, hv_ref, wv_ref, bv_ref, r, edge, cols, first_block)
                o_ref[_strip_rows(r), cols] = (_gelu(yg) * yv).astype(bf16)

        _for_strips(tm // STRIP, strip)

    return _call(
        body, name=name, grid=(2, t // tm), in_specs=_ffn_specs(tm),
        out_specs=[pl.BlockSpec((tm, FFN_COLS), lambda j, i: (i, j))],
        out_shape=[jax.ShapeDtypeStruct((t, FFN_HIDDEN), bf16)],
        semantics=("parallel", "arbitrary"), riders=riders,
    )(a, a, a, a, cw, cw, cb, cb)


def _conv_tile(prev, cur, w_ref, b_ref, cols):
    row = lax.broadcasted_iota(jnp.int32, (SUBLANES, LANES), 0)
    a1 = jnp.where(row < 1, pltpu.roll(prev, 1, axis=0), pltpu.roll(cur, 1, axis=0))
    a2 = jnp.where(row < 2, pltpu.roll(prev, 2, axis=0), pltpu.roll(cur, 2, axis=0))
    return b_ref[:, cols] + w_ref[0:1, cols] * a2 + w_ref[1:2, cols] * a1 + w_ref[2:3, cols] * cur


def _gate_grads(yg, yv, dh):
    cdf = 0.5 * (1.0 + lax.erf(yg * (1.0 / math.sqrt(2.0))))
    pdf = jnp.exp(-0.5 * yg * yg) * (1.0 / math.sqrt(2.0 * math.pi))
    return dh * yv * (cdf + yg * pdf), dh * (yg * cdf)


def ffn_gate_bwd(a, cw, cb, dhid, name, tm=512, riders=None):
    t = a.shape[0]
    tm = _tile(t, tm, STRIP)
    nb = tm // HALO
    last_halo = t // HALO - 1
    nstrip = tm // STRIP
    fh = FFN_HIDDEN

    def body(a_ref, ha_ref, na_ref, w_ref, b_ref, dh_ref, ndh_ref, da_ref, dw_ref, db_ref, acc, dybuf):
        first_block = pl.program_id(0) == 0
        last_block = pl.program_id(0) == pl.num_programs(0) - 1

        @pl.when(first_block)
        def _():
            acc[...] = jnp.zeros_like(acc)

        def transposed_conv(dy, ahead, cols):
            return (w_ref[2:3, cols] * dy + w_ref[1:2, cols] * _rows_ahead(dy, ahead, 1)
                    + w_ref[0:1, cols] * _rows_ahead(dy, ahead, 2)).astype(bf16)

        def strip(r, edge):
            rows = _strip_rows(r)
            for c in range(fh // LANES):
                gate, val = pl.ds(c * LANES, LANES), pl.ds(fh + c * LANES, LANES)
                yg, taps_g = _conv_strip(a_ref, ha_ref, w_ref, b_ref, r, edge, gate, first_block)
                yv, taps_v = _conv_strip(a_ref, ha_ref, w_ref, b_ref, r, edge, val, first_block)
                dyg, dyv = _gate_grads(yg, yv, dh_ref[rows, gate].astype(f32))
                for p, (dy, (a0, a1, a2), cols) in enumerate(((dyg, taps_g, gate), (dyv, taps_v, val))):
                    acc[0, :, cols] += dy * a2
                    acc[1, :, cols] += dy * a1
                    acc[2, :, cols] += dy * a0
                    acc[3, :, cols] += dy
                    if not edge:
                        da_ref[_strip_rows(r - 1), cols] = transposed_conv(dybuf[:, cols], dy, cols)
                    dybuf[:, cols] = dy

        _for_strips(nstrip, strip)

        last_rows = pl.ds((nstrip - 1) * STRIP + SUBLANES, SUBLANES)
        for c in range(fh // LANES):
            gate, val = pl.ds(c * LANES, LANES), pl.ds(fh + c * LANES, LANES)
            yg = _conv_tile(a_ref[last_rows, gate].astype(f32), na_ref[:, gate].astype(f32), w_ref, b_ref, gate)
            yv = _conv_tile(a_ref[last_rows, val].astype(f32), na_ref[:, val].astype(f32), w_ref, b_ref, val)
            for dy, cols in zip(_gate_grads(yg, yv, ndh_ref[:, gate].astype(f32)), (gate, val)):
                dy = jnp.where(last_block, 0.0, dy)
                ahead = jnp.concatenate([dy, jnp.zeros_like(dy)], axis=0)
                da_ref[_strip_rows(nstrip - 1), cols] = transposed_conv(dybuf[:, cols], ahead, cols)

        @pl.when(last_block)
        def _():
            for tap in range(3):
                dw_ref[tap:tap + 1, :] = jnp.sum(acc[tap], axis=0, keepdims=True)
            db_ref[...] = jnp.sum(acc[3], axis=0, keepdims=True)

    def after(i):
        return jnp.minimum((i + 1) * nb, last_halo)

    return _call(
        body, name=name, grid=(t // tm,),
        in_specs=[pl.BlockSpec((tm, 2 * fh), lambda i: (i, 0)),
                  pl.BlockSpec((HALO, 2 * fh), lambda i: (jnp.maximum(i * nb - 1, 0), 0)),
                  pl.BlockSpec((HALO, 2 * fh), lambda i: (after(i), 0)),
                  pl.BlockSpec((3, 2 * fh), lambda i: (0, 0)), pl.BlockSpec((1, 2 * fh), lambda i: (0, 0)),
                  pl.BlockSpec((tm, fh), lambda i: (i, 0)), pl.BlockSpec((HALO, fh), lambda i: (after(i), 0))],
        out_specs=[pl.BlockSpec((tm, 2 * fh), lambda i: (i, 0)), pl.BlockSpec((3, 2 * fh), lambda i: (0, 0)),
                   pl.BlockSpec((1, 2 * fh), lambda i: (0, 0))],
        out_shape=[jax.ShapeDtypeStruct((t, 2 * fh), bf16), jax.ShapeDtypeStruct((3, 2 * fh), f32),
                   jax.ShapeDtypeStruct((1, 2 * fh), f32)],
        scratch_shapes=[pltpu.VMEM((4, STRIP, 2 * fh), f32), pltpu.VMEM((STRIP, 2 * fh), f32)],
        semantics=("arbitrary",), riders=riders,
    )(a, a, a, cw, cb, dhid, dhid)


def ada_mod(c_all, ada_w, ada_b_cols, name):
    cols = ada_w.shape[2]

    def body(c_ref, w_ref, b_ref, o_ref):
        c = c_ref[...]
        cond = (c * jax.nn.sigmoid(c)).astype(bf16)
        o_ref[0] = jnp.dot(cond, w_ref[0].astype(bf16), preferred_element_type=f32) + b_ref[0]

    return pl.pallas_call(
        body, name=name, grid=(DEPTH,),
        in_specs=[pl.BlockSpec((N_DEV, D), lambda i: (0, 0)), pl.BlockSpec((1, D, cols), lambda i: (i, 0, 0)),
                  pl.BlockSpec((1, 1, cols), lambda i: (i, 0, 0))],
        out_specs=pl.BlockSpec((1, N_DEV, cols), lambda i: (i, 0, 0)),
        out_shape=jax.ShapeDtypeStruct((DEPTH, N_DEV, cols), f32), compiler_params=_cparams("parallel"),
    )(c_all, ada_w, ada_b_cols)


def ada_grads(c_all, dmod_cols, dmod_all, name):
    cols = dmod_cols.shape[2]

    def body(c_ref, dm_ref, da_ref, dw_ref, db_ref):
        c = c_ref[...]
        cond = c * jax.nn.sigmoid(c)
        dw_ref[0] = lax.dot_general(cond, dm_ref[0], (((0,), (0,)), ((), ())), precision=lax.Precision.HIGHEST,
                                    preferred_element_type=f32)
        acc = da_ref[0, 0]
        for e in range(1, N_DEV):
            acc = acc + da_ref[e, 0]
        db_ref[0] = acc

    return pl.pallas_call(
        body, name=name, grid=(DEPTH,),
        in_specs=[pl.BlockSpec((N_DEV, D), lambda i: (0, 0)), pl.BlockSpec((1, N_DEV, cols), lambda i: (i, 0, 0)),
                  pl.BlockSpec((N_DEV, 1, 1, 6 * D), lambda i: (0, i, 0, 0))],
        out_specs=(pl.BlockSpec((1, D, cols), lambda i: (i, 0, 0)), pl.BlockSpec((1, 1, 6 * D), lambda i: (i, 0, 0))),
        out_shape=(jax.ShapeDtypeStruct((DEPTH, D, cols), f32), jax.ShapeDtypeStruct((DEPTH, 1, 6 * D), f32)),
        compiler_params=_cparams("parallel"),
    )(c_all, dmod_cols, dmod_all)


def lower_bound_fwd(hg_lb, name):
    n = hg_lb.shape[1]

    def body(l_ref, o_ref):
        o_ref[...] = jax.nn.sigmoid(l_ref[1:2, :] - l_ref[0:1, :])

    return pl.pallas_call(body, name=name, out_shape=jax.ShapeDtypeStruct((1, n), f32))(hg_lb)


def lower_bound_bwd(hg_lb, dlb, name):
    n = hg_lb.shape[1]

    def body(l_ref, d_ref, o_ref):
        p = jax.nn.sigmoid(l_ref[1:2, :] - l_ref[0:1, :])
        g = d_ref[...] * p * (1.0 - p)
        o_ref[0:1, :] = -g
        o_ref[1:2, :] = g

    return pl.pallas_call(body, name=name, out_shape=jax.ShapeDtypeStruct((2, n), f32))(hg_lb, dlb)


def _adamw(w, g, m, v):
    m = ADAM_B1 * m + (1.0 - ADAM_B1) * g
    v = ADAM_B2 * v + (1.0 - ADAM_B2) * jnp.square(g)
    m_hat = m / (1.0 - ADAM_B1 ** ADAM_STEP)
    v_hat = v / (1.0 - ADAM_B2 ** ADAM_STEP)
    delta = -ADAM_LR * (m_hat / (jnp.sqrt(v_hat) + ADAM_EPS) + ADAM_WD * w)
    return delta, m, v


ADAM_BLOCK_BYTES = 32 * 1024 * 1024


def adam_reduced(parts, w, m, v, name):
    layers, r, c = w.shape
    outs = None
    for layer in range(layers):
        outs = _adam_layer(parts[layer], w, m, v, layer, outs, f"{name}_{layer}")
    return outs


def _adam_layer(parts, w, m, v, layer, prev, name):
    layers, r, c = w.shape
    rows = parts[0].shape[1]
    assert all(p.shape == (N_DEV, rows, c) for p in parts) and rows * len(parts) == r
    row_bytes = 2 * (len(parts) * N_DEV * c * 2 + 7 * c * 4)
    tr = _tile(rows, max(16, ADAM_BLOCK_BYTES // row_bytes), 16)
    steps = rows // tr
    n_prev = 0 if prev is None else 4

    def body(*refs):
        p_refs = refs[:len(parts)]
        w_ref, m_ref, v_ref = refs[len(parts):len(parts) + 3]
        g_ref, d_ref, mo_ref, vo_ref = refs[len(parts) + 3 + n_prev:]
        for idx in range(len(parts)):
            @pl.when(pl.program_id(0) == idx)
            def _():
                g = p_refs[idx][0].astype(f32)
                for j in range(1, N_DEV):
                    g = g + p_refs[idx][j].astype(f32)
                g_ref[...] = g
                d_ref[...], mo_ref[...], vo_ref[...] = _adamw(w_ref[...], g, m_ref[...], v_ref[...])

    def part_spec(idx):
        return pl.BlockSpec((N_DEV, tr, c), lambda p, i: (0, jnp.where(p == idx, i, 0), 0))

    blk = pl.BlockSpec((None, tr, c), lambda p, i: (layer, p * steps + i, 0))
    out = jax.ShapeDtypeStruct((layers, r, c), f32)
    n_in = len(parts) + 3
    return pl.pallas_call(
        body, name=name, grid=(len(parts), steps),
        in_specs=[part_spec(idx) for idx in range(len(parts))] + [blk, blk, blk] + [ANY] * n_prev,
        out_specs=(blk, blk, blk, blk), out_shape=(out, out, out, out),
        input_output_aliases={n_in + k: k for k in range(n_prev)},
        compiler_params=_cparams("arbitrary", "arbitrary"),
    )(*parts, w, m, v, *(prev or ()))


def adam_plain(g, w, m, v, name, tr=256):
    r, c = w.shape
    tr = _tile(r, tr, 8)

    def body(g_ref, w_ref, m_ref, v_ref, d_ref, mo_ref, vo_ref):
        d_ref[...], mo_ref[...], vo_ref[...] = _adamw(w_ref[...], g_ref[...], m_ref[...], v_ref[...])

    blk = pl.BlockSpec((tr, c), lambda i: (i, 0))
    out = jax.ShapeDtypeStruct((r, c), f32)
    return pl.pallas_call(
        body, name=name, grid=(r // tr,), in_specs=[blk, blk, blk, blk], out_specs=(blk, blk, blk),
        out_shape=(out, out, out), compiler_params=_cparams("parallel"),
    )(g, w, m, v)


def sum_parts(parts, name):
    _, r, c = parts.shape

    def body(p_ref, o_ref):
        acc = p_ref[0]
        for j in range(1, N_DEV):
            acc = acc + p_ref[j]
        o_ref[...] = acc

    return pl.pallas_call(body, name=name, out_shape=jax.ShapeDtypeStruct((r, c), f32))(parts)


def _pack(arrs, rows_mult=8):
    flat = jnp.concatenate([a.reshape(-1) for a in arrs])
    rows = -(-flat.shape[0] // LANES)
    rows = -(-rows // rows_mult) * rows_mult
    return jnp.pad(flat, (0, rows * LANES - flat.shape[0])).reshape(rows, LANES)


def _unpack(flat, shapes):
    out, at = [], 0
    for s in shapes:
        n = math.prod(s)
        out.append(flat[at:at + n].reshape(s))
        at += n
    return out


def kernel(x, c, gm_w_in, gm_ln_g, gm_ln_b, gm_w_s, gm_b_s, gm_w_out, hg_w_in, hg_lb, hg_gn_g, hg_w_out, ffn_w_up, ffn_conv_w, ffn_conv_b, ffn_w_down, norm_g, ada_w, ada_b, final_g, loss_target, m_gm_w_in, m_gm_ln_g, m_gm_ln_b, m_gm_w_s, m_gm_b_s, m_gm_w_out, m_hg_w_in, m_hg_lb, m_hg_gn_g, m_hg_w_out, m_ffn_w_up, m_ffn_conv_w, m_ffn_conv_b, m_ffn_w_down, m_norm_g, m_ada_w, m_ada_b, m_final_g, v_gm_w_in, v_gm_ln_g, v_gm_ln_b, v_gm_w_s, v_gm_b_s, v_gm_w_out, v_hg_w_in, v_hg_lb, v_hg_gn_g, v_hg_w_out, v_ffn_w_up, v_ffn_conv_w, v_ffn_conv_b, v_ffn_w_down, v_norm_g, v_ada_w, v_ada_b, v_final_g):
    me = _flat(_mesh_pos())
    xt = x[0]
    t = xt.shape[0]

    small_shapes = [(1, D), (2, HG_DIM), (2, HG_DIM), (DEPTH, 2, HG_DIM), (DEPTH, 3, 2 * FFN_HIDDEN // N_DEV)]
    w_in_0 = gm_w_in[0].astype(bf16)
    small_all, w_in_0_all = all_gather([_pack([c, hg_lb, hg_gn_g, norm_g, ffn_conv_w]), w_in_0], "gather_first", relay=True)
    small_all = small_all.reshape(N_DEV, -1)
    at = 0
    pieces = []
    for s in small_shapes:
        n = math.prod(s)
        pieces.append(small_all[:, at:at + n].reshape((N_DEV,) + s))
        at += n
    c_all = pieces[0].reshape(N_DEV, D)
    hg_lb_full = jnp.transpose(pieces[1], (1, 0, 2)).reshape(2, D)
    hg_gn_full = jnp.transpose(pieces[2], (1, 0, 2)).reshape(2, D)
    norm_g_full = jnp.transpose(pieces[3], (1, 2, 0, 3)).reshape(DEPTH, 2, D)
    conv_w_full = jnp.transpose(pieces[4], (1, 2, 0, 3)).reshape(DEPTH, 3, 2 * FFN_HIDDEN)

    lb1 = lower_bound_fwd(hg_lb_full, "lower_bound")
    lbs = [jnp.zeros((1, D), f32), lb1]

    ada_b_cols = lax.dynamic_slice(ada_b, (0, me * ADA_COLS), (DEPTH, ADA_COLS)).reshape(DEPTH, 1, ADA_COLS)
    mod_cols = ada_mod(c_all, ada_w, ada_b_cols, "ada_mod")
    (mod_mine,) = all_to_all([jnp.transpose(mod_cols, (1, 0, 2))], "mod_to_examples")
    mod = jnp.transpose(mod_mine, (1, 0, 2)).reshape(DEPTH, 6, 1, D)

    def layer_shards(i):
        j = i // 2
        w_in, w_out = (gm_w_in, gm_w_out) if i % 2 == 0 else (hg_w_in, hg_w_out)
        return [w_in[j].astype(bf16), w_out[j].astype(bf16), ffn_w_up[i].T.astype(bf16), ffn_w_down[i].astype(bf16)]

    def full_rows(g):
        return g.reshape(N_DEV * g.shape[1], g.shape[2])

    carried_by = {
        "in_0": [(0, 1), (0, 3)], "mix_0": [(0, 2)], "up_0": [(1, 0), (1, 1)], "gate_0": [(1, 2)], "down_0": [(1, 3)],
        "in_1": [(2, 0)], "mix_1": [(2, 1), (2, 2), (2, 3)], "up_1": [(3, 0), (3, 1)], "gate_1": [(3, 2)], "down_1": [(3, 3)],
    }
    shards = [layer_shards(i) for i in range(DEPTH)]
    gathered = {}
    gathered[(0, 0)] = w_in_0_all

    def carry(call, site, **kw):
        items = carried_by.get(site, [])
        outs = call(riders=Riders([shards[l][slot] for l, slot in items], True), **kw)
        for item, g in zip(items, outs[len(outs) - len(items):]):
            gathered[item] = g
        return outs[:len(outs) - len(items)]

    saved = []
    weights = []
    xcur = xt
    h = norm_fwd(xcur, norm_g_full[0, 0:1], mod[0, 1], mod[0, 0], "norm1_0")
    for i in range(DEPTH):
        j = i // 2
        sh1, sc1, g1, sh2, sc2, g2 = [mod[i, p] for p in range(6)]
        gn2 = norm_g_full[i, 1:2]
        s = {"x0": xcur, "h": h}
        w_in = gathered[(i, 0)]
        if i % 2 == 0:
            (z,) = carry(functools.partial(mm_nn, h, w_in, bf16, f"gm_in_{i}"), f"in_{i}")
            bs = gm_b_s[j].reshape(GM_HEADS, GM_BLOCK, 1)
            (mixed,) = carry(functools.partial(gm_mix_fwd, z, gm_w_s[j], bs, gm_ln_g[j:j + 1], gm_ln_b[j:j + 1],
                                               f"gm_mix_{i}"), f"mix_{i}")
            s["z"] = z
        else:
            (proj,) = carry(functools.partial(mm_nn, h, w_in, f32, f"hg_in_{i}"), f"in_{i}")
            mixed, states = carry(functools.partial(hg_scan_fwd, proj, lbs[j], hg_gn_full[j:j + 1], f"hg_scan_{i}"),
                                  f"mix_{i}")
            s["proj"], s["states"] = proj, states
        s["mixed"] = mixed
        w_out = full_rows(gathered[(i, 1)])
        y, x1, h2 = carry(functools.partial(mm_nn_residual, mixed, w_out, xcur, g1, (gn2, sc2, sh2), f"mix_out_{i}"),
                          f"out_{i}")
        s["y"], s["x1"] = y, x1
        w_up = gathered[(i, 2)]
        (a,) = carry(functools.partial(mm_nn, h2, w_up, bf16, f"ffn_up_{i}", transposed=True), f"up_{i}")
        (hid,) = carry(functools.partial(ffn_gate_fwd, a, conv_w_full[i], ffn_conv_b[i:i + 1], f"ffn_gate_{i}"), f"gate_{i}")
        w_down = full_rows(gathered[(i, 3)])
        next_norm = (norm_g_full[i + 1, 0:1], mod[i + 1, 1], mod[i + 1, 0]) if i + 1 < DEPTH else None
        outs = carry(functools.partial(mm_nn_residual, hid, w_down, x1, g2, next_norm, f"ffn_down_{i}"), f"down_{i}")
        fo, x2 = outs[0], outs[1]
        s["h2"], s["a"], s["hid"], s["f"] = h2, a, hid, fo
        weights.append((w_in, w_out, w_up, w_down))
        saved.append(s)
        xcur = x2
        h = outs[2] if next_norm is not None else None

    loss_part, dx, d_final_g, dg2, df = loss_head(xcur, final_g.reshape(1, D), loss_target[0], saved[-1]["f"],
                                                  mod[DEPTH - 1, 5], "loss_head")
    loss = lax.psum(loss_part[0, 0], ("x", "y", "c"))

    def halves(blocked):
        rows = blocked.shape[1] // 2
        return [(blocked, (0, rows)), (blocked, (rows, rows))]

    def by_owner_rows(dw):
        k, n = dw.shape
        return dw.reshape(N_DEV, k // N_DEV, n)

    received = [[[] for _ in range(4)] for _ in range(DEPTH)]

    def send(call, items, **kw):
        outs = call(riders=Riders([arr for arr, _ in items], False), **kw)
        for (_, (layer, slot)), got in zip(items, outs[len(outs) - len(items):]):
            received[layer][slot].append(got)
        return outs[:len(outs) - len(items)]

    dmod = [None] * DEPTH
    d_norm_g = [None] * DEPTH
    d_gm = {k: [None, None] for k in ("ws", "bs", "lng", "lnb")}
    d_hg = {k: [None, None] for k in ("lb", "gn")}
    d_ffn = {k: [None] * DEPTH for k in ("cw", "cb")}
    in_halves = []
    for i in reversed(range(DEPTH)):
        j = i // 2
        s = saved[i]
        w_in, w_out, w_up, w_down = weights[i]
        sh1, sc1, g1, sh2, sc2, g2 = [mod[i, p] for p in range(6)]
        gn1, gn2 = norm_g_full[i, 0:1], norm_g_full[i, 1:2]
        (dw_down,) = mm_tn(s["hid"], df, bf16, f"dw_down_{i}", tn=D)
        scan_carries = i % 2 == 1
        down_item = (by_owner_rows(dw_down), (i, 3))
        (dhid,) = mm_nt(df, w_down, bf16, f"dhid_{i}")
        da, d_ffn["cw"][i], d_ffn["cb"][i] = send(
            functools.partial(ffn_gate_bwd, s["a"], conv_w_full[i], ffn_conv_b[i:i + 1], dhid, f"ffn_gate_bwd_{i}"),
            in_halves)
        (dw_up_t,) = send(functools.partial(mm_tn, da, s["h2"], bf16, f"dw_up_{i}", tn=D), [] if scan_carries else [down_item])
        dw_up_t = dw_up_t.reshape(N_DEV, -1, D)
        up_halves = [(dw_up_t, (i, 2))] if scan_carries else [(part, (i, 2)) for part in halves(dw_up_t)]
        (dh2,) = send(functools.partial(mm_nt, da, w_up, bf16, f"dh2_{i}", transposed=True),
                      [] if scan_carries else up_halves[:1])
        dx1, dgn2, dsc2, dsh2, dg1, dy = norm_bwd(s["x1"], gn2, sc2, sh2, dh2, dx, f"norm2_bwd_{i}", gate=(s["y"], g1))
        (dw_out,) = mm_tn(s["mixed"], dy, bf16, f"dw_mix_out_{i}", tn=D)
        (dmixed,) = mm_nt(dy, w_out, bf16, f"dmixed_{i}")
        if i % 2 == 0:
            bs = gm_b_s[j].reshape(GM_HEADS, GM_BLOCK, 1)
            dpre, dws, dbs, dlng, dlnb = send(
                functools.partial(gm_mix_bwd, s["z"], gm_w_s[j], bs, gm_ln_g[j:j + 1], gm_ln_b[j:j + 1], dmixed,
                                  f"gm_mix_bwd_{i}"), up_halves[1:])
            d_gm["ws"][j], d_gm["bs"][j], d_gm["lng"][j], d_gm["lnb"][j] = dws, dbs.reshape(GM_HEADS, GM_BLOCK), dlng, dlnb
        else:
            dpre, dlb, dgn = send(
                functools.partial(hg_scan_bwd, s["proj"], lbs[j], hg_gn_full[j:j + 1], s["states"], dmixed,
                                  f"hg_scan_bwd_{i}"), [down_item] + up_halves)
            d_hg["lb"][j], d_hg["gn"][j] = dlb, dgn
        (dw_in,) = send(functools.partial(mm_tn_by_owner, s["h"], dpre, f"dw_mix_in_{i}", tm=D), [(by_owner_rows(dw_out), (i, 1))])
        in_halves = [(part, (i, 0)) for part in halves(dw_in)]
        (dh,) = send(functools.partial(mm_nt, dpre, w_in, bf16, f"dh_mix_{i}"), in_halves[:1] if i == 0 else [])
        dmod_i = [None, None, dg1, dsh2, dsc2, dg2]
        if i > 0:
            dx, dgn1, dsc1, dsh1, dg2, df = norm_bwd(s["x0"], gn1, sc1, sh1, dh, dx1, f"norm1_bwd_{i}",
                                                     gate=(saved[i - 1]["f"], mod[i - 1, 5]))
        else:
            dx, dgn1, dsc1, dsh1 = send(functools.partial(norm_bwd, s["x0"], gn1, sc1, sh1, dh, dx1, f"norm1_bwd_{i}"),
                                        in_halves[1:])
        dmod_i[0], dmod_i[1] = dsh1, dsc1
        dmod[i] = jnp.concatenate(dmod_i, axis=1)
        d_norm_g[i] = jnp.concatenate([dgn1, dgn2], axis=0)
    grad_x = dx.reshape(1, t, D)

    (dmod_all,) = all_gather([jnp.concatenate(dmod, axis=0)], "gather_dmod")
    dmod_cols = jnp.transpose(lax.dynamic_slice(dmod_all, (0, 0, me * ADA_COLS), (N_DEV, DEPTH, ADA_COLS)), (1, 0, 2))
    g_ada_w, g_ada_b = ada_grads(c_all, dmod_cols, dmod_all.reshape(N_DEV, DEPTH, 1, 6 * D), "ada_grads")
    g_ada_b = g_ada_b.reshape(DEPTH, 6 * D)

    small_partials = [jnp.concatenate(d_gm["lng"], axis=0), jnp.concatenate(d_gm["lnb"], axis=0),
                      jnp.stack(d_gm["ws"]), jnp.stack(d_gm["bs"]), jnp.concatenate(d_ffn["cb"], axis=0),
                      d_final_g, d_hg["lb"][1], jnp.concatenate(d_hg["gn"], axis=0), jnp.stack(d_norm_g),
                      jnp.stack(d_ffn["cw"])]
    partial_shapes = [p.shape for p in small_partials]
    packed = _pack(small_partials, rows_mult=8 * N_DEV)
    rows = packed.shape[0] // N_DEV
    (recv,) = all_to_all([packed.reshape(N_DEV, rows, LANES)], "small_grads_exchange")
    (summed,) = all_gather([sum_parts(recv, "small_grads_sum")], "small_grads_gather")

    def parts_of(slot, layers):
        return [received[i][slot] for i in layers]

    def swapped(a):
        return jnp.swapaxes(a, 1, 2)

    w_shards = [gm_w_in, gm_w_out, hg_w_in, hg_w_out, swapped(ffn_w_up), ffn_w_down]
    big_parts = [parts_of(0, (0, 2)), parts_of(1, (0, 2)), parts_of(0, (1, 3)), parts_of(1, (1, 3)),
                 parts_of(2, range(DEPTH)), parts_of(3, range(DEPTH))]
    big_m = [m_gm_w_in, m_gm_w_out, m_hg_w_in, m_hg_w_out, swapped(m_ffn_w_up), m_ffn_w_down]
    big_v = [v_gm_w_in, v_gm_w_out, v_hg_w_in, v_hg_w_out, swapped(v_ffn_w_up), v_ffn_w_down]
    big = [adam_reduced(parts, w, m_, v_, f"adam_big_{idx}")
           for idx, (w, m_, v_, parts) in enumerate(zip(w_shards, big_m, big_v, big_parts))]
    big[4] = [swapped(o) for o in big[4]]
    (g_gm_w_in, d_gm_w_in, nm_gm_w_in, nv_gm_w_in), (g_gm_w_out, d_gm_w_out, nm_gm_w_out, nv_gm_w_out), \
        (g_hg_w_in, d_hg_w_in, nm_hg_w_in, nv_hg_w_in), (g_hg_w_out, d_hg_w_out, nm_hg_w_out, nv_hg_w_out), \
        (g_ffn_w_up, d_ffn_w_up, nm_ffn_w_up, nv_ffn_w_up), (g_ffn_w_down, d_ffn_w_down, nm_ffn_w_down, nv_ffn_w_down) = big

    g_ln_g, g_ln_b, g_ws, g_bs, g_cb, g_final, g_lb1, g_gn, g_norm, g_cw = _unpack(summed.reshape(-1), partial_shapes)
    g_final = g_final.reshape(D)

    def my_cols(a, n):
        start = (0,) * (a.ndim - 1) + (me * n,)
        return lax.dynamic_slice(a, start, a.shape[:-1] + (n,))

    g_hg_lb = lower_bound_bwd(hg_lb, my_cols(g_lb1, HG_DIM), "lower_bound_bwd")
    g_hg_gn = my_cols(g_gn, HG_DIM)
    g_norm_g = my_cols(g_norm, HG_DIM)
    g_conv_w = my_cols(g_cw, 2 * FFN_HIDDEN // N_DEV)

    two_d = (-1, ADA_COLS)
    d_ada_w, nm_ada_w, nv_ada_w = [o.reshape(ada_w.shape) for o in adam_plain(
        g_ada_w.reshape(two_d), ada_w.reshape(two_d), m_ada_w.reshape(two_d), v_ada_w.reshape(two_d), "adam_ada_w")]

    small_g = [g_ln_g, g_ln_b, g_ws, g_bs, g_cb, g_ada_b, g_final, g_hg_lb, g_hg_gn, g_norm_g, g_conv_w]
    small_w = [gm_ln_g, gm_ln_b, gm_w_s, gm_b_s, ffn_conv_b, ada_b, final_g, hg_lb, hg_gn_g, norm_g, ffn_conv_w]
    small_m = [m_gm_ln_g, m_gm_ln_b, m_gm_w_s, m_gm_b_s, m_ffn_conv_b, m_ada_b, m_final_g, m_hg_lb, m_hg_gn_g, m_norm_g, m_ffn_conv_w]
    small_v = [v_gm_ln_g, v_gm_ln_b, v_gm_w_s, v_gm_b_s, v_ffn_conv_b, v_ada_b, v_final_g, v_hg_lb, v_hg_gn_g, v_norm_g, v_ffn_conv_w]
    shapes = [w.shape for w in small_w]
    small_g = [g.reshape(s) for g, s in zip(small_g, shapes)]
    outs = adam_plain(_pack(small_g), _pack(small_w), _pack(small_m), _pack(small_v), "adam_small")
    (d_ln_g, d_ln_b, d_ws, d_bs, d_cb, d_ada_b, d_final, d_hg_lb, d_hg_gn, d_norm_g_, d_conv_w), \
        (nm_ln_g, nm_ln_b, nm_ws, nm_bs, nm_cb, nm_ada_b, nm_final, nm_hg_lb, nm_hg_gn, nm_norm_g, nm_conv_w), \
        (nv_ln_g, nv_ln_b, nv_ws, nv_bs, nv_cb, nv_ada_b, nv_final, nv_hg_lb, nv_hg_gn, nv_norm_g, nv_conv_w) = [
            _unpack(o.reshape(-1), shapes) for o in outs]
    g_ln_g, g_ln_b, g_ws, g_bs, g_cb, g_ada_b, g_final, g_hg_lb, g_hg_gn, g_norm_g, g_conv_w = small_g

    grads = (g_gm_w_in, g_ln_g, g_ln_b, g_ws, g_bs, g_gm_w_out, g_hg_w_in, g_hg_lb, g_hg_gn, g_hg_w_out,
             g_ffn_w_up, g_conv_w, g_cb, g_ffn_w_down, g_norm_g, g_ada_w, g_ada_b, g_final)
    deltas = (d_gm_w_in, d_ln_g, d_ln_b, d_ws, d_bs, d_gm_w_out, d_hg_w_in, d_hg_lb, d_hg_gn, d_hg_w_out,
              d_ffn_w_up, d_conv_w, d_cb, d_ffn_w_down, d_norm_g_, d_ada_w, d_ada_b, d_final)
    new_m = (nm_gm_w_in, nm_ln_g, nm_ln_b, nm_ws, nm_bs, nm_gm_w_out, nm_hg_w_in, nm_hg_lb, nm_hg_gn, nm_hg_w_out,
             nm_ffn_w_up, nm_conv_w, nm_cb, nm_ffn_w_down, nm_norm_g, nm_ada_w, nm_ada_b, nm_final)
    new_v = (nv_gm_w_in, nv_ln_g, nv_ln_b, nv_ws, nv_bs, nv_gm_w_out, nv_hg_w_in, nv_hg_lb, nv_hg_gn, nv_hg_w_out,
             nv_ffn_w_up, nv_conv_w, nv_cb, nv_ffn_w_down, nv_norm_g, nv_ada_w, nv_ada_b, nv_final)
    return (loss, grad_x) + grads + deltas + new_m + new_v
```

```python
import functools
import math

import jax
import jax.numpy as jnp
from jax import lax
from jax.experimental import pallas as pl
from jax.experimental.pallas import tpu as pltpu

f32 = jnp.float32
bf16 = jnp.bfloat16
MESH = pl.DeviceIdType.MESH

N_DEV = 8
D = 1024
DEPTH = 4
EPS = 1e-6
GM_WIDTH = 2048
GM_HEADS = 8
GM_HEAD_DIM = 256
GM_BLOCK = 128
CHUNK = 64
HG_HEADS = 8
HG_DIM = 128
FFN_HIDDEN = 2816
ADA_COLS = 6 * D // N_DEV

HG_SUB_FWD = 64
HG_SUB_BWD = 32
HG_PAIR = 8
HG_TOKENS = 128

ADAM_LR = 0.001
ADAM_B1 = 0.9
ADAM_B2 = 0.999
ADAM_EPS = 1e-08
ADAM_WD = 0.01
ADAM_STEP = 10

V7X_VMEM_LIMIT = 56 * 1024 * 1024
LANES = 128


def _cparams(*sem):
    return pltpu.CompilerParams(dimension_semantics=sem or None, vmem_limit_bytes=V7X_VMEM_LIMIT)


def _tile(n, target, mult=LANES):
    best = None
    for t in range(mult, min(n, target) + 1, mult):
        if n % t == 0:
            best = t
    return best or n


WEIGHT_BLOCK_BYTES = 6 * 1024 * 1024


def _weight_tile(n, k):
    return _tile(n, max(LANES, WEIGHT_BLOCK_BYTES // (2 * k)))


@jax.custom_vjp
def _gelu(x):
    return 0.5 * x * (1.0 + lax.erf(x * (1.0 / math.sqrt(2.0))))


def _gelu_fwd(x):
    cdf = 0.5 * (1.0 + lax.erf(x * (1.0 / math.sqrt(2.0))))
    return x * cdf, (x, cdf)


def _gelu_bwd(res, g):
    x, cdf = res
    pdf = jnp.exp(-0.5 * x * x) * (1.0 / math.sqrt(2.0 * math.pi))
    return (g * (cdf + x * pdf),)


_gelu.defvjp(_gelu_fwd, _gelu_bwd)


def _mesh_pos():
    return lax.axis_index("x"), lax.axis_index("y"), lax.axis_index("c")


def _flat(pos):
    return 4 * pos[0] + 2 * pos[1] + pos[2]


def _peer(pos, k):
    return ((1 - pos[0]) if k & 4 else pos[0], (1 - pos[1]) if k & 2 else pos[1], (1 - pos[2]) if k & 1 else pos[2])


def _exchange_copies(ins, outs, send_sems, recv_sems, local_sems, gather):
    pos = _mesh_pos()
    me = _flat(pos)

    def src(i, dest):
        if gather:
            return ins[i]
        ref, rows = ins[i] if isinstance(ins[i], tuple) else (ins[i], None)
        return ref.at[dest] if rows is None else ref.at[dest, pl.ds(*rows)]

    local = [pltpu.make_async_copy(src(i, me), outs[i].at[me], local_sems.at[i]) for i in range(len(ins))]
    sends, recvs = [], []
    for k in range(1, N_DEV):
        peer = _peer(pos, k)
        there = _flat(peer)
        for i in range(len(ins)):
            sems = dict(send_sem=send_sems.at[i * 7 + k - 1], recv_sem=recv_sems.at[i * 7 + k - 1],
                        device_id=peer, device_id_type=MESH)
            sends.append(pltpu.make_async_remote_copy(src_ref=src(i, there), dst_ref=outs[i].at[me], **sems))
            recvs.append(pltpu.make_async_remote_copy(src_ref=src(i, there), dst_ref=outs[i].at[there], **sems))
    return local, sends, recvs


def _exchange_start(*refs):
    local, sends, _ = _exchange_copies(*refs)
    for cp in local + sends:
        cp.start()


def _exchange_wait(*refs):
    local, sends, recvs = _exchange_copies(*refs)
    for cp in recvs:
        cp.wait_recv()
    for cp in sends:
        cp.wait_send()
    for cp in local:
        cp.wait()


OTHER_CHIPS = (2, 4, 6)


def _relay_copies(ins, outs, send_sems, recv_sems, local_sems):
    pos = _mesh_pos()
    me = _flat(pos)
    sibling = _peer(pos, 1)
    local = [pltpu.make_async_copy(ins[i], outs[i].at[me], local_sems.at[i]) for i in range(len(ins))]
    first, passes, recvs = [], {k: [] for k in OTHER_CHIPS}, {k: [] for k in range(1, N_DEV)}
    for i in range(len(ins)):
        def copy(k, src, block, to):
            return pltpu.make_async_remote_copy(
                src_ref=src, dst_ref=outs[i].at[block], send_sem=send_sems.at[i * 7 + k - 1],
                recv_sem=recv_sems.at[i * 7 + k - 1], device_id=to, device_id_type=MESH)

        for k in (1,) + OTHER_CHIPS:
            first.append(copy(k, ins[i], me, _peer(pos, k)))
        for k in OTHER_CHIPS:
            there = _flat(_peer(pos, k))
            passes[k].append(copy(k ^ 1, outs[i].at[there], there, sibling))
        for k in range(1, N_DEV):
            there = _flat(_peer(pos, k))
            recvs[k].append(copy(k, ins[i], there, _peer(pos, k)))
    return local, first, passes, recvs


def _relay_start(ins, outs, *sems):
    local, first, _, _ = _relay_copies(ins, outs, *sems)
    for cp in local + first:
        cp.start()


def _relay_wait(ins, outs, *sems):
    local, first, passes, recvs = _relay_copies(ins, outs, *sems)
    for k in OTHER_CHIPS:
        for cp in recvs[k]:
            cp.wait_recv()
        for cp in passes[k]:
            cp.start()
    for k in (1, 3, 5, 7):
        for cp in recvs[k]:
            cp.wait_recv()
    for cp in first + [cp for k in OTHER_CHIPS for cp in passes[k]]:
        cp.wait_send()
    for cp in local:
        cp.wait()


def _exchange_out_shape(a, gather):
    return jax.ShapeDtypeStruct((N_DEV,) + tuple(a.shape) if gather else tuple(a.shape), a.dtype)


def _exchange_sems(n):
    return [pltpu.SemaphoreType.DMA((7 * n,)), pltpu.SemaphoreType.DMA((7 * n,)), pltpu.SemaphoreType.DMA((n,))]


ANY = pl.BlockSpec(memory_space=pl.ANY)


def _exchange(arrs, gather, name, relay=False):
    n = len(arrs)

    def body(*refs):
        ins, outs = refs[:n], refs[n:2 * n]
        if relay:
            _relay_start(ins, outs, *refs[2 * n:])
            _relay_wait(ins, outs, *refs[2 * n:])
        else:
            _exchange_start(ins, outs, *refs[2 * n:], gather)
            _exchange_wait(ins, outs, *refs[2 * n:], gather)

    return pl.pallas_call(
        body, name=name, out_shape=tuple(_exchange_out_shape(a, gather) for a in arrs),
        in_specs=[ANY] * n, out_specs=tuple([ANY] * n), scratch_shapes=_exchange_sems(n),
    )(*arrs)


def all_gather(arrs, name, relay=False):
    return _exchange(arrs, True, name, relay)


def all_to_all(arrs, name):
    return _exchange(arrs, False, name)


class Riders:
    def __init__(self, arrs, gather):
        self.gather = gather
        self.rows = [a[1] if isinstance(a, tuple) else None for a in arrs]
        self.arrs = [a[0] if isinstance(a, tuple) else a for a in arrs]

    def out_shapes(self):
        shapes = []
        for a, rows in zip(self.arrs, self.rows):
            shape = tuple(a.shape) if rows is None else (a.shape[0], rows[1], a.shape[2])
            shapes.append(jax.ShapeDtypeStruct((N_DEV,) + shape if self.gather else shape, a.dtype))
        return shapes


def _call(body, *, name, grid, in_specs, out_specs, out_shape, semantics, scratch_shapes=(), riders=None):
    if riders is None or not riders.arrs:
        return pl.pallas_call(body, name=name, grid=grid, in_specs=in_specs, out_specs=tuple(out_specs),
                              out_shape=tuple(out_shape), scratch_shapes=list(scratch_shapes),
                              compiler_params=_cparams(*semantics))
    n_in, n_out, n_scr, n_r = len(in_specs), len(out_specs), len(scratch_shapes), len(riders.arrs)
    gather = riders.gather

    def hosted(*refs):
        ins, r_ins = refs[:n_in], refs[n_in:n_in + n_r]
        at = n_in + n_r
        outs, r_outs = refs[at:at + n_out], refs[at + n_out:at + n_out + n_r]
        at += n_out + n_r
        scratch, sems = refs[at:at + n_scr], refs[at + n_scr:]
        first = functools.reduce(jnp.logical_and, [pl.program_id(a) == 0 for a in range(len(grid))])
        last = functools.reduce(jnp.logical_and, [pl.program_id(a) == grid[a] - 1 for a in range(len(grid))])

        r_ins = [(ref, rows) if rows is not None else ref for ref, rows in zip(r_ins, riders.rows)]

        @pl.when(first)
        def _():
            if gather:
                _relay_start(r_ins, r_outs, *sems)
            else:
                _exchange_start(r_ins, r_outs, *sems, gather)

        body(*ins, *outs, *scratch)

        @pl.when(last)
        def _():
            if gather:
                _relay_wait(r_ins, r_outs, *sems)
            else:
                _exchange_wait(r_ins, r_outs, *sems, gather)

    call = pl.pallas_call(
        hosted, name=name, grid=grid, in_specs=list(in_specs) + [ANY] * n_r, out_specs=tuple(out_specs) + (ANY,) * n_r,
        out_shape=tuple(out_shape) + tuple(riders.out_shapes()),
        scratch_shapes=list(scratch_shapes) + _exchange_sems(n_r),
        compiler_params=_cparams(*(("arbitrary",) * len(grid))))
    return lambda *args: call(*args, *riders.arrs)


def _shards_per_step(shape):
    _, k, n = shape
    best = None
    for q in (1, 2, 4, 8):
        if (q * n) % LANES == 0 and (best is None or 2 * k * q * n <= WEIGHT_BLOCK_BYTES):
            best = q
    return best


def mm_nn(a, b, out_dtype, name, tm=512, riders=None, transposed=False):
    m, k = a.shape
    tm = _tile(m, tm, 8)
    if b.ndim == 3:
        shard = b.shape[1] if transposed else b.shape[2]
        n = N_DEV * shard
        per_step = _shards_per_step((N_DEV, k, shard))
        tn = per_step * shard
        b_spec = pl.BlockSpec((per_step,) + b.shape[1:], lambda j, i: (j, 0, 0))
        contract = (((1,), (1,)), ((), ())) if transposed else (((1,), (0,)), ((), ()))

        def body(a_ref, b_ref, o_ref):
            for q in range(per_step):
                o_ref[:, q * shard:(q + 1) * shard] = lax.dot_general(
                    a_ref[...], b_ref[q], contract, preferred_element_type=f32).astype(o_ref.dtype)
    else:
        n = b.shape[1]
        tn = _weight_tile(n, k)
        b_spec = pl.BlockSpec((k, tn), lambda j, i: (0, j))

        def body(a_ref, b_ref, o_ref):
            o_ref[...] = jnp.dot(a_ref[...], b_ref[...], preferred_element_type=f32).astype(o_ref.dtype)

    return _call(
        body, name=name, grid=(n // tn, m // tm),
        in_specs=[pl.BlockSpec((tm, k), lambda j, i: (i, 0)), b_spec],
        out_specs=[pl.BlockSpec((tm, tn), lambda j, i: (i, j))],
        out_shape=[jax.ShapeDtypeStruct((m, n), out_dtype)], semantics=("parallel", "parallel"), riders=riders,
    )(a, b)


def mm_nn_residual(a, b, x, gate, norm, name, tm=512, riders=None):
    m, k = a.shape
    n = b.shape[1]
    tm = _tile(m, tm, 8)

    def body(a_ref, b_ref, x_ref, g_ref, *rest):
        if norm is not None:
            gn_ref, sc_ref, sh_ref, y_ref, o_ref, h_ref = rest
        else:
            y_ref, o_ref = rest
        y = jnp.dot(a_ref[...], b_ref[...], preferred_element_type=f32)
        y_ref[...] = y.astype(bf16)
        x_new = x_ref[...] + g_ref[...] * y
        o_ref[...] = x_new
        if norm is not None:
            h_ref[...] = _norm_fn(x_new, gn_ref[...], sc_ref[...], sh_ref[...]).astype(bf16)

    blk = pl.BlockSpec((tm, n), lambda i: (i, 0))
    in_specs = [pl.BlockSpec((tm, k), lambda i: (i, 0)), pl.BlockSpec((k, n), lambda i: (0, 0)), blk, _row(n)]
    out_specs = [blk, blk]
    out_shape = [jax.ShapeDtypeStruct((m, n), bf16), jax.ShapeDtypeStruct((m, n), f32)]
    args = [a, b, x, gate]
    if norm is not None:
        in_specs += [_row(n)] * 3
        out_specs += [blk]
        out_shape += [jax.ShapeDtypeStruct((m, n), bf16)]
        args += list(norm)
    return _call(body, name=name, grid=(m // tm,), in_specs=in_specs, out_specs=out_specs, out_shape=out_shape,
                 semantics=("parallel",), riders=riders)(*args)


def mm_nt(a, b, out_dtype, name, tm=512, riders=None, transposed=False):
    m = a.shape[0]
    tm = _tile(m, tm, 8)
    if b.ndim == 3:
        shard, k = (b.shape[1], b.shape[2]) if transposed else (b.shape[2], b.shape[1])
        tk = k
        b_spec = pl.BlockSpec(b.shape, lambda i, j: (0, 0, 0))
        width = N_DEV * shard
        contract = (((1,), (0,)), ((), ())) if transposed else (((1,), (1,)), ((), ()))

        def body(a_ref, b_ref, o_ref):
            acc = None
            for q in range(N_DEV):
                part = lax.dot_general(a_ref[:, q * shard:(q + 1) * shard], b_ref[q], contract, preferred_element_type=f32)
                acc = part if acc is None else acc + part
            o_ref[...] = acc.astype(o_ref.dtype)
    else:
        k, width = b.shape
        tk = _weight_tile(k, width)
        b_spec = pl.BlockSpec((tk, width), lambda i, j: (j, 0))

        def body(a_ref, b_ref, o_ref):
            o_ref[...] = lax.dot_general(a_ref[...], b_ref[...], (((1,), (1,)), ((), ())),
                                         preferred_element_type=f32).astype(o_ref.dtype)

    return _call(
        body, name=name, grid=(m // tm, k // tk),
        in_specs=[pl.BlockSpec((tm, width), lambda i, j: (i, 0)), b_spec],
        out_specs=[pl.BlockSpec((tm, tk), lambda i, j: (i, j))],
        out_shape=[jax.ShapeDtypeStruct((m, k), out_dtype)], semantics=("parallel", "parallel"), riders=riders,
    )(a, b)


def mm_tn(a, b, out_dtype, name, tm=512, tn=512, riders=None):
    t, m = a.shape
    n = b.shape[1]
    tm, tn = _tile(m, tm), _tile(n, tn)

    def body(a_ref, b_ref, o_ref):
        o_ref[...] = lax.dot_general(a_ref[...], b_ref[...], (((0,), (0,)), ((), ())),
                                     preferred_element_type=f32).astype(o_ref.dtype)

    return _call(
        body, name=name, grid=(m // tm, n // tn),
        in_specs=[pl.BlockSpec((t, tm), lambda i, j: (0, i)), pl.BlockSpec((t, tn), lambda i, j: (0, j))],
        out_specs=[pl.BlockSpec((tm, tn), lambda i, j: (i, j))],
        out_shape=[jax.ShapeDtypeStruct((m, n), out_dtype)], semantics=("parallel", "parallel"), riders=riders,
    )(a, b)


def mm_tn_by_owner(a, b, name, tm=512, riders=None):
    t, m = a.shape
    n = b.shape[1]
    shard = n // N_DEV
    per_step = 1 if shard % LANES == 0 else 2
    assert (per_step * shard) % LANES == 0
    tm = _tile(m, tm)

    def body(a_ref, b_ref, o_ref):
        acc = lax.dot_general(a_ref[...], b_ref[...], (((0,), (0,)), ((), ())), preferred_element_type=f32)
        for q in range(per_step):
            o_ref[q] = acc[:, q * shard:(q + 1) * shard].astype(bf16)

    return _call(
        body, name=name, grid=(m // tm, N_DEV // per_step),
        in_specs=[pl.BlockSpec((t, tm), lambda i, j: (0, i)), pl.BlockSpec((t, per_step * shard), lambda i, j: (0, j))],
        out_specs=[pl.BlockSpec((per_step, tm, shard), lambda i, j: (j, i, 0))],
        out_shape=[jax.ShapeDtypeStruct((N_DEV, m, shard), bf16)], semantics=("parallel", "parallel"), riders=riders,
    )(a, b)


def _norm_fn(x, gn, sc, sh):
    r = lax.rsqrt(jnp.mean(x * x, axis=-1, keepdims=True) + EPS)
    return (x * r * gn) * (1.0 + sc) + sh


def _row(d):
    return pl.BlockSpec((1, d), lambda i: (0, 0))


def norm_fwd(x, gn, sc, sh, name, tm=512):
    t, d = x.shape
    tm = _tile(t, tm, 8)

    def body(x_ref, gn_ref, sc_ref, sh_ref, h_ref):
        h_ref[...] = _norm_fn(x_ref[...], gn_ref[...], sc_ref[...], sh_ref[...]).astype(bf16)

    return pl.pallas_call(
        body, name=name, grid=(t // tm,),
        in_specs=[pl.BlockSpec((tm, d), lambda i: (i, 0)), _row(d), _row(d), _row(d)],
        out_specs=pl.BlockSpec((tm, d), lambda i: (i, 0)),
        out_shape=jax.ShapeDtypeStruct((t, d), bf16), compiler_params=_cparams("parallel"),
    )(x, gn, sc, sh)


def _gate_bwd(dx, y_ref, g_ref, dgate_ref, dy_ref):
    dgate_ref[...] += jnp.sum(dx * y_ref[...].astype(f32), axis=0, keepdims=True)
    dy_ref[...] = (dx * g_ref[...]).astype(bf16)


def norm_bwd(x, gn, sc, sh, dh, dres, name, gate=None, tm=512, riders=None):
    t, d = x.shape
    tm = _tile(t, tm, 8)

    def body(x_ref, gn_ref, sc_ref, sh_ref, dh_ref, dres_ref, *rest):
        if gate is not None:
            y_ref, g_ref, dx_ref, dgn_ref, dsc_ref, dsh_ref, dgate_ref, dy_ref = rest
        else:
            dx_ref, dgn_ref, dsc_ref, dsh_ref = rest

        @pl.when(pl.program_id(0) == 0)
        def _():
            dgn_ref[...] = jnp.zeros_like(dgn_ref)
            dsc_ref[...] = jnp.zeros_like(dsc_ref)
            dsh_ref[...] = jnp.zeros_like(dsh_ref)
            if gate is not None:
                dgate_ref[...] = jnp.zeros_like(dgate_ref)

        _, vjp = jax.vjp(_norm_fn, x_ref[...], gn_ref[...], sc_ref[...], sh_ref[...])
        dx, dgn, dsc, dsh = vjp(dh_ref[...].astype(f32))
        dx = dx + dres_ref[...]
        dx_ref[...] = dx
        dgn_ref[...] += dgn
        dsc_ref[...] += dsc
        dsh_ref[...] += dsh
        if gate is not None:
            _gate_bwd(dx, y_ref, g_ref, dgate_ref, dy_ref)

    blk = pl.BlockSpec((tm, d), lambda i: (i, 0))
    vec = jax.ShapeDtypeStruct((1, d), f32)
    in_specs = [blk, _row(d), _row(d), _row(d), blk, blk]
    out_specs = [blk, _row(d), _row(d), _row(d)]
    out_shape = [jax.ShapeDtypeStruct((t, d), f32), vec, vec, vec]
    args = [x, gn, sc, sh, dh, dres]
    if gate is not None:
        in_specs += [blk, _row(d)]
        out_specs += [_row(d), blk]
        out_shape += [vec, jax.ShapeDtypeStruct((t, d), bf16)]
        args += list(gate)
    return _call(body, name=name, grid=(t // tm,), in_specs=in_specs, out_specs=out_specs, out_shape=out_shape,
                 semantics=("arbitrary",), riders=riders)(*args)


def _loss_fn(x, g, tgt):
    r = lax.rsqrt(jnp.mean(x * x, axis=-1, keepdims=True) + EPS)
    err = jnp.square(x * r * g - tgt)
    return 0.5 * jnp.sum(jnp.mean(err, axis=-1, keepdims=True), axis=0, keepdims=True)


def loss_head(x, g, tgt, y, gate, name, tm=512):
    t, d = x.shape
    tm = _tile(t, tm, 8)

    def body(x_ref, g_ref, t_ref, y_ref, gate_ref, loss_ref, dx_ref, dg_ref, dgate_ref, dy_ref):
        @pl.when(pl.program_id(0) == 0)
        def _():
            loss_ref[...] = jnp.zeros_like(loss_ref)
            dg_ref[...] = jnp.zeros_like(dg_ref)
            dgate_ref[...] = jnp.zeros_like(dgate_ref)

        loss, vjp = jax.vjp(_loss_fn, x_ref[...], g_ref[...], t_ref[...])
        dx, dg, _ = vjp(jnp.ones((1, 1), f32))
        dx_ref[...] = dx
        loss_ref[...] += loss
        dg_ref[...] += dg
        _gate_bwd(dx, y_ref, gate_ref, dgate_ref, dy_ref)

    blk = pl.BlockSpec((tm, d), lambda i: (i, 0))
    vec = jax.ShapeDtypeStruct((1, d), f32)
    return pl.pallas_call(
        body, name=name, grid=(t // tm,),
        in_specs=[blk, _row(d), blk, blk, _row(d)],
        out_specs=(pl.BlockSpec((1, 1), lambda i: (0, 0)), blk, _row(d), _row(d), blk),
        out_shape=(jax.ShapeDtypeStruct((1, 1), f32), jax.ShapeDtypeStruct((t, d), f32), vec, vec,
                   jax.ShapeDtypeStruct((t, d), bf16)),
        compiler_params=_cparams("arbitrary"),
    )(x, g, tgt, y, gate)


def _gm_block_fn(z, ws, bs, lng, lnb):
    u = _gelu(z[:, :GM_WIDTH])
    vg = _gelu(z[:, GM_WIDTH:])
    mu = jnp.mean(vg, axis=-1, keepdims=True)
    var = jnp.mean(jnp.square(vg - mu), axis=-1, keepdims=True)
    vn = (vg - mu) * lax.rsqrt(var + EPS) * lng + lnb
    row = lax.broadcasted_iota(jnp.int32, (GM_BLOCK, GM_BLOCK), 0) // CHUNK
    col = lax.broadcasted_iota(jnp.int32, (GM_BLOCK, GM_BLOCK), 1) // CHUNK
    parts = []
    for h in range(GM_HEADS):
        w = jnp.where(row >= col, ws[h], 0.0)
        cols = slice(h * GM_HEAD_DIM, (h + 1) * GM_HEAD_DIM)
        s = jnp.dot(w.astype(bf16), vn[:, cols].astype(bf16), preferred_element_type=f32) + bs[h]
        parts.append(u[:, cols] * s)
    return jnp.concatenate(parts, axis=1)


def _gm_param_specs():
    return [pl.BlockSpec((GM_HEADS, GM_BLOCK, GM_BLOCK), lambda i: (0, 0, 0)),
            pl.BlockSpec((GM_HEADS, GM_BLOCK, 1), lambda i: (0, 0, 0)), _row(GM_WIDTH), _row(GM_WIDTH)]


def gm_mix_fwd(z, ws, bs, lng, lnb, name, riders=None):
    t = z.shape[0]

    def body(z_ref, ws_ref, bs_ref, lng_ref, lnb_ref, o_ref):
        o_ref[...] = _gm_block_fn(z_ref[...].astype(f32), ws_ref[...], bs_ref[...], lng_ref[...],
                                  lnb_ref[...]).astype(bf16)

    return _call(
        body, name=name, grid=(t // GM_BLOCK,),
        in_specs=[pl.BlockSpec((GM_BLOCK, 2 * GM_WIDTH), lambda i: (i, 0))] + _gm_param_specs(),
        out_specs=[pl.BlockSpec((GM_BLOCK, GM_WIDTH), lambda i: (i, 0))],
        out_shape=[jax.ShapeDtypeStruct((t, GM_WIDTH), bf16)], semantics=("parallel",), riders=riders,
    )(z, ws, bs, lng, lnb)


def gm_mix_bwd(z, ws, bs, lng, lnb, dgated, name, riders=None):
    t = z.shape[0]

    def body(z_ref, ws_ref, bs_ref, lng_ref, lnb_ref, dg_ref, dz_ref, dws_ref, dbs_ref, dlng_ref, dlnb_ref):
        _, vjp = jax.vjp(_gm_block_fn, z_ref[...].astype(f32), ws_ref[...], bs_ref[...], lng_ref[...], lnb_ref[...])
        dz, dws, dbs, dlng, dlnb = vjp(dg_ref[...].astype(f32))
        dz_ref[...] = dz.astype(bf16)

        @pl.when(pl.program_id(0) == 0)
        def _():
            dws_ref[...] = jnp.zeros_like(dws_ref)
            dbs_ref[...] = jnp.zeros_like(dbs_ref)
            dlng_ref[...] = jnp.zeros_like(dlng_ref)
            dlnb_ref[...] = jnp.zeros_like(dlnb_ref)

        dws_ref[...] += dws
        dbs_ref[...] += dbs
        dlng_ref[...] += dlng
        dlnb_ref[...] += dlnb

    zblk = pl.BlockSpec((GM_BLOCK, 2 * GM_WIDTH), lambda i: (i, 0))
    return _call(
        body, name=name, grid=(t // GM_BLOCK,),
        in_specs=[zblk] + _gm_param_specs() + [pl.BlockSpec((GM_BLOCK, GM_WIDTH), lambda i: (i, 0))],
        out_specs=[zblk] + _gm_param_specs(),
        out_shape=[jax.ShapeDtypeStruct((t, 2 * GM_WIDTH), bf16),
                   jax.ShapeDtypeStruct((GM_HEADS, GM_BLOCK, GM_BLOCK), f32),
                   jax.ShapeDtypeStruct((GM_HEADS, GM_BLOCK, 1), f32),
                   jax.ShapeDtypeStruct((1, GM_WIDTH), f32), jax.ShapeDtypeStruct((1, GM_WIDTH), f32)],
        semantics=("arbitrary",), riders=riders,
    )(z, ws, bs, lng, lnb, dgated)


@functools.partial(jax.custom_vjp, nondiff_argnums=(1,))
def _rows_up(x, shift):
    return x if shift == 0 else pltpu.roll(x, x.shape[1] - shift, axis=1)


def _rows_up_fwd(x, shift):
    return _rows_up(x, shift), None


def _rows_up_bwd(shift, _, g):
    return (g if shift == 0 else pltpu.roll(g, shift, axis=1),)


_rows_up.defvjp(_rows_up_fwd, _rows_up_bwd)


def _hg_block_fn(sub, qp, fz, iv, gp, s0, lb, gn):
    n, ns, d = sub, HG_TOKENS // sub, HG_DIM
    p, nb, per_sub = HG_PAIR, HG_TOKENS // HG_PAIR, sub // HG_PAIR
    f = lb + (1.0 - lb) * jax.nn.sigmoid(fz)
    g = jnp.log(f)
    k = 1.0 - f
    q = qp * jax.nn.sigmoid(qp)
    v = iv.astype(bf16)
    row = lax.broadcasted_iota(jnp.int32, (HG_TOKENS, HG_TOKENS), 0)
    col = lax.broadcasted_iota(jnp.int32, (HG_TOKENS, HG_TOKENS), 1)
    same_sub = col // n == row // n
    tri = ((col <= row) & same_sub).astype(f32)
    cum = jnp.dot(tri, g, precision=lax.Precision.HIGHEST, preferred_element_type=f32)
    cum_b, q_b, k_b, f_b = cum.reshape(nb, p, d), q.reshape(nb, p, d), k.reshape(nb, p, d), f.reshape(nb, p, d)
    j_b = lax.broadcasted_iota(jnp.int32, (nb, p, d), 1)
    j_col = lax.broadcasted_iota(jnp.int32, (nb, p, 1), 1)
    scores_t = jnp.zeros((HG_TOKENS, HG_TOKENS), f32)
    weight = k_b
    for delta in range(p):
        if delta:
            weight = weight * _rows_up(f_b, delta)
        pair = jnp.sum(_rows_up(q_b, delta) * weight, axis=2, keepdims=True)
        pair = jnp.where(j_col < p - delta, pair, 0.0)
        scores_t = scores_t + jnp.where(col == row + delta, pair.reshape(HG_TOKENS, 1), 0.0)
    o = lax.dot_general(scores_t.astype(bf16), v, (((0,), (0,)), ((), ())), preferred_element_type=f32)
    last = cum_b[:, p - 1:p, :]
    before = jnp.concatenate([jnp.zeros((1, 1, d), f32), last[:-1]], axis=0)
    before = jnp.broadcast_to(before, (nb, p, d)).reshape(HG_TOKENS, d)
    block = (lax.broadcasted_iota(jnp.int32, (HG_TOKENS, d), 0) // p) % per_sub
    q_late = q * jnp.exp(jnp.where(block > 0, cum - before, -1e30))
    last_s = last.reshape(ns, per_sub, d)
    q_parts, k_parts = [], []
    for m in range(1, per_sub):
        split = jnp.broadcast_to(last_s[:, m - 1:m, :], (ns, n, d)).reshape(HG_TOKENS, d)
        k_parts.append(k * jnp.exp(jnp.where(block < m, split - cum, -1e30)))
        q_parts.append(jnp.where(block == m, q_late, 0.0))
    scores = lax.dot_general(jnp.concatenate(q_parts, axis=1).astype(bf16), jnp.concatenate(k_parts, axis=1).astype(bf16),
                             (((1,), (1,)), ((), ())), preferred_element_type=f32)
    o = o + jnp.dot(jnp.where(same_sub, scores, 0.0).astype(bf16), v, preferred_element_type=f32)
    cum_s = cum.reshape(ns, n, d)
    tot = cum_s[:, n - 1:n, :]
    kt_t = (k.reshape(ns, n, d) * jnp.exp(tot - cum_s)).reshape(HG_TOKENS, d).T
    lane_sub = lax.broadcasted_iota(jnp.int32, (d, HG_TOKENS), 1) // n
    k_by_sub = jnp.concatenate([jnp.where(lane_sub == b, kt_t, 0.0) for b in range(ns)], axis=0).astype(bf16)
    update = jnp.dot(k_by_sub, v, preferred_element_type=f32)
    decay = jnp.exp(tot.reshape(ns, d)).T
    state = s0
    states = []
    for a in range(ns):
        states.append(state.astype(bf16))
        state = decay[:, a:a + 1] * state + update[a * d:(a + 1) * d]
    qt = q * jnp.exp(cum)
    row_sub = lax.broadcasted_iota(jnp.int32, (HG_TOKENS, d), 0) // n
    q_by_sub = jnp.concatenate([jnp.where(row_sub == a, qt, 0.0) for a in range(ns)], axis=1).astype(bf16)
    o = o + jnp.dot(q_by_sub, jnp.concatenate(states, axis=0), preferred_element_type=f32)
    on = o * lax.rsqrt(jnp.mean(o * o, axis=-1, keepdims=True) + EPS) * gn
    return on * (gp * jax.nn.sigmoid(gp)), state


def _head_parts(ref, h):
    return [ref[:, p * D + h * HG_DIM:p * D + (h + 1) * HG_DIM] for p in range(4)]


def hg_scan_fwd(proj, lb, gn, name, riders=None):
    t = proj.shape[0]
    nt = t // HG_TOKENS

    def body(p_ref, lb_ref, gn_ref, y_ref, s_ref, state):
        @pl.when(pl.program_id(0) == 0)
        def _():
            state[...] = jnp.zeros_like(state)

        for h in range(HG_HEADS):
            cols = slice(h * HG_DIM, (h + 1) * HG_DIM)
            s_ref[h, 0] = state[h]
            y, s1 = _hg_block_fn(HG_SUB_FWD, *_head_parts(p_ref, h), state[h], lb_ref[:, cols], gn_ref[:, cols])
            y_ref[:, cols] = y.astype(bf16)
            state[h] = s1

    return _call(
        body, name=name, grid=(nt,),
        in_specs=[pl.BlockSpec((HG_TOKENS, 4 * D), lambda i: (i, 0)), _row(D), _row(D)],
        out_specs=[pl.BlockSpec((HG_TOKENS, D), lambda i: (i, 0)),
                   pl.BlockSpec((HG_HEADS, 1, HG_DIM, HG_DIM), lambda i: (0, i, 0, 0))],
        out_shape=[jax.ShapeDtypeStruct((t, D), bf16), jax.ShapeDtypeStruct((HG_HEADS, nt, HG_DIM, HG_DIM), f32)],
        scratch_shapes=[pltpu.VMEM((HG_HEADS, HG_DIM, HG_DIM), f32)],
        semantics=("arbitrary",), riders=riders,
    )(proj, lb, gn)


def hg_scan_bwd(proj, lb, gn, states, dy, name, riders=None):
    t = proj.shape[0]
    nt = t // HG_TOKENS

    def body(p_ref, lb_ref, gn_ref, s_ref, dy_ref, dp_ref, dlb_ref, dgn_ref, dstate):
        @pl.when(pl.program_id(0) == 0)
        def _():
            dstate[...] = jnp.zeros_like(dstate)
            dlb_ref[...] = jnp.zeros_like(dlb_ref)
            dgn_ref[...] = jnp.zeros_like(dgn_ref)

        for h in range(HG_HEADS):
            cols = slice(h * HG_DIM, (h + 1) * HG_DIM)
            _, vjp = jax.vjp(functools.partial(_hg_block_fn, HG_SUB_BWD), *_head_parts(p_ref, h), s_ref[h, 0],
                             lb_ref[:, cols], gn_ref[:, cols])
            grads = vjp((dy_ref[:, cols].astype(f32), dstate[h]))
            for p in range(4):
                dp_ref[:, p * D + h * HG_DIM:p * D + (h + 1) * HG_DIM] = grads[p].astype(bf16)
            dstate[h] = grads[4]
            dlb_ref[:, cols] += grads[5]
            dgn_ref[:, cols] += grads[6]

    small = jax.ShapeDtypeStruct((1, D), f32)
    return _call(
        body, name=name, grid=(nt,),
        in_specs=[pl.BlockSpec((HG_TOKENS, 4 * D), lambda i: (nt - 1 - i, 0)), _row(D), _row(D),
                  pl.BlockSpec((HG_HEADS, 1, HG_DIM, HG_DIM), lambda i: (0, nt - 1 - i, 0, 0)),
                  pl.BlockSpec((HG_TOKENS, D), lambda i: (nt - 1 - i, 0))],
        out_specs=[pl.BlockSpec((HG_TOKENS, 4 * D), lambda i: (nt - 1 - i, 0)), _row(D), _row(D)],
        out_shape=[jax.ShapeDtypeStruct((t, 4 * D), bf16), small, small],
        scratch_shapes=[pltpu.VMEM((HG_HEADS, HG_DIM, HG_DIM), f32)],
        semantics=("arbitrary",), riders=riders,
    )(proj, lb, gn, states, dy)


FFN_COLS = 1408
HALO = 8
STRIP = 16


def _ffn_specs(tm):
    nb = tm // HALO
    main_g = pl.BlockSpec((tm, FFN_COLS), lambda j, i: (i, j))
    main_v = pl.BlockSpec((tm, FFN_COLS), lambda j, i: (i, j + 2))
    halo_g = pl.BlockSpec((HALO, FFN_COLS), lambda j, i: (jnp.maximum(i * nb - 1, 0), j))
    halo_v = pl.BlockSpec((HALO, FFN_COLS), lambda j, i: (jnp.maximum(i * nb - 1, 0), j + 2))
    w_g = pl.BlockSpec((3, FFN_COLS), lambda j, i: (0, j))
    w_v = pl.BlockSpec((3, FFN_COLS), lambda j, i: (0, j + 2))
    b_g = pl.BlockSpec((1, FFN_COLS), lambda j, i: (0, j))
    b_v = pl.BlockSpec((1, FFN_COLS), lambda j, i: (0, j + 2))
    return [main_g, halo_g, main_v, halo_v, w_g, w_v, b_g, b_v]


def _strip_rows(r):
    return pl.ds(r * STRIP, STRIP) if isinstance(r, int) else pl.ds(pl.multiple_of(r * STRIP, STRIP), STRIP)


def _for_strips(nstrip, strip, reverse=False):
    if reverse:
        strip(nstrip - 1, True)
        lax.fori_loop(0, nstrip - 1, lambda k, c: (strip(nstrip - 2 - k, False), c)[1], 0)
    else:
        strip(0, True)
        lax.fori_loop(1, nstrip, lambda r, c: (strip(r, False), c)[1], 0)


SUBLANES = 8


def _rows_down(prev, cur, shift):
    row = lax.broadcasted_iota(jnp.int32, (SUBLANES, LANES), 0)
    tiles = [prev[STRIP - SUBLANES:]] + [cur[q * SUBLANES:(q + 1) * SUBLANES] for q in range(STRIP // SUBLANES)]
    turned = [pltpu.roll(x, shift, axis=0) for x in tiles]
    return jnp.concatenate([jnp.where(row < shift, turned[q], turned[q + 1]) for q in range(STRIP // SUBLANES)], axis=0)


def _rows_ahead(cur, nxt, shift):
    row = lax.broadcasted_iota(jnp.int32, (SUBLANES, LANES), 0)
    tiles = [cur[q * SUBLANES:(q + 1) * SUBLANES] for q in range(STRIP // SUBLANES)] + [nxt[:SUBLANES]]
    turned = [pltpu.roll(x, SUBLANES - shift, axis=0) for x in tiles]
    return jnp.concatenate([jnp.where(row >= SUBLANES - shift, turned[q + 1], turned[q])
                            for q in range(STRIP // SUBLANES)], axis=0)


def _conv_strip(main_ref, halo_ref, w_ref, b_ref, r, edge, cols, first_block):
    cur = main_ref[_strip_rows(r), cols].astype(f32)
    if edge:
        h = jnp.where(first_block, 0.0, halo_ref[:, cols].astype(f32))
        prev = jnp.concatenate([jnp.zeros_like(h), h], axis=0)
    else:
        prev = main_ref[_strip_rows(r - 1), cols].astype(f32)
    a1, a2 = _rows_down(prev, cur, 1), _rows_down(prev, cur, 2)
    y = b_ref[:, cols] + w_ref[0:1, cols] * a2 + w_ref[1:2, cols] * a1 + w_ref[2:3, cols] * cur
    return y, (cur, a1, a2)


def ffn_gate_fwd(a, cw, cb, name, tm=512, riders=None):
    t = a.shape[0]
    tm = _tile(t, tm, STRIP)

    def body(ag_ref, hg_ref, av_ref, hv_ref, wg_ref, wv_ref, bg_ref, bv_ref, o_ref):
        first_block = pl.program_id(1) == 0

        def strip(r, edge):
            for c in range(FFN_COLS // LANES):
                cols = pl.ds(c * LANES, LANES)
                yg, _ = _conv_strip(ag_ref, hg_ref, wg_ref, bg_ref, r, edge, cols, first_block)
                yv, _ = _conv_strip(av_ref, hv_ref, wv_ref, bv_ref, r, edge, cols, first_block)
                o_ref[_strip_rows(r), cols] = (_gelu(yg) * yv).astype(bf16)

        _for_strips(tm // STRIP, strip)

    return _call(
        body, name=name, grid=(2, t // tm), in_specs=_ffn_specs(tm),
        out_specs=[pl.BlockSpec((tm, FFN_COLS), lambda j, i: (i, j))],
        out_shape=[jax.ShapeDtypeStruct((t, FFN_HIDDEN), bf16)],
        semantics=("parallel", "arbitrary"), riders=riders,
    )(a, a, a, a, cw, cw, cb, cb)


def _conv_tile(prev, cur, w_ref, b_ref, cols):
    row = lax.broadcasted_iota(jnp.int32, (SUBLANES, LANES), 0)
    a1 = jnp.where(row < 1, pltpu.roll(prev, 1, axis=0), pltpu.roll(cur, 1, axis=0))
    a2 = jnp.where(row < 2, pltpu.roll(prev, 2, axis=0), pltpu.roll(cur, 2, axis=0))
    return b_ref[:, cols] + w_ref[0:1, cols] * a2 + w_ref[1:2, cols] * a1 + w_ref[2:3, cols] * cur


def _gate_grads(yg, yv, dh):
    cdf = 0.5 * (1.0 + lax.erf(yg * (1.0 / math.sqrt(2.0))))
    pdf = jnp.exp(-0.5 * yg * yg) * (1.0 / math.sqrt(2.0 * math.pi))
    return dh * yv * (cdf + yg * pdf), dh * (yg * cdf)


def ffn_gate_bwd(a, cw, cb, dhid, name, tm=512, riders=None):
    t = a.shape[0]
    tm = _tile(t, tm, STRIP)
    nb = tm // HALO
    last_halo = t // HALO - 1
    nstrip = tm // STRIP
    fh = FFN_HIDDEN

    def body(a_ref, ha_ref, na_ref, w_ref, b_ref, dh_ref, ndh_ref, da_ref, dw_ref, db_ref, acc, dybuf):
        first_block = pl.program_id(0) == 0
        last_block = pl.program_id(0) == pl.num_programs(0) - 1

        @pl.when(first_block)
        def _():
            acc[...] = jnp.zeros_like(acc)

        def transposed_conv(dy, ahead, cols):
            return (w_ref[2:3, cols] * dy + w_ref[1:2, cols] * _rows_ahead(dy, ahead, 1)
                    + w_ref[0:1, cols] * _rows_ahead(dy, ahead, 2)).astype(bf16)

        def strip(r, edge):
            rows = _strip_rows(r)
            for c in range(fh // LANES):
                gate, val = pl.ds(c * LANES, LANES), pl.ds(fh + c * LANES, LANES)
                yg, taps_g = _conv_strip(a_ref, ha_ref, w_ref, b_ref, r, edge, gate, first_block)
                yv, taps_v = _conv_strip(a_ref, ha_ref, w_ref, b_ref, r, edge, val, first_block)
                dyg, dyv = _gate_grads(yg, yv, dh_ref[rows, gate].astype(f32))
                for p, (dy, (a0, a1, a2), cols) in enumerate(((dyg, taps_g, gate), (dyv, taps_v, val))):
                    acc[0, :, cols] += dy * a2
                    acc[1, :, cols] += dy * a1
                    acc[2, :, cols] += dy * a0
                    acc[3, :, cols] += dy
                    if not edge:
                        da_ref[_strip_rows(r - 1), cols] = transposed_conv(dybuf[:, cols], dy, cols)
                    dybuf[:, cols] = dy

        _for_strips(nstrip, strip)

        last_rows = pl.ds((nstrip - 1) * STRIP + SUBLANES, SUBLANES)
        for c in range(fh // LANES):
            gate, val = pl.ds(c * LANES, LANES), pl.ds(fh + c * LANES, LANES)
            yg = _conv_tile(a_ref[last_rows, gate].astype(f32), na_ref[:, gate].astype(f32), w_ref, b_ref, gate)
            yv = _conv_tile(a_ref[last_rows, val].astype(f32), na_ref[:, val].astype(f32), w_ref, b_ref, val)
            for dy, cols in zip(_gate_grads(yg, yv, ndh_ref[:, gate].astype(f32)), (gate, val)):
                dy = jnp.where(last_block, 0.0, dy)
                ahead = jnp.concatenate([dy, jnp.zeros_like(dy)], axis=0)
                da_ref[_strip_rows(nstrip - 1), cols] = transposed_conv(dybuf[:, cols], ahead, cols)

        @pl.when(last_block)
        def _():
            for tap in range(3):
                dw_ref[tap:tap + 1, :] = jnp.sum(acc[tap], axis=0, keepdims=True)
            db_ref[...] = jnp.sum(acc[3], axis=0, keepdims=True)

    def after(i):
        return jnp.minimum((i + 1) * nb, last_halo)

    return _call(
        body, name=name, grid=(t // tm,),
        in_specs=[pl.BlockSpec((tm, 2 * fh), lambda i: (i, 0)),
                  pl.BlockSpec((HALO, 2 * fh), lambda i: (jnp.maximum(i * nb - 1, 0), 0)),
                  pl.BlockSpec((HALO, 2 * fh), lambda i: (after(i), 0)),
                  pl.BlockSpec((3, 2 * fh), lambda i: (0, 0)), pl.BlockSpec((1, 2 * fh), lambda i: (0, 0)),
                  pl.BlockSpec((tm, fh), lambda i: (i, 0)), pl.BlockSpec((HALO, fh), lambda i: (after(i), 0))],
        out_specs=[pl.BlockSpec((tm, 2 * fh), lambda i: (i, 0)), pl.BlockSpec((3, 2 * fh), lambda i: (0, 0)),
                   pl.BlockSpec((1, 2 * fh), lambda i: (0, 0))],
        out_shape=[jax.ShapeDtypeStruct((t, 2 * fh), bf16), jax.ShapeDtypeStruct((3, 2 * fh), f32),
                   jax.ShapeDtypeStruct((1, 2 * fh), f32)],
        scratch_shapes=[pltpu.VMEM((4, STRIP, 2 * fh), f32), pltpu.VMEM((STRIP, 2 * fh), f32)],
        semantics=("arbitrary",), riders=riders,
    )(a, a, a, cw, cb, dhid, dhid)


def ada_mod(c_all, ada_w, ada_b_cols, name):
    cols = ada_w.shape[2]

    def body(c_ref, w_ref, b_ref, o_ref):
        c = c_ref[...]
        cond = (c * jax.nn.sigmoid(c)).astype(bf16)
        o_ref[0] = jnp.dot(cond, w_ref[0].astype(bf16), preferred_element_type=f32) + b_ref[0]

    return pl.pallas_call(
        body, name=name, grid=(DEPTH,),
        in_specs=[pl.BlockSpec((N_DEV, D), lambda i: (0, 0)), pl.BlockSpec((1, D, cols), lambda i: (i, 0, 0)),
                  pl.BlockSpec((1, 1, cols), lambda i: (i, 0, 0))],
        out_specs=pl.BlockSpec((1, N_DEV, cols), lambda i: (i, 0, 0)),
        out_shape=jax.ShapeDtypeStruct((DEPTH, N_DEV, cols), f32), compiler_params=_cparams("parallel"),
    )(c_all, ada_w, ada_b_cols)


def ada_grads(c_all, dmod_cols, dmod_all, name):
    cols = dmod_cols.shape[2]

    def body(c_ref, dm_ref, da_ref, dw_ref, db_ref):
        c = c_ref[...]
        cond = c * jax.nn.sigmoid(c)
        dw_ref[0] = lax.dot_general(cond, dm_ref[0], (((0,), (0,)), ((), ())), precision=lax.Precision.HIGHEST,
                                    preferred_element_type=f32)
        acc = da_ref[0, 0]
        for e in range(1, N_DEV):
            acc = acc + da_ref[e, 0]
        db_ref[0] = acc

    return pl.pallas_call(
        body, name=name, grid=(DEPTH,),
        in_specs=[pl.BlockSpec((N_DEV, D), lambda i: (0, 0)), pl.BlockSpec((1, N_DEV, cols), lambda i: (i, 0, 0)),
                  pl.BlockSpec((N_DEV, 1, 1, 6 * D), lambda i: (0, i, 0, 0))],
        out_specs=(pl.BlockSpec((1, D, cols), lambda i: (i, 0, 0)), pl.BlockSpec((1, 1, 6 * D), lambda i: (i, 0, 0))),
        out_shape=(jax.ShapeDtypeStruct((DEPTH, D, cols), f32), jax.ShapeDtypeStruct((DEPTH, 1, 6 * D), f32)),
        compiler_params=_cparams("parallel"),
    )(c_all, dmod_cols, dmod_all)


def lower_bound_fwd(hg_lb, name):
    n = hg_lb.shape[1]

    def body(l_ref, o_ref):
        o_ref[...] = jax.nn.sigmoid(l_ref[1:2, :] - l_ref[0:1, :])

    return pl.pallas_call(body, name=name, out_shape=jax.ShapeDtypeStruct((1, n), f32))(hg_lb)


def lower_bound_bwd(hg_lb, dlb, name):
    n = hg_lb.shape[1]

    def body(l_ref, d_ref, o_ref):
        p = jax.nn.sigmoid(l_ref[1:2, :] - l_ref[0:1, :])
        g = d_ref[...] * p * (1.0 - p)
        o_ref[0:1, :] = -g
        o_ref[1:2, :] = g

    return pl.pallas_call(body, name=name, out_shape=jax.ShapeDtypeStruct((2, n), f32))(hg_lb, dlb)


def _adamw(w, g, m, v):
    m = ADAM_B1 * m + (1.0 - ADAM_B1) * g
    v = ADAM_B2 * v + (1.0 - ADAM_B2) * jnp.square(g)
    m_hat = m / (1.0 - ADAM_B1 ** ADAM_STEP)
    v_hat = v / (1.0 - ADAM_B2 ** ADAM_STEP)
    delta = -ADAM_LR * (m_hat / (jnp.sqrt(v_hat) + ADAM_EPS) + ADAM_WD * w)
    return delta, m, v


ADAM_BLOCK_BYTES = 32 * 1024 * 1024


def adam_reduced(parts, w, m, v, name):
    layers, r, c = w.shape
    outs = None
    for layer in range(layers):
        outs = _adam_layer(parts[layer], w, m, v, layer, outs, f"{name}_{layer}")
    return outs


def _adam_layer(parts, w, m, v, layer, prev, name):
    layers, r, c = w.shape
    rows = parts[0].shape[1]
    assert all(p.shape == (N_DEV, rows, c) for p in parts) and rows * len(parts) == r
    row_bytes = 2 * (len(parts) * N_DEV * c * 2 + 7 * c * 4)
    tr = _tile(rows, max(16, ADAM_BLOCK_BYTES // row_bytes), 16)
    steps = rows // tr
    n_prev = 0 if prev is None else 4

    def body(*refs):
        p_refs = refs[:len(parts)]
        w_ref, m_ref, v_ref = refs[len(parts):len(parts) + 3]
        g_ref, d_ref, mo_ref, vo_ref = refs[len(parts) + 3 + n_prev:]
        for idx in range(len(parts)):
            @pl.when(pl.program_id(0) == idx)
            def _():
                g = p_refs[idx][0].astype(f32)
                for j in range(1, N_DEV):
                    g = g + p_refs[idx][j].astype(f32)
                g_ref[...] = g
                d_ref[...], mo_ref[...], vo_ref[...] = _adamw(w_ref[...], g, m_ref[...], v_ref[...])

    def part_spec(idx):
        return pl.BlockSpec((N_DEV, tr, c), lambda p, i: (0, jnp.where(p == idx, i, 0), 0))

    blk = pl.BlockSpec((None, tr, c), lambda p, i: (layer, p * steps + i, 0))
    out = jax.ShapeDtypeStruct((layers, r, c), f32)
    n_in = len(parts) + 3
    return pl.pallas_call(
        body, name=name, grid=(len(parts), steps),
        in_specs=[part_spec(idx) for idx in range(len(parts))] + [blk, blk, blk] + [ANY] * n_prev,
        out_specs=(blk, blk, blk, blk), out_shape=(out, out, out, out),
        input_output_aliases={n_in + k: k for k in range(n_prev)},
        compiler_params=_cparams("arbitrary", "arbitrary"),
    )(*parts, w, m, v, *(prev or ()))


def adam_plain(g, w, m, v, name, tr=256):
    r, c = w.shape
    tr = _tile(r, tr, 8)

    def body(g_ref, w_ref, m_ref, v_ref, d_ref, mo_ref, vo_ref):
        d_ref[...], mo_ref[...], vo_ref[...] = _adamw(w_ref[...], g_ref[...], m_ref[...], v_ref[...])

    blk = pl.BlockSpec((tr, c), lambda i: (i, 0))
    out = jax.ShapeDtypeStruct((r, c), f32)
    return pl.pallas_call(
        body, name=name, grid=(r // tr,), in_specs=[blk, blk, blk, blk], out_specs=(blk, blk, blk),
        out_shape=(out, out, out), compiler_params=_cparams("parallel"),
    )(g, w, m, v)


def sum_parts(parts, name):
    _, r, c = parts.shape

    def body(p_ref, o_ref):
        acc = p_ref[0]
        for j in range(1, N_DEV):
            acc = acc + p_ref[j]
        o_ref[...] = acc

    return pl.pallas_call(body, name=name, out_shape=jax.ShapeDtypeStruct((r, c), f32))(parts)


def _pack(arrs, rows_mult=8):
    flat = jnp.concatenate([a.reshape(-1) for a in arrs])
    rows = -(-flat.shape[0] // LANES)
    rows = -(-rows // rows_mult) * rows_mult
    return jnp.pad(flat, (0, rows * LANES - flat.shape[0])).reshape(rows, LANES)


def _unpack(flat, shapes):
    out, at = [], 0
    for s in shapes:
        n = math.prod(s)
        out.append(flat[at:at + n].reshape(s))
        at += n
    return out


def kernel(x, c, gm_w_in, gm_ln_g, gm_ln_b, gm_w_s, gm_b_s, gm_w_out, hg_w_in, hg_lb, hg_gn_g, hg_w_out, ffn_w_up, ffn_conv_w, ffn_conv_b, ffn_w_down, norm_g, ada_w, ada_b, final_g, loss_target, m_gm_w_in, m_gm_ln_g, m_gm_ln_b, m_gm_w_s, m_gm_b_s, m_gm_w_out, m_hg_w_in, m_hg_lb, m_hg_gn_g, m_hg_w_out, m_ffn_w_up, m_ffn_conv_w, m_ffn_conv_b, m_ffn_w_down, m_norm_g, m_ada_w, m_ada_b, m_final_g, v_gm_w_in, v_gm_ln_g, v_gm_ln_b, v_gm_w_s, v_gm_b_s, v_gm_w_out, v_hg_w_in, v_hg_lb, v_hg_gn_g, v_hg_w_out, v_ffn_w_up, v_ffn_conv_w, v_ffn_conv_b, v_ffn_w_down, v_norm_g, v_ada_w, v_ada_b, v_final_g):
    me = _flat(_mesh_pos())
    xt = x[0]
    t = xt.shape[0]

    small_shapes = [(1, D), (2, HG_DIM), (2, HG_DIM), (DEPTH, 2, HG_DIM), (DEPTH, 3, 2 * FFN_HIDDEN // N_DEV)]
    w_in_0 = gm_w_in[0].astype(bf16)
    small_all, w_in_0_all = all_gather([_pack([c, hg_lb, hg_gn_g, norm_g, ffn_conv_w]), w_in_0], "gather_first", relay=True)
    small_all = small_all.reshape(N_DEV, -1)
    at = 0
    pieces = []
    for s in small_shapes:
        n = math.prod(s)
        pieces.append(small_all[:, at:at + n].reshape((N_DEV,) + s))
        at += n
    c_all = pieces[0].reshape(N_DEV, D)
    hg_lb_full = jnp.transpose(pieces[1], (1, 0, 2)).reshape(2, D)
    hg_gn_full = jnp.transpose(pieces[2], (1, 0, 2)).reshape(2, D)
    norm_g_full = jnp.transpose(pieces[3], (1, 2, 0, 3)).reshape(DEPTH, 2, D)
    conv_w_full = jnp.transpose(pieces[4], (1, 2, 0, 3)).reshape(DEPTH, 3, 2 * FFN_HIDDEN)

    lb1 = lower_bound_fwd(hg_lb_full, "lower_bound")
    lbs = [jnp.zeros((1, D), f32), lb1]

    ada_b_cols = lax.dynamic_slice(ada_b, (0, me * ADA_COLS), (DEPTH, ADA_COLS)).reshape(DEPTH, 1, ADA_COLS)
    mod_cols = ada_mod(c_all, ada_w, ada_b_cols, "ada_mod")
    (mod_mine,) = all_to_all([jnp.transpose(mod_cols, (1, 0, 2))], "mod_to_examples")
    mod = jnp.transpose(mod_mine, (1, 0, 2)).reshape(DEPTH, 6, 1, D)

    def layer_shards(i):
        j = i // 2
        w_in, w_out = (gm_w_in, gm_w_out) if i % 2 == 0 else (hg_w_in, hg_w_out)
        return [w_in[j].astype(bf16), w_out[j].astype(bf16), ffn_w_up[i].T.astype(bf16), ffn_w_down[i].astype(bf16)]

    def full_rows(g):
        return g.reshape(N_DEV * g.shape[1], g.shape[2])

    carried_by = {
        "in_0": [(0, 1), (0, 3)], "mix_0": [(0, 2)], "up_0": [(1, 0), (1, 1)], "gate_0": [(1, 2)], "down_0": [(1, 3)],
        "in_1": [(2, 0)], "mix_1": [(2, 1), (2, 2), (2, 3)], "up_1": [(3, 0), (3, 1)], "gate_1": [(3, 2)], "down_1": [(3, 3)],
    }
    shards = [layer_shards(i) for i in range(DEPTH)]
    gathered = {}
    gathered[(0, 0)] = w_in_0_all

    def carry(call, site, **kw):
        items = carried_by.get(site, [])
        outs = call(riders=Riders([shards[l][slot] for l, slot in items], True), **kw)
        for item, g in zip(items, outs[len(outs) - len(items):]):
            gathered[item] = g
        return outs[:len(outs) - len(items)]

    saved = []
    weights = []
    xcur = xt
    h = norm_fwd(xcur, norm_g_full[0, 0:1], mod[0, 1], mod[0, 0], "norm1_0")
    for i in range(DEPTH):
        j = i // 2
        sh1, sc1, g1, sh2, sc2, g2 = [mod[i, p] for p in range(6)]
        gn2 = norm_g_full[i, 1:2]
        s = {"x0": xcur, "h": h}
        w_in = gathered[(i, 0)]
        if i % 2 == 0:
            (z,) = carry(functools.partial(mm_nn, h, w_in, bf16, f"gm_in_{i}"), f"in_{i}")
            bs = gm_b_s[j].reshape(GM_HEADS, GM_BLOCK, 1)
            (mixed,) = carry(functools.partial(gm_mix_fwd, z, gm_w_s[j], bs, gm_ln_g[j:j + 1], gm_ln_b[j:j + 1],
                                               f"gm_mix_{i}"), f"mix_{i}")
            s["z"] = z
        else:
            (proj,) = carry(functools.partial(mm_nn, h, w_in, f32, f"hg_in_{i}"), f"in_{i}")
            mixed, states = carry(functools.partial(hg_scan_fwd, proj, lbs[j], hg_gn_full[j:j + 1], f"hg_scan_{i}"),
                                  f"mix_{i}")
            s["proj"], s["states"] = proj, states
        s["mixed"] = mixed
        w_out = full_rows(gathered[(i, 1)])
        y, x1, h2 = carry(functools.partial(mm_nn_residual, mixed, w_out, xcur, g1, (gn2, sc2, sh2), f"mix_out_{i}"),
                          f"out_{i}")
        s["y"], s["x1"] = y, x1
        w_up = gathered[(i, 2)]
        (a,) = carry(functools.partial(mm_nn, h2, w_up, bf16, f"ffn_up_{i}", transposed=True), f"up_{i}")
        (hid,) = carry(functools.partial(ffn_gate_fwd, a, conv_w_full[i], ffn_conv_b[i:i + 1], f"ffn_gate_{i}"), f"gate_{i}")
        w_down = full_rows(gathered[(i, 3)])
        next_norm = (norm_g_full[i + 1, 0:1], mod[i + 1, 1], mod[i + 1, 0]) if i + 1 < DEPTH else None
        outs = carry(functools.partial(mm_nn_residual, hid, w_down, x1, g2, next_norm, f"ffn_down_{i}"), f"down_{i}")
        fo, x2 = outs[0], outs[1]
        s["h2"], s["a"], s["hid"], s["f"] = h2, a, hid, fo
        weights.append((w_in, w_out, w_up, w_down))
        saved.append(s)
        xcur = x2
        h = outs[2] if next_norm is not None else None

    loss_part, dx, d_final_g, dg2, df = loss_head(xcur, final_g.reshape(1, D), loss_target[0], saved[-1]["f"],
                                                  mod[DEPTH - 1, 5], "loss_head")
    loss = lax.psum(loss_part[0, 0], ("x", "y", "c"))

    def halves(blocked):
        rows = blocked.shape[1] // 2
        return [(blocked, (0, rows)), (blocked, (rows, rows))]

    def by_owner_rows(dw):
        k, n = dw.shape
        return dw.reshape(N_DEV, k // N_DEV, n)

    received = [[[] for _ in range(4)] for _ in range(DEPTH)]

    def send(call, items, **kw):
        outs = call(riders=Riders([arr for arr, _ in items], False), **kw)
        for (_, (layer, slot)), got in zip(items, outs[len(outs) - len(items):]):
            received[layer][slot].append(got)
        return outs[:len(outs) - len(items)]

    dmod = [None] * DEPTH
    d_norm_g = [None] * DEPTH
    d_gm = {k: [None, None] for k in ("ws", "bs", "lng", "lnb")}
    d_hg = {k: [None, None] for k in ("lb", "gn")}
    d_ffn = {k: [None] * DEPTH for k in ("cw", "cb")}
    in_halves = []
    for i in reversed(range(DEPTH)):
        j = i // 2
        s = saved[i]
        w_in, w_out, w_up, w_down = weights[i]
        sh1, sc1, g1, sh2, sc2, g2 = [mod[i, p] for p in range(6)]
        gn1, gn2 = norm_g_full[i, 0:1], norm_g_full[i, 1:2]
        (dw_down,) = mm_tn(s["hid"], df, bf16, f"dw_down_{i}", tn=D)
        scan_carries = i % 2 == 1
        down_item = (by_owner_rows(dw_down), (i, 3))
        (dhid,) = mm_nt(df, w_down, bf16, f"dhid_{i}")
        da, d_ffn["cw"][i], d_ffn["cb"][i] = send(
            functools.partial(ffn_gate_bwd, s["a"], conv_w_full[i], ffn_conv_b[i:i + 1], dhid, f"ffn_gate_bwd_{i}"),
            in_halves)
        (dw_up_t,) = send(functools.partial(mm_tn, da, s["h2"], bf16, f"dw_up_{i}", tn=D), [] if scan_carries else [down_item])
        dw_up_t = dw_up_t.reshape(N_DEV, -1, D)
        up_halves = [(dw_up_t, (i, 2))] if scan_carries else [(part, (i, 2)) for part in halves(dw_up_t)]
        (dh2,) = send(functools.partial(mm_nt, da, w_up, bf16, f"dh2_{i}", transposed=True),
                      [] if scan_carries else up_halves[:1])
        dx1, dgn2, dsc2, dsh2, dg1, dy = norm_bwd(s["x1"], gn2, sc2, sh2, dh2, dx, f"norm2_bwd_{i}", gate=(s["y"], g1))
        (dw_out,) = mm_tn(s["mixed"], dy, bf16, f"dw_mix_out_{i}", tn=D)
        (dmixed,) = mm_nt(dy, w_out, bf16, f"dmixed_{i}")
        if i % 2 == 0:
            bs = gm_b_s[j].reshape(GM_HEADS, GM_BLOCK, 1)
            dpre, dws, dbs, dlng, dlnb = send(
                functools.partial(gm_mix_bwd, s["z"], gm_w_s[j], bs, gm_ln_g[j:j + 1], gm_ln_b[j:j + 1], dmixed,
                                  f"gm_mix_bwd_{i}"), up_halves[1:])
            d_gm["ws"][j], d_gm["bs"][j], d_gm["lng"][j], d_gm["lnb"][j] = dws, dbs.reshape(GM_HEADS, GM_BLOCK), dlng, dlnb
        else:
            dpre, dlb, dgn = send(
                functools.partial(hg_scan_bwd, s["proj"], lbs[j], hg_gn_full[j:j + 1], s["states"], dmixed,
                                  f"hg_scan_bwd_{i}"), [down_item] + up_halves)
            d_hg["lb"][j], d_hg["gn"][j] = dlb, dgn
        (dw_in,) = send(functools.partial(mm_tn_by_owner, s["h"], dpre, f"dw_mix_in_{i}", tm=D), [(by_owner_rows(dw_out), (i, 1))])
        in_halves = [(part, (i, 0)) for part in halves(dw_in)]
        (dh,) = send(functools.partial(mm_nt, dpre, w_in, bf16, f"dh_mix_{i}"), in_halves[:1] if i == 0 else [])
        dmod_i = [None, None, dg1, dsh2, dsc2, dg2]
        if i > 0:
            dx, dgn1, dsc1, dsh1, dg2, df = norm_bwd(s["x0"], gn1, sc1, sh1, dh, dx1, f"norm1_bwd_{i}",
                                                     gate=(saved[i - 1]["f"], mod[i - 1, 5]))
        else:
            dx, dgn1, dsc1, dsh1 = send(functools.partial(norm_bwd, s["x0"], gn1, sc1, sh1, dh, dx1, f"norm1_bwd_{i}"),
                                        in_halves[1:])
        dmod_i[0], dmod_i[1] = dsh1, dsc1
        dmod[i] = jnp.concatenate(dmod_i, axis=1)
        d_norm_g[i] = jnp.concatenate([dgn1, dgn2], axis=0)
    grad_x = dx.reshape(1, t, D)

    (dmod_all,) = all_gather([jnp.concatenate(dmod, axis=0)], "gather_dmod")
    dmod_cols = jnp.transpose(lax.dynamic_slice(dmod_all, (0, 0, me * ADA_COLS), (N_DEV, DEPTH, ADA_COLS)), (1, 0, 2))
    g_ada_w, g_ada_b = ada_grads(c_all, dmod_cols, dmod_all.reshape(N_DEV, DEPTH, 1, 6 * D), "ada_grads")
    g_ada_b = g_ada_b.reshape(DEPTH, 6 * D)

    small_partials = [jnp.concatenate(d_gm["lng"], axis=0), jnp.concatenate(d_gm["lnb"], axis=0),
                      jnp.stack(d_gm["ws"]), jnp.stack(d_gm["bs"]), jnp.concatenate(d_ffn["cb"], axis=0),
                      d_final_g, d_hg["lb"][1], jnp.concatenate(d_hg["gn"], axis=0), jnp.stack(d_norm_g),
                      jnp.stack(d_ffn["cw"])]
    partial_shapes = [p.shape for p in small_partials]
    packed = _pack(small_partials, rows_mult=8 * N_DEV)
    rows = packed.shape[0] // N_DEV
    (recv,) = all_to_all([packed.reshape(N_DEV, rows, LANES)], "small_grads_exchange")
    (summed,) = all_gather([sum_parts(recv, "small_grads_sum")], "small_grads_gather")

    def parts_of(slot, layers):
        return [received[i][slot] for i in layers]

    def swapped(a):
        return jnp.swapaxes(a, 1, 2)

    w_shards = [gm_w_in, gm_w_out, hg_w_in, hg_w_out, swapped(ffn_w_up), ffn_w_down]
    big_parts = [parts_of(0, (0, 2)), parts_of(1, (0, 2)), parts_of(0, (1, 3)), parts_of(1, (1, 3)),
                 parts_of(2, range(DEPTH)), parts_of(3, range(DEPTH))]
    big_m = [m_gm_w_in, m_gm_w_out, m_hg_w_in, m_hg_w_out, swapped(m_ffn_w_up), m_ffn_w_down]
    big_v = [v_gm_w_in, v_gm_w_out, v_hg_w_in, v_hg_w_out, swapped(v_ffn_w_up), v_ffn_w_down]
    big = [adam_reduced(parts, w, m_, v_, f"adam_big_{idx}")
           for idx, (w, m_, v_, parts) in enumerate(zip(w_shards, big_m, big_v, big_parts))]
    big[4] = [swapped(o) for o in big[4]]
    (g_gm_w_in, d_gm_w_in, nm_gm_w_in, nv_gm_w_in), (g_gm_w_out, d_gm_w_out, nm_gm_w_out, nv_gm_w_out), \
        (g_hg_w_in, d_hg_w_in, nm_hg_w_in, nv_hg_w_in), (g_hg_w_out, d_hg_w_out, nm_hg_w_out, nv_hg_w_out), \
        (g_ffn_w_up, d_ffn_w_up, nm_ffn_w_up, nv_ffn_w_up), (g_ffn_w_down, d_ffn_w_down, nm_ffn_w_down, nv_ffn_w_down) = big

    g_ln_g, g_ln_b, g_ws, g_bs, g_cb, g_final, g_lb1, g_gn, g_norm, g_cw = _unpack(summed.reshape(-1), partial_shapes)
    g_final = g_final.reshape(D)

    def my_cols(a, n):
        start = (0,) * (a.ndim - 1) + (me * n,)
        return lax.dynamic_slice(a, start, a.shape[:-1] + (n,))

    g_hg_lb = lower_bound_bwd(hg_lb, my_cols(g_lb1, HG_DIM), "lower_bound_bwd")
    g_hg_gn = my_cols(g_gn, HG_DIM)
    g_norm_g = my_cols(g_norm, HG_DIM)
    g_conv_w = my_cols(g_cw, 2 * FFN_HIDDEN // N_DEV)

    two_d = (-1, ADA_COLS)
    d_ada_w, nm_ada_w, nv_ada_w = [o.reshape(ada_w.shape) for o in adam_plain(
        g_ada_w.reshape(two_d), ada_w.reshape(two_d), m_ada_w.reshape(two_d), v_ada_w.reshape(two_d), "adam_ada_w")]

    small_g = [g_ln_g, g_ln_b, g_ws, g_bs, g_cb, g_ada_b, g_final, g_hg_lb, g_hg_gn, g_norm_g, g_conv_w]
    small_w = [gm_ln_g, gm_ln_b, gm_w_s, gm_b_s, ffn_conv_b, ada_b, final_g, hg_lb, hg_gn_g, norm_g, ffn_conv_w]
    small_m = [m_gm_ln_g, m_gm_ln_b, m_gm_w_s, m_gm_b_s, m_ffn_conv_b, m_ada_b, m_final_g, m_hg_lb, m_hg_gn_g, m_norm_g, m_ffn_conv_w]
    small_v = [v_gm_ln_g, v_gm_ln_b, v_gm_w_s, v_gm_b_s, v_ffn_conv_b, v_ada_b, v_final_g, v_hg_lb, v_hg_gn_g, v_norm_g, v_ffn_conv_w]
    shapes = [w.shape for w in small_w]
    small_g = [g.reshape(s) for g, s in zip(small_g, shapes)]
    outs = adam_plain(_pack(small_g), _pack(small_w), _pack(small_m), _pack(small_v), "adam_small")
    (d_ln_g, d_ln_b, d_ws, d_bs, d_cb, d_ada_b, d_final, d_hg_lb, d_hg_gn, d_norm_g_, d_conv_w), \
        (nm_ln_g, nm_ln_b, nm_ws, nm_bs, nm_cb, nm_ada_b, nm_final, nm_hg_lb, nm_hg_gn, nm_norm_g, nm_conv_w), \
        (nv_ln_g, nv_ln_b, nv_ws, nv_bs, nv_cb, nv_ada_b, nv_final, nv_hg_lb, nv_hg_gn, nv_norm_g, nv_conv_w) = [
            _unpack(o.reshape(-1), shapes) for o in outs]
    g_ln_g, g_ln_b, g_ws, g_bs, g_cb, g_ada_b, g_final, g_hg_lb, g_hg_gn, g_norm_g, g_conv_w = small_g

    grads = (g_gm_w_in, g_ln_g, g_ln_b, g_ws, g_bs, g_gm_w_out, g_hg_w_in, g_hg_lb, g_hg_gn, g_hg_w_out,
             g_ffn_w_up, g_conv_w, g_cb, g_ffn_w_down, g_norm_g, g_ada_w, g_ada_b, g_final)
    deltas = (d_gm_w_in, d_ln_g, d_ln_b, d_ws, d_bs, d_gm_w_out, d_hg_w_in, d_hg_lb, d_hg_gn, d_hg_w_out,
              d_ffn_w_up, d_conv_w, d_cb, d_ffn_w_down, d_norm_g_, d_ada_w, d_ada_b, d_final)
    new_m = (nm_gm_w_in, nm_ln_g, nm_ln_b, nm_ws, nm_bs, nm_gm_w_out, nm_hg_w_in, nm_hg_lb, nm_hg_gn, nm_hg_w_out,
             nm_ffn_w_up, nm_conv_w, nm_cb, nm_ffn_w_down, nm_norm_g, nm_ada_w, nm_ada_b, nm_final)
    new_v = (nv_gm_w_in, nv_ln_g, nv_ln_b, nv_ws, nv_bs, nv_gm_w_out, nv_hg_w_in, nv_hg_lb, nv_hg_gn, nv_hg_w_out,
             nv_ffn_w_up, nv_conv_w, nv_cb, nv_ffn_w_down, nv_norm_g, nv_ada_w, nv_ada_b, nv_final)
    return (loss, grad_x) + grads + deltas + new_m + new_v
```
